```python
import math
import jax
import jax.numpy as jnp
from jax import lax
import numpy as np

D_MODEL = 1024
BATCH = 8
SEQ = 8192
DEPTH = 4

BRANCH_WIDTH = D_MODEL // 2
HEAD_DIM = 128
N_HEADS = BRANCH_WIDTH // HEAD_DIM
N_BRANCHES = 3
CONV_WIDTH = 4
CHUNK = 64
SB_BLOCK = 128
N_META = 16
FRONT = SB_BLOCK
PAD_FRONT = FRONT - N_META
NORM_EPS = 1e-6
SPLIT_SIZES = (BRANCH_WIDTH, BRANCH_WIDTH, BRANCH_WIDTH, BRANCH_WIDTH,
               3 * BRANCH_WIDTH, BRANCH_WIDTH, N_HEADS, N_HEADS,
               BRANCH_WIDTH, BRANCH_WIDTH, BRANCH_WIDTH, BRANCH_WIDTH,
               N_BRANCHES * D_MODEL)
N_IN = 12 * BRANCH_WIDTH + 2 * N_HEADS + N_BRANCHES * D_MODEL

kernel_name = 'meta_sb_gdn_hgrn2_gated_hybrid'


def rms_norm(x, w):
    xf = x.astype(jnp.float32)
    y = xf * lax.rsqrt(jnp.mean(xf * xf, axis=-1, keepdims=True) + NORM_EPS)
    return (y * w.astype(jnp.float32)).astype(x.dtype)


def l2_norm(x):
    return x * lax.rsqrt(jnp.sum(x * x, axis=-1, keepdims=True) + NORM_EPS)


def causal_conv(x, w):
    t_len = x.shape[1]
    xp = jnp.pad(x, ((0, 0), (CONV_WIDTH - 1, 0), (0, 0)))
    y = xp[:, 0:t_len] * w[0]
    for i in range(1, CONV_WIDTH):
        y = y + xp[:, i:i + t_len] * w[i]
    return y


def stick_breaking_attention(q, k, v, key_valid):
    t_len, d = q.shape[2], q.shape[3]
    scale = d ** -0.5
    outs = []
    for blk in range(t_len // SB_BLOCK):
        q0 = blk * SB_BLOCK
        q1 = q0 + SB_BLOCK
        z = jnp.einsum('bhqd,bhkd->bhqk', q[:, :, q0:q1], k[:, :, :q1]) * scale
        t_idx = jnp.arange(q0, q1)[:, None]
        s_idx = jnp.arange(q1)[None, :]
        mask = (s_idx < t_idx) & key_valid[None, :q1]
        log_keep = jnp.where(mask, jax.nn.log_sigmoid(-z), 0.0)
        log_passed = lax.cumsum(log_keep, axis=3, reverse=True) - log_keep
        attn = jnp.where(mask, jnp.exp(jax.nn.log_sigmoid(z) + log_passed), 0.0)
        outs.append(jnp.einsum('bhqk,bhkd->bhqd', attn, v[:, :, :q1]))
    return jnp.concatenate(outs, axis=2)


def to_chunks(a):
    bsz, t_len, h = a.shape[0], a.shape[1], a.shape[2]
    a = a.reshape((bsz, t_len // CHUNK, CHUNK, h) + a.shape[3:])
    return jnp.moveaxis(a, 3, 1)


def from_chunks(o):
    bsz, h, n, c, dv = o.shape
    return jnp.moveaxis(o, 1, 3).reshape(bsz, n * c, h, dv)


def gated_delta_rule_chunked(q, k, v, beta, g):
    bsz, _, h, dk = q.shape
    dv = v.shape[-1]
    q = to_chunks(q) * (dk ** -0.5)
    k = to_chunks(k)
    v = to_chunks(v)
    beta = to_chunks(beta)
    G = jnp.cumsum(to_chunks(g), axis=-1)
    causal = np.tril(np.ones((CHUNK, CHUNK), dtype=bool))
    strict = np.tril(np.ones((CHUNK, CHUNK), dtype=bool), k=-1)
    decay = jnp.exp(jnp.where(causal, G[..., :, None] - G[..., None, :], -jnp.inf))
    k_beta = k * beta[..., None]
    m = jnp.where(strict, jnp.einsum('bhncd,bhnsd->bhncs', k_beta, k) * decay, 0.0)
    t_mat = m + jnp.eye(CHUNK, dtype=jnp.float32)
    u = lax.linalg.triangular_solve(t_mat, v * beta[..., None], left_side=True, lower=True, unit_diagonal=True)
    w = lax.linalg.triangular_solve(t_mat, k_beta * jnp.exp(G)[..., None], left_side=True, lower=True, unit_diagonal=True)
    a_qk = jnp.einsum('bhncd,bhnsd->bhncs', q, k) * decay
    q_dec = q * jnp.exp(G)[..., None]
    k_dec = k * jnp.exp(G[..., -1:] - G)[..., None]
    g_last = jnp.exp(G[..., -1])
    xs = tuple(jnp.moveaxis(a, 2, 0) for a in (u, w, a_qk, q_dec, k_dec, g_last))

    def step(state, inp):
        u_c, w_c, aqk_c, qd_c, kd_c, gl_c = inp
        v_new = u_c - jnp.einsum('bhcd,bhde->bhce', w_c, state)
        o = jnp.einsum('bhcd,bhde->bhce', qd_c, state) + jnp.einsum('bhcs,bhse->bhce', aqk_c, v_new)
        state = state * gl_c[..., None, None] + jnp.einsum('bhcd,bhce->bhde', kd_c, v_new)
        return state, o

    s0 = jnp.zeros((bsz, h, dk, dv), jnp.float32)
    _, o = lax.scan(step, s0, xs)
    return from_chunks(jnp.moveaxis(o, 0, 2))


def hgrn2_chunked(q, k, v, g):
    bsz, _, h, dk = q.shape
    dv = v.shape[-1]
    G = jnp.cumsum(to_chunks(g), axis=3)
    xs = tuple(jnp.moveaxis(a, 2, 0) for a in (to_chunks(q), to_chunks(k), to_chunks(v), G))
    causal = np.tril(np.ones((CHUNK, CHUNK), dtype=bool))[:, :, None]

    def step(state, inp):
        q_c, k_c, v_c, g_c = inp
        g_end = g_c[:, :, -1:, :]
        o_inter = jnp.einsum('bhcd,bhde->bhce', q_c * jnp.exp(g_c), state)
        diff = g_c[:, :, :, None, :] - g_c[:, :, None, :, :]
        dec = jnp.exp(jnp.where(causal, diff, -jnp.inf))
        a = jnp.einsum('bhid,bhjd,bhijd->bhij', q_c, k_c, dec)
        o = o_inter + jnp.einsum('bhij,bhje->bhie', a, v_c)
        state = state * jnp.exp(g_end)[:, :, 0, :, None] + jnp.einsum('bhcd,bhce->bhde', k_c * jnp.exp(g_end - g_c), v_c)
        return state, o

    s0 = jnp.zeros((bsz, h, dk, dv), jnp.float32)
    _, o = lax.scan(step, s0, xs)
    return from_chunks(jnp.moveaxis(o, 0, 2))


def hybrid_layer(h, valid, norm_w, w_in, sb_qn, sb_kn, conv_w, a_log, dt_bias, gdn_on, lb, hg_on, w_branch, w_out):
    bsz, t_len, _ = h.shape
    f32 = jnp.float32
    xn = rms_norm(h, norm_w)
    proj = jnp.einsum('btd,dn->btn', xn, w_in)
    split_points = np.cumsum(SPLIT_SIZES)[:-1].tolist()
    (sb_q, sb_k, sb_v, sb_z, gd_qkv, gd_z, gd_b, gd_a,
     hg_q, hg_f, hg_i, hg_z, mix) = jnp.split(proj, split_points, axis=-1)
    vmask = valid[None, :, None].astype(f32)

    def heads(a):
        return a.reshape(bsz, t_len, N_HEADS, HEAD_DIM)

    q = jnp.transpose(rms_norm(heads(sb_q), sb_qn).astype(f32), (0, 2, 1, 3))
    k = jnp.transpose(rms_norm(heads(sb_k), sb_kn).astype(f32), (0, 2, 1, 3))
    v = jnp.transpose(heads(sb_v).astype(f32), (0, 2, 1, 3))
    o_sb = stick_breaking_attention(q, k, v, valid)
    o_sb = jnp.transpose(o_sb, (0, 2, 1, 3)).reshape(bsz, t_len, BRANCH_WIDTH) * jax.nn.silu(sb_z.astype(f32))

    qkv = jax.nn.silu(causal_conv(gd_qkv.astype(f32), conv_w.astype(f32)))
    gq, gk, gv = jnp.split(qkv, 3, axis=-1)
    beta = jax.nn.sigmoid(gd_b.astype(f32)) * vmask
    g = -jnp.exp(a_log.astype(f32)) * jax.nn.softplus(gd_a.astype(f32) + dt_bias.astype(f32))
    o_gd = gated_delta_rule_chunked(l2_norm(heads(gq)), l2_norm(heads(gk)), heads(gv), beta, g)
    o_gd = rms_norm(o_gd, gdn_on).reshape(bsz, t_len, BRANCH_WIDTH) * jax.nn.silu(gd_z.astype(f32))

    lbf = lb.astype(f32)
    f_pre = hg_f.astype(f32)
    forget = lbf + (1.0 - lbf) * jax.nn.sigmoid(f_pre)
    hk = (1.0 - lbf) * jax.nn.sigmoid(-f_pre)
    o_hg = hgrn2_chunked(heads(jax.nn.silu(hg_q.astype(f32))), heads(hk),
                         heads(hg_i.astype(f32) * vmask), heads(jnp.log(forget)))
    o_hg = rms_norm(o_hg, hg_on).reshape(bsz, t_len, BRANCH_WIDTH) * jax.nn.silu(hg_z.astype(f32))

    gates = jax.nn.sigmoid(mix.astype(f32)).reshape(bsz, t_len, N_BRANCHES, D_MODEL)
    y = (gates[:, :, 0] * jnp.einsum('btw,wd->btd', o_sb, w_branch[0])
         + gates[:, :, 1] * jnp.einsum('btw,wd->btd', o_gd, w_branch[1])
         + gates[:, :, 2] * jnp.einsum('btw,wd->btd', o_hg, w_branch[2]))
    out = jnp.einsum('btd,de->bte', y.astype(h.dtype), w_out)
    return h + out.astype(h.dtype)


def _fwd_setup_inputs(seed: int = 0) -> dict:
    key = jax.random.key(seed)
    ks = jax.random.split(key, 16)
    f32 = jnp.float32
    x = jax.random.normal(ks[0], (BATCH, SEQ, D_MODEL), f32)
    meta_tokens = jax.random.normal(ks[1], (N_META, D_MODEL), f32)
    norm_w = 1.0 + 0.02 * jax.random.normal(ks[2], (DEPTH, D_MODEL), f32)
    w_in = jax.random.normal(ks[3], (DEPTH, D_MODEL, N_IN), f32) * D_MODEL ** -0.5
    sb_q_norm = 1.0 + 0.02 * jax.random.normal(ks[4], (DEPTH, HEAD_DIM), f32)
    sb_k_norm = 1.0 + 0.02 * jax.random.normal(ks[5], (DEPTH, HEAD_DIM), f32)
    gdn_conv_w = jax.random.normal(ks[6], (DEPTH, CONV_WIDTH, 3 * BRANCH_WIDTH), f32) * CONV_WIDTH ** -0.5
    gdn_a_log = jnp.log(jax.random.uniform(ks[7], (DEPTH, N_HEADS), f32, 1.0, 16.0))
    dt = jnp.exp(jax.random.uniform(ks[8], (DEPTH, N_HEADS), f32) * (math.log(0.1) - math.log(0.001)) + math.log(0.001))
    gdn_dt_bias = dt + jnp.log(-jnp.expm1(-dt))
    gdn_out_norm = 1.0 + 0.02 * jax.random.normal(ks[9], (DEPTH, HEAD_DIM), f32)
    hgrn_lb_logits = jax.random.normal(ks[10], (DEPTH, BRANCH_WIDTH), f32)
    hgrn_out_norm = 1.0 + 0.02 * jax.random.normal(ks[11], (DEPTH, HEAD_DIM), f32)
    w_branch = jax.random.normal(ks[12], (DEPTH, N_BRANCHES, BRANCH_WIDTH, D_MODEL), f32) * BRANCH_WIDTH ** -0.5
    w_out = jax.random.normal(ks[13], (DEPTH, D_MODEL, D_MODEL), f32) * D_MODEL ** -0.5
    return {'x': x, 'meta_tokens': meta_tokens, 'norm_w': norm_w, 'w_in': w_in,
            'sb_q_norm': sb_q_norm, 'sb_k_norm': sb_k_norm, 'gdn_conv_w': gdn_conv_w,
            'gdn_a_log': gdn_a_log, 'gdn_dt_bias': gdn_dt_bias, 'gdn_out_norm': gdn_out_norm,
            'hgrn_lb_logits': hgrn_lb_logits, 'hgrn_out_norm': hgrn_out_norm,
            'w_branch': w_branch, 'w_out': w_out}


def _fwd_reference(x, meta_tokens, norm_w, w_in, sb_q_norm, sb_k_norm, gdn_conv_w, gdn_a_log,
              gdn_dt_bias, gdn_out_norm, hgrn_lb_logits, hgrn_out_norm, w_branch, w_out):
    bsz = x.shape[0]
    h = jnp.concatenate([
        jnp.zeros((bsz, PAD_FRONT, D_MODEL), x.dtype),
        jnp.broadcast_to(meta_tokens.astype(x.dtype)[None], (bsz, N_META, D_MODEL)),
        x], axis=1)
    t_len = h.shape[1]
    valid = jnp.arange(t_len) >= PAD_FRONT
    p = jax.nn.softmax(hgrn_lb_logits.astype(jnp.float32), axis=0)
    lower_bounds = jnp.cumsum(p, axis=0) - p[0:1]
    for layer in range(DEPTH):
        h = hybrid_layer(h, valid, norm_w[layer], w_in[layer], sb_q_norm[layer], sb_k_norm[layer],
                         gdn_conv_w[layer], gdn_a_log[layer], gdn_dt_bias[layer], gdn_out_norm[layer],
                         lower_bounds[layer], hgrn_out_norm[layer], w_branch[layer], w_out[layer])
    return h[:, FRONT:]


import jax as _jax
import jax.numpy as _jnp

TWIN_FORMAT = 'train_step'
FWD_PARAMS = ['x', 'meta_tokens', 'norm_w', 'w_in', 'sb_q_norm', 'sb_k_norm', 'gdn_conv_w', 'gdn_a_log', 'gdn_dt_bias', 'gdn_out_norm', 'hgrn_lb_logits', 'hgrn_out_norm', 'w_branch', 'w_out']
TWIN_WEIGHTS = ['meta_tokens', 'norm_w', 'w_in', 'sb_q_norm', 'sb_k_norm', 'gdn_conv_w', 'gdn_a_log', 'gdn_dt_bias', 'gdn_out_norm', 'hgrn_lb_logits', 'hgrn_out_norm', 'w_branch', 'w_out']
TWIN_DIFF_INPUT = 'x'
TWIN_INPUTS = ['x', 'meta_tokens', 'norm_w', 'w_in', 'sb_q_norm', 'sb_k_norm', 'gdn_conv_w', 'gdn_a_log', 'gdn_dt_bias', 'gdn_out_norm', 'hgrn_lb_logits', 'hgrn_out_norm', 'w_branch', 'w_out', 'loss_target', 'm_meta_tokens', 'm_norm_w', 'm_w_in', 'm_sb_q_norm', 'm_sb_k_norm', 'm_gdn_conv_w', 'm_gdn_a_log', 'm_gdn_dt_bias', 'm_gdn_out_norm', 'm_hgrn_lb_logits', 'm_hgrn_out_norm', 'm_w_branch', 'm_w_out', 'v_meta_tokens', 'v_norm_w', 'v_w_in', 'v_sb_q_norm', 'v_sb_k_norm', 'v_gdn_conv_w', 'v_gdn_a_log', 'v_gdn_dt_bias', 'v_gdn_out_norm', 'v_hgrn_lb_logits', 'v_hgrn_out_norm', 'v_w_branch', 'v_w_out']
TWIN_OUTPUTS = ['loss', 'grad_x', 'grad_meta_tokens', 'grad_norm_w', 'grad_w_in', 'grad_sb_q_norm', 'grad_sb_k_norm', 'grad_gdn_conv_w', 'grad_gdn_a_log', 'grad_gdn_dt_bias', 'grad_gdn_out_norm', 'grad_hgrn_lb_logits', 'grad_hgrn_out_norm', 'grad_w_branch', 'grad_w_out', 'delta_meta_tokens', 'delta_norm_w', 'delta_w_in', 'delta_sb_q_norm', 'delta_sb_k_norm', 'delta_gdn_conv_w', 'delta_gdn_a_log', 'delta_gdn_dt_bias', 'delta_gdn_out_norm', 'delta_hgrn_lb_logits', 'delta_hgrn_out_norm', 'delta_w_branch', 'delta_w_out', 'new_m_meta_tokens', 'new_m_norm_w', 'new_m_w_in', 'new_m_sb_q_norm', 'new_m_sb_k_norm', 'new_m_gdn_conv_w', 'new_m_gdn_a_log', 'new_m_gdn_dt_bias', 'new_m_gdn_out_norm', 'new_m_hgrn_lb_logits', 'new_m_hgrn_out_norm', 'new_m_w_branch', 'new_m_w_out', 'new_v_meta_tokens', 'new_v_norm_w', 'new_v_w_in', 'new_v_sb_q_norm', 'new_v_sb_k_norm', 'new_v_gdn_conv_w', 'new_v_gdn_a_log', 'new_v_gdn_dt_bias', 'new_v_gdn_out_norm', 'new_v_hgrn_lb_logits', 'new_v_hgrn_out_norm', 'new_v_w_branch', 'new_v_w_out']
TWIN_LEAF_KINDS = {'loss': 'loss', 'grad_x': 'grad_x', 'grad_meta_tokens': 'grad_w', 'grad_norm_w': 'grad_w', 'grad_w_in': 'grad_w', 'grad_sb_q_norm': 'grad_w', 'grad_sb_k_norm': 'grad_w', 'grad_gdn_conv_w': 'grad_w', 'grad_gdn_a_log': 'grad_w', 'grad_gdn_dt_bias': 'grad_w', 'grad_gdn_out_norm': 'grad_w', 'grad_hgrn_lb_logits': 'grad_w', 'grad_hgrn_out_norm': 'grad_w', 'grad_w_branch': 'grad_w', 'grad_w_out': 'grad_w', 'delta_meta_tokens': 'delta_w', 'delta_norm_w': 'delta_w', 'delta_w_in': 'delta_w', 'delta_sb_q_norm': 'delta_w', 'delta_sb_k_norm': 'delta_w', 'delta_gdn_conv_w': 'delta_w', 'delta_gdn_a_log': 'delta_w', 'delta_gdn_dt_bias': 'delta_w', 'delta_gdn_out_norm': 'delta_w', 'delta_hgrn_lb_logits': 'delta_w', 'delta_hgrn_out_norm': 'delta_w', 'delta_w_branch': 'delta_w', 'delta_w_out': 'delta_w', 'new_m_meta_tokens': 'new_m', 'new_m_norm_w': 'new_m', 'new_m_w_in': 'new_m', 'new_m_sb_q_norm': 'new_m', 'new_m_sb_k_norm': 'new_m', 'new_m_gdn_conv_w': 'new_m', 'new_m_gdn_a_log': 'new_m', 'new_m_gdn_dt_bias': 'new_m', 'new_m_gdn_out_norm': 'new_m', 'new_m_hgrn_lb_logits': 'new_m', 'new_m_hgrn_out_norm': 'new_m', 'new_m_w_branch': 'new_m', 'new_m_w_out': 'new_m', 'new_v_meta_tokens': 'new_v', 'new_v_norm_w': 'new_v', 'new_v_w_in': 'new_v', 'new_v_sb_q_norm': 'new_v', 'new_v_sb_k_norm': 'new_v', 'new_v_gdn_conv_w': 'new_v', 'new_v_gdn_a_log': 'new_v', 'new_v_gdn_dt_bias': 'new_v', 'new_v_gdn_out_norm': 'new_v', 'new_v_hgrn_lb_logits': 'new_v', 'new_v_hgrn_out_norm': 'new_v', 'new_v_w_branch': 'new_v', 'new_v_w_out': 'new_v'}


def _forward(args):
    return _fwd_reference(*[args[k] for k in FWD_PARAMS])


def _output_shape():
    def fwd():
        inp = _fwd_setup_inputs(0)
        return _fwd_reference(*[inp[k] for k in FWD_PARAMS])
    out = _jax.eval_shape(fwd)
    return out.shape, out.dtype

N_MICROBATCH = 1
ADAM_LR = 0.001
ADAM_B1 = 0.9
ADAM_B2 = 0.999
ADAM_EPS = 1e-08
ADAM_WD = 0.01
ADAM_STEP = 10
PER_EXAMPLE_BATCH_AXIS = {'x': 0, 'loss_target': 0}
SHARED_INPUTS = []
_WEIGHT_DTYPES = {'meta_tokens': _jnp.float32, 'norm_w': _jnp.float32, 'w_in': _jnp.float32, 'sb_q_norm': _jnp.float32, 'sb_k_norm': _jnp.float32, 'gdn_conv_w': _jnp.float32, 'gdn_a_log': _jnp.float32, 'gdn_dt_bias': _jnp.float32, 'gdn_out_norm': _jnp.float32, 'hgrn_lb_logits': _jnp.float32, 'hgrn_out_norm': _jnp.float32, 'w_branch': _jnp.float32, 'w_out': _jnp.float32}
MOMENT_SCALE = {'meta_tokens': 3.791574e-02, 'norm_w': 2.351421e+01, 'w_in': 3.636684e-01, 'sb_q_norm': 3.197782e+00, 'sb_k_norm': 3.193876e+00, 'gdn_conv_w': 8.687942e-01, 'gdn_a_log': 2.634625e+01, 'gdn_dt_bias': 2.483316e+01, 'gdn_out_norm': 5.927989e+01, 'hgrn_lb_logits': 2.633073e-02, 'hgrn_out_norm': 5.335256e+01, 'w_branch': 7.293989e-01, 'w_out': 1.236479e+00}


def _to_microbatches(a, axis):
    t = _jnp.moveaxis(a, axis, 0)
    t = t.reshape((N_MICROBATCH, t.shape[0] // N_MICROBATCH) + t.shape[1:])
    return _jnp.moveaxis(t, 1, axis + 1)


def setup_inputs(seed: int = 0) -> dict:
    inp = _fwd_setup_inputs(seed)
    key = _jax.random.fold_in(_jax.random.key(seed), 7919)
    shape, _ = _output_shape()
    out = dict(inp)
    out["loss_target"] = _jax.random.normal(_jax.random.fold_in(key, 0), shape, _jnp.float32)
    for i, name in enumerate(TWIN_WEIGHTS):
        w = inp[name].astype(_jnp.float32)
        if MOMENT_SCALE is None:
            s = _jnp.sqrt(_jnp.mean(_jnp.square(w)) + 1e-30)
        else:
            s = MOMENT_SCALE[name]
        km, kv = _jax.random.split(_jax.random.fold_in(key, i + 1))
        out[name] = w
        out["m_" + name] = s * _jax.random.normal(km, w.shape, _jnp.float32)
        out["v_" + name] = (s * s) * _jax.random.uniform(kv, w.shape, _jnp.float32, 0.5, 1.5)
    if N_MICROBATCH > 1:
        for name, axis in PER_EXAMPLE_BATCH_AXIS.items():
            out[name] = _to_microbatches(out[name], axis)
    return {'x': out['x'], 'meta_tokens': out['meta_tokens'], 'norm_w': out['norm_w'], 'w_in': out['w_in'], 'sb_q_norm': out['sb_q_norm'], 'sb_k_norm': out['sb_k_norm'], 'gdn_conv_w': out['gdn_conv_w'], 'gdn_a_log': out['gdn_a_log'], 'gdn_dt_bias': out['gdn_dt_bias'], 'gdn_out_norm': out['gdn_out_norm'], 'hgrn_lb_logits': out['hgrn_lb_logits'], 'hgrn_out_norm': out['hgrn_out_norm'], 'w_branch': out['w_branch'], 'w_out': out['w_out'], 'loss_target': out['loss_target'], 'm_meta_tokens': out['m_meta_tokens'], 'm_norm_w': out['m_norm_w'], 'm_w_in': out['m_w_in'], 'm_sb_q_norm': out['m_sb_q_norm'], 'm_sb_k_norm': out['m_sb_k_norm'], 'm_gdn_conv_w': out['m_gdn_conv_w'], 'm_gdn_a_log': out['m_gdn_a_log'], 'm_gdn_dt_bias': out['m_gdn_dt_bias'], 'm_gdn_out_norm': out['m_gdn_out_norm'], 'm_hgrn_lb_logits': out['m_hgrn_lb_logits'], 'm_hgrn_out_norm': out['m_hgrn_out_norm'], 'm_w_branch': out['m_w_branch'], 'm_w_out': out['m_w_out'], 'v_meta_tokens': out['v_meta_tokens'], 'v_norm_w': out['v_norm_w'], 'v_w_in': out['v_w_in'], 'v_sb_q_norm': out['v_sb_q_norm'], 'v_sb_k_norm': out['v_sb_k_norm'], 'v_gdn_conv_w': out['v_gdn_conv_w'], 'v_gdn_a_log': out['v_gdn_a_log'], 'v_gdn_dt_bias': out['v_gdn_dt_bias'], 'v_gdn_out_norm': out['v_gdn_out_norm'], 'v_hgrn_lb_logits': out['v_hgrn_lb_logits'], 'v_hgrn_out_norm': out['v_hgrn_out_norm'], 'v_w_branch': out['v_w_branch'], 'v_w_out': out['v_w_out']}


def _loss(weights, diff, rest, loss_target):
    with _jax.named_scope("forward"):
        args = {**rest, TWIN_DIFF_INPUT: diff, **{k: w.astype(_WEIGHT_DTYPES[k]) for k, w in weights.items()}}
        y = _forward(args)
    with _jax.named_scope("loss_head"):
        err = _jnp.square(y.astype(_jnp.float32) - loss_target)
        return 0.5 * _jnp.sum(_jnp.mean(err, axis=-1)) if err.ndim else 0.5 * err


def _adamw(w, g, m, v):
    m = ADAM_B1 * m + (1.0 - ADAM_B1) * g
    v = ADAM_B2 * v + (1.0 - ADAM_B2) * _jnp.square(g)
    m_hat = m / (1.0 - ADAM_B1 ** ADAM_STEP)
    v_hat = v / (1.0 - ADAM_B2 ** ADAM_STEP)
    delta = -ADAM_LR * (m_hat / (_jnp.sqrt(v_hat) + ADAM_EPS) + ADAM_WD * w)
    return delta, m, v


def reference(x, meta_tokens, norm_w, w_in, sb_q_norm, sb_k_norm, gdn_conv_w, gdn_a_log, gdn_dt_bias, gdn_out_norm, hgrn_lb_logits, hgrn_out_norm, w_branch, w_out, loss_target, m_meta_tokens, m_norm_w, m_w_in, m_sb_q_norm, m_sb_k_norm, m_gdn_conv_w, m_gdn_a_log, m_gdn_dt_bias, m_gdn_out_norm, m_hgrn_lb_logits, m_hgrn_out_norm, m_w_branch, m_w_out, v_meta_tokens, v_norm_w, v_w_in, v_sb_q_norm, v_sb_k_norm, v_gdn_conv_w, v_gdn_a_log, v_gdn_dt_bias, v_gdn_out_norm, v_hgrn_lb_logits, v_hgrn_out_norm, v_w_branch, v_w_out):
    given = dict(x=x, meta_tokens=meta_tokens, norm_w=norm_w, w_in=w_in, sb_q_norm=sb_q_norm, sb_k_norm=sb_k_norm, gdn_conv_w=gdn_conv_w, gdn_a_log=gdn_a_log, gdn_dt_bias=gdn_dt_bias, gdn_out_norm=gdn_out_norm, hgrn_lb_logits=hgrn_lb_logits, hgrn_out_norm=hgrn_out_norm, w_branch=w_branch, w_out=w_out, loss_target=loss_target, m_meta_tokens=m_meta_tokens, m_norm_w=m_norm_w, m_w_in=m_w_in, m_sb_q_norm=m_sb_q_norm, m_sb_k_norm=m_sb_k_norm, m_gdn_conv_w=m_gdn_conv_w, m_gdn_a_log=m_gdn_a_log, m_gdn_dt_bias=m_gdn_dt_bias, m_gdn_out_norm=m_gdn_out_norm, m_hgrn_lb_logits=m_hgrn_lb_logits, m_hgrn_out_norm=m_hgrn_out_norm, m_w_branch=m_w_branch, m_w_out=m_w_out, v_meta_tokens=v_meta_tokens, v_norm_w=v_norm_w, v_w_in=v_w_in, v_sb_q_norm=v_sb_q_norm, v_sb_k_norm=v_sb_k_norm, v_gdn_conv_w=v_gdn_conv_w, v_gdn_a_log=v_gdn_a_log, v_gdn_dt_bias=v_gdn_dt_bias, v_gdn_out_norm=v_gdn_out_norm, v_hgrn_lb_logits=v_hgrn_lb_logits, v_hgrn_out_norm=v_hgrn_out_norm, v_w_branch=v_w_branch, v_w_out=v_w_out)
    weights = {n: given[n] for n in TWIN_WEIGHTS}
    shared = {n: given[n] for n in SHARED_INPUTS}
    per_example = {n: given[n] for n in ['x']}
    grad_fn = _jax.value_and_grad(_loss, argnums=(0, 1))

    def one_microbatch(ex, loss_target):
        ex = dict(ex)
        diff = ex.pop(TWIN_DIFF_INPUT)
        return grad_fn(weights, diff, {**shared, **ex}, loss_target)

    if N_MICROBATCH == 1:
        loss, (grad_w, grad_x) = one_microbatch(per_example, given["loss_target"])
    else:
        def body(carry, xs):
            loss_sum, grad_sum = carry
            l_k, (gw_k, gx_k) = one_microbatch(xs[0], xs[1])
            with _jax.named_scope("update"):
                return (loss_sum + l_k, _jax.tree.map(_jnp.add, grad_sum, gw_k)), gx_k

        init = (_jnp.zeros((), _jnp.float32), _jax.tree.map(_jnp.zeros_like, weights))
        (loss, grad_w), grad_x = _jax.lax.scan(body, init, (per_example, given["loss_target"]))
    with _jax.named_scope("update"):
        delta_w, new_m, new_v = {}, {}, {}
        for n in TWIN_WEIGHTS:
            delta_w[n], new_m[n], new_v[n] = _adamw(weights[n], grad_w[n], given["m_" + n], given["v_" + n])
    return (loss, grad_x, *[grad_w[n] for n in TWIN_WEIGHTS], *[delta_w[n] for n in TWIN_WEIGHTS],
            *[new_m[n] for n in TWIN_WEIGHTS], *[new_v[n] for n in TWIN_WEIGHTS])
```

```python
import functools

import numpy as np
import jax
import jax.numpy as jnp
from jax import lax
from jax.experimental import pallas as pl
from jax.experimental.pallas import tpu as pltpu

f32 = jnp.float32
bf16 = jnp.bfloat16

D_MODEL = 1024
BRANCH = 512
HEAD = 128
N_HEADS = 4
CHUNK = 64
SB_BLOCK = 128
N_META = 16
FRONT = 128
PAD_FRONT = 112
EPS = 1e-6
DEPTH = 4
N_DEV = 8
N_IN = 9224
N_MAIN = 9216
N_SMALL = 128
SMALL_OFF = 4096
C_SBQ, C_SBK, C_SBV, C_SBZ = 0, 512, 1024, 1536
C_GQKV, C_GZ = 2048, 3584
C_HQ, C_HF, C_HI, C_HZ = 4096, 4608, 5120, 5632
C_MIX = 6144

ADAM_LR, ADAM_B1, ADAM_B2, ADAM_EPS, ADAM_WD, ADAM_STEP = 0.001, 0.9, 0.999, 1e-08, 0.01, 10

VMEM_LIMIT = 56 * 1024 * 1024
MESH = pl.DeviceIdType.MESH

NN = ((1,), (0,))
NT = ((1,), (1,))
TN = ((0,), (0,))


def _dot(a, b, dims=NN):
    return lax.dot_general(a.astype(bf16), b.astype(bf16), (dims, ((), ())), preferred_element_type=f32)


@jax.custom_vjp
def mm(a, b):
    return _dot(a, b, NN)


mm.defvjp(lambda a, b: (_dot(a, b, NN), (a, b)),
          lambda r, g: (_dot(g, r[1], NT), _dot(r[0], g, TN)))


@jax.custom_vjp
def mm_nt(a, b):
    return _dot(a, b, NT)


mm_nt.defvjp(lambda a, b: (_dot(a, b, NT), (a, b)),
             lambda r, g: (_dot(g, r[1], NN), _dot(g, r[0], TN)))


@jax.custom_vjp
def mm_tn(a, b):
    return _dot(a, b, TN)


mm_tn.defvjp(lambda a, b: (_dot(a, b, TN), (a, b)),
             lambda r, g: (_dot(r[1], g, NT), _dot(r[0], g, NN)))


def _split2(x):
    hi = x.astype(bf16)
    lo = (x - hi.astype(f32)).astype(bf16)
    return hi, lo


def _cdot(c, x, dims):
    hi, lo = _split2(x)
    return (lax.dot_general(c, hi, (dims, ((), ())), preferred_element_type=f32)
            + lax.dot_general(c, lo, (dims, ((), ())), preferred_element_type=f32))


@jax.custom_vjp
def cmm(c, x):
    return _cdot(c, x, NN)


cmm.defvjp(lambda c, x: (_cdot(c, x, NN), c),
           lambda c, g: (jnp.zeros_like(c), _cdot(c, g, TN)))


def _sigmoid(x):
    return jax.nn.sigmoid(x)


def _silu(x):
    return x * jax.nn.sigmoid(x)


def _softplus(x):
    return jnp.maximum(x, 0.0) + jnp.log(1.0 + jnp.exp(-jnp.abs(x)))


def _rms(x, w):
    return x * lax.rsqrt(jnp.mean(x * x, axis=-1, keepdims=True) + EPS) * w


def _cparams(sem=None):
    return pltpu.CompilerParams(dimension_semantics=sem, vmem_limit_bytes=VMEM_LIMIT)


TM_IN = 640
TN_IN = 1024


def inproj_fwd(h, nw, w_main, w_small):
    t = h.shape[0]

    def body(h_ref, nw_ref, w_ref, ws_ref, proj_ref, small_ref, xn_ref, xnt_ref):
        @pl.when(pl.program_id(1) == 0)
        def _():
            xn = _rms(h_ref[...], nw_ref[...])
            xn_ref[...] = xn.astype(bf16)
            xnt_ref[...] = jnp.transpose(xn).astype(bf16)
            small_ref[...] = _dot(xn, ws_ref[...])

        proj_ref[...] = jnp.dot(xn_ref[...], w_ref[...], preferred_element_type=f32)

    return pl.pallas_call(
        body, grid=(t // TM_IN, N_MAIN // TN_IN),
        in_specs=[pl.BlockSpec((TM_IN, D_MODEL), lambda i, j: (i, 0)),
                  pl.BlockSpec((1, D_MODEL), lambda i, j: (0, 0)),
                  pl.BlockSpec((D_MODEL, TN_IN), lambda i, j: (0, j)),
                  pl.BlockSpec((D_MODEL, N_SMALL), lambda i, j: (0, 0))],
        out_specs=[pl.BlockSpec((TM_IN, TN_IN), lambda i, j: (i, j)),
                   pl.BlockSpec((TM_IN, N_SMALL), lambda i, j: (i, 0)),
                   pl.BlockSpec((TM_IN, D_MODEL), lambda i, j: (i, 0)),
                   pl.BlockSpec((D_MODEL, TM_IN), lambda i, j: (0, i))],
        out_shape=[jax.ShapeDtypeStruct((t, N_MAIN), f32), jax.ShapeDtypeStruct((t, N_SMALL), f32),
                   jax.ShapeDtypeStruct((t, D_MODEL), bf16), jax.ShapeDtypeStruct((D_MODEL, t), bf16)],
        compiler_params=_cparams(("arbitrary", "arbitrary")), name="inproj_fwd")(h, nw, w_main, w_small)


def inproj_bwd_x(dproj, dsmall, wt_main, wt_small, h, nw, dh_out):
    t = h.shape[0]
    nk = N_MAIN // TN_IN

    def body(dp_ref, ds_ref, wt_ref, wts_ref, h_ref, nw_ref, dho_ref, dhi_ref, dnw_ref, acc):
        i, k = pl.program_id(0), pl.program_id(1)

        @pl.when(k == 0)
        def _():
            acc[...] = _dot(ds_ref[...], wts_ref[...])

        acc[...] += _dot(dp_ref[...], wt_ref[...])

        @pl.when(k == nk - 1)
        def _():
            x = h_ref[...]
            r = lax.rsqrt(jnp.mean(x * x, axis=-1, keepdims=True) + EPS)
            xh = x * r
            dxn = acc[...]
            dxh = dxn * nw_ref[...]
            dhi_ref[...] = dho_ref[...] + r * (dxh - xh * jnp.mean(dxh * xh, axis=-1, keepdims=True))
            part = jnp.sum(dxn * xh, axis=0, keepdims=True)

            @pl.when(i == 0)
            def _():
                dnw_ref[...] = part

            @pl.when(i > 0)
            def _():
                dnw_ref[...] += part

    return pl.pallas_call(
        body, grid=(t // TM_IN, nk),
        in_specs=[pl.BlockSpec((TM_IN, TN_IN), lambda i, k: (i, k)),
                  pl.BlockSpec((TM_IN, N_SMALL), lambda i, k: (i, 0)),
                  pl.BlockSpec((TN_IN, D_MODEL), lambda i, k: (k, 0)),
                  pl.BlockSpec((N_SMALL, D_MODEL), lambda i, k: (0, 0)),
                  pl.BlockSpec((TM_IN, D_MODEL), lambda i, k: (i, 0)),
                  pl.BlockSpec((1, D_MODEL), lambda i, k: (0, 0)),
                  pl.BlockSpec((TM_IN, D_MODEL), lambda i, k: (i, 0))],
        out_specs=[pl.BlockSpec((TM_IN, D_MODEL), lambda i, k: (i, 0)),
                   pl.BlockSpec((1, D_MODEL), lambda i, k: (0, 0))],
        out_shape=[jax.ShapeDtypeStruct((t, D_MODEL), f32), jax.ShapeDtypeStruct((1, D_MODEL), f32)],
        scratch_shapes=[pltpu.VMEM((TM_IN, D_MODEL), f32)],
        compiler_params=_cparams(("arbitrary", "arbitrary")), name="inproj_bwd_x")(
            dproj, dsmall, wt_main, wt_small, h, nw, dh_out)


def inproj_bwd_w(xnt, dproj, dsmall):
    t = xnt.shape[1]
    nt = t // TM_IN

    def body(xnt_ref, dp_ref, ds_ref, dw_ref, dws_ref):
        n, s = pl.program_id(0), pl.program_id(1)
        part = _dot(xnt_ref[...], dp_ref[...])

        @pl.when(s == 0)
        def _():
            dw_ref[...] = part

        @pl.when(s > 0)
        def _():
            dw_ref[...] += part

        @pl.when(n == 0)
        def _():
            ps = _dot(xnt_ref[...], ds_ref[...])

            @pl.when(s == 0)
            def _():
                dws_ref[...] = ps

            @pl.when(s > 0)
            def _():
                dws_ref[...] += ps

    return pl.pallas_call(
        body, grid=(N_MAIN // TN_IN, nt),
        in_specs=[pl.BlockSpec((D_MODEL, TM_IN), lambda n, s: (0, s)),
                  pl.BlockSpec((TM_IN, TN_IN), lambda n, s: (s, n)),
                  pl.BlockSpec((TM_IN, N_SMALL), lambda n, s: (s, 0))],
        out_specs=[pl.BlockSpec((D_MODEL, TN_IN), lambda n, s: (0, n)),
                   pl.BlockSpec((D_MODEL, N_SMALL), lambda n, s: (0, 0))],
        out_shape=[jax.ShapeDtypeStruct((D_MODEL, N_MAIN), f32), jax.ShapeDtypeStruct((D_MODEL, N_SMALL), f32)],
        compiler_params=_cparams(("arbitrary", "arbitrary")), name="inproj_bwd_w")(xnt, dproj, dsmall)


SB_SCALE = HEAD ** -0.5


def _sb_prep(k_ref, v_ref, kw_ref, kn_scr, vb_scr, nb):
    def prep(b, c):
        rows = pl.ds(pl.multiple_of(b * SB_BLOCK, SB_BLOCK), SB_BLOCK)
        kn_scr[rows, :] = _rms(k_ref[rows, :], kw_ref[...]).astype(bf16)
        vb_scr[rows, :] = v_ref[rows, :].astype(bf16)
        return c

    lax.fori_loop(0, nb, prep, 0)


def _sb_tile(qn, kt, i, j, r_carry, u_ex):
    z = lax.dot_general(qn, kt, (NT, ((), ())), preferred_element_type=f32) * SB_SCALE
    t_idx = i * SB_BLOCK + lax.broadcasted_iota(jnp.int32, (SB_BLOCK, SB_BLOCK), 0)
    s_idx = j * SB_BLOCK + lax.broadcasted_iota(jnp.int32, (SB_BLOCK, SB_BLOCK), 1)
    mask = (s_idx < t_idx) & (s_idx >= PAD_FRONT)
    lsz = jnp.minimum(z, 0.0) - jnp.log(1.0 + jnp.exp(-jnp.abs(z)))
    lk = jnp.where(mask, lsz - z, 0.0)
    within = _cdot_right(lk, u_ex)
    a = jnp.where(mask, jnp.exp(lsz + r_carry + within), 0.0)
    return mask, lsz, lk, a


def _cdot_right(x, c):
    hi, lo = _split2(x)
    return jnp.dot(hi, c, preferred_element_type=f32) + jnp.dot(lo, c, preferred_element_type=f32)


def _tri(cmp):
    r = lax.broadcasted_iota(jnp.int32, (SB_BLOCK, SB_BLOCK), 0)
    c = lax.broadcasted_iota(jnp.int32, (SB_BLOCK, SB_BLOCK), 1)
    return jnp.where(cmp(r, c), 1.0, 0.0).astype(bf16)


def sb_fwd(proj, qw, kw):
    t = proj.shape[0]
    nb = t // SB_BLOCK

    def body(q_ref, k_ref, v_ref, qw_ref, kw_ref, o_ref, kn_scr, vb_scr):
        i = pl.program_id(1)

        @pl.when(i == 0)
        def _():
            _sb_prep(k_ref, v_ref, kw_ref, kn_scr, vb_scr, nb)

        qn = _rms(q_ref[...], qw_ref[...]).astype(bf16)
        u_ex = _tri(lambda r, c: r > c)

        def step(jj, carry):
            acc, r_carry = carry
            j = i - jj
            rows = pl.ds(pl.multiple_of(j * SB_BLOCK, SB_BLOCK), SB_BLOCK)
            _, _, lk, a = _sb_tile(qn, kn_scr[rows, :], i, j, r_carry, u_ex)
            a_hi, a_lo = _split2(a)
            vt = vb_scr[rows, :]
            acc = acc + (jnp.dot(a_hi, vt, preferred_element_type=f32) + jnp.dot(a_lo, vt, preferred_element_type=f32))
            return acc, r_carry + jnp.sum(lk, axis=-1, keepdims=True)

        acc, _ = lax.fori_loop(0, i + 1, step, (jnp.zeros((SB_BLOCK, HEAD), f32), jnp.zeros((SB_BLOCK, 1), f32)))
        o_ref[...] = acc

    cb = C_SBK // HEAD
    vb = C_SBV // HEAD
    return pl.pallas_call(
        body, grid=(N_HEADS, nb),
        in_specs=[pl.BlockSpec((SB_BLOCK, HEAD), lambda h, i: (i, h)),
                  pl.BlockSpec((t, HEAD), lambda h, i: (0, cb + h)),
                  pl.BlockSpec((t, HEAD), lambda h, i: (0, vb + h)),
                  pl.BlockSpec((1, HEAD), lambda h, i: (0, 0)),
                  pl.BlockSpec((1, HEAD), lambda h, i: (0, 0))],
        out_specs=pl.BlockSpec((SB_BLOCK, HEAD), lambda h, i: (i, h)),
        out_shape=jax.ShapeDtypeStruct((t, BRANCH), f32),
        scratch_shapes=[pltpu.VMEM((t, HEAD), bf16), pltpu.VMEM((t, HEAD), bf16)],
        compiler_params=_cparams(("arbitrary", "arbitrary")), name="sb_fwd")(proj, proj, proj, qw, kw)


def sb_bwd(proj, qw, kw, o, do):
    t = proj.shape[0]
    nb = t // SB_BLOCK

    def body(q_ref, k_ref, v_ref, qw_ref, kw_ref, o_ref, do_ref, dq_ref, dk_ref, dv_ref, dqw_ref, dkw_ref,
             kn_scr, vb_scr):
        h, i = pl.program_id(0), pl.program_id(1)

        @pl.when(i == 0)
        def _():
            _sb_prep(k_ref, v_ref, kw_ref, kn_scr, vb_scr, nb)
            dk_ref[...] = jnp.zeros_like(dk_ref)
            dv_ref[...] = jnp.zeros_like(dv_ref)

        @pl.when((i == 0) & (h == 0))
        def _():
            dqw_ref[...] = jnp.zeros_like(dqw_ref)
            dkw_ref[...] = jnp.zeros_like(dkw_ref)

        q = q_ref[...]
        rq = lax.rsqrt(jnp.mean(q * q, axis=-1, keepdims=True) + EPS)
        qh = q * rq
        qn = (qh * qw_ref[...]).astype(bf16)
        do_f = do_ref[...]
        dob = do_f.astype(bf16)
        d_row = jnp.sum(dob.astype(f32) * o_ref[...], axis=-1, keepdims=True)
        u_ex = _tri(lambda r, c: r > c)
        u_in = _tri(lambda r, c: r >= c)

        def step(jj, carry):
            dq, r_carry, f_carry = carry
            j = i - jj
            rows = pl.ds(pl.multiple_of(j * SB_BLOCK, SB_BLOCK), SB_BLOCK)
            kt = kn_scr[rows, :]
            vt = vb_scr[rows, :]
            mask, lsz, lk, a = _sb_tile(qn, kt, i, j, r_carry, u_ex)
            da = lax.dot_general(dob, vt, (NT, ((), ())), preferred_element_type=f32)
            e = a * da
            e_suf = f_carry + _cdot_right(e, u_in)
            sg = jnp.exp(lsz)
            dz = jnp.where(mask, e * (1.0 - sg) - (d_row - e_suf) * sg, 0.0) * SB_SCALE
            dzb = dz.astype(bf16)
            dq = dq + jnp.dot(dzb, kt, preferred_element_type=f32)
            dk_ref[rows, :] += lax.dot_general(dzb, qn, (TN, ((), ())), preferred_element_type=f32)
            dv_ref[rows, :] += lax.dot_general(a.astype(bf16), dob, (TN, ((), ())), preferred_element_type=f32)
            return (dq, r_carry + jnp.sum(lk, axis=-1, keepdims=True), f_carry + jnp.sum(e, axis=-1, keepdims=True))

        zero_col = jnp.zeros((SB_BLOCK, 1), f32)
        dqn, _, _ = lax.fori_loop(0, i + 1, step, (jnp.zeros((SB_BLOCK, HEAD), f32), zero_col, zero_col))
        gq = dqn * qw_ref[...]
        dq_ref[...] = rq * (gq - qh * jnp.mean(gq * qh, axis=-1, keepdims=True))
        dqw_ref[...] += jnp.sum(dqn * qh, axis=0, keepdims=True)

        @pl.when(i == nb - 1)
        def _():
            def fin(b, c):
                rows = pl.ds(pl.multiple_of(b * SB_BLOCK, SB_BLOCK), SB_BLOCK)
                kk = k_ref[rows, :]
                rk = lax.rsqrt(jnp.mean(kk * kk, axis=-1, keepdims=True) + EPS)
                kh = kk * rk
                dkn = dk_ref[rows, :]
                gk = dkn * kw_ref[...]
                dk_ref[rows, :] = rk * (gk - kh * jnp.mean(gk * kh, axis=-1, keepdims=True))
                dkw_ref[...] += jnp.sum(dkn * kh, axis=0, keepdims=True)
                return c

            lax.fori_loop(0, nb, fin, 0)

    cb = C_SBK // HEAD
    vb = C_SBV // HEAD
    blk = pl.BlockSpec((SB_BLOCK, HEAD), lambda h, i: (i, h))
    full = pl.BlockSpec((t, HEAD), lambda h, i: (0, h))
    wsp = pl.BlockSpec((1, HEAD), lambda h, i: (0, 0))
    return pl.pallas_call(
        body, grid=(N_HEADS, nb),
        in_specs=[blk, pl.BlockSpec((t, HEAD), lambda h, i: (0, cb + h)),
                  pl.BlockSpec((t, HEAD), lambda h, i: (0, vb + h)), wsp, wsp, blk, blk],
        out_specs=[blk, full, full, wsp, wsp],
        out_shape=[jax.ShapeDtypeStruct((t, BRANCH), f32)] * 3 + [jax.ShapeDtypeStruct((1, HEAD), f32)] * 2,
        scratch_shapes=[pltpu.VMEM((t, HEAD), bf16), pltpu.VMEM((t, HEAD), bf16)],
        compiler_params=_cparams(("arbitrary", "arbitrary")), name="sb_bwd")(proj, proj, proj, qw, kw, o, do)


TM_CONV = 640
CONV_W = 4
GQKV = 3 * BRANCH


def conv_fwd(proj, cw):
    t = proj.shape[0]
    halo_blocks = TM_CONV // 8
    cb = C_GQKV // GQKV
    del cb

    def body(x0_ref, x1_ref, x2_ref, p0_ref, p1_ref, p2_ref, cw_ref, y_ref):
        i = pl.program_id(0)
        for s, (x_ref, p_ref) in enumerate(((x0_ref, p0_ref), (x1_ref, p1_ref), (x2_ref, p2_ref))):
            prev = jnp.where(i > 0, p_ref[...], 0.0)
            xx = jnp.concatenate([prev, x_ref[...]], axis=0)
            cols = slice(s * BRANCH, (s + 1) * BRANCH)
            y = xx[8:] * cw_ref[CONV_W - 1:CONV_W, cols]
            for k in range(CONV_W - 1):
                y = y + pltpu.roll(xx, CONV_W - 1 - k, 0)[8:] * cw_ref[k:k + 1, cols]
            y_ref[:, cols] = y

    c0 = C_GQKV // BRANCH
    xs = [pl.BlockSpec((TM_CONV, BRANCH), functools.partial(lambda i, s: (i, c0 + s), s=s)) for s in range(3)]
    ps = [pl.BlockSpec((8, BRANCH), functools.partial(lambda i, s: (jnp.maximum(i * halo_blocks - 1, 0), c0 + s), s=s))
          for s in range(3)]
    return pl.pallas_call(
        body, grid=(t // TM_CONV,),
        in_specs=xs + ps + [pl.BlockSpec((CONV_W, GQKV), lambda i: (0, 0))],
        out_specs=pl.BlockSpec((TM_CONV, GQKV), lambda i: (i, 0)),
        out_shape=jax.ShapeDtypeStruct((t, GQKV), f32),
        compiler_params=_cparams(("arbitrary",)), name="conv_fwd")(proj, proj, proj, proj, proj, proj, cw)


def conv_bwd(proj, cw, dy):
    t = proj.shape[0]
    nt = t // TM_CONV
    halo_blocks = TM_CONV // 8

    def body(x0_ref, x1_ref, x2_ref, p0_ref, p1_ref, p2_ref, cw_ref, dy_ref, dyn_ref, dx_ref, dw_ref):
        i = pl.program_id(0)

        @pl.when(i == 0)
        def _():
            dw_ref[...] = jnp.zeros_like(dw_ref)

        nxt = jnp.where(i < nt - 1, dyn_ref[...], 0.0)
        dyy = jnp.concatenate([dy_ref[...], nxt], axis=0)
        n_rows = TM_CONV + 8
        dx = dyy[:TM_CONV] * cw_ref[CONV_W - 1:CONV_W, :]
        for k in range(CONV_W - 1):
            sh = CONV_W - 1 - k
            dx = dx + pltpu.roll(dyy, n_rows - sh, 0)[:TM_CONV] * cw_ref[k:k + 1, :]
        dx_ref[...] = dx
        dy_c = dy_ref[...]
        for s, (x_ref, p_ref) in enumerate(((x0_ref, p0_ref), (x1_ref, p1_ref), (x2_ref, p2_ref))):
            prev = jnp.where(i > 0, p_ref[...], 0.0)
            xx = jnp.concatenate([prev, x_ref[...]], axis=0)
            cols = slice(s * BRANCH, (s + 1) * BRANCH)
            for k in range(CONV_W):
                sh = CONV_W - 1 - k
                xs = xx[8:] if sh == 0 else pltpu.roll(xx, sh, 0)[8:]
                dw_ref[k:k + 1, cols] += jnp.sum(xs * dy_c[:, cols], axis=0, keepdims=True)

    c0 = C_GQKV // BRANCH
    xs = [pl.BlockSpec((TM_CONV, BRANCH), functools.partial(lambda i, s: (i, c0 + s), s=s)) for s in range(3)]
    ps = [pl.BlockSpec((8, BRANCH), functools.partial(lambda i, s: (jnp.maximum(i * halo_blocks - 1, 0), c0 + s), s=s))
          for s in range(3)]
    return pl.pallas_call(
        body, grid=(nt,),
        in_specs=xs + ps + [pl.BlockSpec((CONV_W, GQKV), lambda i: (0, 0)),
                            pl.BlockSpec((TM_CONV, GQKV), lambda i: (i, 0)),
                            pl.BlockSpec((8, GQKV), lambda i: (jnp.minimum((i + 1) * halo_blocks, nt * halo_blocks - 1), 0))],
        out_specs=[pl.BlockSpec((TM_CONV, GQKV), lambda i: (i, 0)), pl.BlockSpec((CONV_W, GQKV), lambda i: (0, 0))],
        out_shape=[jax.ShapeDtypeStruct((t, GQKV), f32), jax.ShapeDtypeStruct((CONV_W, GQKV), f32)],
        compiler_params=_cparams(("arbitrary",)), name="conv_bwd")(proj, proj, proj, proj, proj, proj, cw, dy, dy)


def _iota2(n, m, d):
    return lax.broadcasted_iota(jnp.int32, (n, m), d)


def _lane_pick(row_or_mat, idx):
    lanes = lax.broadcasted_iota(jnp.int32, row_or_mat.shape, row_or_mat.ndim - 1)
    return jnp.sum(jnp.where(lanes == idx, row_or_mat, 0.0), axis=-1, keepdims=True)


def _cumsum_consts():
    i = np.arange(CHUNK)
    incl = i[None, :] <= i[:, None]
    suf = i[None, :] > i[:, None]
    return np.concatenate([incl, suf], 0).astype(np.float32)


HG_LEVELS = (64, 32, 16, 8, 4, 2)


def _hgrn_consts():
    i = np.arange(CHUNK)
    rows = [i[None, :] <= i[:, None], i[None, :] > i[:, None]]
    for b in HG_LEVELS:
        ref = (i // b) * b + b // 2 - 1
        second = (i % b) >= b // 2
        rows.append((i[None, :] > ref[:, None]) & (i[None, :] <= i[:, None]) & second[:, None])
        rows.append((i[None, :] > i[:, None]) & (i[None, :] <= ref[:, None]) & (~second)[:, None])
    return np.concatenate(rows, 0).astype(np.float32)


def _gdn_chunk(state, ypre, small, gz, a_log, dt_b, on_w, c2, vm):
    r = _iota2(CHUNK, CHUNK, 0)
    c = _iota2(CHUNK, CHUNK, 1)
    causal = r >= c
    strict = r > c
    outs, new_states = [], []
    for h in range(N_HEADS):
        hs = slice(h * HEAD, (h + 1) * HEAD)
        q = _silu(ypre[:, h * HEAD:(h + 1) * HEAD])
        k = _silu(ypre[:, BRANCH + h * HEAD:BRANCH + (h + 1) * HEAD])
        v = _silu(ypre[:, 2 * BRANCH + h * HEAD:2 * BRANCH + (h + 1) * HEAD])
        q = q * lax.rsqrt(jnp.sum(q * q, axis=-1, keepdims=True) + EPS) * (HEAD ** -0.5)
        k = k * lax.rsqrt(jnp.sum(k * k, axis=-1, keepdims=True) + EPS)
        beta = _sigmoid(_lane_pick(small, h)) * vm
        g = -jnp.exp(_lane_pick(a_log, h)) * _softplus(_lane_pick(small, N_HEADS + h) + _lane_pick(dt_b, h))
        e2 = cmm(c2, jnp.broadcast_to(g, (CHUNK, HEAD)))
        gc = e2[:CHUNK]
        gsuf = e2[CHUNK:]
        g_row = jnp.transpose(gc)[:CHUNK, :]
        dec = jnp.where(causal, jnp.exp(jnp.minimum(gc[:, :CHUNK] - g_row, 0.0)), 0.0)
        kb = k * beta
        m = jnp.where(strict, mm_nt(kb, k) * dec, 0.0)
        x = jnp.concatenate([v * beta, kb * jnp.exp(gc)], axis=1)
        x = x - mm(m, x)
        p = m
        for _ in range(5):
            p = mm(p, p)
            x = x + mm(p, x)
        u, w = x[:, :HEAD], x[:, HEAD:]
        aqk = jnp.where(causal, mm_nt(q, k) * dec, 0.0)
        g_last = jnp.exp(jnp.sum(jnp.broadcast_to(g, (CHUNK, HEAD)), axis=0, keepdims=True))
        s_h = state[h]
        v_new = u - mm(w, s_h)
        o = mm(q * jnp.exp(gc), s_h) + mm(aqk, v_new)
        new_states.append(s_h * g_last + mm_tn(k * jnp.exp(gsuf), v_new))
        outs.append(_rms(o, on_w))
        del hs
    out = jnp.concatenate(outs, axis=1) * _silu(gz)
    return jnp.stack(new_states, axis=0), out


def _hgrn_chunk(state, hq, hf, hi, hz, lb, on_w, cm, vm):
    r = _iota2(CHUNK, CHUNK, 0)
    c = _iota2(CHUNK, CHUNK, 1)
    outs, new_states = [], []
    for h in range(N_HEADS):
        hs = slice(h * HEAD, (h + 1) * HEAD)
        lbh = lb[:, hs]
        q = _silu(hq[:, hs])
        f_pre = hf[:, hs]
        forget = lbh + (1.0 - lbh) * _sigmoid(f_pre)
        k = (1.0 - lbh) * _sigmoid(-f_pre)
        v = hi[:, hs] * vm
        g = jnp.log(forget)
        e = cmm(cm, g)
        gc, gsuf = e[:CHUNK], e[CHUNK:2 * CHUNK]
        st = state[h]
        o = mm_nt(q * jnp.exp(gc), st)
        a = jnp.where(r == c, jnp.sum(q * k, axis=-1, keepdims=True), 0.0)
        for li, b in enumerate(HG_LEVELS):
            d1 = e[(2 + 2 * li) * CHUNK:(3 + 2 * li) * CHUNK]
            d2 = e[(3 + 2 * li) * CHUNK:(4 + 2 * li) * CHUNK]
            sh = b.bit_length() - 1
            pair = ((r >> sh) == (c >> sh)) & ((r & (b - 1)) >= b // 2) & ((c & (b - 1)) < b // 2)
            a = a + jnp.where(pair, mm_nt(q * jnp.exp(d1), k * jnp.exp(d2)), 0.0)
        o = o + mm(a, v)
        g_end = jnp.sum(g, axis=0, keepdims=True)
        new_states.append(st * jnp.exp(g_end) + mm_tn(v, k * jnp.exp(gsuf)))
        outs.append(_rms(o, on_w))
    out = jnp.concatenate(outs, axis=1) * _silu(hz)
    return jnp.stack(new_states, axis=0), out


def _vmask(chunk_idx):
    rows = chunk_idx * CHUNK + lax.broadcasted_iota(jnp.int32, (CHUNK, 1), 0)
    return jnp.where(rows >= PAD_FRONT, 1.0, 0.0)


def _row(n):
    return pl.BlockSpec((1, n), lambda i: (0, 0))


def gdn_fwd(ypre, small, proj, a_log, dt_b, on_w):
    t = ypre.shape[0]
    nc = t // CHUNK
    c2 = jnp.asarray(_cumsum_consts(), bf16)

    def body(y_ref, s_ref, z_ref, al_ref, dt_ref, on_ref, c2_ref, o_ref, st_ref, state):
        i = pl.program_id(0)

        @pl.when(i == 0)
        def _():
            state[...] = jnp.zeros_like(state)

        s_in = state[...]
        st_ref[0] = s_in
        s_new, out = _gdn_chunk(s_in, y_ref[...], s_ref[...], z_ref[...], al_ref[...], dt_ref[...], on_ref[...],
                                c2_ref[...], _vmask(i))
        state[...] = s_new
        o_ref[...] = out

    return pl.pallas_call(
        body, grid=(nc,),
        in_specs=[pl.BlockSpec((CHUNK, GQKV), lambda i: (i, 0)), pl.BlockSpec((CHUNK, N_SMALL), lambda i: (i, 0)),
                  pl.BlockSpec((CHUNK, BRANCH), lambda i: (i, C_GZ // BRANCH)), _row(128), _row(128), _row(128),
                  pl.BlockSpec((2 * CHUNK, CHUNK), lambda i: (0, 0))],
        out_specs=[pl.BlockSpec((CHUNK, BRANCH), lambda i: (i, 0)),
                   pl.BlockSpec((1, N_HEADS, HEAD, HEAD), lambda i: (i, 0, 0, 0))],
        out_shape=[jax.ShapeDtypeStruct((t, BRANCH), f32), jax.ShapeDtypeStruct((nc, N_HEADS, HEAD, HEAD), f32)],
        scratch_shapes=[pltpu.VMEM((N_HEADS, HEAD, HEAD), f32)],
        compiler_params=_cparams(("arbitrary",)), name="gdn_fwd")(ypre, small, proj, a_log, dt_b, on_w, c2)


def gdn_bwd(ypre, small, proj, a_log, dt_b, on_w, states, d_out):
    t = ypre.shape[0]
    nc = t // CHUNK
    c2 = jnp.asarray(_cumsum_consts(), bf16)

    def body(y_ref, s_ref, z_ref, al_ref, dt_ref, on_ref, c2_ref, st_ref, do_ref,
             dy_ref, ds_ref, dz_ref, dal_ref, ddt_ref, don_ref, dstate):
        i = pl.program_id(0)

        @pl.when(i == 0)
        def _():
            dstate[...] = jnp.zeros_like(dstate)
            dal_ref[...] = jnp.zeros_like(dal_ref)
            ddt_ref[...] = jnp.zeros_like(ddt_ref)
            don_ref[...] = jnp.zeros_like(don_ref)

        vm = _vmask(nc - 1 - i)
        c2v = c2_ref[...]
        fn = lambda s, y, sm, z, al, dt, on: _gdn_chunk(s, y, sm, z, al, dt, on, c2v, vm)
        _, vjp = jax.vjp(fn, st_ref[0], y_ref[...], s_ref[...], z_ref[...], al_ref[...], dt_ref[...], on_ref[...])
        d_s, d_y, d_sm, d_z, d_al, d_dt, d_on = vjp((dstate[...], do_ref[...]))
        dstate[...] = d_s
        dy_ref[...] = d_y
        ds_ref[...] = d_sm
        dz_ref[...] = d_z
        dal_ref[...] += d_al
        ddt_ref[...] += d_dt
        don_ref[...] += d_on

    rev = lambda i: (nc - 1 - i, 0)
    return pl.pallas_call(
        body, grid=(nc,),
        in_specs=[pl.BlockSpec((CHUNK, GQKV), rev), pl.BlockSpec((CHUNK, N_SMALL), rev),
                  pl.BlockSpec((CHUNK, BRANCH), lambda i: (nc - 1 - i, C_GZ // BRANCH)), _row(128), _row(128), _row(128),
                  pl.BlockSpec((2 * CHUNK, CHUNK), lambda i: (0, 0)),
                  pl.BlockSpec((1, N_HEADS, HEAD, HEAD), lambda i: (nc - 1 - i, 0, 0, 0)),
                  pl.BlockSpec((CHUNK, BRANCH), rev)],
        out_specs=[pl.BlockSpec((CHUNK, GQKV), rev), pl.BlockSpec((CHUNK, N_SMALL), rev),
                   pl.BlockSpec((CHUNK, BRANCH), rev), _row(128), _row(128), _row(128)],
        out_shape=[jax.ShapeDtypeStruct((t, GQKV), f32), jax.ShapeDtypeStruct((t, N_SMALL), f32),
                   jax.ShapeDtypeStruct((t, BRANCH), f32)] + [jax.ShapeDtypeStruct((1, 128), f32)] * 3,
        scratch_shapes=[pltpu.VMEM((N_HEADS, HEAD, HEAD), f32)],
        compiler_params=_cparams(("arbitrary",)), name="gdn_bwd")(
            ypre, small, proj, a_log, dt_b, on_w, c2, states, d_out)


def hgrn_fwd(proj, lb, on_w):
    t = proj.shape[0]
    nc = t // CHUNK
    cm = jnp.asarray(_hgrn_consts(), bf16)
    ncm = cm.shape[0]

    def body(q_ref, f_ref, i_ref, z_ref, lb_ref, on_ref, cm_ref, o_ref, st_ref, state):
        i = pl.program_id(0)

        @pl.when(i == 0)
        def _():
            state[...] = jnp.zeros_like(state)

        s_in = state[...]
        st_ref[0] = s_in
        s_new, out = _hgrn_chunk(s_in, q_ref[...], f_ref[...], i_ref[...], z_ref[...], lb_ref[...], on_ref[...],
                                 cm_ref[...], _vmask(i))
        state[...] = s_new
        o_ref[...] = out

    sec = lambda off: pl.BlockSpec((CHUNK, BRANCH), functools.partial(lambda i, b: (i, b), b=off // BRANCH))
    return pl.pallas_call(
        body, grid=(nc,),
        in_specs=[sec(C_HQ), sec(C_HF), sec(C_HI), sec(C_HZ), _row(BRANCH), _row(128),
                  pl.BlockSpec((ncm, CHUNK), lambda i: (0, 0))],
        out_specs=[pl.BlockSpec((CHUNK, BRANCH), lambda i: (i, 0)),
                   pl.BlockSpec((1, N_HEADS, HEAD, HEAD), lambda i: (i, 0, 0, 0))],
        out_shape=[jax.ShapeDtypeStruct((t, BRANCH), f32), jax.ShapeDtypeStruct((nc, N_HEADS, HEAD, HEAD), f32)],
        scratch_shapes=[pltpu.VMEM((N_HEADS, HEAD, HEAD), f32)],
        compiler_params=_cparams(("arbitrary",)), name="hgrn_fwd")(proj, proj, proj, proj, lb, on_w, cm)


def hgrn_bwd(proj, lb, on_w, states, d_out):
    t = proj.shape[0]
    nc = t // CHUNK
    cm = jnp.asarray(_hgrn_consts(), bf16)
    ncm = cm.shape[0]

    def body(q_ref, f_ref, i_ref, z_ref, lb_ref, on_ref, cm_ref, st_ref, do_ref, dh_ref, dlb_ref, don_ref, dstate):
        i = pl.program_id(0)

        @pl.when(i == 0)
        def _():
            dstate[...] = jnp.zeros_like(dstate)
            dlb_ref[...] = jnp.zeros_like(dlb_ref)
            don_ref[...] = jnp.zeros_like(don_ref)

        vm = _vmask(nc - 1 - i)
        cmv = cm_ref[...]
        fn = lambda s, a, b, c, d, l, on: _hgrn_chunk(s, a, b, c, d, l, on, cmv, vm)
        _, vjp = jax.vjp(fn, st_ref[0], q_ref[...], f_ref[...], i_ref[...], z_ref[...], lb_ref[...], on_ref[...])
        d_s, d_q, d_f, d_i, d_z, d_lb, d_on = vjp((dstate[...], do_ref[...]))
        dstate[...] = d_s
        dh_ref[...] = jnp.concatenate([d_q, d_f, d_i, d_z], axis=1)
        dlb_ref[...] += d_lb
        don_ref[...] += d_on

    rev = lambda i: (nc - 1 - i, 0)
    sec = lambda off: pl.BlockSpec((CHUNK, BRANCH), functools.partial(lambda i, b: (nc - 1 - i, b), b=off // BRANCH))
    return pl.pallas_call(
        body, grid=(nc,),
        in_specs=[sec(C_HQ), sec(C_HF), sec(C_HI), sec(C_HZ), _row(BRANCH), _row(128),
                  pl.BlockSpec((ncm, CHUNK), lambda i: (0, 0)),
                  pl.BlockSpec((1, N_HEADS, HEAD, HEAD), lambda i: (nc - 1 - i, 0, 0, 0)),
                  pl.BlockSpec((CHUNK, BRANCH), rev)],
        out_specs=[pl.BlockSpec((CHUNK, 4 * BRANCH), rev), _row(BRANCH), _row(128)],
        out_shape=[jax.ShapeDtypeStruct((t, 4 * BRANCH), f32), jax.ShapeDtypeStruct((1, BRANCH), f32),
                   jax.ShapeDtypeStruct((1, 128), f32)],
        scratch_shapes=[pltpu.VMEM((N_HEADS, HEAD, HEAD), f32)],
        compiler_params=_cparams(("arbitrary",)), name="hgrn_bwd")(proj, proj, proj, proj, lb, on_w, cm, states, d_out)


TM_MG = 320


def _const_spec(shape):
    nd = len(shape)
    return pl.BlockSpec(shape, lambda i: (0,) * nd, pipeline_mode=pl.Buffered(1))


def merge_fwd(osb, proj, ogd, ohg, wb, wo, h):
    t = h.shape[0]

    def body(osb_ref, sbz_ref, ogd_ref, ohg_ref, mix_ref, wb_ref, wo_ref, h_ref, out_ref):
        a = osb_ref[...] * _silu(sbz_ref[...])
        y = (_sigmoid(mix_ref[:, 0:D_MODEL]) * _dot(a, wb_ref[0])
             + _sigmoid(mix_ref[:, D_MODEL:2 * D_MODEL]) * _dot(ogd_ref[...], wb_ref[1])
             + _sigmoid(mix_ref[:, 2 * D_MODEL:3 * D_MODEL]) * _dot(ohg_ref[...], wb_ref[2]))
        out_ref[...] = h_ref[...] + _dot(y, wo_ref[...])

    br = pl.BlockSpec((TM_MG, BRANCH), lambda i: (i, 0))
    return pl.pallas_call(
        body, grid=(t // TM_MG,),
        in_specs=[br, pl.BlockSpec((TM_MG, BRANCH), lambda i: (i, C_SBZ // BRANCH)), br, br,
                  pl.BlockSpec((TM_MG, 3 * D_MODEL), lambda i: (i, C_MIX // (3 * D_MODEL))),
                  _const_spec((3, BRANCH, D_MODEL)), _const_spec((D_MODEL, D_MODEL)),
                  pl.BlockSpec((TM_MG, D_MODEL), lambda i: (i, 0))],
        out_specs=pl.BlockSpec((TM_MG, D_MODEL), lambda i: (i, 0)),
        out_shape=jax.ShapeDtypeStruct((t, D_MODEL), f32),
        compiler_params=_cparams(("arbitrary",)), name="merge_fwd")(osb, proj, ogd, ohg, proj, wb, wo, h)


def merge_bwd(osb, proj, ogd, ohg, wb, wbt, wot, dh):
    t = dh.shape[0]

    def body(osb_ref, sbz_ref, ogd_ref, ohg_ref, mix_ref, wb_ref, wbt_ref, wot_ref, dh_ref,
             dosb_ref, dsbz_ref, dogd_ref, dohg_ref, dmix_ref, dwo_ref, dwb_ref):
        i = pl.program_id(0)

        @pl.when(i == 0)
        def _():
            dwo_ref[...] = jnp.zeros_like(dwo_ref)
            dwb_ref[...] = jnp.zeros_like(dwb_ref)

        osb = osb_ref[...]
        sbz = sbz_ref[...]
        sgz = _sigmoid(sbz)
        sz = sbz * sgz
        branch_in = (osb * sz, ogd_ref[...], ohg_ref[...])
        dh_v = dh_ref[...]
        dy = _dot(dh_v, wot_ref[...])
        y = jnp.zeros((TM_MG, D_MODEL), f32)
        d_in = []
        for b in range(3):
            p = _dot(branch_in[b], wb_ref[b])
            g = _sigmoid(mix_ref[:, b * D_MODEL:(b + 1) * D_MODEL])
            y = y + g * p
            dp = dy * g
            dmix_ref[:, b * D_MODEL:(b + 1) * D_MODEL] = dy * p * g * (1.0 - g)
            d_in.append(_dot(dp, wbt_ref[b]))
            dwb_ref[b] += _dot(branch_in[b], dp, TN)
        dwo_ref[...] += _dot(y, dh_v, TN)
        dosb_ref[...] = d_in[0] * sz
        dsbz_ref[...] = d_in[0] * osb * (sgz * (1.0 + sbz * (1.0 - sgz)))
        dogd_ref[...] = d_in[1]
        dohg_ref[...] = d_in[2]

    br = pl.BlockSpec((TM_MG, BRANCH), lambda i: (i, 0))
    return pl.pallas_call(
        body, grid=(t // TM_MG,),
        in_specs=[br, pl.BlockSpec((TM_MG, BRANCH), lambda i: (i, C_SBZ // BRANCH)), br, br,
                  pl.BlockSpec((TM_MG, 3 * D_MODEL), lambda i: (i, C_MIX // (3 * D_MODEL))),
                  _const_spec((3, BRANCH, D_MODEL)), _const_spec((3, D_MODEL, BRANCH)), _const_spec((D_MODEL, D_MODEL)),
                  pl.BlockSpec((TM_MG, D_MODEL), lambda i: (i, 0))],
        out_specs=[br, br, br, br, pl.BlockSpec((TM_MG, 3 * D_MODEL), lambda i: (i, 0)),
                   _const_spec((D_MODEL, D_MODEL)), _const_spec((3, BRANCH, D_MODEL))],
        out_shape=[jax.ShapeDtypeStruct((t, BRANCH), f32)] * 4 + [jax.ShapeDtypeStruct((t, 3 * D_MODEL), f32),
                   jax.ShapeDtypeStruct((D_MODEL, D_MODEL), f32), jax.ShapeDtypeStruct((3, BRANCH, D_MODEL), f32)],
        compiler_params=_cparams(("arbitrary",)), name="merge_bwd")(osb, proj, ogd, ohg, proj, wb, wbt, wot, dh)


def loss_head(h, target):
    t = h.shape[0]
    nb = t // SB_BLOCK

    def body(h_ref, t_ref, dh_ref, loss_ref):
        i = pl.program_id(0)

        @pl.when(i == 0)
        def _():
            loss_ref[...] = jnp.zeros_like(loss_ref)
            dh_ref[...] = jnp.zeros_like(dh_ref)

        @pl.when(i > 0)
        def _():
            err = h_ref[...] - t_ref[...]
            dh_ref[...] = err * (1.0 / D_MODEL)
            loss_ref[...] += jnp.broadcast_to(jnp.sum(err * err) * (0.5 / D_MODEL), loss_ref.shape)

    return pl.pallas_call(
        body, grid=(nb,),
        in_specs=[pl.BlockSpec((SB_BLOCK, D_MODEL), lambda i: (i, 0)),
                  pl.BlockSpec((SB_BLOCK, D_MODEL), lambda i: (jnp.maximum(i - 1, 0), 0))],
        out_specs=[pl.BlockSpec((SB_BLOCK, D_MODEL), lambda i: (i, 0)), pl.BlockSpec((1, 128), lambda i: (0, 0))],
        out_shape=[jax.ShapeDtypeStruct((t, D_MODEL), f32), jax.ShapeDtypeStruct((1, 128), f32)],
        compiler_params=_cparams(("arbitrary",)), name="loss_head")(h, target)


def adamw(parts, w, m, v, rows_per_step, name):
    r, c = w.shape
    tr = min(rows_per_step, r)

    def body(p_ref, w_ref, m_ref, v_ref, g_ref, d_ref, nm_ref, nv_ref):
        g = p_ref[0]
        for k in range(1, N_DEV):
            g = g + p_ref[k]
        m_new = ADAM_B1 * m_ref[...] + (1.0 - ADAM_B1) * g
        v_new = ADAM_B2 * v_ref[...] + (1.0 - ADAM_B2) * jnp.square(g)
        m_hat = m_new / (1.0 - ADAM_B1 ** ADAM_STEP)
        v_hat = v_new / (1.0 - ADAM_B2 ** ADAM_STEP)
        g_ref[...] = g
        d_ref[...] = -ADAM_LR * (m_hat / (jnp.sqrt(v_hat) + ADAM_EPS) + ADAM_WD * w_ref[...])
        nm_ref[...] = m_new
        nv_ref[...] = v_new

    blk = pl.BlockSpec((tr, c), lambda i: (i, 0))
    return pl.pallas_call(
        body, grid=(r // tr,),
        in_specs=[pl.BlockSpec((N_DEV, tr, c), lambda i: (0, i, 0)), blk, blk, blk],
        out_specs=[blk] * 4, out_shape=[jax.ShapeDtypeStruct((r, c), f32)] * 4,
        compiler_params=_cparams(("arbitrary",)), name=name)(parts, w, m, v)


def _mesh_pos():
    return lax.axis_index("x"), lax.axis_index("y"), lax.axis_index("c")


def _peer(pos, k):
    x, y, c = pos
    return (1 - x if k & 4 else x, 1 - y if k & 2 else y, 1 - c if k & 1 else c)


def _lin(pos):
    return 4 * pos[0] + 2 * pos[1] + pos[2]


def exchange(srcs, scatter, name):
    n = len(srcs)
    shapes = [s.shape[1:] if sc else s.shape for s, sc in zip(srcs, scatter)]

    def body(*refs):
        src_refs, dst_refs = refs[:n], refs[n:2 * n]
        send_sems, recv_sems, local_sems = refs[2 * n:]
        me = _mesh_pos()
        me_lin = _lin(me)
        sends, recvs, locals_ = [], [], []
        for t in range(n):
            own = src_refs[t].at[me_lin] if scatter[t] else src_refs[t]
            locals_.append(pltpu.make_async_copy(own, dst_refs[t].at[me_lin], local_sems.at[t]))
            for k in range(1, N_DEV):
                peer = _peer(me, k)
                src = src_refs[t].at[_lin(peer)] if scatter[t] else src_refs[t]
                sends.append(pltpu.make_async_remote_copy(
                    src_ref=src, dst_ref=dst_refs[t].at[me_lin], send_sem=send_sems.at[t, k - 1],
                    recv_sem=recv_sems.at[t, k - 1], device_id=peer, device_id_type=MESH))
                recvs.append(pltpu.make_async_remote_copy(
                    src_ref=src, dst_ref=dst_refs[t].at[_lin(peer)], send_sem=send_sems.at[t, k - 1],
                    recv_sem=recv_sems.at[t, k - 1], device_id=peer, device_id_type=MESH))
        for cp in locals_ + sends:
            cp.start()
        for cp in sends:
            cp.wait_send()
        for cp in recvs:
            cp.wait_recv()
        for cp in locals_:
            cp.wait()

    any_spec = pl.BlockSpec(memory_space=pl.ANY)
    return pl.pallas_call(
        body, in_specs=[any_spec] * n, out_specs=[any_spec] * n,
        out_shape=[jax.ShapeDtypeStruct((N_DEV,) + tuple(sh), s.dtype) for sh, s in zip(shapes, srcs)],
        scratch_shapes=[pltpu.SemaphoreType.DMA((n, N_DEV - 1)), pltpu.SemaphoreType.DMA((n, N_DEV - 1)),
                        pltpu.SemaphoreType.DMA((n,))],
        compiler_params=pltpu.CompilerParams(has_side_effects=True), name=name)(*srcs)


PACK_ROWS = 104


def _pad_rows(a, rows):
    return jnp.pad(a, ((0, rows - a.shape[0]), (0, 0)))


def _pad_lanes(a):
    return jnp.pad(a, ((0, 0), (0, 128 - a.shape[1])))


def _pack(norm_w, sbq, sbk, alog, dtb, gon, lbl, hon, loss_row):
    parts = [norm_w.reshape(32, 128), _pad_rows(sbq, 8), _pad_rows(sbk, 8), _pad_rows(_pad_lanes(alog), 8),
             _pad_rows(_pad_lanes(dtb), 8), _pad_rows(gon, 8), lbl.reshape(16, 128), _pad_rows(hon, 8),
             _pad_rows(loss_row, 8)]
    return jnp.concatenate(parts, axis=0)


def _unpack(p):
    return dict(norm_w=p[0:32].reshape(DEPTH, D_MODEL), sb_q_norm=p[32:36], sb_k_norm=p[40:44],
                gdn_a_log=p[48:52, :N_HEADS], gdn_dt_bias=p[56:60, :N_HEADS], gdn_out_norm=p[64:68],
                hgrn_lb_logits=p[72:88].reshape(DEPTH, BRANCH), hgrn_out_norm=p[88:92], loss=p[96, 0])


def _lower_bounds(logits):
    p = jax.nn.softmax(logits, axis=0)
    return jnp.cumsum(p, axis=0) - p[0:1]


def _unshard_cols(g):
    nd = g.ndim
    g = jnp.moveaxis(g, 0, nd - 2)
    return g.reshape(g.shape[:-2] + (N_DEV * g.shape[-1],))


def _shard_cols(a):
    n = a.shape[-1] // N_DEV
    return jnp.moveaxis(a.reshape(a.shape[:-1] + (N_DEV, n)), -2, 0)


def kernel(x, meta_tokens, norm_w, w_in, sb_q_norm, sb_k_norm, gdn_conv_w, gdn_a_log, gdn_dt_bias, gdn_out_norm, hgrn_lb_logits, hgrn_out_norm, w_branch, w_out, loss_target, m_meta_tokens, m_norm_w, m_w_in, m_sb_q_norm, m_sb_k_norm, m_gdn_conv_w, m_gdn_a_log, m_gdn_dt_bias, m_gdn_out_norm, m_hgrn_lb_logits, m_hgrn_out_norm, m_w_branch, m_w_out, v_meta_tokens, v_norm_w, v_w_in, v_sb_q_norm, v_sb_k_norm, v_gdn_conv_w, v_gdn_a_log, v_gdn_dt_bias, v_gdn_out_norm, v_hgrn_lb_logits, v_hgrn_out_norm, v_w_branch, v_w_out):
    g_win, g_wbr, g_wout, g_meta, g_conv = exchange(
        [w_in.astype(bf16), w_branch.astype(bf16), w_out.astype(bf16), meta_tokens, gdn_conv_w],
        [False] * 5, "gather_weights")
    w_full = _unshard_cols(g_win)
    w_main = jnp.concatenate([w_full[..., :SMALL_OFF], w_full[..., SMALL_OFF + 8:]], axis=-1)
    w_small = jnp.pad(w_full[..., SMALL_OFF:SMALL_OFF + 8], ((0, 0), (0, 0), (0, N_SMALL - 8)))
    wt_main = jnp.swapaxes(w_main, 1, 2)
    wt_small = jnp.swapaxes(w_small, 1, 2)
    wbr = _unshard_cols(g_wbr)
    wbr_t = jnp.swapaxes(wbr, 2, 3)
    wout = jnp.moveaxis(g_wout, 0, 1).reshape(DEPTH, D_MODEL, D_MODEL)
    wout_t = jnp.swapaxes(wout, 1, 2)
    meta = _unshard_cols(g_meta)
    conv_w = _unshard_cols(g_conv)
    lbounds, lb_vjp = jax.vjp(_lower_bounds, hgrn_lb_logits)

    h = jnp.concatenate([jnp.zeros((PAD_FRONT, D_MODEL), f32), meta, x[0]], axis=0)
    row = lambda a: a.reshape(1, -1)
    saved = []
    for l in range(DEPTH):
        proj, small, _, xnt = inproj_fwd(h, row(norm_w[l]), w_main[l], w_small[l])
        osb = sb_fwd(proj, row(sb_q_norm[l]), row(sb_k_norm[l]))
        ypre = conv_fwd(proj, conv_w[l])
        al, dtb = _pad_lanes(row(gdn_a_log[l])), _pad_lanes(row(gdn_dt_bias[l]))
        ogd, gst = gdn_fwd(ypre, small, proj, al, dtb, row(gdn_out_norm[l]))
        ohg, hst = hgrn_fwd(proj, row(lbounds[l]), row(hgrn_out_norm[l]))
        h_next = merge_fwd(osb, proj, ogd, ohg, wbr[l], wout[l], h)
        saved.append((h, proj, small, xnt, osb, ypre, ogd, gst, ohg, hst, al, dtb))
        h = h_next

    dh, loss_row = loss_head(h, loss_target[0])

    gw_main, gw_small, gw_br, gw_out, g_conv_w = [None] * DEPTH, [None] * DEPTH, [None] * DEPTH, [None] * DEPTH, [None] * DEPTH
    g_norm, g_sbq, g_sbk, g_al, g_dt, g_gon, g_lb, g_hon = ([None] * DEPTH for _ in range(8))
    for l in reversed(range(DEPTH)):
        h_l, proj, small, xnt, osb, ypre, ogd, gst, ohg, hst, al, dtb = saved[l]
        d_osb, d_sbz, d_ogd, d_ohg, d_mix, gw_out[l], gw_br[l] = merge_bwd(osb, proj, ogd, ohg, wbr[l], wbr_t[l], wout_t[l], dh)
        d_hg, g_lb[l], g_hon[l] = hgrn_bwd(proj, row(lbounds[l]), row(hgrn_out_norm[l]), hst, d_ohg)
        d_ypre, d_small, d_gz, g_al[l], g_dt[l], g_gon[l] = gdn_bwd(ypre, small, proj, al, dtb, row(gdn_out_norm[l]), gst, d_ogd)
        d_gqkv, g_conv_w[l] = conv_bwd(proj, conv_w[l], d_ypre)
        d_q, d_k, d_v, g_sbq[l], g_sbk[l] = sb_bwd(proj, row(sb_q_norm[l]), row(sb_k_norm[l]), osb, d_osb)
        dproj = jnp.concatenate([d_q, d_k, d_v, d_sbz, d_gqkv, d_gz, d_hg, d_mix], axis=1)
        gw_main[l], gw_small[l] = inproj_bwd_w(xnt, dproj, d_small)
        dh, g_norm[l] = inproj_bwd_x(dproj, d_small, wt_main[l], wt_small[l], h_l, row(norm_w[l]), dh)

    gw_main, gw_small = jnp.stack(gw_main), jnp.stack(gw_small)
    gw_in = jnp.concatenate([gw_main[..., :SMALL_OFF], gw_small[..., :8], gw_main[..., SMALL_OFF:]], axis=-1)
    d_lbl = lb_vjp(jnp.concatenate(g_lb, axis=0))[0]
    cat = lambda rows: jnp.concatenate(rows, axis=0)
    pack = _pack(cat(g_norm), cat(g_sbq), cat(g_sbk), cat(g_al)[:, :N_HEADS], cat(g_dt)[:, :N_HEADS], cat(g_gon),
                 d_lbl, cat(g_hon), loss_row)
    g_meta_full = dh[PAD_FRONT:FRONT]
    r_win, r_wbr, r_wout, r_meta, r_conv, r_pack = exchange(
        [_shard_cols(gw_in), _shard_cols(jnp.stack(gw_br)), jnp.swapaxes(jnp.stack(gw_out).reshape(DEPTH, N_DEV, HEAD, D_MODEL), 0, 1),
         _shard_cols(g_meta_full), _shard_cols(jnp.stack(g_conv_w)), pack],
        [True, True, True, True, True, False], "exchange_grads")

    def upd(parts, w, m, v, rows, name):
        shp = w.shape
        two = (-1, shp[-1])
        outs = adamw(parts.reshape((N_DEV,) + w.reshape(two).shape), w.reshape(two), m.reshape(two), v.reshape(two), rows, name)
        return [o.reshape(shp) for o in outs]

    res = {}
    res["w_in"] = upd(r_win, w_in, m_w_in, v_w_in, 256, "adamw_w_in")
    res["w_branch"] = upd(r_wbr, w_branch, m_w_branch, v_w_branch, 1024, "adamw_w_branch")
    res["w_out"] = upd(r_wout, w_out, m_w_out, v_w_out, 256, "adamw_w_out")
    res["meta_tokens"] = upd(r_meta, meta_tokens, m_meta_tokens, v_meta_tokens, 16, "adamw_meta")
    res["gdn_conv_w"] = upd(r_conv, gdn_conv_w, m_gdn_conv_w, v_gdn_conv_w, 16, "adamw_conv")
    zero_row = jnp.zeros((1, 128), f32)
    w_pack = _pack(norm_w, sb_q_norm, sb_k_norm, gdn_a_log, gdn_dt_bias, gdn_out_norm, hgrn_lb_logits, hgrn_out_norm, zero_row)
    m_pack = _pack(m_norm_w, m_sb_q_norm, m_sb_k_norm, m_gdn_a_log, m_gdn_dt_bias, m_gdn_out_norm, m_hgrn_lb_logits, m_hgrn_out_norm, zero_row)
    v_pack = _pack(v_norm_w, v_sb_q_norm, v_sb_k_norm, v_gdn_a_log, v_gdn_dt_bias, v_gdn_out_norm, v_hgrn_lb_logits, v_hgrn_out_norm, zero_row)
    packed = [_unpack(o) for o in adamw(r_pack, w_pack, m_pack, v_pack, PACK_ROWS, "adamw_replicated")]
    for name in ("norm_w", "sb_q_norm", "sb_k_norm", "gdn_a_log", "gdn_dt_bias", "gdn_out_norm", "hgrn_lb_logits", "hgrn_out_norm"):
        res[name] = [p[name] for p in packed]
    loss = packed[0]["loss"]
    grad_x = dh[FRONT:][None]

    order = ["meta_tokens", "norm_w", "w_in", "sb_q_norm", "sb_k_norm", "gdn_conv_w", "gdn_a_log", "gdn_dt_bias",
             "gdn_out_norm", "hgrn_lb_logits", "hgrn_out_norm", "w_branch", "w_out"]
    return (loss, grad_x, *[res[n][0] for n in order], *[res[n][1] for n in order],
            *[res[n][2] for n in order], *[res[n][3] for n in order])
```

```python
import functools

import numpy as np
import jax
import jax.numpy as jnp
from jax import lax
from jax.experimental import pallas as pl
from jax.experimental.pallas import tpu as pltpu

f32 = jnp.float32
bf16 = jnp.bfloat16

D_MODEL = 1024
BRANCH = 512
HEAD = 128
N_HEADS = 4
CHUNK = 64
SB_BLOCK = 128
N_META = 16
FRONT = 128
PAD_FRONT = 112
EPS = 1e-6
DEPTH = 4
N_DEV = 8
N_IN = 9224
N_MAIN = 9216
N_SMALL = 128
SMALL_OFF = 4096
C_SBQ, C_SBK, C_SBV, C_SBZ = 0, 512, 1024, 1536
C_GQKV, C_GZ = 2048, 3584
C_HQ, C_HF, C_HI, C_HZ = 4096, 4608, 5120, 5632
C_MIX = 6144

ADAM_LR, ADAM_B1, ADAM_B2, ADAM_EPS, ADAM_WD, ADAM_STEP = 0.001, 0.9, 0.999, 1e-08, 0.01, 10

VMEM_LIMIT = 56 * 1024 * 1024
MESH = pl.DeviceIdType.MESH

NN = ((1,), (0,))
NT = ((1,), (1,))
TN = ((0,), (0,))


def _dot(a, b, dims=NN):
    return lax.dot_general(a.astype(bf16), b.astype(bf16), (dims, ((), ())), preferred_element_type=f32)


@jax.custom_vjp
def mm(a, b):
    return _dot(a, b, NN)


mm.defvjp(lambda a, b: (_dot(a, b, NN), (a, b)),
          lambda r, g: (_dot(g, r[1], NT), _dot(r[0], g, TN)))


@jax.custom_vjp
def mm_nt(a, b):
    return _dot(a, b, NT)


mm_nt.defvjp(lambda a, b: (_dot(a, b, NT), (a, b)),
             lambda r, g: (_dot(g, r[1], NN), _dot(g, r[0], TN)))


@jax.custom_vjp
def mm_tn(a, b):
    return _dot(a, b, TN)


mm_tn.defvjp(lambda a, b: (_dot(a, b, TN), (a, b)),
             lambda r, g: (_dot(r[1], g, NT), _dot(r[0], g, NN)))


def _split2(x):
    hi = x.astype(bf16)
    lo = (x - hi.astype(f32)).astype(bf16)
    return hi, lo


def _cdot(c, x, dims):
    hi, lo = _split2(x)
    return (lax.dot_general(c, hi, (dims, ((), ())), preferred_element_type=f32)
            + lax.dot_general(c, lo, (dims, ((), ())), preferred_element_type=f32))


@jax.custom_vjp
def cmm(c, x):
    return _cdot(c, x, NN)


cmm.defvjp(lambda c, x: (_cdot(c, x, NN), c),
           lambda c, g: (jnp.zeros_like(c), _cdot(c, g, TN)))


def _sigmoid(x):
    return jax.nn.sigmoid(x)


def _silu(x):
    return x * jax.nn.sigmoid(x)


def _softplus(x):
    return jnp.maximum(x, 0.0) + jnp.log(1.0 + jnp.exp(-jnp.abs(x)))


def _rms(x, w):
    return x * lax.rsqrt(jnp.mean(x * x, axis=-1, keepdims=True) + EPS) * w


def _cparams(sem=None):
    return pltpu.CompilerParams(dimension_semantics=sem, vmem_limit_bytes=VMEM_LIMIT)


TM_IN = 640
TN_IN = 1024


def inproj_fwd(h, nw, w_main, w_small):
    t = h.shape[0]

    def body(h_ref, nw_ref, w_ref, ws_ref, proj_ref, small_ref, xn_ref, xnt_ref):
        @pl.when(pl.program_id(1) == 0)
        def _():
            xn = _rms(h_ref[...], nw_ref[...])
            xn_ref[...] = xn.astype(bf16)
            xnt_ref[...] = jnp.transpose(xn).astype(bf16)
            small_ref[...] = _dot(xn, ws_ref[...])

        proj_ref[...] = jnp.dot(xn_ref[...], w_ref[...], preferred_element_type=f32)

    return pl.pallas_call(
        body, grid=(t // TM_IN, N_MAIN // TN_IN),
        in_specs=[pl.BlockSpec((TM_IN, D_MODEL), lambda i, j: (i, 0)),
                  pl.BlockSpec((1, D_MODEL), lambda i, j: (0, 0)),
                  pl.BlockSpec((D_MODEL, TN_IN), lambda i, j: (0, j)),
                  pl.BlockSpec((D_MODEL, N_SMALL), lambda i, j: (0, 0))],
        out_specs=[pl.BlockSpec((TM_IN, TN_IN), lambda i, j: (i, j)),
                   pl.BlockSpec((TM_IN, N_SMALL), lambda i, j: (i, 0)),
                   pl.BlockSpec((TM_IN, D_MODEL), lambda i, j: (i, 0)),
                   pl.BlockSpec((D_MODEL, TM_IN), lambda i, j: (0, i))],
        out_shape=[jax.ShapeDtypeStruct((t, N_MAIN), f32), jax.ShapeDtypeStruct((t, N_SMALL), f32),
                   jax.ShapeDtypeStruct((t, D_MODEL), bf16), jax.ShapeDtypeStruct((D_MODEL, t), bf16)],
        compiler_params=_cparams(("arbitrary", "arbitrary")), name="inproj_fwd")(h, nw, w_main, w_small)


def inproj_bwd_x(dproj, dsmall, wt_main, wt_small, h, nw, dh_out):
    t = h.shape[0]
    nk = N_MAIN // TN_IN

    def body(dp_ref, ds_ref, wt_ref, wts_ref, h_ref, nw_ref, dho_ref, dhi_ref, dnw_ref, acc):
        i, k = pl.program_id(0), pl.program_id(1)

        @pl.when(k == 0)
        def _():
            acc[...] = _dot(ds_ref[...], wts_ref[...])

        acc[...] += _dot(dp_ref[...], wt_ref[...])

        @pl.when(k == nk - 1)
        def _():
            x = h_ref[...]
            r = lax.rsqrt(jnp.mean(x * x, axis=-1, keepdims=True) + EPS)
            xh = x * r
            dxn = acc[...]
            dxh = dxn * nw_ref[...]
            dhi_ref[...] = dho_ref[...] + r * (dxh - xh * jnp.mean(dxh * xh, axis=-1, keepdims=True))
            part = jnp.sum(dxn * xh, axis=0, keepdims=True)

            @pl.when(i == 0)
            def _():
                dnw_ref[...] = part

            @pl.when(i > 0)
            def _():
                dnw_ref[...] += part

    return pl.pallas_call(
        body, grid=(t // TM_IN, nk),
        in_specs=[pl.BlockSpec((TM_IN, TN_IN), lambda i, k: (i, k)),
                  pl.BlockSpec((TM_IN, N_SMALL), lambda i, k: (i, 0)),
                  pl.BlockSpec((TN_IN, D_MODEL), lambda i, k: (k, 0)),
                  pl.BlockSpec((N_SMALL, D_MODEL), lambda i, k: (0, 0)),
                  pl.BlockSpec((TM_IN, D_MODEL), lambda i, k: (i, 0)),
                  pl.BlockSpec((1, D_MODEL), lambda i, k: (0, 0)),
                  pl.BlockSpec((TM_IN, D_MODEL), lambda i, k: (i, 0))],
        out_specs=[pl.BlockSpec((TM_IN, D_MODEL), lambda i, k: (i, 0)),
                   pl.BlockSpec((1, D_MODEL), lambda i, k: (0, 0))],
        out_shape=[jax.ShapeDtypeStruct((t, D_MODEL), f32), jax.ShapeDtypeStruct((1, D_MODEL), f32)],
        scratch_shapes=[pltpu.VMEM((TM_IN, D_MODEL), f32)],
        compiler_params=_cparams(("arbitrary", "arbitrary")), name="inproj_bwd_x")(
            dproj, dsmall, wt_main, wt_small, h, nw, dh_out)


def inproj_bwd_w(xnt, dproj, dsmall):
    t = xnt.shape[1]
    nt = t // TM_IN

    def body(xnt_ref, dp_ref, ds_ref, dw_ref, dws_ref):
        n, s = pl.program_id(0), pl.program_id(1)
        part = _dot(xnt_ref[...], dp_ref[...])

        @pl.when(s == 0)
        def _():
            dw_ref[...] = part

        @pl.when(s > 0)
        def _():
            dw_ref[...] += part

        @pl.when(n == 0)
        def _():
            ps = _dot(xnt_ref[...], ds_ref[...])

            @pl.when(s == 0)
            def _():
                dws_ref[...] = ps

            @pl.when(s > 0)
            def _():
                dws_ref[...] += ps

    return pl.pallas_call(
        body, grid=(N_MAIN // TN_IN, nt),
        in_specs=[pl.BlockSpec((D_MODEL, TM_IN), lambda n, s: (0, s)),
                  pl.BlockSpec((TM_IN, TN_IN), lambda n, s: (s, n)),
                  pl.BlockSpec((TM_IN, N_SMALL), lambda n, s: (s, 0))],
        out_specs=[pl.BlockSpec((D_MODEL, TN_IN), lambda n, s: (0, n)),
                   pl.BlockSpec((D_MODEL, N_SMALL), lambda n, s: (0, 0))],
        out_shape=[jax.ShapeDtypeStruct((D_MODEL, N_MAIN), f32), jax.ShapeDtypeStruct((D_MODEL, N_SMALL), f32)],
        compiler_params=_cparams(("arbitrary", "arbitrary")), name="inproj_bwd_w")(xnt, dproj, dsmall)


SB_SCALE = HEAD ** -0.5


SB_SUB = 4
SB_KS = SB_SUB * SB_BLOCK


def _sb_padded(t):
    return -(-t // SB_KS) * SB_KS


def _sb_prep(k_ref, v_ref, kw_ref, kn_scr, vb_scr, nb):
    def prep(b, c):
        rows = pl.ds(pl.multiple_of(b * SB_BLOCK, SB_BLOCK), SB_BLOCK)
        kn_scr[rows, :] = _rms(k_ref[rows, :], kw_ref[...]).astype(bf16)
        vb_scr[rows, :] = v_ref[rows, :].astype(bf16)
        return c

    lax.fori_loop(0, nb, prep, 0)
    pad = kn_scr.shape[0] - nb * SB_BLOCK
    if pad:
        kn_scr[nb * SB_BLOCK:, :] = jnp.zeros((pad, HEAD), bf16)
        vb_scr[nb * SB_BLOCK:, :] = jnp.zeros((pad, HEAD), bf16)


def _tri_ext(cmp):
    r = lax.broadcasted_iota(jnp.int32, (SB_BLOCK, 2 * SB_BLOCK), 0)
    c = lax.broadcasted_iota(jnp.int32, (SB_BLOCK, 2 * SB_BLOCK), 1)
    return jnp.where((c >= SB_BLOCK) | cmp(r, c), 1.0, 0.0).astype(bf16)


def _sb_suffix(x, carry, tri_ext):
    hi, lo = _split2(x)
    parts = [p[:, c * SB_BLOCK:(c + 1) * SB_BLOCK] for p in (hi, lo) for c in range(SB_SUB)]
    w = jnp.dot(jnp.concatenate(parts, axis=0), tri_ext, preferred_element_type=f32)
    outs = [None] * SB_SUB
    for c in reversed(range(SB_SUB)):
        blk = w[c * SB_BLOCK:(c + 1) * SB_BLOCK] + w[(SB_SUB + c) * SB_BLOCK:(SB_SUB + c + 1) * SB_BLOCK]
        outs[c] = carry + blk[:, :SB_BLOCK]
        carry = carry + blk[:, SB_BLOCK:]
    return jnp.concatenate(outs, axis=1), carry


def _sb_scores(qn, kt, i, jb, masked):
    z = lax.dot_general(qn, kt, (NT, ((), ())), preferred_element_type=f32) * SB_SCALE
    lsz = jnp.minimum(z, 0.0) - jnp.log(1.0 + jnp.exp(-jnp.abs(z)))
    lk = lsz - z
    mask = None
    if masked:
        t_idx = i * SB_BLOCK + lax.broadcasted_iota(jnp.int32, (SB_BLOCK, SB_KS), 0)
        s_idx = jb * SB_KS + lax.broadcasted_iota(jnp.int32, (SB_BLOCK, SB_KS), 1)
        mask = (s_idx < t_idx) & (s_idx >= PAD_FRONT)
        lk = jnp.where(mask, lk, 0.0)
    return mask, lsz, lk


def _sb_walk(i, tile, carry):
    n = i // SB_SUB + 1
    carry = tile(n - 1, carry, True)
    carry = lax.fori_loop(0, jnp.maximum(n - 2, 0), lambda jj, c: tile(n - 2 - jj, c, False), carry)
    return lax.cond(n >= 2, lambda c: tile(0, c, True), lambda c: c, carry)


def sb_fwd(proj, qw, kw):
    t = proj.shape[0]
    nb = t // SB_BLOCK

    def body(q_ref, k_ref, v_ref, qw_ref, kw_ref, o_ref, kn_scr, vb_scr):
        i = pl.program_id(1)

        @pl.when(i == 0)
        def _():
            _sb_prep(k_ref, v_ref, kw_ref, kn_scr, vb_scr, nb)

        qn = _rms(q_ref[...], qw_ref[...]).astype(bf16)
        u_ex = _tri_ext(lambda r, c: r > c)

        def tile(jb, carry, masked):
            acc, r_carry = carry
            rows = pl.ds(pl.multiple_of(jb * SB_KS, SB_KS), SB_KS)
            mask, lsz, lk = _sb_scores(qn, kn_scr[rows, :], i, jb, masked)
            passed, r_carry = _sb_suffix(lk, r_carry, u_ex)
            a = jnp.exp(lsz + passed)
            if masked:
                a = jnp.where(mask, a, 0.0)
            a_hi, a_lo = _split2(a)
            both = jnp.dot(jnp.concatenate([a_hi, a_lo], axis=0), vb_scr[rows, :], preferred_element_type=f32)
            return acc + (both[:SB_BLOCK] + both[SB_BLOCK:]), r_carry

        zeros = jnp.zeros((SB_BLOCK, HEAD), f32)
        acc, _ = _sb_walk(i, tile, (zeros, zeros))
        o_ref[...] = acc

    cb = C_SBK // HEAD
    vb = C_SBV // HEAD
    return pl.pallas_call(
        body, grid=(N_HEADS, nb),
        in_specs=[pl.BlockSpec((SB_BLOCK, HEAD), lambda h, i: (i, h)),
                  pl.BlockSpec((t, HEAD), lambda h, i: (0, cb + h)),
                  pl.BlockSpec((t, HEAD), lambda h, i: (0, vb + h)),
                  pl.BlockSpec((1, HEAD), lambda h, i: (0, 0)),
                  pl.BlockSpec((1, HEAD), lambda h, i: (0, 0))],
        out_specs=pl.BlockSpec((SB_BLOCK, HEAD), lambda h, i: (i, h)),
        out_shape=jax.ShapeDtypeStruct((t, BRANCH), f32),
        scratch_shapes=[pltpu.VMEM((_sb_padded(t), HEAD), bf16), pltpu.VMEM((_sb_padded(t), HEAD), bf16)],
        compiler_params=_cparams(("arbitrary", "arbitrary")), name="sb_fwd")(proj, proj, proj, qw, kw)


def sb_bwd(proj, qw, kw, o, do):
    t = proj.shape[0]
    nb = t // SB_BLOCK

    def body(q_ref, k_ref, v_ref, qw_ref, kw_ref, o_ref, do_ref, dq_ref, dk_ref, dv_ref, dqw_ref, dkw_ref,
             kn_scr, vb_scr, dk_acc, dv_acc):
        h, i = pl.program_id(0), pl.program_id(1)

        @pl.when(i == 0)
        def _():
            _sb_prep(k_ref, v_ref, kw_ref, kn_scr, vb_scr, nb)
            dk_acc[...] = jnp.zeros_like(dk_acc)
            dv_acc[...] = jnp.zeros_like(dv_acc)

        @pl.when((i == 0) & (h == 0))
        def _():
            dqw_ref[...] = jnp.zeros_like(dqw_ref)
            dkw_ref[...] = jnp.zeros_like(dkw_ref)

        q = q_ref[...]
        rq = lax.rsqrt(jnp.mean(q * q, axis=-1, keepdims=True) + EPS)
        qh = q * rq
        qn = (qh * qw_ref[...]).astype(bf16)
        do_f = do_ref[...]
        dob = do_f.astype(bf16)
        d_row = jnp.sum(dob.astype(f32) * o_ref[...], axis=-1, keepdims=True)
        u_ex = _tri_ext(lambda r, c: r > c)
        u_in = _tri_ext(lambda r, c: r >= c)

        def tile(jb, carry, masked):
            dq, r_carry, f_carry = carry
            rows = pl.ds(pl.multiple_of(jb * SB_KS, SB_KS), SB_KS)
            kt = kn_scr[rows, :]
            vt = vb_scr[rows, :]
            mask, lsz, lk = _sb_scores(qn, kt, i, jb, masked)
            passed, r_carry = _sb_suffix(lk, r_carry, u_ex)
            a = jnp.exp(lsz + passed)
            if masked:
                a = jnp.where(mask, a, 0.0)
            da = lax.dot_general(dob, vt, (NT, ((), ())), preferred_element_type=f32)
            e = a * da
            e_suf, f_carry = _sb_suffix(e, f_carry, u_in)
            sg = jnp.exp(lsz)
            dz = (e * (1.0 - sg) - (d_row - e_suf) * sg) * SB_SCALE
            if masked:
                dz = jnp.where(mask, dz, 0.0)
            dzb = dz.astype(bf16)
            dq = dq + jnp.dot(dzb, kt, preferred_element_type=f32)
            dk_acc[rows, :] += lax.dot_general(dzb, qn, (TN, ((), ())), preferred_element_type=f32)
            dv_acc[rows, :] += lax.dot_general(a.astype(bf16), dob, (TN, ((), ())), preferred_element_type=f32)
            return dq, r_carry, f_carry

        zeros = jnp.zeros((SB_BLOCK, HEAD), f32)
        dqn, _, _ = _sb_walk(i, tile, (zeros, zeros, zeros))
        gq = dqn * qw_ref[...]
        dq_ref[...] = rq * (gq - qh * jnp.mean(gq * qh, axis=-1, keepdims=True))
        dqw_ref[...] += jnp.sum(dqn * qh, axis=0, keepdims=True)

        @pl.when(i == nb - 1)
        def _():
            def fin(b, c):
                rows = pl.ds(pl.multiple_of(b * SB_BLOCK, SB_BLOCK), SB_BLOCK)
                kk = k_ref[rows, :]
                rk = lax.rsqrt(jnp.mean(kk * kk, axis=-1, keepdims=True) + EPS)
                kh = kk * rk
                dkn = dk_acc[rows, :]
                gk = dkn * kw_ref[...]
                dk_ref[rows, :] = rk * (gk - kh * jnp.mean(gk * kh, axis=-1, keepdims=True))
                dv_ref[rows, :] = dv_acc[rows, :]
                dkw_ref[...] += jnp.sum(dkn * kh, axis=0, keepdims=True)
                return c

            lax.fori_loop(0, nb, fin, 0)

    cb = C_SBK // HEAD
    vb = C_SBV // HEAD
    blk = pl.BlockSpec((SB_BLOCK, HEAD), lambda h, i: (i, h))
    full = pl.BlockSpec((t, HEAD), lambda h, i: (0, h))
    wsp = pl.BlockSpec((1, HEAD), lambda h, i: (0, 0))
    return pl.pallas_call(
        body, grid=(N_HEADS, nb),
        in_specs=[blk, pl.BlockSpec((t, HEAD), lambda h, i: (0, cb + h)),
                  pl.BlockSpec((t, HEAD), lambda h, i: (0, vb + h)), wsp, wsp, blk, blk],
        out_specs=[blk, full, full, wsp, wsp],
        out_shape=[jax.ShapeDtypeStruct((t, BRANCH), f32)] * 3 + [jax.ShapeDtypeStruct((1, HEAD), f32)] * 2,
        scratch_shapes=[pltpu.VMEM((_sb_padded(t), HEAD), bf16), pltpu.VMEM((_sb_padded(t), HEAD), bf16),
                        pltpu.VMEM((_sb_padded(t), HEAD), f32), pltpu.VMEM((_sb_padded(t), HEAD), f32)],
        compiler_params=_cparams(("arbitrary", "arbitrary")), name="sb_bwd")(proj, proj, proj, qw, kw, o, do)


TM_CONV = 640
CONV_W = 4
GQKV = 3 * BRANCH


def conv_fwd(proj, cw):
    t = proj.shape[0]
    halo_blocks = TM_CONV // 8
    cb = C_GQKV // GQKV
    del cb

    def body(x0_ref, x1_ref, x2_ref, p0_ref, p1_ref, p2_ref, cw_ref, y_ref):
        i = pl.program_id(0)
        for s, (x_ref, p_ref) in enumerate(((x0_ref, p0_ref), (x1_ref, p1_ref), (x2_ref, p2_ref))):
            prev = jnp.where(i > 0, p_ref[...], 0.0)
            xx = jnp.concatenate([prev, x_ref[...]], axis=0)
            cols = slice(s * BRANCH, (s + 1) * BRANCH)
            y = xx[8:] * cw_ref[CONV_W - 1:CONV_W, cols]
            for k in range(CONV_W - 1):
                y = y + pltpu.roll(xx, CONV_W - 1 - k, 0)[8:] * cw_ref[k:k + 1, cols]
            y_ref[:, cols] = y

    c0 = C_GQKV // BRANCH
    xs = [pl.BlockSpec((TM_CONV, BRANCH), functools.partial(lambda i, s: (i, c0 + s), s=s)) for s in range(3)]
    ps = [pl.BlockSpec((8, BRANCH), functools.partial(lambda i, s: (jnp.maximum(i * halo_blocks - 1, 0), c0 + s), s=s))
          for s in range(3)]
    return pl.pallas_call(
        body, grid=(t // TM_CONV,),
        in_specs=xs + ps + [pl.BlockSpec((CONV_W, GQKV), lambda i: (0, 0))],
        out_specs=pl.BlockSpec((TM_CONV, GQKV), lambda i: (i, 0)),
        out_shape=jax.ShapeDtypeStruct((t, GQKV), f32),
        compiler_params=_cparams(("arbitrary",)), name="conv_fwd")(proj, proj, proj, proj, proj, proj, cw)


def conv_bwd(proj, cw, dy):
    t = proj.shape[0]
    nt = t // TM_CONV
    halo_blocks = TM_CONV // 8

    def body(x0_ref, x1_ref, x2_ref, p0_ref, p1_ref, p2_ref, cw_ref, dy_ref, dyn_ref, dx_ref, dw_ref):
        i = pl.program_id(0)

        @pl.when(i == 0)
        def _():
            dw_ref[...] = jnp.zeros_like(dw_ref)

        nxt = jnp.where(i < nt - 1, dyn_ref[...], 0.0)
        dyy = jnp.concatenate([dy_ref[...], nxt], axis=0)
        n_rows = TM_CONV + 8
        dx = dyy[:TM_CONV] * cw_ref[CONV_W - 1:CONV_W, :]
        for k in range(CONV_W - 1):
            sh = CONV_W - 1 - k
            dx = dx + pltpu.roll(dyy, n_rows - sh, 0)[:TM_CONV] * cw_ref[k:k + 1, :]
        dx_ref[...] = dx
        dy_c = dy_ref[...]
        for s, (x_ref, p_ref) in enumerate(((x0_ref, p0_ref), (x1_ref, p1_ref), (x2_ref, p2_ref))):
            prev = jnp.where(i > 0, p_ref[...], 0.0)
            xx = jnp.concatenate([prev, x_ref[...]], axis=0)
            cols = slice(s * BRANCH, (s + 1) * BRANCH)
            for k in range(CONV_W):
                sh = CONV_W - 1 - k
                xs = xx[8:] if sh == 0 else pltpu.roll(xx, sh, 0)[8:]
                dw_ref[k:k + 1, cols] += jnp.sum(xs * dy_c[:, cols], axis=0, keepdims=True)

    c0 = C_GQKV // BRANCH
    xs = [pl.BlockSpec((TM_CONV, BRANCH), functools.partial(lambda i, s: (i, c0 + s), s=s)) for s in range(3)]
    ps = [pl.BlockSpec((8, BRANCH), functools.partial(lambda i, s: (jnp.maximum(i * halo_blocks - 1, 0), c0 + s), s=s))
          for s in range(3)]
    return pl.pallas_call(
        body, grid=(nt,),
        in_specs=xs + ps + [pl.BlockSpec((CONV_W, GQKV), lambda i: (0, 0)),
                            pl.BlockSpec((TM_CONV, GQKV), lambda i: (i, 0)),
                            pl.BlockSpec((8, GQKV), lambda i: (jnp.minimum((i + 1) * halo_blocks, nt * halo_blocks - 1), 0))],
        out_specs=[pl.BlockSpec((TM_CONV, GQKV), lambda i: (i, 0)), pl.BlockSpec((CONV_W, GQKV), lambda i: (0, 0))],
        out_shape=[jax.ShapeDtypeStruct((t, GQKV), f32), jax.ShapeDtypeStruct((CONV_W, GQKV), f32)],
        compiler_params=_cparams(("arbitrary",)), name="conv_bwd")(proj, proj, proj, proj, proj, proj, cw, dy, dy)


def _iota2(n, m, d):
    return lax.broadcasted_iota(jnp.int32, (n, m), d)


def _lane_pick(row_or_mat, idx):
    lanes = lax.broadcasted_iota(jnp.int32, row_or_mat.shape, row_or_mat.ndim - 1)
    return jnp.sum(jnp.where(lanes == idx, row_or_mat, 0.0), axis=-1, keepdims=True)


def _cumsum_consts():
    i = np.arange(CHUNK)
    incl = i[None, :] <= i[:, None]
    suf = i[None, :] > i[:, None]
    return np.concatenate([incl, suf], 0).astype(np.float32)


HG_LEVELS = (64, 32, 16, 8, 4, 2)


def _hgrn_consts():
    i = np.arange(CHUNK)
    rows = [i[None, :] <= i[:, None], i[None, :] > i[:, None]]
    for b in HG_LEVELS:
        ref = (i // b) * b + b // 2 - 1
        second = (i % b) >= b // 2
        rows.append((i[None, :] > ref[:, None]) & (i[None, :] <= i[:, None]) & second[:, None])
        rows.append((i[None, :] > i[:, None]) & (i[None, :] <= ref[:, None]) & (~second)[:, None])
    return np.concatenate(rows, 0).astype(np.float32)


def _gdn_chunk(state, ypre, small, gz, a_log, dt_b, on_w, c2, vm):
    r = _iota2(CHUNK, CHUNK, 0)
    c = _iota2(CHUNK, CHUNK, 1)
    causal = r >= c
    strict = r > c
    outs, new_states = [], []
    for h in range(N_HEADS):
        hs = slice(h * HEAD, (h + 1) * HEAD)
        q = _silu(ypre[:, h * HEAD:(h + 1) * HEAD])
        k = _silu(ypre[:, BRANCH + h * HEAD:BRANCH + (h + 1) * HEAD])
        v = _silu(ypre[:, 2 * BRANCH + h * HEAD:2 * BRANCH + (h + 1) * HEAD])
        q = q * lax.rsqrt(jnp.sum(q * q, axis=-1, keepdims=True) + EPS) * (HEAD ** -0.5)
        k = k * lax.rsqrt(jnp.sum(k * k, axis=-1, keepdims=True) + EPS)
        beta = _sigmoid(_lane_pick(small, h)) * vm
        g = -jnp.exp(_lane_pick(a_log, h)) * _softplus(_lane_pick(small, N_HEADS + h) + _lane_pick(dt_b, h))
        e2 = cmm(c2, jnp.broadcast_to(g, (CHUNK, HEAD)))
        gc = e2[:CHUNK]
        gsuf = e2[CHUNK:]
        g_row = jnp.transpose(gc)[:CHUNK, :]
        dec = jnp.where(causal, jnp.exp(jnp.minimum(gc[:, :CHUNK] - g_row, 0.0)), 0.0)
        kb = k * beta
        m = jnp.where(strict, mm_nt(kb, k) * dec, 0.0)
        x = jnp.concatenate([v * beta, kb * jnp.exp(gc)], axis=1)
        x = x - mm(m, x)
        p = m
        for _ in range(5):
            p = mm(p, p)
            x = x + mm(p, x)
        u, w = x[:, :HEAD], x[:, HEAD:]
        aqk = jnp.where(causal, mm_nt(q, k) * dec, 0.0)
        g_last = jnp.exp(jnp.sum(jnp.broadcast_to(g, (CHUNK, HEAD)), axis=0, keepdims=True))
        s_h = state[h]
        v_new = u - mm(w, s_h)
        o = mm(q * jnp.exp(gc), s_h) + mm(aqk, v_new)
        new_states.append(s_h * g_last + mm_tn(k * jnp.exp(gsuf), v_new))
        outs.append(_rms(o, on_w))
        del hs
    out = jnp.concatenate(outs, axis=1) * _silu(gz)
    return jnp.stack(new_states, axis=0), out


def _hgrn_chunk(state, hq, hf, hi, hz, lb, on_w, cm, vm):
    r = _iota2(CHUNK, CHUNK, 0)
    c = _iota2(CHUNK, CHUNK, 1)
    outs, new_states = [], []
    for h in range(N_HEADS):
        hs = slice(h * HEAD, (h + 1) * HEAD)
        lbh = lb[:, hs]
        q = _silu(hq[:, hs])
        f_pre = hf[:, hs]
        forget = lbh + (1.0 - lbh) * _sigmoid(f_pre)
        k = (1.0 - lbh) * _sigmoid(-f_pre)
        v = hi[:, hs] * vm
        g = jnp.log(forget)
        e = cmm(cm, g)
        gc, gsuf = e[:CHUNK], e[CHUNK:2 * CHUNK]
        st = state[h]
        o = mm_nt(q * jnp.exp(gc), st)
        a = jnp.where(r == c, jnp.sum(q * k, axis=-1, keepdims=True), 0.0)
        for li, b in enumerate(HG_LEVELS):
            d1 = e[(2 + 2 * li) * CHUNK:(3 + 2 * li) * CHUNK]
            d2 = e[(3 + 2 * li) * CHUNK:(4 + 2 * li) * CHUNK]
            sh = b.bit_length() - 1
            pair = ((r >> sh) == (c >> sh)) & ((r & (b - 1)) >= b // 2) & ((c & (b - 1)) < b // 2)
            a = a + jnp.where(pair, mm_nt(q * jnp.exp(d1), k * jnp.exp(d2)), 0.0)
        o = o + mm(a, v)
        g_end = jnp.sum(g, axis=0, keepdims=True)
        new_states.append(st * jnp.exp(g_end) + mm_tn(v, k * jnp.exp(gsuf)))
        outs.append(_rms(o, on_w))
    out = jnp.concatenate(outs, axis=1) * _silu(hz)
    return jnp.stack(new_states, axis=0), out


def _vmask(chunk_idx):
    rows = chunk_idx * CHUNK + lax.broadcasted_iota(jnp.int32, (CHUNK, 1), 0)
    return jnp.where(rows >= PAD_FRONT, 1.0, 0.0)


def _row(n):
    return pl.BlockSpec((1, n), lambda i: (0, 0))


def gdn_fwd(ypre, small, proj, a_log, dt_b, on_w):
    t = ypre.shape[0]
    nc = t // CHUNK
    c2 = jnp.asarray(_cumsum_consts(), bf16)

    def body(y_ref, s_ref, z_ref, al_ref, dt_ref, on_ref, c2_ref, o_ref, st_ref, state):
        i = pl.program_id(0)

        @pl.when(i == 0)
        def _():
            state[...] = jnp.zeros_like(state)

        s_in = state[...]
        st_ref[0] = s_in
        s_new, out = _gdn_chunk(s_in, y_ref[...], s_ref[...], z_ref[...], al_ref[...], dt_ref[...], on_ref[...],
                                c2_ref[...], _vmask(i))
        state[...] = s_new
        o_ref[...] = out

    return pl.pallas_call(
        body, grid=(nc,),
        in_specs=[pl.BlockSpec((CHUNK, GQKV), lambda i: (i, 0)), pl.BlockSpec((CHUNK, N_SMALL), lambda i: (i, 0)),
                  pl.BlockSpec((CHUNK, BRANCH), lambda i: (i, C_GZ // BRANCH)), _row(128), _row(128), _row(128),
                  pl.BlockSpec((2 * CHUNK, CHUNK), lambda i: (0, 0))],
        out_specs=[pl.BlockSpec((CHUNK, BRANCH), lambda i: (i, 0)),
                   pl.BlockSpec((1, N_HEADS, HEAD, HEAD), lambda i: (i, 0, 0, 0))],
        out_shape=[jax.ShapeDtypeStruct((t, BRANCH), f32), jax.ShapeDtypeStruct((nc, N_HEADS, HEAD, HEAD), f32)],
        scratch_shapes=[pltpu.VMEM((N_HEADS, HEAD, HEAD), f32)],
        compiler_params=_cparams(("arbitrary",)), name="gdn_fwd")(ypre, small, proj, a_log, dt_b, on_w, c2)


def gdn_bwd(ypre, small, proj, a_log, dt_b, on_w, states, d_out):
    t = ypre.shape[0]
    nc = t // CHUNK
    c2 = jnp.asarray(_cumsum_consts(), bf16)

    def body(y_ref, s_ref, z_ref, al_ref, dt_ref, on_ref, c2_ref, st_ref, do_ref,
             dy_ref, ds_ref, dz_ref, dal_ref, ddt_ref, don_ref, dstate):
        i = pl.program_id(0)

        @pl.when(i == 0)
        def _():
            dstate[...] = jnp.zeros_like(dstate)
            dal_ref[...] = jnp.zeros_like(dal_ref)
            ddt_ref[...] = jnp.zeros_like(ddt_ref)
            don_ref[...] = jnp.zeros_like(don_ref)

        vm = _vmask(nc - 1 - i)
        c2v = c2_ref[...]
        fn = lambda s, y, sm, z, al, dt, on: _gdn_chunk(s, y, sm, z, al, dt, on, c2v, vm)
        _, vjp = jax.vjp(fn, st_ref[0], y_ref[...], s_ref[...], z_ref[...], al_ref[...], dt_ref[...], on_ref[...])
        d_s, d_y, d_sm, d_z, d_al, d_dt, d_on = vjp((dstate[...], do_ref[...]))
        dstate[...] = d_s
        dy_ref[...] = d_y
        ds_ref[...] = d_sm
        dz_ref[...] = d_z
        dal_ref[...] += d_al
        ddt_ref[...] += d_dt
        don_ref[...] += d_on

    rev = lambda i: (nc - 1 - i, 0)
    return pl.pallas_call(
        body, grid=(nc,),
        in_specs=[pl.BlockSpec((CHUNK, GQKV), rev), pl.BlockSpec((CHUNK, N_SMALL), rev),
                  pl.BlockSpec((CHUNK, BRANCH), lambda i: (nc - 1 - i, C_GZ // BRANCH)), _row(128), _row(128), _row(128),
                  pl.BlockSpec((2 * CHUNK, CHUNK), lambda i: (0, 0)),
                  pl.BlockSpec((1, N_HEADS, HEAD, HEAD), lambda i: (nc - 1 - i, 0, 0, 0)),
                  pl.BlockSpec((CHUNK, BRANCH), rev)],
        out_specs=[pl.BlockSpec((CHUNK, GQKV), rev), pl.BlockSpec((CHUNK, N_SMALL), rev),
                   pl.BlockSpec((CHUNK, BRANCH), rev), _row(128), _row(128), _row(128)],
        out_shape=[jax.ShapeDtypeStruct((t, GQKV), f32), jax.ShapeDtypeStruct((t, N_SMALL), f32),
                   jax.ShapeDtypeStruct((t, BRANCH), f32)] + [jax.ShapeDtypeStruct((1, 128), f32)] * 3,
        scratch_shapes=[pltpu.VMEM((N_HEADS, HEAD, HEAD), f32)],
        compiler_params=_cparams(("arbitrary",)), name="gdn_bwd")(
            ypre, small, proj, a_log, dt_b, on_w, c2, states, d_out)


def hgrn_fwd(proj, lb, on_w):
    t = proj.shape[0]
    nc = t // CHUNK
    cm = jnp.asarray(_hgrn_consts(), bf16)
    ncm = cm.shape[0]

    def body(q_ref, f_ref, i_ref, z_ref, lb_ref, on_ref, cm_ref, o_ref, st_ref, state):
        i = pl.program_id(0)

        @pl.when(i == 0)
        def _():
            state[...] = jnp.zeros_like(state)

        s_in = state[...]
        st_ref[0] = s_in
        s_new, out = _hgrn_chunk(s_in, q_ref[...], f_ref[...], i_ref[...], z_ref[...], lb_ref[...], on_ref[...],
                                 cm_ref[...], _vmask(i))
        state[...] = s_new
        o_ref[...] = out

    sec = lambda off: pl.BlockSpec((CHUNK, BRANCH), functools.partial(lambda i, b: (i, b), b=off // BRANCH))
    return pl.pallas_call(
        body, grid=(nc,),
        in_specs=[sec(C_HQ), sec(C_HF), sec(C_HI), sec(C_HZ), _row(BRANCH), _row(128),
                  pl.BlockSpec((ncm, CHUNK), lambda i: (0, 0))],
        out_specs=[pl.BlockSpec((CHUNK, BRANCH), lambda i: (i, 0)),
                   pl.BlockSpec((1, N_HEADS, HEAD, HEAD), lambda i: (i, 0, 0, 0))],
        out_shape=[jax.ShapeDtypeStruct((t, BRANCH), f32), jax.ShapeDtypeStruct((nc, N_HEADS, HEAD, HEAD), f32)],
        scratch_shapes=[pltpu.VMEM((N_HEADS, HEAD, HEAD), f32)],
        compiler_params=_cparams(("arbitrary",)), name="hgrn_fwd")(proj, proj, proj, proj, lb, on_w, cm)


def hgrn_bwd(proj, lb, on_w, states, d_out):
    t = proj.shape[0]
    nc = t // CHUNK
    cm = jnp.asarray(_hgrn_consts(), bf16)
    ncm = cm.shape[0]

    def body(q_ref, f_ref, i_ref, z_ref, lb_ref, on_ref, cm_ref, st_ref, do_ref, dh_ref, dlb_ref, don_ref, dstate):
        i = pl.program_id(0)

        @pl.when(i == 0)
        def _():
            dstate[...] = jnp.zeros_like(dstate)
            dlb_ref[...] = jnp.zeros_like(dlb_ref)
            don_ref[...] = jnp.zeros_like(don_ref)

        vm = _vmask(nc - 1 - i)
        cmv = cm_ref[...]
        fn = lambda s, a, b, c, d, l, on: _hgrn_chunk(s, a, b, c, d, l, on, cmv, vm)
        _, vjp = jax.vjp(fn, st_ref[0], q_ref[...], f_ref[...], i_ref[...], z_ref[...], lb_ref[...], on_ref[...])
        d_s, d_q, d_f, d_i, d_z, d_lb, d_on = vjp((dstate[...], do_ref[...]))
        dstate[...] = d_s
        dh_ref[...] = jnp.concatenate([d_q, d_f, d_i, d_z], axis=1)
        dlb_ref[...] += d_lb
        don_ref[...] += d_on

    rev = lambda i: (nc - 1 - i, 0)
    sec = lambda off: pl.BlockSpec((CHUNK, BRANCH), functools.partial(lambda i, b: (nc - 1 - i, b), b=off // BRANCH))
    return pl.pallas_call(
        body, grid=(nc,),
        in_specs=[sec(C_HQ), sec(C_HF), sec(C_HI), sec(C_HZ), _row(BRANCH), _row(128),
                  pl.BlockSpec((ncm, CHUNK), lambda i: (0, 0)),
                  pl.BlockSpec((1, N_HEADS, HEAD, HEAD), lambda i: (nc - 1 - i, 0, 0, 0)),
                  pl.BlockSpec((CHUNK, BRANCH), rev)],
        out_specs=[pl.BlockSpec((CHUNK, 4 * BRANCH), rev), _row(BRANCH), _row(128)],
        out_shape=[jax.ShapeDtypeStruct((t, 4 * BRANCH), f32), jax.ShapeDtypeStruct((1, BRANCH), f32),
                   jax.ShapeDtypeStruct((1, 128), f32)],
        scratch_shapes=[pltpu.VMEM((N_HEADS, HEAD, HEAD), f32)],
        compiler_params=_cparams(("arbitrary",)), name="hgrn_bwd")(proj, proj, proj, proj, lb, on_w, cm, states, d_out)


TM_MG = 320


def _const_spec(shape):
    nd = len(shape)
    return pl.BlockSpec(shape, lambda i: (0,) * nd, pipeline_mode=pl.Buffered(1))


def merge_fwd(osb, proj, ogd, ohg, wb, wo, h):
    t = h.shape[0]

    def body(osb_ref, sbz_ref, ogd_ref, ohg_ref, mix_ref, wb_ref, wo_ref, h_ref, out_ref):
        a = osb_ref[...] * _silu(sbz_ref[...])
        y = (_sigmoid(mix_ref[:, 0:D_MODEL]) * _dot(a, wb_ref[0])
             + _sigmoid(mix_ref[:, D_MODEL:2 * D_MODEL]) * _dot(ogd_ref[...], wb_ref[1])
             + _sigmoid(mix_ref[:, 2 * D_MODEL:3 * D_MODEL]) * _dot(ohg_ref[...], wb_ref[2]))
        out_ref[...] = h_ref[...] + _dot(y, wo_ref[...])

    br = pl.BlockSpec((TM_MG, BRANCH), lambda i: (i, 0))
    return pl.pallas_call(
        body, grid=(t // TM_MG,),
        in_specs=[br, pl.BlockSpec((TM_MG, BRANCH), lambda i: (i, C_SBZ // BRANCH)), br, br,
                  pl.BlockSpec((TM_MG, 3 * D_MODEL), lambda i: (i, C_MIX // (3 * D_MODEL))),
                  _const_spec((3, BRANCH, D_MODEL)), _const_spec((D_MODEL, D_MODEL)),
                  pl.BlockSpec((TM_MG, D_MODEL), lambda i: (i, 0))],
        out_specs=pl.BlockSpec((TM_MG, D_MODEL), lambda i: (i, 0)),
        out_shape=jax.ShapeDtypeStruct((t, D_MODEL), f32),
        compiler_params=_cparams(("arbitrary",)), name="merge_fwd")(osb, proj, ogd, ohg, proj, wb, wo, h)


def merge_bwd(osb, proj, ogd, ohg, wb, wbt, wot, dh):
    t = dh.shape[0]

    def body(osb_ref, sbz_ref, ogd_ref, ohg_ref, mix_ref, wb_ref, wbt_ref, wot_ref, dh_ref,
             dosb_ref, dsbz_ref, dogd_ref, dohg_ref, dmix_ref, dwo_ref, dwb_ref):
        i = pl.program_id(0)

        @pl.when(i == 0)
        def _():
            dwo_ref[...] = jnp.zeros_like(dwo_ref)
            dwb_ref[...] = jnp.zeros_like(dwb_ref)

        osb = osb_ref[...]
        sbz = sbz_ref[...]
        sgz = _sigmoid(sbz)
        sz = sbz * sgz
        branch_in = (osb * sz, ogd_ref[...], ohg_ref[...])
        dh_v = dh_ref[...]
        dy = _dot(dh_v, wot_ref[...])
        y = jnp.zeros((TM_MG, D_MODEL), f32)
        d_in = []
        for b in range(3):
            p = _dot(branch_in[b], wb_ref[b])
            g = _sigmoid(mix_ref[:, b * D_MODEL:(b + 1) * D_MODEL])
            y = y + g * p
            dp = dy * g
            dmix_ref[:, b * D_MODEL:(b + 1) * D_MODEL] = dy * p * g * (1.0 - g)
            d_in.append(_dot(dp, wbt_ref[b]))
            dwb_ref[b] += _dot(branch_in[b], dp, TN)
        dwo_ref[...] += _dot(y, dh_v, TN)
        dosb_ref[...] = d_in[0] * sz
        dsbz_ref[...] = d_in[0] * osb * (sgz * (1.0 + sbz * (1.0 - sgz)))
        dogd_ref[...] = d_in[1]
        dohg_ref[...] = d_in[2]

    br = pl.BlockSpec((TM_MG, BRANCH), lambda i: (i, 0))
    return pl.pallas_call(
        body, grid=(t // TM_MG,),
        in_specs=[br, pl.BlockSpec((TM_MG, BRANCH), lambda i: (i, C_SBZ // BRANCH)), br, br,
                  pl.BlockSpec((TM_MG, 3 * D_MODEL), lambda i: (i, C_MIX // (3 * D_MODEL))),
                  _const_spec((3, BRANCH, D_MODEL)), _const_spec((3, D_MODEL, BRANCH)), _const_spec((D_MODEL, D_MODEL)),
                  pl.BlockSpec((TM_MG, D_MODEL), lambda i: (i, 0))],
        out_specs=[br, br, br, br, pl.BlockSpec((TM_MG, 3 * D_MODEL), lambda i: (i, 0)),
                   _const_spec((D_MODEL, D_MODEL)), _const_spec((3, BRANCH, D_MODEL))],
        out_shape=[jax.ShapeDtypeStruct((t, BRANCH), f32)] * 4 + [jax.ShapeDtypeStruct((t, 3 * D_MODEL), f32),
                   jax.ShapeDtypeStruct((D_MODEL, D_MODEL), f32), jax.ShapeDtypeStruct((3, BRANCH, D_MODEL), f32)],
        compiler_params=_cparams(("arbitrary",)), name="merge_bwd")(osb, proj, ogd, ohg, proj, wb, wbt, wot, dh)


def loss_head(h, target):
    t = h.shape[0]
    nb = t // SB_BLOCK

    def body(h_ref, t_ref, dh_ref, loss_ref):
        i = pl.program_id(0)

        @pl.when(i == 0)
        def _():
            loss_ref[...] = jnp.zeros_like(loss_ref)
            dh_ref[...] = jnp.zeros_like(dh_ref)

        @pl.when(i > 0)
        def _():
            err = h_ref[...] - t_ref[...]
            dh_ref[...] = err * (1.0 / D_MODEL)
            loss_ref[...] += jnp.broadcast_to(jnp.sum(err * err) * (0.5 / D_MODEL), loss_ref.shape)

    return pl.pallas_call(
        body, grid=(nb,),
        in_specs=[pl.BlockSpec((SB_BLOCK, D_MODEL), lambda i: (i, 0)),
                  pl.BlockSpec((SB_BLOCK, D_MODEL), lambda i: (jnp.maximum(i - 1, 0), 0))],
        out_specs=[pl.BlockSpec((SB_BLOCK, D_MODEL), lambda i: (i, 0)), pl.BlockSpec((1, 128), lambda i: (0, 0))],
        out_shape=[jax.ShapeDtypeStruct((t, D_MODEL), f32), jax.ShapeDtypeStruct((1, 128), f32)],
        compiler_params=_cparams(("arbitrary",)), name="loss_head")(h, target)


def adamw(parts, w, m, v, rows_per_step, name):
    r, c = w.shape
    tr = min(rows_per_step, r)

    def body(p_ref, w_ref, m_ref, v_ref, g_ref, d_ref, nm_ref, nv_ref):
        g = p_ref[0]
        for k in range(1, N_DEV):
            g = g + p_ref[k]
        m_new = ADAM_B1 * m_ref[...] + (1.0 - ADAM_B1) * g
        v_new = ADAM_B2 * v_ref[...] + (1.0 - ADAM_B2) * jnp.square(g)
        m_hat = m_new / (1.0 - ADAM_B1 ** ADAM_STEP)
        v_hat = v_new / (1.0 - ADAM_B2 ** ADAM_STEP)
        g_ref[...] = g
        d_ref[...] = -ADAM_LR * (m_hat / (jnp.sqrt(v_hat) + ADAM_EPS) + ADAM_WD * w_ref[...])
        nm_ref[...] = m_new
        nv_ref[...] = v_new

    blk = pl.BlockSpec((tr, c), lambda i: (i, 0))
    return pl.pallas_call(
        body, grid=(r // tr,),
        in_specs=[pl.BlockSpec((N_DEV, tr, c), lambda i: (0, i, 0)), blk, blk, blk],
        out_specs=[blk] * 4, out_shape=[jax.ShapeDtypeStruct((r, c), f32)] * 4,
        compiler_params=_cparams(("arbitrary",)), name=name)(parts, w, m, v)


def _mesh_pos():
    return lax.axis_index("x"), lax.axis_index("y"), lax.axis_index("c")


def _peer(pos, k):
    x, y, c = pos
    return (1 - x if k & 4 else x, 1 - y if k & 2 else y, 1 - c if k & 1 else c)


def _lin(pos):
    return 4 * pos[0] + 2 * pos[1] + pos[2]


def exchange(srcs, scatter, name):
    n = len(srcs)
    shapes = [s.shape[1:] if sc else s.shape for s, sc in zip(srcs, scatter)]

    def body(*refs):
        src_refs, dst_refs = refs[:n], refs[n:2 * n]
        send_sems, recv_sems, local_sems = refs[2 * n:]
        me = _mesh_pos()
        me_lin = _lin(me)
        sends, recvs, locals_ = [], [], []
        for t in range(n):
            own = src_refs[t].at[me_lin] if scatter[t] else src_refs[t]
            locals_.append(pltpu.make_async_copy(own, dst_refs[t].at[me_lin], local_sems.at[t]))
            for k in range(1, N_DEV):
                peer = _peer(me, k)
                src = src_refs[t].at[_lin(peer)] if scatter[t] else src_refs[t]
                sends.append(pltpu.make_async_remote_copy(
                    src_ref=src, dst_ref=dst_refs[t].at[me_lin], send_sem=send_sems.at[t, k - 1],
                    recv_sem=recv_sems.at[t, k - 1], device_id=peer, device_id_type=MESH))
                recvs.append(pltpu.make_async_remote_copy(
                    src_ref=src, dst_ref=dst_refs[t].at[_lin(peer)], send_sem=send_sems.at[t, k - 1],
                    recv_sem=recv_sems.at[t, k - 1], device_id=peer, device_id_type=MESH))
        for cp in locals_ + sends:
            cp.start()
        for cp in sends:
            cp.wait_send()
        for cp in recvs:
            cp.wait_recv()
        for cp in locals_:
            cp.wait()

    any_spec = pl.BlockSpec(memory_space=pl.ANY)
    return pl.pallas_call(
        body, in_specs=[any_spec] * n, out_specs=[any_spec] * n,
        out_shape=[jax.ShapeDtypeStruct((N_DEV,) + tuple(sh), s.dtype) for sh, s in zip(shapes, srcs)],
        scratch_shapes=[pltpu.SemaphoreType.DMA((n, N_DEV - 1)), pltpu.SemaphoreType.DMA((n, N_DEV - 1)),
                        pltpu.SemaphoreType.DMA((n,))],
        compiler_params=pltpu.CompilerParams(has_side_effects=True), name=name)(*srcs)


PACK_ROWS = 104


def _pad_rows(a, rows):
    return jnp.pad(a, ((0, rows - a.shape[0]), (0, 0)))


def _pad_lanes(a):
    return jnp.pad(a, ((0, 0), (0, 128 - a.shape[1])))


def _pack(norm_w, sbq, sbk, alog, dtb, gon, lbl, hon, loss_row):
    parts = [norm_w.reshape(32, 128), _pad_rows(sbq, 8), _pad_rows(sbk, 8), _pad_rows(_pad_lanes(alog), 8),
             _pad_rows(_pad_lanes(dtb), 8), _pad_rows(gon, 8), lbl.reshape(16, 128), _pad_rows(hon, 8),
             _pad_rows(loss_row, 8)]
    return jnp.concatenate(parts, axis=0)


def _unpack(p):
    return dict(norm_w=p[0:32].reshape(DEPTH, D_MODEL), sb_q_norm=p[32:36], sb_k_norm=p[40:44],
                gdn_a_log=p[48:52, :N_HEADS], gdn_dt_bias=p[56:60, :N_HEADS], gdn_out_norm=p[64:68],
                hgrn_lb_logits=p[72:88].reshape(DEPTH, BRANCH), hgrn_out_norm=p[88:92], loss=p[96, 0])


def _lower_bounds(logits):
    p = jax.nn.softmax(logits, axis=0)
    return jnp.cumsum(p, axis=0) - p[0:1]


def _unshard_cols(g):
    nd = g.ndim
    g = jnp.moveaxis(g, 0, nd - 2)
    return g.reshape(g.shape[:-2] + (N_DEV * g.shape[-1],))


def _shard_cols(a):
    n = a.shape[-1] // N_DEV
    return jnp.moveaxis(a.reshape(a.shape[:-1] + (N_DEV, n)), -2, 0)


def kernel(x, meta_tokens, norm_w, w_in, sb_q_norm, sb_k_norm, gdn_conv_w, gdn_a_log, gdn_dt_bias, gdn_out_norm, hgrn_lb_logits, hgrn_out_norm, w_branch, w_out, loss_target, m_meta_tokens, m_norm_w, m_w_in, m_sb_q_norm, m_sb_k_norm, m_gdn_conv_w, m_gdn_a_log, m_gdn_dt_bias, m_gdn_out_norm, m_hgrn_lb_logits, m_hgrn_out_norm, m_w_branch, m_w_out, v_meta_tokens, v_norm_w, v_w_in, v_sb_q_norm, v_sb_k_norm, v_gdn_conv_w, v_gdn_a_log, v_gdn_dt_bias, v_gdn_out_norm, v_hgrn_lb_logits, v_hgrn_out_norm, v_w_branch, v_w_out):
    g_win, g_wbr, g_wout, g_meta, g_conv = exchange(
        [w_in.astype(bf16), w_branch.astype(bf16), w_out.astype(bf16), meta_tokens, gdn_conv_w],
        [False] * 5, "gather_weights")
    w_full = _unshard_cols(g_win)
    w_main = jnp.concatenate([w_full[..., :SMALL_OFF], w_full[..., SMALL_OFF + 8:]], axis=-1)
    w_small = jnp.pad(w_full[..., SMALL_OFF:SMALL_OFF + 8], ((0, 0), (0, 0), (0, N_SMALL - 8)))
    wt_main = jnp.swapaxes(w_main, 1, 2)
    wt_small = jnp.swapaxes(w_small, 1, 2)
    wbr = _unshard_cols(g_wbr)
    wbr_t = jnp.swapaxes(wbr, 2, 3)
    wout = jnp.moveaxis(g_wout, 0, 1).reshape(DEPTH, D_MODEL, D_MODEL)
    wout_t = jnp.swapaxes(wout, 1, 2)
    meta = _unshard_cols(g_meta)
    conv_w = _unshard_cols(g_conv)
    lbounds, lb_vjp = jax.vjp(_lower_bounds, hgrn_lb_logits)

    h = jnp.concatenate([jnp.zeros((PAD_FRONT, D_MODEL), f32), meta, x[0]], axis=0)
    row = lambda a: a.reshape(1, -1)
    saved = []
    for l in range(DEPTH):
        proj, small, _, xnt = inproj_fwd(h, row(norm_w[l]), w_main[l], w_small[l])
        osb = sb_fwd(proj, row(sb_q_norm[l]), row(sb_k_norm[l]))
        ypre = conv_fwd(proj, conv_w[l])
        al, dtb = _pad_lanes(row(gdn_a_log[l])), _pad_lanes(row(gdn_dt_bias[l]))
        ogd, gst = gdn_fwd(ypre, small, proj, al, dtb, row(gdn_out_norm[l]))
        ohg, hst = hgrn_fwd(proj, row(lbounds[l]), row(hgrn_out_norm[l]))
        h_next = merge_fwd(osb, proj, ogd, ohg, wbr[l], wout[l], h)
        saved.append((h, proj, small, xnt, osb, ypre, ogd, gst, ohg, hst, al, dtb))
        h = h_next

    dh, loss_row = loss_head(h, loss_target[0])

    gw_main, gw_small, gw_br, gw_out, g_conv_w = [None] * DEPTH, [None] * DEPTH, [None] * DEPTH, [None] * DEPTH, [None] * DEPTH
    g_norm, g_sbq, g_sbk, g_al, g_dt, g_gon, g_lb, g_hon = ([None] * DEPTH for _ in range(8))
    for l in reversed(range(DEPTH)):
        h_l, proj, small, xnt, osb, ypre, ogd, gst, ohg, hst, al, dtb = saved[l]
        d_osb, d_sbz, d_ogd, d_ohg, d_mix, gw_out[l], gw_br[l] = merge_bwd(osb, proj, ogd, ohg, wbr[l], wbr_t[l], wout_t[l], dh)
        d_hg, g_lb[l], g_hon[l] = hgrn_bwd(proj, row(lbounds[l]), row(hgrn_out_norm[l]), hst, d_ohg)
        d_ypre, d_small, d_gz, g_al[l], g_dt[l], g_gon[l] = gdn_bwd(ypre, small, proj, al, dtb, row(gdn_out_norm[l]), gst, d_ogd)
        d_gqkv, g_conv_w[l] = conv_bwd(proj, conv_w[l], d_ypre)
        d_q, d_k, d_v, g_sbq[l], g_sbk[l] = sb_bwd(proj, row(sb_q_norm[l]), row(sb_k_norm[l]), osb, d_osb)
        dproj = jnp.concatenate([d_q, d_k, d_v, d_sbz, d_gqkv, d_gz, d_hg, d_mix], axis=1)
        gw_main[l], gw_small[l] = inproj_bwd_w(xnt, dproj, d_small)
        dh, g_norm[l] = inproj_bwd_x(dproj, d_small, wt_main[l], wt_small[l], h_l, row(norm_w[l]), dh)

    gw_main, gw_small = jnp.stack(gw_main), jnp.stack(gw_small)
    gw_in = jnp.concatenate([gw_main[..., :SMALL_OFF], gw_small[..., :8], gw_main[..., SMALL_OFF:]], axis=-1)
    d_lbl = lb_vjp(jnp.concatenate(g_lb, axis=0))[0]
    cat = lambda rows: jnp.concatenate(rows, axis=0)
    pack = _pack(cat(g_norm), cat(g_sbq), cat(g_sbk), cat(g_al)[:, :N_HEADS], cat(g_dt)[:, :N_HEADS], cat(g_gon),
                 d_lbl, cat(g_hon), loss_row)
    g_meta_full = dh[PAD_FRONT:FRONT]
    r_win, r_wbr, r_wout, r_meta, r_conv, r_pack = exchange(
        [_shard_cols(gw_in), _shard_cols(jnp.stack(gw_br)), jnp.swapaxes(jnp.stack(gw_out).reshape(DEPTH, N_DEV, HEAD, D_MODEL), 0, 1),
         _shard_cols(g_meta_full), _shard_cols(jnp.stack(g_conv_w)), pack],
        [True, True, True, True, True, False], "exchange_grads")

    def upd(parts, w, m, v, rows, name):
        shp = w.shape
        two = (-1, shp[-1])
        outs = adamw(parts.reshape((N_DEV,) + w.reshape(two).shape), w.reshape(two), m.reshape(two), v.reshape(two), rows, name)
        return [o.reshape(shp) for o in outs]

    res = {}
    res["w_in"] = upd(r_win, w_in, m_w_in, v_w_in, 256, "adamw_w_in")
    res["w_branch"] = upd(r_wbr, w_branch, m_w_branch, v_w_branch, 1024, "adamw_w_branch")
    res["w_out"] = upd(r_wout, w_out, m_w_out, v_w_out, 256, "adamw_w_out")
    res["meta_tokens"] = upd(r_meta, meta_tokens, m_meta_tokens, v_meta_tokens, 16, "adamw_meta")
    res["gdn_conv_w"] = upd(r_conv, gdn_conv_w, m_gdn_conv_w, v_gdn_conv_w, 16, "adamw_conv")
    zero_row = jnp.zeros((1, 128), f32)
    w_pack = _pack(norm_w, sb_q_norm, sb_k_norm, gdn_a_log, gdn_dt_bias, gdn_out_norm, hgrn_lb_logits, hgrn_out_norm, zero_row)
    m_pack = _pack(m_norm_w, m_sb_q_norm, m_sb_k_norm, m_gdn_a_log, m_gdn_dt_bias, m_gdn_out_norm, m_hgrn_lb_logits, m_hgrn_out_norm, zero_row)
    v_pack = _pack(v_norm_w, v_sb_q_norm, v_sb_k_norm, v_gdn_a_log, v_gdn_dt_bias, v_gdn_out_norm, v_hgrn_lb_logits, v_hgrn_out_norm, zero_row)
    packed = [_unpack(o) for o in adamw(r_pack, w_pack, m_pack, v_pack, PACK_ROWS, "adamw_replicated")]
    for name in ("norm_w", "sb_q_norm", "sb_k_norm", "gdn_a_log", "gdn_dt_bias", "gdn_out_norm", "hgrn_lb_logits", "hgrn_out_norm"):
        res[name] = [p[name] for p in packed]
    loss = packed[0]["loss"]
    grad_x = dh[FRONT:][None]

    order = ["meta_tokens", "norm_w", "w_in", "sb_q_norm", "sb_k_norm", "gdn_conv_w", "gdn_a_log", "gdn_dt_bias",
             "gdn_out_norm", "hgrn_lb_logits", "hgrn_out_norm", "w_branch", "w_out"]
    return (loss, grad_x, *[res[n][0] for n in order], *[res[n][1] for n in order],
            *[res[n][2] for n in order], *[res[n][3] for n in order])
```

```python
import functools

import numpy as np
import jax
import jax.numpy as jnp
from jax import lax
from jax.experimental import pallas as pl
from jax.experimental.pallas import tpu as pltpu

f32 = jnp.float32
bf16 = jnp.bfloat16

D_MODEL = 1024
BRANCH = 512
HEAD = 128
N_HEADS = 4
CHUNK = 64
SB_BLOCK = 128
N_META = 16
FRONT = 128
PAD_FRONT = 112
EPS = 1e-6
DEPTH = 4
N_DEV = 8
N_IN = 9224
N_MAIN = 9216
N_SMALL = 128
SMALL_OFF = 4096
C_SBQ, C_SBK, C_SBV, C_SBZ = 0, 512, 1024, 1536
C_GQKV, C_GZ = 2048, 3584
C_HQ, C_HF, C_HI, C_HZ = 4096, 4608, 5120, 5632
C_MIX = 6144

ADAM_LR, ADAM_B1, ADAM_B2, ADAM_EPS, ADAM_WD, ADAM_STEP = 0.001, 0.9, 0.999, 1e-08, 0.01, 10

VMEM_LIMIT = 56 * 1024 * 1024
MESH = pl.DeviceIdType.MESH

NN = ((1,), (0,))
NT = ((1,), (1,))
TN = ((0,), (0,))


def _dot(a, b, dims=NN):
    return lax.dot_general(a.astype(bf16), b.astype(bf16), (dims, ((), ())), preferred_element_type=f32)


@jax.custom_vjp
def mm(a, b):
    return _dot(a, b, NN)


mm.defvjp(lambda a, b: (_dot(a, b, NN), (a, b)),
          lambda r, g: (_dot(g, r[1], NT), _dot(r[0], g, TN)))


@jax.custom_vjp
def mm_nt(a, b):
    return _dot(a, b, NT)


mm_nt.defvjp(lambda a, b: (_dot(a, b, NT), (a, b)),
             lambda r, g: (_dot(g, r[1], NN), _dot(g, r[0], TN)))


@jax.custom_vjp
def mm_tn(a, b):
    return _dot(a, b, TN)


mm_tn.defvjp(lambda a, b: (_dot(a, b, TN), (a, b)),
             lambda r, g: (_dot(r[1], g, NT), _dot(r[0], g, NN)))


def _split2(x):
    hi = x.astype(bf16)
    lo = (x - hi.astype(f32)).astype(bf16)
    return hi, lo


def _cdot(c, x, dims):
    hi, lo = _split2(x)
    return (lax.dot_general(c, hi, (dims, ((), ())), preferred_element_type=f32)
            + lax.dot_general(c, lo, (dims, ((), ())), preferred_element_type=f32))


@jax.custom_vjp
def cmm(c, x):
    return _cdot(c, x, NN)


cmm.defvjp(lambda c, x: (_cdot(c, x, NN), c),
           lambda c, g: (jnp.zeros_like(c), _cdot(c, g, TN)))


def _sigmoid(x):
    return jax.nn.sigmoid(x)


def _silu(x):
    return x * jax.nn.sigmoid(x)


def _softplus(x):
    return jnp.maximum(x, 0.0) + jnp.log(1.0 + jnp.exp(-jnp.abs(x)))


def _rms(x, w):
    return x * lax.rsqrt(jnp.mean(x * x, axis=-1, keepdims=True) + EPS) * w


def _cparams(sem=None):
    return pltpu.CompilerParams(dimension_semantics=sem, vmem_limit_bytes=VMEM_LIMIT)


TM_IN = 640
TN_IN = 1024


def inproj_fwd(h, nw, w_main, w_small):
    t = h.shape[0]

    def body(h_ref, nw_ref, w_ref, ws_ref, proj_ref, small_ref, xn_ref, xnt_ref):
        @pl.when(pl.program_id(1) == 0)
        def _():
            xn = _rms(h_ref[...], nw_ref[...])
            xn_ref[...] = xn.astype(bf16)
            xnt_ref[...] = jnp.transpose(xn).astype(bf16)
            small_ref[...] = _dot(xn, ws_ref[...])

        proj_ref[...] = jnp.dot(xn_ref[...], w_ref[...], preferred_element_type=f32)

    return pl.pallas_call(
        body, grid=(t // TM_IN, N_MAIN // TN_IN),
        in_specs=[pl.BlockSpec((TM_IN, D_MODEL), lambda i, j: (i, 0)),
                  pl.BlockSpec((1, D_MODEL), lambda i, j: (0, 0)),
                  pl.BlockSpec((D_MODEL, TN_IN), lambda i, j: (0, j)),
                  pl.BlockSpec((D_MODEL, N_SMALL), lambda i, j: (0, 0))],
        out_specs=[pl.BlockSpec((TM_IN, TN_IN), lambda i, j: (i, j)),
                   pl.BlockSpec((TM_IN, N_SMALL), lambda i, j: (i, 0)),
                   pl.BlockSpec((TM_IN, D_MODEL), lambda i, j: (i, 0)),
                   pl.BlockSpec((D_MODEL, TM_IN), lambda i, j: (0, i))],
        out_shape=[jax.ShapeDtypeStruct((t, N_MAIN), f32), jax.ShapeDtypeStruct((t, N_SMALL), f32),
                   jax.ShapeDtypeStruct((t, D_MODEL), bf16), jax.ShapeDtypeStruct((D_MODEL, t), bf16)],
        compiler_params=_cparams(("arbitrary", "arbitrary")), name="inproj_fwd")(h, nw, w_main, w_small)


def inproj_bwd_x(dproj, dsmall, wt_main, wt_small, h, nw, dh_out):
    t = h.shape[0]
    nk = N_MAIN // TN_IN

    def body(dp_ref, ds_ref, wt_ref, wts_ref, h_ref, nw_ref, dho_ref, dhi_ref, dnw_ref, acc):
        i, k = pl.program_id(0), pl.program_id(1)

        @pl.when(k == 0)
        def _():
            acc[...] = _dot(ds_ref[...], wts_ref[...])

        acc[...] += _dot(dp_ref[...], wt_ref[...])

        @pl.when(k == nk - 1)
        def _():
            x = h_ref[...]
            r = lax.rsqrt(jnp.mean(x * x, axis=-1, keepdims=True) + EPS)
            xh = x * r
            dxn = acc[...]
            dxh = dxn * nw_ref[...]
            dhi_ref[...] = dho_ref[...] + r * (dxh - xh * jnp.mean(dxh * xh, axis=-1, keepdims=True))
            part = jnp.sum(dxn * xh, axis=0, keepdims=True)

            @pl.when(i == 0)
            def _():
                dnw_ref[...] = part

            @pl.when(i > 0)
            def _():
                dnw_ref[...] += part

    return pl.pallas_call(
        body, grid=(t // TM_IN, nk),
        in_specs=[pl.BlockSpec((TM_IN, TN_IN), lambda i, k: (i, k)),
                  pl.BlockSpec((TM_IN, N_SMALL), lambda i, k: (i, 0)),
                  pl.BlockSpec((TN_IN, D_MODEL), lambda i, k: (k, 0)),
                  pl.BlockSpec((N_SMALL, D_MODEL), lambda i, k: (0, 0)),
                  pl.BlockSpec((TM_IN, D_MODEL), lambda i, k: (i, 0)),
                  pl.BlockSpec((1, D_MODEL), lambda i, k: (0, 0)),
                  pl.BlockSpec((TM_IN, D_MODEL), lambda i, k: (i, 0))],
        out_specs=[pl.BlockSpec((TM_IN, D_MODEL), lambda i, k: (i, 0)),
                   pl.BlockSpec((1, D_MODEL), lambda i, k: (0, 0))],
        out_shape=[jax.ShapeDtypeStruct((t, D_MODEL), f32), jax.ShapeDtypeStruct((1, D_MODEL), f32)],
        scratch_shapes=[pltpu.VMEM((TM_IN, D_MODEL), f32)],
        compiler_params=_cparams(("arbitrary", "arbitrary")), name="inproj_bwd_x")(
            dproj, dsmall, wt_main, wt_small, h, nw, dh_out)


def inproj_bwd_w(xnt, dproj, dsmall):
    t = xnt.shape[1]
    nt = t // TM_IN

    def body(xnt_ref, dp_ref, ds_ref, dw_ref, dws_ref):
        n, s = pl.program_id(0), pl.program_id(1)
        part = _dot(xnt_ref[...], dp_ref[...])

        @pl.when(s == 0)
        def _():
            dw_ref[...] = part

        @pl.when(s > 0)
        def _():
            dw_ref[...] += part

        @pl.when(n == 0)
        def _():
            ps = _dot(xnt_ref[...], ds_ref[...])

            @pl.when(s == 0)
            def _():
                dws_ref[...] = ps

            @pl.when(s > 0)
            def _():
                dws_ref[...] += ps

    return pl.pallas_call(
        body, grid=(N_MAIN // TN_IN, nt),
        in_specs=[pl.BlockSpec((D_MODEL, TM_IN), lambda n, s: (0, s)),
                  pl.BlockSpec((TM_IN, TN_IN), lambda n, s: (s, n)),
                  pl.BlockSpec((TM_IN, N_SMALL), lambda n, s: (s, 0))],
        out_specs=[pl.BlockSpec((D_MODEL, TN_IN), lambda n, s: (0, n)),
                   pl.BlockSpec((D_MODEL, N_SMALL), lambda n, s: (0, 0))],
        out_shape=[jax.ShapeDtypeStruct((D_MODEL, N_MAIN), f32), jax.ShapeDtypeStruct((D_MODEL, N_SMALL), f32)],
        compiler_params=_cparams(("arbitrary", "arbitrary")), name="inproj_bwd_w")(xnt, dproj, dsmall)


SB_SCALE = HEAD ** -0.5


SB_SUB = 8
SB_KS = SB_SUB * SB_BLOCK


def _sb_padded(t):
    return -(-t // SB_KS) * SB_KS


def _sb_prep(k_ref, v_ref, kw_ref, kn_scr, vb_scr, nb):
    def prep(b, c):
        rows = pl.ds(pl.multiple_of(b * SB_BLOCK, SB_BLOCK), SB_BLOCK)
        kn_scr[rows, :] = _rms(k_ref[rows, :], kw_ref[...]).astype(bf16)
        vb_scr[rows, :] = v_ref[rows, :].astype(bf16)
        return c

    lax.fori_loop(0, nb, prep, 0)
    pad = kn_scr.shape[0] - nb * SB_BLOCK
    if pad:
        kn_scr[nb * SB_BLOCK:, :] = jnp.zeros((pad, HEAD), bf16)
        vb_scr[nb * SB_BLOCK:, :] = jnp.zeros((pad, HEAD), bf16)


def _tri_ext(cmp):
    r = lax.broadcasted_iota(jnp.int32, (SB_BLOCK, 2 * SB_BLOCK), 0)
    c = lax.broadcasted_iota(jnp.int32, (SB_BLOCK, 2 * SB_BLOCK), 1)
    return jnp.where((c >= SB_BLOCK) | cmp(r, c), 1.0, 0.0).astype(bf16)


def _sb_suffix(x, carry, tri_ext):
    hi, lo = _split2(x)
    parts = [p[:, c * SB_BLOCK:(c + 1) * SB_BLOCK] for p in (hi, lo) for c in range(SB_SUB)]
    w = jnp.dot(jnp.concatenate(parts, axis=0), tri_ext, preferred_element_type=f32)
    outs = [None] * SB_SUB
    for c in reversed(range(SB_SUB)):
        blk = w[c * SB_BLOCK:(c + 1) * SB_BLOCK] + w[(SB_SUB + c) * SB_BLOCK:(SB_SUB + c + 1) * SB_BLOCK]
        outs[c] = carry + blk[:, :SB_BLOCK]
        carry = carry + blk[:, SB_BLOCK:]
    return jnp.concatenate(outs, axis=1), carry


def _sb_scores(qn, kt, i, jb, masked):
    z = lax.dot_general(qn, kt, (NT, ((), ())), preferred_element_type=f32) * SB_SCALE
    lsz = jnp.minimum(z, 0.0) - jnp.log(1.0 + jnp.exp(-jnp.abs(z)))
    lk = lsz - z
    mask = None
    if masked:
        t_idx = i * SB_BLOCK + lax.broadcasted_iota(jnp.int32, (SB_BLOCK, SB_KS), 0)
        s_idx = jb * SB_KS + lax.broadcasted_iota(jnp.int32, (SB_BLOCK, SB_KS), 1)
        mask = (s_idx < t_idx) & (s_idx >= PAD_FRONT)
        lk = jnp.where(mask, lk, 0.0)
    return mask, lsz, lk


def _sb_walk(i, tile, carry):
    n = i // SB_SUB + 1
    carry = tile(n - 1, carry, True)
    carry = lax.fori_loop(0, jnp.maximum(n - 2, 0), lambda jj, c: tile(n - 2 - jj, c, False), carry)
    return lax.cond(n >= 2, lambda c: tile(0, c, True), lambda c: c, carry)


def sb_fwd(proj, qw, kw):
    t = proj.shape[0]
    nb = t // SB_BLOCK

    def body(q_ref, k_ref, v_ref, qw_ref, kw_ref, o_ref, kn_scr, vb_scr):
        i = pl.program_id(1)

        @pl.when(i == 0)
        def _():
            _sb_prep(k_ref, v_ref, kw_ref, kn_scr, vb_scr, nb)

        qn = _rms(q_ref[...], qw_ref[...]).astype(bf16)
        u_ex = _tri_ext(lambda r, c: r > c)

        def tile(jb, carry, masked):
            acc, r_carry = carry
            rows = pl.ds(pl.multiple_of(jb * SB_KS, SB_KS), SB_KS)
            mask, lsz, lk = _sb_scores(qn, kn_scr[rows, :], i, jb, masked)
            passed, r_carry = _sb_suffix(lk, r_carry, u_ex)
            a = jnp.exp(lsz + passed)
            if masked:
                a = jnp.where(mask, a, 0.0)
            a_hi, a_lo = _split2(a)
            both = jnp.dot(jnp.concatenate([a_hi, a_lo], axis=0), vb_scr[rows, :], preferred_element_type=f32)
            return acc + (both[:SB_BLOCK] + both[SB_BLOCK:]), r_carry

        zeros = jnp.zeros((SB_BLOCK, HEAD), f32)
        acc, _ = _sb_walk(i, tile, (zeros, zeros))
        o_ref[...] = acc

    cb = C_SBK // HEAD
    vb = C_SBV // HEAD
    return pl.pallas_call(
        body, grid=(N_HEADS, nb),
        in_specs=[pl.BlockSpec((SB_BLOCK, HEAD), lambda h, i: (i, h)),
                  pl.BlockSpec((t, HEAD), lambda h, i: (0, cb + h)),
                  pl.BlockSpec((t, HEAD), lambda h, i: (0, vb + h)),
                  pl.BlockSpec((1, HEAD), lambda h, i: (0, 0)),
                  pl.BlockSpec((1, HEAD), lambda h, i: (0, 0))],
        out_specs=pl.BlockSpec((SB_BLOCK, HEAD), lambda h, i: (i, h)),
        out_shape=jax.ShapeDtypeStruct((t, BRANCH), f32),
        scratch_shapes=[pltpu.VMEM((_sb_padded(t), HEAD), bf16), pltpu.VMEM((_sb_padded(t), HEAD), bf16)],
        compiler_params=_cparams(("arbitrary", "arbitrary")), name="sb_fwd")(proj, proj, proj, qw, kw)


def sb_bwd(proj, qw, kw, o, do):
    t = proj.shape[0]
    nb = t // SB_BLOCK

    def body(q_ref, k_ref, v_ref, qw_ref, kw_ref, o_ref, do_ref, dq_ref, dk_ref, dv_ref, dqw_ref, dkw_ref,
             kn_scr, vb_scr, dk_acc, dv_acc):
        h, i = pl.program_id(0), pl.program_id(1)

        @pl.when(i == 0)
        def _():
            _sb_prep(k_ref, v_ref, kw_ref, kn_scr, vb_scr, nb)
            dk_acc[...] = jnp.zeros_like(dk_acc)
            dv_acc[...] = jnp.zeros_like(dv_acc)

        @pl.when((i == 0) & (h == 0))
        def _():
            dqw_ref[...] = jnp.zeros_like(dqw_ref)
            dkw_ref[...] = jnp.zeros_like(dkw_ref)

        q = q_ref[...]
        rq = lax.rsqrt(jnp.mean(q * q, axis=-1, keepdims=True) + EPS)
        qh = q * rq
        qn = (qh * qw_ref[...]).astype(bf16)
        do_f = do_ref[...]
        dob = do_f.astype(bf16)
        d_row = jnp.sum(dob.astype(f32) * o_ref[...], axis=-1, keepdims=True)
        u_ex = _tri_ext(lambda r, c: r > c)
        u_in = _tri_ext(lambda r, c: r >= c)

        def tile(jb, carry, masked):
            dq, r_carry, f_carry = carry
            rows = pl.ds(pl.multiple_of(jb * SB_KS, SB_KS), SB_KS)
            kt = kn_scr[rows, :]
            vt = vb_scr[rows, :]
            mask, lsz, lk = _sb_scores(qn, kt, i, jb, masked)
            passed, r_carry = _sb_suffix(lk, r_carry, u_ex)
            a = jnp.exp(lsz + passed)
            if masked:
                a = jnp.where(mask, a, 0.0)
            da = lax.dot_general(dob, vt, (NT, ((), ())), preferred_element_type=f32)
            e = a * da
            e_suf, f_carry = _sb_suffix(e, f_carry, u_in)
            sg = jnp.exp(lsz)
            dz = (e * (1.0 - sg) - (d_row - e_suf) * sg) * SB_SCALE
            if masked:
                dz = jnp.where(mask, dz, 0.0)
            dzb = dz.astype(bf16)
            dq = dq + jnp.dot(dzb, kt, preferred_element_type=f32)
            dk_acc[rows, :] += lax.dot_general(dzb, qn, (TN, ((), ())), preferred_element_type=f32)
            dv_acc[rows, :] += lax.dot_general(a.astype(bf16), dob, (TN, ((), ())), preferred_element_type=f32)
            return dq, r_carry, f_carry

        zeros = jnp.zeros((SB_BLOCK, HEAD), f32)
        dqn, _, _ = _sb_walk(i, tile, (zeros, zeros, zeros))
        gq = dqn * qw_ref[...]
        dq_ref[...] = rq * (gq - qh * jnp.mean(gq * qh, axis=-1, keepdims=True))
        dqw_ref[...] += jnp.sum(dqn * qh, axis=0, keepdims=True)

        @pl.when(i == nb - 1)
        def _():
            def fin(b, c):
                rows = pl.ds(pl.multiple_of(b * SB_BLOCK, SB_BLOCK), SB_BLOCK)
                kk = k_ref[rows, :]
                rk = lax.rsqrt(jnp.mean(kk * kk, axis=-1, keepdims=True) + EPS)
                kh = kk * rk
                dkn = dk_acc[rows, :]
                gk = dkn * kw_ref[...]
                dk_ref[rows, :] = rk * (gk - kh * jnp.mean(gk * kh, axis=-1, keepdims=True))
                dv_ref[rows, :] = dv_acc[rows, :]
                dkw_ref[...] += jnp.sum(dkn * kh, axis=0, keepdims=True)
                return c

            lax.fori_loop(0, nb, fin, 0)

    cb = C_SBK // HEAD
    vb = C_SBV // HEAD
    blk = pl.BlockSpec((SB_BLOCK, HEAD), lambda h, i: (i, h))
    full = pl.BlockSpec((t, HEAD), lambda h, i: (0, h))
    wsp = pl.BlockSpec((1, HEAD), lambda h, i: (0, 0))
    return pl.pallas_call(
        body, grid=(N_HEADS, nb),
        in_specs=[blk, pl.BlockSpec((t, HEAD), lambda h, i: (0, cb + h)),
                  pl.BlockSpec((t, HEAD), lambda h, i: (0, vb + h)), wsp, wsp, blk, blk],
        out_specs=[blk, full, full, wsp, wsp],
        out_shape=[jax.ShapeDtypeStruct((t, BRANCH), f32)] * 3 + [jax.ShapeDtypeStruct((1, HEAD), f32)] * 2,
        scratch_shapes=[pltpu.VMEM((_sb_padded(t), HEAD), bf16), pltpu.VMEM((_sb_padded(t), HEAD), bf16),
                        pltpu.VMEM((_sb_padded(t), HEAD), f32), pltpu.VMEM((_sb_padded(t), HEAD), f32)],
        compiler_params=_cparams(("arbitrary", "arbitrary")), name="sb_bwd")(proj, proj, proj, qw, kw, o, do)


TM_CONV = 640
CONV_W = 4
GQKV = 3 * BRANCH


def conv_fwd(proj, cw):
    t = proj.shape[0]
    halo_blocks = TM_CONV // 8
    cb = C_GQKV // GQKV
    del cb

    def body(x0_ref, x1_ref, x2_ref, p0_ref, p1_ref, p2_ref, cw_ref, y_ref):
        i = pl.program_id(0)
        for s, (x_ref, p_ref) in enumerate(((x0_ref, p0_ref), (x1_ref, p1_ref), (x2_ref, p2_ref))):
            prev = jnp.where(i > 0, p_ref[...], 0.0)
            xx = jnp.concatenate([prev, x_ref[...]], axis=0)
            cols = slice(s * BRANCH, (s + 1) * BRANCH)
            y = xx[8:] * cw_ref[CONV_W - 1:CONV_W, cols]
            for k in range(CONV_W - 1):
                y = y + pltpu.roll(xx, CONV_W - 1 - k, 0)[8:] * cw_ref[k:k + 1, cols]
            y_ref[:, cols] = y

    c0 = C_GQKV // BRANCH
    xs = [pl.BlockSpec((TM_CONV, BRANCH), functools.partial(lambda i, s: (i, c0 + s), s=s)) for s in range(3)]
    ps = [pl.BlockSpec((8, BRANCH), functools.partial(lambda i, s: (jnp.maximum(i * halo_blocks - 1, 0), c0 + s), s=s))
          for s in range(3)]
    return pl.pallas_call(
        body, grid=(t // TM_CONV,),
        in_specs=xs + ps + [pl.BlockSpec((CONV_W, GQKV), lambda i: (0, 0))],
        out_specs=pl.BlockSpec((TM_CONV, GQKV), lambda i: (i, 0)),
        out_shape=jax.ShapeDtypeStruct((t, GQKV), f32),
        compiler_params=_cparams(("arbitrary",)), name="conv_fwd")(proj, proj, proj, proj, proj, proj, cw)


def conv_bwd(proj, cw, dy):
    t = proj.shape[0]
    nt = t // TM_CONV
    halo_blocks = TM_CONV // 8

    def body(x0_ref, x1_ref, x2_ref, p0_ref, p1_ref, p2_ref, cw_ref, dy_ref, dyn_ref, dx_ref, dw_ref):
        i = pl.program_id(0)

        @pl.when(i == 0)
        def _():
            dw_ref[...] = jnp.zeros_like(dw_ref)

        nxt = jnp.where(i < nt - 1, dyn_ref[...], 0.0)
        dyy = jnp.concatenate([dy_ref[...], nxt], axis=0)
        n_rows = TM_CONV + 8
        dx = dyy[:TM_CONV] * cw_ref[CONV_W - 1:CONV_W, :]
        for k in range(CONV_W - 1):
            sh = CONV_W - 1 - k
            dx = dx + pltpu.roll(dyy, n_rows - sh, 0)[:TM_CONV] * cw_ref[k:k + 1, :]
        dx_ref[...] = dx
        dy_c = dy_ref[...]
        for s, (x_ref, p_ref) in enumerate(((x0_ref, p0_ref), (x1_ref, p1_ref), (x2_ref, p2_ref))):
            prev = jnp.where(i > 0, p_ref[...], 0.0)
            xx = jnp.concatenate([prev, x_ref[...]], axis=0)
            cols = slice(s * BRANCH, (s + 1) * BRANCH)
            for k in range(CONV_W):
                sh = CONV_W - 1 - k
                xs = xx[8:] if sh == 0 else pltpu.roll(xx, sh, 0)[8:]
                dw_ref[k:k + 1, cols] += jnp.sum(xs * dy_c[:, cols], axis=0, keepdims=True)

    c0 = C_GQKV // BRANCH
    xs = [pl.BlockSpec((TM_CONV, BRANCH), functools.partial(lambda i, s: (i, c0 + s), s=s)) for s in range(3)]
    ps = [pl.BlockSpec((8, BRANCH), functools.partial(lambda i, s: (jnp.maximum(i * halo_blocks - 1, 0), c0 + s), s=s))
          for s in range(3)]
    return pl.pallas_call(
        body, grid=(nt,),
        in_specs=xs + ps + [pl.BlockSpec((CONV_W, GQKV), lambda i: (0, 0)),
                            pl.BlockSpec((TM_CONV, GQKV), lambda i: (i, 0)),
                            pl.BlockSpec((8, GQKV), lambda i: (jnp.minimum((i + 1) * halo_blocks, nt * halo_blocks - 1), 0))],
        out_specs=[pl.BlockSpec((TM_CONV, GQKV), lambda i: (i, 0)), pl.BlockSpec((CONV_W, GQKV), lambda i: (0, 0))],
        out_shape=[jax.ShapeDtypeStruct((t, GQKV), f32), jax.ShapeDtypeStruct((CONV_W, GQKV), f32)],
        compiler_params=_cparams(("arbitrary",)), name="conv_bwd")(proj, proj, proj, proj, proj, proj, cw, dy, dy)


def _iota2(n, m, d):
    return lax.broadcasted_iota(jnp.int32, (n, m), d)


def _lane_pick(row_or_mat, idx):
    lanes = lax.broadcasted_iota(jnp.int32, row_or_mat.shape, row_or_mat.ndim - 1)
    return jnp.sum(jnp.where(lanes == idx, row_or_mat, 0.0), axis=-1, keepdims=True)


def _cumsum_consts():
    i = np.arange(CHUNK)
    incl = i[None, :] <= i[:, None]
    suf = i[None, :] > i[:, None]
    return np.concatenate([incl, suf], 0).astype(np.float32)


HG_LEVELS = (64, 32, 16, 8, 4, 2)


def _hgrn_consts():
    i = np.arange(CHUNK)
    rows = [i[None, :] <= i[:, None], i[None, :] > i[:, None]]
    for b in HG_LEVELS:
        ref = (i // b) * b + b // 2 - 1
        second = (i % b) >= b // 2
        rows.append((i[None, :] > ref[:, None]) & (i[None, :] <= i[:, None]) & second[:, None])
        rows.append((i[None, :] > i[:, None]) & (i[None, :] <= ref[:, None]) & (~second)[:, None])
    return np.concatenate(rows, 0).astype(np.float32)


HC = N_HEADS * CHUNK
BATCH0 = ((0,), (0,))


def _bdot(a, b, contract):
    return lax.dot_general(a.astype(bf16), b.astype(bf16), (contract, BATCH0), preferred_element_type=f32)


B_NN = ((2,), (1,))
B_NT = ((2,), (2,))
B_TN = ((1,), (1,))


@jax.custom_vjp
def bmm(a, b):
    return _bdot(a, b, B_NN)


bmm.defvjp(lambda a, b: (_bdot(a, b, B_NN), (a, b)),
           lambda r, g: (_bdot(g, r[1], B_NT), _bdot(r[0], g, B_TN)))


@jax.custom_vjp
def bmm_nt(a, b):
    return _bdot(a, b, B_NT)


bmm_nt.defvjp(lambda a, b: (_bdot(a, b, B_NT), (a, b)),
              lambda r, g: (_bdot(g, r[1], B_NN), _bdot(g, r[0], B_TN)))


@jax.custom_vjp
def bmm_tn(a, b):
    return _bdot(a, b, B_TN)


bmm_tn.defvjp(lambda a, b: (_bdot(a, b, B_TN), (a, b)),
              lambda r, g: (_bdot(r[1], g, B_NT), _bdot(r[0], g, B_NN)))


def _stack_heads(x):
    return jnp.concatenate([x[:, h * HEAD:(h + 1) * HEAD] for h in range(N_HEADS)], axis=0)


def _unstack_heads(x):
    return jnp.concatenate([x[h * CHUNK:(h + 1) * CHUNK] for h in range(N_HEADS)], axis=1)


def _gdn_chunk(state, ypre, small, gz, a_log, dt_b, on_w, c2, vm):
    r = _iota2(HC, HC, 0)
    c = _iota2(HC, HC, 1)
    same_head = (r >> 6) == (c >> 6)
    causal = same_head & (r >= c)
    strict = same_head & (r > c)
    q = _silu(_stack_heads(ypre[:, :BRANCH]))
    k = _silu(_stack_heads(ypre[:, BRANCH:2 * BRANCH]))
    v = _silu(_stack_heads(ypre[:, 2 * BRANCH:]))
    q = q * lax.rsqrt(jnp.sum(q * q, axis=-1, keepdims=True) + EPS) * (HEAD ** -0.5)
    k = k * lax.rsqrt(jnp.sum(k * k, axis=-1, keepdims=True) + EPS)
    col = lambda f: jnp.concatenate([f(h) for h in range(N_HEADS)], axis=0)
    chunk_col = lambda x: jnp.broadcast_to(x, (CHUNK, 1))
    beta = _sigmoid(col(lambda h: _lane_pick(small, h))) * col(lambda h: vm)
    g = (-jnp.exp(col(lambda h: chunk_col(_lane_pick(a_log, h))))
         * _softplus(col(lambda h: _lane_pick(small, N_HEADS + h)) + col(lambda h: chunk_col(_lane_pick(dt_b, h)))))
    g_l = _unstack_heads(jnp.broadcast_to(g, (HC, HEAD)))
    e2 = cmm(c2, g_l)
    gc = _stack_heads(e2[:CHUNK])
    gsuf = _stack_heads(e2[CHUNK:])
    g_row = jnp.broadcast_to(jnp.transpose(gc)[0:1, :], (HC, HC))
    g_col = jnp.concatenate([gc, gc], axis=1)
    dec = jnp.where(causal, jnp.exp(jnp.minimum(g_col - g_row, 0.0)), 0.0)
    kb = k * beta
    m = jnp.where(strict, mm_nt(kb, k) * dec, 0.0)
    x = jnp.concatenate([v * beta, kb * jnp.exp(gc)], axis=1)
    x = x - mm(m, x)
    p = m
    for _ in range(5):
        p = mm(p, p)
        x = x + mm(p, x)
    u, w = x[:, :HEAD], x[:, HEAD:]
    aqk = jnp.where(causal, mm_nt(q, k) * dec, 0.0)
    tot = jnp.sum(g_l, axis=0, keepdims=True)
    g_last = jnp.exp(jnp.stack([tot[:, h * HEAD:(h + 1) * HEAD] for h in range(N_HEADS)], axis=0))
    per_head = lambda a: a.reshape(N_HEADS, CHUNK, HEAD)
    v_new = u - bmm(per_head(w), state).reshape(HC, HEAD)
    o = bmm(per_head(q * jnp.exp(gc)), state).reshape(HC, HEAD) + mm(aqk, v_new)
    new_state = state * g_last + bmm_tn(per_head(k * jnp.exp(gsuf)), per_head(v_new))
    out = _unstack_heads(_rms(o, on_w)) * _silu(gz)
    return new_state, out


def _hgrn_chunk(state, hq, hf, hi, hz, lb, on_w, cm, vm):
    r = _iota2(HC, HC, 0)
    c = _iota2(HC, HC, 1)
    forget = lb + (1.0 - lb) * _sigmoid(hf)
    g_l = jnp.log(forget)
    e = cmm(cm, g_l)
    q = _stack_heads(_silu(hq))
    k = _stack_heads((1.0 - lb) * _sigmoid(-hf))
    v = _stack_heads(hi * vm)
    sect = lambda n: _stack_heads(e[n * CHUNK:(n + 1) * CHUNK])
    gc, gsuf = sect(0), sect(1)
    per_head = lambda a: a.reshape(N_HEADS, CHUNK, HEAD)
    o = bmm_nt(per_head(q * jnp.exp(gc)), state).reshape(HC, HEAD)
    a = jnp.where(r == c, jnp.sum(q * k, axis=-1, keepdims=True), 0.0)
    for li, b in enumerate(HG_LEVELS):
        sh = b.bit_length() - 1
        pair = ((r >> sh) == (c >> sh)) & ((r & (b - 1)) >= b // 2) & ((c & (b - 1)) < b // 2)
        a = a + jnp.where(pair, mm_nt(q * jnp.exp(sect(2 + 2 * li)), k * jnp.exp(sect(3 + 2 * li))), 0.0)
    o = o + mm(a, v)
    tot = jnp.sum(g_l, axis=0, keepdims=True)
    g_end = jnp.exp(jnp.stack([tot[:, h * HEAD:(h + 1) * HEAD] for h in range(N_HEADS)], axis=0))
    new_state = state * g_end + bmm_tn(per_head(v), per_head(k * jnp.exp(gsuf)))
    out = _unstack_heads(_rms(o, on_w)) * _silu(hz)
    return new_state, out


def _vmask(chunk_idx):
    rows = chunk_idx * CHUNK + lax.broadcasted_iota(jnp.int32, (CHUNK, 1), 0)
    return jnp.where(rows >= PAD_FRONT, 1.0, 0.0)


def _row(n):
    return pl.BlockSpec((1, n), lambda i: (0, 0))


def gdn_fwd(ypre, small, proj, a_log, dt_b, on_w):
    t = ypre.shape[0]
    nc = t // CHUNK
    c2 = jnp.asarray(_cumsum_consts(), bf16)

    def body(y_ref, s_ref, z_ref, al_ref, dt_ref, on_ref, c2_ref, o_ref, st_ref, state):
        i = pl.program_id(0)

        @pl.when(i == 0)
        def _():
            state[...] = jnp.zeros_like(state)

        s_in = state[...]
        st_ref[0] = s_in
        s_new, out = _gdn_chunk(s_in, y_ref[...], s_ref[...], z_ref[...], al_ref[...], dt_ref[...], on_ref[...],
                                c2_ref[...], _vmask(i))
        state[...] = s_new
        o_ref[...] = out

    return pl.pallas_call(
        body, grid=(nc,),
        in_specs=[pl.BlockSpec((CHUNK, GQKV), lambda i: (i, 0)), pl.BlockSpec((CHUNK, N_SMALL), lambda i: (i, 0)),
                  pl.BlockSpec((CHUNK, BRANCH), lambda i: (i, C_GZ // BRANCH)), _row(128), _row(128), _row(128),
                  pl.BlockSpec((2 * CHUNK, CHUNK), lambda i: (0, 0))],
        out_specs=[pl.BlockSpec((CHUNK, BRANCH), lambda i: (i, 0)),
                   pl.BlockSpec((1, N_HEADS, HEAD, HEAD), lambda i: (i, 0, 0, 0))],
        out_shape=[jax.ShapeDtypeStruct((t, BRANCH), f32), jax.ShapeDtypeStruct((nc, N_HEADS, HEAD, HEAD), f32)],
        scratch_shapes=[pltpu.VMEM((N_HEADS, HEAD, HEAD), f32)],
        compiler_params=_cparams(("arbitrary",)), name="gdn_fwd")(ypre, small, proj, a_log, dt_b, on_w, c2)


def gdn_bwd(ypre, small, proj, a_log, dt_b, on_w, states, d_out):
    t = ypre.shape[0]
    nc = t // CHUNK
    c2 = jnp.asarray(_cumsum_consts(), bf16)

    def body(y_ref, s_ref, z_ref, al_ref, dt_ref, on_ref, c2_ref, st_ref, do_ref,
             dy_ref, ds_ref, dz_ref, dal_ref, ddt_ref, don_ref, dstate):
        i = pl.program_id(0)

        @pl.when(i == 0)
        def _():
            dstate[...] = jnp.zeros_like(dstate)
            dal_ref[...] = jnp.zeros_like(dal_ref)
            ddt_ref[...] = jnp.zeros_like(ddt_ref)
            don_ref[...] = jnp.zeros_like(don_ref)

        vm = _vmask(nc - 1 - i)
        c2v = c2_ref[...]
        fn = lambda s, y, sm, z, al, dt, on: _gdn_chunk(s, y, sm, z, al, dt, on, c2v, vm)
        _, vjp = jax.vjp(fn, st_ref[0], y_ref[...], s_ref[...], z_ref[...], al_ref[...], dt_ref[...], on_ref[...])
        d_s, d_y, d_sm, d_z, d_al, d_dt, d_on = vjp((dstate[...], do_ref[...]))
        dstate[...] = d_s
        dy_ref[...] = d_y
        ds_ref[...] = d_sm
        dz_ref[...] = d_z
        dal_ref[...] += d_al
        ddt_ref[...] += d_dt
        don_ref[...] += d_on

    rev = lambda i: (nc - 1 - i, 0)
    return pl.pallas_call(
        body, grid=(nc,),
        in_specs=[pl.BlockSpec((CHUNK, GQKV), rev), pl.BlockSpec((CHUNK, N_SMALL), rev),
                  pl.BlockSpec((CHUNK, BRANCH), lambda i: (nc - 1 - i, C_GZ // BRANCH)), _row(128), _row(128), _row(128),
                  pl.BlockSpec((2 * CHUNK, CHUNK), lambda i: (0, 0)),
                  pl.BlockSpec((1, N_HEADS, HEAD, HEAD), lambda i: (nc - 1 - i, 0, 0, 0)),
                  pl.BlockSpec((CHUNK, BRANCH), rev)],
        out_specs=[pl.BlockSpec((CHUNK, GQKV), rev), pl.BlockSpec((CHUNK, N_SMALL), rev),
                   pl.BlockSpec((CHUNK, BRANCH), rev), _row(128), _row(128), _row(128)],
        out_shape=[jax.ShapeDtypeStruct((t, GQKV), f32), jax.ShapeDtypeStruct((t, N_SMALL), f32),
                   jax.ShapeDtypeStruct((t, BRANCH), f32)] + [jax.ShapeDtypeStruct((1, 128), f32)] * 3,
        scratch_shapes=[pltpu.VMEM((N_HEADS, HEAD, HEAD), f32)],
        compiler_params=_cparams(("arbitrary",)), name="gdn_bwd")(
            ypre, small, proj, a_log, dt_b, on_w, c2, states, d_out)


def hgrn_fwd(proj, lb, on_w):
    t = proj.shape[0]
    nc = t // CHUNK
    cm = jnp.asarray(_hgrn_consts(), bf16)
    ncm = cm.shape[0]

    def body(q_ref, f_ref, i_ref, z_ref, lb_ref, on_ref, cm_ref, o_ref, st_ref, state):
        i = pl.program_id(0)

        @pl.when(i == 0)
        def _():
            state[...] = jnp.zeros_like(state)

        s_in = state[...]
        st_ref[0] = s_in
        s_new, out = _hgrn_chunk(s_in, q_ref[...], f_ref[...], i_ref[...], z_ref[...], lb_ref[...], on_ref[...],
                                 cm_ref[...], _vmask(i))
        state[...] = s_new
        o_ref[...] = out

    sec = lambda off: pl.BlockSpec((CHUNK, BRANCH), functools.partial(lambda i, b: (i, b), b=off // BRANCH))
    return pl.pallas_call(
        body, grid=(nc,),
        in_specs=[sec(C_HQ), sec(C_HF), sec(C_HI), sec(C_HZ), _row(BRANCH), _row(128),
                  pl.BlockSpec((ncm, CHUNK), lambda i: (0, 0))],
        out_specs=[pl.BlockSpec((CHUNK, BRANCH), lambda i: (i, 0)),
                   pl.BlockSpec((1, N_HEADS, HEAD, HEAD), lambda i: (i, 0, 0, 0))],
        out_shape=[jax.ShapeDtypeStruct((t, BRANCH), f32), jax.ShapeDtypeStruct((nc, N_HEADS, HEAD, HEAD), f32)],
        scratch_shapes=[pltpu.VMEM((N_HEADS, HEAD, HEAD), f32)],
        compiler_params=_cparams(("arbitrary",)), name="hgrn_fwd")(proj, proj, proj, proj, lb, on_w, cm)


def hgrn_bwd(proj, lb, on_w, states, d_out):
    t = proj.shape[0]
    nc = t // CHUNK
    cm = jnp.asarray(_hgrn_consts(), bf16)
    ncm = cm.shape[0]

    def body(q_ref, f_ref, i_ref, z_ref, lb_ref, on_ref, cm_ref, st_ref, do_ref, dh_ref, dlb_ref, don_ref, dstate):
        i = pl.program_id(0)

        @pl.when(i == 0)
        def _():
            dstate[...] = jnp.zeros_like(dstate)
            dlb_ref[...] = jnp.zeros_like(dlb_ref)
            don_ref[...] = jnp.zeros_like(don_ref)

        vm = _vmask(nc - 1 - i)
        cmv = cm_ref[...]
        fn = lambda s, a, b, c, d, l, on: _hgrn_chunk(s, a, b, c, d, l, on, cmv, vm)
        _, vjp = jax.vjp(fn, st_ref[0], q_ref[...], f_ref[...], i_ref[...], z_ref[...], lb_ref[...], on_ref[...])
        d_s, d_q, d_f, d_i, d_z, d_lb, d_on = vjp((dstate[...], do_ref[...]))
        dstate[...] = d_s
        dh_ref[...] = jnp.concatenate([d_q, d_f, d_i, d_z], axis=1)
        dlb_ref[...] += d_lb
        don_ref[...] += d_on

    rev = lambda i: (nc - 1 - i, 0)
    sec = lambda off: pl.BlockSpec((CHUNK, BRANCH), functools.partial(lambda i, b: (nc - 1 - i, b), b=off // BRANCH))
    return pl.pallas_call(
        body, grid=(nc,),
        in_specs=[sec(C_HQ), sec(C_HF), sec(C_HI), sec(C_HZ), _row(BRANCH), _row(128),
                  pl.BlockSpec((ncm, CHUNK), lambda i: (0, 0)),
                  pl.BlockSpec((1, N_HEADS, HEAD, HEAD), lambda i: (nc - 1 - i, 0, 0, 0)),
                  pl.BlockSpec((CHUNK, BRANCH), rev)],
        out_specs=[pl.BlockSpec((CHUNK, 4 * BRANCH), rev), _row(BRANCH), _row(128)],
        out_shape=[jax.ShapeDtypeStruct((t, 4 * BRANCH), f32), jax.ShapeDtypeStruct((1, BRANCH), f32),
                   jax.ShapeDtypeStruct((1, 128), f32)],
        scratch_shapes=[pltpu.VMEM((N_HEADS, HEAD, HEAD), f32)],
        compiler_params=_cparams(("arbitrary",)), name="hgrn_bwd")(proj, proj, proj, proj, lb, on_w, cm, states, d_out)


TM_MG = 320


def _const_spec(shape):
    nd = len(shape)
    return pl.BlockSpec(shape, lambda i: (0,) * nd, pipeline_mode=pl.Buffered(1))


def merge_fwd(osb, proj, ogd, ohg, wb, wo, h):
    t = h.shape[0]

    def body(osb_ref, sbz_ref, ogd_ref, ohg_ref, mix_ref, wb_ref, wo_ref, h_ref, out_ref):
        a = osb_ref[...] * _silu(sbz_ref[...])
        y = (_sigmoid(mix_ref[:, 0:D_MODEL]) * _dot(a, wb_ref[0])
             + _sigmoid(mix_ref[:, D_MODEL:2 * D_MODEL]) * _dot(ogd_ref[...], wb_ref[1])
             + _sigmoid(mix_ref[:, 2 * D_MODEL:3 * D_MODEL]) * _dot(ohg_ref[...], wb_ref[2]))
        out_ref[...] = h_ref[...] + _dot(y, wo_ref[...])

    br = pl.BlockSpec((TM_MG, BRANCH), lambda i: (i, 0))
    return pl.pallas_call(
        body, grid=(t // TM_MG,),
        in_specs=[br, pl.BlockSpec((TM_MG, BRANCH), lambda i: (i, C_SBZ // BRANCH)), br, br,
                  pl.BlockSpec((TM_MG, 3 * D_MODEL), lambda i: (i, C_MIX // (3 * D_MODEL))),
                  _const_spec((3, BRANCH, D_MODEL)), _const_spec((D_MODEL, D_MODEL)),
                  pl.BlockSpec((TM_MG, D_MODEL), lambda i: (i, 0))],
        out_specs=pl.BlockSpec((TM_MG, D_MODEL), lambda i: (i, 0)),
        out_shape=jax.ShapeDtypeStruct((t, D_MODEL), f32),
        compiler_params=_cparams(("arbitrary",)), name="merge_fwd")(osb, proj, ogd, ohg, proj, wb, wo, h)


def merge_bwd(osb, proj, ogd, ohg, wb, wbt, wot, dh):
    t = dh.shape[0]

    def body(osb_ref, sbz_ref, ogd_ref, ohg_ref, mix_ref, wb_ref, wbt_ref, wot_ref, dh_ref,
             dosb_ref, dsbz_ref, dogd_ref, dohg_ref, dmix_ref, dwo_ref, dwb_ref):
        i = pl.program_id(0)

        @pl.when(i == 0)
        def _():
            dwo_ref[...] = jnp.zeros_like(dwo_ref)
            dwb_ref[...] = jnp.zeros_like(dwb_ref)

        osb = osb_ref[...]
        sbz = sbz_ref[...]
        sgz = _sigmoid(sbz)
        sz = sbz * sgz
        branch_in = (osb * sz, ogd_ref[...], ohg_ref[...])
        dh_v = dh_ref[...]
        dy = _dot(dh_v, wot_ref[...])
        y = jnp.zeros((TM_MG, D_MODEL), f32)
        d_in = []
        for b in range(3):
            p = _dot(branch_in[b], wb_ref[b])
            g = _sigmoid(mix_ref[:, b * D_MODEL:(b + 1) * D_MODEL])
            y = y + g * p
            dp = dy * g
            dmix_ref[:, b * D_MODEL:(b + 1) * D_MODEL] = dy * p * g * (1.0 - g)
            d_in.append(_dot(dp, wbt_ref[b]))
            dwb_ref[b] += _dot(branch_in[b], dp, TN)
        dwo_ref[...] += _dot(y, dh_v, TN)
        dosb_ref[...] = d_in[0] * sz
        dsbz_ref[...] = d_in[0] * osb * (sgz * (1.0 + sbz * (1.0 - sgz)))
        dogd_ref[...] = d_in[1]
        dohg_ref[...] = d_in[2]

    br = pl.BlockSpec((TM_MG, BRANCH), lambda i: (i, 0))
    return pl.pallas_call(
        body, grid=(t // TM_MG,),
        in_specs=[br, pl.BlockSpec((TM_MG, BRANCH), lambda i: (i, C_SBZ // BRANCH)), br, br,
                  pl.BlockSpec((TM_MG, 3 * D_MODEL), lambda i: (i, C_MIX // (3 * D_MODEL))),
                  _const_spec((3, BRANCH, D_MODEL)), _const_spec((3, D_MODEL, BRANCH)), _const_spec((D_MODEL, D_MODEL)),
                  pl.BlockSpec((TM_MG, D_MODEL), lambda i: (i, 0))],
        out_specs=[br, br, br, br, pl.BlockSpec((TM_MG, 3 * D_MODEL), lambda i: (i, 0)),
                   _const_spec((D_MODEL, D_MODEL)), _const_spec((3, BRANCH, D_MODEL))],
        out_shape=[jax.ShapeDtypeStruct((t, BRANCH), f32)] * 4 + [jax.ShapeDtypeStruct((t, 3 * D_MODEL), f32),
                   jax.ShapeDtypeStruct((D_MODEL, D_MODEL), f32), jax.ShapeDtypeStruct((3, BRANCH, D_MODEL), f32)],
        compiler_params=_cparams(("arbitrary",)), name="merge_bwd")(osb, proj, ogd, ohg, proj, wb, wbt, wot, dh)


def loss_head(h, target):
    t = h.shape[0]
    nb = t // SB_BLOCK

    def body(h_ref, t_ref, dh_ref, loss_ref):
        i = pl.program_id(0)

        @pl.when(i == 0)
        def _():
            loss_ref[...] = jnp.zeros_like(loss_ref)
            dh_ref[...] = jnp.zeros_like(dh_ref)

        @pl.when(i > 0)
        def _():
            err = h_ref[...] - t_ref[...]
            dh_ref[...] = err * (1.0 / D_MODEL)
            loss_ref[...] += jnp.broadcast_to(jnp.sum(err * err) * (0.5 / D_MODEL), loss_ref.shape)

    return pl.pallas_call(
        body, grid=(nb,),
        in_specs=[pl.BlockSpec((SB_BLOCK, D_MODEL), lambda i: (i, 0)),
                  pl.BlockSpec((SB_BLOCK, D_MODEL), lambda i: (jnp.maximum(i - 1, 0), 0))],
        out_specs=[pl.BlockSpec((SB_BLOCK, D_MODEL), lambda i: (i, 0)), pl.BlockSpec((1, 128), lambda i: (0, 0))],
        out_shape=[jax.ShapeDtypeStruct((t, D_MODEL), f32), jax.ShapeDtypeStruct((1, 128), f32)],
        compiler_params=_cparams(("arbitrary",)), name="loss_head")(h, target)


def adamw(parts, w, m, v, rows_per_step, name):
    r, c = w.shape
    tr = min(rows_per_step, r)

    def body(p_ref, w_ref, m_ref, v_ref, g_ref, d_ref, nm_ref, nv_ref):
        g = p_ref[0]
        for k in range(1, N_DEV):
            g = g + p_ref[k]
        m_new = ADAM_B1 * m_ref[...] + (1.0 - ADAM_B1) * g
        v_new = ADAM_B2 * v_ref[...] + (1.0 - ADAM_B2) * jnp.square(g)
        m_hat = m_new / (1.0 - ADAM_B1 ** ADAM_STEP)
        v_hat = v_new / (1.0 - ADAM_B2 ** ADAM_STEP)
        g_ref[...] = g
        d_ref[...] = -ADAM_LR * (m_hat / (jnp.sqrt(v_hat) + ADAM_EPS) + ADAM_WD * w_ref[...])
        nm_ref[...] = m_new
        nv_ref[...] = v_new

    blk = pl.BlockSpec((tr, c), lambda i: (i, 0))
    return pl.pallas_call(
        body, grid=(r // tr,),
        in_specs=[pl.BlockSpec((N_DEV, tr, c), lambda i: (0, i, 0)), blk, blk, blk],
        out_specs=[blk] * 4, out_shape=[jax.ShapeDtypeStruct((r, c), f32)] * 4,
        compiler_params=_cparams(("arbitrary",)), name=name)(parts, w, m, v)


def _mesh_pos():
    return lax.axis_index("x"), lax.axis_index("y"), lax.axis_index("c")


def _peer(pos, k):
    x, y, c = pos
    return (1 - x if k & 4 else x, 1 - y if k & 2 else y, 1 - c if k & 1 else c)


def _lin(pos):
    return 4 * pos[0] + 2 * pos[1] + pos[2]


def exchange(srcs, scatter, name):
    n = len(srcs)
    shapes = [s.shape[1:] if sc else s.shape for s, sc in zip(srcs, scatter)]

    def body(*refs):
        src_refs, dst_refs = refs[:n], refs[n:2 * n]
        send_sems, recv_sems, local_sems = refs[2 * n:]
        me = _mesh_pos()
        me_lin = _lin(me)
        sends, recvs, locals_ = [], [], []
        for t in range(n):
            own = src_refs[t].at[me_lin] if scatter[t] else src_refs[t]
            locals_.append(pltpu.make_async_copy(own, dst_refs[t].at[me_lin], local_sems.at[t]))
            for k in range(1, N_DEV):
                peer = _peer(me, k)
                src = src_refs[t].at[_lin(peer)] if scatter[t] else src_refs[t]
                sends.append(pltpu.make_async_remote_copy(
                    src_ref=src, dst_ref=dst_refs[t].at[me_lin], send_sem=send_sems.at[t, k - 1],
                    recv_sem=recv_sems.at[t, k - 1], device_id=peer, device_id_type=MESH))
                recvs.append(pltpu.make_async_remote_copy(
                    src_ref=src, dst_ref=dst_refs[t].at[_lin(peer)], send_sem=send_sems.at[t, k - 1],
                    recv_sem=recv_sems.at[t, k - 1], device_id=peer, device_id_type=MESH))
        for cp in locals_ + sends:
            cp.start()
        for cp in sends:
            cp.wait_send()
        for cp in recvs:
            cp.wait_recv()
        for cp in locals_:
            cp.wait()

    any_spec = pl.BlockSpec(memory_space=pl.ANY)
    return pl.pallas_call(
        body, in_specs=[any_spec] * n, out_specs=[any_spec] * n,
        out_shape=[jax.ShapeDtypeStruct((N_DEV,) + tuple(sh), s.dtype) for sh, s in zip(shapes, srcs)],
        scratch_shapes=[pltpu.SemaphoreType.DMA((n, N_DEV - 1)), pltpu.SemaphoreType.DMA((n, N_DEV - 1)),
                        pltpu.SemaphoreType.DMA((n,))],
        compiler_params=pltpu.CompilerParams(has_side_effects=True), name=name)(*srcs)


PACK_ROWS = 104


def _pad_rows(a, rows):
    return jnp.pad(a, ((0, rows - a.shape[0]), (0, 0)))


def _pad_lanes(a):
    return jnp.pad(a, ((0, 0), (0, 128 - a.shape[1])))


def _pack(norm_w, sbq, sbk, alog, dtb, gon, lbl, hon, loss_row):
    parts = [norm_w.reshape(32, 128), _pad_rows(sbq, 8), _pad_rows(sbk, 8), _pad_rows(_pad_lanes(alog), 8),
             _pad_rows(_pad_lanes(dtb), 8), _pad_rows(gon, 8), lbl.reshape(16, 128), _pad_rows(hon, 8),
             _pad_rows(loss_row, 8)]
    return jnp.concatenate(parts, axis=0)


def _unpack(p):
    return dict(norm_w=p[0:32].reshape(DEPTH, D_MODEL), sb_q_norm=p[32:36], sb_k_norm=p[40:44],
                gdn_a_log=p[48:52, :N_HEADS], gdn_dt_bias=p[56:60, :N_HEADS], gdn_out_norm=p[64:68],
                hgrn_lb_logits=p[72:88].reshape(DEPTH, BRANCH), hgrn_out_norm=p[88:92], loss=p[96, 0])


def _lower_bounds(logits):
    p = jax.nn.softmax(logits, axis=0)
    return jnp.cumsum(p, axis=0) - p[0:1]


def _unshard_cols(g):
    nd = g.ndim
    g = jnp.moveaxis(g, 0, nd - 2)
    return g.reshape(g.shape[:-2] + (N_DEV * g.shape[-1],))


def _shard_cols(a):
    n = a.shape[-1] // N_DEV
    return jnp.moveaxis(a.reshape(a.shape[:-1] + (N_DEV, n)), -2, 0)


def kernel(x, meta_tokens, norm_w, w_in, sb_q_norm, sb_k_norm, gdn_conv_w, gdn_a_log, gdn_dt_bias, gdn_out_norm, hgrn_lb_logits, hgrn_out_norm, w_branch, w_out, loss_target, m_meta_tokens, m_norm_w, m_w_in, m_sb_q_norm, m_sb_k_norm, m_gdn_conv_w, m_gdn_a_log, m_gdn_dt_bias, m_gdn_out_norm, m_hgrn_lb_logits, m_hgrn_out_norm, m_w_branch, m_w_out, v_meta_tokens, v_norm_w, v_w_in, v_sb_q_norm, v_sb_k_norm, v_gdn_conv_w, v_gdn_a_log, v_gdn_dt_bias, v_gdn_out_norm, v_hgrn_lb_logits, v_hgrn_out_norm, v_w_branch, v_w_out):
    g_win, g_wbr, g_wout, g_meta, g_conv = exchange(
        [w_in.astype(bf16), w_branch.astype(bf16), w_out.astype(bf16), meta_tokens, gdn_conv_w],
        [False] * 5, "gather_weights")
    w_full = _unshard_cols(g_win)
    w_main = jnp.concatenate([w_full[..., :SMALL_OFF], w_full[..., SMALL_OFF + 8:]], axis=-1)
    w_small = jnp.pad(w_full[..., SMALL_OFF:SMALL_OFF + 8], ((0, 0), (0, 0), (0, N_SMALL - 8)))
    wt_main = jnp.swapaxes(w_main, 1, 2)
    wt_small = jnp.swapaxes(w_small, 1, 2)
    wbr = _unshard_cols(g_wbr)
    wbr_t = jnp.swapaxes(wbr, 2, 3)
    wout = jnp.moveaxis(g_wout, 0, 1).reshape(DEPTH, D_MODEL, D_MODEL)
    wout_t = jnp.swapaxes(wout, 1, 2)
    meta = _unshard_cols(g_meta)
    conv_w = _unshard_cols(g_conv)
    lbounds, lb_vjp = jax.vjp(_lower_bounds, hgrn_lb_logits)

    h = jnp.concatenate([jnp.zeros((PAD_FRONT, D_MODEL), f32), meta, x[0]], axis=0)
    row = lambda a: a.reshape(1, -1)
    saved = []
    for l in range(DEPTH):
        proj, small, _, xnt = inproj_fwd(h, row(norm_w[l]), w_main[l], w_small[l])
        osb = sb_fwd(proj, row(sb_q_norm[l]), row(sb_k_norm[l]))
        ypre = conv_fwd(proj, conv_w[l])
        al, dtb = _pad_lanes(row(gdn_a_log[l])), _pad_lanes(row(gdn_dt_bias[l]))
        ogd, gst = gdn_fwd(ypre, small, proj, al, dtb, row(gdn_out_norm[l]))
        ohg, hst = hgrn_fwd(proj, row(lbounds[l]), row(hgrn_out_norm[l]))
        h_next = merge_fwd(osb, proj, ogd, ohg, wbr[l], wout[l], h)
        saved.append((h, proj, small, xnt, osb, ypre, ogd, gst, ohg, hst, al, dtb))
        h = h_next

    dh, loss_row = loss_head(h, loss_target[0])

    gw_main, gw_small, gw_br, gw_out, g_conv_w = [None] * DEPTH, [None] * DEPTH, [None] * DEPTH, [None] * DEPTH, [None] * DEPTH
    g_norm, g_sbq, g_sbk, g_al, g_dt, g_gon, g_lb, g_hon = ([None] * DEPTH for _ in range(8))
    for l in reversed(range(DEPTH)):
        h_l, proj, small, xnt, osb, ypre, ogd, gst, ohg, hst, al, dtb = saved[l]
        d_osb, d_sbz, d_ogd, d_ohg, d_mix, gw_out[l], gw_br[l] = merge_bwd(osb, proj, ogd, ohg, wbr[l], wbr_t[l], wout_t[l], dh)
        d_hg, g_lb[l], g_hon[l] = hgrn_bwd(proj, row(lbounds[l]), row(hgrn_out_norm[l]), hst, d_ohg)
        d_ypre, d_small, d_gz, g_al[l], g_dt[l], g_gon[l] = gdn_bwd(ypre, small, proj, al, dtb, row(gdn_out_norm[l]), gst, d_ogd)
        d_gqkv, g_conv_w[l] = conv_bwd(proj, conv_w[l], d_ypre)
        d_q, d_k, d_v, g_sbq[l], g_sbk[l] = sb_bwd(proj, row(sb_q_norm[l]), row(sb_k_norm[l]), osb, d_osb)
        dproj = jnp.concatenate([d_q, d_k, d_v, d_sbz, d_gqkv, d_gz, d_hg, d_mix], axis=1)
        gw_main[l], gw_small[l] = inproj_bwd_w(xnt, dproj, d_small)
        dh, g_norm[l] = inproj_bwd_x(dproj, d_small, wt_main[l], wt_small[l], h_l, row(norm_w[l]), dh)

    gw_main, gw_small = jnp.stack(gw_main), jnp.stack(gw_small)
    gw_in = jnp.concatenate([gw_main[..., :SMALL_OFF], gw_small[..., :8], gw_main[..., SMALL_OFF:]], axis=-1)
    d_lbl = lb_vjp(jnp.concatenate(g_lb, axis=0))[0]
    cat = lambda rows: jnp.concatenate(rows, axis=0)
    pack = _pack(cat(g_norm), cat(g_sbq), cat(g_sbk), cat(g_al)[:, :N_HEADS], cat(g_dt)[:, :N_HEADS], cat(g_gon),
                 d_lbl, cat(g_hon), loss_row)
    g_meta_full = dh[PAD_FRONT:FRONT]
    r_win, r_wbr, r_wout, r_meta, r_conv, r_pack = exchange(
        [_shard_cols(gw_in), _shard_cols(jnp.stack(gw_br)), jnp.swapaxes(jnp.stack(gw_out).reshape(DEPTH, N_DEV, HEAD, D_MODEL), 0, 1),
         _shard_cols(g_meta_full), _shard_cols(jnp.stack(g_conv_w)), pack],
        [True, True, True, True, True, False], "exchange_grads")

    def upd(parts, w, m, v, rows, name):
        shp = w.shape
        two = (-1, shp[-1])
        outs = adamw(parts.reshape((N_DEV,) + w.reshape(two).shape), w.reshape(two), m.reshape(two), v.reshape(two), rows, name)
        return [o.reshape(shp) for o in outs]

    res = {}
    res["w_in"] = upd(r_win, w_in, m_w_in, v_w_in, 256, "adamw_w_in")
    res["w_branch"] = upd(r_wbr, w_branch, m_w_branch, v_w_branch, 1024, "adamw_w_branch")
    res["w_out"] = upd(r_wout, w_out, m_w_out, v_w_out, 256, "adamw_w_out")
    res["meta_tokens"] = upd(r_meta, meta_tokens, m_meta_tokens, v_meta_tokens, 16, "adamw_meta")
    res["gdn_conv_w"] = upd(r_conv, gdn_conv_w, m_gdn_conv_w, v_gdn_conv_w, 16, "adamw_conv")
    zero_row = jnp.zeros((1, 128), f32)
    w_pack = _pack(norm_w, sb_q_norm, sb_k_norm, gdn_a_log, gdn_dt_bias, gdn_out_norm, hgrn_lb_logits, hgrn_out_norm, zero_row)
    m_pack = _pack(m_norm_w, m_sb_q_norm, m_sb_k_norm, m_gdn_a_log, m_gdn_dt_bias, m_gdn_out_norm, m_hgrn_lb_logits, m_hgrn_out_norm, zero_row)
    v_pack = _pack(v_norm_w, v_sb_q_norm, v_sb_k_norm, v_gdn_a_log, v_gdn_dt_bias, v_gdn_out_norm, v_hgrn_lb_logits, v_hgrn_out_norm, zero_row)
    packed = [_unpack(o) for o in adamw(r_pack, w_pack, m_pack, v_pack, PACK_ROWS, "adamw_replicated")]
    for name in ("norm_w", "sb_q_norm", "sb_k_norm", "gdn_a_log", "gdn_dt_bias", "gdn_out_norm", "hgrn_lb_logits", "hgrn_out_norm"):
        res[name] = [p[name] for p in packed]
    loss = packed[0]["loss"]
    grad_x = dh[FRONT:][None]

    order = ["meta_tokens", "norm_w", "w_in", "sb_q_norm", "sb_k_norm", "gdn_conv_w", "gdn_a_log", "gdn_dt_bias",
             "gdn_out_norm", "hgrn_lb_logits", "hgrn_out_norm", "w_branch", "w_out"]
    return (loss, grad_x, *[res[n][0] for n in order], *[res[n][1] for n in order],
            *[res[n][2] for n in order], *[res[n][3] for n in order])
```

```python
import functools

import numpy as np
import jax
import jax.numpy as jnp
from jax import lax
from jax.experimental import pallas as pl
from jax.experimental.pallas import tpu as pltpu

f32 = jnp.float32
bf16 = jnp.bfloat16

D_MODEL = 1024
BRANCH = 512
HEAD = 128
N_HEADS = 4
CHUNK = 64
SB_BLOCK = 128
N_META = 16
FRONT = 128
PAD_FRONT = 112
EPS = 1e-6
DEPTH = 4
N_DEV = 8
N_IN = 9224
N_MAIN = 9216
N_SMALL = 128
SMALL_OFF = 4096
C_SBQ, C_SBK, C_SBV, C_SBZ = 0, 512, 1024, 1536
C_GQKV, C_GZ = 2048, 3584
C_HQ, C_HF, C_HI, C_HZ = 4096, 4608, 5120, 5632
C_MIX = 6144

ADAM_LR, ADAM_B1, ADAM_B2, ADAM_EPS, ADAM_WD, ADAM_STEP = 0.001, 0.9, 0.999, 1e-08, 0.01, 10

VMEM_LIMIT = 56 * 1024 * 1024
MESH = pl.DeviceIdType.MESH

NN = ((1,), (0,))
NT = ((1,), (1,))
TN = ((0,), (0,))


def _dot(a, b, dims=NN):
    return lax.dot_general(a.astype(bf16), b.astype(bf16), (dims, ((), ())), preferred_element_type=f32)


@jax.custom_vjp
def mm(a, b):
    return _dot(a, b, NN)


mm.defvjp(lambda a, b: (_dot(a, b, NN), (a, b)),
          lambda r, g: (_dot(g, r[1], NT), _dot(r[0], g, TN)))


@jax.custom_vjp
def mm_nt(a, b):
    return _dot(a, b, NT)


mm_nt.defvjp(lambda a, b: (_dot(a, b, NT), (a, b)),
             lambda r, g: (_dot(g, r[1], NN), _dot(g, r[0], TN)))


@jax.custom_vjp
def mm_tn(a, b):
    return _dot(a, b, TN)


mm_tn.defvjp(lambda a, b: (_dot(a, b, TN), (a, b)),
             lambda r, g: (_dot(r[1], g, NT), _dot(r[0], g, NN)))


def _split2(x):
    hi = x.astype(bf16)
    lo = (x - hi.astype(f32)).astype(bf16)
    return hi, lo


def _cdot(c, x, dims):
    hi, lo = _split2(x)
    return (lax.dot_general(c, hi, (dims, ((), ())), preferred_element_type=f32)
            + lax.dot_general(c, lo, (dims, ((), ())), preferred_element_type=f32))


@jax.custom_vjp
def cmm(c, x):
    return _cdot(c, x, NN)


cmm.defvjp(lambda c, x: (_cdot(c, x, NN), c),
           lambda c, g: (jnp.zeros_like(c), _cdot(c, g, TN)))


def _sigmoid(x):
    return jax.nn.sigmoid(x)


def _silu(x):
    return x * jax.nn.sigmoid(x)


def _softplus(x):
    return jnp.maximum(x, 0.0) + jnp.log(1.0 + jnp.exp(-jnp.abs(x)))


def _rms(x, w):
    return x * lax.rsqrt(jnp.mean(x * x, axis=-1, keepdims=True) + EPS) * w


def _cparams(sem=None):
    return pltpu.CompilerParams(dimension_semantics=sem, vmem_limit_bytes=VMEM_LIMIT)


TM_IN = 640
TN_IN = 1024


def inproj_fwd(h, nw, w_main, w_small):
    t = h.shape[0]

    def body(h_ref, nw_ref, w_ref, ws_ref, proj_ref, small_ref, xn_ref, xnt_ref):
        @pl.when(pl.program_id(1) == 0)
        def _():
            xn = _rms(h_ref[...], nw_ref[...])
            xn_ref[...] = xn.astype(bf16)
            xnt_ref[...] = jnp.transpose(xn).astype(bf16)
            small_ref[...] = _dot(xn, ws_ref[...])

        proj_ref[...] = jnp.dot(xn_ref[...], w_ref[...], preferred_element_type=f32)

    return pl.pallas_call(
        body, grid=(t // TM_IN, N_MAIN // TN_IN),
        in_specs=[pl.BlockSpec((TM_IN, D_MODEL), lambda i, j: (i, 0)),
                  pl.BlockSpec((1, D_MODEL), lambda i, j: (0, 0)),
                  pl.BlockSpec((D_MODEL, TN_IN), lambda i, j: (0, j)),
                  pl.BlockSpec((D_MODEL, N_SMALL), lambda i, j: (0, 0))],
        out_specs=[pl.BlockSpec((TM_IN, TN_IN), lambda i, j: (i, j)),
                   pl.BlockSpec((TM_IN, N_SMALL), lambda i, j: (i, 0)),
                   pl.BlockSpec((TM_IN, D_MODEL), lambda i, j: (i, 0)),
                   pl.BlockSpec((D_MODEL, TM_IN), lambda i, j: (0, i))],
        out_shape=[jax.ShapeDtypeStruct((t, N_MAIN), f32), jax.ShapeDtypeStruct((t, N_SMALL), f32),
                   jax.ShapeDtypeStruct((t, D_MODEL), bf16), jax.ShapeDtypeStruct((D_MODEL, t), bf16)],
        compiler_params=_cparams(("arbitrary", "arbitrary")), name="inproj_fwd")(h, nw, w_main, w_small)


def inproj_bwd_x(dproj, dsmall, wt_main, wt_small, h, nw, dh_out):
    t = h.shape[0]
    nk = N_MAIN // TN_IN

    def body(dp_ref, ds_ref, wt_ref, wts_ref, h_ref, nw_ref, dho_ref, dhi_ref, dnw_ref, acc):
        i, k = pl.program_id(0), pl.program_id(1)

        @pl.when(k == 0)
        def _():
            acc[...] = _dot(ds_ref[...], wts_ref[...])

        acc[...] += _dot(dp_ref[...], wt_ref[...])

        @pl.when(k == nk - 1)
        def _():
            x = h_ref[...]
            r = lax.rsqrt(jnp.mean(x * x, axis=-1, keepdims=True) + EPS)
            xh = x * r
            dxn = acc[...]
            dxh = dxn * nw_ref[...]
            dhi_ref[...] = dho_ref[...] + r * (dxh - xh * jnp.mean(dxh * xh, axis=-1, keepdims=True))
            part = jnp.sum(dxn * xh, axis=0, keepdims=True)

            @pl.when(i == 0)
            def _():
                dnw_ref[...] = part

            @pl.when(i > 0)
            def _():
                dnw_ref[...] += part

    return pl.pallas_call(
        body, grid=(t // TM_IN, nk),
        in_specs=[pl.BlockSpec((TM_IN, TN_IN), lambda i, k: (i, k)),
                  pl.BlockSpec((TM_IN, N_SMALL), lambda i, k: (i, 0)),
                  pl.BlockSpec((TN_IN, D_MODEL), lambda i, k: (k, 0)),
                  pl.BlockSpec((N_SMALL, D_MODEL), lambda i, k: (0, 0)),
                  pl.BlockSpec((TM_IN, D_MODEL), lambda i, k: (i, 0)),
                  pl.BlockSpec((1, D_MODEL), lambda i, k: (0, 0)),
                  pl.BlockSpec((TM_IN, D_MODEL), lambda i, k: (i, 0))],
        out_specs=[pl.BlockSpec((TM_IN, D_MODEL), lambda i, k: (i, 0)),
                   pl.BlockSpec((1, D_MODEL), lambda i, k: (0, 0))],
        out_shape=[jax.ShapeDtypeStruct((t, D_MODEL), f32), jax.ShapeDtypeStruct((1, D_MODEL), f32)],
        scratch_shapes=[pltpu.VMEM((TM_IN, D_MODEL), f32)],
        compiler_params=_cparams(("arbitrary", "arbitrary")), name="inproj_bwd_x")(
            dproj, dsmall, wt_main, wt_small, h, nw, dh_out)


def inproj_bwd_w(xnt, dproj, dsmall):
    t = xnt.shape[1]
    nt = t // TM_IN

    def body(xnt_ref, dp_ref, ds_ref, dw_ref, dws_ref):
        n, s = pl.program_id(0), pl.program_id(1)
        part = _dot(xnt_ref[...], dp_ref[...])

        @pl.when(s == 0)
        def _():
            dw_ref[...] = part

        @pl.when(s > 0)
        def _():
            dw_ref[...] += part

        @pl.when(n == 0)
        def _():
            ps = _dot(xnt_ref[...], ds_ref[...])

            @pl.when(s == 0)
            def _():
                dws_ref[...] = ps

            @pl.when(s > 0)
            def _():
                dws_ref[...] += ps

    return pl.pallas_call(
        body, grid=(N_MAIN // TN_IN, nt),
        in_specs=[pl.BlockSpec((D_MODEL, TM_IN), lambda n, s: (0, s)),
                  pl.BlockSpec((TM_IN, TN_IN), lambda n, s: (s, n)),
                  pl.BlockSpec((TM_IN, N_SMALL), lambda n, s: (s, 0))],
        out_specs=[pl.BlockSpec((D_MODEL, TN_IN), lambda n, s: (0, n)),
                   pl.BlockSpec((D_MODEL, N_SMALL), lambda n, s: (0, 0))],
        out_shape=[jax.ShapeDtypeStruct((D_MODEL, N_MAIN), f32), jax.ShapeDtypeStruct((D_MODEL, N_SMALL), f32)],
        compiler_params=_cparams(("arbitrary", "arbitrary")), name="inproj_bwd_w")(xnt, dproj, dsmall)


SB_SCALE = HEAD ** -0.5


SB_SUB = 4
SB_KS = SB_SUB * SB_BLOCK


def _sb_padded(t):
    return -(-t // SB_KS) * SB_KS


def _sb_prep(k_ref, v_ref, kw_ref, kn_scr, vb_scr, nb):
    def prep(b, c):
        rows = pl.ds(pl.multiple_of(b * SB_BLOCK, SB_BLOCK), SB_BLOCK)
        kn_scr[rows, :] = _rms(k_ref[rows, :], kw_ref[...]).astype(bf16)
        vb_scr[rows, :] = v_ref[rows, :].astype(bf16)
        return c

    lax.fori_loop(0, nb, prep, 0)
    pad = kn_scr.shape[0] - nb * SB_BLOCK
    if pad:
        kn_scr[nb * SB_BLOCK:, :] = jnp.zeros((pad, HEAD), bf16)
        vb_scr[nb * SB_BLOCK:, :] = jnp.zeros((pad, HEAD), bf16)


def _tri_ext(cmp):
    r = lax.broadcasted_iota(jnp.int32, (SB_BLOCK, 2 * SB_BLOCK), 0)
    c = lax.broadcasted_iota(jnp.int32, (SB_BLOCK, 2 * SB_BLOCK), 1)
    return jnp.where((c >= SB_BLOCK) | cmp(r, c), 1.0, 0.0).astype(bf16)


def _sb_suffix(x, carry, tri_ext):
    hi, lo = _split2(x)
    parts = [p[:, c * SB_BLOCK:(c + 1) * SB_BLOCK] for p in (hi, lo) for c in range(SB_SUB)]
    w = jnp.dot(jnp.concatenate(parts, axis=0), tri_ext, preferred_element_type=f32)
    outs = [None] * SB_SUB
    for c in reversed(range(SB_SUB)):
        blk = w[c * SB_BLOCK:(c + 1) * SB_BLOCK] + w[(SB_SUB + c) * SB_BLOCK:(SB_SUB + c + 1) * SB_BLOCK]
        outs[c] = carry + blk[:, :SB_BLOCK]
        carry = carry + blk[:, SB_BLOCK:]
    return jnp.concatenate(outs, axis=1), carry


def _sb_scores(qn, kt, i, jb, masked):
    z = lax.dot_general(qn, kt, (NT, ((), ())), preferred_element_type=f32) * SB_SCALE
    lsz = jnp.minimum(z, 0.0) - jnp.log(1.0 + jnp.exp(-jnp.abs(z)))
    lk = lsz - z
    mask = None
    if masked:
        t_idx = i * SB_BLOCK + lax.broadcasted_iota(jnp.int32, (SB_BLOCK, SB_KS), 0)
        s_idx = jb * SB_KS + lax.broadcasted_iota(jnp.int32, (SB_BLOCK, SB_KS), 1)
        mask = (s_idx < t_idx) & (s_idx >= PAD_FRONT)
        lk = jnp.where(mask, lk, 0.0)
    return mask, lsz, lk


SB_DEAD = -104.0


def _sb_walk(i, tile, carry):
    n = i // SB_SUB + 1
    live = lambda c: jnp.max(c[1]) > SB_DEAD
    carry = tile(n - 1, carry, True)
    _, carry = lax.while_loop(lambda st: (st[0] >= 1) & live(st[1]),
                              lambda st: (st[0] - 1, tile(st[0], st[1], False)), (n - 2, carry))
    return lax.cond((n >= 2) & live(carry), lambda c: tile(0, c, True), lambda c: c, carry)


def sb_fwd(proj, qw, kw):
    t = proj.shape[0]
    nb = t // SB_BLOCK

    def body(q_ref, k_ref, v_ref, qw_ref, kw_ref, o_ref, kn_scr, vb_scr):
        i = pl.program_id(1)

        @pl.when(i == 0)
        def _():
            _sb_prep(k_ref, v_ref, kw_ref, kn_scr, vb_scr, nb)

        qn = _rms(q_ref[...], qw_ref[...]).astype(bf16)
        u_ex = _tri_ext(lambda r, c: r > c)

        def tile(jb, carry, masked):
            acc, r_carry = carry
            rows = pl.ds(pl.multiple_of(jb * SB_KS, SB_KS), SB_KS)
            mask, lsz, lk = _sb_scores(qn, kn_scr[rows, :], i, jb, masked)
            passed, r_carry = _sb_suffix(lk, r_carry, u_ex)
            a = jnp.exp(lsz + passed)
            if masked:
                a = jnp.where(mask, a, 0.0)
            a_hi, a_lo = _split2(a)
            both = jnp.dot(jnp.concatenate([a_hi, a_lo], axis=0), vb_scr[rows, :], preferred_element_type=f32)
            return acc + (both[:SB_BLOCK] + both[SB_BLOCK:]), r_carry

        zeros = jnp.zeros((SB_BLOCK, HEAD), f32)
        acc, _ = _sb_walk(i, tile, (zeros, zeros))
        o_ref[...] = acc

    cb = C_SBK // HEAD
    vb = C_SBV // HEAD
    return pl.pallas_call(
        body, grid=(N_HEADS, nb),
        in_specs=[pl.BlockSpec((SB_BLOCK, HEAD), lambda h, i: (i, h)),
                  pl.BlockSpec((t, HEAD), lambda h, i: (0, cb + h)),
                  pl.BlockSpec((t, HEAD), lambda h, i: (0, vb + h)),
                  pl.BlockSpec((1, HEAD), lambda h, i: (0, 0)),
                  pl.BlockSpec((1, HEAD), lambda h, i: (0, 0))],
        out_specs=pl.BlockSpec((SB_BLOCK, HEAD), lambda h, i: (i, h)),
        out_shape=jax.ShapeDtypeStruct((t, BRANCH), f32),
        scratch_shapes=[pltpu.VMEM((_sb_padded(t), HEAD), bf16), pltpu.VMEM((_sb_padded(t), HEAD), bf16)],
        compiler_params=_cparams(("arbitrary", "arbitrary")), name="sb_fwd")(proj, proj, proj, qw, kw)


def sb_bwd(proj, qw, kw, o, do):
    t = proj.shape[0]
    nb = t // SB_BLOCK

    def body(q_ref, k_ref, v_ref, qw_ref, kw_ref, o_ref, do_ref, dq_ref, dk_ref, dv_ref, dqw_ref, dkw_ref,
             kn_scr, vb_scr, dk_acc, dv_acc):
        h, i = pl.program_id(0), pl.program_id(1)

        @pl.when(i == 0)
        def _():
            _sb_prep(k_ref, v_ref, kw_ref, kn_scr, vb_scr, nb)
            dk_acc[...] = jnp.zeros_like(dk_acc)
            dv_acc[...] = jnp.zeros_like(dv_acc)

        @pl.when((i == 0) & (h == 0))
        def _():
            dqw_ref[...] = jnp.zeros_like(dqw_ref)
            dkw_ref[...] = jnp.zeros_like(dkw_ref)

        q = q_ref[...]
        rq = lax.rsqrt(jnp.mean(q * q, axis=-1, keepdims=True) + EPS)
        qh = q * rq
        qn = (qh * qw_ref[...]).astype(bf16)
        do_f = do_ref[...]
        dob = do_f.astype(bf16)
        d_row = jnp.sum(dob.astype(f32) * o_ref[...], axis=-1, keepdims=True)
        u_ex = _tri_ext(lambda r, c: r > c)
        u_in = _tri_ext(lambda r, c: r >= c)

        def tile(jb, carry, masked):
            dq, r_carry, f_carry = carry
            rows = pl.ds(pl.multiple_of(jb * SB_KS, SB_KS), SB_KS)
            kt = kn_scr[rows, :]
            vt = vb_scr[rows, :]
            mask, lsz, lk = _sb_scores(qn, kt, i, jb, masked)
            passed, r_carry = _sb_suffix(lk, r_carry, u_ex)
            a = jnp.exp(lsz + passed)
            if masked:
                a = jnp.where(mask, a, 0.0)
            da = lax.dot_general(dob, vt, (NT, ((), ())), preferred_element_type=f32)
            e = a * da
            e_suf, f_carry = _sb_suffix(e, f_carry, u_in)
            sg = jnp.exp(lsz)
            dz = (e * (1.0 - sg) - (d_row - e_suf) * sg) * SB_SCALE
            if masked:
                dz = jnp.where(mask, dz, 0.0)
            dzb = dz.astype(bf16)
            dq = dq + jnp.dot(dzb, kt, preferred_element_type=f32)
            dk_acc[rows, :] += lax.dot_general(dzb, qn, (TN, ((), ())), preferred_element_type=f32)
            dv_acc[rows, :] += lax.dot_general(a.astype(bf16), dob, (TN, ((), ())), preferred_element_type=f32)
            return dq, r_carry, f_carry

        zeros = jnp.zeros((SB_BLOCK, HEAD), f32)
        dqn, _, _ = _sb_walk(i, tile, (zeros, zeros, zeros))
        gq = dqn * qw_ref[...]
        dq_ref[...] = rq * (gq - qh * jnp.mean(gq * qh, axis=-1, keepdims=True))
        dqw_ref[...] += jnp.sum(dqn * qh, axis=0, keepdims=True)

        @pl.when(i == nb - 1)
        def _():
            def fin(b, c):
                rows = pl.ds(pl.multiple_of(b * SB_BLOCK, SB_BLOCK), SB_BLOCK)
                kk = k_ref[rows, :]
                rk = lax.rsqrt(jnp.mean(kk * kk, axis=-1, keepdims=True) + EPS)
                kh = kk * rk
                dkn = dk_acc[rows, :]
                gk = dkn * kw_ref[...]
                dk_ref[rows, :] = rk * (gk - kh * jnp.mean(gk * kh, axis=-1, keepdims=True))
                dv_ref[rows, :] = dv_acc[rows, :]
                dkw_ref[...] += jnp.sum(dkn * kh, axis=0, keepdims=True)
                return c

            lax.fori_loop(0, nb, fin, 0)

    cb = C_SBK // HEAD
    vb = C_SBV // HEAD
    blk = pl.BlockSpec((SB_BLOCK, HEAD), lambda h, i: (i, h))
    full = pl.BlockSpec((t, HEAD), lambda h, i: (0, h))
    wsp = pl.BlockSpec((1, HEAD), lambda h, i: (0, 0))
    return pl.pallas_call(
        body, grid=(N_HEADS, nb),
        in_specs=[blk, pl.BlockSpec((t, HEAD), lambda h, i: (0, cb + h)),
                  pl.BlockSpec((t, HEAD), lambda h, i: (0, vb + h)), wsp, wsp, blk, blk],
        out_specs=[blk, full, full, wsp, wsp],
        out_shape=[jax.ShapeDtypeStruct((t, BRANCH), f32)] * 3 + [jax.ShapeDtypeStruct((1, HEAD), f32)] * 2,
        scratch_shapes=[pltpu.VMEM((_sb_padded(t), HEAD), bf16), pltpu.VMEM((_sb_padded(t), HEAD), bf16),
                        pltpu.VMEM((_sb_padded(t), HEAD), f32), pltpu.VMEM((_sb_padded(t), HEAD), f32)],
        compiler_params=_cparams(("arbitrary", "arbitrary")), name="sb_bwd")(proj, proj, proj, qw, kw, o, do)


TM_CONV = 640
CONV_W = 4
GQKV = 3 * BRANCH


def conv_fwd(proj, cw):
    t = proj.shape[0]
    halo_blocks = TM_CONV // 8
    cb = C_GQKV // GQKV
    del cb

    def body(x0_ref, x1_ref, x2_ref, p0_ref, p1_ref, p2_ref, cw_ref, y_ref):
        i = pl.program_id(0)
        for s, (x_ref, p_ref) in enumerate(((x0_ref, p0_ref), (x1_ref, p1_ref), (x2_ref, p2_ref))):
            prev = jnp.where(i > 0, p_ref[...], 0.0)
            xx = jnp.concatenate([prev, x_ref[...]], axis=0)
            cols = slice(s * BRANCH, (s + 1) * BRANCH)
            y = xx[8:] * cw_ref[CONV_W - 1:CONV_W, cols]
            for k in range(CONV_W - 1):
                y = y + pltpu.roll(xx, CONV_W - 1 - k, 0)[8:] * cw_ref[k:k + 1, cols]
            y_ref[:, cols] = y

    c0 = C_GQKV // BRANCH
    xs = [pl.BlockSpec((TM_CONV, BRANCH), functools.partial(lambda i, s: (i, c0 + s), s=s)) for s in range(3)]
    ps = [pl.BlockSpec((8, BRANCH), functools.partial(lambda i, s: (jnp.maximum(i * halo_blocks - 1, 0), c0 + s), s=s))
          for s in range(3)]
    return pl.pallas_call(
        body, grid=(t // TM_CONV,),
        in_specs=xs + ps + [pl.BlockSpec((CONV_W, GQKV), lambda i: (0, 0))],
        out_specs=pl.BlockSpec((TM_CONV, GQKV), lambda i: (i, 0)),
        out_shape=jax.ShapeDtypeStruct((t, GQKV), f32),
        compiler_params=_cparams(("arbitrary",)), name="conv_fwd")(proj, proj, proj, proj, proj, proj, cw)


def conv_bwd(proj, cw, dy):
    t = proj.shape[0]
    nt = t // TM_CONV
    halo_blocks = TM_CONV // 8

    def body(x0_ref, x1_ref, x2_ref, p0_ref, p1_ref, p2_ref, cw_ref, dy_ref, dyn_ref, dx_ref, dw_ref):
        i = pl.program_id(0)

        @pl.when(i == 0)
        def _():
            dw_ref[...] = jnp.zeros_like(dw_ref)

        nxt = jnp.where(i < nt - 1, dyn_ref[...], 0.0)
        dyy = jnp.concatenate([dy_ref[...], nxt], axis=0)
        n_rows = TM_CONV + 8
        dx = dyy[:TM_CONV] * cw_ref[CONV_W - 1:CONV_W, :]
        for k in range(CONV_W - 1):
            sh = CONV_W - 1 - k
            dx = dx + pltpu.roll(dyy, n_rows - sh, 0)[:TM_CONV] * cw_ref[k:k + 1, :]
        dx_ref[...] = dx
        dy_c = dy_ref[...]
        for s, (x_ref, p_ref) in enumerate(((x0_ref, p0_ref), (x1_ref, p1_ref), (x2_ref, p2_ref))):
            prev = jnp.where(i > 0, p_ref[...], 0.0)
            xx = jnp.concatenate([prev, x_ref[...]], axis=0)
            cols = slice(s * BRANCH, (s + 1) * BRANCH)
            for k in range(CONV_W):
                sh = CONV_W - 1 - k
                xs = xx[8:] if sh == 0 else pltpu.roll(xx, sh, 0)[8:]
                dw_ref[k:k + 1, cols] += jnp.sum(xs * dy_c[:, cols], axis=0, keepdims=True)

    c0 = C_GQKV // BRANCH
    xs = [pl.BlockSpec((TM_CONV, BRANCH), functools.partial(lambda i, s: (i, c0 + s), s=s)) for s in range(3)]
    ps = [pl.BlockSpec((8, BRANCH), functools.partial(lambda i, s: (jnp.maximum(i * halo_blocks - 1, 0), c0 + s), s=s))
          for s in range(3)]
    return pl.pallas_call(
        body, grid=(nt,),
        in_specs=xs + ps + [pl.BlockSpec((CONV_W, GQKV), lambda i: (0, 0)),
                            pl.BlockSpec((TM_CONV, GQKV), lambda i: (i, 0)),
                            pl.BlockSpec((8, GQKV), lambda i: (jnp.minimum((i + 1) * halo_blocks, nt * halo_blocks - 1), 0))],
        out_specs=[pl.BlockSpec((TM_CONV, GQKV), lambda i: (i, 0)), pl.BlockSpec((CONV_W, GQKV), lambda i: (0, 0))],
        out_shape=[jax.ShapeDtypeStruct((t, GQKV), f32), jax.ShapeDtypeStruct((CONV_W, GQKV), f32)],
        compiler_params=_cparams(("arbitrary",)), name="conv_bwd")(proj, proj, proj, proj, proj, proj, cw, dy, dy)


def _iota2(n, m, d):
    return lax.broadcasted_iota(jnp.int32, (n, m), d)


def _lane_pick(row_or_mat, idx):
    lanes = lax.broadcasted_iota(jnp.int32, row_or_mat.shape, row_or_mat.ndim - 1)
    return jnp.sum(jnp.where(lanes == idx, row_or_mat, 0.0), axis=-1, keepdims=True)


def _cumsum_consts():
    i = np.arange(CHUNK)
    incl = i[None, :] <= i[:, None]
    suf = i[None, :] > i[:, None]
    return np.concatenate([incl, suf], 0).astype(np.float32)


HG_LEVELS = (64, 32, 16, 8, 4, 2)


def _hgrn_consts():
    i = np.arange(CHUNK)
    rows = [i[None, :] <= i[:, None], i[None, :] > i[:, None]]
    for b in HG_LEVELS:
        ref = (i // b) * b + b // 2 - 1
        second = (i % b) >= b // 2
        rows.append((i[None, :] > ref[:, None]) & (i[None, :] <= i[:, None]) & second[:, None])
        rows.append((i[None, :] > i[:, None]) & (i[None, :] <= ref[:, None]) & (~second)[:, None])
    return np.concatenate(rows, 0).astype(np.float32)


HC = N_HEADS * CHUNK
BATCH0 = ((0,), (0,))


def _bdot(a, b, contract):
    return lax.dot_general(a.astype(bf16), b.astype(bf16), (contract, BATCH0), preferred_element_type=f32)


B_NN = ((2,), (1,))
B_NT = ((2,), (2,))
B_TN = ((1,), (1,))


@jax.custom_vjp
def bmm(a, b):
    return _bdot(a, b, B_NN)


bmm.defvjp(lambda a, b: (_bdot(a, b, B_NN), (a, b)),
           lambda r, g: (_bdot(g, r[1], B_NT), _bdot(r[0], g, B_TN)))


@jax.custom_vjp
def bmm_nt(a, b):
    return _bdot(a, b, B_NT)


bmm_nt.defvjp(lambda a, b: (_bdot(a, b, B_NT), (a, b)),
              lambda r, g: (_bdot(g, r[1], B_NN), _bdot(g, r[0], B_TN)))


@jax.custom_vjp
def bmm_tn(a, b):
    return _bdot(a, b, B_TN)


bmm_tn.defvjp(lambda a, b: (_bdot(a, b, B_TN), (a, b)),
              lambda r, g: (_bdot(r[1], g, B_NT), _bdot(r[0], g, B_NN)))


def _stack_heads(x):
    return jnp.concatenate([x[:, h * HEAD:(h + 1) * HEAD] for h in range(N_HEADS)], axis=0)


def _unstack_heads(x):
    return jnp.concatenate([x[h * CHUNK:(h + 1) * CHUNK] for h in range(N_HEADS)], axis=1)


def _gdn_chunk(state, ypre, small, gz, a_log, dt_b, on_w, c2, vm):
    r = _iota2(HC, HC, 0)
    c = _iota2(HC, HC, 1)
    same_head = (r >> 6) == (c >> 6)
    causal = same_head & (r >= c)
    strict = same_head & (r > c)
    q = _silu(_stack_heads(ypre[:, :BRANCH]))
    k = _silu(_stack_heads(ypre[:, BRANCH:2 * BRANCH]))
    v = _silu(_stack_heads(ypre[:, 2 * BRANCH:]))
    q = q * lax.rsqrt(jnp.sum(q * q, axis=-1, keepdims=True) + EPS) * (HEAD ** -0.5)
    k = k * lax.rsqrt(jnp.sum(k * k, axis=-1, keepdims=True) + EPS)
    col = lambda f: jnp.concatenate([f(h) for h in range(N_HEADS)], axis=0)
    chunk_col = lambda x: jnp.broadcast_to(x, (CHUNK, 1))
    beta = _sigmoid(col(lambda h: _lane_pick(small, h))) * col(lambda h: vm)
    g = (-jnp.exp(col(lambda h: chunk_col(_lane_pick(a_log, h))))
         * _softplus(col(lambda h: _lane_pick(small, N_HEADS + h)) + col(lambda h: chunk_col(_lane_pick(dt_b, h)))))
    g_l = _unstack_heads(jnp.broadcast_to(g, (HC, HEAD)))
    e2 = cmm(c2, g_l)
    gc = _stack_heads(e2[:CHUNK])
    gsuf = _stack_heads(e2[CHUNK:])
    g_row = jnp.broadcast_to(jnp.transpose(gc)[0:1, :], (HC, HC))
    g_col = jnp.concatenate([gc, gc], axis=1)
    dec = jnp.where(causal, jnp.exp(jnp.minimum(g_col - g_row, 0.0)), 0.0)
    kb = k * beta
    m = jnp.where(strict, mm_nt(kb, k) * dec, 0.0)
    x = jnp.concatenate([v * beta, kb * jnp.exp(gc)], axis=1)
    x = x - mm(m, x)
    p = m
    for _ in range(5):
        p = mm(p, p)
        x = x + mm(p, x)
    u, w = x[:, :HEAD], x[:, HEAD:]
    aqk = jnp.where(causal, mm_nt(q, k) * dec, 0.0)
    tot = jnp.sum(g_l, axis=0, keepdims=True)
    g_last = jnp.exp(jnp.stack([tot[:, h * HEAD:(h + 1) * HEAD] for h in range(N_HEADS)], axis=0))
    per_head = lambda a: a.reshape(N_HEADS, CHUNK, HEAD)
    v_new = u - bmm(per_head(w), state).reshape(HC, HEAD)
    o = bmm(per_head(q * jnp.exp(gc)), state).reshape(HC, HEAD) + mm(aqk, v_new)
    new_state = state * g_last + bmm_tn(per_head(k * jnp.exp(gsuf)), per_head(v_new))
    out = _unstack_heads(_rms(o, on_w)) * _silu(gz)
    return new_state, out


def _hgrn_chunk(state, hq, hf, hi, hz, lb, on_w, cm, vm):
    r = _iota2(HC, HC, 0)
    c = _iota2(HC, HC, 1)
    forget = lb + (1.0 - lb) * _sigmoid(hf)
    g_l = jnp.log(forget)
    e = cmm(cm, g_l)
    q = _stack_heads(_silu(hq))
    k = _stack_heads((1.0 - lb) * _sigmoid(-hf))
    v = _stack_heads(hi * vm)
    sect = lambda n: _stack_heads(e[n * CHUNK:(n + 1) * CHUNK])
    gc, gsuf = sect(0), sect(1)
    per_head = lambda a: a.reshape(N_HEADS, CHUNK, HEAD)
    o = bmm_nt(per_head(q * jnp.exp(gc)), state).reshape(HC, HEAD)
    a = jnp.where(r == c, jnp.sum(q * k, axis=-1, keepdims=True), 0.0)
    for li, b in enumerate(HG_LEVELS):
        sh = b.bit_length() - 1
        pair = ((r >> sh) == (c >> sh)) & ((r & (b - 1)) >= b // 2) & ((c & (b - 1)) < b // 2)
        a = a + jnp.where(pair, mm_nt(q * jnp.exp(sect(2 + 2 * li)), k * jnp.exp(sect(3 + 2 * li))), 0.0)
    o = o + mm(a, v)
    tot = jnp.sum(g_l, axis=0, keepdims=True)
    g_end = jnp.exp(jnp.stack([tot[:, h * HEAD:(h + 1) * HEAD] for h in range(N_HEADS)], axis=0))
    new_state = state * g_end + bmm_tn(per_head(v), per_head(k * jnp.exp(gsuf)))
    out = _unstack_heads(_rms(o, on_w)) * _silu(hz)
    return new_state, out


def _vmask(chunk_idx):
    rows = chunk_idx * CHUNK + lax.broadcasted_iota(jnp.int32, (CHUNK, 1), 0)
    return jnp.where(rows >= PAD_FRONT, 1.0, 0.0)


def _row(n):
    return pl.BlockSpec((1, n), lambda i: (0, 0))


def gdn_fwd(ypre, small, proj, a_log, dt_b, on_w):
    t = ypre.shape[0]
    nc = t // CHUNK
    c2 = jnp.asarray(_cumsum_consts(), bf16)

    def body(y_ref, s_ref, z_ref, al_ref, dt_ref, on_ref, c2_ref, o_ref, st_ref, state):
        i = pl.program_id(0)

        @pl.when(i == 0)
        def _():
            state[...] = jnp.zeros_like(state)

        s_in = state[...]
        st_ref[0] = s_in
        s_new, out = _gdn_chunk(s_in, y_ref[...], s_ref[...], z_ref[...], al_ref[...], dt_ref[...], on_ref[...],
                                c2_ref[...], _vmask(i))
        state[...] = s_new
        o_ref[...] = out

    return pl.pallas_call(
        body, grid=(nc,),
        in_specs=[pl.BlockSpec((CHUNK, GQKV), lambda i: (i, 0)), pl.BlockSpec((CHUNK, N_SMALL), lambda i: (i, 0)),
                  pl.BlockSpec((CHUNK, BRANCH), lambda i: (i, C_GZ // BRANCH)), _row(128), _row(128), _row(128),
                  pl.BlockSpec((2 * CHUNK, CHUNK), lambda i: (0, 0))],
        out_specs=[pl.BlockSpec((CHUNK, BRANCH), lambda i: (i, 0)),
                   pl.BlockSpec((1, N_HEADS, HEAD, HEAD), lambda i: (i, 0, 0, 0))],
        out_shape=[jax.ShapeDtypeStruct((t, BRANCH), f32), jax.ShapeDtypeStruct((nc, N_HEADS, HEAD, HEAD), f32)],
        scratch_shapes=[pltpu.VMEM((N_HEADS, HEAD, HEAD), f32)],
        compiler_params=_cparams(("arbitrary",)), name="gdn_fwd")(ypre, small, proj, a_log, dt_b, on_w, c2)


def gdn_bwd(ypre, small, proj, a_log, dt_b, on_w, states, d_out):
    t = ypre.shape[0]
    nc = t // CHUNK
    c2 = jnp.asarray(_cumsum_consts(), bf16)

    def body(y_ref, s_ref, z_ref, al_ref, dt_ref, on_ref, c2_ref, st_ref, do_ref,
             dy_ref, ds_ref, dz_ref, dal_ref, ddt_ref, don_ref, dstate):
        i = pl.program_id(0)

        @pl.when(i == 0)
        def _():
            dstate[...] = jnp.zeros_like(dstate)
            dal_ref[...] = jnp.zeros_like(dal_ref)
            ddt_ref[...] = jnp.zeros_like(ddt_ref)
            don_ref[...] = jnp.zeros_like(don_ref)

        vm = _vmask(nc - 1 - i)
        c2v = c2_ref[...]
        fn = lambda s, y, sm, z, al, dt, on: _gdn_chunk(s, y, sm, z, al, dt, on, c2v, vm)
        _, vjp = jax.vjp(fn, st_ref[0], y_ref[...], s_ref[...], z_ref[...], al_ref[...], dt_ref[...], on_ref[...])
        d_s, d_y, d_sm, d_z, d_al, d_dt, d_on = vjp((dstate[...], do_ref[...]))
        dstate[...] = d_s
        dy_ref[...] = d_y
        ds_ref[...] = d_sm
        dz_ref[...] = d_z
        dal_ref[...] += d_al
        ddt_ref[...] += d_dt
        don_ref[...] += d_on

    rev = lambda i: (nc - 1 - i, 0)
    return pl.pallas_call(
        body, grid=(nc,),
        in_specs=[pl.BlockSpec((CHUNK, GQKV), rev), pl.BlockSpec((CHUNK, N_SMALL), rev),
                  pl.BlockSpec((CHUNK, BRANCH), lambda i: (nc - 1 - i, C_GZ // BRANCH)), _row(128), _row(128), _row(128),
                  pl.BlockSpec((2 * CHUNK, CHUNK), lambda i: (0, 0)),
                  pl.BlockSpec((1, N_HEADS, HEAD, HEAD), lambda i: (nc - 1 - i, 0, 0, 0)),
                  pl.BlockSpec((CHUNK, BRANCH), rev)],
        out_specs=[pl.BlockSpec((CHUNK, GQKV), rev), pl.BlockSpec((CHUNK, N_SMALL), rev),
                   pl.BlockSpec((CHUNK, BRANCH), rev), _row(128), _row(128), _row(128)],
        out_shape=[jax.ShapeDtypeStruct((t, GQKV), f32), jax.ShapeDtypeStruct((t, N_SMALL), f32),
                   jax.ShapeDtypeStruct((t, BRANCH), f32)] + [jax.ShapeDtypeStruct((1, 128), f32)] * 3,
        scratch_shapes=[pltpu.VMEM((N_HEADS, HEAD, HEAD), f32)],
        compiler_params=_cparams(("arbitrary",)), name="gdn_bwd")(
            ypre, small, proj, a_log, dt_b, on_w, c2, states, d_out)


def hgrn_fwd(proj, lb, on_w):
    t = proj.shape[0]
    nc = t // CHUNK
    cm = jnp.asarray(_hgrn_consts(), bf16)
    ncm = cm.shape[0]

    def body(q_ref, f_ref, i_ref, z_ref, lb_ref, on_ref, cm_ref, o_ref, st_ref, state):
        i = pl.program_id(0)

        @pl.when(i == 0)
        def _():
            state[...] = jnp.zeros_like(state)

        s_in = state[...]
        st_ref[0] = s_in
        s_new, out = _hgrn_chunk(s_in, q_ref[...], f_ref[...], i_ref[...], z_ref[...], lb_ref[...], on_ref[...],
                                 cm_ref[...], _vmask(i))
        state[...] = s_new
        o_ref[...] = out

    sec = lambda off: pl.BlockSpec((CHUNK, BRANCH), functools.partial(lambda i, b: (i, b), b=off // BRANCH))
    return pl.pallas_call(
        body, grid=(nc,),
        in_specs=[sec(C_HQ), sec(C_HF), sec(C_HI), sec(C_HZ), _row(BRANCH), _row(128),
                  pl.BlockSpec((ncm, CHUNK), lambda i: (0, 0))],
        out_specs=[pl.BlockSpec((CHUNK, BRANCH), lambda i: (i, 0)),
                   pl.BlockSpec((1, N_HEADS, HEAD, HEAD), lambda i: (i, 0, 0, 0))],
        out_shape=[jax.ShapeDtypeStruct((t, BRANCH), f32), jax.ShapeDtypeStruct((nc, N_HEADS, HEAD, HEAD), f32)],
        scratch_shapes=[pltpu.VMEM((N_HEADS, HEAD, HEAD), f32)],
        compiler_params=_cparams(("arbitrary",)), name="hgrn_fwd")(proj, proj, proj, proj, lb, on_w, cm)


def hgrn_bwd(proj, lb, on_w, states, d_out):
    t = proj.shape[0]
    nc = t // CHUNK
    cm = jnp.asarray(_hgrn_consts(), bf16)
    ncm = cm.shape[0]

    def body(q_ref, f_ref, i_ref, z_ref, lb_ref, on_ref, cm_ref, st_ref, do_ref, dh_ref, dlb_ref, don_ref, dstate):
        i = pl.program_id(0)

        @pl.when(i == 0)
        def _():
            dstate[...] = jnp.zeros_like(dstate)
            dlb_ref[...] = jnp.zeros_like(dlb_ref)
            don_ref[...] = jnp.zeros_like(don_ref)

        vm = _vmask(nc - 1 - i)
        cmv = cm_ref[...]
        fn = lambda s, a, b, c, d, l, on: _hgrn_chunk(s, a, b, c, d, l, on, cmv, vm)
        _, vjp = jax.vjp(fn, st_ref[0], q_ref[...], f_ref[...], i_ref[...], z_ref[...], lb_ref[...], on_ref[...])
        d_s, d_q, d_f, d_i, d_z, d_lb, d_on = vjp((dstate[...], do_ref[...]))
        dstate[...] = d_s
        dh_ref[...] = jnp.concatenate([d_q, d_f, d_i, d_z], axis=1)
        dlb_ref[...] += d_lb
        don_ref[...] += d_on

    rev = lambda i: (nc - 1 - i, 0)
    sec = lambda off: pl.BlockSpec((CHUNK, BRANCH), functools.partial(lambda i, b: (nc - 1 - i, b), b=off // BRANCH))
    return pl.pallas_call(
        body, grid=(nc,),
        in_specs=[sec(C_HQ), sec(C_HF), sec(C_HI), sec(C_HZ), _row(BRANCH), _row(128),
                  pl.BlockSpec((ncm, CHUNK), lambda i: (0, 0)),
                  pl.BlockSpec((1, N_HEADS, HEAD, HEAD), lambda i: (nc - 1 - i, 0, 0, 0)),
                  pl.BlockSpec((CHUNK, BRANCH), rev)],
        out_specs=[pl.BlockSpec((CHUNK, 4 * BRANCH), rev), _row(BRANCH), _row(128)],
        out_shape=[jax.ShapeDtypeStruct((t, 4 * BRANCH), f32), jax.ShapeDtypeStruct((1, BRANCH), f32),
                   jax.ShapeDtypeStruct((1, 128), f32)],
        scratch_shapes=[pltpu.VMEM((N_HEADS, HEAD, HEAD), f32)],
        compiler_params=_cparams(("arbitrary",)), name="hgrn_bwd")(proj, proj, proj, proj, lb, on_w, cm, states, d_out)


TM_MG = 320


def _const_spec(shape):
    nd = len(shape)
    return pl.BlockSpec(shape, lambda i: (0,) * nd, pipeline_mode=pl.Buffered(1))


def merge_fwd(osb, proj, ogd, ohg, wb, wo, h):
    t = h.shape[0]

    def body(osb_ref, sbz_ref, ogd_ref, ohg_ref, mix_ref, wb_ref, wo_ref, h_ref, out_ref):
        a = osb_ref[...] * _silu(sbz_ref[...])
        y = (_sigmoid(mix_ref[:, 0:D_MODEL]) * _dot(a, wb_ref[0])
             + _sigmoid(mix_ref[:, D_MODEL:2 * D_MODEL]) * _dot(ogd_ref[...], wb_ref[1])
             + _sigmoid(mix_ref[:, 2 * D_MODEL:3 * D_MODEL]) * _dot(ohg_ref[...], wb_ref[2]))
        out_ref[...] = h_ref[...] + _dot(y, wo_ref[...])

    br = pl.BlockSpec((TM_MG, BRANCH), lambda i: (i, 0))
    return pl.pallas_call(
        body, grid=(t // TM_MG,),
        in_specs=[br, pl.BlockSpec((TM_MG, BRANCH), lambda i: (i, C_SBZ // BRANCH)), br, br,
                  pl.BlockSpec((TM_MG, 3 * D_MODEL), lambda i: (i, C_MIX // (3 * D_MODEL))),
                  _const_spec((3, BRANCH, D_MODEL)), _const_spec((D_MODEL, D_MODEL)),
                  pl.BlockSpec((TM_MG, D_MODEL), lambda i: (i, 0))],
        out_specs=pl.BlockSpec((TM_MG, D_MODEL), lambda i: (i, 0)),
        out_shape=jax.ShapeDtypeStruct((t, D_MODEL), f32),
        compiler_params=_cparams(("arbitrary",)), name="merge_fwd")(osb, proj, ogd, ohg, proj, wb, wo, h)


def merge_bwd(osb, proj, ogd, ohg, wb, wbt, wot, dh):
    t = dh.shape[0]

    def body(osb_ref, sbz_ref, ogd_ref, ohg_ref, mix_ref, wb_ref, wbt_ref, wot_ref, dh_ref,
             dosb_ref, dsbz_ref, dogd_ref, dohg_ref, dmix_ref, dwo_ref, dwb_ref):
        i = pl.program_id(0)

        @pl.when(i == 0)
        def _():
            dwo_ref[...] = jnp.zeros_like(dwo_ref)
            dwb_ref[...] = jnp.zeros_like(dwb_ref)

        osb = osb_ref[...]
        sbz = sbz_ref[...]
        sgz = _sigmoid(sbz)
        sz = sbz * sgz
        branch_in = (osb * sz, ogd_ref[...], ohg_ref[...])
        dh_v = dh_ref[...]
        dy = _dot(dh_v, wot_ref[...])
        y = jnp.zeros((TM_MG, D_MODEL), f32)
        d_in = []
        for b in range(3):
            p = _dot(branch_in[b], wb_ref[b])
            g = _sigmoid(mix_ref[:, b * D_MODEL:(b + 1) * D_MODEL])
            y = y + g * p
            dp = dy * g
            dmix_ref[:, b * D_MODEL:(b + 1) * D_MODEL] = dy * p * g * (1.0 - g)
            d_in.append(_dot(dp, wbt_ref[b]))
            dwb_ref[b] += _dot(branch_in[b], dp, TN)
        dwo_ref[...] += _dot(y, dh_v, TN)
        dosb_ref[...] = d_in[0] * sz
        dsbz_ref[...] = d_in[0] * osb * (sgz * (1.0 + sbz * (1.0 - sgz)))
        dogd_ref[...] = d_in[1]
        dohg_ref[...] = d_in[2]

    br = pl.BlockSpec((TM_MG, BRANCH), lambda i: (i, 0))
    return pl.pallas_call(
        body, grid=(t // TM_MG,),
        in_specs=[br, pl.BlockSpec((TM_MG, BRANCH), lambda i: (i, C_SBZ // BRANCH)), br, br,
                  pl.BlockSpec((TM_MG, 3 * D_MODEL), lambda i: (i, C_MIX // (3 * D_MODEL))),
                  _const_spec((3, BRANCH, D_MODEL)), _const_spec((3, D_MODEL, BRANCH)), _const_spec((D_MODEL, D_MODEL)),
                  pl.BlockSpec((TM_MG, D_MODEL), lambda i: (i, 0))],
        out_specs=[br, br, br, br, pl.BlockSpec((TM_MG, 3 * D_MODEL), lambda i: (i, 0)),
                   _const_spec((D_MODEL, D_MODEL)), _const_spec((3, BRANCH, D_MODEL))],
        out_shape=[jax.ShapeDtypeStruct((t, BRANCH), f32)] * 4 + [jax.ShapeDtypeStruct((t, 3 * D_MODEL), f32),
                   jax.ShapeDtypeStruct((D_MODEL, D_MODEL), f32), jax.ShapeDtypeStruct((3, BRANCH, D_MODEL), f32)],
        compiler_params=_cparams(("arbitrary",)), name="merge_bwd")(osb, proj, ogd, ohg, proj, wb, wbt, wot, dh)


def loss_head(h, target):
    t = h.shape[0]
    nb = t // SB_BLOCK

    def body(h_ref, t_ref, dh_ref, loss_ref):
        i = pl.program_id(0)

        @pl.when(i == 0)
        def _():
            loss_ref[...] = jnp.zeros_like(loss_ref)
            dh_ref[...] = jnp.zeros_like(dh_ref)

        @pl.when(i > 0)
        def _():
            err = h_ref[...] - t_ref[...]
            dh_ref[...] = err * (1.0 / D_MODEL)
            loss_ref[...] += jnp.broadcast_to(jnp.sum(err * err) * (0.5 / D_MODEL), loss_ref.shape)

    return pl.pallas_call(
        body, grid=(nb,),
        in_specs=[pl.BlockSpec((SB_BLOCK, D_MODEL), lambda i: (i, 0)),
                  pl.BlockSpec((SB_BLOCK, D_MODEL), lambda i: (jnp.maximum(i - 1, 0), 0))],
        out_specs=[pl.BlockSpec((SB_BLOCK, D_MODEL), lambda i: (i, 0)), pl.BlockSpec((1, 128), lambda i: (0, 0))],
        out_shape=[jax.ShapeDtypeStruct((t, D_MODEL), f32), jax.ShapeDtypeStruct((1, 128), f32)],
        compiler_params=_cparams(("arbitrary",)), name="loss_head")(h, target)


def adamw(parts, w, m, v, rows_per_step, name):
    r, c = w.shape
    tr = min(rows_per_step, r)

    def body(p_ref, w_ref, m_ref, v_ref, g_ref, d_ref, nm_ref, nv_ref):
        g = p_ref[0]
        for k in range(1, N_DEV):
            g = g + p_ref[k]
        m_new = ADAM_B1 * m_ref[...] + (1.0 - ADAM_B1) * g
        v_new = ADAM_B2 * v_ref[...] + (1.0 - ADAM_B2) * jnp.square(g)
        m_hat = m_new / (1.0 - ADAM_B1 ** ADAM_STEP)
        v_hat = v_new / (1.0 - ADAM_B2 ** ADAM_STEP)
        g_ref[...] = g
        d_ref[...] = -ADAM_LR * (m_hat / (jnp.sqrt(v_hat) + ADAM_EPS) + ADAM_WD * w_ref[...])
        nm_ref[...] = m_new
        nv_ref[...] = v_new

    blk = pl.BlockSpec((tr, c), lambda i: (i, 0))
    return pl.pallas_call(
        body, grid=(r // tr,),
        in_specs=[pl.BlockSpec((N_DEV, tr, c), lambda i: (0, i, 0)), blk, blk, blk],
        out_specs=[blk] * 4, out_shape=[jax.ShapeDtypeStruct((r, c), f32)] * 4,
        compiler_params=_cparams(("arbitrary",)), name=name)(parts, w, m, v)


def _mesh_pos():
    return lax.axis_index("x"), lax.axis_index("y"), lax.axis_index("c")


def _peer(pos, k):
    x, y, c = pos
    return (1 - x if k & 4 else x, 1 - y if k & 2 else y, 1 - c if k & 1 else c)


def _lin(pos):
    return 4 * pos[0] + 2 * pos[1] + pos[2]


def exchange(srcs, scatter, name):
    n = len(srcs)
    shapes = [s.shape[1:] if sc else s.shape for s, sc in zip(srcs, scatter)]

    def body(*refs):
        src_refs, dst_refs = refs[:n], refs[n:2 * n]
        send_sems, recv_sems, local_sems = refs[2 * n:]
        me = _mesh_pos()
        me_lin = _lin(me)
        sends, recvs, locals_ = [], [], []
        for t in range(n):
            own = src_refs[t].at[me_lin] if scatter[t] else src_refs[t]
            locals_.append(pltpu.make_async_copy(own, dst_refs[t].at[me_lin], local_sems.at[t]))
            for k in range(1, N_DEV):
                peer = _peer(me, k)
                src = src_refs[t].at[_lin(peer)] if scatter[t] else src_refs[t]
                sends.append(pltpu.make_async_remote_copy(
                    src_ref=src, dst_ref=dst_refs[t].at[me_lin], send_sem=send_sems.at[t, k - 1],
                    recv_sem=recv_sems.at[t, k - 1], device_id=peer, device_id_type=MESH))
                recvs.append(pltpu.make_async_remote_copy(
                    src_ref=src, dst_ref=dst_refs[t].at[_lin(peer)], send_sem=send_sems.at[t, k - 1],
                    recv_sem=recv_sems.at[t, k - 1], device_id=peer, device_id_type=MESH))
        for cp in locals_ + sends:
            cp.start()
        for cp in sends:
            cp.wait_send()
        for cp in recvs:
            cp.wait_recv()
        for cp in locals_:
            cp.wait()

    any_spec = pl.BlockSpec(memory_space=pl.ANY)
    return pl.pallas_call(
        body, in_specs=[any_spec] * n, out_specs=[any_spec] * n,
        out_shape=[jax.ShapeDtypeStruct((N_DEV,) + tuple(sh), s.dtype) for sh, s in zip(shapes, srcs)],
        scratch_shapes=[pltpu.SemaphoreType.DMA((n, N_DEV - 1)), pltpu.SemaphoreType.DMA((n, N_DEV - 1)),
                        pltpu.SemaphoreType.DMA((n,))],
        compiler_params=pltpu.CompilerParams(has_side_effects=True), name=name)(*srcs)


PACK_ROWS = 104


def _pad_rows(a, rows):
    return jnp.pad(a, ((0, rows - a.shape[0]), (0, 0)))


def _pad_lanes(a):
    return jnp.pad(a, ((0, 0), (0, 128 - a.shape[1])))


def _pack(norm_w, sbq, sbk, alog, dtb, gon, lbl, hon, loss_row):
    parts = [norm_w.reshape(32, 128), _pad_rows(sbq, 8), _pad_rows(sbk, 8), _pad_rows(_pad_lanes(alog), 8),
             _pad_rows(_pad_lanes(dtb), 8), _pad_rows(gon, 8), lbl.reshape(16, 128), _pad_rows(hon, 8),
             _pad_rows(loss_row, 8)]
    return jnp.concatenate(parts, axis=0)


def _unpack(p):
    return dict(norm_w=p[0:32].reshape(DEPTH, D_MODEL), sb_q_norm=p[32:36], sb_k_norm=p[40:44],
                gdn_a_log=p[48:52, :N_HEADS], gdn_dt_bias=p[56:60, :N_HEADS], gdn_out_norm=p[64:68],
                hgrn_lb_logits=p[72:88].reshape(DEPTH, BRANCH), hgrn_out_norm=p[88:92], loss=p[96, 0])


def _lower_bounds(logits):
    p = jax.nn.softmax(logits, axis=0)
    return jnp.cumsum(p, axis=0) - p[0:1]


def _unshard_cols(g):
    nd = g.ndim
    g = jnp.moveaxis(g, 0, nd - 2)
    return g.reshape(g.shape[:-2] + (N_DEV * g.shape[-1],))


def _shard_cols(a):
    n = a.shape[-1] // N_DEV
    return jnp.moveaxis(a.reshape(a.shape[:-1] + (N_DEV, n)), -2, 0)


def kernel(x, meta_tokens, norm_w, w_in, sb_q_norm, sb_k_norm, gdn_conv_w, gdn_a_log, gdn_dt_bias, gdn_out_norm, hgrn_lb_logits, hgrn_out_norm, w_branch, w_out, loss_target, m_meta_tokens, m_norm_w, m_w_in, m_sb_q_norm, m_sb_k_norm, m_gdn_conv_w, m_gdn_a_log, m_gdn_dt_bias, m_gdn_out_norm, m_hgrn_lb_logits, m_hgrn_out_norm, m_w_branch, m_w_out, v_meta_tokens, v_norm_w, v_w_in, v_sb_q_norm, v_sb_k_norm, v_gdn_conv_w, v_gdn_a_log, v_gdn_dt_bias, v_gdn_out_norm, v_hgrn_lb_logits, v_hgrn_out_norm, v_w_branch, v_w_out):
    g_win, g_wbr, g_wout, g_meta, g_conv = exchange(
        [w_in.astype(bf16), w_branch.astype(bf16), w_out.astype(bf16), meta_tokens, gdn_conv_w],
        [False] * 5, "gather_weights")
    w_full = _unshard_cols(g_win)
    w_main = jnp.concatenate([w_full[..., :SMALL_OFF], w_full[..., SMALL_OFF + 8:]], axis=-1)
    w_small = jnp.pad(w_full[..., SMALL_OFF:SMALL_OFF + 8], ((0, 0), (0, 0), (0, N_SMALL - 8)))
    wt_main = jnp.swapaxes(w_main, 1, 2)
    wt_small = jnp.swapaxes(w_small, 1, 2)
    wbr = _unshard_cols(g_wbr)
    wbr_t = jnp.swapaxes(wbr, 2, 3)
    wout = jnp.moveaxis(g_wout, 0, 1).reshape(DEPTH, D_MODEL, D_MODEL)
    wout_t = jnp.swapaxes(wout, 1, 2)
    meta = _unshard_cols(g_meta)
    conv_w = _unshard_cols(g_conv)
    lbounds, lb_vjp = jax.vjp(_lower_bounds, hgrn_lb_logits)

    h = jnp.concatenate([jnp.zeros((PAD_FRONT, D_MODEL), f32), meta, x[0]], axis=0)
    row = lambda a: a.reshape(1, -1)
    saved = []
    for l in range(DEPTH):
        proj, small, _, xnt = inproj_fwd(h, row(norm_w[l]), w_main[l], w_small[l])
        osb = sb_fwd(proj, row(sb_q_norm[l]), row(sb_k_norm[l]))
        ypre = conv_fwd(proj, conv_w[l])
        al, dtb = _pad_lanes(row(gdn_a_log[l])), _pad_lanes(row(gdn_dt_bias[l]))
        ogd, gst = gdn_fwd(ypre, small, proj, al, dtb, row(gdn_out_norm[l]))
        ohg, hst = hgrn_fwd(proj, row(lbounds[l]), row(hgrn_out_norm[l]))
        h_next = merge_fwd(osb, proj, ogd, ohg, wbr[l], wout[l], h)
        saved.append((h, proj, small, xnt, osb, ypre, ogd, gst, ohg, hst, al, dtb))
        h = h_next

    dh, loss_row = loss_head(h, loss_target[0])

    gw_main, gw_small, gw_br, gw_out, g_conv_w = [None] * DEPTH, [None] * DEPTH, [None] * DEPTH, [None] * DEPTH, [None] * DEPTH
    g_norm, g_sbq, g_sbk, g_al, g_dt, g_gon, g_lb, g_hon = ([None] * DEPTH for _ in range(8))
    for l in reversed(range(DEPTH)):
        h_l, proj, small, xnt, osb, ypre, ogd, gst, ohg, hst, al, dtb = saved[l]
        d_osb, d_sbz, d_ogd, d_ohg, d_mix, gw_out[l], gw_br[l] = merge_bwd(osb, proj, ogd, ohg, wbr[l], wbr_t[l], wout_t[l], dh)
        d_hg, g_lb[l], g_hon[l] = hgrn_bwd(proj, row(lbounds[l]), row(hgrn_out_norm[l]), hst, d_ohg)
        d_ypre, d_small, d_gz, g_al[l], g_dt[l], g_gon[l] = gdn_bwd(ypre, small, proj, al, dtb, row(gdn_out_norm[l]), gst, d_ogd)
        d_gqkv, g_conv_w[l] = conv_bwd(proj, conv_w[l], d_ypre)
        d_q, d_k, d_v, g_sbq[l], g_sbk[l] = sb_bwd(proj, row(sb_q_norm[l]), row(sb_k_norm[l]), osb, d_osb)
        dproj = jnp.concatenate([d_q, d_k, d_v, d_sbz, d_gqkv, d_gz, d_hg, d_mix], axis=1)
        gw_main[l], gw_small[l] = inproj_bwd_w(xnt, dproj, d_small)
        dh, g_norm[l] = inproj_bwd_x(dproj, d_small, wt_main[l], wt_small[l], h_l, row(norm_w[l]), dh)

    gw_main, gw_small = jnp.stack(gw_main), jnp.stack(gw_small)
    gw_in = jnp.concatenate([gw_main[..., :SMALL_OFF], gw_small[..., :8], gw_main[..., SMALL_OFF:]], axis=-1)
    d_lbl = lb_vjp(jnp.concatenate(g_lb, axis=0))[0]
    cat = lambda rows: jnp.concatenate(rows, axis=0)
    pack = _pack(cat(g_norm), cat(g_sbq), cat(g_sbk), cat(g_al)[:, :N_HEADS], cat(g_dt)[:, :N_HEADS], cat(g_gon),
                 d_lbl, cat(g_hon), loss_row)
    g_meta_full = dh[PAD_FRONT:FRONT]
    r_win, r_wbr, r_wout, r_meta, r_conv, r_pack = exchange(
        [_shard_cols(gw_in), _shard_cols(jnp.stack(gw_br)), jnp.swapaxes(jnp.stack(gw_out).reshape(DEPTH, N_DEV, HEAD, D_MODEL), 0, 1),
         _shard_cols(g_meta_full), _shard_cols(jnp.stack(g_conv_w)), pack],
        [True, True, True, True, True, False], "exchange_grads")

    def upd(parts, w, m, v, rows, name):
        shp = w.shape
        two = (-1, shp[-1])
        outs = adamw(parts.reshape((N_DEV,) + w.reshape(two).shape), w.reshape(two), m.reshape(two), v.reshape(two), rows, name)
        return [o.reshape(shp) for o in outs]

    res = {}
    res["w_in"] = upd(r_win, w_in, m_w_in, v_w_in, 256, "adamw_w_in")
    res["w_branch"] = upd(r_wbr, w_branch, m_w_branch, v_w_branch, 1024, "adamw_w_branch")
    res["w_out"] = upd(r_wout, w_out, m_w_out, v_w_out, 256, "adamw_w_out")
    res["meta_tokens"] = upd(r_meta, meta_tokens, m_meta_tokens, v_meta_tokens, 16, "adamw_meta")
    res["gdn_conv_w"] = upd(r_conv, gdn_conv_w, m_gdn_conv_w, v_gdn_conv_w, 16, "adamw_conv")
    zero_row = jnp.zeros((1, 128), f32)
    w_pack = _pack(norm_w, sb_q_norm, sb_k_norm, gdn_a_log, gdn_dt_bias, gdn_out_norm, hgrn_lb_logits, hgrn_out_norm, zero_row)
    m_pack = _pack(m_norm_w, m_sb_q_norm, m_sb_k_norm, m_gdn_a_log, m_gdn_dt_bias, m_gdn_out_norm, m_hgrn_lb_logits, m_hgrn_out_norm, zero_row)
    v_pack = _pack(v_norm_w, v_sb_q_norm, v_sb_k_norm, v_gdn_a_log, v_gdn_dt_bias, v_gdn_out_norm, v_hgrn_lb_logits, v_hgrn_out_norm, zero_row)
    packed = [_unpack(o) for o in adamw(r_pack, w_pack, m_pack, v_pack, PACK_ROWS, "adamw_replicated")]
    for name in ("norm_w", "sb_q_norm", "sb_k_norm", "gdn_a_log", "gdn_dt_bias", "gdn_out_norm", "hgrn_lb_logits", "hgrn_out_norm"):
        res[name] = [p[name] for p in packed]
    loss = packed[0]["loss"]
    grad_x = dh[FRONT:][None]

    order = ["meta_tokens", "norm_w", "w_in", "sb_q_norm", "sb_k_norm", "gdn_conv_w", "gdn_a_log", "gdn_dt_bias",
             "gdn_out_norm", "hgrn_lb_logits", "hgrn_out_norm", "w_branch", "w_out"]
    return (loss, grad_x, *[res[n][0] for n in order], *[res[n][1] for n in order],
            *[res[n][2] for n in order], *[res[n][3] for n in order])
```

```python
import functools

import numpy as np
import jax
import jax.numpy as jnp
from jax import lax
from jax.experimental import pallas as pl
from jax.experimental.pallas import tpu as pltpu

f32 = jnp.float32
bf16 = jnp.bfloat16

D_MODEL = 1024
BRANCH = 512
HEAD = 128
N_HEADS = 4
CHUNK = 64
SB_BLOCK = 128
N_META = 16
FRONT = 128
PAD_FRONT = 112
EPS = 1e-6
DEPTH = 4
N_DEV = 8
N_IN = 9224
N_MAIN = 9216
N_SMALL = 128
SMALL_OFF = 4096
C_SBZ, C_MIX = 0, 512
C_GZ = 3584
C_HQ, C_HF, C_HI, C_HZ = 4096, 4608, 5120, 5632
C_GQKV = 6144
C_SBQ, C_SBK, C_SBV = 7680, 8192, 8704
W_MERGE = BRANCH + 3 * D_MODEL
W_IN_ORDER = ((1536, 2048), (6152, 9224), (3584, 4096), (4104, 6152), (2048, 3584), (0, 1536))

ADAM_LR, ADAM_B1, ADAM_B2, ADAM_EPS, ADAM_WD, ADAM_STEP = 0.001, 0.9, 0.999, 1e-08, 0.01, 10

VMEM_LIMIT = 56 * 1024 * 1024
MESH = pl.DeviceIdType.MESH

NN = ((1,), (0,))
NT = ((1,), (1,))
TN = ((0,), (0,))


def _dot(a, b, dims=NN):
    return lax.dot_general(a.astype(bf16), b.astype(bf16), (dims, ((), ())), preferred_element_type=f32)


@jax.custom_vjp
def mm(a, b):
    return _dot(a, b, NN)


mm.defvjp(lambda a, b: (_dot(a, b, NN), (a, b)),
          lambda r, g: (_dot(g, r[1], NT), _dot(r[0], g, TN)))


@jax.custom_vjp
def mm_nt(a, b):
    return _dot(a, b, NT)


mm_nt.defvjp(lambda a, b: (_dot(a, b, NT), (a, b)),
             lambda r, g: (_dot(g, r[1], NN), _dot(g, r[0], TN)))


@jax.custom_vjp
def mm_tn(a, b):
    return _dot(a, b, TN)


mm_tn.defvjp(lambda a, b: (_dot(a, b, TN), (a, b)),
             lambda r, g: (_dot(r[1], g, NT), _dot(r[0], g, NN)))


def _split2(x):
    hi = x.astype(bf16)
    lo = (x - hi.astype(f32)).astype(bf16)
    return hi, lo


def _cdot(c, x, dims):
    hi, lo = _split2(x)
    return (lax.dot_general(c, hi, (dims, ((), ())), preferred_element_type=f32)
            + lax.dot_general(c, lo, (dims, ((), ())), preferred_element_type=f32))


@jax.custom_vjp
def cmm(c, x):
    return _cdot(c, x, NN)


cmm.defvjp(lambda c, x: (_cdot(c, x, NN), c),
           lambda c, g: (jnp.zeros_like(c), _cdot(c, g, TN)))


def _sigmoid(x):
    return jax.nn.sigmoid(x)


def _silu(x):
    return x * jax.nn.sigmoid(x)


def _softplus(x):
    return jnp.maximum(x, 0.0) + jnp.log(1.0 + jnp.exp(-jnp.abs(x)))


def _rms(x, w):
    return x * lax.rsqrt(jnp.mean(x * x, axis=-1, keepdims=True) + EPS) * w


def _cparams(sem=None):
    return pltpu.CompilerParams(dimension_semantics=sem, vmem_limit_bytes=VMEM_LIMIT)


TM_IN = 640
TN_IN = 1024


def inproj_fwd(h, nw, w_main, w_small):
    t = h.shape[0]

    def body(h_ref, nw_ref, w_ref, ws_ref, proj_ref, small_ref, xn_ref, xnt_ref):
        @pl.when(pl.program_id(1) == 0)
        def _():
            xn = _rms(h_ref[...], nw_ref[...])
            xn_ref[...] = xn.astype(bf16)
            xnt_ref[...] = jnp.transpose(xn).astype(bf16)
            small_ref[...] = _dot(xn, ws_ref[...])

        proj_ref[...] = jnp.dot(xn_ref[...], w_ref[...], preferred_element_type=f32)

    return pl.pallas_call(
        body, grid=(t // TM_IN, N_MAIN // TN_IN),
        in_specs=[pl.BlockSpec((TM_IN, D_MODEL), lambda i, j: (i, 0)),
                  pl.BlockSpec((1, D_MODEL), lambda i, j: (0, 0)),
                  pl.BlockSpec((D_MODEL, TN_IN), lambda i, j: (0, j)),
                  pl.BlockSpec((D_MODEL, N_SMALL), lambda i, j: (0, 0))],
        out_specs=[pl.BlockSpec((TM_IN, TN_IN), lambda i, j: (i, j)),
                   pl.BlockSpec((TM_IN, N_SMALL), lambda i, j: (i, 0)),
                   pl.BlockSpec((TM_IN, D_MODEL), lambda i, j: (i, 0)),
                   pl.BlockSpec((D_MODEL, TM_IN), lambda i, j: (0, i))],
        out_shape=[jax.ShapeDtypeStruct((t, N_MAIN), f32), jax.ShapeDtypeStruct((t, N_SMALL), f32),
                   jax.ShapeDtypeStruct((t, D_MODEL), bf16), jax.ShapeDtypeStruct((D_MODEL, t), bf16)],
        compiler_params=_cparams(("arbitrary", "arbitrary")), name="inproj_fwd")(h, nw, w_main, w_small)


def inproj_bwd_x(dproj, dsmall, wt_main, wt_small, h, nw, dh_out):
    t = h.shape[0]
    nk = N_MAIN // TN_IN

    def body(dp_ref, ds_ref, wt_ref, wts_ref, h_ref, nw_ref, dho_ref, dhi_ref, dnw_ref, acc):
        i, k = pl.program_id(0), pl.program_id(1)

        @pl.when(k == 0)
        def _():
            acc[...] = _dot(ds_ref[...], wts_ref[...])

        acc[...] += _dot(dp_ref[...], wt_ref[...])

        @pl.when(k == nk - 1)
        def _():
            x = h_ref[...]
            r = lax.rsqrt(jnp.mean(x * x, axis=-1, keepdims=True) + EPS)
            xh = x * r
            dxn = acc[...]
            dxh = dxn * nw_ref[...]
            dhi_ref[...] = dho_ref[...] + r * (dxh - xh * jnp.mean(dxh * xh, axis=-1, keepdims=True))
            part = jnp.sum(dxn * xh, axis=0, keepdims=True)

            @pl.when(i == 0)
            def _():
                dnw_ref[...] = part

            @pl.when(i > 0)
            def _():
                dnw_ref[...] += part

    return pl.pallas_call(
        body, grid=(t // TM_IN, nk),
        in_specs=[pl.BlockSpec((TM_IN, TN_IN), lambda i, k: (i, k)),
                  pl.BlockSpec((TM_IN, N_SMALL), lambda i, k: (i, 0)),
                  pl.BlockSpec((TN_IN, D_MODEL), lambda i, k: (k, 0)),
                  pl.BlockSpec((N_SMALL, D_MODEL), lambda i, k: (0, 0)),
                  pl.BlockSpec((TM_IN, D_MODEL), lambda i, k: (i, 0)),
                  pl.BlockSpec((1, D_MODEL), lambda i, k: (0, 0)),
                  pl.BlockSpec((TM_IN, D_MODEL), lambda i, k: (i, 0))],
        out_specs=[pl.BlockSpec((TM_IN, D_MODEL), lambda i, k: (i, 0)),
                   pl.BlockSpec((1, D_MODEL), lambda i, k: (0, 0))],
        out_shape=[jax.ShapeDtypeStruct((t, D_MODEL), f32), jax.ShapeDtypeStruct((1, D_MODEL), f32)],
        scratch_shapes=[pltpu.VMEM((TM_IN, D_MODEL), f32)],
        compiler_params=_cparams(("arbitrary", "arbitrary")), name="inproj_bwd_x")(
            dproj, dsmall, wt_main, wt_small, h, nw, dh_out)


def inproj_bwd_w(xnt, dproj, dsmall):
    t = xnt.shape[1]
    nt = t // TM_IN

    def body(xnt_ref, dp_ref, ds_ref, dw_ref, dws_ref):
        n, s = pl.program_id(0), pl.program_id(1)
        part = _dot(xnt_ref[...], dp_ref[...])

        @pl.when(s == 0)
        def _():
            dw_ref[...] = part

        @pl.when(s > 0)
        def _():
            dw_ref[...] += part

        @pl.when(n == 0)
        def _():
            ps = _dot(xnt_ref[...], ds_ref[...])

            @pl.when(s == 0)
            def _():
                dws_ref[...] = ps

            @pl.when(s > 0)
            def _():
                dws_ref[...] += ps

    return pl.pallas_call(
        body, grid=(N_MAIN // TN_IN, nt),
        in_specs=[pl.BlockSpec((D_MODEL, TM_IN), lambda n, s: (0, s)),
                  pl.BlockSpec((TM_IN, TN_IN), lambda n, s: (s, n)),
                  pl.BlockSpec((TM_IN, N_SMALL), lambda n, s: (s, 0))],
        out_specs=[pl.BlockSpec((D_MODEL, TN_IN), lambda n, s: (0, n)),
                   pl.BlockSpec((D_MODEL, N_SMALL), lambda n, s: (0, 0))],
        out_shape=[jax.ShapeDtypeStruct((D_MODEL, N_MAIN), f32), jax.ShapeDtypeStruct((D_MODEL, N_SMALL), f32)],
        compiler_params=_cparams(("arbitrary", "arbitrary")), name="inproj_bwd_w")(xnt, dproj, dsmall)


SB_SCALE = HEAD ** -0.5


SB_SUB = 4
SB_KS = SB_SUB * SB_BLOCK


def _sb_padded(t):
    return -(-t // SB_KS) * SB_KS


def _sb_prep(k_ref, v_ref, kw_ref, kn_scr, vb_scr, nb):
    def prep(b, c):
        rows = pl.ds(pl.multiple_of(b * SB_BLOCK, SB_BLOCK), SB_BLOCK)
        kn_scr[rows, :] = _rms(k_ref[rows, :], kw_ref[...]).astype(bf16)
        vb_scr[rows, :] = v_ref[rows, :].astype(bf16)
        return c

    lax.fori_loop(0, nb, prep, 0)
    pad = kn_scr.shape[0] - nb * SB_BLOCK
    if pad:
        kn_scr[nb * SB_BLOCK:, :] = jnp.zeros((pad, HEAD), bf16)
        vb_scr[nb * SB_BLOCK:, :] = jnp.zeros((pad, HEAD), bf16)


def _tri_ext(cmp):
    r = lax.broadcasted_iota(jnp.int32, (SB_BLOCK, 2 * SB_BLOCK), 0)
    c = lax.broadcasted_iota(jnp.int32, (SB_BLOCK, 2 * SB_BLOCK), 1)
    return jnp.where((c >= SB_BLOCK) | cmp(r, c), 1.0, 0.0).astype(bf16)


def _sb_suffix(x, carry, tri_ext):
    hi, lo = _split2(x)
    parts = [p[:, c * SB_BLOCK:(c + 1) * SB_BLOCK] for p in (hi, lo) for c in range(SB_SUB)]
    w = jnp.dot(jnp.concatenate(parts, axis=0), tri_ext, preferred_element_type=f32)
    outs = [None] * SB_SUB
    for c in reversed(range(SB_SUB)):
        blk = w[c * SB_BLOCK:(c + 1) * SB_BLOCK] + w[(SB_SUB + c) * SB_BLOCK:(SB_SUB + c + 1) * SB_BLOCK]
        outs[c] = carry + blk[:, :SB_BLOCK]
        carry = carry + blk[:, SB_BLOCK:]
    return jnp.concatenate(outs, axis=1), carry


def _sb_scores(qn, kt, i, jb, masked):
    z = lax.dot_general(qn, kt, (NT, ((), ())), preferred_element_type=f32) * SB_SCALE
    lsz = jnp.minimum(z, 0.0) - jnp.log(1.0 + jnp.exp(-jnp.abs(z)))
    lk = lsz - z
    mask = None
    if masked:
        t_idx = i * SB_BLOCK + lax.broadcasted_iota(jnp.int32, (SB_BLOCK, SB_KS), 0)
        s_idx = jb * SB_KS + lax.broadcasted_iota(jnp.int32, (SB_BLOCK, SB_KS), 1)
        mask = (s_idx < t_idx) & (s_idx >= PAD_FRONT)
        lk = jnp.where(mask, lk, 0.0)
    return mask, lsz, lk


SB_DEAD = -104.0


def _sb_walk(i, tile, carry):
    n = i // SB_SUB + 1
    live = lambda c: jnp.max(c[1]) > SB_DEAD
    carry = tile(n - 1, carry, True)
    _, carry = lax.while_loop(lambda st: (st[0] >= 1) & live(st[1]),
                              lambda st: (st[0] - 1, tile(st[0], st[1], False)), (n - 2, carry))
    return lax.cond((n >= 2) & live(carry), lambda c: tile(0, c, True), lambda c: c, carry)


def sb_fwd(proj, qw, kw):
    t = proj.shape[0]
    nb = t // SB_BLOCK

    def body(q_ref, k_ref, v_ref, qw_ref, kw_ref, o_ref, kn_scr, vb_scr):
        i = pl.program_id(1)

        @pl.when(i == 0)
        def _():
            _sb_prep(k_ref, v_ref, kw_ref, kn_scr, vb_scr, nb)

        qn = _rms(q_ref[...], qw_ref[...]).astype(bf16)
        u_ex = _tri_ext(lambda r, c: r > c)

        def tile(jb, carry, masked):
            acc, r_carry = carry
            rows = pl.ds(pl.multiple_of(jb * SB_KS, SB_KS), SB_KS)
            mask, lsz, lk = _sb_scores(qn, kn_scr[rows, :], i, jb, masked)
            passed, r_carry = _sb_suffix(lk, r_carry, u_ex)
            a = jnp.exp(lsz + passed)
            if masked:
                a = jnp.where(mask, a, 0.0)
            a_hi, a_lo = _split2(a)
            both = jnp.dot(jnp.concatenate([a_hi, a_lo], axis=0), vb_scr[rows, :], preferred_element_type=f32)
            return acc + (both[:SB_BLOCK] + both[SB_BLOCK:]), r_carry

        zeros = jnp.zeros((SB_BLOCK, HEAD), f32)
        acc, _ = _sb_walk(i, tile, (zeros, zeros))
        o_ref[...] = acc

    qb, cb, vb = C_SBQ // HEAD, C_SBK // HEAD, C_SBV // HEAD
    return pl.pallas_call(
        body, grid=(N_HEADS, nb),
        in_specs=[pl.BlockSpec((SB_BLOCK, HEAD), lambda h, i: (i, qb + h)),
                  pl.BlockSpec((t, HEAD), lambda h, i: (0, cb + h)),
                  pl.BlockSpec((t, HEAD), lambda h, i: (0, vb + h)),
                  pl.BlockSpec((1, HEAD), lambda h, i: (0, 0)),
                  pl.BlockSpec((1, HEAD), lambda h, i: (0, 0))],
        out_specs=pl.BlockSpec((SB_BLOCK, HEAD), lambda h, i: (i, h)),
        out_shape=jax.ShapeDtypeStruct((t, BRANCH), f32),
        scratch_shapes=[pltpu.VMEM((_sb_padded(t), HEAD), bf16), pltpu.VMEM((_sb_padded(t), HEAD), bf16)],
        compiler_params=_cparams(("arbitrary", "arbitrary")), name="sb_fwd")(proj, proj, proj, qw, kw)


def sb_bwd(proj, qw, kw, o, do, dproj):
    t = proj.shape[0]
    nb = t // SB_BLOCK

    def body(q_ref, k_ref, v_ref, qw_ref, kw_ref, o_ref, do_ref, _, dp_ref, dqw_ref, dkw_ref,
             kn_scr, vb_scr, dk_acc, dv_acc, dq_stage, dq_sems, kv_sems):
        h, i = pl.program_id(0), pl.program_id(1)
        step = h * nb + i
        slot = step % 2

        def dq_copy(sl, head):
            return pltpu.make_async_copy(
                dq_stage.at[sl], dp_ref.at[pl.ds(pl.multiple_of(i * SB_BLOCK, SB_BLOCK), SB_BLOCK),
                                           pl.ds(C_SBQ + head * HEAD, HEAD)], dq_sems.at[sl])

        @pl.when(i == 0)
        def _():
            _sb_prep(k_ref, v_ref, kw_ref, kn_scr, vb_scr, nb)
            dk_acc[...] = jnp.zeros_like(dk_acc)
            dv_acc[...] = jnp.zeros_like(dv_acc)

        @pl.when((i == 0) & (h == 0))
        def _():
            dqw_ref[...] = jnp.zeros_like(dqw_ref)
            dkw_ref[...] = jnp.zeros_like(dkw_ref)

        q = q_ref[...]
        rq = lax.rsqrt(jnp.mean(q * q, axis=-1, keepdims=True) + EPS)
        qh = q * rq
        qn = (qh * qw_ref[...]).astype(bf16)
        do_f = do_ref[...]
        dob = do_f.astype(bf16)
        d_row = jnp.sum(dob.astype(f32) * o_ref[...], axis=-1, keepdims=True)
        u_ex = _tri_ext(lambda r, c: r > c)
        u_in = _tri_ext(lambda r, c: r >= c)

        def tile(jb, carry, masked):
            dq, r_carry, f_carry = carry
            rows = pl.ds(pl.multiple_of(jb * SB_KS, SB_KS), SB_KS)
            kt = kn_scr[rows, :]
            vt = vb_scr[rows, :]
            mask, lsz, lk = _sb_scores(qn, kt, i, jb, masked)
            passed, r_carry = _sb_suffix(lk, r_carry, u_ex)
            a = jnp.exp(lsz + passed)
            if masked:
                a = jnp.where(mask, a, 0.0)
            da = lax.dot_general(dob, vt, (NT, ((), ())), preferred_element_type=f32)
            e = a * da
            e_suf, f_carry = _sb_suffix(e, f_carry, u_in)
            sg = jnp.exp(lsz)
            dz = (e * (1.0 - sg) - (d_row - e_suf) * sg) * SB_SCALE
            if masked:
                dz = jnp.where(mask, dz, 0.0)
            dzb = dz.astype(bf16)
            dq = dq + jnp.dot(dzb, kt, preferred_element_type=f32)
            dk_acc[rows, :] += lax.dot_general(dzb, qn, (TN, ((), ())), preferred_element_type=f32)
            dv_acc[rows, :] += lax.dot_general(a.astype(bf16), dob, (TN, ((), ())), preferred_element_type=f32)
            return dq, r_carry, f_carry

        zeros = jnp.zeros((SB_BLOCK, HEAD), f32)
        dqn, _, _ = _sb_walk(i, tile, (zeros, zeros, zeros))
        gq = dqn * qw_ref[...]
        dqw_ref[...] += jnp.sum(dqn * qh, axis=0, keepdims=True)

        @pl.when(step >= 2)
        def _():
            dq_copy(slot, 0).wait()

        dq_stage[slot] = rq * (gq - qh * jnp.mean(gq * qh, axis=-1, keepdims=True))
        for head in range(N_HEADS):
            @pl.when(h == head)
            def _(head=head):
                dq_copy(slot, head).start()

        @pl.when(i == nb - 1)
        def _():
            def fin(b, c):
                rows = pl.ds(pl.multiple_of(b * SB_BLOCK, SB_BLOCK), SB_BLOCK)
                kk = k_ref[rows, :]
                rk = lax.rsqrt(jnp.mean(kk * kk, axis=-1, keepdims=True) + EPS)
                kh = kk * rk
                dkn = dk_acc[rows, :]
                gk = dkn * kw_ref[...]
                dk_acc[rows, :] = rk * (gk - kh * jnp.mean(gk * kh, axis=-1, keepdims=True))
                dkw_ref[...] += jnp.sum(dkn * kh, axis=0, keepdims=True)
                return c

            lax.fori_loop(0, nb, fin, 0)
            for head in range(N_HEADS):
                @pl.when(h == head)
                def _(head=head):
                    outs = [pltpu.make_async_copy(acc.at[pl.ds(0, t)], dp_ref.at[:, pl.ds(c0 + head * HEAD, HEAD)],
                                                  kv_sems.at[n])
                            for n, (acc, c0) in enumerate(((dk_acc, C_SBK), (dv_acc, C_SBV)))]
                    for cp in outs:
                        cp.start()
                    for cp in outs:
                        cp.wait()

        @pl.when(step == N_HEADS * nb - 1)
        def _():
            dq_copy(1 - slot, 0).wait()
            dq_copy(slot, 0).wait()

    qb, cb, vb = C_SBQ // HEAD, C_SBK // HEAD, C_SBV // HEAD
    blk = pl.BlockSpec((SB_BLOCK, HEAD), lambda h, i: (i, h))
    wsp = pl.BlockSpec((1, HEAD), lambda h, i: (0, 0))
    any_spec = pl.BlockSpec(memory_space=pl.ANY)
    return pl.pallas_call(
        body, grid=(N_HEADS, nb),
        in_specs=[pl.BlockSpec((SB_BLOCK, HEAD), lambda h, i: (i, qb + h)),
                  pl.BlockSpec((t, HEAD), lambda h, i: (0, cb + h)),
                  pl.BlockSpec((t, HEAD), lambda h, i: (0, vb + h)), wsp, wsp, blk, blk, any_spec],
        out_specs=[any_spec, wsp, wsp],
        out_shape=[jax.ShapeDtypeStruct((t, N_MAIN), f32)] + [jax.ShapeDtypeStruct((1, HEAD), f32)] * 2,
        scratch_shapes=[pltpu.VMEM((_sb_padded(t), HEAD), bf16), pltpu.VMEM((_sb_padded(t), HEAD), bf16),
                        pltpu.VMEM((_sb_padded(t), HEAD), f32), pltpu.VMEM((_sb_padded(t), HEAD), f32),
                        pltpu.VMEM((2, SB_BLOCK, HEAD), f32), pltpu.SemaphoreType.DMA((2,)), pltpu.SemaphoreType.DMA((2,))],
        input_output_aliases={7: 0},
        compiler_params=_cparams(("arbitrary", "arbitrary")), name="sb_bwd")(proj, proj, proj, qw, kw, o, do, dproj)


TM_CONV = 640
CONV_W = 4
GQKV = 3 * BRANCH


def conv_fwd(proj, cw):
    t = proj.shape[0]
    halo_blocks = TM_CONV // 8

    def body(x0_ref, x1_ref, x2_ref, p0_ref, p1_ref, p2_ref, cw_ref, y_ref):
        i = pl.program_id(0)
        for s, (x_ref, p_ref) in enumerate(((x0_ref, p0_ref), (x1_ref, p1_ref), (x2_ref, p2_ref))):
            prev = jnp.where(i > 0, p_ref[...], 0.0)
            xx = jnp.concatenate([prev, x_ref[...]], axis=0)
            cols = slice(s * BRANCH, (s + 1) * BRANCH)
            y = xx[8:] * cw_ref[CONV_W - 1:CONV_W, cols]
            for k in range(CONV_W - 1):
                y = y + pltpu.roll(xx, CONV_W - 1 - k, 0)[8:] * cw_ref[k:k + 1, cols]
            y_ref[:, cols] = y

    c0 = C_GQKV // BRANCH
    xs = [pl.BlockSpec((TM_CONV, BRANCH), functools.partial(lambda i, s: (i, c0 + s), s=s)) for s in range(3)]
    ps = [pl.BlockSpec((8, BRANCH), functools.partial(lambda i, s: (jnp.maximum(i * halo_blocks - 1, 0), c0 + s), s=s))
          for s in range(3)]
    return pl.pallas_call(
        body, grid=(t // TM_CONV,),
        in_specs=xs + ps + [pl.BlockSpec((CONV_W, GQKV), lambda i: (0, 0))],
        out_specs=pl.BlockSpec((TM_CONV, GQKV), lambda i: (i, 0)),
        out_shape=jax.ShapeDtypeStruct((t, GQKV), f32),
        compiler_params=_cparams(("arbitrary",)), name="conv_fwd")(proj, proj, proj, proj, proj, proj, cw)


def conv_bwd(proj, cw, dy, dproj):
    t = proj.shape[0]
    nt = t // TM_CONV
    halo_blocks = TM_CONV // 8

    def body(x0_ref, x1_ref, x2_ref, p0_ref, p1_ref, p2_ref, cw_ref, dy_ref, dyn_ref, _, dx_ref, dw_ref):
        i = pl.program_id(0)

        @pl.when(i == 0)
        def _():
            dw_ref[...] = jnp.zeros_like(dw_ref)

        nxt = jnp.where(i < nt - 1, dyn_ref[...], 0.0)
        dyy = jnp.concatenate([dy_ref[...], nxt], axis=0)
        n_rows = TM_CONV + 8
        dx = dyy[:TM_CONV] * cw_ref[CONV_W - 1:CONV_W, :]
        for k in range(CONV_W - 1):
            sh = CONV_W - 1 - k
            dx = dx + pltpu.roll(dyy, n_rows - sh, 0)[:TM_CONV] * cw_ref[k:k + 1, :]
        dx_ref[...] = dx
        dy_c = dy_ref[...]
        for s, (x_ref, p_ref) in enumerate(((x0_ref, p0_ref), (x1_ref, p1_ref), (x2_ref, p2_ref))):
            prev = jnp.where(i > 0, p_ref[...], 0.0)
            xx = jnp.concatenate([prev, x_ref[...]], axis=0)
            cols = slice(s * BRANCH, (s + 1) * BRANCH)
            for k in range(CONV_W):
                sh = CONV_W - 1 - k
                xs = xx[8:] if sh == 0 else pltpu.roll(xx, sh, 0)[8:]
                dw_ref[k:k + 1, cols] += jnp.sum(xs * dy_c[:, cols], axis=0, keepdims=True)

    c0 = C_GQKV // BRANCH
    xs = [pl.BlockSpec((TM_CONV, BRANCH), functools.partial(lambda i, s: (i, c0 + s), s=s)) for s in range(3)]
    ps = [pl.BlockSpec((8, BRANCH), functools.partial(lambda i, s: (jnp.maximum(i * halo_blocks - 1, 0), c0 + s), s=s))
          for s in range(3)]
    return pl.pallas_call(
        body, grid=(nt,),
        in_specs=xs + ps + [pl.BlockSpec((CONV_W, GQKV), lambda i: (0, 0)),
                            pl.BlockSpec((TM_CONV, GQKV), lambda i: (i, 0)),
                            pl.BlockSpec((8, GQKV), lambda i: (jnp.minimum((i + 1) * halo_blocks, nt * halo_blocks - 1), 0)),
                            pl.BlockSpec(memory_space=pl.ANY)],
        out_specs=[pl.BlockSpec((TM_CONV, GQKV), lambda i: (i, C_GQKV // GQKV)), pl.BlockSpec((CONV_W, GQKV), lambda i: (0, 0))],
        out_shape=[jax.ShapeDtypeStruct((t, N_MAIN), f32), jax.ShapeDtypeStruct((CONV_W, GQKV), f32)],
        input_output_aliases={9: 0},
        compiler_params=_cparams(("arbitrary",)), name="conv_bwd")(proj, proj, proj, proj, proj, proj, cw, dy, dy, dproj)


def _iota2(n, m, d):
    return lax.broadcasted_iota(jnp.int32, (n, m), d)


def _lane_pick(row_or_mat, idx):
    lanes = lax.broadcasted_iota(jnp.int32, row_or_mat.shape, row_or_mat.ndim - 1)
    return jnp.sum(jnp.where(lanes == idx, row_or_mat, 0.0), axis=-1, keepdims=True)


def _cumsum_consts():
    i = np.arange(CHUNK)
    incl = i[None, :] <= i[:, None]
    suf = i[None, :] > i[:, None]
    return np.concatenate([incl, suf], 0).astype(np.float32)


HG_LEVELS = (64, 32, 16, 8, 4, 2)


def _hgrn_consts():
    i = np.arange(CHUNK)
    rows = [i[None, :] <= i[:, None], i[None, :] > i[:, None]]
    for b in HG_LEVELS:
        ref = (i // b) * b + b // 2 - 1
        second = (i % b) >= b // 2
        rows.append((i[None, :] > ref[:, None]) & (i[None, :] <= i[:, None]) & second[:, None])
        rows.append((i[None, :] > i[:, None]) & (i[None, :] <= ref[:, None]) & (~second)[:, None])
    return np.concatenate(rows, 0).astype(np.float32)


HC = N_HEADS * CHUNK
BATCH0 = ((0,), (0,))


def _bdot(a, b, contract):
    return lax.dot_general(a.astype(bf16), b.astype(bf16), (contract, BATCH0), preferred_element_type=f32)


B_NN = ((2,), (1,))
B_NT = ((2,), (2,))
B_TN = ((1,), (1,))


@jax.custom_vjp
def bmm(a, b):
    return _bdot(a, b, B_NN)


bmm.defvjp(lambda a, b: (_bdot(a, b, B_NN), (a, b)),
           lambda r, g: (_bdot(g, r[1], B_NT), _bdot(r[0], g, B_TN)))


@jax.custom_vjp
def bmm_nt(a, b):
    return _bdot(a, b, B_NT)


bmm_nt.defvjp(lambda a, b: (_bdot(a, b, B_NT), (a, b)),
              lambda r, g: (_bdot(g, r[1], B_NN), _bdot(g, r[0], B_TN)))


@jax.custom_vjp
def bmm_tn(a, b):
    return _bdot(a, b, B_TN)


bmm_tn.defvjp(lambda a, b: (_bdot(a, b, B_TN), (a, b)),
              lambda r, g: (_bdot(r[1], g, B_NT), _bdot(r[0], g, B_NN)))


def _stack_heads(x):
    return jnp.concatenate([x[:, h * HEAD:(h + 1) * HEAD] for h in range(N_HEADS)], axis=0)


def _unstack_heads(x):
    return jnp.concatenate([x[h * CHUNK:(h + 1) * CHUNK] for h in range(N_HEADS)], axis=1)


def _gdn_chunk(state, ypre, small, gz, a_log, dt_b, on_w, c2, vm):
    r = _iota2(HC, HC, 0)
    c = _iota2(HC, HC, 1)
    same_head = (r >> 6) == (c >> 6)
    causal = same_head & (r >= c)
    strict = same_head & (r > c)
    q = _silu(_stack_heads(ypre[:, :BRANCH]))
    k = _silu(_stack_heads(ypre[:, BRANCH:2 * BRANCH]))
    v = _silu(_stack_heads(ypre[:, 2 * BRANCH:]))
    q = q * lax.rsqrt(jnp.sum(q * q, axis=-1, keepdims=True) + EPS) * (HEAD ** -0.5)
    k = k * lax.rsqrt(jnp.sum(k * k, axis=-1, keepdims=True) + EPS)
    col = lambda f: jnp.concatenate([f(h) for h in range(N_HEADS)], axis=0)
    chunk_col = lambda x: jnp.broadcast_to(x, (CHUNK, 1))
    beta = _sigmoid(col(lambda h: _lane_pick(small, h))) * col(lambda h: vm)
    g = (-jnp.exp(col(lambda h: chunk_col(_lane_pick(a_log, h))))
         * _softplus(col(lambda h: _lane_pick(small, N_HEADS + h)) + col(lambda h: chunk_col(_lane_pick(dt_b, h)))))
    g_l = _unstack_heads(jnp.broadcast_to(g, (HC, HEAD)))
    e2 = cmm(c2, g_l)
    gc = _stack_heads(e2[:CHUNK])
    gsuf = _stack_heads(e2[CHUNK:])
    g_row = jnp.broadcast_to(jnp.transpose(gc)[0:1, :], (HC, HC))
    g_col = jnp.concatenate([gc, gc], axis=1)
    dec = jnp.where(causal, jnp.exp(jnp.minimum(g_col - g_row, 0.0)), 0.0)
    kb = k * beta
    m = jnp.where(strict, mm_nt(kb, k) * dec, 0.0)
    x = jnp.concatenate([v * beta, kb * jnp.exp(gc)], axis=1)
    x = x - mm(m, x)
    p = m
    for _ in range(5):
        p = mm(p, p)
        x = x + mm(p, x)
    u, w = x[:, :HEAD], x[:, HEAD:]
    aqk = jnp.where(causal, mm_nt(q, k) * dec, 0.0)
    tot = jnp.sum(g_l, axis=0, keepdims=True)
    g_last = jnp.exp(jnp.stack([tot[:, h * HEAD:(h + 1) * HEAD] for h in range(N_HEADS)], axis=0))
    per_head = lambda a: a.reshape(N_HEADS, CHUNK, HEAD)
    v_new = u - bmm(per_head(w), state).reshape(HC, HEAD)
    o = bmm(per_head(q * jnp.exp(gc)), state).reshape(HC, HEAD) + mm(aqk, v_new)
    new_state = state * g_last + bmm_tn(per_head(k * jnp.exp(gsuf)), per_head(v_new))
    out = _unstack_heads(_rms(o, on_w)) * _silu(gz)
    return new_state, out


def _hgrn_chunk(state, hq, hf, hi, hz, lb, on_w, cm, vm):
    r = _iota2(HC, HC, 0)
    c = _iota2(HC, HC, 1)
    forget = lb + (1.0 - lb) * _sigmoid(hf)
    g_l = jnp.log(forget)
    e = cmm(cm, g_l)
    q = _stack_heads(_silu(hq))
    k = _stack_heads((1.0 - lb) * _sigmoid(-hf))
    v = _stack_heads(hi * vm)
    sect = lambda n: _stack_heads(e[n * CHUNK:(n + 1) * CHUNK])
    gc, gsuf = sect(0), sect(1)
    per_head = lambda a: a.reshape(N_HEADS, CHUNK, HEAD)
    o = bmm_nt(per_head(q * jnp.exp(gc)), state).reshape(HC, HEAD)
    a = jnp.where(r == c, jnp.sum(q * k, axis=-1, keepdims=True), 0.0)
    for li, b in enumerate(HG_LEVELS):
        sh = b.bit_length() - 1
        pair = ((r >> sh) == (c >> sh)) & ((r & (b - 1)) >= b // 2) & ((c & (b - 1)) < b // 2)
        a = a + jnp.where(pair, mm_nt(q * jnp.exp(sect(2 + 2 * li)), k * jnp.exp(sect(3 + 2 * li))), 0.0)
    o = o + mm(a, v)
    tot = jnp.sum(g_l, axis=0, keepdims=True)
    g_end = jnp.exp(jnp.stack([tot[:, h * HEAD:(h + 1) * HEAD] for h in range(N_HEADS)], axis=0))
    new_state = state * g_end + bmm_tn(per_head(v), per_head(k * jnp.exp(gsuf)))
    out = _unstack_heads(_rms(o, on_w)) * _silu(hz)
    return new_state, out


def _vmask(chunk_idx):
    rows = chunk_idx * CHUNK + lax.broadcasted_iota(jnp.int32, (CHUNK, 1), 0)
    return jnp.where(rows >= PAD_FRONT, 1.0, 0.0)


def _row(n):
    return pl.BlockSpec((1, n), lambda i: (0, 0))


def gdn_fwd(ypre, small, proj, a_log, dt_b, on_w):
    t = ypre.shape[0]
    nc = t // CHUNK
    c2 = jnp.asarray(_cumsum_consts(), bf16)

    def body(y_ref, s_ref, z_ref, al_ref, dt_ref, on_ref, c2_ref, o_ref, st_ref, state):
        i = pl.program_id(0)

        @pl.when(i == 0)
        def _():
            state[...] = jnp.zeros_like(state)

        s_in = state[...]
        st_ref[0] = s_in
        s_new, out = _gdn_chunk(s_in, y_ref[...], s_ref[...], z_ref[...], al_ref[...], dt_ref[...], on_ref[...],
                                c2_ref[...], _vmask(i))
        state[...] = s_new
        o_ref[...] = out

    return pl.pallas_call(
        body, grid=(nc,),
        in_specs=[pl.BlockSpec((CHUNK, GQKV), lambda i: (i, 0)), pl.BlockSpec((CHUNK, N_SMALL), lambda i: (i, 0)),
                  pl.BlockSpec((CHUNK, BRANCH), lambda i: (i, C_GZ // BRANCH)), _row(128), _row(128), _row(128),
                  pl.BlockSpec((2 * CHUNK, CHUNK), lambda i: (0, 0))],
        out_specs=[pl.BlockSpec((CHUNK, BRANCH), lambda i: (i, 0)),
                   pl.BlockSpec((1, N_HEADS, HEAD, HEAD), lambda i: (i, 0, 0, 0))],
        out_shape=[jax.ShapeDtypeStruct((t, BRANCH), f32), jax.ShapeDtypeStruct((nc, N_HEADS, HEAD, HEAD), f32)],
        scratch_shapes=[pltpu.VMEM((N_HEADS, HEAD, HEAD), f32)],
        compiler_params=_cparams(("arbitrary",)), name="gdn_fwd")(ypre, small, proj, a_log, dt_b, on_w, c2)


def gdn_bwd(ypre, small, proj, a_log, dt_b, on_w, states, d_out, dproj):
    t = ypre.shape[0]
    nc = t // CHUNK
    c2 = jnp.asarray(_cumsum_consts(), bf16)

    def body(y_ref, s_ref, z_ref, al_ref, dt_ref, on_ref, c2_ref, st_ref, do_ref, _,
             dy_ref, ds_ref, dz_ref, dal_ref, ddt_ref, don_ref, dstate):
        i = pl.program_id(0)

        @pl.when(i == 0)
        def _():
            dstate[...] = jnp.zeros_like(dstate)
            dal_ref[...] = jnp.zeros_like(dal_ref)
            ddt_ref[...] = jnp.zeros_like(ddt_ref)
            don_ref[...] = jnp.zeros_like(don_ref)

        vm = _vmask(nc - 1 - i)
        c2v = c2_ref[...]
        fn = lambda s, y, sm, z, al, dt, on: _gdn_chunk(s, y, sm, z, al, dt, on, c2v, vm)
        _, vjp = jax.vjp(fn, st_ref[0], y_ref[...], s_ref[...], z_ref[...], al_ref[...], dt_ref[...], on_ref[...])
        d_s, d_y, d_sm, d_z, d_al, d_dt, d_on = vjp((dstate[...], do_ref[...]))
        dstate[...] = d_s
        dy_ref[...] = d_y
        ds_ref[...] = d_sm
        dz_ref[...] = d_z
        dal_ref[...] += d_al
        ddt_ref[...] += d_dt
        don_ref[...] += d_on

    rev = lambda i: (nc - 1 - i, 0)
    return pl.pallas_call(
        body, grid=(nc,),
        in_specs=[pl.BlockSpec((CHUNK, GQKV), rev), pl.BlockSpec((CHUNK, N_SMALL), rev),
                  pl.BlockSpec((CHUNK, BRANCH), lambda i: (nc - 1 - i, C_GZ // BRANCH)), _row(128), _row(128), _row(128),
                  pl.BlockSpec((2 * CHUNK, CHUNK), lambda i: (0, 0)),
                  pl.BlockSpec((1, N_HEADS, HEAD, HEAD), lambda i: (nc - 1 - i, 0, 0, 0)),
                  pl.BlockSpec((CHUNK, BRANCH), rev), pl.BlockSpec(memory_space=pl.ANY)],
        out_specs=[pl.BlockSpec((CHUNK, GQKV), rev), pl.BlockSpec((CHUNK, N_SMALL), rev),
                   pl.BlockSpec((CHUNK, BRANCH), lambda i: (nc - 1 - i, C_GZ // BRANCH)), _row(128), _row(128), _row(128)],
        out_shape=[jax.ShapeDtypeStruct((t, GQKV), f32), jax.ShapeDtypeStruct((t, N_SMALL), f32),
                   jax.ShapeDtypeStruct((t, N_MAIN), f32)] + [jax.ShapeDtypeStruct((1, 128), f32)] * 3,
        scratch_shapes=[pltpu.VMEM((N_HEADS, HEAD, HEAD), f32)], input_output_aliases={9: 2},
        compiler_params=_cparams(("arbitrary",)), name="gdn_bwd")(
            ypre, small, proj, a_log, dt_b, on_w, c2, states, d_out, dproj)


def hgrn_fwd(proj, lb, on_w):
    t = proj.shape[0]
    nc = t // CHUNK
    cm = jnp.asarray(_hgrn_consts(), bf16)
    ncm = cm.shape[0]

    def body(q_ref, f_ref, i_ref, z_ref, lb_ref, on_ref, cm_ref, o_ref, st_ref, state):
        i = pl.program_id(0)

        @pl.when(i == 0)
        def _():
            state[...] = jnp.zeros_like(state)

        s_in = state[...]
        st_ref[0] = s_in
        s_new, out = _hgrn_chunk(s_in, q_ref[...], f_ref[...], i_ref[...], z_ref[...], lb_ref[...], on_ref[...],
                                 cm_ref[...], _vmask(i))
        state[...] = s_new
        o_ref[...] = out

    sec = lambda off: pl.BlockSpec((CHUNK, BRANCH), functools.partial(lambda i, b: (i, b), b=off // BRANCH))
    return pl.pallas_call(
        body, grid=(nc,),
        in_specs=[sec(C_HQ), sec(C_HF), sec(C_HI), sec(C_HZ), _row(BRANCH), _row(128),
                  pl.BlockSpec((ncm, CHUNK), lambda i: (0, 0))],
        out_specs=[pl.BlockSpec((CHUNK, BRANCH), lambda i: (i, 0)),
                   pl.BlockSpec((1, N_HEADS, HEAD, HEAD), lambda i: (i, 0, 0, 0))],
        out_shape=[jax.ShapeDtypeStruct((t, BRANCH), f32), jax.ShapeDtypeStruct((nc, N_HEADS, HEAD, HEAD), f32)],
        scratch_shapes=[pltpu.VMEM((N_HEADS, HEAD, HEAD), f32)],
        compiler_params=_cparams(("arbitrary",)), name="hgrn_fwd")(proj, proj, proj, proj, lb, on_w, cm)


def hgrn_bwd(proj, lb, on_w, states, d_out, dproj):
    t = proj.shape[0]
    nc = t // CHUNK
    cm = jnp.asarray(_hgrn_consts(), bf16)
    ncm = cm.shape[0]

    def body(q_ref, f_ref, i_ref, z_ref, lb_ref, on_ref, cm_ref, st_ref, do_ref, _, dh_ref, dlb_ref, don_ref, dstate):
        i = pl.program_id(0)

        @pl.when(i == 0)
        def _():
            dstate[...] = jnp.zeros_like(dstate)
            dlb_ref[...] = jnp.zeros_like(dlb_ref)
            don_ref[...] = jnp.zeros_like(don_ref)

        vm = _vmask(nc - 1 - i)
        cmv = cm_ref[...]
        fn = lambda s, a, b, c, d, l, on: _hgrn_chunk(s, a, b, c, d, l, on, cmv, vm)
        _, vjp = jax.vjp(fn, st_ref[0], q_ref[...], f_ref[...], i_ref[...], z_ref[...], lb_ref[...], on_ref[...])
        d_s, d_q, d_f, d_i, d_z, d_lb, d_on = vjp((dstate[...], do_ref[...]))
        dstate[...] = d_s
        dh_ref[...] = jnp.concatenate([d_q, d_f, d_i, d_z], axis=1)
        dlb_ref[...] += d_lb
        don_ref[...] += d_on

    rev = lambda i: (nc - 1 - i, 0)
    sec = lambda off: pl.BlockSpec((CHUNK, BRANCH), functools.partial(lambda i, b: (nc - 1 - i, b), b=off // BRANCH))
    return pl.pallas_call(
        body, grid=(nc,),
        in_specs=[sec(C_HQ), sec(C_HF), sec(C_HI), sec(C_HZ), _row(BRANCH), _row(128),
                  pl.BlockSpec((ncm, CHUNK), lambda i: (0, 0)),
                  pl.BlockSpec((1, N_HEADS, HEAD, HEAD), lambda i: (nc - 1 - i, 0, 0, 0)),
                  pl.BlockSpec((CHUNK, BRANCH), rev), pl.BlockSpec(memory_space=pl.ANY)],
        out_specs=[pl.BlockSpec((CHUNK, 4 * BRANCH), lambda i: (nc - 1 - i, C_HQ // (4 * BRANCH))), _row(BRANCH), _row(128)],
        out_shape=[jax.ShapeDtypeStruct((t, N_MAIN), f32), jax.ShapeDtypeStruct((1, BRANCH), f32),
                   jax.ShapeDtypeStruct((1, 128), f32)],
        scratch_shapes=[pltpu.VMEM((N_HEADS, HEAD, HEAD), f32)], input_output_aliases={9: 0},
        compiler_params=_cparams(("arbitrary",)), name="hgrn_bwd")(proj, proj, proj, proj, lb, on_w, cm, states, d_out, dproj)


TM_MG = 320


def _const_spec(shape):
    nd = len(shape)
    return pl.BlockSpec(shape, lambda i: (0,) * nd, pipeline_mode=pl.Buffered(1))


def merge_fwd(osb, proj, ogd, ohg, wb, wo, h):
    t = h.shape[0]

    def body(osb_ref, ogd_ref, ohg_ref, zm_ref, wb_ref, wo_ref, h_ref, out_ref):
        a = osb_ref[...] * _silu(zm_ref[:, :BRANCH])
        gate = lambda b: _sigmoid(zm_ref[:, C_MIX + b * D_MODEL:C_MIX + (b + 1) * D_MODEL])
        y = (gate(0) * _dot(a, wb_ref[0]) + gate(1) * _dot(ogd_ref[...], wb_ref[1])
             + gate(2) * _dot(ohg_ref[...], wb_ref[2]))
        out_ref[...] = h_ref[...] + _dot(y, wo_ref[...])

    br = pl.BlockSpec((TM_MG, BRANCH), lambda i: (i, 0))
    return pl.pallas_call(
        body, grid=(t // TM_MG,),
        in_specs=[br, br, br, pl.BlockSpec((TM_MG, W_MERGE), lambda i: (i, 0)),
                  _const_spec((3, BRANCH, D_MODEL)), _const_spec((D_MODEL, D_MODEL)),
                  pl.BlockSpec((TM_MG, D_MODEL), lambda i: (i, 0))],
        out_specs=pl.BlockSpec((TM_MG, D_MODEL), lambda i: (i, 0)),
        out_shape=jax.ShapeDtypeStruct((t, D_MODEL), f32),
        compiler_params=_cparams(("arbitrary",)), name="merge_fwd")(osb, ogd, ohg, proj, wb, wo, h)


def merge_bwd(osb, proj, ogd, ohg, wb, wbt, wot, dh):
    t = dh.shape[0]

    def body(osb_ref, ogd_ref, ohg_ref, zm_ref, wb_ref, wbt_ref, wot_ref, dh_ref,
             dosb_ref, dogd_ref, dohg_ref, dzm_ref, dwo_ref, dwb_ref):
        i = pl.program_id(0)

        @pl.when(i == 0)
        def _():
            dwo_ref[...] = jnp.zeros_like(dwo_ref)
            dwb_ref[...] = jnp.zeros_like(dwb_ref)

        osb = osb_ref[...]
        sbz = zm_ref[:, :BRANCH]
        sgz = _sigmoid(sbz)
        sz = sbz * sgz
        branch_in = (osb * sz, ogd_ref[...], ohg_ref[...])
        dh_v = dh_ref[...]
        dy = _dot(dh_v, wot_ref[...])
        y = jnp.zeros((TM_MG, D_MODEL), f32)
        d_in = []
        for b in range(3):
            cols = slice(C_MIX + b * D_MODEL, C_MIX + (b + 1) * D_MODEL)
            p = _dot(branch_in[b], wb_ref[b])
            g = _sigmoid(zm_ref[:, cols])
            y = y + g * p
            dp = dy * g
            dzm_ref[:, cols] = dy * p * g * (1.0 - g)
            d_in.append(_dot(dp, wbt_ref[b]))
            dwb_ref[b] += _dot(branch_in[b], dp, TN)
        dwo_ref[...] += _dot(y, dh_v, TN)
        dosb_ref[...] = d_in[0] * sz
        dzm_ref[:, :BRANCH] = d_in[0] * osb * (sgz * (1.0 + sbz * (1.0 - sgz)))
        dogd_ref[...] = d_in[1]
        dohg_ref[...] = d_in[2]

    br = pl.BlockSpec((TM_MG, BRANCH), lambda i: (i, 0))
    zm = pl.BlockSpec((TM_MG, W_MERGE), lambda i: (i, 0))
    return pl.pallas_call(
        body, grid=(t // TM_MG,),
        in_specs=[br, br, br, zm,
                  _const_spec((3, BRANCH, D_MODEL)), _const_spec((3, D_MODEL, BRANCH)), _const_spec((D_MODEL, D_MODEL)),
                  pl.BlockSpec((TM_MG, D_MODEL), lambda i: (i, 0))],
        out_specs=[br, br, br, zm, _const_spec((D_MODEL, D_MODEL)), _const_spec((3, BRANCH, D_MODEL))],
        out_shape=[jax.ShapeDtypeStruct((t, BRANCH), f32)] * 3 + [jax.ShapeDtypeStruct((t, N_MAIN), f32),
                   jax.ShapeDtypeStruct((D_MODEL, D_MODEL), f32), jax.ShapeDtypeStruct((3, BRANCH, D_MODEL), f32)],
        compiler_params=_cparams(("arbitrary",)), name="merge_bwd")(osb, ogd, ohg, proj, wb, wbt, wot, dh)


def loss_head(h, target):
    t = h.shape[0]
    nb = t // SB_BLOCK

    def body(h_ref, t_ref, dh_ref, loss_ref):
        i = pl.program_id(0)

        @pl.when(i == 0)
        def _():
            loss_ref[...] = jnp.zeros_like(loss_ref)
            dh_ref[...] = jnp.zeros_like(dh_ref)

        @pl.when(i > 0)
        def _():
            err = h_ref[...] - t_ref[...]
            dh_ref[...] = err * (1.0 / D_MODEL)
            loss_ref[...] += jnp.broadcast_to(jnp.sum(err * err) * (0.5 / D_MODEL), loss_ref.shape)

    return pl.pallas_call(
        body, grid=(nb,),
        in_specs=[pl.BlockSpec((SB_BLOCK, D_MODEL), lambda i: (i, 0)),
                  pl.BlockSpec((SB_BLOCK, D_MODEL), lambda i: (jnp.maximum(i - 1, 0), 0))],
        out_specs=[pl.BlockSpec((SB_BLOCK, D_MODEL), lambda i: (i, 0)), pl.BlockSpec((1, 128), lambda i: (0, 0))],
        out_shape=[jax.ShapeDtypeStruct((t, D_MODEL), f32), jax.ShapeDtypeStruct((1, 128), f32)],
        compiler_params=_cparams(("arbitrary",)), name="loss_head")(h, target)


def adamw(parts, w, m, v, rows_per_step, name):
    r, c = w.shape
    tr = min(rows_per_step, r)

    def body(p_ref, w_ref, m_ref, v_ref, g_ref, d_ref, nm_ref, nv_ref):
        g = p_ref[0].astype(f32)
        for k in range(1, N_DEV):
            g = g + p_ref[k].astype(f32)
        m_new = ADAM_B1 * m_ref[...] + (1.0 - ADAM_B1) * g
        v_new = ADAM_B2 * v_ref[...] + (1.0 - ADAM_B2) * jnp.square(g)
        m_hat = m_new / (1.0 - ADAM_B1 ** ADAM_STEP)
        v_hat = v_new / (1.0 - ADAM_B2 ** ADAM_STEP)
        g_ref[...] = g
        d_ref[...] = -ADAM_LR * (m_hat / (jnp.sqrt(v_hat) + ADAM_EPS) + ADAM_WD * w_ref[...])
        nm_ref[...] = m_new
        nv_ref[...] = v_new

    blk = pl.BlockSpec((tr, c), lambda i: (i, 0))
    return pl.pallas_call(
        body, grid=(r // tr,),
        in_specs=[pl.BlockSpec((N_DEV, tr, c), lambda i: (0, i, 0)), blk, blk, blk],
        out_specs=[blk] * 4, out_shape=[jax.ShapeDtypeStruct((r, c), f32)] * 4,
        compiler_params=_cparams(("arbitrary",)), name=name)(parts, w, m, v)


def _mesh_pos():
    return lax.axis_index("x"), lax.axis_index("y"), lax.axis_index("c")


def _peer(pos, k):
    x, y, c = pos
    return (1 - x if k & 4 else x, 1 - y if k & 2 else y, 1 - c if k & 1 else c)


def _lin(pos):
    return 4 * pos[0] + 2 * pos[1] + pos[2]


def exchange(srcs, scatter, name):
    n = len(srcs)
    shapes = [s.shape[1:] if sc else s.shape for s, sc in zip(srcs, scatter)]

    def body(*refs):
        src_refs, dst_refs = refs[:n], refs[n:2 * n]
        send_sems, recv_sems, local_sems = refs[2 * n:]
        me = _mesh_pos()
        me_lin = _lin(me)
        sends, recvs, locals_ = [], [], []
        for t in range(n):
            own = src_refs[t].at[me_lin] if scatter[t] else src_refs[t]
            locals_.append(pltpu.make_async_copy(own, dst_refs[t].at[me_lin], local_sems.at[t]))
            for k in range(1, N_DEV):
                peer = _peer(me, k)
                src = src_refs[t].at[_lin(peer)] if scatter[t] else src_refs[t]
                sends.append(pltpu.make_async_remote_copy(
                    src_ref=src, dst_ref=dst_refs[t].at[me_lin], send_sem=send_sems.at[t, k - 1],
                    recv_sem=recv_sems.at[t, k - 1], device_id=peer, device_id_type=MESH))
                recvs.append(pltpu.make_async_remote_copy(
                    src_ref=src, dst_ref=dst_refs[t].at[_lin(peer)], send_sem=send_sems.at[t, k - 1],
                    recv_sem=recv_sems.at[t, k - 1], device_id=peer, device_id_type=MESH))
        for cp in locals_ + sends:
            cp.start()
        for cp in sends:
            cp.wait_send()
        for cp in recvs:
            cp.wait_recv()
        for cp in locals_:
            cp.wait()

    any_spec = pl.BlockSpec(memory_space=pl.ANY)
    return pl.pallas_call(
        body, in_specs=[any_spec] * n, out_specs=[any_spec] * n,
        out_shape=[jax.ShapeDtypeStruct((N_DEV,) + tuple(sh), s.dtype) for sh, s in zip(shapes, srcs)],
        scratch_shapes=[pltpu.SemaphoreType.DMA((n, N_DEV - 1)), pltpu.SemaphoreType.DMA((n, N_DEV - 1)),
                        pltpu.SemaphoreType.DMA((n,))],
        compiler_params=pltpu.CompilerParams(has_side_effects=True), name=name)(*srcs)


def gather_two_level(srcs, name):
    n = len(srcs)
    n_cp = N_DEV - 1

    def body(*refs):
        src_refs, dst_refs = refs[:n], refs[n:2 * n]
        send_sems, recv_sems, local_sems = refs[2 * n:]
        x, y, c = _mesh_pos()
        me, sibling = (x, y, c), (x, y, 1 - c)
        chips = [(1 - x, y), (x, 1 - y), (1 - x, 1 - y)]

        def copy(t, k, block, to, src=None):
            slot = dst_refs[t].at[_lin(block)]
            return pltpu.make_async_remote_copy(
                src_ref=slot if src is None else src, dst_ref=slot, send_sem=send_sems.at[t, k],
                recv_sem=recv_sems.at[t, k], device_id=to, device_id_type=MESH)

        mine, first, passed = [], [], []
        for t in range(n):
            mine.append(pltpu.make_async_copy(src_refs[t], dst_refs[t].at[_lin(me)], local_sems.at[t]))
            first.append(copy(t, 0, me, sibling, src=src_refs[t]))
            first += [copy(t, 1 + j, me, (*chip, c), src=src_refs[t]) for j, chip in enumerate(chips)]
        for cp in mine + first:
            cp.start()
        for j, chip in enumerate(chips):
            for t in range(n):
                copy(t, 1 + j, (*chip, c), me).wait_recv()
                fwd = copy(t, 4 + j, (*chip, c), sibling)
                fwd.start()
                passed.append(fwd)
        for t in range(n):
            copy(t, 0, sibling, me).wait_recv()
            for j, chip in enumerate(chips):
                copy(t, 4 + j, (*chip, 1 - c), me).wait_recv()
        for cp in first + passed:
            cp.wait_send()
        for cp in mine:
            cp.wait()

    any_spec = pl.BlockSpec(memory_space=pl.ANY)
    return pl.pallas_call(
        body, in_specs=[any_spec] * n, out_specs=[any_spec] * n,
        out_shape=[jax.ShapeDtypeStruct((N_DEV,) + tuple(s.shape), s.dtype) for s in srcs],
        scratch_shapes=[pltpu.SemaphoreType.DMA((n, n_cp)), pltpu.SemaphoreType.DMA((n, n_cp)),
                        pltpu.SemaphoreType.DMA((n,))],
        compiler_params=pltpu.CompilerParams(has_side_effects=True), name=name)(*srcs)


PACK_ROWS = 104


def _pad_rows(a, rows):
    return jnp.pad(a, ((0, rows - a.shape[0]), (0, 0)))


def _pad_lanes(a):
    return jnp.pad(a, ((0, 0), (0, 128 - a.shape[1])))


def _pack(norm_w, sbq, sbk, alog, dtb, gon, lbl, hon, loss_row):
    parts = [norm_w.reshape(32, 128), _pad_rows(sbq, 8), _pad_rows(sbk, 8), _pad_rows(_pad_lanes(alog), 8),
             _pad_rows(_pad_lanes(dtb), 8), _pad_rows(gon, 8), lbl.reshape(16, 128), _pad_rows(hon, 8),
             _pad_rows(loss_row, 8)]
    return jnp.concatenate(parts, axis=0)


def _unpack(p):
    return dict(norm_w=p[0:32].reshape(DEPTH, D_MODEL), sb_q_norm=p[32:36], sb_k_norm=p[40:44],
                gdn_a_log=p[48:52, :N_HEADS], gdn_dt_bias=p[56:60, :N_HEADS], gdn_out_norm=p[64:68],
                hgrn_lb_logits=p[72:88].reshape(DEPTH, BRANCH), hgrn_out_norm=p[88:92], loss=p[96, 0])


def _lower_bounds(logits):
    p = jax.nn.softmax(logits, axis=0)
    return jnp.cumsum(p, axis=0) - p[0:1]


def _unshard_cols(g):
    nd = g.ndim
    g = jnp.moveaxis(g, 0, nd - 2)
    return g.reshape(g.shape[:-2] + (N_DEV * g.shape[-1],))


def _shard_cols(a):
    n = a.shape[-1] // N_DEV
    return jnp.moveaxis(a.reshape(a.shape[:-1] + (N_DEV, n)), -2, 0)


def kernel(x, meta_tokens, norm_w, w_in, sb_q_norm, sb_k_norm, gdn_conv_w, gdn_a_log, gdn_dt_bias, gdn_out_norm, hgrn_lb_logits, hgrn_out_norm, w_branch, w_out, loss_target, m_meta_tokens, m_norm_w, m_w_in, m_sb_q_norm, m_sb_k_norm, m_gdn_conv_w, m_gdn_a_log, m_gdn_dt_bias, m_gdn_out_norm, m_hgrn_lb_logits, m_hgrn_out_norm, m_w_branch, m_w_out, v_meta_tokens, v_norm_w, v_w_in, v_sb_q_norm, v_sb_k_norm, v_gdn_conv_w, v_gdn_a_log, v_gdn_dt_bias, v_gdn_out_norm, v_hgrn_lb_logits, v_hgrn_out_norm, v_w_branch, v_w_out):
    g_win, g_wbr, g_wout, g_meta, g_conv = gather_two_level(
        [w_in.astype(bf16), w_branch.astype(bf16), w_out.astype(bf16), meta_tokens, gdn_conv_w], "gather_weights")
    w_full = _unshard_cols(g_win)
    w_main = jnp.concatenate([w_full[..., a:b] for a, b in W_IN_ORDER], axis=-1)
    w_small = jnp.pad(w_full[..., SMALL_OFF:SMALL_OFF + 8], ((0, 0), (0, 0), (0, N_SMALL - 8)))
    wt_main = jnp.swapaxes(w_main, 1, 2)
    wt_small = jnp.swapaxes(w_small, 1, 2)
    wbr = _unshard_cols(g_wbr)
    wbr_t = jnp.swapaxes(wbr, 2, 3)
    wout = jnp.moveaxis(g_wout, 0, 1).reshape(DEPTH, D_MODEL, D_MODEL)
    wout_t = jnp.swapaxes(wout, 1, 2)
    meta = _unshard_cols(g_meta)
    conv_w = _unshard_cols(g_conv)
    lbounds, lb_vjp = jax.vjp(_lower_bounds, hgrn_lb_logits)

    h = jnp.concatenate([jnp.zeros((PAD_FRONT, D_MODEL), f32), meta, x[0]], axis=0)
    row = lambda a: a.reshape(1, -1)
    saved = []
    for l in range(DEPTH):
        proj, small, _, xnt = inproj_fwd(h, row(norm_w[l]), w_main[l], w_small[l])
        osb = sb_fwd(proj, row(sb_q_norm[l]), row(sb_k_norm[l]))
        ypre = conv_fwd(proj, conv_w[l])
        al, dtb = _pad_lanes(row(gdn_a_log[l])), _pad_lanes(row(gdn_dt_bias[l]))
        ogd, gst = gdn_fwd(ypre, small, proj, al, dtb, row(gdn_out_norm[l]))
        ohg, hst = hgrn_fwd(proj, row(lbounds[l]), row(hgrn_out_norm[l]))
        h_next = merge_fwd(osb, proj, ogd, ohg, wbr[l], wout[l], h)
        saved.append((h, proj, small, xnt, osb, ypre, ogd, gst, ohg, hst, al, dtb))
        h = h_next

    dh, loss_row = loss_head(h, loss_target[0])

    gw_main, gw_small, gw_br, gw_out, g_conv_w = [None] * DEPTH, [None] * DEPTH, [None] * DEPTH, [None] * DEPTH, [None] * DEPTH
    g_norm, g_sbq, g_sbk, g_al, g_dt, g_gon, g_lb, g_hon = ([None] * DEPTH for _ in range(8))
    for l in reversed(range(DEPTH)):
        h_l, proj, small, xnt, osb, ypre, ogd, gst, ohg, hst, al, dtb = saved[l]
        d_osb, d_ogd, d_ohg, dproj, gw_out[l], gw_br[l] = merge_bwd(osb, proj, ogd, ohg, wbr[l], wbr_t[l], wout_t[l], dh)
        dproj, g_lb[l], g_hon[l] = hgrn_bwd(proj, row(lbounds[l]), row(hgrn_out_norm[l]), hst, d_ohg, dproj)
        d_ypre, d_small, dproj, g_al[l], g_dt[l], g_gon[l] = gdn_bwd(ypre, small, proj, al, dtb, row(gdn_out_norm[l]), gst, d_ogd, dproj)
        dproj, g_conv_w[l] = conv_bwd(proj, conv_w[l], d_ypre, dproj)
        dproj, g_sbq[l], g_sbk[l] = sb_bwd(proj, row(sb_q_norm[l]), row(sb_k_norm[l]), osb, d_osb, dproj)
        gw_main[l], gw_small[l] = inproj_bwd_w(xnt, dproj, d_small)
        dh, g_norm[l] = inproj_bwd_x(dproj, d_small, wt_main[l], wt_small[l], h_l, row(norm_w[l]), dh)

    gw_main, gw_small = jnp.stack(gw_main), jnp.stack(gw_small)
    starts = np.cumsum([0] + [b - a for a, b in W_IN_ORDER])
    pieces = sorted((a, gw_main[..., int(s):int(s) + b - a]) for (a, b), s in zip(W_IN_ORDER, starts))
    pieces.append((SMALL_OFF, gw_small[..., :8]))
    gw_in = jnp.concatenate([p for _, p in sorted(pieces, key=lambda ap: ap[0])], axis=-1)
    d_lbl = lb_vjp(jnp.concatenate(g_lb, axis=0))[0]
    cat = lambda rows: jnp.concatenate(rows, axis=0)
    pack = _pack(cat(g_norm), cat(g_sbq), cat(g_sbk), cat(g_al)[:, :N_HEADS], cat(g_dt)[:, :N_HEADS], cat(g_gon),
                 d_lbl, cat(g_hon), loss_row)
    g_meta_full = dh[PAD_FRONT:FRONT]
    r_win, r_wbr, r_wout, r_meta, r_conv, r_pack = exchange(
        [_shard_cols(gw_in).astype(bf16), _shard_cols(jnp.stack(gw_br)).astype(bf16),
         jnp.swapaxes(jnp.stack(gw_out).reshape(DEPTH, N_DEV, HEAD, D_MODEL), 0, 1).astype(bf16),
         _shard_cols(g_meta_full), _shard_cols(jnp.stack(g_conv_w)), pack],
        [True, True, True, True, True, False], "exchange_grads")

    def upd(parts, w, m, v, rows, name):
        shp = w.shape
        two = (-1, shp[-1])
        outs = adamw(parts.reshape((N_DEV,) + w.reshape(two).shape), w.reshape(two), m.reshape(two), v.reshape(two), rows, name)
        return [o.reshape(shp) for o in outs]

    res = {}
    res["w_in"] = upd(r_win, w_in, m_w_in, v_w_in, 256, "adamw_w_in")
    res["w_branch"] = upd(r_wbr, w_branch, m_w_branch, v_w_branch, 1024, "adamw_w_branch")
    res["w_out"] = upd(r_wout, w_out, m_w_out, v_w_out, 256, "adamw_w_out")
    res["meta_tokens"] = upd(r_meta, meta_tokens, m_meta_tokens, v_meta_tokens, 16, "adamw_meta")
    res["gdn_conv_w"] = upd(r_conv, gdn_conv_w, m_gdn_conv_w, v_gdn_conv_w, 16, "adamw_conv")
    zero_row = jnp.zeros((1, 128), f32)
    w_pack = _pack(norm_w, sb_q_norm, sb_k_norm, gdn_a_log, gdn_dt_bias, gdn_out_norm, hgrn_lb_logits, hgrn_out_norm, zero_row)
    m_pack = _pack(m_norm_w, m_sb_q_norm, m_sb_k_norm, m_gdn_a_log, m_gdn_dt_bias, m_gdn_out_norm, m_hgrn_lb_logits, m_hgrn_out_norm, zero_row)
    v_pack = _pack(v_norm_w, v_sb_q_norm, v_sb_k_norm, v_gdn_a_log, v_gdn_dt_bias, v_gdn_out_norm, v_hgrn_lb_logits, v_hgrn_out_norm, zero_row)
    packed = [_unpack(o) for o in adamw(r_pack, w_pack, m_pack, v_pack, PACK_ROWS, "adamw_replicated")]
    for name in ("norm_w", "sb_q_norm", "sb_k_norm", "gdn_a_log", "gdn_dt_bias", "gdn_out_norm", "hgrn_lb_logits", "hgrn_out_norm"):
        res[name] = [p[name] for p in packed]
    loss = packed[0]["loss"]
    grad_x = dh[FRONT:][None]

    order = ["meta_tokens", "norm_w", "w_in", "sb_q_norm", "sb_k_norm", "gdn_conv_w", "gdn_a_log", "gdn_dt_bias",
             "gdn_out_norm", "hgrn_lb_logits", "hgrn_out_norm", "w_branch", "w_out"]
    return (loss, grad_x, *[res[n][0] for n in order], *[res[n][1] for n in order],
            *[res[n][2] for n in order], *[res[n][3] for n in order])
```

```python
import functools

import numpy as np
import jax
import jax.numpy as jnp
from jax import lax
from jax.experimental import pallas as pl
from jax.experimental.pallas import tpu as pltpu

f32 = jnp.float32
bf16 = jnp.bfloat16

D_MODEL = 1024
BRANCH = 512
HEAD = 128
N_HEADS = 4
CHUNK = 64
SB_BLOCK = 128
N_META = 16
FRONT = 128
PAD_FRONT = 112
EPS = 1e-6
DEPTH = 4
N_DEV = 8
N_IN = 9224
N_MAIN = 9216
N_SMALL = 128
SMALL_OFF = 4096
C_SBZ, C_MIX = 0, 512
C_GZ = 3584
C_HQ, C_HF, C_HI, C_HZ = 4096, 4608, 5120, 5632
C_GQKV = 6144
C_SBQ, C_SBK, C_SBV = 7680, 8192, 8704
W_MERGE = BRANCH + 3 * D_MODEL
W_IN_ORDER = ((1536, 2048), (6152, 9224), (3584, 4096), (4104, 6152), (2048, 3584), (0, 1536))

ADAM_LR, ADAM_B1, ADAM_B2, ADAM_EPS, ADAM_WD, ADAM_STEP = 0.001, 0.9, 0.999, 1e-08, 0.01, 10

VMEM_LIMIT = 56 * 1024 * 1024
MESH = pl.DeviceIdType.MESH

NN = ((1,), (0,))
NT = ((1,), (1,))
TN = ((0,), (0,))


def _dot(a, b, dims=NN):
    return lax.dot_general(a.astype(bf16), b.astype(bf16), (dims, ((), ())), preferred_element_type=f32)


@jax.custom_vjp
def mm(a, b):
    return _dot(a, b, NN)


mm.defvjp(lambda a, b: (_dot(a, b, NN), (a, b)),
          lambda r, g: (_dot(g, r[1], NT), _dot(r[0], g, TN)))


@jax.custom_vjp
def mm_nt(a, b):
    return _dot(a, b, NT)


mm_nt.defvjp(lambda a, b: (_dot(a, b, NT), (a, b)),
             lambda r, g: (_dot(g, r[1], NN), _dot(g, r[0], TN)))


@jax.custom_vjp
def mm_tn(a, b):
    return _dot(a, b, TN)


mm_tn.defvjp(lambda a, b: (_dot(a, b, TN), (a, b)),
             lambda r, g: (_dot(r[1], g, NT), _dot(r[0], g, NN)))


def _split2(x):
    hi = x.astype(bf16)
    lo = (x - hi.astype(f32)).astype(bf16)
    return hi, lo


def _cdot(c, x, dims):
    hi, lo = _split2(x)
    return (lax.dot_general(c, hi, (dims, ((), ())), preferred_element_type=f32)
            + lax.dot_general(c, lo, (dims, ((), ())), preferred_element_type=f32))


@jax.custom_vjp
def cmm(c, x):
    return _cdot(c, x, NN)


cmm.defvjp(lambda c, x: (_cdot(c, x, NN), c),
           lambda c, g: (jnp.zeros_like(c), _cdot(c, g, TN)))


def _sigmoid(x):
    return jax.nn.sigmoid(x)


def _silu(x):
    return x * jax.nn.sigmoid(x)


def _softplus(x):
    return jnp.maximum(x, 0.0) + jnp.log(1.0 + jnp.exp(-jnp.abs(x)))


def _rms(x, w):
    return x * lax.rsqrt(jnp.mean(x * x, axis=-1, keepdims=True) + EPS) * w


def _cparams(sem=None):
    return pltpu.CompilerParams(dimension_semantics=sem, vmem_limit_bytes=VMEM_LIMIT)


TILES_FWD = (1664, 512)
TILES_BWD_X = (832, 1024)
TILES_BWD_W = (1664, 1024)


def _row_tile(t, want, unit):
    return max(d for d in range(unit, want + 1, unit) if t % d == 0)


def inproj_fwd(h, nw, w_main, w_small):
    t = h.shape[0]
    TM_IN, TN_IN = _row_tile(t, TILES_FWD[0], 128), TILES_FWD[1]

    def body(h_ref, nw_ref, w_ref, ws_ref, proj_ref, small_ref, xn_ref, xnt_ref):
        @pl.when(pl.program_id(1) == 0)
        def _():
            xn = _rms(h_ref[...], nw_ref[...])
            xn_ref[...] = xn.astype(bf16)
            xnt_ref[...] = jnp.transpose(xn).astype(bf16)
            small_ref[...] = _dot(xn, ws_ref[...])

        proj_ref[...] = jnp.dot(xn_ref[...], w_ref[...], preferred_element_type=f32)

    return pl.pallas_call(
        body, grid=(t // TM_IN, N_MAIN // TN_IN),
        in_specs=[pl.BlockSpec((TM_IN, D_MODEL), lambda i, j: (i, 0)),
                  pl.BlockSpec((1, D_MODEL), lambda i, j: (0, 0)),
                  pl.BlockSpec((D_MODEL, TN_IN), lambda i, j: (0, j)),
                  pl.BlockSpec((D_MODEL, N_SMALL), lambda i, j: (0, 0))],
        out_specs=[pl.BlockSpec((TM_IN, TN_IN), lambda i, j: (i, j)),
                   pl.BlockSpec((TM_IN, N_SMALL), lambda i, j: (i, 0)),
                   pl.BlockSpec((TM_IN, D_MODEL), lambda i, j: (i, 0)),
                   pl.BlockSpec((D_MODEL, TM_IN), lambda i, j: (0, i))],
        out_shape=[jax.ShapeDtypeStruct((t, N_MAIN), f32), jax.ShapeDtypeStruct((t, N_SMALL), f32),
                   jax.ShapeDtypeStruct((t, D_MODEL), bf16), jax.ShapeDtypeStruct((D_MODEL, t), bf16)],
        compiler_params=_cparams(("arbitrary", "arbitrary")), name="inproj_fwd")(h, nw, w_main, w_small)


def inproj_bwd_x(dproj, dsmall, wt_main, wt_small, h, nw, dh_out):
    t = h.shape[0]
    TM_IN, TN_IN = _row_tile(t, TILES_BWD_X[0], 64), TILES_BWD_X[1]
    nk = N_MAIN // TN_IN

    def body(dp_ref, ds_ref, wt_ref, wts_ref, h_ref, nw_ref, dho_ref, dhi_ref, dnw_ref, acc):
        i, k = pl.program_id(0), pl.program_id(1)

        @pl.when(k == 0)
        def _():
            acc[...] = _dot(ds_ref[...], wts_ref[...])

        acc[...] += _dot(dp_ref[...], wt_ref[...])

        @pl.when(k == nk - 1)
        def _():
            x = h_ref[...]
            r = lax.rsqrt(jnp.mean(x * x, axis=-1, keepdims=True) + EPS)
            xh = x * r
            dxn = acc[...]
            dxh = dxn * nw_ref[...]
            dhi_ref[...] = dho_ref[...] + r * (dxh - xh * jnp.mean(dxh * xh, axis=-1, keepdims=True))
            part = jnp.sum(dxn * xh, axis=0, keepdims=True)

            @pl.when(i == 0)
            def _():
                dnw_ref[...] = part

            @pl.when(i > 0)
            def _():
                dnw_ref[...] += part

    return pl.pallas_call(
        body, grid=(t // TM_IN, nk),
        in_specs=[pl.BlockSpec((TM_IN, TN_IN), lambda i, k: (i, k)),
                  pl.BlockSpec((TM_IN, N_SMALL), lambda i, k: (i, 0)),
                  pl.BlockSpec((TN_IN, D_MODEL), lambda i, k: (k, 0)),
                  pl.BlockSpec((N_SMALL, D_MODEL), lambda i, k: (0, 0)),
                  pl.BlockSpec((TM_IN, D_MODEL), lambda i, k: (i, 0)),
                  pl.BlockSpec((1, D_MODEL), lambda i, k: (0, 0)),
                  pl.BlockSpec((TM_IN, D_MODEL), lambda i, k: (i, 0))],
        out_specs=[pl.BlockSpec((TM_IN, D_MODEL), lambda i, k: (i, 0)),
                   pl.BlockSpec((1, D_MODEL), lambda i, k: (0, 0))],
        out_shape=[jax.ShapeDtypeStruct((t, D_MODEL), f32), jax.ShapeDtypeStruct((1, D_MODEL), f32)],
        scratch_shapes=[pltpu.VMEM((TM_IN, D_MODEL), f32)],
        compiler_params=_cparams(("arbitrary", "arbitrary")), name="inproj_bwd_x")(
            dproj, dsmall, wt_main, wt_small, h, nw, dh_out)


def inproj_bwd_w(xnt, dproj, dsmall):
    t = xnt.shape[1]
    TM_IN, TN_IN = _row_tile(t, TILES_BWD_W[0], 128), TILES_BWD_W[1]
    nt = t // TM_IN

    def body(xnt_ref, dp_ref, ds_ref, dw_ref, dws_ref):
        n, s = pl.program_id(0), pl.program_id(1)
        part = _dot(xnt_ref[...], dp_ref[...])

        @pl.when(s == 0)
        def _():
            dw_ref[...] = part

        @pl.when(s > 0)
        def _():
            dw_ref[...] += part

        @pl.when(n == 0)
        def _():
            ps = _dot(xnt_ref[...], ds_ref[...])

            @pl.when(s == 0)
            def _():
                dws_ref[...] = ps

            @pl.when(s > 0)
            def _():
                dws_ref[...] += ps

    return pl.pallas_call(
        body, grid=(N_MAIN // TN_IN, nt),
        in_specs=[pl.BlockSpec((D_MODEL, TM_IN), lambda n, s: (0, s)),
                  pl.BlockSpec((TM_IN, TN_IN), lambda n, s: (s, n)),
                  pl.BlockSpec((TM_IN, N_SMALL), lambda n, s: (s, 0))],
        out_specs=[pl.BlockSpec((D_MODEL, TN_IN), lambda n, s: (0, n)),
                   pl.BlockSpec((D_MODEL, N_SMALL), lambda n, s: (0, 0))],
        out_shape=[jax.ShapeDtypeStruct((D_MODEL, N_MAIN), f32), jax.ShapeDtypeStruct((D_MODEL, N_SMALL), f32)],
        compiler_params=_cparams(("arbitrary", "arbitrary")), name="inproj_bwd_w")(xnt, dproj, dsmall)


SB_SCALE = HEAD ** -0.5


SB_SUB = 4
SB_KS = SB_SUB * SB_BLOCK


def _sb_padded(t):
    return -(-t // SB_KS) * SB_KS


def _sb_prep(k_ref, v_ref, kw_ref, kn_scr, vb_scr, nb):
    def prep(b, c):
        rows = pl.ds(pl.multiple_of(b * SB_BLOCK, SB_BLOCK), SB_BLOCK)
        kn_scr[rows, :] = _rms(k_ref[rows, :], kw_ref[...]).astype(bf16)
        vb_scr[rows, :] = v_ref[rows, :].astype(bf16)
        return c

    lax.fori_loop(0, nb, prep, 0)
    pad = kn_scr.shape[0] - nb * SB_BLOCK
    if pad:
        kn_scr[nb * SB_BLOCK:, :] = jnp.zeros((pad, HEAD), bf16)
        vb_scr[nb * SB_BLOCK:, :] = jnp.zeros((pad, HEAD), bf16)


def _tri_ext(cmp):
    r = lax.broadcasted_iota(jnp.int32, (SB_BLOCK, 2 * SB_BLOCK), 0)
    c = lax.broadcasted_iota(jnp.int32, (SB_BLOCK, 2 * SB_BLOCK), 1)
    return jnp.where((c >= SB_BLOCK) | cmp(r, c), 1.0, 0.0).astype(bf16)


def _sb_suffix(x, carry, tri_ext):
    hi, lo = _split2(x)
    parts = [p[:, c * SB_BLOCK:(c + 1) * SB_BLOCK] for p in (hi, lo) for c in range(SB_SUB)]
    w = jnp.dot(jnp.concatenate(parts, axis=0), tri_ext, preferred_element_type=f32)
    outs = [None] * SB_SUB
    for c in reversed(range(SB_SUB)):
        blk = w[c * SB_BLOCK:(c + 1) * SB_BLOCK] + w[(SB_SUB + c) * SB_BLOCK:(SB_SUB + c + 1) * SB_BLOCK]
        outs[c] = carry + blk[:, :SB_BLOCK]
        carry = carry + blk[:, SB_BLOCK:]
    return jnp.concatenate(outs, axis=1), carry


def _sb_scores(qn, kt, i, jb, masked):
    z = lax.dot_general(qn, kt, (NT, ((), ())), preferred_element_type=f32) * SB_SCALE
    lsz = jnp.minimum(z, 0.0) - jnp.log(1.0 + jnp.exp(-jnp.abs(z)))
    lk = lsz - z
    mask = None
    if masked:
        t_idx = i * SB_BLOCK + lax.broadcasted_iota(jnp.int32, (SB_BLOCK, SB_KS), 0)
        s_idx = jb * SB_KS + lax.broadcasted_iota(jnp.int32, (SB_BLOCK, SB_KS), 1)
        mask = (s_idx < t_idx) & (s_idx >= PAD_FRONT)
        lk = jnp.where(mask, lk, 0.0)
    return mask, lsz, lk


SB_DEAD = -104.0


def _sb_walk(i, tile, carry):
    n = i // SB_SUB + 1
    live = lambda c: jnp.max(c[1]) > SB_DEAD
    carry = tile(n - 1, carry, True)
    _, carry = lax.while_loop(lambda st: (st[0] >= 1) & live(st[1]),
                              lambda st: (st[0] - 1, tile(st[0], st[1], False)), (n - 2, carry))
    return lax.cond((n >= 2) & live(carry), lambda c: tile(0, c, True), lambda c: c, carry)


def sb_fwd(proj, qw, kw):
    t = proj.shape[0]
    nb = t // SB_BLOCK

    def body(q_ref, k_ref, v_ref, qw_ref, kw_ref, o_ref, kn_scr, vb_scr):
        i = pl.program_id(1)

        @pl.when(i == 0)
        def _():
            _sb_prep(k_ref, v_ref, kw_ref, kn_scr, vb_scr, nb)

        qn = _rms(q_ref[...], qw_ref[...]).astype(bf16)
        u_ex = _tri_ext(lambda r, c: r > c)

        def tile(jb, carry, masked):
            acc, r_carry = carry
            rows = pl.ds(pl.multiple_of(jb * SB_KS, SB_KS), SB_KS)
            mask, lsz, lk = _sb_scores(qn, kn_scr[rows, :], i, jb, masked)
            passed, r_carry = _sb_suffix(lk, r_carry, u_ex)
            a = jnp.exp(lsz + passed)
            if masked:
                a = jnp.where(mask, a, 0.0)
            a_hi, a_lo = _split2(a)
            both = jnp.dot(jnp.concatenate([a_hi, a_lo], axis=0), vb_scr[rows, :], preferred_element_type=f32)
            return acc + (both[:SB_BLOCK] + both[SB_BLOCK:]), r_carry

        zeros = jnp.zeros((SB_BLOCK, HEAD), f32)
        acc, _ = _sb_walk(i, tile, (zeros, zeros))
        o_ref[...] = acc

    qb, cb, vb = C_SBQ // HEAD, C_SBK // HEAD, C_SBV // HEAD
    return pl.pallas_call(
        body, grid=(N_HEADS, nb),
        in_specs=[pl.BlockSpec((SB_BLOCK, HEAD), lambda h, i: (i, qb + h)),
                  pl.BlockSpec((t, HEAD), lambda h, i: (0, cb + h)),
                  pl.BlockSpec((t, HEAD), lambda h, i: (0, vb + h)),
                  pl.BlockSpec((1, HEAD), lambda h, i: (0, 0)),
                  pl.BlockSpec((1, HEAD), lambda h, i: (0, 0))],
        out_specs=pl.BlockSpec((SB_BLOCK, HEAD), lambda h, i: (i, h)),
        out_shape=jax.ShapeDtypeStruct((t, BRANCH), f32),
        scratch_shapes=[pltpu.VMEM((_sb_padded(t), HEAD), bf16), pltpu.VMEM((_sb_padded(t), HEAD), bf16)],
        compiler_params=_cparams(("arbitrary", "arbitrary")), name="sb_fwd")(proj, proj, proj, qw, kw)


def sb_bwd(proj, qw, kw, o, do, dproj):
    t = proj.shape[0]
    nb = t // SB_BLOCK

    def body(q_ref, k_ref, v_ref, qw_ref, kw_ref, o_ref, do_ref, _, dp_ref, dqw_ref, dkw_ref,
             kn_scr, vb_scr, dk_acc, dv_acc, dq_stage, dq_sems, kv_sems):
        h, i = pl.program_id(0), pl.program_id(1)
        step = h * nb + i
        slot = step % 2

        def dq_copy(sl, head):
            return pltpu.make_async_copy(
                dq_stage.at[sl], dp_ref.at[pl.ds(pl.multiple_of(i * SB_BLOCK, SB_BLOCK), SB_BLOCK),
                                           pl.ds(C_SBQ + head * HEAD, HEAD)], dq_sems.at[sl])

        @pl.when(i == 0)
        def _():
            _sb_prep(k_ref, v_ref, kw_ref, kn_scr, vb_scr, nb)
            dk_acc[...] = jnp.zeros_like(dk_acc)
            dv_acc[...] = jnp.zeros_like(dv_acc)

        @pl.when((i == 0) & (h == 0))
        def _():
            dqw_ref[...] = jnp.zeros_like(dqw_ref)
            dkw_ref[...] = jnp.zeros_like(dkw_ref)

        q = q_ref[...]
        rq = lax.rsqrt(jnp.mean(q * q, axis=-1, keepdims=True) + EPS)
        qh = q * rq
        qn = (qh * qw_ref[...]).astype(bf16)
        do_f = do_ref[...]
        dob = do_f.astype(bf16)
        d_row = jnp.sum(dob.astype(f32) * o_ref[...], axis=-1, keepdims=True)
        u_ex = _tri_ext(lambda r, c: r > c)
        u_in = _tri_ext(lambda r, c: r >= c)

        def tile(jb, carry, masked):
            dq, r_carry, f_carry = carry
            rows = pl.ds(pl.multiple_of(jb * SB_KS, SB_KS), SB_KS)
            kt = kn_scr[rows, :]
            vt = vb_scr[rows, :]
            mask, lsz, lk = _sb_scores(qn, kt, i, jb, masked)
            passed, r_carry = _sb_suffix(lk, r_carry, u_ex)
            a = jnp.exp(lsz + passed)
            if masked:
                a = jnp.where(mask, a, 0.0)
            da = lax.dot_general(dob, vt, (NT, ((), ())), preferred_element_type=f32)
            e = a * da
            e_suf, f_carry = _sb_suffix(e, f_carry, u_in)
            sg = jnp.exp(lsz)
            dz = (e * (1.0 - sg) - (d_row - e_suf) * sg) * SB_SCALE
            if masked:
                dz = jnp.where(mask, dz, 0.0)
            dzb = dz.astype(bf16)
            dq = dq + jnp.dot(dzb, kt, preferred_element_type=f32)
            dk_acc[rows, :] += lax.dot_general(dzb, qn, (TN, ((), ())), preferred_element_type=f32)
            dv_acc[rows, :] += lax.dot_general(a.astype(bf16), dob, (TN, ((), ())), preferred_element_type=f32)
            return dq, r_carry, f_carry

        zeros = jnp.zeros((SB_BLOCK, HEAD), f32)
        dqn, _, _ = _sb_walk(i, tile, (zeros, zeros, zeros))
        gq = dqn * qw_ref[...]
        dqw_ref[...] += jnp.sum(dqn * qh, axis=0, keepdims=True)

        @pl.when(step >= 2)
        def _():
            dq_copy(slot, 0).wait()

        dq_stage[slot] = rq * (gq - qh * jnp.mean(gq * qh, axis=-1, keepdims=True))
        for head in range(N_HEADS):
            @pl.when(h == head)
            def _(head=head):
                dq_copy(slot, head).start()

        @pl.when(i == nb - 1)
        def _():
            def fin(b, c):
                rows = pl.ds(pl.multiple_of(b * SB_BLOCK, SB_BLOCK), SB_BLOCK)
                kk = k_ref[rows, :]
                rk = lax.rsqrt(jnp.mean(kk * kk, axis=-1, keepdims=True) + EPS)
                kh = kk * rk
                dkn = dk_acc[rows, :]
                gk = dkn * kw_ref[...]
                dk_acc[rows, :] = rk * (gk - kh * jnp.mean(gk * kh, axis=-1, keepdims=True))
                dkw_ref[...] += jnp.sum(dkn * kh, axis=0, keepdims=True)
                return c

            lax.fori_loop(0, nb, fin, 0)
            for head in range(N_HEADS):
                @pl.when(h == head)
                def _(head=head):
                    outs = [pltpu.make_async_copy(acc.at[pl.ds(0, t)], dp_ref.at[:, pl.ds(c0 + head * HEAD, HEAD)],
                                                  kv_sems.at[n])
                            for n, (acc, c0) in enumerate(((dk_acc, C_SBK), (dv_acc, C_SBV)))]
                    for cp in outs:
                        cp.start()
                    for cp in outs:
                        cp.wait()

        @pl.when(step == N_HEADS * nb - 1)
        def _():
            dq_copy(1 - slot, 0).wait()
            dq_copy(slot, 0).wait()

    qb, cb, vb = C_SBQ // HEAD, C_SBK // HEAD, C_SBV // HEAD
    blk = pl.BlockSpec((SB_BLOCK, HEAD), lambda h, i: (i, h))
    wsp = pl.BlockSpec((1, HEAD), lambda h, i: (0, 0))
    any_spec = pl.BlockSpec(memory_space=pl.ANY)
    return pl.pallas_call(
        body, grid=(N_HEADS, nb),
        in_specs=[pl.BlockSpec((SB_BLOCK, HEAD), lambda h, i: (i, qb + h)),
                  pl.BlockSpec((t, HEAD), lambda h, i: (0, cb + h)),
                  pl.BlockSpec((t, HEAD), lambda h, i: (0, vb + h)), wsp, wsp, blk, blk, any_spec],
        out_specs=[any_spec, wsp, wsp],
        out_shape=[jax.ShapeDtypeStruct((t, N_MAIN), f32)] + [jax.ShapeDtypeStruct((1, HEAD), f32)] * 2,
        scratch_shapes=[pltpu.VMEM((_sb_padded(t), HEAD), bf16), pltpu.VMEM((_sb_padded(t), HEAD), bf16),
                        pltpu.VMEM((_sb_padded(t), HEAD), f32), pltpu.VMEM((_sb_padded(t), HEAD), f32),
                        pltpu.VMEM((2, SB_BLOCK, HEAD), f32), pltpu.SemaphoreType.DMA((2,)), pltpu.SemaphoreType.DMA((2,))],
        input_output_aliases={7: 0},
        compiler_params=_cparams(("arbitrary", "arbitrary")), name="sb_bwd")(proj, proj, proj, qw, kw, o, do, dproj)


TM_CONV = 640
CONV_W = 4
GQKV = 3 * BRANCH


def conv_fwd(proj, cw):
    t = proj.shape[0]
    halo_blocks = TM_CONV // 8

    def body(x0_ref, x1_ref, x2_ref, p0_ref, p1_ref, p2_ref, cw_ref, y_ref):
        i = pl.program_id(0)
        for s, (x_ref, p_ref) in enumerate(((x0_ref, p0_ref), (x1_ref, p1_ref), (x2_ref, p2_ref))):
            prev = jnp.where(i > 0, p_ref[...], 0.0)
            xx = jnp.concatenate([prev, x_ref[...]], axis=0)
            cols = slice(s * BRANCH, (s + 1) * BRANCH)
            y = xx[8:] * cw_ref[CONV_W - 1:CONV_W, cols]
            for k in range(CONV_W - 1):
                y = y + pltpu.roll(xx, CONV_W - 1 - k, 0)[8:] * cw_ref[k:k + 1, cols]
            y_ref[:, cols] = y

    c0 = C_GQKV // BRANCH
    xs = [pl.BlockSpec((TM_CONV, BRANCH), functools.partial(lambda i, s: (i, c0 + s), s=s)) for s in range(3)]
    ps = [pl.BlockSpec((8, BRANCH), functools.partial(lambda i, s: (jnp.maximum(i * halo_blocks - 1, 0), c0 + s), s=s))
          for s in range(3)]
    return pl.pallas_call(
        body, grid=(t // TM_CONV,),
        in_specs=xs + ps + [pl.BlockSpec((CONV_W, GQKV), lambda i: (0, 0))],
        out_specs=pl.BlockSpec((TM_CONV, GQKV), lambda i: (i, 0)),
        out_shape=jax.ShapeDtypeStruct((t, GQKV), f32),
        compiler_params=_cparams(("arbitrary",)), name="conv_fwd")(proj, proj, proj, proj, proj, proj, cw)


def conv_bwd(proj, cw, dy, dproj):
    t = proj.shape[0]
    nt = t // TM_CONV
    halo_blocks = TM_CONV // 8

    def body(x0_ref, x1_ref, x2_ref, p0_ref, p1_ref, p2_ref, cw_ref, dy_ref, dyn_ref, _, dx_ref, dw_ref):
        i = pl.program_id(0)

        @pl.when(i == 0)
        def _():
            dw_ref[...] = jnp.zeros_like(dw_ref)

        nxt = jnp.where(i < nt - 1, dyn_ref[...], 0.0)
        dyy = jnp.concatenate([dy_ref[...], nxt], axis=0)
        n_rows = TM_CONV + 8
        dx = dyy[:TM_CONV] * cw_ref[CONV_W - 1:CONV_W, :]
        for k in range(CONV_W - 1):
            sh = CONV_W - 1 - k
            dx = dx + pltpu.roll(dyy, n_rows - sh, 0)[:TM_CONV] * cw_ref[k:k + 1, :]
        dx_ref[...] = dx
        dy_c = dy_ref[...]
        for s, (x_ref, p_ref) in enumerate(((x0_ref, p0_ref), (x1_ref, p1_ref), (x2_ref, p2_ref))):
            prev = jnp.where(i > 0, p_ref[...], 0.0)
            xx = jnp.concatenate([prev, x_ref[...]], axis=0)
            cols = slice(s * BRANCH, (s + 1) * BRANCH)
            for k in range(CONV_W):
                sh = CONV_W - 1 - k
                xs = xx[8:] if sh == 0 else pltpu.roll(xx, sh, 0)[8:]
                dw_ref[k:k + 1, cols] += jnp.sum(xs * dy_c[:, cols], axis=0, keepdims=True)

    c0 = C_GQKV // BRANCH
    xs = [pl.BlockSpec((TM_CONV, BRANCH), functools.partial(lambda i, s: (i, c0 + s), s=s)) for s in range(3)]
    ps = [pl.BlockSpec((8, BRANCH), functools.partial(lambda i, s: (jnp.maximum(i * halo_blocks - 1, 0), c0 + s), s=s))
          for s in range(3)]
    return pl.pallas_call(
        body, grid=(nt,),
        in_specs=xs + ps + [pl.BlockSpec((CONV_W, GQKV), lambda i: (0, 0)),
                            pl.BlockSpec((TM_CONV, GQKV), lambda i: (i, 0)),
                            pl.BlockSpec((8, GQKV), lambda i: (jnp.minimum((i + 1) * halo_blocks, nt * halo_blocks - 1), 0)),
                            pl.BlockSpec(memory_space=pl.ANY)],
        out_specs=[pl.BlockSpec((TM_CONV, GQKV), lambda i: (i, C_GQKV // GQKV)), pl.BlockSpec((CONV_W, GQKV), lambda i: (0, 0))],
        out_shape=[jax.ShapeDtypeStruct((t, N_MAIN), f32), jax.ShapeDtypeStruct((CONV_W, GQKV), f32)],
        input_output_aliases={9: 0},
        compiler_params=_cparams(("arbitrary",)), name="conv_bwd")(proj, proj, proj, proj, proj, proj, cw, dy, dy, dproj)


def _iota2(n, m, d):
    return lax.broadcasted_iota(jnp.int32, (n, m), d)


def _lane_pick(row_or_mat, idx):
    lanes = lax.broadcasted_iota(jnp.int32, row_or_mat.shape, row_or_mat.ndim - 1)
    return jnp.sum(jnp.where(lanes == idx, row_or_mat, 0.0), axis=-1, keepdims=True)


def _cumsum_consts():
    i = np.arange(CHUNK)
    incl = i[None, :] <= i[:, None]
    suf = i[None, :] > i[:, None]
    return np.concatenate([incl, suf], 0).astype(np.float32)


HG_LEVELS = (64, 32, 16, 8, 4, 2)


def _hgrn_consts():
    i = np.arange(CHUNK)
    rows = [i[None, :] <= i[:, None], i[None, :] > i[:, None]]
    for b in HG_LEVELS:
        ref = (i // b) * b + b // 2 - 1
        second = (i % b) >= b // 2
        rows.append((i[None, :] > ref[:, None]) & (i[None, :] <= i[:, None]) & second[:, None])
        rows.append((i[None, :] > i[:, None]) & (i[None, :] <= ref[:, None]) & (~second)[:, None])
    return np.concatenate(rows, 0).astype(np.float32)


N_SQUARINGS = 5


def _solve_chain(m, rhs):
    x = rhs - mm(m, rhs)
    powers = [m]
    for _ in range(N_SQUARINGS):
        powers.append(mm(powers[-1], powers[-1]))
        x = x + mm(powers[-1], x)
    return x, powers


@jax.custom_vjp
def unit_lower_solve(m, rhs):
    return _solve_chain(m, rhs)[0]


def _solve_fwd(m, rhs):
    x, powers = _solve_chain(m, rhs)
    return x, (powers, x)


def _solve_bwd(res, g):
    powers, x = res
    y = g - mm_tn(powers[0], g)
    for p in powers[1:]:
        y = y + mm_tn(p, y)
    return -mm_nt(y, x), y


unit_lower_solve.defvjp(_solve_fwd, _solve_bwd)

HC = N_HEADS * CHUNK
BATCH0 = ((0,), (0,))


def _bdot(a, b, contract):
    return lax.dot_general(a.astype(bf16), b.astype(bf16), (contract, BATCH0), preferred_element_type=f32)


B_NN = ((2,), (1,))
B_NT = ((2,), (2,))
B_TN = ((1,), (1,))


@jax.custom_vjp
def bmm(a, b):
    return _bdot(a, b, B_NN)


bmm.defvjp(lambda a, b: (_bdot(a, b, B_NN), (a, b)),
           lambda r, g: (_bdot(g, r[1], B_NT), _bdot(r[0], g, B_TN)))


@jax.custom_vjp
def bmm_nt(a, b):
    return _bdot(a, b, B_NT)


bmm_nt.defvjp(lambda a, b: (_bdot(a, b, B_NT), (a, b)),
              lambda r, g: (_bdot(g, r[1], B_NN), _bdot(g, r[0], B_TN)))


@jax.custom_vjp
def bmm_tn(a, b):
    return _bdot(a, b, B_TN)


bmm_tn.defvjp(lambda a, b: (_bdot(a, b, B_TN), (a, b)),
              lambda r, g: (_bdot(r[1], g, B_NT), _bdot(r[0], g, B_NN)))


def _stack_heads(x):
    return jnp.concatenate([x[:, h * HEAD:(h + 1) * HEAD] for h in range(N_HEADS)], axis=0)


def _unstack_heads(x):
    return jnp.concatenate([x[h * CHUNK:(h + 1) * CHUNK] for h in range(N_HEADS)], axis=1)


def _gdn_chunk(state, ypre, small, gz, a_log, dt_b, on_w, c2, vm):
    r = _iota2(HC, HC, 0)
    c = _iota2(HC, HC, 1)
    same_head = (r >> 6) == (c >> 6)
    causal = same_head & (r >= c)
    strict = same_head & (r > c)
    q = _silu(_stack_heads(ypre[:, :BRANCH]))
    k = _silu(_stack_heads(ypre[:, BRANCH:2 * BRANCH]))
    v = _silu(_stack_heads(ypre[:, 2 * BRANCH:]))
    q = q * lax.rsqrt(jnp.sum(q * q, axis=-1, keepdims=True) + EPS) * (HEAD ** -0.5)
    k = k * lax.rsqrt(jnp.sum(k * k, axis=-1, keepdims=True) + EPS)
    col = lambda f: jnp.concatenate([f(h) for h in range(N_HEADS)], axis=0)
    chunk_col = lambda x: jnp.broadcast_to(x, (CHUNK, 1))
    beta = _sigmoid(col(lambda h: _lane_pick(small, h))) * col(lambda h: vm)
    g = (-jnp.exp(col(lambda h: chunk_col(_lane_pick(a_log, h))))
         * _softplus(col(lambda h: _lane_pick(small, N_HEADS + h)) + col(lambda h: chunk_col(_lane_pick(dt_b, h)))))
    g_l = _unstack_heads(jnp.broadcast_to(g, (HC, HEAD)))
    e2 = cmm(c2, g_l)
    gc = _stack_heads(e2[:CHUNK])
    gsuf = _stack_heads(e2[CHUNK:])
    g_row = jnp.broadcast_to(jnp.transpose(gc)[0:1, :], (HC, HC))
    g_col = jnp.concatenate([gc, gc], axis=1)
    dec = jnp.where(causal, jnp.exp(jnp.minimum(g_col - g_row, 0.0)), 0.0)
    kb = k * beta
    m = jnp.where(strict, mm_nt(kb, k) * dec, 0.0)
    x = unit_lower_solve(m, jnp.concatenate([v * beta, kb * jnp.exp(gc)], axis=1))
    u, w = x[:, :HEAD], x[:, HEAD:]
    aqk = jnp.where(causal, mm_nt(q, k) * dec, 0.0)
    tot = jnp.sum(g_l, axis=0, keepdims=True)
    g_last = jnp.exp(jnp.stack([tot[:, h * HEAD:(h + 1) * HEAD] for h in range(N_HEADS)], axis=0))
    per_head = lambda a: a.reshape(N_HEADS, CHUNK, HEAD)
    v_new = u - bmm(per_head(w), state).reshape(HC, HEAD)
    o = bmm(per_head(q * jnp.exp(gc)), state).reshape(HC, HEAD) + mm(aqk, v_new)
    new_state = state * g_last + bmm_tn(per_head(k * jnp.exp(gsuf)), per_head(v_new))
    out = _unstack_heads(_rms(o, on_w)) * _silu(gz)
    return new_state, out


def _hgrn_chunk(state, hq, hf, hi, hz, lb, on_w, cm, vm):
    r = _iota2(HC, HC, 0)
    c = _iota2(HC, HC, 1)
    forget = lb + (1.0 - lb) * _sigmoid(hf)
    g_l = jnp.log(forget)
    e = cmm(cm, g_l)
    q = _stack_heads(_silu(hq))
    k = _stack_heads((1.0 - lb) * _sigmoid(-hf))
    v = _stack_heads(hi * vm)
    sect = lambda n: _stack_heads(e[n * CHUNK:(n + 1) * CHUNK])
    gc, gsuf = sect(0), sect(1)
    per_head = lambda a: a.reshape(N_HEADS, CHUNK, HEAD)
    o = bmm_nt(per_head(q * jnp.exp(gc)), state).reshape(HC, HEAD)
    a = jnp.where(r == c, jnp.sum(q * k, axis=-1, keepdims=True), 0.0)
    for li, b in enumerate(HG_LEVELS):
        sh = b.bit_length() - 1
        pair = ((r >> sh) == (c >> sh)) & ((r & (b - 1)) >= b // 2) & ((c & (b - 1)) < b // 2)
        a = a + jnp.where(pair, mm_nt(q * jnp.exp(sect(2 + 2 * li)), k * jnp.exp(sect(3 + 2 * li))), 0.0)
    o = o + mm(a, v)
    tot = jnp.sum(g_l, axis=0, keepdims=True)
    g_end = jnp.exp(jnp.stack([tot[:, h * HEAD:(h + 1) * HEAD] for h in range(N_HEADS)], axis=0))
    new_state = state * g_end + bmm_tn(per_head(v), per_head(k * jnp.exp(gsuf)))
    out = _unstack_heads(_rms(o, on_w)) * _silu(hz)
    return new_state, out


def _vmask(chunk_idx):
    rows = chunk_idx * CHUNK + lax.broadcasted_iota(jnp.int32, (CHUNK, 1), 0)
    return jnp.where(rows >= PAD_FRONT, 1.0, 0.0)


def _row(n):
    return pl.BlockSpec((1, n), lambda i: (0, 0))


def gdn_fwd(ypre, small, proj, a_log, dt_b, on_w):
    t = ypre.shape[0]
    nc = t // CHUNK
    c2 = jnp.asarray(_cumsum_consts(), bf16)

    def body(y_ref, s_ref, z_ref, al_ref, dt_ref, on_ref, c2_ref, o_ref, st_ref, state):
        i = pl.program_id(0)

        @pl.when(i == 0)
        def _():
            state[...] = jnp.zeros_like(state)

        s_in = state[...]
        st_ref[0] = s_in
        s_new, out = _gdn_chunk(s_in, y_ref[...], s_ref[...], z_ref[...], al_ref[...], dt_ref[...], on_ref[...],
                                c2_ref[...], _vmask(i))
        state[...] = s_new
        o_ref[...] = out

    return pl.pallas_call(
        body, grid=(nc,),
        in_specs=[pl.BlockSpec((CHUNK, GQKV), lambda i: (i, 0)), pl.BlockSpec((CHUNK, N_SMALL), lambda i: (i, 0)),
                  pl.BlockSpec((CHUNK, BRANCH), lambda i: (i, C_GZ // BRANCH)), _row(128), _row(128), _row(128),
                  pl.BlockSpec((2 * CHUNK, CHUNK), lambda i: (0, 0))],
        out_specs=[pl.BlockSpec((CHUNK, BRANCH), lambda i: (i, 0)),
                   pl.BlockSpec((1, N_HEADS, HEAD, HEAD), lambda i: (i, 0, 0, 0))],
        out_shape=[jax.ShapeDtypeStruct((t, BRANCH), f32), jax.ShapeDtypeStruct((nc, N_HEADS, HEAD, HEAD), f32)],
        scratch_shapes=[pltpu.VMEM((N_HEADS, HEAD, HEAD), f32)],
        compiler_params=_cparams(("arbitrary",)), name="gdn_fwd")(ypre, small, proj, a_log, dt_b, on_w, c2)


def gdn_bwd(ypre, small, proj, a_log, dt_b, on_w, states, d_out, dproj):
    t = ypre.shape[0]
    nc = t // CHUNK
    c2 = jnp.asarray(_cumsum_consts(), bf16)

    def body(y_ref, s_ref, z_ref, al_ref, dt_ref, on_ref, c2_ref, st_ref, do_ref, _,
             dy_ref, ds_ref, dz_ref, dal_ref, ddt_ref, don_ref, dstate):
        i = pl.program_id(0)

        @pl.when(i == 0)
        def _():
            dstate[...] = jnp.zeros_like(dstate)
            dal_ref[...] = jnp.zeros_like(dal_ref)
            ddt_ref[...] = jnp.zeros_like(ddt_ref)
            don_ref[...] = jnp.zeros_like(don_ref)

        vm = _vmask(nc - 1 - i)
        c2v = c2_ref[...]
        fn = lambda s, y, sm, z, al, dt, on: _gdn_chunk(s, y, sm, z, al, dt, on, c2v, vm)
        _, vjp = jax.vjp(fn, st_ref[0], y_ref[...], s_ref[...], z_ref[...], al_ref[...], dt_ref[...], on_ref[...])
        d_s, d_y, d_sm, d_z, d_al, d_dt, d_on = vjp((dstate[...], do_ref[...]))
        dstate[...] = d_s
        dy_ref[...] = d_y
        ds_ref[...] = d_sm
        dz_ref[...] = d_z
        dal_ref[...] += d_al
        ddt_ref[...] += d_dt
        don_ref[...] += d_on

    rev = lambda i: (nc - 1 - i, 0)
    return pl.pallas_call(
        body, grid=(nc,),
        in_specs=[pl.BlockSpec((CHUNK, GQKV), rev), pl.BlockSpec((CHUNK, N_SMALL), rev),
                  pl.BlockSpec((CHUNK, BRANCH), lambda i: (nc - 1 - i, C_GZ // BRANCH)), _row(128), _row(128), _row(128),
                  pl.BlockSpec((2 * CHUNK, CHUNK), lambda i: (0, 0)),
                  pl.BlockSpec((1, N_HEADS, HEAD, HEAD), lambda i: (nc - 1 - i, 0, 0, 0)),
                  pl.BlockSpec((CHUNK, BRANCH), rev), pl.BlockSpec(memory_space=pl.ANY)],
        out_specs=[pl.BlockSpec((CHUNK, GQKV), rev), pl.BlockSpec((CHUNK, N_SMALL), rev),
                   pl.BlockSpec((CHUNK, BRANCH), lambda i: (nc - 1 - i, C_GZ // BRANCH)), _row(128), _row(128), _row(128)],
        out_shape=[jax.ShapeDtypeStruct((t, GQKV), f32), jax.ShapeDtypeStruct((t, N_SMALL), f32),
                   jax.ShapeDtypeStruct((t, N_MAIN), f32)] + [jax.ShapeDtypeStruct((1, 128), f32)] * 3,
        scratch_shapes=[pltpu.VMEM((N_HEADS, HEAD, HEAD), f32)], input_output_aliases={9: 2},
        compiler_params=_cparams(("arbitrary",)), name="gdn_bwd")(
            ypre, small, proj, a_log, dt_b, on_w, c2, states, d_out, dproj)


def hgrn_fwd(proj, lb, on_w):
    t = proj.shape[0]
    nc = t // CHUNK
    cm = jnp.asarray(_hgrn_consts(), bf16)
    ncm = cm.shape[0]

    def body(q_ref, f_ref, i_ref, z_ref, lb_ref, on_ref, cm_ref, o_ref, st_ref, state):
        i = pl.program_id(0)

        @pl.when(i == 0)
        def _():
            state[...] = jnp.zeros_like(state)

        s_in = state[...]
        st_ref[0] = s_in
        s_new, out = _hgrn_chunk(s_in, q_ref[...], f_ref[...], i_ref[...], z_ref[...], lb_ref[...], on_ref[...],
                                 cm_ref[...], _vmask(i))
        state[...] = s_new
        o_ref[...] = out

    sec = lambda off: pl.BlockSpec((CHUNK, BRANCH), functools.partial(lambda i, b: (i, b), b=off // BRANCH))
    return pl.pallas_call(
        body, grid=(nc,),
        in_specs=[sec(C_HQ), sec(C_HF), sec(C_HI), sec(C_HZ), _row(BRANCH), _row(128),
                  pl.BlockSpec((ncm, CHUNK), lambda i: (0, 0))],
        out_specs=[pl.BlockSpec((CHUNK, BRANCH), lambda i: (i, 0)),
                   pl.BlockSpec((1, N_HEADS, HEAD, HEAD), lambda i: (i, 0, 0, 0))],
        out_shape=[jax.ShapeDtypeStruct((t, BRANCH), f32), jax.ShapeDtypeStruct((nc, N_HEADS, HEAD, HEAD), f32)],
        scratch_shapes=[pltpu.VMEM((N_HEADS, HEAD, HEAD), f32)],
        compiler_params=_cparams(("arbitrary",)), name="hgrn_fwd")(proj, proj, proj, proj, lb, on_w, cm)


def hgrn_bwd(proj, lb, on_w, states, d_out, dproj):
    t = proj.shape[0]
    nc = t // CHUNK
    cm = jnp.asarray(_hgrn_consts(), bf16)
    ncm = cm.shape[0]

    def body(q_ref, f_ref, i_ref, z_ref, lb_ref, on_ref, cm_ref, st_ref, do_ref, _, dh_ref, dlb_ref, don_ref, dstate):
        i = pl.program_id(0)

        @pl.when(i == 0)
        def _():
            dstate[...] = jnp.zeros_like(dstate)
            dlb_ref[...] = jnp.zeros_like(dlb_ref)
            don_ref[...] = jnp.zeros_like(don_ref)

        vm = _vmask(nc - 1 - i)
        cmv = cm_ref[...]
        fn = lambda s, a, b, c, d, l, on: _hgrn_chunk(s, a, b, c, d, l, on, cmv, vm)
        _, vjp = jax.vjp(fn, st_ref[0], q_ref[...], f_ref[...], i_ref[...], z_ref[...], lb_ref[...], on_ref[...])
        d_s, d_q, d_f, d_i, d_z, d_lb, d_on = vjp((dstate[...], do_ref[...]))
        dstate[...] = d_s
        dh_ref[...] = jnp.concatenate([d_q, d_f, d_i, d_z], axis=1)
        dlb_ref[...] += d_lb
        don_ref[...] += d_on

    rev = lambda i: (nc - 1 - i, 0)
    sec = lambda off: pl.BlockSpec((CHUNK, BRANCH), functools.partial(lambda i, b: (nc - 1 - i, b), b=off // BRANCH))
    return pl.pallas_call(
        body, grid=(nc,),
        in_specs=[sec(C_HQ), sec(C_HF), sec(C_HI), sec(C_HZ), _row(BRANCH), _row(128),
                  pl.BlockSpec((ncm, CHUNK), lambda i: (0, 0)),
                  pl.BlockSpec((1, N_HEADS, HEAD, HEAD), lambda i: (nc - 1 - i, 0, 0, 0)),
                  pl.BlockSpec((CHUNK, BRANCH), rev), pl.BlockSpec(memory_space=pl.ANY)],
        out_specs=[pl.BlockSpec((CHUNK, 4 * BRANCH), lambda i: (nc - 1 - i, C_HQ // (4 * BRANCH))), _row(BRANCH), _row(128)],
        out_shape=[jax.ShapeDtypeStruct((t, N_MAIN), f32), jax.ShapeDtypeStruct((1, BRANCH), f32),
                   jax.ShapeDtypeStruct((1, 128), f32)],
        scratch_shapes=[pltpu.VMEM((N_HEADS, HEAD, HEAD), f32)], input_output_aliases={9: 0},
        compiler_params=_cparams(("arbitrary",)), name="hgrn_bwd")(proj, proj, proj, proj, lb, on_w, cm, states, d_out, dproj)


TM_MG = 320


def _const_spec(shape):
    nd = len(shape)
    return pl.BlockSpec(shape, lambda i: (0,) * nd, pipeline_mode=pl.Buffered(1))


def merge_fwd(osb, proj, ogd, ohg, wb, wo, h):
    t = h.shape[0]

    def body(osb_ref, ogd_ref, ohg_ref, zm_ref, wb_ref, wo_ref, h_ref, out_ref):
        a = osb_ref[...] * _silu(zm_ref[:, :BRANCH])
        gate = lambda b: _sigmoid(zm_ref[:, C_MIX + b * D_MODEL:C_MIX + (b + 1) * D_MODEL])
        y = (gate(0) * _dot(a, wb_ref[0]) + gate(1) * _dot(ogd_ref[...], wb_ref[1])
             + gate(2) * _dot(ohg_ref[...], wb_ref[2]))
        out_ref[...] = h_ref[...] + _dot(y, wo_ref[...])

    br = pl.BlockSpec((TM_MG, BRANCH), lambda i: (i, 0))
    return pl.pallas_call(
        body, grid=(t // TM_MG,),
        in_specs=[br, br, br, pl.BlockSpec((TM_MG, W_MERGE), lambda i: (i, 0)),
                  _const_spec((3, BRANCH, D_MODEL)), _const_spec((D_MODEL, D_MODEL)),
                  pl.BlockSpec((TM_MG, D_MODEL), lambda i: (i, 0))],
        out_specs=pl.BlockSpec((TM_MG, D_MODEL), lambda i: (i, 0)),
        out_shape=jax.ShapeDtypeStruct((t, D_MODEL), f32),
        compiler_params=_cparams(("arbitrary",)), name="merge_fwd")(osb, ogd, ohg, proj, wb, wo, h)


def merge_bwd(osb, proj, ogd, ohg, wb, wbt, wot, dh):
    t = dh.shape[0]

    def body(osb_ref, ogd_ref, ohg_ref, zm_ref, wb_ref, wbt_ref, wot_ref, dh_ref,
             dosb_ref, dogd_ref, dohg_ref, dzm_ref, dwo_ref, dwb_ref):
        i = pl.program_id(0)

        @pl.when(i == 0)
        def _():
            dwo_ref[...] = jnp.zeros_like(dwo_ref)
            dwb_ref[...] = jnp.zeros_like(dwb_ref)

        osb = osb_ref[...]
        sbz = zm_ref[:, :BRANCH]
        sgz = _sigmoid(sbz)
        sz = sbz * sgz
        branch_in = (osb * sz, ogd_ref[...], ohg_ref[...])
        dh_v = dh_ref[...]
        dy = _dot(dh_v, wot_ref[...])
        y = jnp.zeros((TM_MG, D_MODEL), f32)
        d_in = []
        for b in range(3):
            cols = slice(C_MIX + b * D_MODEL, C_MIX + (b + 1) * D_MODEL)
            p = _dot(branch_in[b], wb_ref[b])
            g = _sigmoid(zm_ref[:, cols])
            y = y + g * p
            dp = dy * g
            dzm_ref[:, cols] = dy * p * g * (1.0 - g)
            d_in.append(_dot(dp, wbt_ref[b]))
            dwb_ref[b] += _dot(branch_in[b], dp, TN)
        dwo_ref[...] += _dot(y, dh_v, TN)
        dosb_ref[...] = d_in[0] * sz
        dzm_ref[:, :BRANCH] = d_in[0] * osb * (sgz * (1.0 + sbz * (1.0 - sgz)))
        dogd_ref[...] = d_in[1]
        dohg_ref[...] = d_in[2]

    br = pl.BlockSpec((TM_MG, BRANCH), lambda i: (i, 0))
    zm = pl.BlockSpec((TM_MG, W_MERGE), lambda i: (i, 0))
    return pl.pallas_call(
        body, grid=(t // TM_MG,),
        in_specs=[br, br, br, zm,
                  _const_spec((3, BRANCH, D_MODEL)), _const_spec((3, D_MODEL, BRANCH)), _const_spec((D_MODEL, D_MODEL)),
                  pl.BlockSpec((TM_MG, D_MODEL), lambda i: (i, 0))],
        out_specs=[br, br, br, zm, _const_spec((D_MODEL, D_MODEL)), _const_spec((3, BRANCH, D_MODEL))],
        out_shape=[jax.ShapeDtypeStruct((t, BRANCH), f32)] * 3 + [jax.ShapeDtypeStruct((t, N_MAIN), f32),
                   jax.ShapeDtypeStruct((D_MODEL, D_MODEL), f32), jax.ShapeDtypeStruct((3, BRANCH, D_MODEL), f32)],
        compiler_params=_cparams(("arbitrary",)), name="merge_bwd")(osb, ogd, ohg, proj, wb, wbt, wot, dh)


def loss_head(h, target):
    t = h.shape[0]
    nb = t // SB_BLOCK

    def body(h_ref, t_ref, dh_ref, loss_ref):
        i = pl.program_id(0)

        @pl.when(i == 0)
        def _():
            loss_ref[...] = jnp.zeros_like(loss_ref)
            dh_ref[...] = jnp.zeros_like(dh_ref)

        @pl.when(i > 0)
        def _():
            err = h_ref[...] - t_ref[...]
            dh_ref[...] = err * (1.0 / D_MODEL)
            loss_ref[...] += jnp.broadcast_to(jnp.sum(err * err) * (0.5 / D_MODEL), loss_ref.shape)

    return pl.pallas_call(
        body, grid=(nb,),
        in_specs=[pl.BlockSpec((SB_BLOCK, D_MODEL), lambda i: (i, 0)),
                  pl.BlockSpec((SB_BLOCK, D_MODEL), lambda i: (jnp.maximum(i - 1, 0), 0))],
        out_specs=[pl.BlockSpec((SB_BLOCK, D_MODEL), lambda i: (i, 0)), pl.BlockSpec((1, 128), lambda i: (0, 0))],
        out_shape=[jax.ShapeDtypeStruct((t, D_MODEL), f32), jax.ShapeDtypeStruct((1, 128), f32)],
        compiler_params=_cparams(("arbitrary",)), name="loss_head")(h, target)


def adamw(parts, w, m, v, rows_per_step, name):
    r, c = w.shape
    tr = min(rows_per_step, r)

    def body(p_ref, w_ref, m_ref, v_ref, g_ref, d_ref, nm_ref, nv_ref):
        g = p_ref[0].astype(f32)
        for k in range(1, N_DEV):
            g = g + p_ref[k].astype(f32)
        m_new = ADAM_B1 * m_ref[...] + (1.0 - ADAM_B1) * g
        v_new = ADAM_B2 * v_ref[...] + (1.0 - ADAM_B2) * jnp.square(g)
        m_hat = m_new / (1.0 - ADAM_B1 ** ADAM_STEP)
        v_hat = v_new / (1.0 - ADAM_B2 ** ADAM_STEP)
        g_ref[...] = g
        d_ref[...] = -ADAM_LR * (m_hat / (jnp.sqrt(v_hat) + ADAM_EPS) + ADAM_WD * w_ref[...])
        nm_ref[...] = m_new
        nv_ref[...] = v_new

    blk = pl.BlockSpec((tr, c), lambda i: (i, 0))
    return pl.pallas_call(
        body, grid=(r // tr,),
        in_specs=[pl.BlockSpec((N_DEV, tr, c), lambda i: (0, i, 0)), blk, blk, blk],
        out_specs=[blk] * 4, out_shape=[jax.ShapeDtypeStruct((r, c), f32)] * 4,
        compiler_params=_cparams(("arbitrary",)), name=name)(parts, w, m, v)


def _mesh_pos():
    return lax.axis_index("x"), lax.axis_index("y"), lax.axis_index("c")


def _peer(pos, k):
    x, y, c = pos
    return (1 - x if k & 4 else x, 1 - y if k & 2 else y, 1 - c if k & 1 else c)


def _lin(pos):
    return 4 * pos[0] + 2 * pos[1] + pos[2]


def exchange(srcs, scatter, name):
    n = len(srcs)
    shapes = [s.shape[1:] if sc else s.shape for s, sc in zip(srcs, scatter)]

    def body(*refs):
        src_refs, dst_refs = refs[:n], refs[n:2 * n]
        send_sems, recv_sems, local_sems = refs[2 * n:]
        me = _mesh_pos()
        me_lin = _lin(me)
        sends, recvs, locals_ = [], [], []
        for t in range(n):
            own = src_refs[t].at[me_lin] if scatter[t] else src_refs[t]
            locals_.append(pltpu.make_async_copy(own, dst_refs[t].at[me_lin], local_sems.at[t]))
            for k in range(1, N_DEV):
                peer = _peer(me, k)
                src = src_refs[t].at[_lin(peer)] if scatter[t] else src_refs[t]
                sends.append(pltpu.make_async_remote_copy(
                    src_ref=src, dst_ref=dst_refs[t].at[me_lin], send_sem=send_sems.at[t, k - 1],
                    recv_sem=recv_sems.at[t, k - 1], device_id=peer, device_id_type=MESH))
                recvs.append(pltpu.make_async_remote_copy(
                    src_ref=src, dst_ref=dst_refs[t].at[_lin(peer)], send_sem=send_sems.at[t, k - 1],
                    recv_sem=recv_sems.at[t, k - 1], device_id=peer, device_id_type=MESH))
        for cp in locals_ + sends:
            cp.start()
        for cp in sends:
            cp.wait_send()
        for cp in recvs:
            cp.wait_recv()
        for cp in locals_:
            cp.wait()

    any_spec = pl.BlockSpec(memory_space=pl.ANY)
    return pl.pallas_call(
        body, in_specs=[any_spec] * n, out_specs=[any_spec] * n,
        out_shape=[jax.ShapeDtypeStruct((N_DEV,) + tuple(sh), s.dtype) for sh, s in zip(shapes, srcs)],
        scratch_shapes=[pltpu.SemaphoreType.DMA((n, N_DEV - 1)), pltpu.SemaphoreType.DMA((n, N_DEV - 1)),
                        pltpu.SemaphoreType.DMA((n,))],
        compiler_params=pltpu.CompilerParams(has_side_effects=True), name=name)(*srcs)


def gather_two_level(srcs, name):
    n = len(srcs)
    n_cp = N_DEV - 1

    def body(*refs):
        src_refs, dst_refs = refs[:n], refs[n:2 * n]
        send_sems, recv_sems, local_sems = refs[2 * n:]
        x, y, c = _mesh_pos()
        me, sibling = (x, y, c), (x, y, 1 - c)
        chips = [(1 - x, y), (x, 1 - y), (1 - x, 1 - y)]

        def copy(t, k, block, to, src=None):
            slot = dst_refs[t].at[_lin(block)]
            return pltpu.make_async_remote_copy(
                src_ref=slot if src is None else src, dst_ref=slot, send_sem=send_sems.at[t, k],
                recv_sem=recv_sems.at[t, k], device_id=to, device_id_type=MESH)

        mine, first, passed = [], [], []
        for t in range(n):
            mine.append(pltpu.make_async_copy(src_refs[t], dst_refs[t].at[_lin(me)], local_sems.at[t]))
            first.append(copy(t, 0, me, sibling, src=src_refs[t]))
            first += [copy(t, 1 + j, me, (*chip, c), src=src_refs[t]) for j, chip in enumerate(chips)]
        for cp in mine + first:
            cp.start()
        for j, chip in enumerate(chips):
            for t in range(n):
                copy(t, 1 + j, (*chip, c), me).wait_recv()
                fwd = copy(t, 4 + j, (*chip, c), sibling)
                fwd.start()
                passed.append(fwd)
        for t in range(n):
            copy(t, 0, sibling, me).wait_recv()
            for j, chip in enumerate(chips):
                copy(t, 4 + j, (*chip, 1 - c), me).wait_recv()
        for cp in first + passed:
            cp.wait_send()
        for cp in mine:
            cp.wait()

    any_spec = pl.BlockSpec(memory_space=pl.ANY)
    return pl.pallas_call(
        body, in_specs=[any_spec] * n, out_specs=[any_spec] * n,
        out_shape=[jax.ShapeDtypeStruct((N_DEV,) + tuple(s.shape), s.dtype) for s in srcs],
        scratch_shapes=[pltpu.SemaphoreType.DMA((n, n_cp)), pltpu.SemaphoreType.DMA((n, n_cp)),
                        pltpu.SemaphoreType.DMA((n,))],
        compiler_params=pltpu.CompilerParams(has_side_effects=True), name=name)(*srcs)


PACK_ROWS = 104


def _pad_rows(a, rows):
    return jnp.pad(a, ((0, rows - a.shape[0]), (0, 0)))


def _pad_lanes(a):
    return jnp.pad(a, ((0, 0), (0, 128 - a.shape[1])))


def _pack(norm_w, sbq, sbk, alog, dtb, gon, lbl, hon, loss_row):
    parts = [norm_w.reshape(32, 128), _pad_rows(sbq, 8), _pad_rows(sbk, 8), _pad_rows(_pad_lanes(alog), 8),
             _pad_rows(_pad_lanes(dtb), 8), _pad_rows(gon, 8), lbl.reshape(16, 128), _pad_rows(hon, 8),
             _pad_rows(loss_row, 8)]
    return jnp.concatenate(parts, axis=0)


def _unpack(p):
    return dict(norm_w=p[0:32].reshape(DEPTH, D_MODEL), sb_q_norm=p[32:36], sb_k_norm=p[40:44],
                gdn_a_log=p[48:52, :N_HEADS], gdn_dt_bias=p[56:60, :N_HEADS], gdn_out_norm=p[64:68],
                hgrn_lb_logits=p[72:88].reshape(DEPTH, BRANCH), hgrn_out_norm=p[88:92], loss=p[96, 0])


def _lower_bounds(logits):
    p = jax.nn.softmax(logits, axis=0)
    return jnp.cumsum(p, axis=0) - p[0:1]


def _unshard_cols(g):
    nd = g.ndim
    g = jnp.moveaxis(g, 0, nd - 2)
    return g.reshape(g.shape[:-2] + (N_DEV * g.shape[-1],))


def _shard_cols(a):
    n = a.shape[-1] // N_DEV
    return jnp.moveaxis(a.reshape(a.shape[:-1] + (N_DEV, n)), -2, 0)


def kernel(x, meta_tokens, norm_w, w_in, sb_q_norm, sb_k_norm, gdn_conv_w, gdn_a_log, gdn_dt_bias, gdn_out_norm, hgrn_lb_logits, hgrn_out_norm, w_branch, w_out, loss_target, m_meta_tokens, m_norm_w, m_w_in, m_sb_q_norm, m_sb_k_norm, m_gdn_conv_w, m_gdn_a_log, m_gdn_dt_bias, m_gdn_out_norm, m_hgrn_lb_logits, m_hgrn_out_norm, m_w_branch, m_w_out, v_meta_tokens, v_norm_w, v_w_in, v_sb_q_norm, v_sb_k_norm, v_gdn_conv_w, v_gdn_a_log, v_gdn_dt_bias, v_gdn_out_norm, v_hgrn_lb_logits, v_hgrn_out_norm, v_w_branch, v_w_out):
    g_win, g_wbr, g_wout, g_meta, g_conv = gather_two_level(
        [w_in.astype(bf16), w_branch.astype(bf16), w_out.astype(bf16), meta_tokens, gdn_conv_w], "gather_weights")
    w_full = _unshard_cols(g_win)
    w_main = jnp.concatenate([w_full[..., a:b] for a, b in W_IN_ORDER], axis=-1)
    w_small = jnp.pad(w_full[..., SMALL_OFF:SMALL_OFF + 8], ((0, 0), (0, 0), (0, N_SMALL - 8)))
    wt_main = jnp.swapaxes(w_main, 1, 2)
    wt_small = jnp.swapaxes(w_small, 1, 2)
    wbr = _unshard_cols(g_wbr)
    wbr_t = jnp.swapaxes(wbr, 2, 3)
    wout = jnp.moveaxis(g_wout, 0, 1).reshape(DEPTH, D_MODEL, D_MODEL)
    wout_t = jnp.swapaxes(wout, 1, 2)
    meta = _unshard_cols(g_meta)
    conv_w = _unshard_cols(g_conv)
    lbounds, lb_vjp = jax.vjp(_lower_bounds, hgrn_lb_logits)

    h = jnp.concatenate([jnp.zeros((PAD_FRONT, D_MODEL), f32), meta, x[0]], axis=0)
    row = lambda a: a.reshape(1, -1)
    saved = []
    for l in range(DEPTH):
        proj, small, _, xnt = inproj_fwd(h, row(norm_w[l]), w_main[l], w_small[l])
        osb = sb_fwd(proj, row(sb_q_norm[l]), row(sb_k_norm[l]))
        ypre = conv_fwd(proj, conv_w[l])
        al, dtb = _pad_lanes(row(gdn_a_log[l])), _pad_lanes(row(gdn_dt_bias[l]))
        ogd, gst = gdn_fwd(ypre, small, proj, al, dtb, row(gdn_out_norm[l]))
        ohg, hst = hgrn_fwd(proj, row(lbounds[l]), row(hgrn_out_norm[l]))
        h_next = merge_fwd(osb, proj, ogd, ohg, wbr[l], wout[l], h)
        saved.append((h, proj, small, xnt, osb, ypre, ogd, gst, ohg, hst, al, dtb))
        h = h_next

    dh, loss_row = loss_head(h, loss_target[0])

    gw_main, gw_small, gw_br, gw_out, g_conv_w = [None] * DEPTH, [None] * DEPTH, [None] * DEPTH, [None] * DEPTH, [None] * DEPTH
    g_norm, g_sbq, g_sbk, g_al, g_dt, g_gon, g_lb, g_hon = ([None] * DEPTH for _ in range(8))
    for l in reversed(range(DEPTH)):
        h_l, proj, small, xnt, osb, ypre, ogd, gst, ohg, hst, al, dtb = saved[l]
        d_osb, d_ogd, d_ohg, dproj, gw_out[l], gw_br[l] = merge_bwd(osb, proj, ogd, ohg, wbr[l], wbr_t[l], wout_t[l], dh)
        dproj, g_lb[l], g_hon[l] = hgrn_bwd(proj, row(lbounds[l]), row(hgrn_out_norm[l]), hst, d_ohg, dproj)
        d_ypre, d_small, dproj, g_al[l], g_dt[l], g_gon[l] = gdn_bwd(ypre, small, proj, al, dtb, row(gdn_out_norm[l]), gst, d_ogd, dproj)
        dproj, g_conv_w[l] = conv_bwd(proj, conv_w[l], d_ypre, dproj)
        dproj, g_sbq[l], g_sbk[l] = sb_bwd(proj, row(sb_q_norm[l]), row(sb_k_norm[l]), osb, d_osb, dproj)
        gw_main[l], gw_small[l] = inproj_bwd_w(xnt, dproj, d_small)
        dh, g_norm[l] = inproj_bwd_x(dproj, d_small, wt_main[l], wt_small[l], h_l, row(norm_w[l]), dh)

    gw_main, gw_small = jnp.stack(gw_main), jnp.stack(gw_small)
    starts = np.cumsum([0] + [b - a for a, b in W_IN_ORDER])
    pieces = sorted((a, gw_main[..., int(s):int(s) + b - a]) for (a, b), s in zip(W_IN_ORDER, starts))
    pieces.append((SMALL_OFF, gw_small[..., :8]))
    gw_in = jnp.concatenate([p for _, p in sorted(pieces, key=lambda ap: ap[0])], axis=-1)
    d_lbl = lb_vjp(jnp.concatenate(g_lb, axis=0))[0]
    cat = lambda rows: jnp.concatenate(rows, axis=0)
    pack = _pack(cat(g_norm), cat(g_sbq), cat(g_sbk), cat(g_al)[:, :N_HEADS], cat(g_dt)[:, :N_HEADS], cat(g_gon),
                 d_lbl, cat(g_hon), loss_row)
    g_meta_full = dh[PAD_FRONT:FRONT]
    r_win, r_wbr, r_wout, r_meta, r_conv, r_pack = exchange(
        [_shard_cols(gw_in).astype(bf16), _shard_cols(jnp.stack(gw_br)).astype(bf16),
         jnp.swapaxes(jnp.stack(gw_out).reshape(DEPTH, N_DEV, HEAD, D_MODEL), 0, 1).astype(bf16),
         _shard_cols(g_meta_full), _shard_cols(jnp.stack(g_conv_w)), pack],
        [True, True, True, True, True, False], "exchange_grads")

    def upd(parts, w, m, v, rows, name):
        shp = w.shape
        two = (-1, shp[-1])
        outs = adamw(parts.reshape((N_DEV,) + w.reshape(two).shape), w.reshape(two), m.reshape(two), v.reshape(two), rows, name)
        return [o.reshape(shp) for o in outs]

    res = {}
    res["w_in"] = upd(r_win, w_in, m_w_in, v_w_in, 256, "adamw_w_in")
    res["w_branch"] = upd(r_wbr, w_branch, m_w_branch, v_w_branch, 1024, "adamw_w_branch")
    res["w_out"] = upd(r_wout, w_out, m_w_out, v_w_out, 256, "adamw_w_out")
    res["meta_tokens"] = upd(r_meta, meta_tokens, m_meta_tokens, v_meta_tokens, 16, "adamw_meta")
    res["gdn_conv_w"] = upd(r_conv, gdn_conv_w, m_gdn_conv_w, v_gdn_conv_w, 16, "adamw_conv")
    zero_row = jnp.zeros((1, 128), f32)
    w_pack = _pack(norm_w, sb_q_norm, sb_k_norm, gdn_a_log, gdn_dt_bias, gdn_out_norm, hgrn_lb_logits, hgrn_out_norm, zero_row)
    m_pack = _pack(m_norm_w, m_sb_q_norm, m_sb_k_norm, m_gdn_a_log, m_gdn_dt_bias, m_gdn_out_norm, m_hgrn_lb_logits, m_hgrn_out_norm, zero_row)
    v_pack = _pack(v_norm_w, v_sb_q_norm, v_sb_k_norm, v_gdn_a_log, v_gdn_dt_bias, v_gdn_out_norm, v_hgrn_lb_logits, v_hgrn_out_norm, zero_row)
    packed = [_unpack(o) for o in adamw(r_pack, w_pack, m_pack, v_pack, PACK_ROWS, "adamw_replicated")]
    for name in ("norm_w", "sb_q_norm", "sb_k_norm", "gdn_a_log", "gdn_dt_bias", "gdn_out_norm", "hgrn_lb_logits", "hgrn_out_norm"):
        res[name] = [p[name] for p in packed]
    loss = packed[0]["loss"]
    grad_x = dh[FRONT:][None]

    order = ["meta_tokens", "norm_w", "w_in", "sb_q_norm", "sb_k_norm", "gdn_conv_w", "gdn_a_log", "gdn_dt_bias",
             "gdn_out_norm", "hgrn_lb_logits", "hgrn_out_norm", "w_branch", "w_out"]
    return (loss, grad_x, *[res[n][0] for n in order], *[res[n][1] for n in order],
            *[res[n][2] for n in order], *[res[n][3] for n in order])
```

```python
import functools

import numpy as np
import jax
import jax.numpy as jnp
from jax import lax
from jax.experimental import pallas as pl
from jax.experimental.pallas import tpu as pltpu

f32 = jnp.float32
bf16 = jnp.bfloat16

D_MODEL = 1024
BRANCH = 512
HEAD = 128
N_HEADS = 4
CHUNK = 64
SB_BLOCK = 128
N_META = 16
FRONT = 128
PAD_FRONT = 112
EPS = 1e-6
DEPTH = 4
N_DEV = 8
N_IN = 9224
N_MAIN = 9216
N_SMALL = 128
SMALL_OFF = 4096
C_SBZ, C_MIX = 0, 512
C_GZ = 3584
C_HQ, C_HF, C_HI, C_HZ = 4096, 4608, 5120, 5632
C_GQKV = 6144
C_SBQ, C_SBK, C_SBV = 7680, 8192, 8704
W_MERGE = BRANCH + 3 * D_MODEL
W_IN_ORDER = ((1536, 2048), (6152, 9224), (3584, 4096), (4104, 6152), (2048, 3584), (0, 1536))

ADAM_LR, ADAM_B1, ADAM_B2, ADAM_EPS, ADAM_WD, ADAM_STEP = 0.001, 0.9, 0.999, 1e-08, 0.01, 10

VMEM_LIMIT = 56 * 1024 * 1024
MESH = pl.DeviceIdType.MESH

NN = ((1,), (0,))
NT = ((1,), (1,))
TN = ((0,), (0,))


def _dot(a, b, dims=NN):
    return lax.dot_general(a.astype(bf16), b.astype(bf16), (dims, ((), ())), preferred_element_type=f32)


@jax.custom_vjp
def mm(a, b):
    return _dot(a, b, NN)


mm.defvjp(lambda a, b: (_dot(a, b, NN), (a, b)),
          lambda r, g: (_dot(g, r[1], NT), _dot(r[0], g, TN)))


@jax.custom_vjp
def mm_nt(a, b):
    return _dot(a, b, NT)


mm_nt.defvjp(lambda a, b: (_dot(a, b, NT), (a, b)),
             lambda r, g: (_dot(g, r[1], NN), _dot(g, r[0], TN)))


@jax.custom_vjp
def mm_tn(a, b):
    return _dot(a, b, TN)


mm_tn.defvjp(lambda a, b: (_dot(a, b, TN), (a, b)),
             lambda r, g: (_dot(r[1], g, NT), _dot(r[0], g, NN)))


def _split2(x):
    hi = x.astype(bf16)
    lo = (x - hi.astype(f32)).astype(bf16)
    return hi, lo


def _cdot(c, x, dims):
    hi, lo = _split2(x)
    return (lax.dot_general(c, hi, (dims, ((), ())), preferred_element_type=f32)
            + lax.dot_general(c, lo, (dims, ((), ())), preferred_element_type=f32))


@jax.custom_vjp
def cmm(c, x):
    return _cdot(c, x, NN)


cmm.defvjp(lambda c, x: (_cdot(c, x, NN), c),
           lambda c, g: (jnp.zeros_like(c), _cdot(c, g, TN)))


def _sigmoid(x):
    return jax.nn.sigmoid(x)


def _silu(x):
    return x * jax.nn.sigmoid(x)


def _softplus(x):
    return jnp.maximum(x, 0.0) + jnp.log(1.0 + jnp.exp(-jnp.abs(x)))


def _rms(x, w):
    return x * lax.rsqrt(jnp.mean(x * x, axis=-1, keepdims=True) + EPS) * w


def _cparams(sem=None):
    return pltpu.CompilerParams(dimension_semantics=sem, vmem_limit_bytes=VMEM_LIMIT)


TILES_FWD = (1664, 512)
TILES_BWD_X = (832, 1024)
TILES_BWD_W = (1664, 1024)


def _row_tile(t, want, unit):
    return max(d for d in range(unit, want + 1, unit) if t % d == 0)


def inproj_fwd(h, nw, w_main, w_small):
    t = h.shape[0]
    TM_IN, TN_IN = _row_tile(t, TILES_FWD[0], 128), TILES_FWD[1]

    def body(h_ref, nw_ref, w_ref, ws_ref, proj_ref, small_ref, xn_ref, xnt_ref):
        @pl.when(pl.program_id(1) == 0)
        def _():
            xn = _rms(h_ref[...], nw_ref[...])
            xn_ref[...] = xn.astype(bf16)
            xnt_ref[...] = jnp.transpose(xn).astype(bf16)
            small_ref[...] = _dot(xn, ws_ref[...])

        proj_ref[...] = jnp.dot(xn_ref[...], w_ref[...], preferred_element_type=f32)

    return pl.pallas_call(
        body, grid=(t // TM_IN, N_MAIN // TN_IN),
        in_specs=[pl.BlockSpec((TM_IN, D_MODEL), lambda i, j: (i, 0)),
                  pl.BlockSpec((1, D_MODEL), lambda i, j: (0, 0)),
                  pl.BlockSpec((D_MODEL, TN_IN), lambda i, j: (0, j)),
                  pl.BlockSpec((D_MODEL, N_SMALL), lambda i, j: (0, 0))],
        out_specs=[pl.BlockSpec((TM_IN, TN_IN), lambda i, j: (i, j)),
                   pl.BlockSpec((TM_IN, N_SMALL), lambda i, j: (i, 0)),
                   pl.BlockSpec((TM_IN, D_MODEL), lambda i, j: (i, 0)),
                   pl.BlockSpec((D_MODEL, TM_IN), lambda i, j: (0, i))],
        out_shape=[jax.ShapeDtypeStruct((t, N_MAIN), f32), jax.ShapeDtypeStruct((t, N_SMALL), f32),
                   jax.ShapeDtypeStruct((t, D_MODEL), bf16), jax.ShapeDtypeStruct((D_MODEL, t), bf16)],
        compiler_params=_cparams(("arbitrary", "arbitrary")), name="inproj_fwd")(h, nw, w_main, w_small)


def inproj_bwd_x(dproj, dsmall, wt_main, wt_small, h, nw, dh_out):
    t = h.shape[0]
    TM_IN, TN_IN = _row_tile(t, TILES_BWD_X[0], 64), TILES_BWD_X[1]
    nk = N_MAIN // TN_IN

    def body(dp_ref, ds_ref, wt_ref, wts_ref, h_ref, nw_ref, dho_ref, dhi_ref, dnw_ref, acc):
        i, k = pl.program_id(0), pl.program_id(1)

        @pl.when(k == 0)
        def _():
            acc[...] = _dot(ds_ref[...], wts_ref[...])

        acc[...] += _dot(dp_ref[...], wt_ref[...])

        @pl.when(k == nk - 1)
        def _():
            x = h_ref[...]
            r = lax.rsqrt(jnp.mean(x * x, axis=-1, keepdims=True) + EPS)
            xh = x * r
            dxn = acc[...]
            dxh = dxn * nw_ref[...]
            dhi_ref[...] = dho_ref[...] + r * (dxh - xh * jnp.mean(dxh * xh, axis=-1, keepdims=True))
            part = jnp.sum(dxn * xh, axis=0, keepdims=True)

            @pl.when(i == 0)
            def _():
                dnw_ref[...] = part

            @pl.when(i > 0)
            def _():
                dnw_ref[...] += part

    return pl.pallas_call(
        body, grid=(t // TM_IN, nk),
        in_specs=[pl.BlockSpec((TM_IN, TN_IN), lambda i, k: (i, k)),
                  pl.BlockSpec((TM_IN, N_SMALL), lambda i, k: (i, 0)),
                  pl.BlockSpec((TN_IN, D_MODEL), lambda i, k: (k, 0)),
                  pl.BlockSpec((N_SMALL, D_MODEL), lambda i, k: (0, 0)),
                  pl.BlockSpec((TM_IN, D_MODEL), lambda i, k: (i, 0)),
                  pl.BlockSpec((1, D_MODEL), lambda i, k: (0, 0)),
                  pl.BlockSpec((TM_IN, D_MODEL), lambda i, k: (i, 0))],
        out_specs=[pl.BlockSpec((TM_IN, D_MODEL), lambda i, k: (i, 0)),
                   pl.BlockSpec((1, D_MODEL), lambda i, k: (0, 0))],
        out_shape=[jax.ShapeDtypeStruct((t, D_MODEL), f32), jax.ShapeDtypeStruct((1, D_MODEL), f32)],
        scratch_shapes=[pltpu.VMEM((TM_IN, D_MODEL), f32)],
        compiler_params=_cparams(("arbitrary", "arbitrary")), name="inproj_bwd_x")(
            dproj, dsmall, wt_main, wt_small, h, nw, dh_out)


def inproj_bwd_w(xnt, dproj, dsmall):
    t = xnt.shape[1]
    TM_IN, TN_IN = _row_tile(t, TILES_BWD_W[0], 128), TILES_BWD_W[1]
    nt = t // TM_IN

    def body(xnt_ref, dp_ref, ds_ref, dw_ref, dws_ref):
        n, s = pl.program_id(0), pl.program_id(1)
        part = _dot(xnt_ref[...], dp_ref[...])

        @pl.when(s == 0)
        def _():
            dw_ref[...] = part

        @pl.when(s > 0)
        def _():
            dw_ref[...] += part

        @pl.when(n == 0)
        def _():
            ps = _dot(xnt_ref[...], ds_ref[...])

            @pl.when(s == 0)
            def _():
                dws_ref[...] = ps

            @pl.when(s > 0)
            def _():
                dws_ref[...] += ps

    return pl.pallas_call(
        body, grid=(N_MAIN // TN_IN, nt),
        in_specs=[pl.BlockSpec((D_MODEL, TM_IN), lambda n, s: (0, s)),
                  pl.BlockSpec((TM_IN, TN_IN), lambda n, s: (s, n)),
                  pl.BlockSpec((TM_IN, N_SMALL), lambda n, s: (s, 0))],
        out_specs=[pl.BlockSpec((D_MODEL, TN_IN), lambda n, s: (0, n)),
                   pl.BlockSpec((D_MODEL, N_SMALL), lambda n, s: (0, 0))],
        out_shape=[jax.ShapeDtypeStruct((D_MODEL, N_MAIN), f32), jax.ShapeDtypeStruct((D_MODEL, N_SMALL), f32)],
        compiler_params=_cparams(("arbitrary", "arbitrary")), name="inproj_bwd_w")(xnt, dproj, dsmall)


SB_SCALE = HEAD ** -0.5


SB_SUB = 4
SB_KS = SB_SUB * SB_BLOCK


def _sb_padded(t):
    return -(-t // SB_KS) * SB_KS


def _sb_prep(k_ref, v_ref, kw_ref, kn_scr, vb_scr, nb):
    def prep(b, c):
        rows = pl.ds(pl.multiple_of(b * SB_BLOCK, SB_BLOCK), SB_BLOCK)
        kn_scr[rows, :] = _rms(k_ref[rows, :], kw_ref[...]).astype(bf16)
        vb_scr[rows, :] = v_ref[rows, :].astype(bf16)
        return c

    lax.fori_loop(0, nb, prep, 0)
    pad = kn_scr.shape[0] - nb * SB_BLOCK
    if pad:
        kn_scr[nb * SB_BLOCK:, :] = jnp.zeros((pad, HEAD), bf16)
        vb_scr[nb * SB_BLOCK:, :] = jnp.zeros((pad, HEAD), bf16)


def _tri_ext(cmp):
    r = lax.broadcasted_iota(jnp.int32, (SB_BLOCK, 2 * SB_BLOCK), 0)
    c = lax.broadcasted_iota(jnp.int32, (SB_BLOCK, 2 * SB_BLOCK), 1)
    return jnp.where((c >= SB_BLOCK) | cmp(r, c), 1.0, 0.0).astype(bf16)


def _sb_suffix(x, carry, tri_ext):
    hi, lo = _split2(x)
    parts = [p[:, c * SB_BLOCK:(c + 1) * SB_BLOCK] for p in (hi, lo) for c in range(SB_SUB)]
    w = jnp.dot(jnp.concatenate(parts, axis=0), tri_ext, preferred_element_type=f32)
    outs = [None] * SB_SUB
    for c in reversed(range(SB_SUB)):
        blk = w[c * SB_BLOCK:(c + 1) * SB_BLOCK] + w[(SB_SUB + c) * SB_BLOCK:(SB_SUB + c + 1) * SB_BLOCK]
        outs[c] = carry + blk[:, :SB_BLOCK]
        carry = carry + blk[:, SB_BLOCK:]
    return jnp.concatenate(outs, axis=1), carry


def _sb_scores(qn, kt, i, jb, masked):
    z = lax.dot_general(qn, kt, (NT, ((), ())), preferred_element_type=f32) * SB_SCALE
    lsz = jnp.minimum(z, 0.0) - jnp.log(1.0 + jnp.exp(-jnp.abs(z)))
    lk = lsz - z
    mask = None
    if masked:
        t_idx = i * SB_BLOCK + lax.broadcasted_iota(jnp.int32, (SB_BLOCK, SB_KS), 0)
        s_idx = jb * SB_KS + lax.broadcasted_iota(jnp.int32, (SB_BLOCK, SB_KS), 1)
        mask = (s_idx < t_idx) & (s_idx >= PAD_FRONT)
        lk = jnp.where(mask, lk, 0.0)
    return mask, lsz, lk


SB_DEAD = -104.0


def _sb_walk(i, tile, carry):
    n = i // SB_SUB + 1
    live = lambda c: jnp.max(c[1]) > SB_DEAD
    carry = tile(n - 1, carry, True)
    _, carry = lax.while_loop(lambda st: (st[0] >= 1) & live(st[1]),
                              lambda st: (st[0] - 1, tile(st[0], st[1], False)), (n - 2, carry))
    return lax.cond((n >= 2) & live(carry), lambda c: tile(0, c, True), lambda c: c, carry)


def sb_fwd(proj, qw, kw):
    t = proj.shape[0]
    nb = t // SB_BLOCK

    def body(q_ref, k_ref, v_ref, qw_ref, kw_ref, o_ref, kn_scr, vb_scr):
        i = pl.program_id(1)

        @pl.when(i == 0)
        def _():
            _sb_prep(k_ref, v_ref, kw_ref, kn_scr, vb_scr, nb)

        qn = _rms(q_ref[...], qw_ref[...]).astype(bf16)
        u_ex = _tri_ext(lambda r, c: r > c)

        def tile(jb, carry, masked):
            acc, r_carry = carry
            rows = pl.ds(pl.multiple_of(jb * SB_KS, SB_KS), SB_KS)
            mask, lsz, lk = _sb_scores(qn, kn_scr[rows, :], i, jb, masked)
            passed, r_carry = _sb_suffix(lk, r_carry, u_ex)
            a = jnp.exp(lsz + passed)
            if masked:
                a = jnp.where(mask, a, 0.0)
            a_hi, a_lo = _split2(a)
            both = jnp.dot(jnp.concatenate([a_hi, a_lo], axis=0), vb_scr[rows, :], preferred_element_type=f32)
            return acc + (both[:SB_BLOCK] + both[SB_BLOCK:]), r_carry

        zeros = jnp.zeros((SB_BLOCK, HEAD), f32)
        acc, _ = _sb_walk(i, tile, (zeros, zeros))
        o_ref[...] = acc

    qb, cb, vb = C_SBQ // HEAD, C_SBK // HEAD, C_SBV // HEAD
    return pl.pallas_call(
        body, grid=(N_HEADS, nb),
        in_specs=[pl.BlockSpec((SB_BLOCK, HEAD), lambda h, i: (i, qb + h)),
                  pl.BlockSpec((t, HEAD), lambda h, i: (0, cb + h)),
                  pl.BlockSpec((t, HEAD), lambda h, i: (0, vb + h)),
                  pl.BlockSpec((1, HEAD), lambda h, i: (0, 0)),
                  pl.BlockSpec((1, HEAD), lambda h, i: (0, 0))],
        out_specs=pl.BlockSpec((SB_BLOCK, HEAD), lambda h, i: (i, h)),
        out_shape=jax.ShapeDtypeStruct((t, BRANCH), f32),
        scratch_shapes=[pltpu.VMEM((_sb_padded(t), HEAD), bf16), pltpu.VMEM((_sb_padded(t), HEAD), bf16)],
        compiler_params=_cparams(("arbitrary", "arbitrary")), name="sb_fwd")(proj, proj, proj, qw, kw)


def sb_bwd(proj, qw, kw, o, do, dproj):
    t = proj.shape[0]
    nb = t // SB_BLOCK

    def body(q_ref, k_ref, v_ref, qw_ref, kw_ref, o_ref, do_ref, _, dp_ref, dqw_ref, dkw_ref,
             kn_scr, vb_scr, dk_acc, dv_acc, dq_stage, dq_sems, kv_sems):
        h, i = pl.program_id(0), pl.program_id(1)
        step = h * nb + i
        slot = step % 2

        def dq_copy(sl, head):
            return pltpu.make_async_copy(
                dq_stage.at[sl], dp_ref.at[pl.ds(pl.multiple_of(i * SB_BLOCK, SB_BLOCK), SB_BLOCK),
                                           pl.ds(C_SBQ + head * HEAD, HEAD)], dq_sems.at[sl])

        @pl.when(i == 0)
        def _():
            _sb_prep(k_ref, v_ref, kw_ref, kn_scr, vb_scr, nb)
            dk_acc[...] = jnp.zeros_like(dk_acc)
            dv_acc[...] = jnp.zeros_like(dv_acc)

        @pl.when((i == 0) & (h == 0))
        def _():
            dqw_ref[...] = jnp.zeros_like(dqw_ref)
            dkw_ref[...] = jnp.zeros_like(dkw_ref)

        q = q_ref[...]
        rq = lax.rsqrt(jnp.mean(q * q, axis=-1, keepdims=True) + EPS)
        qh = q * rq
        qn = (qh * qw_ref[...]).astype(bf16)
        do_f = do_ref[...]
        dob = do_f.astype(bf16)
        d_row = jnp.sum(dob.astype(f32) * o_ref[...], axis=-1, keepdims=True)
        u_ex = _tri_ext(lambda r, c: r > c)
        u_in = _tri_ext(lambda r, c: r >= c)

        def tile(jb, carry, masked):
            dq, r_carry, f_carry = carry
            rows = pl.ds(pl.multiple_of(jb * SB_KS, SB_KS), SB_KS)
            kt = kn_scr[rows, :]
            vt = vb_scr[rows, :]
            mask, lsz, lk = _sb_scores(qn, kt, i, jb, masked)
            passed, r_carry = _sb_suffix(lk, r_carry, u_ex)
            a = jnp.exp(lsz + passed)
            if masked:
                a = jnp.where(mask, a, 0.0)
            da = lax.dot_general(dob, vt, (NT, ((), ())), preferred_element_type=f32)
            e = a * da
            e_suf, f_carry = _sb_suffix(e, f_carry, u_in)
            sg = jnp.exp(lsz)
            dz = (e * (1.0 - sg) - (d_row - e_suf) * sg) * SB_SCALE
            if masked:
                dz = jnp.where(mask, dz, 0.0)
            dzb = dz.astype(bf16)
            dq = dq + jnp.dot(dzb, kt, preferred_element_type=f32)
            dk_acc[rows, :] += lax.dot_general(dzb, qn, (TN, ((), ())), preferred_element_type=f32)
            dv_acc[rows, :] += lax.dot_general(a.astype(bf16), dob, (TN, ((), ())), preferred_element_type=f32)
            return dq, r_carry, f_carry

        zeros = jnp.zeros((SB_BLOCK, HEAD), f32)
        dqn, _, _ = _sb_walk(i, tile, (zeros, zeros, zeros))
        gq = dqn * qw_ref[...]
        dqw_ref[...] += jnp.sum(dqn * qh, axis=0, keepdims=True)

        @pl.when(step >= 2)
        def _():
            dq_copy(slot, 0).wait()

        dq_stage[slot] = rq * (gq - qh * jnp.mean(gq * qh, axis=-1, keepdims=True))
        for head in range(N_HEADS):
            @pl.when(h == head)
            def _(head=head):
                dq_copy(slot, head).start()

        @pl.when(i == nb - 1)
        def _():
            def fin(b, c):
                rows = pl.ds(pl.multiple_of(b * SB_BLOCK, SB_BLOCK), SB_BLOCK)
                kk = k_ref[rows, :]
                rk = lax.rsqrt(jnp.mean(kk * kk, axis=-1, keepdims=True) + EPS)
                kh = kk * rk
                dkn = dk_acc[rows, :]
                gk = dkn * kw_ref[...]
                dk_acc[rows, :] = rk * (gk - kh * jnp.mean(gk * kh, axis=-1, keepdims=True))
                dkw_ref[...] += jnp.sum(dkn * kh, axis=0, keepdims=True)
                return c

            lax.fori_loop(0, nb, fin, 0)
            for head in range(N_HEADS):
                @pl.when(h == head)
                def _(head=head):
                    outs = [pltpu.make_async_copy(acc.at[pl.ds(0, t)], dp_ref.at[:, pl.ds(c0 + head * HEAD, HEAD)],
                                                  kv_sems.at[n])
                            for n, (acc, c0) in enumerate(((dk_acc, C_SBK), (dv_acc, C_SBV)))]
                    for cp in outs:
                        cp.start()
                    for cp in outs:
                        cp.wait()

        @pl.when(step == N_HEADS * nb - 1)
        def _():
            dq_copy(1 - slot, 0).wait()
            dq_copy(slot, 0).wait()

    qb, cb, vb = C_SBQ // HEAD, C_SBK // HEAD, C_SBV // HEAD
    blk = pl.BlockSpec((SB_BLOCK, HEAD), lambda h, i: (i, h))
    wsp = pl.BlockSpec((1, HEAD), lambda h, i: (0, 0))
    any_spec = pl.BlockSpec(memory_space=pl.ANY)
    return pl.pallas_call(
        body, grid=(N_HEADS, nb),
        in_specs=[pl.BlockSpec((SB_BLOCK, HEAD), lambda h, i: (i, qb + h)),
                  pl.BlockSpec((t, HEAD), lambda h, i: (0, cb + h)),
                  pl.BlockSpec((t, HEAD), lambda h, i: (0, vb + h)), wsp, wsp, blk, blk, any_spec],
        out_specs=[any_spec, wsp, wsp],
        out_shape=[jax.ShapeDtypeStruct((t, N_MAIN), f32)] + [jax.ShapeDtypeStruct((1, HEAD), f32)] * 2,
        scratch_shapes=[pltpu.VMEM((_sb_padded(t), HEAD), bf16), pltpu.VMEM((_sb_padded(t), HEAD), bf16),
                        pltpu.VMEM((_sb_padded(t), HEAD), f32), pltpu.VMEM((_sb_padded(t), HEAD), f32),
                        pltpu.VMEM((2, SB_BLOCK, HEAD), f32), pltpu.SemaphoreType.DMA((2,)), pltpu.SemaphoreType.DMA((2,))],
        input_output_aliases={7: 0},
        compiler_params=_cparams(("arbitrary", "arbitrary")), name="sb_bwd")(proj, proj, proj, qw, kw, o, do, dproj)


TM_CONV = 640
CONV_W = 4
GQKV = 3 * BRANCH


def conv_fwd(proj, cw):
    t = proj.shape[0]
    halo_blocks = TM_CONV // 8

    def body(x0_ref, x1_ref, x2_ref, p0_ref, p1_ref, p2_ref, cw_ref, y_ref):
        i = pl.program_id(0)
        for s, (x_ref, p_ref) in enumerate(((x0_ref, p0_ref), (x1_ref, p1_ref), (x2_ref, p2_ref))):
            prev = jnp.where(i > 0, p_ref[...], 0.0)
            xx = jnp.concatenate([prev, x_ref[...]], axis=0)
            cols = slice(s * BRANCH, (s + 1) * BRANCH)
            y = xx[8:] * cw_ref[CONV_W - 1:CONV_W, cols]
            for k in range(CONV_W - 1):
                y = y + pltpu.roll(xx, CONV_W - 1 - k, 0)[8:] * cw_ref[k:k + 1, cols]
            y_ref[:, cols] = y

    c0 = C_GQKV // BRANCH
    xs = [pl.BlockSpec((TM_CONV, BRANCH), functools.partial(lambda i, s: (i, c0 + s), s=s)) for s in range(3)]
    ps = [pl.BlockSpec((8, BRANCH), functools.partial(lambda i, s: (jnp.maximum(i * halo_blocks - 1, 0), c0 + s), s=s))
          for s in range(3)]
    return pl.pallas_call(
        body, grid=(t // TM_CONV,),
        in_specs=xs + ps + [pl.BlockSpec((CONV_W, GQKV), lambda i: (0, 0))],
        out_specs=pl.BlockSpec((TM_CONV, GQKV), lambda i: (i, 0)),
        out_shape=jax.ShapeDtypeStruct((t, GQKV), f32),
        compiler_params=_cparams(("arbitrary",)), name="conv_fwd")(proj, proj, proj, proj, proj, proj, cw)


def conv_bwd(proj, cw, dy, dproj):
    t = proj.shape[0]
    nt = t // TM_CONV
    halo_blocks = TM_CONV // 8

    def body(x0_ref, x1_ref, x2_ref, p0_ref, p1_ref, p2_ref, cw_ref, dy_ref, dyn_ref, _, dx_ref, dw_ref):
        i = pl.program_id(0)

        @pl.when(i == 0)
        def _():
            dw_ref[...] = jnp.zeros_like(dw_ref)

        nxt = jnp.where(i < nt - 1, dyn_ref[...], 0.0)
        dyy = jnp.concatenate([dy_ref[...], nxt], axis=0)
        n_rows = TM_CONV + 8
        dx = dyy[:TM_CONV] * cw_ref[CONV_W - 1:CONV_W, :]
        for k in range(CONV_W - 1):
            sh = CONV_W - 1 - k
            dx = dx + pltpu.roll(dyy, n_rows - sh, 0)[:TM_CONV] * cw_ref[k:k + 1, :]
        dx_ref[...] = dx
        dy_c = dy_ref[...]
        for s, (x_ref, p_ref) in enumerate(((x0_ref, p0_ref), (x1_ref, p1_ref), (x2_ref, p2_ref))):
            prev = jnp.where(i > 0, p_ref[...], 0.0)
            xx = jnp.concatenate([prev, x_ref[...]], axis=0)
            cols = slice(s * BRANCH, (s + 1) * BRANCH)
            for k in range(CONV_W):
                sh = CONV_W - 1 - k
                xs = xx[8:] if sh == 0 else pltpu.roll(xx, sh, 0)[8:]
                dw_ref[k:k + 1, cols] += jnp.sum(xs * dy_c[:, cols], axis=0, keepdims=True)

    c0 = C_GQKV // BRANCH
    xs = [pl.BlockSpec((TM_CONV, BRANCH), functools.partial(lambda i, s: (i, c0 + s), s=s)) for s in range(3)]
    ps = [pl.BlockSpec((8, BRANCH), functools.partial(lambda i, s: (jnp.maximum(i * halo_blocks - 1, 0), c0 + s), s=s))
          for s in range(3)]
    return pl.pallas_call(
        body, grid=(nt,),
        in_specs=xs + ps + [pl.BlockSpec((CONV_W, GQKV), lambda i: (0, 0)),
                            pl.BlockSpec((TM_CONV, GQKV), lambda i: (i, 0)),
                            pl.BlockSpec((8, GQKV), lambda i: (jnp.minimum((i + 1) * halo_blocks, nt * halo_blocks - 1), 0)),
                            pl.BlockSpec(memory_space=pl.ANY)],
        out_specs=[pl.BlockSpec((TM_CONV, GQKV), lambda i: (i, C_GQKV // GQKV)), pl.BlockSpec((CONV_W, GQKV), lambda i: (0, 0))],
        out_shape=[jax.ShapeDtypeStruct((t, N_MAIN), f32), jax.ShapeDtypeStruct((CONV_W, GQKV), f32)],
        input_output_aliases={9: 0},
        compiler_params=_cparams(("arbitrary",)), name="conv_bwd")(proj, proj, proj, proj, proj, proj, cw, dy, dy, dproj)


def _iota2(n, m, d):
    return lax.broadcasted_iota(jnp.int32, (n, m), d)


def _lane_pick(row_or_mat, idx):
    lanes = lax.broadcasted_iota(jnp.int32, row_or_mat.shape, row_or_mat.ndim - 1)
    return jnp.sum(jnp.where(lanes == idx, row_or_mat, 0.0), axis=-1, keepdims=True)


def _cumsum_consts():
    i = np.arange(CHUNK)
    incl = i[None, :] <= i[:, None]
    suf = i[None, :] > i[:, None]
    return np.concatenate([incl, suf], 0).astype(np.float32)


HG_LEVELS = (64, 32, 16, 8, 4, 2)


def _hgrn_consts():
    i = np.arange(CHUNK)
    rows = [i[None, :] <= i[:, None], i[None, :] > i[:, None]]
    for b in HG_LEVELS:
        ref = (i // b) * b + b // 2 - 1
        second = (i % b) >= b // 2
        rows.append((i[None, :] > ref[:, None]) & (i[None, :] <= i[:, None]) & second[:, None])
        rows.append((i[None, :] > i[:, None]) & (i[None, :] <= ref[:, None]) & (~second)[:, None])
    return np.concatenate(rows, 0).astype(np.float32)


N_SQUARINGS = 5


def _solve_chain(ms, rhss):
    xs = [r - mm(m, r) for m, r in zip(ms, rhss)]
    powers = [list(ms)]
    for _ in range(N_SQUARINGS):
        powers.append([mm(p, p) for p in powers[-1]])
        xs = [x + mm(p, x) for p, x in zip(powers[-1], xs)]
    return tuple(xs), powers


@jax.custom_vjp
def unit_lower_solve_multi(ms, rhss):
    return _solve_chain(ms, rhss)[0]


def _solve_fwd(ms, rhss):
    xs, powers = _solve_chain(ms, rhss)
    return xs, (powers, xs)


def _solve_bwd(res, gs):
    powers, xs = res
    ys = [g - mm_tn(p, g) for p, g in zip(powers[0], gs)]
    for ps in powers[1:]:
        ys = [y + mm_tn(p, y) for p, y in zip(ps, ys)]
    return tuple(-mm_nt(y, x) for y, x in zip(ys, xs)), tuple(ys)


unit_lower_solve_multi.defvjp(_solve_fwd, _solve_bwd)

HC = N_HEADS * CHUNK
BATCH0 = ((0,), (0,))


def _bdot(a, b, contract):
    return lax.dot_general(a.astype(bf16), b.astype(bf16), (contract, BATCH0), preferred_element_type=f32)


B_NN = ((2,), (1,))
B_NT = ((2,), (2,))
B_TN = ((1,), (1,))


@jax.custom_vjp
def bmm(a, b):
    return _bdot(a, b, B_NN)


bmm.defvjp(lambda a, b: (_bdot(a, b, B_NN), (a, b)),
           lambda r, g: (_bdot(g, r[1], B_NT), _bdot(r[0], g, B_TN)))


@jax.custom_vjp
def bmm_nt(a, b):
    return _bdot(a, b, B_NT)


bmm_nt.defvjp(lambda a, b: (_bdot(a, b, B_NT), (a, b)),
              lambda r, g: (_bdot(g, r[1], B_NN), _bdot(g, r[0], B_TN)))


@jax.custom_vjp
def bmm_tn(a, b):
    return _bdot(a, b, B_TN)


bmm_tn.defvjp(lambda a, b: (_bdot(a, b, B_TN), (a, b)),
              lambda r, g: (_bdot(r[1], g, B_NT), _bdot(r[0], g, B_NN)))


def _stack_heads(x):
    return jnp.concatenate([x[:, h * HEAD:(h + 1) * HEAD] for h in range(N_HEADS)], axis=0)


def _unstack_heads(x):
    return jnp.concatenate([x[h * CHUNK:(h + 1) * CHUNK] for h in range(N_HEADS)], axis=1)


REC_CHUNKS = 5
REC_ROWS = REC_CHUNKS * CHUNK


def _interleave(gens):
    n = len(gens)
    sends, results, done = [None] * n, [None] * n, [False] * n
    while not all(done):
        asks = []
        for j in range(n):
            if done[j]:
                continue
            try:
                ask = gens[j].send(sends[j])
                if ask is not None:
                    asks.append((j, ask))
            except StopIteration as stop:
                results[j], done[j] = stop.value, True
            sends[j] = None
        if asks:
            xs = unit_lower_solve_multi(tuple(a[1] for _, a in asks), tuple(a[2] for _, a in asks))
            for (j, _), x in zip(asks, xs):
                sends[j] = x
    return results


def _chunks_of(a):
    return [a[j * CHUNK:(j + 1) * CHUNK] for j in range(REC_CHUNKS)]


def _gdn_step(state, ypre, small, gz, a_log, dt_b, on_w, c2, first_chunk):
    gens = [_gdn_intra(y, s, a_log, dt_b, c2, _vmask(first_chunk + j))
            for j, (y, s) in enumerate(zip(_chunks_of(ypre), _chunks_of(small)))]
    outs = []
    for intra, z in zip(_interleave(gens), _chunks_of(gz)):
        state, o = _gdn_inter(state, intra, z, on_w)
        outs.append(o)
    return state, jnp.concatenate(outs, axis=0)


def _gdn_inter(state, intra, gz, on_w):
    u, w, aqk, q_dec, k_dec, g_last = intra
    per_head = lambda a: a.reshape(N_HEADS, CHUNK, HEAD)
    v_new = u - bmm(per_head(w), state).reshape(HC, HEAD)
    o = bmm(per_head(q_dec), state).reshape(HC, HEAD) + mm(aqk, v_new)
    new_state = state * g_last + bmm_tn(per_head(k_dec), per_head(v_new))
    return new_state, _unstack_heads(_rms(o, on_w)) * _silu(gz)


def _gdn_intra(ypre, small, a_log, dt_b, c2, vm):
    r = _iota2(HC, HC, 0)
    c = _iota2(HC, HC, 1)
    same_head = (r >> 6) == (c >> 6)
    causal = same_head & (r >= c)
    strict = same_head & (r > c)
    q = _silu(_stack_heads(ypre[:, :BRANCH]))
    k = _silu(_stack_heads(ypre[:, BRANCH:2 * BRANCH]))
    v = _silu(_stack_heads(ypre[:, 2 * BRANCH:]))
    q = q * lax.rsqrt(jnp.sum(q * q, axis=-1, keepdims=True) + EPS) * (HEAD ** -0.5)
    k = k * lax.rsqrt(jnp.sum(k * k, axis=-1, keepdims=True) + EPS)
    col = lambda f: jnp.concatenate([f(h) for h in range(N_HEADS)], axis=0)
    chunk_col = lambda x: jnp.broadcast_to(x, (CHUNK, 1))
    beta = _sigmoid(col(lambda h: _lane_pick(small, h))) * col(lambda h: vm)
    g = (-jnp.exp(col(lambda h: chunk_col(_lane_pick(a_log, h))))
         * _softplus(col(lambda h: _lane_pick(small, N_HEADS + h)) + col(lambda h: chunk_col(_lane_pick(dt_b, h)))))
    g_l = _unstack_heads(jnp.broadcast_to(g, (HC, HEAD)))
    e2 = cmm(c2, g_l)
    yield
    gc = _stack_heads(e2[:CHUNK])
    gsuf = _stack_heads(e2[CHUNK:])
    g_row = jnp.broadcast_to(jnp.transpose(gc)[0:1, :], (HC, HC))
    g_col = jnp.concatenate([gc, gc], axis=1)
    dec = jnp.where(causal, jnp.exp(jnp.minimum(g_col - g_row, 0.0)), 0.0)
    kb = k * beta
    kk = mm_nt(kb, k)
    qk = mm_nt(q, k)
    yield
    m = jnp.where(strict, kk * dec, 0.0)
    x = yield ("solve", m, jnp.concatenate([v * beta, kb * jnp.exp(gc)], axis=1))
    aqk = jnp.where(causal, qk * dec, 0.0)
    tot = jnp.sum(g_l, axis=0, keepdims=True)
    g_last = jnp.exp(jnp.stack([tot[:, h * HEAD:(h + 1) * HEAD] for h in range(N_HEADS)], axis=0))
    return x[:, :HEAD], x[:, HEAD:], aqk, q * jnp.exp(gc), k * jnp.exp(gsuf), g_last


def _hgrn_step(state, hq, hf, hi, hz, lb, on_w, cm, first_chunk):
    gens = [_hgrn_intra(q, f, i, lb, cm, _vmask(first_chunk + j))
            for j, (q, f, i) in enumerate(zip(_chunks_of(hq), _chunks_of(hf), _chunks_of(hi)))]
    outs = []
    for (q_dec, k_dec, v, o_intra, g_end), z in zip(_interleave(gens), _chunks_of(hz)):
        per_head = lambda a: a.reshape(N_HEADS, CHUNK, HEAD)
        o = bmm_nt(per_head(q_dec), state).reshape(HC, HEAD) + o_intra
        state = state * g_end + bmm_tn(per_head(v), per_head(k_dec))
        outs.append(_unstack_heads(_rms(o, on_w)) * _silu(z))
    return state, jnp.concatenate(outs, axis=0)


def _hgrn_intra(hq, hf, hi, lb, cm, vm):
    r = _iota2(HC, HC, 0)
    c = _iota2(HC, HC, 1)
    forget = lb + (1.0 - lb) * _sigmoid(hf)
    g_l = jnp.log(forget)
    e = cmm(cm, g_l)
    q = _stack_heads(_silu(hq))
    k = _stack_heads((1.0 - lb) * _sigmoid(-hf))
    v = _stack_heads(hi * vm)
    yield
    sect = lambda n: _stack_heads(e[n * CHUNK:(n + 1) * CHUNK])
    gc, gsuf = sect(0), sect(1)
    a = jnp.where(r == c, jnp.sum(q * k, axis=-1, keepdims=True), 0.0)
    for li, b in enumerate(HG_LEVELS):
        sh = b.bit_length() - 1
        pair = ((r >> sh) == (c >> sh)) & ((r & (b - 1)) >= b // 2) & ((c & (b - 1)) < b // 2)
        a = a + jnp.where(pair, mm_nt(q * jnp.exp(sect(2 + 2 * li)), k * jnp.exp(sect(3 + 2 * li))), 0.0)
    yield
    o_intra = mm(a, v)
    tot = jnp.sum(g_l, axis=0, keepdims=True)
    g_end = jnp.exp(jnp.stack([tot[:, h * HEAD:(h + 1) * HEAD] for h in range(N_HEADS)], axis=0))
    return q * jnp.exp(gc), k * jnp.exp(gsuf), v, o_intra, g_end


def _vmask(chunk_idx):
    rows = chunk_idx * CHUNK + lax.broadcasted_iota(jnp.int32, (CHUNK, 1), 0)
    return jnp.where(rows >= PAD_FRONT, 1.0, 0.0)


def _row(n):
    return pl.BlockSpec((1, n), lambda i: (0, 0))


def gdn_fwd(ypre, small, proj, a_log, dt_b, on_w):
    t = ypre.shape[0]
    nc = t // REC_ROWS
    c2 = jnp.asarray(_cumsum_consts(), bf16)

    def body(y_ref, s_ref, z_ref, al_ref, dt_ref, on_ref, c2_ref, o_ref, st_ref, state):
        i = pl.program_id(0)

        @pl.when(i == 0)
        def _():
            state[...] = jnp.zeros_like(state)

        s_in = state[...]
        st_ref[0] = s_in
        s_new, out = _gdn_step(s_in, y_ref[...], s_ref[...], z_ref[...], al_ref[...], dt_ref[...], on_ref[...],
                               c2_ref[...], i * REC_CHUNKS)
        state[...] = s_new
        o_ref[...] = out

    return pl.pallas_call(
        body, grid=(nc,),
        in_specs=[pl.BlockSpec((REC_ROWS,GQKV), lambda i: (i, 0)), pl.BlockSpec((REC_ROWS,N_SMALL), lambda i: (i, 0)),
                  pl.BlockSpec((REC_ROWS,BRANCH), lambda i: (i, C_GZ // BRANCH)), _row(128), _row(128), _row(128),
                  pl.BlockSpec((2 * CHUNK, CHUNK), lambda i: (0, 0))],
        out_specs=[pl.BlockSpec((REC_ROWS,BRANCH), lambda i: (i, 0)),
                   pl.BlockSpec((1, N_HEADS, HEAD, HEAD), lambda i: (i, 0, 0, 0))],
        out_shape=[jax.ShapeDtypeStruct((t, BRANCH), f32), jax.ShapeDtypeStruct((nc, N_HEADS, HEAD, HEAD), f32)],
        scratch_shapes=[pltpu.VMEM((N_HEADS, HEAD, HEAD), f32)],
        compiler_params=_cparams(("arbitrary",)), name="gdn_fwd")(ypre, small, proj, a_log, dt_b, on_w, c2)


def gdn_bwd(ypre, small, proj, a_log, dt_b, on_w, states, d_out, dproj):
    t = ypre.shape[0]
    nc = t // REC_ROWS
    c2 = jnp.asarray(_cumsum_consts(), bf16)

    def body(y_ref, s_ref, z_ref, al_ref, dt_ref, on_ref, c2_ref, st_ref, do_ref, _,
             dy_ref, ds_ref, dz_ref, dal_ref, ddt_ref, don_ref, dstate):
        i = pl.program_id(0)

        @pl.when(i == 0)
        def _():
            dstate[...] = jnp.zeros_like(dstate)
            dal_ref[...] = jnp.zeros_like(dal_ref)
            ddt_ref[...] = jnp.zeros_like(ddt_ref)
            don_ref[...] = jnp.zeros_like(don_ref)

        c2v = c2_ref[...]
        fn = lambda s, y, sm, z, al, dt, on: _gdn_step(s, y, sm, z, al, dt, on, c2v, (nc - 1 - i) * REC_CHUNKS)
        _, vjp = jax.vjp(fn, st_ref[0], y_ref[...], s_ref[...], z_ref[...], al_ref[...], dt_ref[...], on_ref[...])
        d_s, d_y, d_sm, d_z, d_al, d_dt, d_on = vjp((dstate[...], do_ref[...]))
        dstate[...] = d_s
        dy_ref[...] = d_y
        ds_ref[...] = d_sm
        dz_ref[...] = d_z
        dal_ref[...] += d_al
        ddt_ref[...] += d_dt
        don_ref[...] += d_on

    rev = lambda i: (nc - 1 - i, 0)
    return pl.pallas_call(
        body, grid=(nc,),
        in_specs=[pl.BlockSpec((REC_ROWS,GQKV), rev), pl.BlockSpec((REC_ROWS,N_SMALL), rev),
                  pl.BlockSpec((REC_ROWS,BRANCH), lambda i: (nc - 1 - i, C_GZ // BRANCH)), _row(128), _row(128), _row(128),
                  pl.BlockSpec((2 * CHUNK, CHUNK), lambda i: (0, 0)),
                  pl.BlockSpec((1, N_HEADS, HEAD, HEAD), lambda i: (nc - 1 - i, 0, 0, 0)),
                  pl.BlockSpec((REC_ROWS,BRANCH), rev), pl.BlockSpec(memory_space=pl.ANY)],
        out_specs=[pl.BlockSpec((REC_ROWS,GQKV), rev), pl.BlockSpec((REC_ROWS,N_SMALL), rev),
                   pl.BlockSpec((REC_ROWS,BRANCH), lambda i: (nc - 1 - i, C_GZ // BRANCH)), _row(128), _row(128), _row(128)],
        out_shape=[jax.ShapeDtypeStruct((t, GQKV), f32), jax.ShapeDtypeStruct((t, N_SMALL), f32),
                   jax.ShapeDtypeStruct((t, N_MAIN), f32)] + [jax.ShapeDtypeStruct((1, 128), f32)] * 3,
        scratch_shapes=[pltpu.VMEM((N_HEADS, HEAD, HEAD), f32)], input_output_aliases={9: 2},
        compiler_params=_cparams(("arbitrary",)), name="gdn_bwd")(
            ypre, small, proj, a_log, dt_b, on_w, c2, states, d_out, dproj)


def hgrn_fwd(proj, lb, on_w):
    t = proj.shape[0]
    nc = t // REC_ROWS
    cm = jnp.asarray(_hgrn_consts(), bf16)
    ncm = cm.shape[0]

    def body(q_ref, f_ref, i_ref, z_ref, lb_ref, on_ref, cm_ref, o_ref, st_ref, state):
        i = pl.program_id(0)

        @pl.when(i == 0)
        def _():
            state[...] = jnp.zeros_like(state)

        s_in = state[...]
        st_ref[0] = s_in
        s_new, out = _hgrn_step(s_in, q_ref[...], f_ref[...], i_ref[...], z_ref[...], lb_ref[...], on_ref[...],
                                cm_ref[...], i * REC_CHUNKS)
        state[...] = s_new
        o_ref[...] = out

    sec = lambda off: pl.BlockSpec((REC_ROWS,BRANCH), functools.partial(lambda i, b: (i, b), b=off // BRANCH))
    return pl.pallas_call(
        body, grid=(nc,),
        in_specs=[sec(C_HQ), sec(C_HF), sec(C_HI), sec(C_HZ), _row(BRANCH), _row(128),
                  pl.BlockSpec((ncm, CHUNK), lambda i: (0, 0))],
        out_specs=[pl.BlockSpec((REC_ROWS,BRANCH), lambda i: (i, 0)),
                   pl.BlockSpec((1, N_HEADS, HEAD, HEAD), lambda i: (i, 0, 0, 0))],
        out_shape=[jax.ShapeDtypeStruct((t, BRANCH), f32), jax.ShapeDtypeStruct((nc, N_HEADS, HEAD, HEAD), f32)],
        scratch_shapes=[pltpu.VMEM((N_HEADS, HEAD, HEAD), f32)],
        compiler_params=_cparams(("arbitrary",)), name="hgrn_fwd")(proj, proj, proj, proj, lb, on_w, cm)


def hgrn_bwd(proj, lb, on_w, states, d_out, dproj):
    t = proj.shape[0]
    nc = t // REC_ROWS
    cm = jnp.asarray(_hgrn_consts(), bf16)
    ncm = cm.shape[0]

    def body(q_ref, f_ref, i_ref, z_ref, lb_ref, on_ref, cm_ref, st_ref, do_ref, _, dh_ref, dlb_ref, don_ref, dstate):
        i = pl.program_id(0)

        @pl.when(i == 0)
        def _():
            dstate[...] = jnp.zeros_like(dstate)
            dlb_ref[...] = jnp.zeros_like(dlb_ref)
            don_ref[...] = jnp.zeros_like(don_ref)

        cmv = cm_ref[...]
        fn = lambda s, a, b, c, d, l, on: _hgrn_step(s, a, b, c, d, l, on, cmv, (nc - 1 - i) * REC_CHUNKS)
        _, vjp = jax.vjp(fn, st_ref[0], q_ref[...], f_ref[...], i_ref[...], z_ref[...], lb_ref[...], on_ref[...])
        d_s, d_q, d_f, d_i, d_z, d_lb, d_on = vjp((dstate[...], do_ref[...]))
        dstate[...] = d_s
        dh_ref[...] = jnp.concatenate([d_q, d_f, d_i, d_z], axis=1)
        dlb_ref[...] += d_lb
        don_ref[...] += d_on

    rev = lambda i: (nc - 1 - i, 0)
    sec = lambda off: pl.BlockSpec((REC_ROWS,BRANCH), functools.partial(lambda i, b: (nc - 1 - i, b), b=off // BRANCH))
    return pl.pallas_call(
        body, grid=(nc,),
        in_specs=[sec(C_HQ), sec(C_HF), sec(C_HI), sec(C_HZ), _row(BRANCH), _row(128),
                  pl.BlockSpec((ncm, CHUNK), lambda i: (0, 0)),
                  pl.BlockSpec((1, N_HEADS, HEAD, HEAD), lambda i: (nc - 1 - i, 0, 0, 0)),
                  pl.BlockSpec((REC_ROWS,BRANCH), rev), pl.BlockSpec(memory_space=pl.ANY)],
        out_specs=[pl.BlockSpec((REC_ROWS,4 * BRANCH), lambda i: (nc - 1 - i, C_HQ // (4 * BRANCH))), _row(BRANCH), _row(128)],
        out_shape=[jax.ShapeDtypeStruct((t, N_MAIN), f32), jax.ShapeDtypeStruct((1, BRANCH), f32),
                   jax.ShapeDtypeStruct((1, 128), f32)],
        scratch_shapes=[pltpu.VMEM((N_HEADS, HEAD, HEAD), f32)], input_output_aliases={9: 0},
        compiler_params=_cparams(("arbitrary",)), name="hgrn_bwd")(proj, proj, proj, proj, lb, on_w, cm, states, d_out, dproj)


TM_MG = 320


def _const_spec(shape):
    nd = len(shape)
    return pl.BlockSpec(shape, lambda i: (0,) * nd, pipeline_mode=pl.Buffered(1))


def merge_fwd(osb, proj, ogd, ohg, wb, wo, h):
    t = h.shape[0]

    def body(osb_ref, ogd_ref, ohg_ref, zm_ref, wb_ref, wo_ref, h_ref, out_ref):
        a = osb_ref[...] * _silu(zm_ref[:, :BRANCH])
        gate = lambda b: _sigmoid(zm_ref[:, C_MIX + b * D_MODEL:C_MIX + (b + 1) * D_MODEL])
        y = (gate(0) * _dot(a, wb_ref[0]) + gate(1) * _dot(ogd_ref[...], wb_ref[1])
             + gate(2) * _dot(ohg_ref[...], wb_ref[2]))
        out_ref[...] = h_ref[...] + _dot(y, wo_ref[...])

    br = pl.BlockSpec((TM_MG, BRANCH), lambda i: (i, 0))
    return pl.pallas_call(
        body, grid=(t // TM_MG,),
        in_specs=[br, br, br, pl.BlockSpec((TM_MG, W_MERGE), lambda i: (i, 0)),
                  _const_spec((3, BRANCH, D_MODEL)), _const_spec((D_MODEL, D_MODEL)),
                  pl.BlockSpec((TM_MG, D_MODEL), lambda i: (i, 0))],
        out_specs=pl.BlockSpec((TM_MG, D_MODEL), lambda i: (i, 0)),
        out_shape=jax.ShapeDtypeStruct((t, D_MODEL), f32),
        compiler_params=_cparams(("arbitrary",)), name="merge_fwd")(osb, ogd, ohg, proj, wb, wo, h)


def merge_bwd(osb, proj, ogd, ohg, wb, wbt, wot, dh):
    t = dh.shape[0]

    def body(osb_ref, ogd_ref, ohg_ref, zm_ref, wb_ref, wbt_ref, wot_ref, dh_ref,
             dosb_ref, dogd_ref, dohg_ref, dzm_ref, dwo_ref, dwb_ref):
        i = pl.program_id(0)

        @pl.when(i == 0)
        def _():
            dwo_ref[...] = jnp.zeros_like(dwo_ref)
            dwb_ref[...] = jnp.zeros_like(dwb_ref)

        osb = osb_ref[...]
        sbz = zm_ref[:, :BRANCH]
        sgz = _sigmoid(sbz)
        sz = sbz * sgz
        branch_in = (osb * sz, ogd_ref[...], ohg_ref[...])
        dh_v = dh_ref[...]
        dy = _dot(dh_v, wot_ref[...])
        y = jnp.zeros((TM_MG, D_MODEL), f32)
        d_in = []
        for b in range(3):
            cols = slice(C_MIX + b * D_MODEL, C_MIX + (b + 1) * D_MODEL)
            p = _dot(branch_in[b], wb_ref[b])
            g = _sigmoid(zm_ref[:, cols])
            y = y + g * p
            dp = dy * g
            dzm_ref[:, cols] = dy * p * g * (1.0 - g)
            d_in.append(_dot(dp, wbt_ref[b]))
            dwb_ref[b] += _dot(branch_in[b], dp, TN)
        dwo_ref[...] += _dot(y, dh_v, TN)
        dosb_ref[...] = d_in[0] * sz
        dzm_ref[:, :BRANCH] = d_in[0] * osb * (sgz * (1.0 + sbz * (1.0 - sgz)))
        dogd_ref[...] = d_in[1]
        dohg_ref[...] = d_in[2]

    br = pl.BlockSpec((TM_MG, BRANCH), lambda i: (i, 0))
    zm = pl.BlockSpec((TM_MG, W_MERGE), lambda i: (i, 0))
    return pl.pallas_call(
        body, grid=(t // TM_MG,),
        in_specs=[br, br, br, zm,
                  _const_spec((3, BRANCH, D_MODEL)), _const_spec((3, D_MODEL, BRANCH)), _const_spec((D_MODEL, D_MODEL)),
                  pl.BlockSpec((TM_MG, D_MODEL), lambda i: (i, 0))],
        out_specs=[br, br, br, zm, _const_spec((D_MODEL, D_MODEL)), _const_spec((3, BRANCH, D_MODEL))],
        out_shape=[jax.ShapeDtypeStruct((t, BRANCH), f32)] * 3 + [jax.ShapeDtypeStruct((t, N_MAIN), f32),
                   jax.ShapeDtypeStruct((D_MODEL, D_MODEL), f32), jax.ShapeDtypeStruct((3, BRANCH, D_MODEL), f32)],
        compiler_params=_cparams(("arbitrary",)), name="merge_bwd")(osb, ogd, ohg, proj, wb, wbt, wot, dh)


def loss_head(h, target):
    t = h.shape[0]
    nb = t // SB_BLOCK

    def body(h_ref, t_ref, dh_ref, loss_ref):
        i = pl.program_id(0)

        @pl.when(i == 0)
        def _():
            loss_ref[...] = jnp.zeros_like(loss_ref)
            dh_ref[...] = jnp.zeros_like(dh_ref)

        @pl.when(i > 0)
        def _():
            err = h_ref[...] - t_ref[...]
            dh_ref[...] = err * (1.0 / D_MODEL)
            loss_ref[...] += jnp.broadcast_to(jnp.sum(err * err) * (0.5 / D_MODEL), loss_ref.shape)

    return pl.pallas_call(
        body, grid=(nb,),
        in_specs=[pl.BlockSpec((SB_BLOCK, D_MODEL), lambda i: (i, 0)),
                  pl.BlockSpec((SB_BLOCK, D_MODEL), lambda i: (jnp.maximum(i - 1, 0), 0))],
        out_specs=[pl.BlockSpec((SB_BLOCK, D_MODEL), lambda i: (i, 0)), pl.BlockSpec((1, 128), lambda i: (0, 0))],
        out_shape=[jax.ShapeDtypeStruct((t, D_MODEL), f32), jax.ShapeDtypeStruct((1, 128), f32)],
        compiler_params=_cparams(("arbitrary",)), name="loss_head")(h, target)


def adamw(parts, w, m, v, rows_per_step, name):
    r, c = w.shape
    tr = min(rows_per_step, r)

    def body(p_ref, w_ref, m_ref, v_ref, g_ref, d_ref, nm_ref, nv_ref):
        g = p_ref[0].astype(f32)
        for k in range(1, N_DEV):
            g = g + p_ref[k].astype(f32)
        m_new = ADAM_B1 * m_ref[...] + (1.0 - ADAM_B1) * g
        v_new = ADAM_B2 * v_ref[...] + (1.0 - ADAM_B2) * jnp.square(g)
        m_hat = m_new / (1.0 - ADAM_B1 ** ADAM_STEP)
        v_hat = v_new / (1.0 - ADAM_B2 ** ADAM_STEP)
        g_ref[...] = g
        d_ref[...] = -ADAM_LR * (m_hat / (jnp.sqrt(v_hat) + ADAM_EPS) + ADAM_WD * w_ref[...])
        nm_ref[...] = m_new
        nv_ref[...] = v_new

    blk = pl.BlockSpec((tr, c), lambda i: (i, 0))
    return pl.pallas_call(
        body, grid=(r // tr,),
        in_specs=[pl.BlockSpec((N_DEV, tr, c), lambda i: (0, i, 0)), blk, blk, blk],
        out_specs=[blk] * 4, out_shape=[jax.ShapeDtypeStruct((r, c), f32)] * 4,
        compiler_params=_cparams(("arbitrary",)), name=name)(parts, w, m, v)


def _mesh_pos():
    return lax.axis_index("x"), lax.axis_index("y"), lax.axis_index("c")


def _peer(pos, k):
    x, y, c = pos
    return (1 - x if k & 4 else x, 1 - y if k & 2 else y, 1 - c if k & 1 else c)


def _lin(pos):
    return 4 * pos[0] + 2 * pos[1] + pos[2]


def exchange(srcs, scatter, name):
    n = len(srcs)
    shapes = [s.shape[1:] if sc else s.shape for s, sc in zip(srcs, scatter)]

    def body(*refs):
        src_refs, dst_refs = refs[:n], refs[n:2 * n]
        send_sems, recv_sems, local_sems = refs[2 * n:]
        me = _mesh_pos()
        me_lin = _lin(me)
        sends, recvs, locals_ = [], [], []
        for t in range(n):
            own = src_refs[t].at[me_lin] if scatter[t] else src_refs[t]
            locals_.append(pltpu.make_async_copy(own, dst_refs[t].at[me_lin], local_sems.at[t]))
            for k in range(1, N_DEV):
                peer = _peer(me, k)
                src = src_refs[t].at[_lin(peer)] if scatter[t] else src_refs[t]
                sends.append(pltpu.make_async_remote_copy(
                    src_ref=src, dst_ref=dst_refs[t].at[me_lin], send_sem=send_sems.at[t, k - 1],
                    recv_sem=recv_sems.at[t, k - 1], device_id=peer, device_id_type=MESH))
                recvs.append(pltpu.make_async_remote_copy(
                    src_ref=src, dst_ref=dst_refs[t].at[_lin(peer)], send_sem=send_sems.at[t, k - 1],
                    recv_sem=recv_sems.at[t, k - 1], device_id=peer, device_id_type=MESH))
        for cp in locals_ + sends:
            cp.start()
        for cp in sends:
            cp.wait_send()
        for cp in recvs:
            cp.wait_recv()
        for cp in locals_:
            cp.wait()

    any_spec = pl.BlockSpec(memory_space=pl.ANY)
    return pl.pallas_call(
        body, in_specs=[any_spec] * n, out_specs=[any_spec] * n,
        out_shape=[jax.ShapeDtypeStruct((N_DEV,) + tuple(sh), s.dtype) for sh, s in zip(shapes, srcs)],
        scratch_shapes=[pltpu.SemaphoreType.DMA((n, N_DEV - 1)), pltpu.SemaphoreType.DMA((n, N_DEV - 1)),
                        pltpu.SemaphoreType.DMA((n,))],
        compiler_params=pltpu.CompilerParams(has_side_effects=True), name=name)(*srcs)


def gather_two_level(srcs, name):
    n = len(srcs)
    n_cp = N_DEV - 1

    def body(*refs):
        src_refs, dst_refs = refs[:n], refs[n:2 * n]
        send_sems, recv_sems, local_sems = refs[2 * n:]
        x, y, c = _mesh_pos()
        me, sibling = (x, y, c), (x, y, 1 - c)
        chips = [(1 - x, y), (x, 1 - y), (1 - x, 1 - y)]

        def copy(t, k, block, to, src=None):
            slot = dst_refs[t].at[_lin(block)]
            return pltpu.make_async_remote_copy(
                src_ref=slot if src is None else src, dst_ref=slot, send_sem=send_sems.at[t, k],
                recv_sem=recv_sems.at[t, k], device_id=to, device_id_type=MESH)

        mine, first, passed = [], [], []
        for t in range(n):
            mine.append(pltpu.make_async_copy(src_refs[t], dst_refs[t].at[_lin(me)], local_sems.at[t]))
            first.append(copy(t, 0, me, sibling, src=src_refs[t]))
            first += [copy(t, 1 + j, me, (*chip, c), src=src_refs[t]) for j, chip in enumerate(chips)]
        for cp in mine + first:
            cp.start()
        for j, chip in enumerate(chips):
            for t in range(n):
                copy(t, 1 + j, (*chip, c), me).wait_recv()
                fwd = copy(t, 4 + j, (*chip, c), sibling)
                fwd.start()
                passed.append(fwd)
        for t in range(n):
            copy(t, 0, sibling, me).wait_recv()
            for j, chip in enumerate(chips):
                copy(t, 4 + j, (*chip, 1 - c), me).wait_recv()
        for cp in first + passed:
            cp.wait_send()
        for cp in mine:
            cp.wait()

    any_spec = pl.BlockSpec(memory_space=pl.ANY)
    return pl.pallas_call(
        body, in_specs=[any_spec] * n, out_specs=[any_spec] * n,
        out_shape=[jax.ShapeDtypeStruct((N_DEV,) + tuple(s.shape), s.dtype) for s in srcs],
        scratch_shapes=[pltpu.SemaphoreType.DMA((n, n_cp)), pltpu.SemaphoreType.DMA((n, n_cp)),
                        pltpu.SemaphoreType.DMA((n,))],
        compiler_params=pltpu.CompilerParams(has_side_effects=True), name=name)(*srcs)


PACK_ROWS = 104


def _pad_rows(a, rows):
    return jnp.pad(a, ((0, rows - a.shape[0]), (0, 0)))


def _pad_lanes(a):
    return jnp.pad(a, ((0, 0), (0, 128 - a.shape[1])))


def _pack(norm_w, sbq, sbk, alog, dtb, gon, lbl, hon, loss_row):
    parts = [norm_w.reshape(32, 128), _pad_rows(sbq, 8), _pad_rows(sbk, 8), _pad_rows(_pad_lanes(alog), 8),
             _pad_rows(_pad_lanes(dtb), 8), _pad_rows(gon, 8), lbl.reshape(16, 128), _pad_rows(hon, 8),
             _pad_rows(loss_row, 8)]
    return jnp.concatenate(parts, axis=0)


def _unpack(p):
    return dict(norm_w=p[0:32].reshape(DEPTH, D_MODEL), sb_q_norm=p[32:36], sb_k_norm=p[40:44],
                gdn_a_log=p[48:52, :N_HEADS], gdn_dt_bias=p[56:60, :N_HEADS], gdn_out_norm=p[64:68],
                hgrn_lb_logits=p[72:88].reshape(DEPTH, BRANCH), hgrn_out_norm=p[88:92], loss=p[96, 0])


def _lower_bounds(logits):
    p = jax.nn.softmax(logits, axis=0)
    return jnp.cumsum(p, axis=0) - p[0:1]


def _unshard_cols(g):
    nd = g.ndim
    g = jnp.moveaxis(g, 0, nd - 2)
    return g.reshape(g.shape[:-2] + (N_DEV * g.shape[-1],))


def _shard_cols(a):
    n = a.shape[-1] // N_DEV
    return jnp.moveaxis(a.reshape(a.shape[:-1] + (N_DEV, n)), -2, 0)


def kernel(x, meta_tokens, norm_w, w_in, sb_q_norm, sb_k_norm, gdn_conv_w, gdn_a_log, gdn_dt_bias, gdn_out_norm, hgrn_lb_logits, hgrn_out_norm, w_branch, w_out, loss_target, m_meta_tokens, m_norm_w, m_w_in, m_sb_q_norm, m_sb_k_norm, m_gdn_conv_w, m_gdn_a_log, m_gdn_dt_bias, m_gdn_out_norm, m_hgrn_lb_logits, m_hgrn_out_norm, m_w_branch, m_w_out, v_meta_tokens, v_norm_w, v_w_in, v_sb_q_norm, v_sb_k_norm, v_gdn_conv_w, v_gdn_a_log, v_gdn_dt_bias, v_gdn_out_norm, v_hgrn_lb_logits, v_hgrn_out_norm, v_w_branch, v_w_out):
    g_win, g_wbr, g_wout, g_meta, g_conv = gather_two_level(
        [w_in.astype(bf16), w_branch.astype(bf16), w_out.astype(bf16), meta_tokens, gdn_conv_w], "gather_weights")
    w_full = _unshard_cols(g_win)
    w_main = jnp.concatenate([w_full[..., a:b] for a, b in W_IN_ORDER], axis=-1)
    w_small = jnp.pad(w_full[..., SMALL_OFF:SMALL_OFF + 8], ((0, 0), (0, 0), (0, N_SMALL - 8)))
    wt_main = jnp.swapaxes(w_main, 1, 2)
    wt_small = jnp.swapaxes(w_small, 1, 2)
    wbr = _unshard_cols(g_wbr)
    wbr_t = jnp.swapaxes(wbr, 2, 3)
    wout = jnp.moveaxis(g_wout, 0, 1).reshape(DEPTH, D_MODEL, D_MODEL)
    wout_t = jnp.swapaxes(wout, 1, 2)
    meta = _unshard_cols(g_meta)
    conv_w = _unshard_cols(g_conv)
    lbounds, lb_vjp = jax.vjp(_lower_bounds, hgrn_lb_logits)

    h = jnp.concatenate([jnp.zeros((PAD_FRONT, D_MODEL), f32), meta, x[0]], axis=0)
    row = lambda a: a.reshape(1, -1)
    saved = []
    for l in range(DEPTH):
        proj, small, _, xnt = inproj_fwd(h, row(norm_w[l]), w_main[l], w_small[l])
        osb = sb_fwd(proj, row(sb_q_norm[l]), row(sb_k_norm[l]))
        ypre = conv_fwd(proj, conv_w[l])
        al, dtb = _pad_lanes(row(gdn_a_log[l])), _pad_lanes(row(gdn_dt_bias[l]))
        ogd, gst = gdn_fwd(ypre, small, proj, al, dtb, row(gdn_out_norm[l]))
        ohg, hst = hgrn_fwd(proj, row(lbounds[l]), row(hgrn_out_norm[l]))
        h_next = merge_fwd(osb, proj, ogd, ohg, wbr[l], wout[l], h)
        saved.append((h, proj, small, xnt, osb, ypre, ogd, gst, ohg, hst, al, dtb))
        h = h_next

    dh, loss_row = loss_head(h, loss_target[0])

    gw_main, gw_small, gw_br, gw_out, g_conv_w = [None] * DEPTH, [None] * DEPTH, [None] * DEPTH, [None] * DEPTH, [None] * DEPTH
    g_norm, g_sbq, g_sbk, g_al, g_dt, g_gon, g_lb, g_hon = ([None] * DEPTH for _ in range(8))
    for l in reversed(range(DEPTH)):
        h_l, proj, small, xnt, osb, ypre, ogd, gst, ohg, hst, al, dtb = saved[l]
        d_osb, d_ogd, d_ohg, dproj, gw_out[l], gw_br[l] = merge_bwd(osb, proj, ogd, ohg, wbr[l], wbr_t[l], wout_t[l], dh)
        dproj, g_lb[l], g_hon[l] = hgrn_bwd(proj, row(lbounds[l]), row(hgrn_out_norm[l]), hst, d_ohg, dproj)
        d_ypre, d_small, dproj, g_al[l], g_dt[l], g_gon[l] = gdn_bwd(ypre, small, proj, al, dtb, row(gdn_out_norm[l]), gst, d_ogd, dproj)
        dproj, g_conv_w[l] = conv_bwd(proj, conv_w[l], d_ypre, dproj)
        dproj, g_sbq[l], g_sbk[l] = sb_bwd(proj, row(sb_q_norm[l]), row(sb_k_norm[l]), osb, d_osb, dproj)
        gw_main[l], gw_small[l] = inproj_bwd_w(xnt, dproj, d_small)
        dh, g_norm[l] = inproj_bwd_x(dproj, d_small, wt_main[l], wt_small[l], h_l, row(norm_w[l]), dh)

    gw_main, gw_small = jnp.stack(gw_main), jnp.stack(gw_small)
    starts = np.cumsum([0] + [b - a for a, b in W_IN_ORDER])
    pieces = sorted((a, gw_main[..., int(s):int(s) + b - a]) for (a, b), s in zip(W_IN_ORDER, starts))
    pieces.append((SMALL_OFF, gw_small[..., :8]))
    gw_in = jnp.concatenate([p for _, p in sorted(pieces, key=lambda ap: ap[0])], axis=-1)
    d_lbl = lb_vjp(jnp.concatenate(g_lb, axis=0))[0]
    cat = lambda rows: jnp.concatenate(rows, axis=0)
    pack = _pack(cat(g_norm), cat(g_sbq), cat(g_sbk), cat(g_al)[:, :N_HEADS], cat(g_dt)[:, :N_HEADS], cat(g_gon),
                 d_lbl, cat(g_hon), loss_row)
    g_meta_full = dh[PAD_FRONT:FRONT]
    r_win, r_wbr, r_wout, r_meta, r_conv, r_pack = exchange(
        [_shard_cols(gw_in).astype(bf16), _shard_cols(jnp.stack(gw_br)).astype(bf16),
         jnp.swapaxes(jnp.stack(gw_out).reshape(DEPTH, N_DEV, HEAD, D_MODEL), 0, 1).astype(bf16),
         _shard_cols(g_meta_full), _shard_cols(jnp.stack(g_conv_w)), pack],
        [True, True, True, True, True, False], "exchange_grads")

    def upd(parts, w, m, v, rows, name):
        shp = w.shape
        two = (-1, shp[-1])
        outs = adamw(parts.reshape((N_DEV,) + w.reshape(two).shape), w.reshape(two), m.reshape(two), v.reshape(two), rows, name)
        return [o.reshape(shp) for o in outs]

    res = {}
    res["w_in"] = upd(r_win, w_in, m_w_in, v_w_in, 256, "adamw_w_in")
    res["w_branch"] = upd(r_wbr, w_branch, m_w_branch, v_w_branch, 1024, "adamw_w_branch")
    res["w_out"] = upd(r_wout, w_out, m_w_out, v_w_out, 256, "adamw_w_out")
    res["meta_tokens"] = upd(r_meta, meta_tokens, m_meta_tokens, v_meta_tokens, 16, "adamw_meta")
    res["gdn_conv_w"] = upd(r_conv, gdn_conv_w, m_gdn_conv_w, v_gdn_conv_w, 16, "adamw_conv")
    zero_row = jnp.zeros((1, 128), f32)
    w_pack = _pack(norm_w, sb_q_norm, sb_k_norm, gdn_a_log, gdn_dt_bias, gdn_out_norm, hgrn_lb_logits, hgrn_out_norm, zero_row)
    m_pack = _pack(m_norm_w, m_sb_q_norm, m_sb_k_norm, m_gdn_a_log, m_gdn_dt_bias, m_gdn_out_norm, m_hgrn_lb_logits, m_hgrn_out_norm, zero_row)
    v_pack = _pack(v_norm_w, v_sb_q_norm, v_sb_k_norm, v_gdn_a_log, v_gdn_dt_bias, v_gdn_out_norm, v_hgrn_lb_logits, v_hgrn_out_norm, zero_row)
    packed = [_unpack(o) for o in adamw(r_pack, w_pack, m_pack, v_pack, PACK_ROWS, "adamw_replicated")]
    for name in ("norm_w", "sb_q_norm", "sb_k_norm", "gdn_a_log", "gdn_dt_bias", "gdn_out_norm", "hgrn_lb_logits", "hgrn_out_norm"):
        res[name] = [p[name] for p in packed]
    loss = packed[0]["loss"]
    grad_x = dh[FRONT:][None]

    order = ["meta_tokens", "norm_w", "w_in", "sb_q_norm", "sb_k_norm", "gdn_conv_w", "gdn_a_log", "gdn_dt_bias",
             "gdn_out_norm", "hgrn_lb_logits", "hgrn_out_norm", "w_branch", "w_out"]
    return (loss, grad_x, *[res[n][0] for n in order], *[res[n][1] for n in order],
            *[res[n][2] for n in order], *[res[n][3] for n in order])
```

```python
import functools

import numpy as np
import jax
import jax.numpy as jnp
from jax import lax
from jax.experimental import pallas as pl
from jax.experimental.pallas import tpu as pltpu

f32 = jnp.float32
bf16 = jnp.bfloat16

D_MODEL = 1024
BRANCH = 512
HEAD = 128
N_HEADS = 4
CHUNK = 64
SB_BLOCK = 128
N_META = 16
FRONT = 128
PAD_FRONT = 112
EPS = 1e-6
DEPTH = 4
N_DEV = 8
N_IN = 9224
N_MAIN = 9216
N_SMALL = 128
SMALL_OFF = 4096
C_SBZ, C_MIX = 0, 512
C_GZ = 3584
C_HQ, C_HF, C_HI, C_HZ = 4096, 4608, 5120, 5632
C_GQKV = 6144
C_SBQ, C_SBK, C_SBV = 7680, 8192, 8704
W_MERGE = BRANCH + 3 * D_MODEL
W_IN_ORDER = ((1536, 2048), (6152, 9224), (3584, 4096), (4104, 6152), (2048, 3584), (0, 1536))

ADAM_LR, ADAM_B1, ADAM_B2, ADAM_EPS, ADAM_WD, ADAM_STEP = 0.001, 0.9, 0.999, 1e-08, 0.01, 10

VMEM_LIMIT = 56 * 1024 * 1024
MESH = pl.DeviceIdType.MESH

NN = ((1,), (0,))
NT = ((1,), (1,))
TN = ((0,), (0,))


def _dot(a, b, dims=NN):
    return lax.dot_general(a.astype(bf16), b.astype(bf16), (dims, ((), ())), preferred_element_type=f32)


@jax.custom_vjp
def mm(a, b):
    return _dot(a, b, NN)


mm.defvjp(lambda a, b: (_dot(a, b, NN), (a, b)),
          lambda r, g: (_dot(g, r[1], NT), _dot(r[0], g, TN)))


@jax.custom_vjp
def mm_nt(a, b):
    return _dot(a, b, NT)


mm_nt.defvjp(lambda a, b: (_dot(a, b, NT), (a, b)),
             lambda r, g: (_dot(g, r[1], NN), _dot(g, r[0], TN)))


@jax.custom_vjp
def mm_tn(a, b):
    return _dot(a, b, TN)


mm_tn.defvjp(lambda a, b: (_dot(a, b, TN), (a, b)),
             lambda r, g: (_dot(r[1], g, NT), _dot(r[0], g, NN)))


def _split2(x):
    hi = x.astype(bf16)
    lo = (x - hi.astype(f32)).astype(bf16)
    return hi, lo


def _cdot(c, x, dims):
    hi, lo = _split2(x)
    return (lax.dot_general(c, hi, (dims, ((), ())), preferred_element_type=f32)
            + lax.dot_general(c, lo, (dims, ((), ())), preferred_element_type=f32))


@jax.custom_vjp
def cmm(c, x):
    return _cdot(c, x, NN)


cmm.defvjp(lambda c, x: (_cdot(c, x, NN), c),
           lambda c, g: (jnp.zeros_like(c), _cdot(c, g, TN)))


def _sigmoid(x):
    return jax.nn.sigmoid(x)


def _silu(x):
    return x * jax.nn.sigmoid(x)


def _softplus(x):
    return jnp.maximum(x, 0.0) + jnp.log(1.0 + jnp.exp(-jnp.abs(x)))


def _rms(x, w):
    return x * lax.rsqrt(jnp.mean(x * x, axis=-1, keepdims=True) + EPS) * w


def _cparams(sem=None):
    return pltpu.CompilerParams(dimension_semantics=sem, vmem_limit_bytes=VMEM_LIMIT)


TILES_FWD = (1664, 512)
TILES_BWD_X = (832, 1024)
TILES_BWD_W = (1664, 1024)


def _row_tile(t, want, unit):
    return max(d for d in range(unit, want + 1, unit) if t % d == 0)


def inproj_fwd(h, nw, w_main, w_small):
    t = h.shape[0]
    TM_IN, TN_IN = _row_tile(t, TILES_FWD[0], 128), TILES_FWD[1]

    def body(h_ref, nw_ref, w_ref, ws_ref, proj_ref, small_ref, xn_ref, xnt_ref):
        @pl.when(pl.program_id(1) == 0)
        def _():
            xn = _rms(h_ref[...], nw_ref[...])
            xn_ref[...] = xn.astype(bf16)
            xnt_ref[...] = jnp.transpose(xn).astype(bf16)
            small_ref[...] = _dot(xn, ws_ref[...])

        proj_ref[...] = jnp.dot(xn_ref[...], w_ref[...], preferred_element_type=f32)

    return pl.pallas_call(
        body, grid=(t // TM_IN, N_MAIN // TN_IN),
        in_specs=[pl.BlockSpec((TM_IN, D_MODEL), lambda i, j: (i, 0)),
                  pl.BlockSpec((1, D_MODEL), lambda i, j: (0, 0)),
                  pl.BlockSpec((D_MODEL, TN_IN), lambda i, j: (0, j)),
                  pl.BlockSpec((D_MODEL, N_SMALL), lambda i, j: (0, 0))],
        out_specs=[pl.BlockSpec((TM_IN, TN_IN), lambda i, j: (i, j)),
                   pl.BlockSpec((TM_IN, N_SMALL), lambda i, j: (i, 0)),
                   pl.BlockSpec((TM_IN, D_MODEL), lambda i, j: (i, 0)),
                   pl.BlockSpec((D_MODEL, TM_IN), lambda i, j: (0, i))],
        out_shape=[jax.ShapeDtypeStruct((t, N_MAIN), f32), jax.ShapeDtypeStruct((t, N_SMALL), f32),
                   jax.ShapeDtypeStruct((t, D_MODEL), bf16), jax.ShapeDtypeStruct((D_MODEL, t), bf16)],
        compiler_params=_cparams(("arbitrary", "arbitrary")), name="inproj_fwd")(h, nw, w_main, w_small)


def inproj_bwd_x(dproj, dsmall, wt_main, wt_small, h, nw, dh_out):
    t = h.shape[0]
    TM_IN, TN_IN = _row_tile(t, TILES_BWD_X[0], 64), TILES_BWD_X[1]
    nk = N_MAIN // TN_IN

    def body(dp_ref, ds_ref, wt_ref, wts_ref, h_ref, nw_ref, dho_ref, dhi_ref, dnw_ref, acc):
        i, k = pl.program_id(0), pl.program_id(1)

        @pl.when(k == 0)
        def _():
            acc[...] = _dot(ds_ref[...], wts_ref[...])

        acc[...] += _dot(dp_ref[...], wt_ref[...])

        @pl.when(k == nk - 1)
        def _():
            x = h_ref[...]
            r = lax.rsqrt(jnp.mean(x * x, axis=-1, keepdims=True) + EPS)
            xh = x * r
            dxn = acc[...]
            dxh = dxn * nw_ref[...]
            dhi_ref[...] = dho_ref[...] + r * (dxh - xh * jnp.mean(dxh * xh, axis=-1, keepdims=True))
            part = jnp.sum(dxn * xh, axis=0, keepdims=True)

            @pl.when(i == 0)
            def _():
                dnw_ref[...] = part

            @pl.when(i > 0)
            def _():
                dnw_ref[...] += part

    return pl.pallas_call(
        body, grid=(t // TM_IN, nk),
        in_specs=[pl.BlockSpec((TM_IN, TN_IN), lambda i, k: (i, k)),
                  pl.BlockSpec((TM_IN, N_SMALL), lambda i, k: (i, 0)),
                  pl.BlockSpec((TN_IN, D_MODEL), lambda i, k: (k, 0)),
                  pl.BlockSpec((N_SMALL, D_MODEL), lambda i, k: (0, 0)),
                  pl.BlockSpec((TM_IN, D_MODEL), lambda i, k: (i, 0)),
                  pl.BlockSpec((1, D_MODEL), lambda i, k: (0, 0)),
                  pl.BlockSpec((TM_IN, D_MODEL), lambda i, k: (i, 0))],
        out_specs=[pl.BlockSpec((TM_IN, D_MODEL), lambda i, k: (i, 0)),
                   pl.BlockSpec((1, D_MODEL), lambda i, k: (0, 0))],
        out_shape=[jax.ShapeDtypeStruct((t, D_MODEL), f32), jax.ShapeDtypeStruct((1, D_MODEL), f32)],
        scratch_shapes=[pltpu.VMEM((TM_IN, D_MODEL), f32)],
        compiler_params=_cparams(("arbitrary", "arbitrary")), name="inproj_bwd_x")(
            dproj, dsmall, wt_main, wt_small, h, nw, dh_out)


def inproj_bwd_w(xnt, dproj, dsmall):
    t = xnt.shape[1]
    TM_IN, TN_IN = _row_tile(t, TILES_BWD_W[0], 128), TILES_BWD_W[1]
    nt = t // TM_IN

    def body(xnt_ref, dp_ref, ds_ref, dw_ref, dws_ref):
        n, s = pl.program_id(0), pl.program_id(1)
        part = _dot(xnt_ref[...], dp_ref[...])

        @pl.when(s == 0)
        def _():
            dw_ref[...] = part

        @pl.when(s > 0)
        def _():
            dw_ref[...] += part

        @pl.when(n == 0)
        def _():
            ps = _dot(xnt_ref[...], ds_ref[...])

            @pl.when(s == 0)
            def _():
                dws_ref[...] = ps

            @pl.when(s > 0)
            def _():
                dws_ref[...] += ps

    return pl.pallas_call(
        body, grid=(N_MAIN // TN_IN, nt),
        in_specs=[pl.BlockSpec((D_MODEL, TM_IN), lambda n, s: (0, s)),
                  pl.BlockSpec((TM_IN, TN_IN), lambda n, s: (s, n)),
                  pl.BlockSpec((TM_IN, N_SMALL), lambda n, s: (s, 0))],
        out_specs=[pl.BlockSpec((D_MODEL, TN_IN), lambda n, s: (0, n)),
                   pl.BlockSpec((D_MODEL, N_SMALL), lambda n, s: (0, 0))],
        out_shape=[jax.ShapeDtypeStruct((D_MODEL, N_MAIN), f32), jax.ShapeDtypeStruct((D_MODEL, N_SMALL), f32)],
        compiler_params=_cparams(("arbitrary", "arbitrary")), name="inproj_bwd_w")(xnt, dproj, dsmall)


SB_SCALE = HEAD ** -0.5


SB_SUB = 4
SB_KS = SB_SUB * SB_BLOCK


SB_PADR = SB_KS - SB_BLOCK


def _sb_padded(t):
    return t + SB_PADR


def _sb_rows(i, d):
    start = (i + 1) * SB_BLOCK - (d + 1) * SB_KS
    return start, pl.ds(pl.multiple_of(start + SB_PADR, SB_BLOCK), SB_KS)


def _sb_prep(k_ref, v_ref, kw_ref, kn_scr, vb_scr, nb):
    def prep(b, c):
        rows = pl.ds(pl.multiple_of(b * SB_BLOCK, SB_BLOCK), SB_BLOCK)
        pad_rows = pl.ds(pl.multiple_of(SB_PADR + b * SB_BLOCK, SB_BLOCK), SB_BLOCK)
        kn_scr[pad_rows, :] = _rms(k_ref[rows, :], kw_ref[...]).astype(bf16)
        vb_scr[pad_rows, :] = v_ref[rows, :].astype(bf16)
        return c

    lax.fori_loop(0, nb, prep, 0)
    kn_scr[:SB_PADR, :] = jnp.zeros((SB_PADR, HEAD), bf16)
    vb_scr[:SB_PADR, :] = jnp.zeros((SB_PADR, HEAD), bf16)


def _tri_ext(cmp):
    r = lax.broadcasted_iota(jnp.int32, (SB_BLOCK, 2 * SB_BLOCK), 0)
    c = lax.broadcasted_iota(jnp.int32, (SB_BLOCK, 2 * SB_BLOCK), 1)
    return jnp.where((c >= SB_BLOCK) | cmp(r, c), 1.0, 0.0).astype(bf16)


def _sb_suffix(x, carry, tri_ext):
    hi, lo = _split2(x)
    parts = [p[:, c * SB_BLOCK:(c + 1) * SB_BLOCK] for p in (hi, lo) for c in range(SB_SUB)]
    w = jnp.dot(jnp.concatenate(parts, axis=0), tri_ext, preferred_element_type=f32)
    outs = [None] * SB_SUB
    for c in reversed(range(SB_SUB)):
        blk = w[c * SB_BLOCK:(c + 1) * SB_BLOCK] + w[(SB_SUB + c) * SB_BLOCK:(SB_SUB + c + 1) * SB_BLOCK]
        outs[c] = carry + blk[:, :SB_BLOCK]
        carry = carry + blk[:, SB_BLOCK:]
    return jnp.concatenate(outs, axis=1), carry


def _sb_scores(qn, kt, i, start, masked):
    z = lax.dot_general(qn, kt, (NT, ((), ())), preferred_element_type=f32) * SB_SCALE
    lsz = jnp.minimum(z, 0.0) - jnp.log(1.0 + jnp.exp(-jnp.abs(z)))
    lk = lsz - z
    mask = None
    if masked:
        t_idx = i * SB_BLOCK + lax.broadcasted_iota(jnp.int32, (SB_BLOCK, SB_KS), 0)
        s_idx = start + lax.broadcasted_iota(jnp.int32, (SB_BLOCK, SB_KS), 1)
        mask = (s_idx < t_idx) & (s_idx >= PAD_FRONT)
        lk = jnp.where(mask, lk, 0.0)
    return mask, lsz, lk


SB_DEAD = -104.0


def _sb_walk(i, tile, carry):
    n = (i + SB_SUB) // SB_SUB
    live = lambda c: jnp.max(c[1]) > SB_DEAD
    carry = tile(0, carry, True)
    _, carry = lax.while_loop(lambda st: (st[0] <= n - 2) & live(st[1]),
                              lambda st: (st[0] + 1, tile(st[0], st[1], False)), (1, carry))
    return lax.cond((n >= 2) & live(carry), lambda c: tile(n - 1, c, True), lambda c: c, carry)


def sb_fwd(proj, qw, kw):
    t = proj.shape[0]
    nb = t // SB_BLOCK

    def body(q_ref, k_ref, v_ref, qw_ref, kw_ref, o_ref, kn_scr, vb_scr):
        i = pl.program_id(1)

        @pl.when(i == 0)
        def _():
            _sb_prep(k_ref, v_ref, kw_ref, kn_scr, vb_scr, nb)

        qn = _rms(q_ref[...], qw_ref[...]).astype(bf16)
        u_ex = _tri_ext(lambda r, c: r > c)

        def tile(jb, carry, masked):
            acc, r_carry = carry
            start, rows = _sb_rows(i, jb)
            mask, lsz, lk = _sb_scores(qn, kn_scr[rows, :], i, start, masked)
            passed, r_carry = _sb_suffix(lk, r_carry, u_ex)
            a = jnp.exp(lsz + passed)
            if masked:
                a = jnp.where(mask, a, 0.0)
            a_hi, a_lo = _split2(a)
            both = jnp.dot(jnp.concatenate([a_hi, a_lo], axis=0), vb_scr[rows, :], preferred_element_type=f32)
            return acc + (both[:SB_BLOCK] + both[SB_BLOCK:]), r_carry

        zeros = jnp.zeros((SB_BLOCK, HEAD), f32)
        acc, _ = _sb_walk(i, tile, (zeros, zeros))
        o_ref[...] = acc

    qb, cb, vb = C_SBQ // HEAD, C_SBK // HEAD, C_SBV // HEAD
    return pl.pallas_call(
        body, grid=(N_HEADS, nb),
        in_specs=[pl.BlockSpec((SB_BLOCK, HEAD), lambda h, i: (i, qb + h)),
                  pl.BlockSpec((t, HEAD), lambda h, i: (0, cb + h)),
                  pl.BlockSpec((t, HEAD), lambda h, i: (0, vb + h)),
                  pl.BlockSpec((1, HEAD), lambda h, i: (0, 0)),
                  pl.BlockSpec((1, HEAD), lambda h, i: (0, 0))],
        out_specs=pl.BlockSpec((SB_BLOCK, HEAD), lambda h, i: (i, h)),
        out_shape=jax.ShapeDtypeStruct((t, BRANCH), f32),
        scratch_shapes=[pltpu.VMEM((_sb_padded(t), HEAD), bf16), pltpu.VMEM((_sb_padded(t), HEAD), bf16)],
        compiler_params=_cparams(("arbitrary", "arbitrary")), name="sb_fwd")(proj, proj, proj, qw, kw)


def sb_bwd(proj, qw, kw, o, do, dproj):
    t = proj.shape[0]
    nb = t // SB_BLOCK

    def body(q_ref, k_ref, v_ref, qw_ref, kw_ref, o_ref, do_ref, _, dp_ref, dqw_ref, dkw_ref,
             kn_scr, vb_scr, dk_acc, dv_acc, dq_stage, dq_sems, kv_sems):
        h, i = pl.program_id(0), pl.program_id(1)
        step = h * nb + i
        slot = step % 2

        def dq_copy(sl, head):
            return pltpu.make_async_copy(
                dq_stage.at[sl], dp_ref.at[pl.ds(pl.multiple_of(i * SB_BLOCK, SB_BLOCK), SB_BLOCK),
                                           pl.ds(C_SBQ + head * HEAD, HEAD)], dq_sems.at[sl])

        @pl.when(i == 0)
        def _():
            _sb_prep(k_ref, v_ref, kw_ref, kn_scr, vb_scr, nb)
            dk_acc[...] = jnp.zeros_like(dk_acc)
            dv_acc[...] = jnp.zeros_like(dv_acc)

        @pl.when((i == 0) & (h == 0))
        def _():
            dqw_ref[...] = jnp.zeros_like(dqw_ref)
            dkw_ref[...] = jnp.zeros_like(dkw_ref)

        q = q_ref[...]
        rq = lax.rsqrt(jnp.mean(q * q, axis=-1, keepdims=True) + EPS)
        qh = q * rq
        qn = (qh * qw_ref[...]).astype(bf16)
        do_f = do_ref[...]
        dob = do_f.astype(bf16)
        d_row = jnp.sum(dob.astype(f32) * o_ref[...], axis=-1, keepdims=True)
        u_ex = _tri_ext(lambda r, c: r > c)
        u_in = _tri_ext(lambda r, c: r >= c)

        def tile(jb, carry, masked):
            dq, r_carry, f_carry = carry
            start, rows = _sb_rows(i, jb)
            kt = kn_scr[rows, :]
            vt = vb_scr[rows, :]
            mask, lsz, lk = _sb_scores(qn, kt, i, start, masked)
            passed, r_carry = _sb_suffix(lk, r_carry, u_ex)
            a = jnp.exp(lsz + passed)
            if masked:
                a = jnp.where(mask, a, 0.0)
            da = lax.dot_general(dob, vt, (NT, ((), ())), preferred_element_type=f32)
            e = a * da
            e_suf, f_carry = _sb_suffix(e, f_carry, u_in)
            sg = jnp.exp(lsz)
            dz = (e * (1.0 - sg) - (d_row - e_suf) * sg) * SB_SCALE
            if masked:
                dz = jnp.where(mask, dz, 0.0)
            dzb = dz.astype(bf16)
            dq = dq + jnp.dot(dzb, kt, preferred_element_type=f32)
            dk_acc[rows, :] += lax.dot_general(dzb, qn, (TN, ((), ())), preferred_element_type=f32)
            dv_acc[rows, :] += lax.dot_general(a.astype(bf16), dob, (TN, ((), ())), preferred_element_type=f32)
            return dq, r_carry, f_carry

        zeros = jnp.zeros((SB_BLOCK, HEAD), f32)
        dqn, _, _ = _sb_walk(i, tile, (zeros, zeros, zeros))
        gq = dqn * qw_ref[...]
        dqw_ref[...] += jnp.sum(dqn * qh, axis=0, keepdims=True)

        @pl.when(step >= 2)
        def _():
            dq_copy(slot, 0).wait()

        dq_stage[slot] = rq * (gq - qh * jnp.mean(gq * qh, axis=-1, keepdims=True))
        for head in range(N_HEADS):
            @pl.when(h == head)
            def _(head=head):
                dq_copy(slot, head).start()

        @pl.when(i == nb - 1)
        def _():
            def fin(b, c):
                rows = pl.ds(pl.multiple_of(b * SB_BLOCK, SB_BLOCK), SB_BLOCK)
                pad_rows = pl.ds(pl.multiple_of(SB_PADR + b * SB_BLOCK, SB_BLOCK), SB_BLOCK)
                kk = k_ref[rows, :]
                rk = lax.rsqrt(jnp.mean(kk * kk, axis=-1, keepdims=True) + EPS)
                kh = kk * rk
                dkn = dk_acc[pad_rows, :]
                gk = dkn * kw_ref[...]
                dk_acc[pad_rows, :] = rk * (gk - kh * jnp.mean(gk * kh, axis=-1, keepdims=True))
                dkw_ref[...] += jnp.sum(dkn * kh, axis=0, keepdims=True)
                return c

            lax.fori_loop(0, nb, fin, 0)
            for head in range(N_HEADS):
                @pl.when(h == head)
                def _(head=head):
                    outs = [pltpu.make_async_copy(acc.at[pl.ds(SB_PADR, t)], dp_ref.at[:, pl.ds(c0 + head * HEAD, HEAD)],
                                                  kv_sems.at[n])
                            for n, (acc, c0) in enumerate(((dk_acc, C_SBK), (dv_acc, C_SBV)))]
                    for cp in outs:
                        cp.start()
                    for cp in outs:
                        cp.wait()

        @pl.when(step == N_HEADS * nb - 1)
        def _():
            dq_copy(1 - slot, 0).wait()
            dq_copy(slot, 0).wait()

    qb, cb, vb = C_SBQ // HEAD, C_SBK // HEAD, C_SBV // HEAD
    blk = pl.BlockSpec((SB_BLOCK, HEAD), lambda h, i: (i, h))
    wsp = pl.BlockSpec((1, HEAD), lambda h, i: (0, 0))
    any_spec = pl.BlockSpec(memory_space=pl.ANY)
    return pl.pallas_call(
        body, grid=(N_HEADS, nb),
        in_specs=[pl.BlockSpec((SB_BLOCK, HEAD), lambda h, i: (i, qb + h)),
                  pl.BlockSpec((t, HEAD), lambda h, i: (0, cb + h)),
                  pl.BlockSpec((t, HEAD), lambda h, i: (0, vb + h)), wsp, wsp, blk, blk, any_spec],
        out_specs=[any_spec, wsp, wsp],
        out_shape=[jax.ShapeDtypeStruct((t, N_MAIN), f32)] + [jax.ShapeDtypeStruct((1, HEAD), f32)] * 2,
        scratch_shapes=[pltpu.VMEM((_sb_padded(t), HEAD), bf16), pltpu.VMEM((_sb_padded(t), HEAD), bf16),
                        pltpu.VMEM((_sb_padded(t), HEAD), f32), pltpu.VMEM((_sb_padded(t), HEAD), f32),
                        pltpu.VMEM((2, SB_BLOCK, HEAD), f32), pltpu.SemaphoreType.DMA((2,)), pltpu.SemaphoreType.DMA((2,))],
        input_output_aliases={7: 0},
        compiler_params=_cparams(("arbitrary", "arbitrary")), name="sb_bwd")(proj, proj, proj, qw, kw, o, do, dproj)


TM_CONV = 640
CONV_W = 4
GQKV = 3 * BRANCH


def conv_fwd(proj, cw):
    t = proj.shape[0]
    halo_blocks = TM_CONV // 8

    def body(x0_ref, x1_ref, x2_ref, p0_ref, p1_ref, p2_ref, cw_ref, y_ref):
        i = pl.program_id(0)
        for s, (x_ref, p_ref) in enumerate(((x0_ref, p0_ref), (x1_ref, p1_ref), (x2_ref, p2_ref))):
            prev = jnp.where(i > 0, p_ref[...], 0.0)
            xx = jnp.concatenate([prev, x_ref[...]], axis=0)
            cols = slice(s * BRANCH, (s + 1) * BRANCH)
            y = xx[8:] * cw_ref[CONV_W - 1:CONV_W, cols]
            for k in range(CONV_W - 1):
                y = y + pltpu.roll(xx, CONV_W - 1 - k, 0)[8:] * cw_ref[k:k + 1, cols]
            y_ref[:, cols] = y

    c0 = C_GQKV // BRANCH
    xs = [pl.BlockSpec((TM_CONV, BRANCH), functools.partial(lambda i, s: (i, c0 + s), s=s)) for s in range(3)]
    ps = [pl.BlockSpec((8, BRANCH), functools.partial(lambda i, s: (jnp.maximum(i * halo_blocks - 1, 0), c0 + s), s=s))
          for s in range(3)]
    return pl.pallas_call(
        body, grid=(t // TM_CONV,),
        in_specs=xs + ps + [pl.BlockSpec((CONV_W, GQKV), lambda i: (0, 0))],
        out_specs=pl.BlockSpec((TM_CONV, GQKV), lambda i: (i, 0)),
        out_shape=jax.ShapeDtypeStruct((t, GQKV), f32),
        compiler_params=_cparams(("arbitrary",)), name="conv_fwd")(proj, proj, proj, proj, proj, proj, cw)


def conv_bwd(proj, cw, dy, dproj):
    t = proj.shape[0]
    nt = t // TM_CONV
    halo_blocks = TM_CONV // 8

    def body(x0_ref, x1_ref, x2_ref, p0_ref, p1_ref, p2_ref, cw_ref, dy_ref, dyn_ref, _, dx_ref, dw_ref):
        i = pl.program_id(0)

        @pl.when(i == 0)
        def _():
            dw_ref[...] = jnp.zeros_like(dw_ref)

        nxt = jnp.where(i < nt - 1, dyn_ref[...], 0.0)
        dyy = jnp.concatenate([dy_ref[...], nxt], axis=0)
        n_rows = TM_CONV + 8
        dx = dyy[:TM_CONV] * cw_ref[CONV_W - 1:CONV_W, :]
        for k in range(CONV_W - 1):
            sh = CONV_W - 1 - k
            dx = dx + pltpu.roll(dyy, n_rows - sh, 0)[:TM_CONV] * cw_ref[k:k + 1, :]
        dx_ref[...] = dx
        dy_c = dy_ref[...]
        for s, (x_ref, p_ref) in enumerate(((x0_ref, p0_ref), (x1_ref, p1_ref), (x2_ref, p2_ref))):
            prev = jnp.where(i > 0, p_ref[...], 0.0)
            xx = jnp.concatenate([prev, x_ref[...]], axis=0)
            cols = slice(s * BRANCH, (s + 1) * BRANCH)
            for k in range(CONV_W):
                sh = CONV_W - 1 - k
                xs = xx[8:] if sh == 0 else pltpu.roll(xx, sh, 0)[8:]
                dw_ref[k:k + 1, cols] += jnp.sum(xs * dy_c[:, cols], axis=0, keepdims=True)

    c0 = C_GQKV // BRANCH
    xs = [pl.BlockSpec((TM_CONV, BRANCH), functools.partial(lambda i, s: (i, c0 + s), s=s)) for s in range(3)]
    ps = [pl.BlockSpec((8, BRANCH), functools.partial(lambda i, s: (jnp.maximum(i * halo_blocks - 1, 0), c0 + s), s=s))
          for s in range(3)]
    return pl.pallas_call(
        body, grid=(nt,),
        in_specs=xs + ps + [pl.BlockSpec((CONV_W, GQKV), lambda i: (0, 0)),
                            pl.BlockSpec((TM_CONV, GQKV), lambda i: (i, 0)),
                            pl.BlockSpec((8, GQKV), lambda i: (jnp.minimum((i + 1) * halo_blocks, nt * halo_blocks - 1), 0)),
                            pl.BlockSpec(memory_space=pl.ANY)],
        out_specs=[pl.BlockSpec((TM_CONV, GQKV), lambda i: (i, C_GQKV // GQKV)), pl.BlockSpec((CONV_W, GQKV), lambda i: (0, 0))],
        out_shape=[jax.ShapeDtypeStruct((t, N_MAIN), f32), jax.ShapeDtypeStruct((CONV_W, GQKV), f32)],
        input_output_aliases={9: 0},
        compiler_params=_cparams(("arbitrary",)), name="conv_bwd")(proj, proj, proj, proj, proj, proj, cw, dy, dy, dproj)


def _iota2(n, m, d):
    return lax.broadcasted_iota(jnp.int32, (n, m), d)


def _lane_pick(row_or_mat, idx):
    lanes = lax.broadcasted_iota(jnp.int32, row_or_mat.shape, row_or_mat.ndim - 1)
    return jnp.sum(jnp.where(lanes == idx, row_or_mat, 0.0), axis=-1, keepdims=True)


def _cumsum_consts():
    i = np.arange(CHUNK)
    incl = i[None, :] <= i[:, None]
    suf = i[None, :] > i[:, None]
    return np.concatenate([incl, suf], 0).astype(np.float32)


HG_LEVELS = (64, 32, 16, 8, 4, 2)


def _hgrn_consts():
    i = np.arange(CHUNK)
    rows = [i[None, :] <= i[:, None], i[None, :] > i[:, None]]
    for b in HG_LEVELS:
        ref = (i // b) * b + b // 2 - 1
        second = (i % b) >= b // 2
        rows.append((i[None, :] > ref[:, None]) & (i[None, :] <= i[:, None]) & second[:, None])
        rows.append((i[None, :] > i[:, None]) & (i[None, :] <= ref[:, None]) & (~second)[:, None])
    return np.concatenate(rows, 0).astype(np.float32)


N_SQUARINGS = 5


def _solve_chain(ms, rhss):
    xs = [r - mm(m, r) for m, r in zip(ms, rhss)]
    powers = [list(ms)]
    for _ in range(N_SQUARINGS):
        powers.append([mm(p, p) for p in powers[-1]])
        xs = [x + mm(p, x) for p, x in zip(powers[-1], xs)]
    return tuple(xs), powers


@jax.custom_vjp
def unit_lower_solve_multi(ms, rhss):
    return _solve_chain(ms, rhss)[0]


def _solve_fwd(ms, rhss):
    xs, powers = _solve_chain(ms, rhss)
    return xs, (powers, xs)


def _solve_bwd(res, gs):
    powers, xs = res
    ys = [g - mm_tn(p, g) for p, g in zip(powers[0], gs)]
    for ps in powers[1:]:
        ys = [y + mm_tn(p, y) for p, y in zip(ps, ys)]
    return tuple(-mm_nt(y, x) for y, x in zip(ys, xs)), tuple(ys)


unit_lower_solve_multi.defvjp(_solve_fwd, _solve_bwd)

HC = N_HEADS * CHUNK
BATCH0 = ((0,), (0,))


def _bdot(a, b, contract):
    return lax.dot_general(a.astype(bf16), b.astype(bf16), (contract, BATCH0), preferred_element_type=f32)


B_NN = ((2,), (1,))
B_NT = ((2,), (2,))
B_TN = ((1,), (1,))


@jax.custom_vjp
def bmm(a, b):
    return _bdot(a, b, B_NN)


bmm.defvjp(lambda a, b: (_bdot(a, b, B_NN), (a, b)),
           lambda r, g: (_bdot(g, r[1], B_NT), _bdot(r[0], g, B_TN)))


@jax.custom_vjp
def bmm_nt(a, b):
    return _bdot(a, b, B_NT)


bmm_nt.defvjp(lambda a, b: (_bdot(a, b, B_NT), (a, b)),
              lambda r, g: (_bdot(g, r[1], B_NN), _bdot(g, r[0], B_TN)))


@jax.custom_vjp
def bmm_tn(a, b):
    return _bdot(a, b, B_TN)


bmm_tn.defvjp(lambda a, b: (_bdot(a, b, B_TN), (a, b)),
              lambda r, g: (_bdot(r[1], g, B_NT), _bdot(r[0], g, B_NN)))


def _stack_heads(x):
    return jnp.concatenate([x[:, h * HEAD:(h + 1) * HEAD] for h in range(N_HEADS)], axis=0)


def _unstack_heads(x):
    return jnp.concatenate([x[h * CHUNK:(h + 1) * CHUNK] for h in range(N_HEADS)], axis=1)


REC_CHUNKS = 5
REC_ROWS = REC_CHUNKS * CHUNK


def _interleave(gens):
    n = len(gens)
    sends, results, done = [None] * n, [None] * n, [False] * n
    while not all(done):
        asks = []
        for j in range(n):
            if done[j]:
                continue
            try:
                ask = gens[j].send(sends[j])
                if ask is not None:
                    asks.append((j, ask))
            except StopIteration as stop:
                results[j], done[j] = stop.value, True
            sends[j] = None
        if asks:
            xs = unit_lower_solve_multi(tuple(a[1] for _, a in asks), tuple(a[2] for _, a in asks))
            for (j, _), x in zip(asks, xs):
                sends[j] = x
    return results


def _chunks_of(a):
    return [a[j * CHUNK:(j + 1) * CHUNK] for j in range(REC_CHUNKS)]


def _gdn_step(state, ypre, small, gz, a_log, dt_b, on_w, c2, first_chunk):
    gens = [_gdn_intra(y, s, a_log, dt_b, c2, _vmask(first_chunk + j))
            for j, (y, s) in enumerate(zip(_chunks_of(ypre), _chunks_of(small)))]
    outs = []
    for intra, z in zip(_interleave(gens), _chunks_of(gz)):
        state, o = _gdn_inter(state, intra, z, on_w)
        outs.append(o)
    return state, jnp.concatenate(outs, axis=0)


def _gdn_inter(state, intra, gz, on_w):
    u, w, aqk, q_dec, k_dec, g_last = intra
    per_head = lambda a: a.reshape(N_HEADS, CHUNK, HEAD)
    v_new = u - bmm(per_head(w), state).reshape(HC, HEAD)
    o = bmm(per_head(q_dec), state).reshape(HC, HEAD) + mm(aqk, v_new)
    new_state = state * g_last + bmm_tn(per_head(k_dec), per_head(v_new))
    return new_state, _unstack_heads(_rms(o, on_w)) * _silu(gz)


def _gdn_intra(ypre, small, a_log, dt_b, c2, vm):
    r = _iota2(HC, HC, 0)
    c = _iota2(HC, HC, 1)
    same_head = (r >> 6) == (c >> 6)
    causal = same_head & (r >= c)
    strict = same_head & (r > c)
    q = _silu(_stack_heads(ypre[:, :BRANCH]))
    k = _silu(_stack_heads(ypre[:, BRANCH:2 * BRANCH]))
    v = _silu(_stack_heads(ypre[:, 2 * BRANCH:]))
    q = q * lax.rsqrt(jnp.sum(q * q, axis=-1, keepdims=True) + EPS) * (HEAD ** -0.5)
    k = k * lax.rsqrt(jnp.sum(k * k, axis=-1, keepdims=True) + EPS)
    col = lambda f: jnp.concatenate([f(h) for h in range(N_HEADS)], axis=0)
    chunk_col = lambda x: jnp.broadcast_to(x, (CHUNK, 1))
    beta = _sigmoid(col(lambda h: _lane_pick(small, h))) * col(lambda h: vm)
    g = (-jnp.exp(col(lambda h: chunk_col(_lane_pick(a_log, h))))
         * _softplus(col(lambda h: _lane_pick(small, N_HEADS + h)) + col(lambda h: chunk_col(_lane_pick(dt_b, h)))))
    g_l = _unstack_heads(jnp.broadcast_to(g, (HC, HEAD)))
    e2 = cmm(c2, g_l)
    yield
    gc = _stack_heads(e2[:CHUNK])
    gsuf = _stack_heads(e2[CHUNK:])
    g_row = jnp.broadcast_to(jnp.transpose(gc)[0:1, :], (HC, HC))
    g_col = jnp.concatenate([gc, gc], axis=1)
    dec = jnp.where(causal, jnp.exp(jnp.minimum(g_col - g_row, 0.0)), 0.0)
    kb = k * beta
    kk = mm_nt(kb, k)
    qk = mm_nt(q, k)
    yield
    m = jnp.where(strict, kk * dec, 0.0)
    x = yield ("solve", m, jnp.concatenate([v * beta, kb * jnp.exp(gc)], axis=1))
    aqk = jnp.where(causal, qk * dec, 0.0)
    tot = jnp.sum(g_l, axis=0, keepdims=True)
    g_last = jnp.exp(jnp.stack([tot[:, h * HEAD:(h + 1) * HEAD] for h in range(N_HEADS)], axis=0))
    return x[:, :HEAD], x[:, HEAD:], aqk, q * jnp.exp(gc), k * jnp.exp(gsuf), g_last


def _hgrn_step(state, hq, hf, hi, hz, lb, on_w, cm, first_chunk):
    gens = [_hgrn_intra(q, f, i, lb, cm, _vmask(first_chunk + j))
            for j, (q, f, i) in enumerate(zip(_chunks_of(hq), _chunks_of(hf), _chunks_of(hi)))]
    outs = []
    for (q_dec, k_dec, v, o_intra, g_end), z in zip(_interleave(gens), _chunks_of(hz)):
        per_head = lambda a: a.reshape(N_HEADS, CHUNK, HEAD)
        o = bmm_nt(per_head(q_dec), state).reshape(HC, HEAD) + o_intra
        state = state * g_end + bmm_tn(per_head(v), per_head(k_dec))
        outs.append(_unstack_heads(_rms(o, on_w)) * _silu(z))
    return state, jnp.concatenate(outs, axis=0)


def _hgrn_intra(hq, hf, hi, lb, cm, vm):
    r = _iota2(HC, HC, 0)
    c = _iota2(HC, HC, 1)
    forget = lb + (1.0 - lb) * _sigmoid(hf)
    g_l = jnp.log(forget)
    e = cmm(cm, g_l)
    q = _stack_heads(_silu(hq))
    k = _stack_heads((1.0 - lb) * _sigmoid(-hf))
    v = _stack_heads(hi * vm)
    yield
    sect = lambda n: _stack_heads(e[n * CHUNK:(n + 1) * CHUNK])
    gc, gsuf = sect(0), sect(1)
    a = jnp.where(r == c, jnp.sum(q * k, axis=-1, keepdims=True), 0.0)
    for li, b in enumerate(HG_LEVELS):
        sh = b.bit_length() - 1
        pair = ((r >> sh) == (c >> sh)) & ((r & (b - 1)) >= b // 2) & ((c & (b - 1)) < b // 2)
        a = a + jnp.where(pair, mm_nt(q * jnp.exp(sect(2 + 2 * li)), k * jnp.exp(sect(3 + 2 * li))), 0.0)
    yield
    o_intra = mm(a, v)
    tot = jnp.sum(g_l, axis=0, keepdims=True)
    g_end = jnp.exp(jnp.stack([tot[:, h * HEAD:(h + 1) * HEAD] for h in range(N_HEADS)], axis=0))
    return q * jnp.exp(gc), k * jnp.exp(gsuf), v, o_intra, g_end


def _vmask(chunk_idx):
    rows = chunk_idx * CHUNK + lax.broadcasted_iota(jnp.int32, (CHUNK, 1), 0)
    return jnp.where(rows >= PAD_FRONT, 1.0, 0.0)


def _row(n):
    return pl.BlockSpec((1, n), lambda i: (0, 0))


def gdn_fwd(ypre, small, proj, a_log, dt_b, on_w):
    t = ypre.shape[0]
    nc = t // REC_ROWS
    c2 = jnp.asarray(_cumsum_consts(), bf16)

    def body(y_ref, s_ref, z_ref, al_ref, dt_ref, on_ref, c2_ref, o_ref, st_ref, state):
        i = pl.program_id(0)

        @pl.when(i == 0)
        def _():
            state[...] = jnp.zeros_like(state)

        s_in = state[...]
        st_ref[0] = s_in
        s_new, out = _gdn_step(s_in, y_ref[...], s_ref[...], z_ref[...], al_ref[...], dt_ref[...], on_ref[...],
                               c2_ref[...], i * REC_CHUNKS)
        state[...] = s_new
        o_ref[...] = out

    return pl.pallas_call(
        body, grid=(nc,),
        in_specs=[pl.BlockSpec((REC_ROWS,GQKV), lambda i: (i, 0)), pl.BlockSpec((REC_ROWS,N_SMALL), lambda i: (i, 0)),
                  pl.BlockSpec((REC_ROWS,BRANCH), lambda i: (i, C_GZ // BRANCH)), _row(128), _row(128), _row(128),
                  pl.BlockSpec((2 * CHUNK, CHUNK), lambda i: (0, 0))],
        out_specs=[pl.BlockSpec((REC_ROWS,BRANCH), lambda i: (i, 0)),
                   pl.BlockSpec((1, N_HEADS, HEAD, HEAD), lambda i: (i, 0, 0, 0))],
        out_shape=[jax.ShapeDtypeStruct((t, BRANCH), f32), jax.ShapeDtypeStruct((nc, N_HEADS, HEAD, HEAD), f32)],
        scratch_shapes=[pltpu.VMEM((N_HEADS, HEAD, HEAD), f32)],
        compiler_params=_cparams(("arbitrary",)), name="gdn_fwd")(ypre, small, proj, a_log, dt_b, on_w, c2)


def gdn_bwd(ypre, small, proj, a_log, dt_b, on_w, states, d_out, dproj):
    t = ypre.shape[0]
    nc = t // REC_ROWS
    c2 = jnp.asarray(_cumsum_consts(), bf16)

    def body(y_ref, s_ref, z_ref, al_ref, dt_ref, on_ref, c2_ref, st_ref, do_ref, _,
             dy_ref, ds_ref, dz_ref, dal_ref, ddt_ref, don_ref, dstate):
        i = pl.program_id(0)

        @pl.when(i == 0)
        def _():
            dstate[...] = jnp.zeros_like(dstate)
            dal_ref[...] = jnp.zeros_like(dal_ref)
            ddt_ref[...] = jnp.zeros_like(ddt_ref)
            don_ref[...] = jnp.zeros_like(don_ref)

        c2v = c2_ref[...]
        fn = lambda s, y, sm, z, al, dt, on: _gdn_step(s, y, sm, z, al, dt, on, c2v, (nc - 1 - i) * REC_CHUNKS)
        _, vjp = jax.vjp(fn, st_ref[0], y_ref[...], s_ref[...], z_ref[...], al_ref[...], dt_ref[...], on_ref[...])
        d_s, d_y, d_sm, d_z, d_al, d_dt, d_on = vjp((dstate[...], do_ref[...]))
        dstate[...] = d_s
        dy_ref[...] = d_y
        ds_ref[...] = d_sm
        dz_ref[...] = d_z
        dal_ref[...] += d_al
        ddt_ref[...] += d_dt
        don_ref[...] += d_on

    rev = lambda i: (nc - 1 - i, 0)
    return pl.pallas_call(
        body, grid=(nc,),
        in_specs=[pl.BlockSpec((REC_ROWS,GQKV), rev), pl.BlockSpec((REC_ROWS,N_SMALL), rev),
                  pl.BlockSpec((REC_ROWS,BRANCH), lambda i: (nc - 1 - i, C_GZ // BRANCH)), _row(128), _row(128), _row(128),
                  pl.BlockSpec((2 * CHUNK, CHUNK), lambda i: (0, 0)),
                  pl.BlockSpec((1, N_HEADS, HEAD, HEAD), lambda i: (nc - 1 - i, 0, 0, 0)),
                  pl.BlockSpec((REC_ROWS,BRANCH), rev), pl.BlockSpec(memory_space=pl.ANY)],
        out_specs=[pl.BlockSpec((REC_ROWS,GQKV), rev), pl.BlockSpec((REC_ROWS,N_SMALL), rev),
                   pl.BlockSpec((REC_ROWS,BRANCH), lambda i: (nc - 1 - i, C_GZ // BRANCH)), _row(128), _row(128), _row(128)],
        out_shape=[jax.ShapeDtypeStruct((t, GQKV), f32), jax.ShapeDtypeStruct((t, N_SMALL), f32),
                   jax.ShapeDtypeStruct((t, N_MAIN), f32)] + [jax.ShapeDtypeStruct((1, 128), f32)] * 3,
        scratch_shapes=[pltpu.VMEM((N_HEADS, HEAD, HEAD), f32)], input_output_aliases={9: 2},
        compiler_params=_cparams(("arbitrary",)), name="gdn_bwd")(
            ypre, small, proj, a_log, dt_b, on_w, c2, states, d_out, dproj)


def hgrn_fwd(proj, lb, on_w):
    t = proj.shape[0]
    nc = t // REC_ROWS
    cm = jnp.asarray(_hgrn_consts(), bf16)
    ncm = cm.shape[0]

    def body(q_ref, f_ref, i_ref, z_ref, lb_ref, on_ref, cm_ref, o_ref, st_ref, state):
        i = pl.program_id(0)

        @pl.when(i == 0)
        def _():
            state[...] = jnp.zeros_like(state)

        s_in = state[...]
        st_ref[0] = s_in
        s_new, out = _hgrn_step(s_in, q_ref[...], f_ref[...], i_ref[...], z_ref[...], lb_ref[...], on_ref[...],
                                cm_ref[...], i * REC_CHUNKS)
        state[...] = s_new
        o_ref[...] = out

    sec = lambda off: pl.BlockSpec((REC_ROWS,BRANCH), functools.partial(lambda i, b: (i, b), b=off // BRANCH))
    return pl.pallas_call(
        body, grid=(nc,),
        in_specs=[sec(C_HQ), sec(C_HF), sec(C_HI), sec(C_HZ), _row(BRANCH), _row(128),
                  pl.BlockSpec((ncm, CHUNK), lambda i: (0, 0))],
        out_specs=[pl.BlockSpec((REC_ROWS,BRANCH), lambda i: (i, 0)),
                   pl.BlockSpec((1, N_HEADS, HEAD, HEAD), lambda i: (i, 0, 0, 0))],
        out_shape=[jax.ShapeDtypeStruct((t, BRANCH), f32), jax.ShapeDtypeStruct((nc, N_HEADS, HEAD, HEAD), f32)],
        scratch_shapes=[pltpu.VMEM((N_HEADS, HEAD, HEAD), f32)],
        compiler_params=_cparams(("arbitrary",)), name="hgrn_fwd")(proj, proj, proj, proj, lb, on_w, cm)


def hgrn_bwd(proj, lb, on_w, states, d_out, dproj):
    t = proj.shape[0]
    nc = t // REC_ROWS
    cm = jnp.asarray(_hgrn_consts(), bf16)
    ncm = cm.shape[0]

    def body(q_ref, f_ref, i_ref, z_ref, lb_ref, on_ref, cm_ref, st_ref, do_ref, _, dh_ref, dlb_ref, don_ref, dstate):
        i = pl.program_id(0)

        @pl.when(i == 0)
        def _():
            dstate[...] = jnp.zeros_like(dstate)
            dlb_ref[...] = jnp.zeros_like(dlb_ref)
            don_ref[...] = jnp.zeros_like(don_ref)

        cmv = cm_ref[...]
        fn = lambda s, a, b, c, d, l, on: _hgrn_step(s, a, b, c, d, l, on, cmv, (nc - 1 - i) * REC_CHUNKS)
        _, vjp = jax.vjp(fn, st_ref[0], q_ref[...], f_ref[...], i_ref[...], z_ref[...], lb_ref[...], on_ref[...])
        d_s, d_q, d_f, d_i, d_z, d_lb, d_on = vjp((dstate[...], do_ref[...]))
        dstate[...] = d_s
        dh_ref[...] = jnp.concatenate([d_q, d_f, d_i, d_z], axis=1)
        dlb_ref[...] += d_lb
        don_ref[...] += d_on

    rev = lambda i: (nc - 1 - i, 0)
    sec = lambda off: pl.BlockSpec((REC_ROWS,BRANCH), functools.partial(lambda i, b: (nc - 1 - i, b), b=off // BRANCH))
    return pl.pallas_call(
        body, grid=(nc,),
        in_specs=[sec(C_HQ), sec(C_HF), sec(C_HI), sec(C_HZ), _row(BRANCH), _row(128),
                  pl.BlockSpec((ncm, CHUNK), lambda i: (0, 0)),
                  pl.BlockSpec((1, N_HEADS, HEAD, HEAD), lambda i: (nc - 1 - i, 0, 0, 0)),
                  pl.BlockSpec((REC_ROWS,BRANCH), rev), pl.BlockSpec(memory_space=pl.ANY)],
        out_specs=[pl.BlockSpec((REC_ROWS,4 * BRANCH), lambda i: (nc - 1 - i, C_HQ // (4 * BRANCH))), _row(BRANCH), _row(128)],
        out_shape=[jax.ShapeDtypeStruct((t, N_MAIN), f32), jax.ShapeDtypeStruct((1, BRANCH), f32),
                   jax.ShapeDtypeStruct((1, 128), f32)],
        scratch_shapes=[pltpu.VMEM((N_HEADS, HEAD, HEAD), f32)], input_output_aliases={9: 0},
        compiler_params=_cparams(("arbitrary",)), name="hgrn_bwd")(proj, proj, proj, proj, lb, on_w, cm, states, d_out, dproj)


TM_MG = 320


def _const_spec(shape):
    nd = len(shape)
    return pl.BlockSpec(shape, lambda i: (0,) * nd, pipeline_mode=pl.Buffered(1))


def merge_fwd(osb, proj, ogd, ohg, wb, wo, h):
    t = h.shape[0]

    def body(osb_ref, ogd_ref, ohg_ref, zm_ref, wb_ref, wo_ref, h_ref, out_ref):
        a = osb_ref[...] * _silu(zm_ref[:, :BRANCH])
        gate = lambda b: _sigmoid(zm_ref[:, C_MIX + b * D_MODEL:C_MIX + (b + 1) * D_MODEL])
        y = (gate(0) * _dot(a, wb_ref[0]) + gate(1) * _dot(ogd_ref[...], wb_ref[1])
             + gate(2) * _dot(ohg_ref[...], wb_ref[2]))
        out_ref[...] = h_ref[...] + _dot(y, wo_ref[...])

    br = pl.BlockSpec((TM_MG, BRANCH), lambda i: (i, 0))
    return pl.pallas_call(
        body, grid=(t // TM_MG,),
        in_specs=[br, br, br, pl.BlockSpec((TM_MG, W_MERGE), lambda i: (i, 0)),
                  _const_spec((3, BRANCH, D_MODEL)), _const_spec((D_MODEL, D_MODEL)),
                  pl.BlockSpec((TM_MG, D_MODEL), lambda i: (i, 0))],
        out_specs=pl.BlockSpec((TM_MG, D_MODEL), lambda i: (i, 0)),
        out_shape=jax.ShapeDtypeStruct((t, D_MODEL), f32),
        compiler_params=_cparams(("arbitrary",)), name="merge_fwd")(osb, ogd, ohg, proj, wb, wo, h)


def merge_bwd(osb, proj, ogd, ohg, wb, wbt, wot, dh):
    t = dh.shape[0]

    def body(osb_ref, ogd_ref, ohg_ref, zm_ref, wb_ref, wbt_ref, wot_ref, dh_ref,
             dosb_ref, dogd_ref, dohg_ref, dzm_ref, dwo_ref, dwb_ref):
        i = pl.program_id(0)

        @pl.when(i == 0)
        def _():
            dwo_ref[...] = jnp.zeros_like(dwo_ref)
            dwb_ref[...] = jnp.zeros_like(dwb_ref)

        osb = osb_ref[...]
        sbz = zm_ref[:, :BRANCH]
        sgz = _sigmoid(sbz)
        sz = sbz * sgz
        branch_in = (osb * sz, ogd_ref[...], ohg_ref[...])
        dh_v = dh_ref[...]
        dy = _dot(dh_v, wot_ref[...])
        y = jnp.zeros((TM_MG, D_MODEL), f32)
        d_in = []
        for b in range(3):
            cols = slice(C_MIX + b * D_MODEL, C_MIX + (b + 1) * D_MODEL)
            p = _dot(branch_in[b], wb_ref[b])
            g = _sigmoid(zm_ref[:, cols])
            y = y + g * p
            dp = dy * g
            dzm_ref[:, cols] = dy * p * g * (1.0 - g)
            d_in.append(_dot(dp, wbt_ref[b]))
            dwb_ref[b] += _dot(branch_in[b], dp, TN)
        dwo_ref[...] += _dot(y, dh_v, TN)
        dosb_ref[...] = d_in[0] * sz
        dzm_ref[:, :BRANCH] = d_in[0] * osb * (sgz * (1.0 + sbz * (1.0 - sgz)))
        dogd_ref[...] = d_in[1]
        dohg_ref[...] = d_in[2]

    br = pl.BlockSpec((TM_MG, BRANCH), lambda i: (i, 0))
    zm = pl.BlockSpec((TM_MG, W_MERGE), lambda i: (i, 0))
    return pl.pallas_call(
        body, grid=(t // TM_MG,),
        in_specs=[br, br, br, zm,
                  _const_spec((3, BRANCH, D_MODEL)), _const_spec((3, D_MODEL, BRANCH)), _const_spec((D_MODEL, D_MODEL)),
                  pl.BlockSpec((TM_MG, D_MODEL), lambda i: (i, 0))],
        out_specs=[br, br, br, zm, _const_spec((D_MODEL, D_MODEL)), _const_spec((3, BRANCH, D_MODEL))],
        out_shape=[jax.ShapeDtypeStruct((t, BRANCH), f32)] * 3 + [jax.ShapeDtypeStruct((t, N_MAIN), f32),
                   jax.ShapeDtypeStruct((D_MODEL, D_MODEL), f32), jax.ShapeDtypeStruct((3, BRANCH, D_MODEL), f32)],
        compiler_params=_cparams(("arbitrary",)), name="merge_bwd")(osb, ogd, ohg, proj, wb, wbt, wot, dh)


def loss_head(h, target):
    t = h.shape[0]
    nb = t // SB_BLOCK

    def body(h_ref, t_ref, dh_ref, loss_ref):
        i = pl.program_id(0)

        @pl.when(i == 0)
        def _():
            loss_ref[...] = jnp.zeros_like(loss_ref)
            dh_ref[...] = jnp.zeros_like(dh_ref)

        @pl.when(i > 0)
        def _():
            err = h_ref[...] - t_ref[...]
            dh_ref[...] = err * (1.0 / D_MODEL)
            loss_ref[...] += jnp.broadcast_to(jnp.sum(err * err) * (0.5 / D_MODEL), loss_ref.shape)

    return pl.pallas_call(
        body, grid=(nb,),
        in_specs=[pl.BlockSpec((SB_BLOCK, D_MODEL), lambda i: (i, 0)),
                  pl.BlockSpec((SB_BLOCK, D_MODEL), lambda i: (jnp.maximum(i - 1, 0), 0))],
        out_specs=[pl.BlockSpec((SB_BLOCK, D_MODEL), lambda i: (i, 0)), pl.BlockSpec((1, 128), lambda i: (0, 0))],
        out_shape=[jax.ShapeDtypeStruct((t, D_MODEL), f32), jax.ShapeDtypeStruct((1, 128), f32)],
        compiler_params=_cparams(("arbitrary",)), name="loss_head")(h, target)


def adamw(parts, w, m, v, rows_per_step, name):
    r, c = w.shape
    tr = min(rows_per_step, r)

    def body(p_ref, w_ref, m_ref, v_ref, g_ref, d_ref, nm_ref, nv_ref):
        g = p_ref[0].astype(f32)
        for k in range(1, N_DEV):
            g = g + p_ref[k].astype(f32)
        m_new = ADAM_B1 * m_ref[...] + (1.0 - ADAM_B1) * g
        v_new = ADAM_B2 * v_ref[...] + (1.0 - ADAM_B2) * jnp.square(g)
        m_hat = m_new / (1.0 - ADAM_B1 ** ADAM_STEP)
        v_hat = v_new / (1.0 - ADAM_B2 ** ADAM_STEP)
        g_ref[...] = g
        d_ref[...] = -ADAM_LR * (m_hat / (jnp.sqrt(v_hat) + ADAM_EPS) + ADAM_WD * w_ref[...])
        nm_ref[...] = m_new
        nv_ref[...] = v_new

    blk = pl.BlockSpec((tr, c), lambda i: (i, 0))
    return pl.pallas_call(
        body, grid=(r // tr,),
        in_specs=[pl.BlockSpec((N_DEV, tr, c), lambda i: (0, i, 0)), blk, blk, blk],
        out_specs=[blk] * 4, out_shape=[jax.ShapeDtypeStruct((r, c), f32)] * 4,
        compiler_params=_cparams(("arbitrary",)), name=name)(parts, w, m, v)


def _mesh_pos():
    return lax.axis_index("x"), lax.axis_index("y"), lax.axis_index("c")


def _peer(pos, k):
    x, y, c = pos
    return (1 - x if k & 4 else x, 1 - y if k & 2 else y, 1 - c if k & 1 else c)


def _lin(pos):
    return 4 * pos[0] + 2 * pos[1] + pos[2]


def exchange(srcs, scatter, name):
    n = len(srcs)
    shapes = [s.shape[1:] if sc else s.shape for s, sc in zip(srcs, scatter)]

    def body(*refs):
        src_refs, dst_refs = refs[:n], refs[n:2 * n]
        send_sems, recv_sems, local_sems = refs[2 * n:]
        me = _mesh_pos()
        me_lin = _lin(me)
        sends, recvs, locals_ = [], [], []
        for t in range(n):
            own = src_refs[t].at[me_lin] if scatter[t] else src_refs[t]
            locals_.append(pltpu.make_async_copy(own, dst_refs[t].at[me_lin], local_sems.at[t]))
            for k in range(1, N_DEV):
                peer = _peer(me, k)
                src = src_refs[t].at[_lin(peer)] if scatter[t] else src_refs[t]
                sends.append(pltpu.make_async_remote_copy(
                    src_ref=src, dst_ref=dst_refs[t].at[me_lin], send_sem=send_sems.at[t, k - 1],
                    recv_sem=recv_sems.at[t, k - 1], device_id=peer, device_id_type=MESH))
                recvs.append(pltpu.make_async_remote_copy(
                    src_ref=src, dst_ref=dst_refs[t].at[_lin(peer)], send_sem=send_sems.at[t, k - 1],
                    recv_sem=recv_sems.at[t, k - 1], device_id=peer, device_id_type=MESH))
        for cp in locals_ + sends:
            cp.start()
        for cp in sends:
            cp.wait_send()
        for cp in recvs:
            cp.wait_recv()
        for cp in locals_:
            cp.wait()

    any_spec = pl.BlockSpec(memory_space=pl.ANY)
    return pl.pallas_call(
        body, in_specs=[any_spec] * n, out_specs=[any_spec] * n,
        out_shape=[jax.ShapeDtypeStruct((N_DEV,) + tuple(sh), s.dtype) for sh, s in zip(shapes, srcs)],
        scratch_shapes=[pltpu.SemaphoreType.DMA((n, N_DEV - 1)), pltpu.SemaphoreType.DMA((n, N_DEV - 1)),
                        pltpu.SemaphoreType.DMA((n,))],
        compiler_params=pltpu.CompilerParams(has_side_effects=True), name=name)(*srcs)


def gather_two_level(srcs, name):
    n = len(srcs)
    n_cp = N_DEV - 1

    def body(*refs):
        src_refs, dst_refs = refs[:n], refs[n:2 * n]
        send_sems, recv_sems, local_sems = refs[2 * n:]
        x, y, c = _mesh_pos()
        me, sibling = (x, y, c), (x, y, 1 - c)
        chips = [(1 - x, y), (x, 1 - y), (1 - x, 1 - y)]

        def copy(t, k, block, to, src=None):
            slot = dst_refs[t].at[_lin(block)]
            return pltpu.make_async_remote_copy(
                src_ref=slot if src is None else src, dst_ref=slot, send_sem=send_sems.at[t, k],
                recv_sem=recv_sems.at[t, k], device_id=to, device_id_type=MESH)

        mine, first, passed = [], [], []
        for t in range(n):
            mine.append(pltpu.make_async_copy(src_refs[t], dst_refs[t].at[_lin(me)], local_sems.at[t]))
            first.append(copy(t, 0, me, sibling, src=src_refs[t]))
            first += [copy(t, 1 + j, me, (*chip, c), src=src_refs[t]) for j, chip in enumerate(chips)]
        for cp in mine + first:
            cp.start()
        for j, chip in enumerate(chips):
            for t in range(n):
                copy(t, 1 + j, (*chip, c), me).wait_recv()
                fwd = copy(t, 4 + j, (*chip, c), sibling)
                fwd.start()
                passed.append(fwd)
        for t in range(n):
            copy(t, 0, sibling, me).wait_recv()
            for j, chip in enumerate(chips):
                copy(t, 4 + j, (*chip, 1 - c), me).wait_recv()
        for cp in first + passed:
            cp.wait_send()
        for cp in mine:
            cp.wait()

    any_spec = pl.BlockSpec(memory_space=pl.ANY)
    return pl.pallas_call(
        body, in_specs=[any_spec] * n, out_specs=[any_spec] * n,
        out_shape=[jax.ShapeDtypeStruct((N_DEV,) + tuple(s.shape), s.dtype) for s in srcs],
        scratch_shapes=[pltpu.SemaphoreType.DMA((n, n_cp)), pltpu.SemaphoreType.DMA((n, n_cp)),
                        pltpu.SemaphoreType.DMA((n,))],
        compiler_params=pltpu.CompilerParams(has_side_effects=True), name=name)(*srcs)


PACK_ROWS = 104


def _pad_rows(a, rows):
    return jnp.pad(a, ((0, rows - a.shape[0]), (0, 0)))


def _pad_lanes(a):
    return jnp.pad(a, ((0, 0), (0, 128 - a.shape[1])))


def _pack(norm_w, sbq, sbk, alog, dtb, gon, lbl, hon, loss_row):
    parts = [norm_w.reshape(32, 128), _pad_rows(sbq, 8), _pad_rows(sbk, 8), _pad_rows(_pad_lanes(alog), 8),
             _pad_rows(_pad_lanes(dtb), 8), _pad_rows(gon, 8), lbl.reshape(16, 128), _pad_rows(hon, 8),
             _pad_rows(loss_row, 8)]
    return jnp.concatenate(parts, axis=0)


def _unpack(p):
    return dict(norm_w=p[0:32].reshape(DEPTH, D_MODEL), sb_q_norm=p[32:36], sb_k_norm=p[40:44],
                gdn_a_log=p[48:52, :N_HEADS], gdn_dt_bias=p[56:60, :N_HEADS], gdn_out_norm=p[64:68],
                hgrn_lb_logits=p[72:88].reshape(DEPTH, BRANCH), hgrn_out_norm=p[88:92], loss=p[96, 0])


def _lower_bounds(logits):
    p = jax.nn.softmax(logits, axis=0)
    return jnp.cumsum(p, axis=0) - p[0:1]


def _unshard_cols(g):
    nd = g.ndim
    g = jnp.moveaxis(g, 0, nd - 2)
    return g.reshape(g.shape[:-2] + (N_DEV * g.shape[-1],))


def _shard_cols(a):
    n = a.shape[-1] // N_DEV
    return jnp.moveaxis(a.reshape(a.shape[:-1] + (N_DEV, n)), -2, 0)


def kernel(x, meta_tokens, norm_w, w_in, sb_q_norm, sb_k_norm, gdn_conv_w, gdn_a_log, gdn_dt_bias, gdn_out_norm, hgrn_lb_logits, hgrn_out_norm, w_branch, w_out, loss_target, m_meta_tokens, m_norm_w, m_w_in, m_sb_q_norm, m_sb_k_norm, m_gdn_conv_w, m_gdn_a_log, m_gdn_dt_bias, m_gdn_out_norm, m_hgrn_lb_logits, m_hgrn_out_norm, m_w_branch, m_w_out, v_meta_tokens, v_norm_w, v_w_in, v_sb_q_norm, v_sb_k_norm, v_gdn_conv_w, v_gdn_a_log, v_gdn_dt_bias, v_gdn_out_norm, v_hgrn_lb_logits, v_hgrn_out_norm, v_w_branch, v_w_out):
    g_win, g_wbr, g_wout, g_meta, g_conv = gather_two_level(
        [w_in.astype(bf16), w_branch.astype(bf16), w_out.astype(bf16), meta_tokens, gdn_conv_w], "gather_weights")
    w_full = _unshard_cols(g_win)
    w_main = jnp.concatenate([w_full[..., a:b] for a, b in W_IN_ORDER], axis=-1)
    w_small = jnp.pad(w_full[..., SMALL_OFF:SMALL_OFF + 8], ((0, 0), (0, 0), (0, N_SMALL - 8)))
    wt_main = jnp.swapaxes(w_main, 1, 2)
    wt_small = jnp.swapaxes(w_small, 1, 2)
    wbr = _unshard_cols(g_wbr)
    wbr_t = jnp.swapaxes(wbr, 2, 3)
    wout = jnp.moveaxis(g_wout, 0, 1).reshape(DEPTH, D_MODEL, D_MODEL)
    wout_t = jnp.swapaxes(wout, 1, 2)
    meta = _unshard_cols(g_meta)
    conv_w = _unshard_cols(g_conv)
    lbounds, lb_vjp = jax.vjp(_lower_bounds, hgrn_lb_logits)

    h = jnp.concatenate([jnp.zeros((PAD_FRONT, D_MODEL), f32), meta, x[0]], axis=0)
    row = lambda a: a.reshape(1, -1)
    saved = []
    for l in range(DEPTH):
        proj, small, _, xnt = inproj_fwd(h, row(norm_w[l]), w_main[l], w_small[l])
        osb = sb_fwd(proj, row(sb_q_norm[l]), row(sb_k_norm[l]))
        ypre = conv_fwd(proj, conv_w[l])
        al, dtb = _pad_lanes(row(gdn_a_log[l])), _pad_lanes(row(gdn_dt_bias[l]))
        ogd, gst = gdn_fwd(ypre, small, proj, al, dtb, row(gdn_out_norm[l]))
        ohg, hst = hgrn_fwd(proj, row(lbounds[l]), row(hgrn_out_norm[l]))
        h_next = merge_fwd(osb, proj, ogd, ohg, wbr[l], wout[l], h)
        saved.append((h, proj, small, xnt, osb, ypre, ogd, gst, ohg, hst, al, dtb))
        h = h_next

    dh, loss_row = loss_head(h, loss_target[0])

    gw_main, gw_small, gw_br, gw_out, g_conv_w = [None] * DEPTH, [None] * DEPTH, [None] * DEPTH, [None] * DEPTH, [None] * DEPTH
    g_norm, g_sbq, g_sbk, g_al, g_dt, g_gon, g_lb, g_hon = ([None] * DEPTH for _ in range(8))
    for l in reversed(range(DEPTH)):
        h_l, proj, small, xnt, osb, ypre, ogd, gst, ohg, hst, al, dtb = saved[l]
        d_osb, d_ogd, d_ohg, dproj, gw_out[l], gw_br[l] = merge_bwd(osb, proj, ogd, ohg, wbr[l], wbr_t[l], wout_t[l], dh)
        dproj, g_lb[l], g_hon[l] = hgrn_bwd(proj, row(lbounds[l]), row(hgrn_out_norm[l]), hst, d_ohg, dproj)
        d_ypre, d_small, dproj, g_al[l], g_dt[l], g_gon[l] = gdn_bwd(ypre, small, proj, al, dtb, row(gdn_out_norm[l]), gst, d_ogd, dproj)
        dproj, g_conv_w[l] = conv_bwd(proj, conv_w[l], d_ypre, dproj)
        dproj, g_sbq[l], g_sbk[l] = sb_bwd(proj, row(sb_q_norm[l]), row(sb_k_norm[l]), osb, d_osb, dproj)
        gw_main[l], gw_small[l] = inproj_bwd_w(xnt, dproj, d_small)
        dh, g_norm[l] = inproj_bwd_x(dproj, d_small, wt_main[l], wt_small[l], h_l, row(norm_w[l]), dh)

    gw_main, gw_small = jnp.stack(gw_main), jnp.stack(gw_small)
    starts = np.cumsum([0] + [b - a for a, b in W_IN_ORDER])
    pieces = sorted((a, gw_main[..., int(s):int(s) + b - a]) for (a, b), s in zip(W_IN_ORDER, starts))
    pieces.append((SMALL_OFF, gw_small[..., :8]))
    gw_in = jnp.concatenate([p for _, p in sorted(pieces, key=lambda ap: ap[0])], axis=-1)
    d_lbl = lb_vjp(jnp.concatenate(g_lb, axis=0))[0]
    cat = lambda rows: jnp.concatenate(rows, axis=0)
    pack = _pack(cat(g_norm), cat(g_sbq), cat(g_sbk), cat(g_al)[:, :N_HEADS], cat(g_dt)[:, :N_HEADS], cat(g_gon),
                 d_lbl, cat(g_hon), loss_row)
    g_meta_full = dh[PAD_FRONT:FRONT]
    r_win, r_wbr, r_wout, r_meta, r_conv, r_pack = exchange(
        [_shard_cols(gw_in).astype(bf16), _shard_cols(jnp.stack(gw_br)).astype(bf16),
         jnp.swapaxes(jnp.stack(gw_out).reshape(DEPTH, N_DEV, HEAD, D_MODEL), 0, 1).astype(bf16),
         _shard_cols(g_meta_full), _shard_cols(jnp.stack(g_conv_w)), pack],
        [True, True, True, True, True, False], "exchange_grads")

    def upd(parts, w, m, v, rows, name):
        shp = w.shape
        two = (-1, shp[-1])
        outs = adamw(parts.reshape((N_DEV,) + w.reshape(two).shape), w.reshape(two), m.reshape(two), v.reshape(two), rows, name)
        return [o.reshape(shp) for o in outs]

    res = {}
    res["w_in"] = upd(r_win, w_in, m_w_in, v_w_in, 256, "adamw_w_in")
    res["w_branch"] = upd(r_wbr, w_branch, m_w_branch, v_w_branch, 1024, "adamw_w_branch")
    res["w_out"] = upd(r_wout, w_out, m_w_out, v_w_out, 256, "adamw_w_out")
    res["meta_tokens"] = upd(r_meta, meta_tokens, m_meta_tokens, v_meta_tokens, 16, "adamw_meta")
    res["gdn_conv_w"] = upd(r_conv, gdn_conv_w, m_gdn_conv_w, v_gdn_conv_w, 16, "adamw_conv")
    zero_row = jnp.zeros((1, 128), f32)
    w_pack = _pack(norm_w, sb_q_norm, sb_k_norm, gdn_a_log, gdn_dt_bias, gdn_out_norm, hgrn_lb_logits, hgrn_out_norm, zero_row)
    m_pack = _pack(m_norm_w, m_sb_q_norm, m_sb_k_norm, m_gdn_a_log, m_gdn_dt_bias, m_gdn_out_norm, m_hgrn_lb_logits, m_hgrn_out_norm, zero_row)
    v_pack = _pack(v_norm_w, v_sb_q_norm, v_sb_k_norm, v_gdn_a_log, v_gdn_dt_bias, v_gdn_out_norm, v_hgrn_lb_logits, v_hgrn_out_norm, zero_row)
    packed = [_unpack(o) for o in adamw(r_pack, w_pack, m_pack, v_pack, PACK_ROWS, "adamw_replicated")]
    for name in ("norm_w", "sb_q_norm", "sb_k_norm", "gdn_a_log", "gdn_dt_bias", "gdn_out_norm", "hgrn_lb_logits", "hgrn_out_norm"):
        res[name] = [p[name] for p in packed]
    loss = packed[0]["loss"]
    grad_x = dh[FRONT:][None]

    order = ["meta_tokens", "norm_w", "w_in", "sb_q_norm", "sb_k_norm", "gdn_conv_w", "gdn_a_log", "gdn_dt_bias",
             "gdn_out_norm", "hgrn_lb_logits", "hgrn_out_norm", "w_branch", "w_out"]
    return (loss, grad_x, *[res[n][0] for n in order], *[res[n][1] for n in order],
            *[res[n][2] for n in order], *[res[n][3] for n in order])
```

```python
import functools

import numpy as np
import jax
import jax.numpy as jnp
from jax import lax
from jax.experimental import pallas as pl
from jax.experimental.pallas import tpu as pltpu

f32 = jnp.float32
bf16 = jnp.bfloat16

D_MODEL = 1024
BRANCH = 512
HEAD = 128
N_HEADS = 4
CHUNK = 64
SB_BLOCK = 128
N_META = 16
FRONT = 128
PAD_FRONT = 112
EPS = 1e-6
DEPTH = 4
N_DEV = 8
N_IN = 9224
N_MAIN = 9216
N_SMALL = 128
SMALL_OFF = 4096
C_SBZ, C_MIX = 0, 512
C_GZ = 3584
C_HQ, C_HF, C_HI, C_HZ = 4096, 4608, 5120, 5632
C_GQKV = 6144
C_SBQ, C_SBK, C_SBV = 7680, 8192, 8704
W_MERGE = BRANCH + 3 * D_MODEL
W_IN_ORDER = ((1536, 2048), (6152, 9224), (3584, 4096), (4104, 6152), (2048, 3584), (0, 1536))

ADAM_LR, ADAM_B1, ADAM_B2, ADAM_EPS, ADAM_WD, ADAM_STEP = 0.001, 0.9, 0.999, 1e-08, 0.01, 10

VMEM_LIMIT = 56 * 1024 * 1024
MESH = pl.DeviceIdType.MESH

NN = ((1,), (0,))
NT = ((1,), (1,))
TN = ((0,), (0,))


def _dot(a, b, dims=NN):
    return lax.dot_general(a.astype(bf16), b.astype(bf16), (dims, ((), ())), preferred_element_type=f32)


@jax.custom_vjp
def mm(a, b):
    return _dot(a, b, NN)


mm.defvjp(lambda a, b: (_dot(a, b, NN), (a, b)),
          lambda r, g: (_dot(g, r[1], NT), _dot(r[0], g, TN)))


@jax.custom_vjp
def mm_nt(a, b):
    return _dot(a, b, NT)


mm_nt.defvjp(lambda a, b: (_dot(a, b, NT), (a, b)),
             lambda r, g: (_dot(g, r[1], NN), _dot(g, r[0], TN)))


@jax.custom_vjp
def mm_tn(a, b):
    return _dot(a, b, TN)


mm_tn.defvjp(lambda a, b: (_dot(a, b, TN), (a, b)),
             lambda r, g: (_dot(r[1], g, NT), _dot(r[0], g, NN)))


def _split2(x):
    hi = x.astype(bf16)
    lo = (x - hi.astype(f32)).astype(bf16)
    return hi, lo


def _cdot(c, x, dims):
    hi, lo = _split2(x)
    return (lax.dot_general(c, hi, (dims, ((), ())), preferred_element_type=f32)
            + lax.dot_general(c, lo, (dims, ((), ())), preferred_element_type=f32))


@jax.custom_vjp
def cmm(c, x):
    return _cdot(c, x, NN)


cmm.defvjp(lambda c, x: (_cdot(c, x, NN), c),
           lambda c, g: (jnp.zeros_like(c), _cdot(c, g, TN)))


def _sigmoid(x):
    return jax.nn.sigmoid(x)


def _silu(x):
    return x * jax.nn.sigmoid(x)


def _softplus(x):
    return jnp.maximum(x, 0.0) + jnp.log(1.0 + jnp.exp(-jnp.abs(x)))


def _rms(x, w):
    return x * lax.rsqrt(jnp.mean(x * x, axis=-1, keepdims=True) + EPS) * w


def _cparams(sem=None):
    return pltpu.CompilerParams(dimension_semantics=sem, vmem_limit_bytes=VMEM_LIMIT)


TILES_FWD = (1664, 1024)
TILES_BWD_X = (832, 1024)
TILES_BWD_W = (1664, 1024)


def _row_tile(t, want, unit):
    return max(d for d in range(unit, want + 1, unit) if t % d == 0)


def inproj_fwd(h, nw, w_main, w_small):
    t = h.shape[0]
    TM_IN, TN_IN = _row_tile(t, TILES_FWD[0], 128), TILES_FWD[1]

    def body(h_ref, nw_ref, w_ref, ws_ref, proj_ref, small_ref, xn_ref, xnt_ref):
        @pl.when(pl.program_id(1) == 0)
        def _():
            xn = _rms(h_ref[...], nw_ref[...])
            xn_ref[...] = xn.astype(bf16)
            xnt_ref[...] = jnp.transpose(xn).astype(bf16)
            small_ref[...] = _dot(xn, ws_ref[...])

        proj_ref[...] = jnp.dot(xn_ref[...], w_ref[...], preferred_element_type=f32)

    return pl.pallas_call(
        body, grid=(t // TM_IN, N_MAIN // TN_IN),
        in_specs=[pl.BlockSpec((TM_IN, D_MODEL), lambda i, j: (i, 0)),
                  pl.BlockSpec((1, D_MODEL), lambda i, j: (0, 0)),
                  pl.BlockSpec((D_MODEL, TN_IN), lambda i, j: (0, j)),
                  pl.BlockSpec((D_MODEL, N_SMALL), lambda i, j: (0, 0))],
        out_specs=[pl.BlockSpec((TM_IN, TN_IN), lambda i, j: (i, j)),
                   pl.BlockSpec((TM_IN, N_SMALL), lambda i, j: (i, 0)),
                   pl.BlockSpec((TM_IN, D_MODEL), lambda i, j: (i, 0)),
                   pl.BlockSpec((D_MODEL, TM_IN), lambda i, j: (0, i))],
        out_shape=[jax.ShapeDtypeStruct((t, N_MAIN), f32), jax.ShapeDtypeStruct((t, N_SMALL), f32),
                   jax.ShapeDtypeStruct((t, D_MODEL), bf16), jax.ShapeDtypeStruct((D_MODEL, t), bf16)],
        compiler_params=_cparams(("arbitrary", "arbitrary")), name="inproj_fwd")(h, nw, w_main, w_small)


def inproj_bwd_x(dproj, dsmall, wt_main, wt_small, h, nw, dh_out):
    t = h.shape[0]
    TM_IN, TN_IN = _row_tile(t, TILES_BWD_X[0], 64), TILES_BWD_X[1]
    nk = N_MAIN // TN_IN

    def body(dp_ref, ds_ref, wt_ref, wts_ref, h_ref, nw_ref, dho_ref, dhi_ref, dnw_ref, acc):
        i, k = pl.program_id(0), pl.program_id(1)

        @pl.when(k == 0)
        def _():
            acc[...] = _dot(ds_ref[...], wts_ref[...])

        acc[...] += _dot(dp_ref[...], wt_ref[...])

        @pl.when(k == nk - 1)
        def _():
            x = h_ref[...]
            r = lax.rsqrt(jnp.mean(x * x, axis=-1, keepdims=True) + EPS)
            xh = x * r
            dxn = acc[...]
            dxh = dxn * nw_ref[...]
            dhi_ref[...] = dho_ref[...] + r * (dxh - xh * jnp.mean(dxh * xh, axis=-1, keepdims=True))
            part = jnp.sum(dxn * xh, axis=0, keepdims=True)

            @pl.when(i == 0)
            def _():
                dnw_ref[...] = part

            @pl.when(i > 0)
            def _():
                dnw_ref[...] += part

    return pl.pallas_call(
        body, grid=(t // TM_IN, nk),
        in_specs=[pl.BlockSpec((TM_IN, TN_IN), lambda i, k: (i, k)),
                  pl.BlockSpec((TM_IN, N_SMALL), lambda i, k: (i, 0)),
                  pl.BlockSpec((TN_IN, D_MODEL), lambda i, k: (k, 0)),
                  pl.BlockSpec((N_SMALL, D_MODEL), lambda i, k: (0, 0)),
                  pl.BlockSpec((TM_IN, D_MODEL), lambda i, k: (i, 0)),
                  pl.BlockSpec((1, D_MODEL), lambda i, k: (0, 0)),
                  pl.BlockSpec((TM_IN, D_MODEL), lambda i, k: (i, 0))],
        out_specs=[pl.BlockSpec((TM_IN, D_MODEL), lambda i, k: (i, 0)),
                   pl.BlockSpec((1, D_MODEL), lambda i, k: (0, 0))],
        out_shape=[jax.ShapeDtypeStruct((t, D_MODEL), f32), jax.ShapeDtypeStruct((1, D_MODEL), f32)],
        scratch_shapes=[pltpu.VMEM((TM_IN, D_MODEL), f32)],
        compiler_params=_cparams(("arbitrary", "arbitrary")), name="inproj_bwd_x")(
            dproj, dsmall, wt_main, wt_small, h, nw, dh_out)


def inproj_bwd_w(xnt, dproj, dsmall):
    t = xnt.shape[1]
    TM_IN, TN_IN = _row_tile(t, TILES_BWD_W[0], 128), TILES_BWD_W[1]
    nt = t // TM_IN

    def body(xnt_ref, dp_ref, ds_ref, dw_ref, dws_ref):
        n, s = pl.program_id(0), pl.program_id(1)
        part = _dot(xnt_ref[...], dp_ref[...])

        @pl.when(s == 0)
        def _():
            dw_ref[...] = part

        @pl.when(s > 0)
        def _():
            dw_ref[...] += part

        @pl.when(n == 0)
        def _():
            ps = _dot(xnt_ref[...], ds_ref[...])

            @pl.when(s == 0)
            def _():
                dws_ref[...] = ps

            @pl.when(s > 0)
            def _():
                dws_ref[...] += ps

    return pl.pallas_call(
        body, grid=(N_MAIN // TN_IN, nt),
        in_specs=[pl.BlockSpec((D_MODEL, TM_IN), lambda n, s: (0, s)),
                  pl.BlockSpec((TM_IN, TN_IN), lambda n, s: (s, n)),
                  pl.BlockSpec((TM_IN, N_SMALL), lambda n, s: (s, 0))],
        out_specs=[pl.BlockSpec((D_MODEL, TN_IN), lambda n, s: (0, n)),
                   pl.BlockSpec((D_MODEL, N_SMALL), lambda n, s: (0, 0))],
        out_shape=[jax.ShapeDtypeStruct((D_MODEL, N_MAIN), f32), jax.ShapeDtypeStruct((D_MODEL, N_SMALL), f32)],
        compiler_params=_cparams(("arbitrary", "arbitrary")), name="inproj_bwd_w")(xnt, dproj, dsmall)


SB_SCALE = HEAD ** -0.5


SB_SUB = 3
SB_KS = SB_SUB * SB_BLOCK


SB_PADR = SB_KS - SB_BLOCK


def _sb_padded(t):
    return t + SB_PADR


def _sb_rows(i, d):
    start = (i + 1) * SB_BLOCK - (d + 1) * SB_KS
    return start, pl.ds(pl.multiple_of(start + SB_PADR, SB_BLOCK), SB_KS)


def _sb_prep(k_ref, v_ref, kw_ref, kn_scr, vb_scr, nb):
    def prep(b, c):
        rows = pl.ds(pl.multiple_of(b * SB_BLOCK, SB_BLOCK), SB_BLOCK)
        pad_rows = pl.ds(pl.multiple_of(SB_PADR + b * SB_BLOCK, SB_BLOCK), SB_BLOCK)
        kn_scr[pad_rows, :] = _rms(k_ref[rows, :], kw_ref[...]).astype(bf16)
        vb_scr[pad_rows, :] = v_ref[rows, :].astype(bf16)
        return c

    lax.fori_loop(0, nb, prep, 0)
    kn_scr[:SB_PADR, :] = jnp.zeros((SB_PADR, HEAD), bf16)
    vb_scr[:SB_PADR, :] = jnp.zeros((SB_PADR, HEAD), bf16)


def _tri_ext(cmp):
    r = lax.broadcasted_iota(jnp.int32, (SB_BLOCK, 2 * SB_BLOCK), 0)
    c = lax.broadcasted_iota(jnp.int32, (SB_BLOCK, 2 * SB_BLOCK), 1)
    return jnp.where((c >= SB_BLOCK) | cmp(r, c), 1.0, 0.0).astype(bf16)


def _sb_suffix(x, carry, tri_ext):
    hi, lo = _split2(x)
    parts = [p[:, c * SB_BLOCK:(c + 1) * SB_BLOCK] for p in (hi, lo) for c in range(SB_SUB)]
    w = jnp.dot(jnp.concatenate(parts, axis=0), tri_ext, preferred_element_type=f32)
    outs = [None] * SB_SUB
    for c in reversed(range(SB_SUB)):
        blk = w[c * SB_BLOCK:(c + 1) * SB_BLOCK] + w[(SB_SUB + c) * SB_BLOCK:(SB_SUB + c + 1) * SB_BLOCK]
        outs[c] = carry + blk[:, :SB_BLOCK]
        carry = carry + blk[:, SB_BLOCK:]
    return jnp.concatenate(outs, axis=1), carry


def _sb_scores(qn, kt, i, start, masked):
    z = lax.dot_general(qn, kt, (NT, ((), ())), preferred_element_type=f32) * SB_SCALE
    lsz = jnp.minimum(z, 0.0) - jnp.log(1.0 + jnp.exp(-jnp.abs(z)))
    lk = lsz - z
    mask = None
    if masked:
        t_idx = i * SB_BLOCK + lax.broadcasted_iota(jnp.int32, (SB_BLOCK, SB_KS), 0)
        s_idx = start + lax.broadcasted_iota(jnp.int32, (SB_BLOCK, SB_KS), 1)
        mask = (s_idx < t_idx) & (s_idx >= PAD_FRONT)
        lk = jnp.where(mask, lk, 0.0)
    return mask, lsz, lk


SB_DEAD = -104.0


def _sb_walk(i, tile, carry):
    n = (i + SB_SUB) // SB_SUB
    live = lambda c: jnp.max(c[1]) > SB_DEAD
    carry = tile(0, carry, True)
    _, carry = lax.while_loop(lambda st: (st[0] <= n - 2) & live(st[1]),
                              lambda st: (st[0] + 1, tile(st[0], st[1], False)), (1, carry))
    return lax.cond((n >= 2) & live(carry), lambda c: tile(n - 1, c, True), lambda c: c, carry)


def sb_fwd(proj, qw, kw):
    t = proj.shape[0]
    nb = t // SB_BLOCK

    def body(q_ref, k_ref, v_ref, qw_ref, kw_ref, o_ref, kn_scr, vb_scr):
        i = pl.program_id(1)

        @pl.when(i == 0)
        def _():
            _sb_prep(k_ref, v_ref, kw_ref, kn_scr, vb_scr, nb)

        qn = _rms(q_ref[...], qw_ref[...]).astype(bf16)
        u_ex = _tri_ext(lambda r, c: r > c)

        def tile(jb, carry, masked):
            acc, r_carry = carry
            start, rows = _sb_rows(i, jb)
            mask, lsz, lk = _sb_scores(qn, kn_scr[rows, :], i, start, masked)
            passed, r_carry = _sb_suffix(lk, r_carry, u_ex)
            a = jnp.exp(lsz + passed)
            if masked:
                a = jnp.where(mask, a, 0.0)
            a_hi, a_lo = _split2(a)
            both = jnp.dot(jnp.concatenate([a_hi, a_lo], axis=0), vb_scr[rows, :], preferred_element_type=f32)
            return acc + (both[:SB_BLOCK] + both[SB_BLOCK:]), r_carry

        zeros = jnp.zeros((SB_BLOCK, HEAD), f32)
        acc, _ = _sb_walk(i, tile, (zeros, zeros))
        o_ref[...] = acc

    qb, cb, vb = C_SBQ // HEAD, C_SBK // HEAD, C_SBV // HEAD
    return pl.pallas_call(
        body, grid=(N_HEADS, nb),
        in_specs=[pl.BlockSpec((SB_BLOCK, HEAD), lambda h, i: (i, qb + h)),
                  pl.BlockSpec((t, HEAD), lambda h, i: (0, cb + h)),
                  pl.BlockSpec((t, HEAD), lambda h, i: (0, vb + h)),
                  pl.BlockSpec((1, HEAD), lambda h, i: (0, 0)),
                  pl.BlockSpec((1, HEAD), lambda h, i: (0, 0))],
        out_specs=pl.BlockSpec((SB_BLOCK, HEAD), lambda h, i: (i, h)),
        out_shape=jax.ShapeDtypeStruct((t, BRANCH), f32),
        scratch_shapes=[pltpu.VMEM((_sb_padded(t), HEAD), bf16), pltpu.VMEM((_sb_padded(t), HEAD), bf16)],
        compiler_params=_cparams(("arbitrary", "arbitrary")), name="sb_fwd")(proj, proj, proj, qw, kw)


def sb_bwd(proj, qw, kw, o, do, dproj):
    t = proj.shape[0]
    nb = t // SB_BLOCK

    def body(q_ref, k_ref, v_ref, qw_ref, kw_ref, o_ref, do_ref, _, dp_ref, dqw_ref, dkw_ref,
             kn_scr, vb_scr, dk_acc, dv_acc, dq_stage, dq_sems, kv_sems):
        h, i = pl.program_id(0), pl.program_id(1)
        step = h * nb + i
        slot = step % 2

        def dq_copy(sl, head):
            return pltpu.make_async_copy(
                dq_stage.at[sl], dp_ref.at[pl.ds(pl.multiple_of(i * SB_BLOCK, SB_BLOCK), SB_BLOCK),
                                           pl.ds(C_SBQ + head * HEAD, HEAD)], dq_sems.at[sl])

        @pl.when(i == 0)
        def _():
            _sb_prep(k_ref, v_ref, kw_ref, kn_scr, vb_scr, nb)
            dk_acc[...] = jnp.zeros_like(dk_acc)
            dv_acc[...] = jnp.zeros_like(dv_acc)

        @pl.when((i == 0) & (h == 0))
        def _():
            dqw_ref[...] = jnp.zeros_like(dqw_ref)
            dkw_ref[...] = jnp.zeros_like(dkw_ref)

        q = q_ref[...]
        rq = lax.rsqrt(jnp.mean(q * q, axis=-1, keepdims=True) + EPS)
        qh = q * rq
        qn = (qh * qw_ref[...]).astype(bf16)
        do_f = do_ref[...]
        dob = do_f.astype(bf16)
        d_row = jnp.sum(dob.astype(f32) * o_ref[...], axis=-1, keepdims=True)
        u_ex = _tri_ext(lambda r, c: r > c)
        u_in = _tri_ext(lambda r, c: r >= c)

        def tile(jb, carry, masked):
            dq, r_carry, f_carry = carry
            start, rows = _sb_rows(i, jb)
            kt = kn_scr[rows, :]
            vt = vb_scr[rows, :]
            mask, lsz, lk = _sb_scores(qn, kt, i, start, masked)
            passed, r_carry = _sb_suffix(lk, r_carry, u_ex)
            a = jnp.exp(lsz + passed)
            if masked:
                a = jnp.where(mask, a, 0.0)
            da = lax.dot_general(dob, vt, (NT, ((), ())), preferred_element_type=f32)
            e = a * da
            e_suf, f_carry = _sb_suffix(e, f_carry, u_in)
            sg = jnp.exp(lsz)
            dz = (e * (1.0 - sg) - (d_row - e_suf) * sg) * SB_SCALE
            if masked:
                dz = jnp.where(mask, dz, 0.0)
            dzb = dz.astype(bf16)
            dq = dq + jnp.dot(dzb, kt, preferred_element_type=f32)
            dk_acc[rows, :] += lax.dot_general(dzb, qn, (TN, ((), ())), preferred_element_type=f32)
            dv_acc[rows, :] += lax.dot_general(a.astype(bf16), dob, (TN, ((), ())), preferred_element_type=f32)
            return dq, r_carry, f_carry

        zeros = jnp.zeros((SB_BLOCK, HEAD), f32)
        dqn, _, _ = _sb_walk(i, tile, (zeros, zeros, zeros))
        gq = dqn * qw_ref[...]
        dqw_ref[...] += jnp.sum(dqn * qh, axis=0, keepdims=True)

        @pl.when(step >= 2)
        def _():
            dq_copy(slot, 0).wait()

        dq_stage[slot] = rq * (gq - qh * jnp.mean(gq * qh, axis=-1, keepdims=True))
        for head in range(N_HEADS):
            @pl.when(h == head)
            def _(head=head):
                dq_copy(slot, head).start()

        @pl.when(i == nb - 1)
        def _():
            def fin(b, c):
                rows = pl.ds(pl.multiple_of(b * SB_BLOCK, SB_BLOCK), SB_BLOCK)
                pad_rows = pl.ds(pl.multiple_of(SB_PADR + b * SB_BLOCK, SB_BLOCK), SB_BLOCK)
                kk = k_ref[rows, :]
                rk = lax.rsqrt(jnp.mean(kk * kk, axis=-1, keepdims=True) + EPS)
                kh = kk * rk
                dkn = dk_acc[pad_rows, :]
                gk = dkn * kw_ref[...]
                dk_acc[pad_rows, :] = rk * (gk - kh * jnp.mean(gk * kh, axis=-1, keepdims=True))
                dkw_ref[...] += jnp.sum(dkn * kh, axis=0, keepdims=True)
                return c

            lax.fori_loop(0, nb, fin, 0)
            for head in range(N_HEADS):
                @pl.when(h == head)
                def _(head=head):
                    outs = [pltpu.make_async_copy(acc.at[pl.ds(SB_PADR, t)], dp_ref.at[:, pl.ds(c0 + head * HEAD, HEAD)],
                                                  kv_sems.at[n])
                            for n, (acc, c0) in enumerate(((dk_acc, C_SBK), (dv_acc, C_SBV)))]
                    for cp in outs:
                        cp.start()
                    for cp in outs:
                        cp.wait()

        @pl.when(step == N_HEADS * nb - 1)
        def _():
            dq_copy(1 - slot, 0).wait()
            dq_copy(slot, 0).wait()

    qb, cb, vb = C_SBQ // HEAD, C_SBK // HEAD, C_SBV // HEAD
    blk = pl.BlockSpec((SB_BLOCK, HEAD), lambda h, i: (i, h))
    wsp = pl.BlockSpec((1, HEAD), lambda h, i: (0, 0))
    any_spec = pl.BlockSpec(memory_space=pl.ANY)
    return pl.pallas_call(
        body, grid=(N_HEADS, nb),
        in_specs=[pl.BlockSpec((SB_BLOCK, HEAD), lambda h, i: (i, qb + h)),
                  pl.BlockSpec((t, HEAD), lambda h, i: (0, cb + h)),
                  pl.BlockSpec((t, HEAD), lambda h, i: (0, vb + h)), wsp, wsp, blk, blk, any_spec],
        out_specs=[any_spec, wsp, wsp],
        out_shape=[jax.ShapeDtypeStruct((t, N_MAIN), f32)] + [jax.ShapeDtypeStruct((1, HEAD), f32)] * 2,
        scratch_shapes=[pltpu.VMEM((_sb_padded(t), HEAD), bf16), pltpu.VMEM((_sb_padded(t), HEAD), bf16),
                        pltpu.VMEM((_sb_padded(t), HEAD), f32), pltpu.VMEM((_sb_padded(t), HEAD), f32),
                        pltpu.VMEM((2, SB_BLOCK, HEAD), f32), pltpu.SemaphoreType.DMA((2,)), pltpu.SemaphoreType.DMA((2,))],
        input_output_aliases={7: 0},
        compiler_params=_cparams(("arbitrary", "arbitrary")), name="sb_bwd")(proj, proj, proj, qw, kw, o, do, dproj)


TM_CONV = 640
CONV_W = 4
GQKV = 3 * BRANCH


def conv_fwd(proj, cw):
    t = proj.shape[0]
    halo_blocks = TM_CONV // 8

    def body(x0_ref, x1_ref, x2_ref, p0_ref, p1_ref, p2_ref, cw_ref, y_ref):
        i = pl.program_id(0)
        for s, (x_ref, p_ref) in enumerate(((x0_ref, p0_ref), (x1_ref, p1_ref), (x2_ref, p2_ref))):
            prev = jnp.where(i > 0, p_ref[...], 0.0)
            xx = jnp.concatenate([prev, x_ref[...]], axis=0)
            cols = slice(s * BRANCH, (s + 1) * BRANCH)
            y = xx[8:] * cw_ref[CONV_W - 1:CONV_W, cols]
            for k in range(CONV_W - 1):
                y = y + pltpu.roll(xx, CONV_W - 1 - k, 0)[8:] * cw_ref[k:k + 1, cols]
            y_ref[:, cols] = y

    c0 = C_GQKV // BRANCH
    xs = [pl.BlockSpec((TM_CONV, BRANCH), functools.partial(lambda i, s: (i, c0 + s), s=s)) for s in range(3)]
    ps = [pl.BlockSpec((8, BRANCH), functools.partial(lambda i, s: (jnp.maximum(i * halo_blocks - 1, 0), c0 + s), s=s))
          for s in range(3)]
    return pl.pallas_call(
        body, grid=(t // TM_CONV,),
        in_specs=xs + ps + [pl.BlockSpec((CONV_W, GQKV), lambda i: (0, 0))],
        out_specs=pl.BlockSpec((TM_CONV, GQKV), lambda i: (i, 0)),
        out_shape=jax.ShapeDtypeStruct((t, GQKV), f32),
        compiler_params=_cparams(("arbitrary",)), name="conv_fwd")(proj, proj, proj, proj, proj, proj, cw)


def conv_bwd(proj, cw, dy, dproj):
    t = proj.shape[0]
    nt = t // TM_CONV
    halo_blocks = TM_CONV // 8

    def body(x0_ref, x1_ref, x2_ref, p0_ref, p1_ref, p2_ref, cw_ref, dy_ref, dyn_ref, _, dx_ref, dw_ref):
        i = pl.program_id(0)

        @pl.when(i == 0)
        def _():
            dw_ref[...] = jnp.zeros_like(dw_ref)

        nxt = jnp.where(i < nt - 1, dyn_ref[...], 0.0)
        dyy = jnp.concatenate([dy_ref[...], nxt], axis=0)
        n_rows = TM_CONV + 8
        dx = dyy[:TM_CONV] * cw_ref[CONV_W - 1:CONV_W, :]
        for k in range(CONV_W - 1):
            sh = CONV_W - 1 - k
            dx = dx + pltpu.roll(dyy, n_rows - sh, 0)[:TM_CONV] * cw_ref[k:k + 1, :]
        dx_ref[...] = dx
        dy_c = dy_ref[...]
        for s, (x_ref, p_ref) in enumerate(((x0_ref, p0_ref), (x1_ref, p1_ref), (x2_ref, p2_ref))):
            prev = jnp.where(i > 0, p_ref[...], 0.0)
            xx = jnp.concatenate([prev, x_ref[...]], axis=0)
            cols = slice(s * BRANCH, (s + 1) * BRANCH)
            for k in range(CONV_W):
                sh = CONV_W - 1 - k
                xs = xx[8:] if sh == 0 else pltpu.roll(xx, sh, 0)[8:]
                dw_ref[k:k + 1, cols] += jnp.sum(xs * dy_c[:, cols], axis=0, keepdims=True)

    c0 = C_GQKV // BRANCH
    xs = [pl.BlockSpec((TM_CONV, BRANCH), functools.partial(lambda i, s: (i, c0 + s), s=s)) for s in range(3)]
    ps = [pl.BlockSpec((8, BRANCH), functools.partial(lambda i, s: (jnp.maximum(i * halo_blocks - 1, 0), c0 + s), s=s))
          for s in range(3)]
    return pl.pallas_call(
        body, grid=(nt,),
        in_specs=xs + ps + [pl.BlockSpec((CONV_W, GQKV), lambda i: (0, 0)),
                            pl.BlockSpec((TM_CONV, GQKV), lambda i: (i, 0)),
                            pl.BlockSpec((8, GQKV), lambda i: (jnp.minimum((i + 1) * halo_blocks, nt * halo_blocks - 1), 0)),
                            pl.BlockSpec(memory_space=pl.ANY)],
        out_specs=[pl.BlockSpec((TM_CONV, GQKV), lambda i: (i, C_GQKV // GQKV)), pl.BlockSpec((CONV_W, GQKV), lambda i: (0, 0))],
        out_shape=[jax.ShapeDtypeStruct((t, N_MAIN), f32), jax.ShapeDtypeStruct((CONV_W, GQKV), f32)],
        input_output_aliases={9: 0},
        compiler_params=_cparams(("arbitrary",)), name="conv_bwd")(proj, proj, proj, proj, proj, proj, cw, dy, dy, dproj)


def _iota2(n, m, d):
    return lax.broadcasted_iota(jnp.int32, (n, m), d)


def _lane_pick(row_or_mat, idx):
    lanes = lax.broadcasted_iota(jnp.int32, row_or_mat.shape, row_or_mat.ndim - 1)
    return jnp.sum(jnp.where(lanes == idx, row_or_mat, 0.0), axis=-1, keepdims=True)


def _cumsum_consts():
    i = np.arange(CHUNK)
    incl = i[None, :] <= i[:, None]
    suf = i[None, :] > i[:, None]
    return np.concatenate([incl, suf], 0).astype(np.float32)


HG_LEVELS = (64, 32, 16, 8, 4, 2)


def _hgrn_consts():
    i = np.arange(CHUNK)
    rows = [i[None, :] <= i[:, None], i[None, :] > i[:, None]]
    for b in HG_LEVELS:
        ref = (i // b) * b + b // 2 - 1
        second = (i % b) >= b // 2
        rows.append((i[None, :] > ref[:, None]) & (i[None, :] <= i[:, None]) & second[:, None])
        rows.append((i[None, :] > i[:, None]) & (i[None, :] <= ref[:, None]) & (~second)[:, None])
    return np.concatenate(rows, 0).astype(np.float32)


N_SQUARINGS = 5


def _solve_chain(ms, rhss):
    xs = [r - mm(m, r) for m, r in zip(ms, rhss)]
    powers = [list(ms)]
    for _ in range(N_SQUARINGS):
        powers.append([mm(p, p) for p in powers[-1]])
        xs = [x + mm(p, x) for p, x in zip(powers[-1], xs)]
    return tuple(xs), powers


@jax.custom_vjp
def unit_lower_solve_multi(ms, rhss):
    return _solve_chain(ms, rhss)[0]


def _solve_fwd(ms, rhss):
    xs, powers = _solve_chain(ms, rhss)
    return xs, (powers, xs)


def _solve_bwd(res, gs):
    powers, xs = res
    ys = [g - mm_tn(p, g) for p, g in zip(powers[0], gs)]
    for ps in powers[1:]:
        ys = [y + mm_tn(p, y) for p, y in zip(ps, ys)]
    return tuple(-mm_nt(y, x) for y, x in zip(ys, xs)), tuple(ys)


unit_lower_solve_multi.defvjp(_solve_fwd, _solve_bwd)

HC = N_HEADS * CHUNK
BATCH0 = ((0,), (0,))


def _bdot(a, b, contract):
    return lax.dot_general(a.astype(bf16), b.astype(bf16), (contract, BATCH0), preferred_element_type=f32)


B_NN = ((2,), (1,))
B_NT = ((2,), (2,))
B_TN = ((1,), (1,))


@jax.custom_vjp
def bmm(a, b):
    return _bdot(a, b, B_NN)


bmm.defvjp(lambda a, b: (_bdot(a, b, B_NN), (a, b)),
           lambda r, g: (_bdot(g, r[1], B_NT), _bdot(r[0], g, B_TN)))


@jax.custom_vjp
def bmm_nt(a, b):
    return _bdot(a, b, B_NT)


bmm_nt.defvjp(lambda a, b: (_bdot(a, b, B_NT), (a, b)),
              lambda r, g: (_bdot(g, r[1], B_NN), _bdot(g, r[0], B_TN)))


@jax.custom_vjp
def bmm_tn(a, b):
    return _bdot(a, b, B_TN)


bmm_tn.defvjp(lambda a, b: (_bdot(a, b, B_TN), (a, b)),
              lambda r, g: (_bdot(r[1], g, B_NT), _bdot(r[0], g, B_NN)))


def _stack_heads(x):
    return jnp.concatenate([x[:, h * HEAD:(h + 1) * HEAD] for h in range(N_HEADS)], axis=0)


def _unstack_heads(x):
    return jnp.concatenate([x[h * CHUNK:(h + 1) * CHUNK] for h in range(N_HEADS)], axis=1)


REC_CHUNKS = 5
REC_ROWS = REC_CHUNKS * CHUNK


def _interleave(gens):
    n = len(gens)
    sends, results, done = [None] * n, [None] * n, [False] * n
    while not all(done):
        asks = []
        for j in range(n):
            if done[j]:
                continue
            try:
                ask = gens[j].send(sends[j])
                if ask is not None:
                    asks.append((j, ask))
            except StopIteration as stop:
                results[j], done[j] = stop.value, True
            sends[j] = None
        if asks:
            xs = unit_lower_solve_multi(tuple(a[1] for _, a in asks), tuple(a[2] for _, a in asks))
            for (j, _), x in zip(asks, xs):
                sends[j] = x
    return results


def _chunks_of(a):
    return [a[j * CHUNK:(j + 1) * CHUNK] for j in range(REC_CHUNKS)]


def _gdn_step(state, ypre, small, gz, a_log, dt_b, on_w, c2, first_chunk):
    masks = _gdn_masks()
    gens = [_gdn_intra(y, s, a_log, dt_b, c2, _vmask(first_chunk + j), masks)
            for j, (y, s) in enumerate(zip(_chunks_of(ypre), _chunks_of(small)))]
    outs = []
    for intra, z in zip(_interleave(gens), _chunks_of(gz)):
        state, o = _gdn_inter(state, intra, z, on_w)
        outs.append(o)
    return state, jnp.concatenate(outs, axis=0)


def _gdn_inter(state, intra, gz, on_w):
    u, w, aqk, q_dec, k_dec, g_last = intra
    per_head = lambda a: a.reshape(N_HEADS, CHUNK, HEAD)
    v_new = u - bmm(per_head(w), state).reshape(HC, HEAD)
    o = bmm(per_head(q_dec), state).reshape(HC, HEAD) + mm(aqk, v_new)
    new_state = state * g_last + bmm_tn(per_head(k_dec), per_head(v_new))
    return new_state, _unstack_heads(_rms(o, on_w)) * _silu(gz)


def _gdn_masks():
    r = _iota2(HC, HC, 0)
    c = _iota2(HC, HC, 1)
    same_head = (r >> (CHUNK.bit_length() - 1)) == (c >> (CHUNK.bit_length() - 1))
    return same_head & (r >= c), same_head & (r > c)


def _gdn_intra(ypre, small, a_log, dt_b, c2, vm, masks):
    causal, strict = masks
    q = _silu(_stack_heads(ypre[:, :BRANCH]))
    k = _silu(_stack_heads(ypre[:, BRANCH:2 * BRANCH]))
    v = _silu(_stack_heads(ypre[:, 2 * BRANCH:]))
    q = q * lax.rsqrt(jnp.sum(q * q, axis=-1, keepdims=True) + EPS) * (HEAD ** -0.5)
    k = k * lax.rsqrt(jnp.sum(k * k, axis=-1, keepdims=True) + EPS)
    col = lambda f: jnp.concatenate([f(h) for h in range(N_HEADS)], axis=0)
    chunk_col = lambda x: jnp.broadcast_to(x, (CHUNK, 1))
    beta = _sigmoid(col(lambda h: _lane_pick(small, h))) * col(lambda h: vm)
    g = (-jnp.exp(col(lambda h: chunk_col(_lane_pick(a_log, h))))
         * _softplus(col(lambda h: _lane_pick(small, N_HEADS + h)) + col(lambda h: chunk_col(_lane_pick(dt_b, h)))))
    g_l = _unstack_heads(jnp.broadcast_to(g, (HC, HEAD)))
    e2 = cmm(c2, g_l)
    yield
    gc = _stack_heads(e2[:CHUNK])
    gsuf = _stack_heads(e2[CHUNK:])
    g_row = jnp.broadcast_to(jnp.transpose(gc)[0:1, :], (HC, HC))
    g_col = jnp.concatenate([gc, gc], axis=1)
    dec = jnp.where(causal, jnp.exp(jnp.minimum(g_col - g_row, 0.0)), 0.0)
    kb = k * beta
    kk = mm_nt(kb, k)
    qk = mm_nt(q, k)
    yield
    m = jnp.where(strict, kk * dec, 0.0)
    x = yield ("solve", m, jnp.concatenate([v * beta, kb * jnp.exp(gc)], axis=1))
    aqk = jnp.where(causal, qk * dec, 0.0)
    tot = jnp.sum(g_l, axis=0, keepdims=True)
    g_last = jnp.exp(jnp.stack([tot[:, h * HEAD:(h + 1) * HEAD] for h in range(N_HEADS)], axis=0))
    return x[:, :HEAD], x[:, HEAD:], aqk, q * jnp.exp(gc), k * jnp.exp(gsuf), g_last


def _hgrn_step(state, hq, hf, hi, hz, lb, on_w, cm, first_chunk):
    masks = _hgrn_masks()
    gens = [_hgrn_intra(q, f, i, lb, cm, _vmask(first_chunk + j), masks)
            for j, (q, f, i) in enumerate(zip(_chunks_of(hq), _chunks_of(hf), _chunks_of(hi)))]
    outs = []
    for (q_dec, k_dec, v, o_intra, g_end), z in zip(_interleave(gens), _chunks_of(hz)):
        per_head = lambda a: a.reshape(N_HEADS, CHUNK, HEAD)
        o = bmm_nt(per_head(q_dec), state).reshape(HC, HEAD) + o_intra
        state = state * g_end + bmm_tn(per_head(v), per_head(k_dec))
        outs.append(_unstack_heads(_rms(o, on_w)) * _silu(z))
    return state, jnp.concatenate(outs, axis=0)


def _hgrn_masks():
    r = _iota2(HC, HC, 0)
    c = _iota2(HC, HC, 1)
    pairs = []
    for b in HG_LEVELS:
        sh = b.bit_length() - 1
        pairs.append(((r >> sh) == (c >> sh)) & ((r & (b - 1)) >= b // 2) & ((c & (b - 1)) < b // 2))
    return r == c, pairs


def _hgrn_intra(hq, hf, hi, lb, cm, vm, masks):
    diag, pairs = masks
    forget = lb + (1.0 - lb) * _sigmoid(hf)
    g_l = jnp.log(forget)
    e = cmm(cm, g_l)
    q = _stack_heads(_silu(hq))
    k = _stack_heads((1.0 - lb) * _sigmoid(-hf))
    v = _stack_heads(hi * vm)
    yield
    sect = lambda n: _stack_heads(e[n * CHUNK:(n + 1) * CHUNK])
    gc, gsuf = sect(0), sect(1)
    a = jnp.where(diag, jnp.sum(q * k, axis=-1, keepdims=True), 0.0)
    for li, pair in enumerate(pairs):
        a = a + jnp.where(pair, mm_nt(q * jnp.exp(sect(2 + 2 * li)), k * jnp.exp(sect(3 + 2 * li))), 0.0)
    yield
    o_intra = mm(a, v)
    tot = jnp.sum(g_l, axis=0, keepdims=True)
    g_end = jnp.exp(jnp.stack([tot[:, h * HEAD:(h + 1) * HEAD] for h in range(N_HEADS)], axis=0))
    return q * jnp.exp(gc), k * jnp.exp(gsuf), v, o_intra, g_end


def _vmask(chunk_idx):
    rows = chunk_idx * CHUNK + lax.broadcasted_iota(jnp.int32, (CHUNK, 1), 0)
    return jnp.where(rows >= PAD_FRONT, 1.0, 0.0)


def _row(n):
    return pl.BlockSpec((1, n), lambda i: (0, 0))


def gdn_fwd(ypre, small, proj, a_log, dt_b, on_w):
    t = ypre.shape[0]
    nc = t // REC_ROWS
    c2 = jnp.asarray(_cumsum_consts(), bf16)

    def body(y_ref, s_ref, z_ref, al_ref, dt_ref, on_ref, c2_ref, o_ref, st_ref, state):
        i = pl.program_id(0)

        @pl.when(i == 0)
        def _():
            state[...] = jnp.zeros_like(state)

        s_in = state[...]
        st_ref[0] = s_in
        s_new, out = _gdn_step(s_in, y_ref[...], s_ref[...], z_ref[...], al_ref[...], dt_ref[...], on_ref[...],
                               c2_ref[...], i * REC_CHUNKS)
        state[...] = s_new
        o_ref[...] = out

    return pl.pallas_call(
        body, grid=(nc,),
        in_specs=[pl.BlockSpec((REC_ROWS,GQKV), lambda i: (i, 0)), pl.BlockSpec((REC_ROWS,N_SMALL), lambda i: (i, 0)),
                  pl.BlockSpec((REC_ROWS,BRANCH), lambda i: (i, C_GZ // BRANCH)), _row(128), _row(128), _row(128),
                  pl.BlockSpec((2 * CHUNK, CHUNK), lambda i: (0, 0))],
        out_specs=[pl.BlockSpec((REC_ROWS,BRANCH), lambda i: (i, 0)),
                   pl.BlockSpec((1, N_HEADS, HEAD, HEAD), lambda i: (i, 0, 0, 0))],
        out_shape=[jax.ShapeDtypeStruct((t, BRANCH), f32), jax.ShapeDtypeStruct((nc, N_HEADS, HEAD, HEAD), f32)],
        scratch_shapes=[pltpu.VMEM((N_HEADS, HEAD, HEAD), f32)],
        compiler_params=_cparams(("arbitrary",)), name="gdn_fwd")(ypre, small, proj, a_log, dt_b, on_w, c2)


def gdn_bwd(ypre, small, proj, a_log, dt_b, on_w, states, d_out, dproj):
    t = ypre.shape[0]
    nc = t // REC_ROWS
    c2 = jnp.asarray(_cumsum_consts(), bf16)

    def body(y_ref, s_ref, z_ref, al_ref, dt_ref, on_ref, c2_ref, st_ref, do_ref, _,
             dy_ref, ds_ref, dz_ref, dal_ref, ddt_ref, don_ref, dstate):
        i = pl.program_id(0)

        @pl.when(i == 0)
        def _():
            dstate[...] = jnp.zeros_like(dstate)
            dal_ref[...] = jnp.zeros_like(dal_ref)
            ddt_ref[...] = jnp.zeros_like(ddt_ref)
            don_ref[...] = jnp.zeros_like(don_ref)

        c2v = c2_ref[...]
        fn = lambda s, y, sm, z, al, dt, on: _gdn_step(s, y, sm, z, al, dt, on, c2v, (nc - 1 - i) * REC_CHUNKS)
        _, vjp = jax.vjp(fn, st_ref[0], y_ref[...], s_ref[...], z_ref[...], al_ref[...], dt_ref[...], on_ref[...])
        d_s, d_y, d_sm, d_z, d_al, d_dt, d_on = vjp((dstate[...], do_ref[...]))
        dstate[...] = d_s
        dy_ref[...] = d_y
        ds_ref[...] = d_sm
        dz_ref[...] = d_z
        dal_ref[...] += d_al
        ddt_ref[...] += d_dt
        don_ref[...] += d_on

    rev = lambda i: (nc - 1 - i, 0)
    return pl.pallas_call(
        body, grid=(nc,),
        in_specs=[pl.BlockSpec((REC_ROWS,GQKV), rev), pl.BlockSpec((REC_ROWS,N_SMALL), rev),
                  pl.BlockSpec((REC_ROWS,BRANCH), lambda i: (nc - 1 - i, C_GZ // BRANCH)), _row(128), _row(128), _row(128),
                  pl.BlockSpec((2 * CHUNK, CHUNK), lambda i: (0, 0)),
                  pl.BlockSpec((1, N_HEADS, HEAD, HEAD), lambda i: (nc - 1 - i, 0, 0, 0)),
                  pl.BlockSpec((REC_ROWS,BRANCH), rev), pl.BlockSpec(memory_space=pl.ANY)],
        out_specs=[pl.BlockSpec((REC_ROWS,GQKV), rev), pl.BlockSpec((REC_ROWS,N_SMALL), rev),
                   pl.BlockSpec((REC_ROWS,BRANCH), lambda i: (nc - 1 - i, C_GZ // BRANCH)), _row(128), _row(128), _row(128)],
        out_shape=[jax.ShapeDtypeStruct((t, GQKV), f32), jax.ShapeDtypeStruct((t, N_SMALL), f32),
                   jax.ShapeDtypeStruct((t, N_MAIN), f32)] + [jax.ShapeDtypeStruct((1, 128), f32)] * 3,
        scratch_shapes=[pltpu.VMEM((N_HEADS, HEAD, HEAD), f32)], input_output_aliases={9: 2},
        compiler_params=_cparams(("arbitrary",)), name="gdn_bwd")(
            ypre, small, proj, a_log, dt_b, on_w, c2, states, d_out, dproj)


def hgrn_fwd(proj, lb, on_w):
    t = proj.shape[0]
    nc = t // REC_ROWS
    cm = jnp.asarray(_hgrn_consts(), bf16)
    ncm = cm.shape[0]

    def body(q_ref, f_ref, i_ref, z_ref, lb_ref, on_ref, cm_ref, o_ref, st_ref, state):
        i = pl.program_id(0)

        @pl.when(i == 0)
        def _():
            state[...] = jnp.zeros_like(state)

        s_in = state[...]
        st_ref[0] = s_in
        s_new, out = _hgrn_step(s_in, q_ref[...], f_ref[...], i_ref[...], z_ref[...], lb_ref[...], on_ref[...],
                                cm_ref[...], i * REC_CHUNKS)
        state[...] = s_new
        o_ref[...] = out

    sec = lambda off: pl.BlockSpec((REC_ROWS,BRANCH), functools.partial(lambda i, b: (i, b), b=off // BRANCH))
    return pl.pallas_call(
        body, grid=(nc,),
        in_specs=[sec(C_HQ), sec(C_HF), sec(C_HI), sec(C_HZ), _row(BRANCH), _row(128),
                  pl.BlockSpec((ncm, CHUNK), lambda i: (0, 0))],
        out_specs=[pl.BlockSpec((REC_ROWS,BRANCH), lambda i: (i, 0)),
                   pl.BlockSpec((1, N_HEADS, HEAD, HEAD), lambda i: (i, 0, 0, 0))],
        out_shape=[jax.ShapeDtypeStruct((t, BRANCH), f32), jax.ShapeDtypeStruct((nc, N_HEADS, HEAD, HEAD), f32)],
        scratch_shapes=[pltpu.VMEM((N_HEADS, HEAD, HEAD), f32)],
        compiler_params=_cparams(("arbitrary",)), name="hgrn_fwd")(proj, proj, proj, proj, lb, on_w, cm)


def hgrn_bwd(proj, lb, on_w, states, d_out, dproj):
    t = proj.shape[0]
    nc = t // REC_ROWS
    cm = jnp.asarray(_hgrn_consts(), bf16)
    ncm = cm.shape[0]

    def body(q_ref, f_ref, i_ref, z_ref, lb_ref, on_ref, cm_ref, st_ref, do_ref, _, dh_ref, dlb_ref, don_ref, dstate):
        i = pl.program_id(0)

        @pl.when(i == 0)
        def _():
            dstate[...] = jnp.zeros_like(dstate)
            dlb_ref[...] = jnp.zeros_like(dlb_ref)
            don_ref[...] = jnp.zeros_like(don_ref)

        cmv = cm_ref[...]
        fn = lambda s, a, b, c, d, l, on: _hgrn_step(s, a, b, c, d, l, on, cmv, (nc - 1 - i) * REC_CHUNKS)
        _, vjp = jax.vjp(fn, st_ref[0], q_ref[...], f_ref[...], i_ref[...], z_ref[...], lb_ref[...], on_ref[...])
        d_s, d_q, d_f, d_i, d_z, d_lb, d_on = vjp((dstate[...], do_ref[...]))
        dstate[...] = d_s
        dh_ref[...] = jnp.concatenate([d_q, d_f, d_i, d_z], axis=1)
        dlb_ref[...] += d_lb
        don_ref[...] += d_on

    rev = lambda i: (nc - 1 - i, 0)
    sec = lambda off: pl.BlockSpec((REC_ROWS,BRANCH), functools.partial(lambda i, b: (nc - 1 - i, b), b=off // BRANCH))
    return pl.pallas_call(
        body, grid=(nc,),
        in_specs=[sec(C_HQ), sec(C_HF), sec(C_HI), sec(C_HZ), _row(BRANCH), _row(128),
                  pl.BlockSpec((ncm, CHUNK), lambda i: (0, 0)),
                  pl.BlockSpec((1, N_HEADS, HEAD, HEAD), lambda i: (nc - 1 - i, 0, 0, 0)),
                  pl.BlockSpec((REC_ROWS,BRANCH), rev), pl.BlockSpec(memory_space=pl.ANY)],
        out_specs=[pl.BlockSpec((REC_ROWS,4 * BRANCH), lambda i: (nc - 1 - i, C_HQ // (4 * BRANCH))), _row(BRANCH), _row(128)],
        out_shape=[jax.ShapeDtypeStruct((t, N_MAIN), f32), jax.ShapeDtypeStruct((1, BRANCH), f32),
                   jax.ShapeDtypeStruct((1, 128), f32)],
        scratch_shapes=[pltpu.VMEM((N_HEADS, HEAD, HEAD), f32)], input_output_aliases={9: 0},
        compiler_params=_cparams(("arbitrary",)), name="hgrn_bwd")(proj, proj, proj, proj, lb, on_w, cm, states, d_out, dproj)


TM_MG = 320


def _const_spec(shape):
    nd = len(shape)
    return pl.BlockSpec(shape, lambda i: (0,) * nd, pipeline_mode=pl.Buffered(1))


def merge_fwd(osb, proj, ogd, ohg, wb, wo, h):
    t = h.shape[0]

    def body(osb_ref, ogd_ref, ohg_ref, zm_ref, wb_ref, wo_ref, h_ref, out_ref):
        a = osb_ref[...] * _silu(zm_ref[:, :BRANCH])
        gate = lambda b: _sigmoid(zm_ref[:, C_MIX + b * D_MODEL:C_MIX + (b + 1) * D_MODEL])
        y = (gate(0) * _dot(a, wb_ref[0]) + gate(1) * _dot(ogd_ref[...], wb_ref[1])
             + gate(2) * _dot(ohg_ref[...], wb_ref[2]))
        out_ref[...] = h_ref[...] + _dot(y, wo_ref[...])

    br = pl.BlockSpec((TM_MG, BRANCH), lambda i: (i, 0))
    return pl.pallas_call(
        body, grid=(t // TM_MG,),
        in_specs=[br, br, br, pl.BlockSpec((TM_MG, W_MERGE), lambda i: (i, 0)),
                  _const_spec((3, BRANCH, D_MODEL)), _const_spec((D_MODEL, D_MODEL)),
                  pl.BlockSpec((TM_MG, D_MODEL), lambda i: (i, 0))],
        out_specs=pl.BlockSpec((TM_MG, D_MODEL), lambda i: (i, 0)),
        out_shape=jax.ShapeDtypeStruct((t, D_MODEL), f32),
        compiler_params=_cparams(("arbitrary",)), name="merge_fwd")(osb, ogd, ohg, proj, wb, wo, h)


def merge_bwd(osb, proj, ogd, ohg, wb, wbt, wot, dh):
    t = dh.shape[0]

    def body(osb_ref, ogd_ref, ohg_ref, zm_ref, wb_ref, wbt_ref, wot_ref, dh_ref,
             dosb_ref, dogd_ref, dohg_ref, dzm_ref, dwo_ref, dwb_ref):
        i = pl.program_id(0)

        @pl.when(i == 0)
        def _():
            dwo_ref[...] = jnp.zeros_like(dwo_ref)
            dwb_ref[...] = jnp.zeros_like(dwb_ref)

        osb = osb_ref[...]
        sbz = zm_ref[:, :BRANCH]
        sgz = _sigmoid(sbz)
        sz = sbz * sgz
        branch_in = (osb * sz, ogd_ref[...], ohg_ref[...])
        dh_v = dh_ref[...]
        dy = _dot(dh_v, wot_ref[...])
        y = jnp.zeros((TM_MG, D_MODEL), f32)
        d_in = []
        for b in range(3):
            cols = slice(C_MIX + b * D_MODEL, C_MIX + (b + 1) * D_MODEL)
            p = _dot(branch_in[b], wb_ref[b])
            g = _sigmoid(zm_ref[:, cols])
            y = y + g * p
            dp = dy * g
            dzm_ref[:, cols] = dy * p * g * (1.0 - g)
            d_in.append(_dot(dp, wbt_ref[b]))
            dwb_ref[b] += _dot(branch_in[b], dp, TN)
        dwo_ref[...] += _dot(y, dh_v, TN)
        dosb_ref[...] = d_in[0] * sz
        dzm_ref[:, :BRANCH] = d_in[0] * osb * (sgz * (1.0 + sbz * (1.0 - sgz)))
        dogd_ref[...] = d_in[1]
        dohg_ref[...] = d_in[2]

    br = pl.BlockSpec((TM_MG, BRANCH), lambda i: (i, 0))
    zm = pl.BlockSpec((TM_MG, W_MERGE), lambda i: (i, 0))
    return pl.pallas_call(
        body, grid=(t // TM_MG,),
        in_specs=[br, br, br, zm,
                  _const_spec((3, BRANCH, D_MODEL)), _const_spec((3, D_MODEL, BRANCH)), _const_spec((D_MODEL, D_MODEL)),
                  pl.BlockSpec((TM_MG, D_MODEL), lambda i: (i, 0))],
        out_specs=[br, br, br, zm, _const_spec((D_MODEL, D_MODEL)), _const_spec((3, BRANCH, D_MODEL))],
        out_shape=[jax.ShapeDtypeStruct((t, BRANCH), f32)] * 3 + [jax.ShapeDtypeStruct((t, N_MAIN), f32),
                   jax.ShapeDtypeStruct((D_MODEL, D_MODEL), f32), jax.ShapeDtypeStruct((3, BRANCH, D_MODEL), f32)],
        compiler_params=_cparams(("arbitrary",)), name="merge_bwd")(osb, ogd, ohg, proj, wb, wbt, wot, dh)


def loss_head(h, target):
    t = h.shape[0]
    nb = t // SB_BLOCK

    def body(h_ref, t_ref, dh_ref, loss_ref):
        i = pl.program_id(0)

        @pl.when(i == 0)
        def _():
            loss_ref[...] = jnp.zeros_like(loss_ref)
            dh_ref[...] = jnp.zeros_like(dh_ref)

        @pl.when(i > 0)
        def _():
            err = h_ref[...] - t_ref[...]
            dh_ref[...] = err * (1.0 / D_MODEL)
            loss_ref[...] += jnp.broadcast_to(jnp.sum(err * err) * (0.5 / D_MODEL), loss_ref.shape)

    return pl.pallas_call(
        body, grid=(nb,),
        in_specs=[pl.BlockSpec((SB_BLOCK, D_MODEL), lambda i: (i, 0)),
                  pl.BlockSpec((SB_BLOCK, D_MODEL), lambda i: (jnp.maximum(i - 1, 0), 0))],
        out_specs=[pl.BlockSpec((SB_BLOCK, D_MODEL), lambda i: (i, 0)), pl.BlockSpec((1, 128), lambda i: (0, 0))],
        out_shape=[jax.ShapeDtypeStruct((t, D_MODEL), f32), jax.ShapeDtypeStruct((1, 128), f32)],
        compiler_params=_cparams(("arbitrary",)), name="loss_head")(h, target)


def adamw(parts, w, m, v, rows_per_step, name):
    r, c = w.shape
    tr = min(rows_per_step, r)
    n_parts = parts.shape[0]

    def body(p_ref, w_ref, m_ref, v_ref, g_ref, d_ref, nm_ref, nv_ref):
        g = p_ref[0].astype(f32)
        for k in range(1, n_parts):
            g = g + p_ref[k].astype(f32)
        m_new = ADAM_B1 * m_ref[...] + (1.0 - ADAM_B1) * g
        v_new = ADAM_B2 * v_ref[...] + (1.0 - ADAM_B2) * jnp.square(g)
        m_hat = m_new / (1.0 - ADAM_B1 ** ADAM_STEP)
        v_hat = v_new / (1.0 - ADAM_B2 ** ADAM_STEP)
        g_ref[...] = g
        d_ref[...] = -ADAM_LR * (m_hat / (jnp.sqrt(v_hat) + ADAM_EPS) + ADAM_WD * w_ref[...])
        nm_ref[...] = m_new
        nv_ref[...] = v_new

    blk = pl.BlockSpec((tr, c), lambda i: (i, 0))
    return pl.pallas_call(
        body, grid=(r // tr,),
        in_specs=[pl.BlockSpec((n_parts, tr, c), lambda i: (0, i, 0)), blk, blk, blk],
        out_specs=[blk] * 4, out_shape=[jax.ShapeDtypeStruct((r, c), f32)] * 4,
        compiler_params=_cparams(("arbitrary",)), name=name)(parts, w, m, v)


def _mesh_pos():
    return lax.axis_index("x"), lax.axis_index("y"), lax.axis_index("c")


def _peer(pos, k):
    x, y, c = pos
    return (1 - x if k & 4 else x, 1 - y if k & 2 else y, 1 - c if k & 1 else c)


def _lin(pos):
    return 4 * pos[0] + 2 * pos[1] + pos[2]


N_CHIPS = 4


def _chip(pos):
    return 2 * pos[0] + pos[1]


def exchange(srcs, scatter, name):
    n = len(srcs)
    shapes = [s.shape[1:] if sc else s.shape for s, sc in zip(srcs, scatter)]
    n_slots = [N_CHIPS if sc == "chips" else N_DEV for sc in scatter]

    def body(*refs):
        src_refs, dst_refs = refs[:n], refs[n:2 * n]
        send_sems, recv_sems, local_sems = refs[2 * n:]
        me = _mesh_pos()
        sends, recvs, locals_ = [], [], []
        for t in range(n):
            slot = _chip if scatter[t] == "chips" else _lin
            own = src_refs[t].at[slot(me)] if scatter[t] else src_refs[t]
            locals_.append(pltpu.make_async_copy(own, dst_refs[t].at[slot(me)], local_sems.at[t]))
            for k in range(1, N_DEV):
                if scatter[t] == "chips" and k & 1:
                    continue
                peer = _peer(me, k)
                src = src_refs[t].at[slot(peer)] if scatter[t] else src_refs[t]
                sends.append(pltpu.make_async_remote_copy(
                    src_ref=src, dst_ref=dst_refs[t].at[slot(me)], send_sem=send_sems.at[t, k - 1],
                    recv_sem=recv_sems.at[t, k - 1], device_id=peer, device_id_type=MESH))
                recvs.append(pltpu.make_async_remote_copy(
                    src_ref=src, dst_ref=dst_refs[t].at[slot(peer)], send_sem=send_sems.at[t, k - 1],
                    recv_sem=recv_sems.at[t, k - 1], device_id=peer, device_id_type=MESH))
        for cp in locals_ + sends:
            cp.start()
        for cp in sends:
            cp.wait_send()
        for cp in recvs:
            cp.wait_recv()
        for cp in locals_:
            cp.wait()

    any_spec = pl.BlockSpec(memory_space=pl.ANY)
    return pl.pallas_call(
        body, in_specs=[any_spec] * n, out_specs=[any_spec] * n,
        out_shape=[jax.ShapeDtypeStruct((ns,) + tuple(sh), s.dtype) for ns, sh, s in zip(n_slots, shapes, srcs)],
        scratch_shapes=[pltpu.SemaphoreType.DMA((n, N_DEV - 1)), pltpu.SemaphoreType.DMA((n, N_DEV - 1)),
                        pltpu.SemaphoreType.DMA((n,))],
        compiler_params=pltpu.CompilerParams(has_side_effects=True), name=name)(*srcs)


def exchange_sibling(stacks, name):
    n = len(stacks)

    def body(*refs):
        src_refs, dst_refs = refs[:n], refs[n:2 * n]
        send_sems, recv_sems = refs[2 * n:]
        x, y, c = _mesh_pos()
        copies = [pltpu.make_async_remote_copy(
            src_ref=src_refs[t].at[2 * q + (1 - c)], dst_ref=dst_refs[t].at[q], send_sem=send_sems.at[t, q],
            recv_sem=recv_sems.at[t, q], device_id=(x, y, 1 - c), device_id_type=MESH)
            for t in range(n) for q in range(N_CHIPS)]
        for cp in copies:
            cp.start()
        for cp in copies:
            cp.wait()

    any_spec = pl.BlockSpec(memory_space=pl.ANY)
    return pl.pallas_call(
        body, in_specs=[any_spec] * n, out_specs=[any_spec] * n,
        out_shape=[jax.ShapeDtypeStruct((N_CHIPS,) + tuple(s.shape[1:]), s.dtype) for s in stacks],
        scratch_shapes=[pltpu.SemaphoreType.DMA((n, N_CHIPS)), pltpu.SemaphoreType.DMA((n, N_CHIPS))],
        compiler_params=pltpu.CompilerParams(has_side_effects=True), name=name)(*stacks)


def add_partials(stack, received, core, rows_per_step, name):
    _, r, c = stack.shape
    tr = min(rows_per_step, r)

    def body(core_ref, own_ref, rcv_ref, out_ref):
        del core_ref
        out_ref[...] = (own_ref[...].astype(f32) + rcv_ref[...].astype(f32)).astype(out_ref.dtype)

    return pl.pallas_call(
        body,
        grid_spec=pltpu.PrefetchScalarGridSpec(
            num_scalar_prefetch=1, grid=(N_CHIPS, r // tr),
            in_specs=[pl.BlockSpec((None, None, tr, c), lambda q, i, core_ref: (q, core_ref[0], i, 0)),
                      pl.BlockSpec((None, tr, c), lambda q, i, core_ref: (q, i, 0))],
            out_specs=pl.BlockSpec((None, tr, c), lambda q, i, core_ref: (q, i, 0))),
        out_shape=jax.ShapeDtypeStruct((N_CHIPS, r, c), stack.dtype),
        compiler_params=_cparams(("arbitrary", "arbitrary")), name=name)(
            core, stack.reshape(N_CHIPS, 2, r, c), received)


def gather_two_level(srcs, name):
    n = len(srcs)
    n_cp = N_DEV - 1

    def body(*refs):
        src_refs, dst_refs = refs[:n], refs[n:2 * n]
        send_sems, recv_sems, local_sems = refs[2 * n:]
        x, y, c = _mesh_pos()
        me, sibling = (x, y, c), (x, y, 1 - c)
        chips = [(1 - x, y), (x, 1 - y), (1 - x, 1 - y)]

        def copy(t, k, block, to, src=None):
            slot = dst_refs[t].at[_lin(block)]
            return pltpu.make_async_remote_copy(
                src_ref=slot if src is None else src, dst_ref=slot, send_sem=send_sems.at[t, k],
                recv_sem=recv_sems.at[t, k], device_id=to, device_id_type=MESH)

        mine, first, passed = [], [], []
        for t in range(n):
            mine.append(pltpu.make_async_copy(src_refs[t], dst_refs[t].at[_lin(me)], local_sems.at[t]))
            first.append(copy(t, 0, me, sibling, src=src_refs[t]))
            first += [copy(t, 1 + j, me, (*chip, c), src=src_refs[t]) for j, chip in enumerate(chips)]
        for cp in mine + first:
            cp.start()
        for j, chip in enumerate(chips):
            for t in range(n):
                copy(t, 1 + j, (*chip, c), me).wait_recv()
                fwd = copy(t, 4 + j, (*chip, c), sibling)
                fwd.start()
                passed.append(fwd)
        for t in range(n):
            copy(t, 0, sibling, me).wait_recv()
            for j, chip in enumerate(chips):
                copy(t, 4 + j, (*chip, 1 - c), me).wait_recv()
        for cp in first + passed:
            cp.wait_send()
        for cp in mine:
            cp.wait()

    any_spec = pl.BlockSpec(memory_space=pl.ANY)
    return pl.pallas_call(
        body, in_specs=[any_spec] * n, out_specs=[any_spec] * n,
        out_shape=[jax.ShapeDtypeStruct((N_DEV,) + tuple(s.shape), s.dtype) for s in srcs],
        scratch_shapes=[pltpu.SemaphoreType.DMA((n, n_cp)), pltpu.SemaphoreType.DMA((n, n_cp)),
                        pltpu.SemaphoreType.DMA((n,))],
        compiler_params=pltpu.CompilerParams(has_side_effects=True), name=name)(*srcs)


PACK_ROWS = 104


def _pad_rows(a, rows):
    return jnp.pad(a, ((0, rows - a.shape[0]), (0, 0)))


def _pad_lanes(a):
    return jnp.pad(a, ((0, 0), (0, 128 - a.shape[1])))


def _pack(norm_w, sbq, sbk, alog, dtb, gon, lbl, hon, loss_row):
    parts = [norm_w.reshape(32, 128), _pad_rows(sbq, 8), _pad_rows(sbk, 8), _pad_rows(_pad_lanes(alog), 8),
             _pad_rows(_pad_lanes(dtb), 8), _pad_rows(gon, 8), lbl.reshape(16, 128), _pad_rows(hon, 8),
             _pad_rows(loss_row, 8)]
    return jnp.concatenate(parts, axis=0)


def _unpack(p):
    return dict(norm_w=p[0:32].reshape(DEPTH, D_MODEL), sb_q_norm=p[32:36], sb_k_norm=p[40:44],
                gdn_a_log=p[48:52, :N_HEADS], gdn_dt_bias=p[56:60, :N_HEADS], gdn_out_norm=p[64:68],
                hgrn_lb_logits=p[72:88].reshape(DEPTH, BRANCH), hgrn_out_norm=p[88:92], loss=p[96, 0])


def _lower_bounds(logits):
    p = jax.nn.softmax(logits, axis=0)
    return jnp.cumsum(p, axis=0) - p[0:1]


def _unshard_cols(g):
    nd = g.ndim
    g = jnp.moveaxis(g, 0, nd - 2)
    return g.reshape(g.shape[:-2] + (N_DEV * g.shape[-1],))


def _shard_cols(a):
    n = a.shape[-1] // N_DEV
    return jnp.moveaxis(a.reshape(a.shape[:-1] + (N_DEV, n)), -2, 0)


def kernel(x, meta_tokens, norm_w, w_in, sb_q_norm, sb_k_norm, gdn_conv_w, gdn_a_log, gdn_dt_bias, gdn_out_norm, hgrn_lb_logits, hgrn_out_norm, w_branch, w_out, loss_target, m_meta_tokens, m_norm_w, m_w_in, m_sb_q_norm, m_sb_k_norm, m_gdn_conv_w, m_gdn_a_log, m_gdn_dt_bias, m_gdn_out_norm, m_hgrn_lb_logits, m_hgrn_out_norm, m_w_branch, m_w_out, v_meta_tokens, v_norm_w, v_w_in, v_sb_q_norm, v_sb_k_norm, v_gdn_conv_w, v_gdn_a_log, v_gdn_dt_bias, v_gdn_out_norm, v_hgrn_lb_logits, v_hgrn_out_norm, v_w_branch, v_w_out):
    g_win, g_wbr, g_wout, g_meta, g_conv = gather_two_level(
        [w_in.astype(bf16), w_branch.astype(bf16), w_out.astype(bf16), meta_tokens, gdn_conv_w], "gather_weights")
    w_full = _unshard_cols(g_win)
    w_main = jnp.concatenate([w_full[..., a:b] for a, b in W_IN_ORDER], axis=-1)
    w_small = jnp.pad(w_full[..., SMALL_OFF:SMALL_OFF + 8], ((0, 0), (0, 0), (0, N_SMALL - 8)))
    wt_main = jnp.swapaxes(w_main, 1, 2)
    wt_small = jnp.swapaxes(w_small, 1, 2)
    wbr = _unshard_cols(g_wbr)
    wbr_t = jnp.swapaxes(wbr, 2, 3)
    wout = jnp.moveaxis(g_wout, 0, 1).reshape(DEPTH, D_MODEL, D_MODEL)
    wout_t = jnp.swapaxes(wout, 1, 2)
    meta = _unshard_cols(g_meta)
    conv_w = _unshard_cols(g_conv)
    lbounds, lb_vjp = jax.vjp(_lower_bounds, hgrn_lb_logits)

    h = jnp.concatenate([jnp.zeros((PAD_FRONT, D_MODEL), f32), meta, x[0]], axis=0)
    row = lambda a: a.reshape(1, -1)
    saved = []
    for l in range(DEPTH):
        proj, small, _, xnt = inproj_fwd(h, row(norm_w[l]), w_main[l], w_small[l])
        osb = sb_fwd(proj, row(sb_q_norm[l]), row(sb_k_norm[l]))
        ypre = conv_fwd(proj, conv_w[l])
        al, dtb = _pad_lanes(row(gdn_a_log[l])), _pad_lanes(row(gdn_dt_bias[l]))
        ogd, gst = gdn_fwd(ypre, small, proj, al, dtb, row(gdn_out_norm[l]))
        ohg, hst = hgrn_fwd(proj, row(lbounds[l]), row(hgrn_out_norm[l]))
        h_next = merge_fwd(osb, proj, ogd, ohg, wbr[l], wout[l], h)
        saved.append((h, proj, small, xnt, osb, ypre, ogd, gst, ohg, hst, al, dtb))
        h = h_next

    dh, loss_row = loss_head(h, loss_target[0])

    gw_main, gw_small, gw_br, gw_out, g_conv_w = [None] * DEPTH, [None] * DEPTH, [None] * DEPTH, [None] * DEPTH, [None] * DEPTH
    g_norm, g_sbq, g_sbk, g_al, g_dt, g_gon, g_lb, g_hon = ([None] * DEPTH for _ in range(8))
    for l in reversed(range(DEPTH)):
        h_l, proj, small, xnt, osb, ypre, ogd, gst, ohg, hst, al, dtb = saved[l]
        d_osb, d_ogd, d_ohg, dproj, gw_out[l], gw_br[l] = merge_bwd(osb, proj, ogd, ohg, wbr[l], wbr_t[l], wout_t[l], dh)
        dproj, g_lb[l], g_hon[l] = hgrn_bwd(proj, row(lbounds[l]), row(hgrn_out_norm[l]), hst, d_ohg, dproj)
        d_ypre, d_small, dproj, g_al[l], g_dt[l], g_gon[l] = gdn_bwd(ypre, small, proj, al, dtb, row(gdn_out_norm[l]), gst, d_ogd, dproj)
        dproj, g_conv_w[l] = conv_bwd(proj, conv_w[l], d_ypre, dproj)
        dproj, g_sbq[l], g_sbk[l] = sb_bwd(proj, row(sb_q_norm[l]), row(sb_k_norm[l]), osb, d_osb, dproj)
        gw_main[l], gw_small[l] = inproj_bwd_w(xnt, dproj, d_small)
        dh, g_norm[l] = inproj_bwd_x(dproj, d_small, wt_main[l], wt_small[l], h_l, row(norm_w[l]), dh)

    gw_main, gw_small = jnp.stack(gw_main), jnp.stack(gw_small)
    starts = np.cumsum([0] + [b - a for a, b in W_IN_ORDER])
    pieces = sorted((a, gw_main[..., int(s):int(s) + b - a]) for (a, b), s in zip(W_IN_ORDER, starts))
    pieces.append((SMALL_OFF, gw_small[..., :8]))
    gw_in = jnp.concatenate([p for _, p in sorted(pieces, key=lambda ap: ap[0])], axis=-1)
    d_lbl = lb_vjp(jnp.concatenate(g_lb, axis=0))[0]
    cat = lambda rows: jnp.concatenate(rows, axis=0)
    pack = _pack(cat(g_norm), cat(g_sbq), cat(g_sbk), cat(g_al)[:, :N_HEADS], cat(g_dt)[:, :N_HEADS], cat(g_gon),
                 d_lbl, cat(g_hon), loss_row)
    g_meta_full = dh[PAD_FRONT:FRONT]
    big = [_shard_cols(gw_in).astype(bf16).reshape(N_DEV, -1, w_in.shape[-1]),
           _shard_cols(jnp.stack(gw_br)).astype(bf16).reshape(N_DEV, -1, w_branch.shape[-1]),
           jnp.swapaxes(jnp.stack(gw_out).reshape(DEPTH, N_DEV, HEAD, D_MODEL), 0, 1).astype(bf16).reshape(N_DEV, -1, D_MODEL)]
    from_sibling = exchange_sibling(big, "exchange_sibling")
    core = lax.axis_index("c").astype(jnp.int32).reshape(1)
    chip_parts = [add_partials(s, r, core, rows, "add_partials_" + nm)
                  for s, r, rows, nm in zip(big, from_sibling, (512, 2048, 512), ("w_in", "w_branch", "w_out"))]
    r_win, r_wbr, r_wout, r_meta, r_conv, r_pack = exchange(
        chip_parts + [_shard_cols(g_meta_full), _shard_cols(jnp.stack(g_conv_w)), pack],
        ["chips", "chips", "chips", True, True, False], "exchange_grads")

    def upd(parts, w, m, v, rows, name):
        shp = w.shape
        two = (-1, shp[-1])
        outs = adamw(parts.reshape((parts.shape[0],) + w.reshape(two).shape), w.reshape(two), m.reshape(two), v.reshape(two), rows, name)
        return [o.reshape(shp) for o in outs]

    res = {}
    res["w_in"] = upd(r_win, w_in, m_w_in, v_w_in, 256, "adamw_w_in")
    res["w_branch"] = upd(r_wbr, w_branch, m_w_branch, v_w_branch, 1024, "adamw_w_branch")
    res["w_out"] = upd(r_wout, w_out, m_w_out, v_w_out, 256, "adamw_w_out")
    res["meta_tokens"] = upd(r_meta, meta_tokens, m_meta_tokens, v_meta_tokens, 16, "adamw_meta")
    res["gdn_conv_w"] = upd(r_conv, gdn_conv_w, m_gdn_conv_w, v_gdn_conv_w, 16, "adamw_conv")
    zero_row = jnp.zeros((1, 128), f32)
    w_pack = _pack(norm_w, sb_q_norm, sb_k_norm, gdn_a_log, gdn_dt_bias, gdn_out_norm, hgrn_lb_logits, hgrn_out_norm, zero_row)
    m_pack = _pack(m_norm_w, m_sb_q_norm, m_sb_k_norm, m_gdn_a_log, m_gdn_dt_bias, m_gdn_out_norm, m_hgrn_lb_logits, m_hgrn_out_norm, zero_row)
    v_pack = _pack(v_norm_w, v_sb_q_norm, v_sb_k_norm, v_gdn_a_log, v_gdn_dt_bias, v_gdn_out_norm, v_hgrn_lb_logits, v_hgrn_out_norm, zero_row)
    packed = [_unpack(o) for o in adamw(r_pack, w_pack, m_pack, v_pack, PACK_ROWS, "adamw_replicated")]
    for name in ("norm_w", "sb_q_norm", "sb_k_norm", "gdn_a_log", "gdn_dt_bias", "gdn_out_norm", "hgrn_lb_logits", "hgrn_out_norm"):
        res[name] = [p[name] for p in packed]
    loss = packed[0]["loss"]
    grad_x = dh[FRONT:][None]

    order = ["meta_tokens", "norm_w", "w_in", "sb_q_norm", "sb_k_norm", "gdn_conv_w", "gdn_a_log", "gdn_dt_bias",
             "gdn_out_norm", "hgrn_lb_logits", "hgrn_out_norm", "w_branch", "w_out"]
    return (loss, grad_x, *[res[n][0] for n in order], *[res[n][1] for n in order],
            *[res[n][2] for n in order], *[res[n][3] for n in order])
```

```python
import functools

import numpy as np
import jax
import jax.numpy as jnp
from jax import lax
from jax.experimental import pallas as pl
from jax.experimental.pallas import tpu as pltpu

f32 = jnp.float32
bf16 = jnp.bfloat16

D_MODEL = 1024
BRANCH = 512
HEAD = 128
N_HEADS = 4
CHUNK = 64
SB_BLOCK = 128
N_META = 16
FRONT = 128
PAD_FRONT = 112
EPS = 1e-6
DEPTH = 4
N_DEV = 8
N_IN = 9224
N_MAIN = 9216
N_SMALL = 128
SMALL_OFF = 4096
C_SBZ, C_MIX = 0, 512
C_GZ = 3584
C_HQ, C_HF, C_HI, C_HZ = 4096, 4608, 5120, 5632
C_GQKV = 6144
C_SBQ, C_SBK, C_SBV = 7680, 8192, 8704
W_MERGE = BRANCH + 3 * D_MODEL
W_IN_ORDER = ((1536, 2048), (6152, 9224), (3584, 4096), (4104, 6152), (2048, 3584), (0, 1536))

ADAM_LR, ADAM_B1, ADAM_B2, ADAM_EPS, ADAM_WD, ADAM_STEP = 0.001, 0.9, 0.999, 1e-08, 0.01, 10

VMEM_LIMIT = 56 * 1024 * 1024
MESH = pl.DeviceIdType.MESH

NN = ((1,), (0,))
NT = ((1,), (1,))
TN = ((0,), (0,))


def _dot(a, b, dims=NN):
    return lax.dot_general(a.astype(bf16), b.astype(bf16), (dims, ((), ())), preferred_element_type=f32)


@jax.custom_vjp
def mm(a, b):
    return _dot(a, b, NN)


mm.defvjp(lambda a, b: (_dot(a, b, NN), (a, b)),
          lambda r, g: (_dot(g, r[1], NT), _dot(r[0], g, TN)))


@jax.custom_vjp
def mm_nt(a, b):
    return _dot(a, b, NT)


mm_nt.defvjp(lambda a, b: (_dot(a, b, NT), (a, b)),
             lambda r, g: (_dot(g, r[1], NN), _dot(g, r[0], TN)))


@jax.custom_vjp
def mm_tn(a, b):
    return _dot(a, b, TN)


mm_tn.defvjp(lambda a, b: (_dot(a, b, TN), (a, b)),
             lambda r, g: (_dot(r[1], g, NT), _dot(r[0], g, NN)))


def _split2(x):
    hi = x.astype(bf16)
    lo = (x - hi.astype(f32)).astype(bf16)
    return hi, lo


def _cdot(c, x, dims):
    hi, lo = _split2(x)
    return (lax.dot_general(c, hi, (dims, ((), ())), preferred_element_type=f32)
            + lax.dot_general(c, lo, (dims, ((), ())), preferred_element_type=f32))


@jax.custom_vjp
def cmm(c, x):
    return _cdot(c, x, NN)


cmm.defvjp(lambda c, x: (_cdot(c, x, NN), c),
           lambda c, g: (jnp.zeros_like(c), _cdot(c, g, TN)))


def _sigmoid(x):
    return jax.nn.sigmoid(x)


def _silu(x):
    return x * jax.nn.sigmoid(x)


def _softplus(x):
    return jnp.maximum(x, 0.0) + jnp.log(1.0 + jnp.exp(-jnp.abs(x)))


def _rms(x, w):
    return x * lax.rsqrt(jnp.mean(x * x, axis=-1, keepdims=True) + EPS) * w


def _cparams(sem=None):
    return pltpu.CompilerParams(dimension_semantics=sem, vmem_limit_bytes=VMEM_LIMIT)


TILES_FWD = (1664, 1024)
TILES_BWD_X = (832, 1024)
TILES_BWD_W = (1664, 1024)


def _row_tile(t, want, unit):
    return max(d for d in range(unit, want + 1, unit) if t % d == 0)


def inproj_fwd(h, nw, w_main, w_small):
    t = h.shape[0]
    TM_IN, TN_IN = _row_tile(t, TILES_FWD[0], 128), TILES_FWD[1]

    def body(h_ref, nw_ref, w_ref, ws_ref, proj_ref, small_ref, xn_ref, xnt_ref):
        @pl.when(pl.program_id(1) == 0)
        def _():
            xn = _rms(h_ref[...], nw_ref[...])
            xn_ref[...] = xn.astype(bf16)
            xnt_ref[...] = jnp.transpose(xn).astype(bf16)
            small_ref[...] = _dot(xn, ws_ref[...])

        proj_ref[...] = jnp.dot(xn_ref[...], w_ref[...], preferred_element_type=f32)

    return pl.pallas_call(
        body, grid=(t // TM_IN, N_MAIN // TN_IN),
        in_specs=[pl.BlockSpec((TM_IN, D_MODEL), lambda i, j: (i, 0)),
                  pl.BlockSpec((1, D_MODEL), lambda i, j: (0, 0)),
                  pl.BlockSpec((D_MODEL, TN_IN), lambda i, j: (0, j)),
                  pl.BlockSpec((D_MODEL, N_SMALL), lambda i, j: (0, 0))],
        out_specs=[pl.BlockSpec((TM_IN, TN_IN), lambda i, j: (i, j)),
                   pl.BlockSpec((TM_IN, N_SMALL), lambda i, j: (i, 0)),
                   pl.BlockSpec((TM_IN, D_MODEL), lambda i, j: (i, 0)),
                   pl.BlockSpec((D_MODEL, TM_IN), lambda i, j: (0, i))],
        out_shape=[jax.ShapeDtypeStruct((t, N_MAIN), f32), jax.ShapeDtypeStruct((t, N_SMALL), f32),
                   jax.ShapeDtypeStruct((t, D_MODEL), bf16), jax.ShapeDtypeStruct((D_MODEL, t), bf16)],
        compiler_params=_cparams(("arbitrary", "arbitrary")), name="inproj_fwd")(h, nw, w_main, w_small)


def inproj_bwd_x(dproj, dsmall, wt_main, wt_small, h, nw, dh_out):
    t = h.shape[0]
    TM_IN, TN_IN = _row_tile(t, TILES_BWD_X[0], 64), TILES_BWD_X[1]
    nk = N_MAIN // TN_IN

    def body(dp_ref, ds_ref, wt_ref, wts_ref, h_ref, nw_ref, dho_ref, dhi_ref, dnw_ref, acc):
        i, k = pl.program_id(0), pl.program_id(1)

        @pl.when(k == 0)
        def _():
            acc[...] = _dot(ds_ref[...], wts_ref[...])

        acc[...] += _dot(dp_ref[...], wt_ref[...])

        @pl.when(k == nk - 1)
        def _():
            x = h_ref[...]
            r = lax.rsqrt(jnp.mean(x * x, axis=-1, keepdims=True) + EPS)
            xh = x * r
            dxn = acc[...]
            dxh = dxn * nw_ref[...]
            dhi_ref[...] = dho_ref[...] + r * (dxh - xh * jnp.mean(dxh * xh, axis=-1, keepdims=True))
            part = jnp.sum(dxn * xh, axis=0, keepdims=True)

            @pl.when(i == 0)
            def _():
                dnw_ref[...] = part

            @pl.when(i > 0)
            def _():
                dnw_ref[...] += part

    return pl.pallas_call(
        body, grid=(t // TM_IN, nk),
        in_specs=[pl.BlockSpec((TM_IN, TN_IN), lambda i, k: (i, k)),
                  pl.BlockSpec((TM_IN, N_SMALL), lambda i, k: (i, 0)),
                  pl.BlockSpec((TN_IN, D_MODEL), lambda i, k: (k, 0)),
                  pl.BlockSpec((N_SMALL, D_MODEL), lambda i, k: (0, 0)),
                  pl.BlockSpec((TM_IN, D_MODEL), lambda i, k: (i, 0)),
                  pl.BlockSpec((1, D_MODEL), lambda i, k: (0, 0)),
                  pl.BlockSpec((TM_IN, D_MODEL), lambda i, k: (i, 0))],
        out_specs=[pl.BlockSpec((TM_IN, D_MODEL), lambda i, k: (i, 0)),
                   pl.BlockSpec((1, D_MODEL), lambda i, k: (0, 0))],
        out_shape=[jax.ShapeDtypeStruct((t, D_MODEL), f32), jax.ShapeDtypeStruct((1, D_MODEL), f32)],
        scratch_shapes=[pltpu.VMEM((TM_IN, D_MODEL), f32)],
        compiler_params=_cparams(("arbitrary", "arbitrary")), name="inproj_bwd_x")(
            dproj, dsmall, wt_main, wt_small, h, nw, dh_out)


def inproj_bwd_w(xnt, dproj, dsmall):
    t = xnt.shape[1]
    TM_IN, TN_IN = _row_tile(t, TILES_BWD_W[0], 128), TILES_BWD_W[1]
    nt = t // TM_IN

    def body(xnt_ref, dp_ref, ds_ref, dw_ref, dws_ref):
        n, s = pl.program_id(0), pl.program_id(1)
        part = _dot(xnt_ref[...], dp_ref[...])

        @pl.when(s == 0)
        def _():
            dw_ref[...] = part

        @pl.when(s > 0)
        def _():
            dw_ref[...] += part

        @pl.when(n == 0)
        def _():
            ps = _dot(xnt_ref[...], ds_ref[...])

            @pl.when(s == 0)
            def _():
                dws_ref[...] = ps

            @pl.when(s > 0)
            def _():
                dws_ref[...] += ps

    return pl.pallas_call(
        body, grid=(N_MAIN // TN_IN, nt),
        in_specs=[pl.BlockSpec((D_MODEL, TM_IN), lambda n, s: (0, s)),
                  pl.BlockSpec((TM_IN, TN_IN), lambda n, s: (s, n)),
                  pl.BlockSpec((TM_IN, N_SMALL), lambda n, s: (s, 0))],
        out_specs=[pl.BlockSpec((D_MODEL, TN_IN), lambda n, s: (0, n)),
                   pl.BlockSpec((D_MODEL, N_SMALL), lambda n, s: (0, 0))],
        out_shape=[jax.ShapeDtypeStruct((D_MODEL, N_MAIN), f32), jax.ShapeDtypeStruct((D_MODEL, N_SMALL), f32)],
        compiler_params=_cparams(("arbitrary", "arbitrary")), name="inproj_bwd_w")(xnt, dproj, dsmall)


SB_SCALE = HEAD ** -0.5


SB_SUB = 3
SB_KS = SB_SUB * SB_BLOCK


SB_PADR = SB_KS - SB_BLOCK


def _sb_padded(t):
    return t + SB_PADR


def _sb_rows(i, d):
    start = (i + 1) * SB_BLOCK - (d + 1) * SB_KS
    return start, pl.ds(pl.multiple_of(start + SB_PADR, SB_BLOCK), SB_KS)


def _sb_prep(k_ref, v_ref, kw_ref, kn_scr, vb_scr, nb):
    def prep(b, c):
        rows = pl.ds(pl.multiple_of(b * SB_BLOCK, SB_BLOCK), SB_BLOCK)
        pad_rows = pl.ds(pl.multiple_of(SB_PADR + b * SB_BLOCK, SB_BLOCK), SB_BLOCK)
        kn_scr[pad_rows, :] = _rms(k_ref[rows, :], kw_ref[...]).astype(bf16)
        vb_scr[pad_rows, :] = v_ref[rows, :].astype(bf16)
        return c

    lax.fori_loop(0, nb, prep, 0)
    kn_scr[:SB_PADR, :] = jnp.zeros((SB_PADR, HEAD), bf16)
    vb_scr[:SB_PADR, :] = jnp.zeros((SB_PADR, HEAD), bf16)


def _tri_ext(cmp):
    r = lax.broadcasted_iota(jnp.int32, (SB_BLOCK, 2 * SB_BLOCK), 0)
    c = lax.broadcasted_iota(jnp.int32, (SB_BLOCK, 2 * SB_BLOCK), 1)
    return jnp.where((c >= SB_BLOCK) | cmp(r, c), 1.0, 0.0).astype(bf16)


def _sb_suffix(x, carry, tri_ext):
    hi, lo = _split2(x)
    parts = [p[:, c * SB_BLOCK:(c + 1) * SB_BLOCK] for p in (hi, lo) for c in range(SB_SUB)]
    w = jnp.dot(jnp.concatenate(parts, axis=0), tri_ext, preferred_element_type=f32)
    outs = [None] * SB_SUB
    for c in reversed(range(SB_SUB)):
        blk = w[c * SB_BLOCK:(c + 1) * SB_BLOCK] + w[(SB_SUB + c) * SB_BLOCK:(SB_SUB + c + 1) * SB_BLOCK]
        outs[c] = carry + blk[:, :SB_BLOCK]
        carry = carry + blk[:, SB_BLOCK:]
    return jnp.concatenate(outs, axis=1), carry


def _sb_scores(qn, kt, i, start, masked):
    z = lax.dot_general(qn, kt, (NT, ((), ())), preferred_element_type=f32) * SB_SCALE
    lsz = jnp.minimum(z, 0.0) - jnp.log(1.0 + jnp.exp(-jnp.abs(z)))
    lk = lsz - z
    mask = None
    if masked:
        t_idx = i * SB_BLOCK + lax.broadcasted_iota(jnp.int32, (SB_BLOCK, SB_KS), 0)
        s_idx = start + lax.broadcasted_iota(jnp.int32, (SB_BLOCK, SB_KS), 1)
        mask = (s_idx < t_idx) & (s_idx >= PAD_FRONT)
        lk = jnp.where(mask, lk, 0.0)
    return mask, lsz, lk


SB_DEAD = -104.0


def _sb_walk(i, tile, carry, first_done=False):
    n = (i + SB_SUB) // SB_SUB
    live = lambda c: jnp.max(c[1]) > SB_DEAD
    if not first_done:
        carry = tile(0, carry, True)
    _, carry = lax.while_loop(lambda st: (st[0] <= n - 2) & live(st[1]),
                              lambda st: (st[0] + 1, tile(st[0], st[1], False)), (1, carry))
    return lax.cond((n >= 2) & live(carry), lambda c: tile(n - 1, c, True), lambda c: c, carry)


SB_QB = 5
SB_QROWS = SB_QB * SB_BLOCK


def _drain(gen):
    return _interleave([gen])[0]


def sb_fwd(proj, qw, kw):
    t = proj.shape[0]
    nb = t // SB_BLOCK

    def body(q_ref, k_ref, v_ref, qw_ref, kw_ref, o_ref, kn_scr, vb_scr):
        grp = pl.program_id(1)

        @pl.when(grp == 0)
        def _():
            _sb_prep(k_ref, v_ref, kw_ref, kn_scr, vb_scr, nb)

        u_ex = _tri_ext(lambda r, c: r > c)

        def tile(i, qn, jb, carry, masked):
            acc, r_carry = carry
            start, rows = _sb_rows(i, jb)
            mask, lsz, lk = _sb_scores(qn, kn_scr[rows, :], i, start, masked)
            yield
            passed, r_carry = _sb_suffix(lk, r_carry, u_ex)
            yield
            a = jnp.exp(lsz + passed)
            if masked:
                a = jnp.where(mask, a, 0.0)
            a_hi, a_lo = _split2(a)
            both = jnp.dot(jnp.concatenate([a_hi, a_lo], axis=0), vb_scr[rows, :], preferred_element_type=f32)
            yield
            return acc + (both[:SB_BLOCK] + both[SB_BLOCK:]), r_carry

        zeros = jnp.zeros((SB_BLOCK, HEAD), f32)
        blocks = [(grp * SB_QB + b, _rms(q_ref[b * SB_BLOCK:(b + 1) * SB_BLOCK, :], qw_ref[...]).astype(bf16))
                  for b in range(SB_QB)]
        firsts = _interleave([tile(i, qn, 0, (zeros, zeros), True) for i, qn in blocks])
        for b, ((i, qn), carry) in enumerate(zip(blocks, firsts)):
            step = lambda jb, c, masked, i=i, qn=qn: _drain(tile(i, qn, jb, c, masked))
            acc, _ = _sb_walk(i, step, carry, first_done=True)
            o_ref[b * SB_BLOCK:(b + 1) * SB_BLOCK, :] = acc

    qb, cb, vb = C_SBQ // HEAD, C_SBK // HEAD, C_SBV // HEAD
    return pl.pallas_call(
        body, grid=(N_HEADS, nb // SB_QB),
        in_specs=[pl.BlockSpec((SB_QROWS, HEAD), lambda h, i: (i, qb + h)),
                  pl.BlockSpec((t, HEAD), lambda h, i: (0, cb + h)),
                  pl.BlockSpec((t, HEAD), lambda h, i: (0, vb + h)),
                  pl.BlockSpec((1, HEAD), lambda h, i: (0, 0)),
                  pl.BlockSpec((1, HEAD), lambda h, i: (0, 0))],
        out_specs=pl.BlockSpec((SB_QROWS, HEAD), lambda h, i: (i, h)),
        out_shape=jax.ShapeDtypeStruct((t, BRANCH), f32),
        scratch_shapes=[pltpu.VMEM((_sb_padded(t), HEAD), bf16), pltpu.VMEM((_sb_padded(t), HEAD), bf16)],
        compiler_params=_cparams(("arbitrary", "arbitrary")), name="sb_fwd")(proj, proj, proj, qw, kw)


def sb_bwd(proj, qw, kw, o, do, dproj):
    t = proj.shape[0]
    nb = t // SB_BLOCK

    def body(q_ref, k_ref, v_ref, qw_ref, kw_ref, o_ref, do_ref, _, dp_ref, dqw_ref, dkw_ref,
             kn_scr, vb_scr, dk_acc, dv_acc, dq_stage, dq_sems, kv_sems):
        h, grp = pl.program_id(0), pl.program_id(1)
        n_grp = nb // SB_QB
        step = h * n_grp + grp
        slot = step % 2

        def dq_copy(sl, head):
            return pltpu.make_async_copy(
                dq_stage.at[sl], dp_ref.at[pl.ds(pl.multiple_of(grp * SB_QROWS, SB_QROWS), SB_QROWS),
                                           pl.ds(C_SBQ + head * HEAD, HEAD)], dq_sems.at[sl])

        @pl.when(grp == 0)
        def _():
            _sb_prep(k_ref, v_ref, kw_ref, kn_scr, vb_scr, nb)
            dk_acc[...] = jnp.zeros_like(dk_acc)
            dv_acc[...] = jnp.zeros_like(dv_acc)

        @pl.when((grp == 0) & (h == 0))
        def _():
            dqw_ref[...] = jnp.zeros_like(dqw_ref)
            dkw_ref[...] = jnp.zeros_like(dkw_ref)

        u_ex = _tri_ext(lambda r, c: r > c)
        u_in = _tri_ext(lambda r, c: r >= c)

        def tile(i, qn, dob, d_row, jb, carry, masked):
            dq, r_carry, f_carry = carry
            start, rows = _sb_rows(i, jb)
            kt = kn_scr[rows, :]
            vt = vb_scr[rows, :]
            mask, lsz, lk = _sb_scores(qn, kt, i, start, masked)
            da = lax.dot_general(dob, vt, (NT, ((), ())), preferred_element_type=f32)
            yield
            passed, r_carry = _sb_suffix(lk, r_carry, u_ex)
            yield
            a = jnp.exp(lsz + passed)
            if masked:
                a = jnp.where(mask, a, 0.0)
            e = a * da
            e_suf, f_carry = _sb_suffix(e, f_carry, u_in)
            yield
            sg = jnp.exp(lsz)
            dz = (e * (1.0 - sg) - (d_row - e_suf) * sg) * SB_SCALE
            if masked:
                dz = jnp.where(mask, dz, 0.0)
            dzb = dz.astype(bf16)
            dq = dq + jnp.dot(dzb, kt, preferred_element_type=f32)
            dk_acc[rows, :] += lax.dot_general(dzb, qn, (TN, ((), ())), preferred_element_type=f32)
            dv_acc[rows, :] += lax.dot_general(a.astype(bf16), dob, (TN, ((), ())), preferred_element_type=f32)
            yield
            return dq, r_carry, f_carry

        zeros = jnp.zeros((SB_BLOCK, HEAD), f32)
        blocks = []
        for b in range(SB_QB):
            rows_b = slice(b * SB_BLOCK, (b + 1) * SB_BLOCK)
            q = q_ref[rows_b, :]
            rq = lax.rsqrt(jnp.mean(q * q, axis=-1, keepdims=True) + EPS)
            qh = q * rq
            dob = do_ref[rows_b, :].astype(bf16)
            d_row = jnp.sum(dob.astype(f32) * o_ref[rows_b, :], axis=-1, keepdims=True)
            blocks.append((grp * SB_QB + b, (qh * qw_ref[...]).astype(bf16), dob, d_row, rq, qh))
        firsts = _interleave([tile(i, qn, dob, d_row, 0, (zeros, zeros, zeros), True)
                              for i, qn, dob, d_row, _, _ in blocks])

        @pl.when(step >= 2)
        def _():
            dq_copy(slot, 0).wait()

        for b, ((i, qn, dob, d_row, rq, qh), carry) in enumerate(zip(blocks, firsts)):
            one = lambda jb, c, masked, i=i, qn=qn, dob=dob, d_row=d_row: _drain(tile(i, qn, dob, d_row, jb, c, masked))
            dqn, _, _ = _sb_walk(i, one, carry, first_done=True)
            gq = dqn * qw_ref[...]
            dqw_ref[...] += jnp.sum(dqn * qh, axis=0, keepdims=True)
            dq_stage[slot, b * SB_BLOCK:(b + 1) * SB_BLOCK, :] = rq * (gq - qh * jnp.mean(gq * qh, axis=-1, keepdims=True))
        for head in range(N_HEADS):
            @pl.when(h == head)
            def _(head=head):
                dq_copy(slot, head).start()

        @pl.when(grp == n_grp - 1)
        def _():
            def fin(b, c):
                rows = pl.ds(pl.multiple_of(b * SB_BLOCK, SB_BLOCK), SB_BLOCK)
                pad_rows = pl.ds(pl.multiple_of(SB_PADR + b * SB_BLOCK, SB_BLOCK), SB_BLOCK)
                kk = k_ref[rows, :]
                rk = lax.rsqrt(jnp.mean(kk * kk, axis=-1, keepdims=True) + EPS)
                kh = kk * rk
                dkn = dk_acc[pad_rows, :]
                gk = dkn * kw_ref[...]
                dk_acc[pad_rows, :] = rk * (gk - kh * jnp.mean(gk * kh, axis=-1, keepdims=True))
                dkw_ref[...] += jnp.sum(dkn * kh, axis=0, keepdims=True)
                return c

            lax.fori_loop(0, nb, fin, 0)
            for head in range(N_HEADS):
                @pl.when(h == head)
                def _(head=head):
                    outs = [pltpu.make_async_copy(acc.at[pl.ds(SB_PADR, t)], dp_ref.at[:, pl.ds(c0 + head * HEAD, HEAD)],
                                                  kv_sems.at[n])
                            for n, (acc, c0) in enumerate(((dk_acc, C_SBK), (dv_acc, C_SBV)))]
                    for cp in outs:
                        cp.start()
                    for cp in outs:
                        cp.wait()

        @pl.when(step == N_HEADS * n_grp - 1)
        def _():
            dq_copy(1 - slot, 0).wait()
            dq_copy(slot, 0).wait()

    qb, cb, vb = C_SBQ // HEAD, C_SBK // HEAD, C_SBV // HEAD
    blk = pl.BlockSpec((SB_QROWS, HEAD), lambda h, i: (i, h))
    wsp = pl.BlockSpec((1, HEAD), lambda h, i: (0, 0))
    any_spec = pl.BlockSpec(memory_space=pl.ANY)
    return pl.pallas_call(
        body, grid=(N_HEADS, nb // SB_QB),
        in_specs=[pl.BlockSpec((SB_QROWS, HEAD), lambda h, i: (i, qb + h)),
                  pl.BlockSpec((t, HEAD), lambda h, i: (0, cb + h)),
                  pl.BlockSpec((t, HEAD), lambda h, i: (0, vb + h)), wsp, wsp, blk, blk, any_spec],
        out_specs=[any_spec, wsp, wsp],
        out_shape=[jax.ShapeDtypeStruct((t, N_MAIN), f32)] + [jax.ShapeDtypeStruct((1, HEAD), f32)] * 2,
        scratch_shapes=[pltpu.VMEM((_sb_padded(t), HEAD), bf16), pltpu.VMEM((_sb_padded(t), HEAD), bf16),
                        pltpu.VMEM((_sb_padded(t), HEAD), f32), pltpu.VMEM((_sb_padded(t), HEAD), f32),
                        pltpu.VMEM((2, SB_QROWS, HEAD), f32), pltpu.SemaphoreType.DMA((2,)), pltpu.SemaphoreType.DMA((2,))],
        input_output_aliases={7: 0},
        compiler_params=_cparams(("arbitrary", "arbitrary")), name="sb_bwd")(proj, proj, proj, qw, kw, o, do, dproj)


TM_CONV = 640
CONV_W = 4
GQKV = 3 * BRANCH


def conv_fwd(proj, cw):
    t = proj.shape[0]
    halo_blocks = TM_CONV // 8

    def body(x0_ref, x1_ref, x2_ref, p0_ref, p1_ref, p2_ref, cw_ref, y_ref):
        i = pl.program_id(0)
        for s, (x_ref, p_ref) in enumerate(((x0_ref, p0_ref), (x1_ref, p1_ref), (x2_ref, p2_ref))):
            prev = jnp.where(i > 0, p_ref[...], 0.0)
            xx = jnp.concatenate([prev, x_ref[...]], axis=0)
            cols = slice(s * BRANCH, (s + 1) * BRANCH)
            y = xx[8:] * cw_ref[CONV_W - 1:CONV_W, cols]
            for k in range(CONV_W - 1):
                y = y + pltpu.roll(xx, CONV_W - 1 - k, 0)[8:] * cw_ref[k:k + 1, cols]
            y_ref[:, cols] = y

    c0 = C_GQKV // BRANCH
    xs = [pl.BlockSpec((TM_CONV, BRANCH), functools.partial(lambda i, s: (i, c0 + s), s=s)) for s in range(3)]
    ps = [pl.BlockSpec((8, BRANCH), functools.partial(lambda i, s: (jnp.maximum(i * halo_blocks - 1, 0), c0 + s), s=s))
          for s in range(3)]
    return pl.pallas_call(
        body, grid=(t // TM_CONV,),
        in_specs=xs + ps + [pl.BlockSpec((CONV_W, GQKV), lambda i: (0, 0))],
        out_specs=pl.BlockSpec((TM_CONV, GQKV), lambda i: (i, 0)),
        out_shape=jax.ShapeDtypeStruct((t, GQKV), f32),
        compiler_params=_cparams(("arbitrary",)), name="conv_fwd")(proj, proj, proj, proj, proj, proj, cw)


def conv_bwd(proj, cw, dy, dproj):
    t = proj.shape[0]
    nt = t // TM_CONV
    halo_blocks = TM_CONV // 8

    def body(x0_ref, x1_ref, x2_ref, p0_ref, p1_ref, p2_ref, cw_ref, dy_ref, dyn_ref, _, dx_ref, dw_ref):
        i = pl.program_id(0)

        @pl.when(i == 0)
        def _():
            dw_ref[...] = jnp.zeros_like(dw_ref)

        nxt = jnp.where(i < nt - 1, dyn_ref[...], 0.0)
        dyy = jnp.concatenate([dy_ref[...], nxt], axis=0)
        n_rows = TM_CONV + 8
        dx = dyy[:TM_CONV] * cw_ref[CONV_W - 1:CONV_W, :]
        for k in range(CONV_W - 1):
            sh = CONV_W - 1 - k
            dx = dx + pltpu.roll(dyy, n_rows - sh, 0)[:TM_CONV] * cw_ref[k:k + 1, :]
        dx_ref[...] = dx
        dy_c = dy_ref[...]
        for s, (x_ref, p_ref) in enumerate(((x0_ref, p0_ref), (x1_ref, p1_ref), (x2_ref, p2_ref))):
            prev = jnp.where(i > 0, p_ref[...], 0.0)
            xx = jnp.concatenate([prev, x_ref[...]], axis=0)
            cols = slice(s * BRANCH, (s + 1) * BRANCH)
            for k in range(CONV_W):
                sh = CONV_W - 1 - k
                xs = xx[8:] if sh == 0 else pltpu.roll(xx, sh, 0)[8:]
                dw_ref[k:k + 1, cols] += jnp.sum(xs * dy_c[:, cols], axis=0, keepdims=True)

    c0 = C_GQKV // BRANCH
    xs = [pl.BlockSpec((TM_CONV, BRANCH), functools.partial(lambda i, s: (i, c0 + s), s=s)) for s in range(3)]
    ps = [pl.BlockSpec((8, BRANCH), functools.partial(lambda i, s: (jnp.maximum(i * halo_blocks - 1, 0), c0 + s), s=s))
          for s in range(3)]
    return pl.pallas_call(
        body, grid=(nt,),
        in_specs=xs + ps + [pl.BlockSpec((CONV_W, GQKV), lambda i: (0, 0)),
                            pl.BlockSpec((TM_CONV, GQKV), lambda i: (i, 0)),
                            pl.BlockSpec((8, GQKV), lambda i: (jnp.minimum((i + 1) * halo_blocks, nt * halo_blocks - 1), 0)),
                            pl.BlockSpec(memory_space=pl.ANY)],
        out_specs=[pl.BlockSpec((TM_CONV, GQKV), lambda i: (i, C_GQKV // GQKV)), pl.BlockSpec((CONV_W, GQKV), lambda i: (0, 0))],
        out_shape=[jax.ShapeDtypeStruct((t, N_MAIN), f32), jax.ShapeDtypeStruct((CONV_W, GQKV), f32)],
        input_output_aliases={9: 0},
        compiler_params=_cparams(("arbitrary",)), name="conv_bwd")(proj, proj, proj, proj, proj, proj, cw, dy, dy, dproj)


def _iota2(n, m, d):
    return lax.broadcasted_iota(jnp.int32, (n, m), d)


def _lane_pick(row_or_mat, idx):
    lanes = lax.broadcasted_iota(jnp.int32, row_or_mat.shape, row_or_mat.ndim - 1)
    return jnp.sum(jnp.where(lanes == idx, row_or_mat, 0.0), axis=-1, keepdims=True)


def _cumsum_consts():
    i = np.arange(CHUNK)
    incl = i[None, :] <= i[:, None]
    suf = i[None, :] > i[:, None]
    return np.concatenate([incl, suf], 0).astype(np.float32)


HG_LEVELS = (64, 32, 16, 8, 4, 2)


def _hgrn_consts():
    i = np.arange(CHUNK)
    rows = [i[None, :] <= i[:, None], i[None, :] > i[:, None]]
    for b in HG_LEVELS:
        ref = (i // b) * b + b // 2 - 1
        second = (i % b) >= b // 2
        rows.append((i[None, :] > ref[:, None]) & (i[None, :] <= i[:, None]) & second[:, None])
        rows.append((i[None, :] > i[:, None]) & (i[None, :] <= ref[:, None]) & (~second)[:, None])
    return np.concatenate(rows, 0).astype(np.float32)


N_SQUARINGS = 5


def _solve_chain(ms, rhss):
    xs = [r - mm(m, r) for m, r in zip(ms, rhss)]
    powers = [list(ms)]
    for _ in range(N_SQUARINGS):
        powers.append([mm(p, p) for p in powers[-1]])
        xs = [x + mm(p, x) for p, x in zip(powers[-1], xs)]
    return tuple(xs), powers


@jax.custom_vjp
def unit_lower_solve_multi(ms, rhss):
    return _solve_chain(ms, rhss)[0]


def _solve_fwd(ms, rhss):
    xs, powers = _solve_chain(ms, rhss)
    return xs, (powers, xs)


def _solve_bwd(res, gs):
    powers, xs = res
    ys = [g - mm_tn(p, g) for p, g in zip(powers[0], gs)]
    for ps in powers[1:]:
        ys = [y + mm_tn(p, y) for p, y in zip(ps, ys)]
    return tuple(-mm_nt(y, x) for y, x in zip(ys, xs)), tuple(ys)


unit_lower_solve_multi.defvjp(_solve_fwd, _solve_bwd)

HC = N_HEADS * CHUNK
BATCH0 = ((0,), (0,))


def _bdot(a, b, contract):
    return lax.dot_general(a.astype(bf16), b.astype(bf16), (contract, BATCH0), preferred_element_type=f32)


B_NN = ((2,), (1,))
B_NT = ((2,), (2,))
B_TN = ((1,), (1,))


@jax.custom_vjp
def bmm(a, b):
    return _bdot(a, b, B_NN)


bmm.defvjp(lambda a, b: (_bdot(a, b, B_NN), (a, b)),
           lambda r, g: (_bdot(g, r[1], B_NT), _bdot(r[0], g, B_TN)))


@jax.custom_vjp
def bmm_nt(a, b):
    return _bdot(a, b, B_NT)


bmm_nt.defvjp(lambda a, b: (_bdot(a, b, B_NT), (a, b)),
              lambda r, g: (_bdot(g, r[1], B_NN), _bdot(g, r[0], B_TN)))


@jax.custom_vjp
def bmm_tn(a, b):
    return _bdot(a, b, B_TN)


bmm_tn.defvjp(lambda a, b: (_bdot(a, b, B_TN), (a, b)),
              lambda r, g: (_bdot(r[1], g, B_NT), _bdot(r[0], g, B_NN)))


def _stack_heads(x):
    return jnp.concatenate([x[:, h * HEAD:(h + 1) * HEAD] for h in range(N_HEADS)], axis=0)


def _unstack_heads(x):
    return jnp.concatenate([x[h * CHUNK:(h + 1) * CHUNK] for h in range(N_HEADS)], axis=1)


REC_CHUNKS = 5
REC_ROWS = REC_CHUNKS * CHUNK


def _interleave(gens):
    n = len(gens)
    sends, results, done = [None] * n, [None] * n, [False] * n
    while not all(done):
        asks = []
        for j in range(n):
            if done[j]:
                continue
            try:
                ask = gens[j].send(sends[j])
                if ask is not None:
                    asks.append((j, ask))
            except StopIteration as stop:
                results[j], done[j] = stop.value, True
            sends[j] = None
        if asks:
            xs = unit_lower_solve_multi(tuple(a[1] for _, a in asks), tuple(a[2] for _, a in asks))
            for (j, _), x in zip(asks, xs):
                sends[j] = x
    return results


def _chunks_of(a):
    return [a[j * CHUNK:(j + 1) * CHUNK] for j in range(REC_CHUNKS)]


def _gdn_step(state, ypre, small, gz, a_log, dt_b, on_w, c2, first_chunk):
    masks = _gdn_masks()
    gens = [_gdn_intra(y, s, a_log, dt_b, c2, _vmask(first_chunk + j), masks)
            for j, (y, s) in enumerate(zip(_chunks_of(ypre), _chunks_of(small)))]
    outs = []
    for intra, z in zip(_interleave(gens), _chunks_of(gz)):
        state, o = _gdn_inter(state, intra, z, on_w)
        outs.append(o)
    return state, jnp.concatenate(outs, axis=0)


def _gdn_inter(state, intra, gz, on_w):
    u, w, aqk, q_dec, k_dec, g_last = intra
    per_head = lambda a: a.reshape(N_HEADS, CHUNK, HEAD)
    v_new = u - bmm(per_head(w), state).reshape(HC, HEAD)
    o = bmm(per_head(q_dec), state).reshape(HC, HEAD) + mm(aqk, v_new)
    new_state = state * g_last + bmm_tn(per_head(k_dec), per_head(v_new))
    return new_state, _unstack_heads(_rms(o, on_w)) * _silu(gz)


def _gdn_masks():
    r = _iota2(HC, HC, 0)
    c = _iota2(HC, HC, 1)
    same_head = (r >> (CHUNK.bit_length() - 1)) == (c >> (CHUNK.bit_length() - 1))
    return same_head & (r >= c), same_head & (r > c)


def _gdn_intra(ypre, small, a_log, dt_b, c2, vm, masks):
    causal, strict = masks
    q = _silu(_stack_heads(ypre[:, :BRANCH]))
    k = _silu(_stack_heads(ypre[:, BRANCH:2 * BRANCH]))
    v = _silu(_stack_heads(ypre[:, 2 * BRANCH:]))
    q = q * lax.rsqrt(jnp.sum(q * q, axis=-1, keepdims=True) + EPS) * (HEAD ** -0.5)
    k = k * lax.rsqrt(jnp.sum(k * k, axis=-1, keepdims=True) + EPS)
    col = lambda f: jnp.concatenate([f(h) for h in range(N_HEADS)], axis=0)
    chunk_col = lambda x: jnp.broadcast_to(x, (CHUNK, 1))
    beta = _sigmoid(col(lambda h: _lane_pick(small, h))) * col(lambda h: vm)
    g = (-jnp.exp(col(lambda h: chunk_col(_lane_pick(a_log, h))))
         * _softplus(col(lambda h: _lane_pick(small, N_HEADS + h)) + col(lambda h: chunk_col(_lane_pick(dt_b, h)))))
    g_l = _unstack_heads(jnp.broadcast_to(g, (HC, HEAD)))
    e2 = cmm(c2, g_l)
    yield
    gc = _stack_heads(e2[:CHUNK])
    gsuf = _stack_heads(e2[CHUNK:])
    g_row = jnp.broadcast_to(jnp.transpose(gc)[0:1, :], (HC, HC))
    g_col = jnp.concatenate([gc, gc], axis=1)
    dec = jnp.where(causal, jnp.exp(jnp.minimum(g_col - g_row, 0.0)), 0.0)
    kb = k * beta
    kk = mm_nt(kb, k)
    qk = mm_nt(q, k)
    yield
    m = jnp.where(strict, kk * dec, 0.0)
    x = yield ("solve", m, jnp.concatenate([v * beta, kb * jnp.exp(gc)], axis=1))
    aqk = jnp.where(causal, qk * dec, 0.0)
    tot = jnp.sum(g_l, axis=0, keepdims=True)
    g_last = jnp.exp(jnp.stack([tot[:, h * HEAD:(h + 1) * HEAD] for h in range(N_HEADS)], axis=0))
    return x[:, :HEAD], x[:, HEAD:], aqk, q * jnp.exp(gc), k * jnp.exp(gsuf), g_last


def _hgrn_step(state, hq, hf, hi, hz, lb, on_w, cm, first_chunk):
    masks = _hgrn_masks()
    gens = [_hgrn_intra(q, f, i, lb, cm, _vmask(first_chunk + j), masks)
            for j, (q, f, i) in enumerate(zip(_chunks_of(hq), _chunks_of(hf), _chunks_of(hi)))]
    outs = []
    for (q_dec, k_dec, v, o_intra, g_end), z in zip(_interleave(gens), _chunks_of(hz)):
        per_head = lambda a: a.reshape(N_HEADS, CHUNK, HEAD)
        o = bmm_nt(per_head(q_dec), state).reshape(HC, HEAD) + o_intra
        state = state * g_end + bmm_tn(per_head(v), per_head(k_dec))
        outs.append(_unstack_heads(_rms(o, on_w)) * _silu(z))
    return state, jnp.concatenate(outs, axis=0)


def _hgrn_masks():
    r = _iota2(HC, HC, 0)
    c = _iota2(HC, HC, 1)
    pairs = []
    for b in HG_LEVELS:
        sh = b.bit_length() - 1
        pairs.append(((r >> sh) == (c >> sh)) & ((r & (b - 1)) >= b // 2) & ((c & (b - 1)) < b // 2))
    return r == c, pairs


def _hgrn_intra(hq, hf, hi, lb, cm, vm, masks):
    diag, pairs = masks
    forget = lb + (1.0 - lb) * _sigmoid(hf)
    g_l = jnp.log(forget)
    e = cmm(cm, g_l)
    q = _stack_heads(_silu(hq))
    k = _stack_heads((1.0 - lb) * _sigmoid(-hf))
    v = _stack_heads(hi * vm)
    yield
    sect = lambda n: _stack_heads(e[n * CHUNK:(n + 1) * CHUNK])
    gc, gsuf = sect(0), sect(1)
    a = jnp.where(diag, jnp.sum(q * k, axis=-1, keepdims=True), 0.0)
    for li, pair in enumerate(pairs):
        a = a + jnp.where(pair, mm_nt(q * jnp.exp(sect(2 + 2 * li)), k * jnp.exp(sect(3 + 2 * li))), 0.0)
    yield
    o_intra = mm(a, v)
    tot = jnp.sum(g_l, axis=0, keepdims=True)
    g_end = jnp.exp(jnp.stack([tot[:, h * HEAD:(h + 1) * HEAD] for h in range(N_HEADS)], axis=0))
    return q * jnp.exp(gc), k * jnp.exp(gsuf), v, o_intra, g_end


def _vmask(chunk_idx):
    rows = chunk_idx * CHUNK + lax.broadcasted_iota(jnp.int32, (CHUNK, 1), 0)
    return jnp.where(rows >= PAD_FRONT, 1.0, 0.0)


def _row(n):
    return pl.BlockSpec((1, n), lambda i: (0, 0))


def gdn_fwd(ypre, small, proj, a_log, dt_b, on_w):
    t = ypre.shape[0]
    nc = t // REC_ROWS
    c2 = jnp.asarray(_cumsum_consts(), bf16)

    def body(y_ref, s_ref, z_ref, al_ref, dt_ref, on_ref, c2_ref, o_ref, st_ref, state):
        i = pl.program_id(0)

        @pl.when(i == 0)
        def _():
            state[...] = jnp.zeros_like(state)

        s_in = state[...]
        st_ref[0] = s_in
        s_new, out = _gdn_step(s_in, y_ref[...], s_ref[...], z_ref[...], al_ref[...], dt_ref[...], on_ref[...],
                               c2_ref[...], i * REC_CHUNKS)
        state[...] = s_new
        o_ref[...] = out

    return pl.pallas_call(
        body, grid=(nc,),
        in_specs=[pl.BlockSpec((REC_ROWS,GQKV), lambda i: (i, 0)), pl.BlockSpec((REC_ROWS,N_SMALL), lambda i: (i, 0)),
                  pl.BlockSpec((REC_ROWS,BRANCH), lambda i: (i, C_GZ // BRANCH)), _row(128), _row(128), _row(128),
                  pl.BlockSpec((2 * CHUNK, CHUNK), lambda i: (0, 0))],
        out_specs=[pl.BlockSpec((REC_ROWS,BRANCH), lambda i: (i, 0)),
                   pl.BlockSpec((1, N_HEADS, HEAD, HEAD), lambda i: (i, 0, 0, 0))],
        out_shape=[jax.ShapeDtypeStruct((t, BRANCH), f32), jax.ShapeDtypeStruct((nc, N_HEADS, HEAD, HEAD), f32)],
        scratch_shapes=[pltpu.VMEM((N_HEADS, HEAD, HEAD), f32)],
        compiler_params=_cparams(("arbitrary",)), name="gdn_fwd")(ypre, small, proj, a_log, dt_b, on_w, c2)


def gdn_bwd(ypre, small, proj, a_log, dt_b, on_w, states, d_out, dproj):
    t = ypre.shape[0]
    nc = t // REC_ROWS
    c2 = jnp.asarray(_cumsum_consts(), bf16)

    def body(y_ref, s_ref, z_ref, al_ref, dt_ref, on_ref, c2_ref, st_ref, do_ref, _,
             dy_ref, ds_ref, dz_ref, dal_ref, ddt_ref, don_ref, dstate):
        i = pl.program_id(0)

        @pl.when(i == 0)
        def _():
            dstate[...] = jnp.zeros_like(dstate)
            dal_ref[...] = jnp.zeros_like(dal_ref)
            ddt_ref[...] = jnp.zeros_like(ddt_ref)
            don_ref[...] = jnp.zeros_like(don_ref)

        c2v = c2_ref[...]
        fn = lambda s, y, sm, z, al, dt, on: _gdn_step(s, y, sm, z, al, dt, on, c2v, (nc - 1 - i) * REC_CHUNKS)
        _, vjp = jax.vjp(fn, st_ref[0], y_ref[...], s_ref[...], z_ref[...], al_ref[...], dt_ref[...], on_ref[...])
        d_s, d_y, d_sm, d_z, d_al, d_dt, d_on = vjp((dstate[...], do_ref[...]))
        dstate[...] = d_s
        dy_ref[...] = d_y
        ds_ref[...] = d_sm
        dz_ref[...] = d_z
        dal_ref[...] += d_al
        ddt_ref[...] += d_dt
        don_ref[...] += d_on

    rev = lambda i: (nc - 1 - i, 0)
    return pl.pallas_call(
        body, grid=(nc,),
        in_specs=[pl.BlockSpec((REC_ROWS,GQKV), rev), pl.BlockSpec((REC_ROWS,N_SMALL), rev),
                  pl.BlockSpec((REC_ROWS,BRANCH), lambda i: (nc - 1 - i, C_GZ // BRANCH)), _row(128), _row(128), _row(128),
                  pl.BlockSpec((2 * CHUNK, CHUNK), lambda i: (0, 0)),
                  pl.BlockSpec((1, N_HEADS, HEAD, HEAD), lambda i: (nc - 1 - i, 0, 0, 0)),
                  pl.BlockSpec((REC_ROWS,BRANCH), rev), pl.BlockSpec(memory_space=pl.ANY)],
        out_specs=[pl.BlockSpec((REC_ROWS,GQKV), rev), pl.BlockSpec((REC_ROWS,N_SMALL), rev),
                   pl.BlockSpec((REC_ROWS,BRANCH), lambda i: (nc - 1 - i, C_GZ // BRANCH)), _row(128), _row(128), _row(128)],
        out_shape=[jax.ShapeDtypeStruct((t, GQKV), f32), jax.ShapeDtypeStruct((t, N_SMALL), f32),
                   jax.ShapeDtypeStruct((t, N_MAIN), f32)] + [jax.ShapeDtypeStruct((1, 128), f32)] * 3,
        scratch_shapes=[pltpu.VMEM((N_HEADS, HEAD, HEAD), f32)], input_output_aliases={9: 2},
        compiler_params=_cparams(("arbitrary",)), name="gdn_bwd")(
            ypre, small, proj, a_log, dt_b, on_w, c2, states, d_out, dproj)


def hgrn_fwd(proj, lb, on_w):
    t = proj.shape[0]
    nc = t // REC_ROWS
    cm = jnp.asarray(_hgrn_consts(), bf16)
    ncm = cm.shape[0]

    def body(q_ref, f_ref, i_ref, z_ref, lb_ref, on_ref, cm_ref, o_ref, st_ref, state):
        i = pl.program_id(0)

        @pl.when(i == 0)
        def _():
            state[...] = jnp.zeros_like(state)

        s_in = state[...]
        st_ref[0] = s_in
        s_new, out = _hgrn_step(s_in, q_ref[...], f_ref[...], i_ref[...], z_ref[...], lb_ref[...], on_ref[...],
                                cm_ref[...], i * REC_CHUNKS)
        state[...] = s_new
        o_ref[...] = out

    sec = lambda off: pl.BlockSpec((REC_ROWS,BRANCH), functools.partial(lambda i, b: (i, b), b=off // BRANCH))
    return pl.pallas_call(
        body, grid=(nc,),
        in_specs=[sec(C_HQ), sec(C_HF), sec(C_HI), sec(C_HZ), _row(BRANCH), _row(128),
                  pl.BlockSpec((ncm, CHUNK), lambda i: (0, 0))],
        out_specs=[pl.BlockSpec((REC_ROWS,BRANCH), lambda i: (i, 0)),
                   pl.BlockSpec((1, N_HEADS, HEAD, HEAD), lambda i: (i, 0, 0, 0))],
        out_shape=[jax.ShapeDtypeStruct((t, BRANCH), f32), jax.ShapeDtypeStruct((nc, N_HEADS, HEAD, HEAD), f32)],
        scratch_shapes=[pltpu.VMEM((N_HEADS, HEAD, HEAD), f32)],
        compiler_params=_cparams(("arbitrary",)), name="hgrn_fwd")(proj, proj, proj, proj, lb, on_w, cm)


def hgrn_bwd(proj, lb, on_w, states, d_out, dproj):
    t = proj.shape[0]
    nc = t // REC_ROWS
    cm = jnp.asarray(_hgrn_consts(), bf16)
    ncm = cm.shape[0]

    def body(q_ref, f_ref, i_ref, z_ref, lb_ref, on_ref, cm_ref, st_ref, do_ref, _, dh_ref, dlb_ref, don_ref, dstate):
        i = pl.program_id(0)

        @pl.when(i == 0)
        def _():
            dstate[...] = jnp.zeros_like(dstate)
            dlb_ref[...] = jnp.zeros_like(dlb_ref)
            don_ref[...] = jnp.zeros_like(don_ref)

        cmv = cm_ref[...]
        fn = lambda s, a, b, c, d, l, on: _hgrn_step(s, a, b, c, d, l, on, cmv, (nc - 1 - i) * REC_CHUNKS)
        _, vjp = jax.vjp(fn, st_ref[0], q_ref[...], f_ref[...], i_ref[...], z_ref[...], lb_ref[...], on_ref[...])
        d_s, d_q, d_f, d_i, d_z, d_lb, d_on = vjp((dstate[...], do_ref[...]))
        dstate[...] = d_s
        dh_ref[...] = jnp.concatenate([d_q, d_f, d_i, d_z], axis=1)
        dlb_ref[...] += d_lb
        don_ref[...] += d_on

    rev = lambda i: (nc - 1 - i, 0)
    sec = lambda off: pl.BlockSpec((REC_ROWS,BRANCH), functools.partial(lambda i, b: (nc - 1 - i, b), b=off // BRANCH))
    return pl.pallas_call(
        body, grid=(nc,),
        in_specs=[sec(C_HQ), sec(C_HF), sec(C_HI), sec(C_HZ), _row(BRANCH), _row(128),
                  pl.BlockSpec((ncm, CHUNK), lambda i: (0, 0)),
                  pl.BlockSpec((1, N_HEADS, HEAD, HEAD), lambda i: (nc - 1 - i, 0, 0, 0)),
                  pl.BlockSpec((REC_ROWS,BRANCH), rev), pl.BlockSpec(memory_space=pl.ANY)],
        out_specs=[pl.BlockSpec((REC_ROWS,4 * BRANCH), lambda i: (nc - 1 - i, C_HQ // (4 * BRANCH))), _row(BRANCH), _row(128)],
        out_shape=[jax.ShapeDtypeStruct((t, N_MAIN), f32), jax.ShapeDtypeStruct((1, BRANCH), f32),
                   jax.ShapeDtypeStruct((1, 128), f32)],
        scratch_shapes=[pltpu.VMEM((N_HEADS, HEAD, HEAD), f32)], input_output_aliases={9: 0},
        compiler_params=_cparams(("arbitrary",)), name="hgrn_bwd")(proj, proj, proj, proj, lb, on_w, cm, states, d_out, dproj)


TM_MG = 320


def _const_spec(shape):
    nd = len(shape)
    return pl.BlockSpec(shape, lambda i: (0,) * nd, pipeline_mode=pl.Buffered(1))


def merge_fwd(osb, proj, ogd, ohg, wb, wo, h):
    t = h.shape[0]

    def body(osb_ref, ogd_ref, ohg_ref, zm_ref, wb_ref, wo_ref, h_ref, out_ref):
        a = osb_ref[...] * _silu(zm_ref[:, :BRANCH])
        gate = lambda b: _sigmoid(zm_ref[:, C_MIX + b * D_MODEL:C_MIX + (b + 1) * D_MODEL])
        y = (gate(0) * _dot(a, wb_ref[0]) + gate(1) * _dot(ogd_ref[...], wb_ref[1])
             + gate(2) * _dot(ohg_ref[...], wb_ref[2]))
        out_ref[...] = h_ref[...] + _dot(y, wo_ref[...])

    br = pl.BlockSpec((TM_MG, BRANCH), lambda i: (i, 0))
    return pl.pallas_call(
        body, grid=(t // TM_MG,),
        in_specs=[br, br, br, pl.BlockSpec((TM_MG, W_MERGE), lambda i: (i, 0)),
                  _const_spec((3, BRANCH, D_MODEL)), _const_spec((D_MODEL, D_MODEL)),
                  pl.BlockSpec((TM_MG, D_MODEL), lambda i: (i, 0))],
        out_specs=pl.BlockSpec((TM_MG, D_MODEL), lambda i: (i, 0)),
        out_shape=jax.ShapeDtypeStruct((t, D_MODEL), f32),
        compiler_params=_cparams(("arbitrary",)), name="merge_fwd")(osb, ogd, ohg, proj, wb, wo, h)


def merge_bwd(osb, proj, ogd, ohg, wb, wbt, wot, dh):
    t = dh.shape[0]

    def body(osb_ref, ogd_ref, ohg_ref, zm_ref, wb_ref, wbt_ref, wot_ref, dh_ref,
             dosb_ref, dogd_ref, dohg_ref, dzm_ref, dwo_ref, dwb_ref):
        i = pl.program_id(0)

        @pl.when(i == 0)
        def _():
            dwo_ref[...] = jnp.zeros_like(dwo_ref)
            dwb_ref[...] = jnp.zeros_like(dwb_ref)

        osb = osb_ref[...]
        sbz = zm_ref[:, :BRANCH]
        sgz = _sigmoid(sbz)
        sz = sbz * sgz
        branch_in = (osb * sz, ogd_ref[...], ohg_ref[...])
        dh_v = dh_ref[...]
        dy = _dot(dh_v, wot_ref[...])
        y = jnp.zeros((TM_MG, D_MODEL), f32)
        d_in = []
        for b in range(3):
            cols = slice(C_MIX + b * D_MODEL, C_MIX + (b + 1) * D_MODEL)
            p = _dot(branch_in[b], wb_ref[b])
            g = _sigmoid(zm_ref[:, cols])
            y = y + g * p
            dp = dy * g
            dzm_ref[:, cols] = dy * p * g * (1.0 - g)
            d_in.append(_dot(dp, wbt_ref[b]))
            dwb_ref[b] += _dot(branch_in[b], dp, TN)
        dwo_ref[...] += _dot(y, dh_v, TN)
        dosb_ref[...] = d_in[0] * sz
        dzm_ref[:, :BRANCH] = d_in[0] * osb * (sgz * (1.0 + sbz * (1.0 - sgz)))
        dogd_ref[...] = d_in[1]
        dohg_ref[...] = d_in[2]

    br = pl.BlockSpec((TM_MG, BRANCH), lambda i: (i, 0))
    zm = pl.BlockSpec((TM_MG, W_MERGE), lambda i: (i, 0))
    return pl.pallas_call(
        body, grid=(t // TM_MG,),
        in_specs=[br, br, br, zm,
                  _const_spec((3, BRANCH, D_MODEL)), _const_spec((3, D_MODEL, BRANCH)), _const_spec((D_MODEL, D_MODEL)),
                  pl.BlockSpec((TM_MG, D_MODEL), lambda i: (i, 0))],
        out_specs=[br, br, br, zm, _const_spec((D_MODEL, D_MODEL)), _const_spec((3, BRANCH, D_MODEL))],
        out_shape=[jax.ShapeDtypeStruct((t, BRANCH), f32)] * 3 + [jax.ShapeDtypeStruct((t, N_MAIN), f32),
                   jax.ShapeDtypeStruct((D_MODEL, D_MODEL), f32), jax.ShapeDtypeStruct((3, BRANCH, D_MODEL), f32)],
        compiler_params=_cparams(("arbitrary",)), name="merge_bwd")(osb, ogd, ohg, proj, wb, wbt, wot, dh)


def loss_head(h, target):
    t = h.shape[0]
    nb = t // SB_BLOCK

    def body(h_ref, t_ref, dh_ref, loss_ref):
        i = pl.program_id(0)

        @pl.when(i == 0)
        def _():
            loss_ref[...] = jnp.zeros_like(loss_ref)
            dh_ref[...] = jnp.zeros_like(dh_ref)

        @pl.when(i > 0)
        def _():
            err = h_ref[...] - t_ref[...]
            dh_ref[...] = err * (1.0 / D_MODEL)
            loss_ref[...] += jnp.broadcast_to(jnp.sum(err * err) * (0.5 / D_MODEL), loss_ref.shape)

    return pl.pallas_call(
        body, grid=(nb,),
        in_specs=[pl.BlockSpec((SB_BLOCK, D_MODEL), lambda i: (i, 0)),
                  pl.BlockSpec((SB_BLOCK, D_MODEL), lambda i: (jnp.maximum(i - 1, 0), 0))],
        out_specs=[pl.BlockSpec((SB_BLOCK, D_MODEL), lambda i: (i, 0)), pl.BlockSpec((1, 128), lambda i: (0, 0))],
        out_shape=[jax.ShapeDtypeStruct((t, D_MODEL), f32), jax.ShapeDtypeStruct((1, 128), f32)],
        compiler_params=_cparams(("arbitrary",)), name="loss_head")(h, target)


def adamw(parts, w, m, v, rows_per_step, name):
    r, c = w.shape
    tr = min(rows_per_step, r)
    n_parts = parts.shape[0]

    def body(p_ref, w_ref, m_ref, v_ref, g_ref, d_ref, nm_ref, nv_ref):
        g = p_ref[0].astype(f32)
        for k in range(1, n_parts):
            g = g + p_ref[k].astype(f32)
        m_new = ADAM_B1 * m_ref[...] + (1.0 - ADAM_B1) * g
        v_new = ADAM_B2 * v_ref[...] + (1.0 - ADAM_B2) * jnp.square(g)
        m_hat = m_new / (1.0 - ADAM_B1 ** ADAM_STEP)
        v_hat = v_new / (1.0 - ADAM_B2 ** ADAM_STEP)
        g_ref[...] = g
        d_ref[...] = -ADAM_LR * (m_hat / (jnp.sqrt(v_hat) + ADAM_EPS) + ADAM_WD * w_ref[...])
        nm_ref[...] = m_new
        nv_ref[...] = v_new

    blk = pl.BlockSpec((tr, c), lambda i: (i, 0))
    return pl.pallas_call(
        body, grid=(r // tr,),
        in_specs=[pl.BlockSpec((n_parts, tr, c), lambda i: (0, i, 0)), blk, blk, blk],
        out_specs=[blk] * 4, out_shape=[jax.ShapeDtypeStruct((r, c), f32)] * 4,
        compiler_params=_cparams(("arbitrary",)), name=name)(parts, w, m, v)


def _mesh_pos():
    return lax.axis_index("x"), lax.axis_index("y"), lax.axis_index("c")


def _peer(pos, k):
    x, y, c = pos
    return (1 - x if k & 4 else x, 1 - y if k & 2 else y, 1 - c if k & 1 else c)


def _lin(pos):
    return 4 * pos[0] + 2 * pos[1] + pos[2]


N_CHIPS = 4


def _chip(pos):
    return 2 * pos[0] + pos[1]


def exchange(srcs, scatter, name):
    n = len(srcs)
    shapes = [s.shape[1:] if sc else s.shape for s, sc in zip(srcs, scatter)]
    n_slots = [N_CHIPS if sc == "chips" else N_DEV for sc in scatter]

    def body(*refs):
        src_refs, dst_refs = refs[:n], refs[n:2 * n]
        send_sems, recv_sems, local_sems = refs[2 * n:]
        me = _mesh_pos()
        sends, recvs, locals_ = [], [], []
        for t in range(n):
            slot = _chip if scatter[t] == "chips" else _lin
            own = src_refs[t].at[slot(me)] if scatter[t] else src_refs[t]
            locals_.append(pltpu.make_async_copy(own, dst_refs[t].at[slot(me)], local_sems.at[t]))
            for k in range(1, N_DEV):
                if scatter[t] == "chips" and k & 1:
                    continue
                peer = _peer(me, k)
                src = src_refs[t].at[slot(peer)] if scatter[t] else src_refs[t]
                sends.append(pltpu.make_async_remote_copy(
                    src_ref=src, dst_ref=dst_refs[t].at[slot(me)], send_sem=send_sems.at[t, k - 1],
                    recv_sem=recv_sems.at[t, k - 1], device_id=peer, device_id_type=MESH))
                recvs.append(pltpu.make_async_remote_copy(
                    src_ref=src, dst_ref=dst_refs[t].at[slot(peer)], send_sem=send_sems.at[t, k - 1],
                    recv_sem=recv_sems.at[t, k - 1], device_id=peer, device_id_type=MESH))
        for cp in locals_ + sends:
            cp.start()
        for cp in sends:
            cp.wait_send()
        for cp in recvs:
            cp.wait_recv()
        for cp in locals_:
            cp.wait()

    any_spec = pl.BlockSpec(memory_space=pl.ANY)
    return pl.pallas_call(
        body, in_specs=[any_spec] * n, out_specs=[any_spec] * n,
        out_shape=[jax.ShapeDtypeStruct((ns,) + tuple(sh), s.dtype) for ns, sh, s in zip(n_slots, shapes, srcs)],
        scratch_shapes=[pltpu.SemaphoreType.DMA((n, N_DEV - 1)), pltpu.SemaphoreType.DMA((n, N_DEV - 1)),
                        pltpu.SemaphoreType.DMA((n,))],
        compiler_params=pltpu.CompilerParams(has_side_effects=True), name=name)(*srcs)


def exchange_sibling(stacks, name):
    n = len(stacks)

    def body(*refs):
        src_refs, dst_refs = refs[:n], refs[n:2 * n]
        send_sems, recv_sems = refs[2 * n:]
        x, y, c = _mesh_pos()
        copies = [pltpu.make_async_remote_copy(
            src_ref=src_refs[t].at[2 * q + (1 - c)], dst_ref=dst_refs[t].at[q], send_sem=send_sems.at[t, q],
            recv_sem=recv_sems.at[t, q], device_id=(x, y, 1 - c), device_id_type=MESH)
            for t in range(n) for q in range(N_CHIPS)]
        for cp in copies:
            cp.start()
        for cp in copies:
            cp.wait()

    any_spec = pl.BlockSpec(memory_space=pl.ANY)
    return pl.pallas_call(
        body, in_specs=[any_spec] * n, out_specs=[any_spec] * n,
        out_shape=[jax.ShapeDtypeStruct((N_CHIPS,) + tuple(s.shape[1:]), s.dtype) for s in stacks],
        scratch_shapes=[pltpu.SemaphoreType.DMA((n, N_CHIPS)), pltpu.SemaphoreType.DMA((n, N_CHIPS))],
        compiler_params=pltpu.CompilerParams(has_side_effects=True), name=name)(*stacks)


def add_partials(stack, received, core, rows_per_step, name):
    _, r, c = stack.shape
    tr = min(rows_per_step, r)

    def body(core_ref, own_ref, rcv_ref, out_ref):
        del core_ref
        out_ref[...] = (own_ref[...].astype(f32) + rcv_ref[...].astype(f32)).astype(out_ref.dtype)

    return pl.pallas_call(
        body,
        grid_spec=pltpu.PrefetchScalarGridSpec(
            num_scalar_prefetch=1, grid=(N_CHIPS, r // tr),
            in_specs=[pl.BlockSpec((None, None, tr, c), lambda q, i, core_ref: (q, core_ref[0], i, 0)),
                      pl.BlockSpec((None, tr, c), lambda q, i, core_ref: (q, i, 0))],
            out_specs=pl.BlockSpec((None, tr, c), lambda q, i, core_ref: (q, i, 0))),
        out_shape=jax.ShapeDtypeStruct((N_CHIPS, r, c), stack.dtype),
        compiler_params=_cparams(("arbitrary", "arbitrary")), name=name)(
            core, stack.reshape(N_CHIPS, 2, r, c), received)


def gather_two_level(srcs, name):
    n = len(srcs)
    n_cp = N_DEV - 1

    def body(*refs):
        src_refs, dst_refs = refs[:n], refs[n:2 * n]
        send_sems, recv_sems, local_sems = refs[2 * n:]
        x, y, c = _mesh_pos()
        me, sibling = (x, y, c), (x, y, 1 - c)
        chips = [(1 - x, y), (x, 1 - y), (1 - x, 1 - y)]

        def copy(t, k, block, to, src=None):
            slot = dst_refs[t].at[_lin(block)]
            return pltpu.make_async_remote_copy(
                src_ref=slot if src is None else src, dst_ref=slot, send_sem=send_sems.at[t, k],
                recv_sem=recv_sems.at[t, k], device_id=to, device_id_type=MESH)

        mine, first, passed = [], [], []
        for t in range(n):
            mine.append(pltpu.make_async_copy(src_refs[t], dst_refs[t].at[_lin(me)], local_sems.at[t]))
            first.append(copy(t, 0, me, sibling, src=src_refs[t]))
            first += [copy(t, 1 + j, me, (*chip, c), src=src_refs[t]) for j, chip in enumerate(chips)]
        for cp in mine + first:
            cp.start()
        for j, chip in enumerate(chips):
            for t in range(n):
                copy(t, 1 + j, (*chip, c), me).wait_recv()
                fwd = copy(t, 4 + j, (*chip, c), sibling)
                fwd.start()
                passed.append(fwd)
        for t in range(n):
            copy(t, 0, sibling, me).wait_recv()
            for j, chip in enumerate(chips):
                copy(t, 4 + j, (*chip, 1 - c), me).wait_recv()
        for cp in first + passed:
            cp.wait_send()
        for cp in mine:
            cp.wait()

    any_spec = pl.BlockSpec(memory_space=pl.ANY)
    return pl.pallas_call(
        body, in_specs=[any_spec] * n, out_specs=[any_spec] * n,
        out_shape=[jax.ShapeDtypeStruct((N_DEV,) + tuple(s.shape), s.dtype) for s in srcs],
        scratch_shapes=[pltpu.SemaphoreType.DMA((n, n_cp)), pltpu.SemaphoreType.DMA((n, n_cp)),
                        pltpu.SemaphoreType.DMA((n,))],
        compiler_params=pltpu.CompilerParams(has_side_effects=True), name=name)(*srcs)


PACK_ROWS = 104


def _pad_rows(a, rows):
    return jnp.pad(a, ((0, rows - a.shape[0]), (0, 0)))


def _pad_lanes(a):
    return jnp.pad(a, ((0, 0), (0, 128 - a.shape[1])))


def _pack(norm_w, sbq, sbk, alog, dtb, gon, lbl, hon, loss_row):
    parts = [norm_w.reshape(32, 128), _pad_rows(sbq, 8), _pad_rows(sbk, 8), _pad_rows(_pad_lanes(alog), 8),
             _pad_rows(_pad_lanes(dtb), 8), _pad_rows(gon, 8), lbl.reshape(16, 128), _pad_rows(hon, 8),
             _pad_rows(loss_row, 8)]
    return jnp.concatenate(parts, axis=0)


def _unpack(p):
    return dict(norm_w=p[0:32].reshape(DEPTH, D_MODEL), sb_q_norm=p[32:36], sb_k_norm=p[40:44],
                gdn_a_log=p[48:52, :N_HEADS], gdn_dt_bias=p[56:60, :N_HEADS], gdn_out_norm=p[64:68],
                hgrn_lb_logits=p[72:88].reshape(DEPTH, BRANCH), hgrn_out_norm=p[88:92], loss=p[96, 0])


def _lower_bounds(logits):
    p = jax.nn.softmax(logits, axis=0)
    return jnp.cumsum(p, axis=0) - p[0:1]


def _unshard_cols(g):
    nd = g.ndim
    g = jnp.moveaxis(g, 0, nd - 2)
    return g.reshape(g.shape[:-2] + (N_DEV * g.shape[-1],))


def _shard_cols(a):
    n = a.shape[-1] // N_DEV
    return jnp.moveaxis(a.reshape(a.shape[:-1] + (N_DEV, n)), -2, 0)


def kernel(x, meta_tokens, norm_w, w_in, sb_q_norm, sb_k_norm, gdn_conv_w, gdn_a_log, gdn_dt_bias, gdn_out_norm, hgrn_lb_logits, hgrn_out_norm, w_branch, w_out, loss_target, m_meta_tokens, m_norm_w, m_w_in, m_sb_q_norm, m_sb_k_norm, m_gdn_conv_w, m_gdn_a_log, m_gdn_dt_bias, m_gdn_out_norm, m_hgrn_lb_logits, m_hgrn_out_norm, m_w_branch, m_w_out, v_meta_tokens, v_norm_w, v_w_in, v_sb_q_norm, v_sb_k_norm, v_gdn_conv_w, v_gdn_a_log, v_gdn_dt_bias, v_gdn_out_norm, v_hgrn_lb_logits, v_hgrn_out_norm, v_w_branch, v_w_out):
    g_win, g_wbr, g_wout, g_meta, g_conv = gather_two_level(
        [w_in.astype(bf16), w_branch.astype(bf16), w_out.astype(bf16), meta_tokens, gdn_conv_w], "gather_weights")
    w_full = _unshard_cols(g_win)
    w_main = jnp.concatenate([w_full[..., a:b] for a, b in W_IN_ORDER], axis=-1)
    w_small = jnp.pad(w_full[..., SMALL_OFF:SMALL_OFF + 8], ((0, 0), (0, 0), (0, N_SMALL - 8)))
    wt_main = jnp.swapaxes(w_main, 1, 2)
    wt_small = jnp.swapaxes(w_small, 1, 2)
    wbr = _unshard_cols(g_wbr)
    wbr_t = jnp.swapaxes(wbr, 2, 3)
    wout = jnp.moveaxis(g_wout, 0, 1).reshape(DEPTH, D_MODEL, D_MODEL)
    wout_t = jnp.swapaxes(wout, 1, 2)
    meta = _unshard_cols(g_meta)
    conv_w = _unshard_cols(g_conv)
    lbounds, lb_vjp = jax.vjp(_lower_bounds, hgrn_lb_logits)

    h = jnp.concatenate([jnp.zeros((PAD_FRONT, D_MODEL), f32), meta, x[0]], axis=0)
    row = lambda a: a.reshape(1, -1)
    saved = []
    for l in range(DEPTH):
        proj, small, _, xnt = inproj_fwd(h, row(norm_w[l]), w_main[l], w_small[l])
        osb = sb_fwd(proj, row(sb_q_norm[l]), row(sb_k_norm[l]))
        ypre = conv_fwd(proj, conv_w[l])
        al, dtb = _pad_lanes(row(gdn_a_log[l])), _pad_lanes(row(gdn_dt_bias[l]))
        ogd, gst = gdn_fwd(ypre, small, proj, al, dtb, row(gdn_out_norm[l]))
        ohg, hst = hgrn_fwd(proj, row(lbounds[l]), row(hgrn_out_norm[l]))
        h_next = merge_fwd(osb, proj, ogd, ohg, wbr[l], wout[l], h)
        saved.append((h, proj, small, xnt, osb, ypre, ogd, gst, ohg, hst, al, dtb))
        h = h_next

    dh, loss_row = loss_head(h, loss_target[0])

    gw_main, gw_small, gw_br, gw_out, g_conv_w = [None] * DEPTH, [None] * DEPTH, [None] * DEPTH, [None] * DEPTH, [None] * DEPTH
    g_norm, g_sbq, g_sbk, g_al, g_dt, g_gon, g_lb, g_hon = ([None] * DEPTH for _ in range(8))
    for l in reversed(range(DEPTH)):
        h_l, proj, small, xnt, osb, ypre, ogd, gst, ohg, hst, al, dtb = saved[l]
        d_osb, d_ogd, d_ohg, dproj, gw_out[l], gw_br[l] = merge_bwd(osb, proj, ogd, ohg, wbr[l], wbr_t[l], wout_t[l], dh)
        dproj, g_lb[l], g_hon[l] = hgrn_bwd(proj, row(lbounds[l]), row(hgrn_out_norm[l]), hst, d_ohg, dproj)
        d_ypre, d_small, dproj, g_al[l], g_dt[l], g_gon[l] = gdn_bwd(ypre, small, proj, al, dtb, row(gdn_out_norm[l]), gst, d_ogd, dproj)
        dproj, g_conv_w[l] = conv_bwd(proj, conv_w[l], d_ypre, dproj)
        dproj, g_sbq[l], g_sbk[l] = sb_bwd(proj, row(sb_q_norm[l]), row(sb_k_norm[l]), osb, d_osb, dproj)
        gw_main[l], gw_small[l] = inproj_bwd_w(xnt, dproj, d_small)
        dh, g_norm[l] = inproj_bwd_x(dproj, d_small, wt_main[l], wt_small[l], h_l, row(norm_w[l]), dh)

    gw_main, gw_small = jnp.stack(gw_main), jnp.stack(gw_small)
    starts = np.cumsum([0] + [b - a for a, b in W_IN_ORDER])
    pieces = sorted((a, gw_main[..., int(s):int(s) + b - a]) for (a, b), s in zip(W_IN_ORDER, starts))
    pieces.append((SMALL_OFF, gw_small[..., :8]))
    gw_in = jnp.concatenate([p for _, p in sorted(pieces, key=lambda ap: ap[0])], axis=-1)
    d_lbl = lb_vjp(jnp.concatenate(g_lb, axis=0))[0]
    cat = lambda rows: jnp.concatenate(rows, axis=0)
    pack = _pack(cat(g_norm), cat(g_sbq), cat(g_sbk), cat(g_al)[:, :N_HEADS], cat(g_dt)[:, :N_HEADS], cat(g_gon),
                 d_lbl, cat(g_hon), loss_row)
    g_meta_full = dh[PAD_FRONT:FRONT]
    big = [_shard_cols(gw_in).astype(bf16).reshape(N_DEV, -1, w_in.shape[-1]),
           _shard_cols(jnp.stack(gw_br)).astype(bf16).reshape(N_DEV, -1, w_branch.shape[-1]),
           jnp.swapaxes(jnp.stack(gw_out).reshape(DEPTH, N_DEV, HEAD, D_MODEL), 0, 1).astype(bf16).reshape(N_DEV, -1, D_MODEL)]
    from_sibling = exchange_sibling(big, "exchange_sibling")
    core = lax.axis_index("c").astype(jnp.int32).reshape(1)
    chip_parts = [add_partials(s, r, core, rows, "add_partials_" + nm)
                  for s, r, rows, nm in zip(big, from_sibling, (512, 2048, 512), ("w_in", "w_branch", "w_out"))]
    r_win, r_wbr, r_wout, r_meta, r_conv, r_pack = exchange(
        chip_parts + [_shard_cols(g_meta_full), _shard_cols(jnp.stack(g_conv_w)), pack],
        ["chips", "chips", "chips", True, True, False], "exchange_grads")

    def upd(parts, w, m, v, rows, name):
        shp = w.shape
        two = (-1, shp[-1])
        outs = adamw(parts.reshape((parts.shape[0],) + w.reshape(two).shape), w.reshape(two), m.reshape(two), v.reshape(two), rows, name)
        return [o.reshape(shp) for o in outs]

    res = {}
    res["w_in"] = upd(r_win, w_in, m_w_in, v_w_in, 256, "adamw_w_in")
    res["w_branch"] = upd(r_wbr, w_branch, m_w_branch, v_w_branch, 1024, "adamw_w_branch")
    res["w_out"] = upd(r_wout, w_out, m_w_out, v_w_out, 256, "adamw_w_out")
    res["meta_tokens"] = upd(r_meta, meta_tokens, m_meta_tokens, v_meta_tokens, 16, "adamw_meta")
    res["gdn_conv_w"] = upd(r_conv, gdn_conv_w, m_gdn_conv_w, v_gdn_conv_w, 16, "adamw_conv")
    zero_row = jnp.zeros((1, 128), f32)
    w_pack = _pack(norm_w, sb_q_norm, sb_k_norm, gdn_a_log, gdn_dt_bias, gdn_out_norm, hgrn_lb_logits, hgrn_out_norm, zero_row)
    m_pack = _pack(m_norm_w, m_sb_q_norm, m_sb_k_norm, m_gdn_a_log, m_gdn_dt_bias, m_gdn_out_norm, m_hgrn_lb_logits, m_hgrn_out_norm, zero_row)
    v_pack = _pack(v_norm_w, v_sb_q_norm, v_sb_k_norm, v_gdn_a_log, v_gdn_dt_bias, v_gdn_out_norm, v_hgrn_lb_logits, v_hgrn_out_norm, zero_row)
    packed = [_unpack(o) for o in adamw(r_pack, w_pack, m_pack, v_pack, PACK_ROWS, "adamw_replicated")]
    for name in ("norm_w", "sb_q_norm", "sb_k_norm", "gdn_a_log", "gdn_dt_bias", "gdn_out_norm", "hgrn_lb_logits", "hgrn_out_norm"):
        res[name] = [p[name] for p in packed]
    loss = packed[0]["loss"]
    grad_x = dh[FRONT:][None]

    order = ["meta_tokens", "norm_w", "w_in", "sb_q_norm", "sb_k_norm", "gdn_conv_w", "gdn_a_log", "gdn_dt_bias",
             "gdn_out_norm", "hgrn_lb_logits", "hgrn_out_norm", "w_branch", "w_out"]
    return (loss, grad_x, *[res[n][0] for n in order], *[res[n][1] for n in order],
            *[res[n][2] for n in order], *[res[n][3] for n in order])
```

```python
import functools

import numpy as np
import jax
import jax.numpy as jnp
from jax import lax
from jax.experimental import pallas as pl
from jax.experimental.pallas import tpu as pltpu

f32 = jnp.float32
bf16 = jnp.bfloat16

D_MODEL = 1024
BRANCH = 512
HEAD = 128
N_HEADS = 4
CHUNK = 64
SB_BLOCK = 128
N_META = 16
FRONT = 128
PAD_FRONT = 112
EPS = 1e-6
DEPTH = 4
N_DEV = 8
N_IN = 9224
N_MAIN = 9216
N_SMALL = 128
SMALL_OFF = 4096
C_SBZ, C_MIX = 0, 512
C_GZ = 3584
C_HQ, C_HF, C_HI, C_HZ = 4096, 4608, 5120, 5632
C_GQKV = 6144
C_SBQ, C_SBK, C_SBV = 7680, 8192, 8704
W_MERGE = BRANCH + 3 * D_MODEL
W_IN_ORDER = ((1536, 2048), (6152, 9224), (3584, 4096), (4104, 6152), (2048, 3584), (0, 1536))

ADAM_LR, ADAM_B1, ADAM_B2, ADAM_EPS, ADAM_WD, ADAM_STEP = 0.001, 0.9, 0.999, 1e-08, 0.01, 10

VMEM_LIMIT = 56 * 1024 * 1024
MESH = pl.DeviceIdType.MESH

NN = ((1,), (0,))
NT = ((1,), (1,))
TN = ((0,), (0,))


def _dot(a, b, dims=NN):
    return lax.dot_general(a.astype(bf16), b.astype(bf16), (dims, ((), ())), preferred_element_type=f32)


@jax.custom_vjp
def mm(a, b):
    return _dot(a, b, NN)


mm.defvjp(lambda a, b: (_dot(a, b, NN), (a, b)),
          lambda r, g: (_dot(g, r[1], NT), _dot(r[0], g, TN)))


@jax.custom_vjp
def mm_nt(a, b):
    return _dot(a, b, NT)


mm_nt.defvjp(lambda a, b: (_dot(a, b, NT), (a, b)),
             lambda r, g: (_dot(g, r[1], NN), _dot(g, r[0], TN)))


@jax.custom_vjp
def mm_tn(a, b):
    return _dot(a, b, TN)


mm_tn.defvjp(lambda a, b: (_dot(a, b, TN), (a, b)),
             lambda r, g: (_dot(r[1], g, NT), _dot(r[0], g, NN)))


def _split2(x):
    hi = x.astype(bf16)
    lo = (x - hi.astype(f32)).astype(bf16)
    return hi, lo


def _cdot(c, x, dims):
    hi, lo = _split2(x)
    return (lax.dot_general(c, hi, (dims, ((), ())), preferred_element_type=f32)
            + lax.dot_general(c, lo, (dims, ((), ())), preferred_element_type=f32))


@jax.custom_vjp
def cmm(c, x):
    return _cdot(c, x, NN)


cmm.defvjp(lambda c, x: (_cdot(c, x, NN), c),
           lambda c, g: (jnp.zeros_like(c), _cdot(c, g, TN)))


def _sigmoid(x):
    return jax.nn.sigmoid(x)


def _silu(x):
    return x * jax.nn.sigmoid(x)


def _softplus(x):
    return jnp.maximum(x, 0.0) + jnp.log(1.0 + jnp.exp(-jnp.abs(x)))


def _rms(x, w):
    return x * lax.rsqrt(jnp.mean(x * x, axis=-1, keepdims=True) + EPS) * w


def _cparams(sem=None):
    return pltpu.CompilerParams(dimension_semantics=sem, vmem_limit_bytes=VMEM_LIMIT)


TILES_FWD = (1664, 1024)
TILES_BWD_X = (832, 1024)
TILES_BWD_W = (1664, 1024)


def _row_tile(t, want, unit):
    return max(d for d in range(unit, want + 1, unit) if t % d == 0)


def inproj_fwd(h, nw, w_main, w_small):
    t = h.shape[0]
    TM_IN, TN_IN = _row_tile(t, TILES_FWD[0], 128), TILES_FWD[1]

    def body(h_ref, nw_ref, w_ref, ws_ref, proj_ref, small_ref, xn_ref, xnt_ref):
        @pl.when(pl.program_id(1) == 0)
        def _():
            xn = _rms(h_ref[...], nw_ref[...])
            xn_ref[...] = xn.astype(bf16)
            xnt_ref[...] = jnp.transpose(xn).astype(bf16)
            small_ref[...] = _dot(xn, ws_ref[...])

        proj_ref[...] = jnp.dot(xn_ref[...], w_ref[...], preferred_element_type=f32)

    return pl.pallas_call(
        body, grid=(t // TM_IN, N_MAIN // TN_IN),
        in_specs=[pl.BlockSpec((TM_IN, D_MODEL), lambda i, j: (i, 0)),
                  pl.BlockSpec((1, D_MODEL), lambda i, j: (0, 0)),
                  pl.BlockSpec((D_MODEL, TN_IN), lambda i, j: (0, j)),
                  pl.BlockSpec((D_MODEL, N_SMALL), lambda i, j: (0, 0))],
        out_specs=[pl.BlockSpec((TM_IN, TN_IN), lambda i, j: (i, j)),
                   pl.BlockSpec((TM_IN, N_SMALL), lambda i, j: (i, 0)),
                   pl.BlockSpec((TM_IN, D_MODEL), lambda i, j: (i, 0)),
                   pl.BlockSpec((D_MODEL, TM_IN), lambda i, j: (0, i))],
        out_shape=[jax.ShapeDtypeStruct((t, N_MAIN), f32), jax.ShapeDtypeStruct((t, N_SMALL), f32),
                   jax.ShapeDtypeStruct((t, D_MODEL), bf16), jax.ShapeDtypeStruct((D_MODEL, t), bf16)],
        compiler_params=_cparams(("arbitrary", "arbitrary")), name="inproj_fwd")(h, nw, w_main, w_small)


def inproj_bwd_x(dproj, dsmall, wt_main, wt_small, h, nw, dh_out):
    t = h.shape[0]
    TM_IN, TN_IN = _row_tile(t, TILES_BWD_X[0], 64), TILES_BWD_X[1]
    nk = N_MAIN // TN_IN

    def body(dp_ref, ds_ref, wt_ref, wts_ref, h_ref, nw_ref, dho_ref, dhi_ref, dnw_ref, acc):
        i, k = pl.program_id(0), pl.program_id(1)

        @pl.when(k == 0)
        def _():
            acc[...] = _dot(ds_ref[...], wts_ref[...])

        acc[...] += _dot(dp_ref[...], wt_ref[...])

        @pl.when(k == nk - 1)
        def _():
            x = h_ref[...]
            r = lax.rsqrt(jnp.mean(x * x, axis=-1, keepdims=True) + EPS)
            xh = x * r
            dxn = acc[...]
            dxh = dxn * nw_ref[...]
            dhi_ref[...] = dho_ref[...] + r * (dxh - xh * jnp.mean(dxh * xh, axis=-1, keepdims=True))
            part = jnp.sum(dxn * xh, axis=0, keepdims=True)

            @pl.when(i == 0)
            def _():
                dnw_ref[...] = part

            @pl.when(i > 0)
            def _():
                dnw_ref[...] += part

    return pl.pallas_call(
        body, grid=(t // TM_IN, nk),
        in_specs=[pl.BlockSpec((TM_IN, TN_IN), lambda i, k: (i, k)),
                  pl.BlockSpec((TM_IN, N_SMALL), lambda i, k: (i, 0)),
                  pl.BlockSpec((TN_IN, D_MODEL), lambda i, k: (k, 0)),
                  pl.BlockSpec((N_SMALL, D_MODEL), lambda i, k: (0, 0)),
                  pl.BlockSpec((TM_IN, D_MODEL), lambda i, k: (i, 0)),
                  pl.BlockSpec((1, D_MODEL), lambda i, k: (0, 0)),
                  pl.BlockSpec((TM_IN, D_MODEL), lambda i, k: (i, 0))],
        out_specs=[pl.BlockSpec((TM_IN, D_MODEL), lambda i, k: (i, 0)),
                   pl.BlockSpec((1, D_MODEL), lambda i, k: (0, 0))],
        out_shape=[jax.ShapeDtypeStruct((t, D_MODEL), f32), jax.ShapeDtypeStruct((1, D_MODEL), f32)],
        scratch_shapes=[pltpu.VMEM((TM_IN, D_MODEL), f32)],
        compiler_params=_cparams(("arbitrary", "arbitrary")), name="inproj_bwd_x")(
            dproj, dsmall, wt_main, wt_small, h, nw, dh_out)


def inproj_bwd_w(xnt, dproj, dsmall):
    t = xnt.shape[1]
    TM_IN, TN_IN = _row_tile(t, TILES_BWD_W[0], 128), TILES_BWD_W[1]
    nt = t // TM_IN

    def body(xnt_ref, dp_ref, ds_ref, dw_ref, dws_ref, acc):
        n, s = pl.program_id(0), pl.program_id(1)
        part = _dot(xnt_ref[...], dp_ref[...])

        @pl.when(s == 0)
        def _():
            acc[...] = part

        @pl.when(s > 0)
        def _():
            acc[...] += part

        @pl.when(s == nt - 1)
        def _():
            dw_ref[...] = acc[...].astype(bf16)

        @pl.when(n == 0)
        def _():
            ps = _dot(xnt_ref[...], ds_ref[...])

            @pl.when(s == 0)
            def _():
                dws_ref[...] = ps

            @pl.when(s > 0)
            def _():
                dws_ref[...] += ps

    return pl.pallas_call(
        body, grid=(N_MAIN // TN_IN, nt),
        in_specs=[pl.BlockSpec((D_MODEL, TM_IN), lambda n, s: (0, s)),
                  pl.BlockSpec((TM_IN, TN_IN), lambda n, s: (s, n)),
                  pl.BlockSpec((TM_IN, N_SMALL), lambda n, s: (s, 0))],
        out_specs=[pl.BlockSpec((D_MODEL, TN_IN), lambda n, s: (0, n)),
                   pl.BlockSpec((D_MODEL, N_SMALL), lambda n, s: (0, 0))],
        out_shape=[jax.ShapeDtypeStruct((D_MODEL, N_MAIN), bf16), jax.ShapeDtypeStruct((D_MODEL, N_SMALL), f32)],
        scratch_shapes=[pltpu.VMEM((D_MODEL, TN_IN), f32)],
        compiler_params=_cparams(("arbitrary", "arbitrary")), name="inproj_bwd_w")(xnt, dproj, dsmall)


SB_SCALE = HEAD ** -0.5


SB_SUB = 3
SB_KS = SB_SUB * SB_BLOCK


SB_PADR = SB_KS - SB_BLOCK


def _sb_padded(t):
    return t + SB_PADR


def _sb_rows(i, d):
    start = (i + 1) * SB_BLOCK - (d + 1) * SB_KS
    return start, pl.ds(pl.multiple_of(start + SB_PADR, SB_BLOCK), SB_KS)


SB_PREP_ROWS = 5 * SB_BLOCK


def _sb_prep(k_ref, v_ref, kw_ref, kn_scr, vb_scr, nb):
    def prep(b, c):
        rows = pl.ds(pl.multiple_of(b * SB_PREP_ROWS, SB_BLOCK), SB_PREP_ROWS)
        pad_rows = pl.ds(pl.multiple_of(SB_PADR + b * SB_PREP_ROWS, SB_BLOCK), SB_PREP_ROWS)
        kn_scr[pad_rows, :] = _rms(k_ref[rows, :], kw_ref[...]).astype(bf16)
        vb_scr[pad_rows, :] = v_ref[rows, :].astype(bf16)
        return c

    lax.fori_loop(0, nb * SB_BLOCK // SB_PREP_ROWS, prep, 0)
    kn_scr[:SB_PADR, :] = jnp.zeros((SB_PADR, HEAD), bf16)
    vb_scr[:SB_PADR, :] = jnp.zeros((SB_PADR, HEAD), bf16)


def _tri_ext(cmp):
    r = lax.broadcasted_iota(jnp.int32, (SB_BLOCK, 2 * SB_BLOCK), 0)
    c = lax.broadcasted_iota(jnp.int32, (SB_BLOCK, 2 * SB_BLOCK), 1)
    return jnp.where((c >= SB_BLOCK) | cmp(r, c), 1.0, 0.0).astype(bf16)


def _sb_suffix(x, carry, tri_ext):
    hi, lo = _split2(x)
    parts = [p[:, c * SB_BLOCK:(c + 1) * SB_BLOCK] for p in (hi, lo) for c in range(SB_SUB)]
    w = jnp.dot(jnp.concatenate(parts, axis=0), tri_ext, preferred_element_type=f32)
    outs = [None] * SB_SUB
    for c in reversed(range(SB_SUB)):
        blk = w[c * SB_BLOCK:(c + 1) * SB_BLOCK] + w[(SB_SUB + c) * SB_BLOCK:(SB_SUB + c + 1) * SB_BLOCK]
        outs[c] = carry + blk[:, :SB_BLOCK]
        carry = carry + blk[:, SB_BLOCK:]
    return jnp.concatenate(outs, axis=1), carry


def _sb_scores(qn, kt, i, start, masked):
    z = lax.dot_general(qn, kt, (NT, ((), ())), preferred_element_type=f32) * SB_SCALE
    lsz = jnp.minimum(z, 0.0) - jnp.log(1.0 + jnp.exp(-jnp.abs(z)))
    lk = lsz - z
    mask = None
    if masked:
        t_idx = i * SB_BLOCK + lax.broadcasted_iota(jnp.int32, (SB_BLOCK, SB_KS), 0)
        s_idx = start + lax.broadcasted_iota(jnp.int32, (SB_BLOCK, SB_KS), 1)
        mask = (s_idx < t_idx) & (s_idx >= PAD_FRONT)
        lk = jnp.where(mask, lk, 0.0)
    return mask, lsz, lk


SB_DEAD = -104.0


def _sb_walk(i, tile, carry, first_done=False):
    n = (i + SB_SUB) // SB_SUB
    live = lambda c: jnp.max(c[1]) > SB_DEAD
    if not first_done:
        carry = tile(0, carry, True)
    _, carry = lax.while_loop(lambda st: (st[0] <= n - 2) & live(st[1]),
                              lambda st: (st[0] + 1, tile(st[0], st[1], False)), (1, carry))
    return lax.cond((n >= 2) & live(carry), lambda c: tile(n - 1, c, True), lambda c: c, carry)


SB_QB = 5
SB_QROWS = SB_QB * SB_BLOCK


def _drain(gen):
    return _interleave([gen])[0]


def sb_fwd(proj, qw, kw):
    t = proj.shape[0]
    nb = t // SB_BLOCK

    def body(q_ref, k_ref, v_ref, qw_ref, kw_ref, o_ref, kn_scr, vb_scr):
        grp = pl.program_id(1)

        @pl.when(grp == 0)
        def _():
            _sb_prep(k_ref, v_ref, kw_ref, kn_scr, vb_scr, nb)

        u_ex = _tri_ext(lambda r, c: r > c)

        def tile(i, qn, jb, carry, masked):
            acc, r_carry = carry
            start, rows = _sb_rows(i, jb)
            mask, lsz, lk = _sb_scores(qn, kn_scr[rows, :], i, start, masked)
            yield
            passed, r_carry = _sb_suffix(lk, r_carry, u_ex)
            yield
            a = jnp.exp(lsz + passed)
            if masked:
                a = jnp.where(mask, a, 0.0)
            a_hi, a_lo = _split2(a)
            both = jnp.dot(jnp.concatenate([a_hi, a_lo], axis=0), vb_scr[rows, :], preferred_element_type=f32)
            yield
            return acc + (both[:SB_BLOCK] + both[SB_BLOCK:]), r_carry

        zeros = jnp.zeros((SB_BLOCK, HEAD), f32)
        blocks = [(grp * SB_QB + b, _rms(q_ref[b * SB_BLOCK:(b + 1) * SB_BLOCK, :], qw_ref[...]).astype(bf16))
                  for b in range(SB_QB)]
        firsts = _interleave([tile(i, qn, 0, (zeros, zeros), True) for i, qn in blocks])
        for b, ((i, qn), carry) in enumerate(zip(blocks, firsts)):
            step = lambda jb, c, masked, i=i, qn=qn: _drain(tile(i, qn, jb, c, masked))
            acc, _ = _sb_walk(i, step, carry, first_done=True)
            o_ref[b * SB_BLOCK:(b + 1) * SB_BLOCK, :] = acc

    qb, cb, vb = C_SBQ // HEAD, C_SBK // HEAD, C_SBV // HEAD
    return pl.pallas_call(
        body, grid=(N_HEADS, nb // SB_QB),
        in_specs=[pl.BlockSpec((SB_QROWS, HEAD), lambda h, i: (i, qb + h)),
                  pl.BlockSpec((t, HEAD), lambda h, i: (0, cb + h)),
                  pl.BlockSpec((t, HEAD), lambda h, i: (0, vb + h)),
                  pl.BlockSpec((1, HEAD), lambda h, i: (0, 0)),
                  pl.BlockSpec((1, HEAD), lambda h, i: (0, 0))],
        out_specs=pl.BlockSpec((SB_QROWS, HEAD), lambda h, i: (i, h)),
        out_shape=jax.ShapeDtypeStruct((t, BRANCH), f32),
        scratch_shapes=[pltpu.VMEM((_sb_padded(t), HEAD), bf16), pltpu.VMEM((_sb_padded(t), HEAD), bf16)],
        compiler_params=_cparams(("arbitrary", "arbitrary")), name="sb_fwd")(proj, proj, proj, qw, kw)


def sb_bwd(proj, qw, kw, o, do, dproj):
    t = proj.shape[0]
    nb = t // SB_BLOCK

    def body(q_ref, k_ref, v_ref, qw_ref, kw_ref, o_ref, do_ref, _, dp_ref, dqw_ref, dkw_ref,
             kn_scr, vb_scr, dk_acc, dv_acc, dq_stage, dq_sems, kv_sems):
        h, grp = pl.program_id(0), pl.program_id(1)
        n_grp = nb // SB_QB
        step = h * n_grp + grp
        slot = step % 2

        def dq_copy(sl, head):
            return pltpu.make_async_copy(
                dq_stage.at[sl], dp_ref.at[pl.ds(pl.multiple_of(grp * SB_QROWS, SB_QROWS), SB_QROWS),
                                           pl.ds(C_SBQ + head * HEAD, HEAD)], dq_sems.at[sl])

        @pl.when(grp == 0)
        def _():
            _sb_prep(k_ref, v_ref, kw_ref, kn_scr, vb_scr, nb)
            dk_acc[...] = jnp.zeros_like(dk_acc)
            dv_acc[...] = jnp.zeros_like(dv_acc)

        @pl.when((grp == 0) & (h == 0))
        def _():
            dqw_ref[...] = jnp.zeros_like(dqw_ref)
            dkw_ref[...] = jnp.zeros_like(dkw_ref)

        u_ex = _tri_ext(lambda r, c: r > c)
        u_in = _tri_ext(lambda r, c: r >= c)

        def tile(i, qn, dob, d_row, jb, carry, masked):
            dq, r_carry, f_carry = carry
            start, rows = _sb_rows(i, jb)
            kt = kn_scr[rows, :]
            vt = vb_scr[rows, :]
            mask, lsz, lk = _sb_scores(qn, kt, i, start, masked)
            da = lax.dot_general(dob, vt, (NT, ((), ())), preferred_element_type=f32)
            yield
            passed, r_carry = _sb_suffix(lk, r_carry, u_ex)
            yield
            a = jnp.exp(lsz + passed)
            if masked:
                a = jnp.where(mask, a, 0.0)
            e = a * da
            e_suf, f_carry = _sb_suffix(e, f_carry, u_in)
            yield
            sg = jnp.exp(lsz)
            dz = (e * (1.0 - sg) - (d_row - e_suf) * sg) * SB_SCALE
            if masked:
                dz = jnp.where(mask, dz, 0.0)
            dzb = dz.astype(bf16)
            dq = dq + jnp.dot(dzb, kt, preferred_element_type=f32)
            dk_acc[rows, :] += lax.dot_general(dzb, qn, (TN, ((), ())), preferred_element_type=f32)
            dv_acc[rows, :] += lax.dot_general(a.astype(bf16), dob, (TN, ((), ())), preferred_element_type=f32)
            yield
            return dq, r_carry, f_carry

        zeros = jnp.zeros((SB_BLOCK, HEAD), f32)
        blocks = []
        for b in range(SB_QB):
            rows_b = slice(b * SB_BLOCK, (b + 1) * SB_BLOCK)
            q = q_ref[rows_b, :]
            rq = lax.rsqrt(jnp.mean(q * q, axis=-1, keepdims=True) + EPS)
            qh = q * rq
            dob = do_ref[rows_b, :].astype(bf16)
            d_row = jnp.sum(dob.astype(f32) * o_ref[rows_b, :], axis=-1, keepdims=True)
            blocks.append((grp * SB_QB + b, (qh * qw_ref[...]).astype(bf16), dob, d_row, rq, qh))
        firsts = _interleave([tile(i, qn, dob, d_row, 0, (zeros, zeros, zeros), True)
                              for i, qn, dob, d_row, _, _ in blocks])

        @pl.when(step >= 2)
        def _():
            dq_copy(slot, 0).wait()

        for b, ((i, qn, dob, d_row, rq, qh), carry) in enumerate(zip(blocks, firsts)):
            one = lambda jb, c, masked, i=i, qn=qn, dob=dob, d_row=d_row: _drain(tile(i, qn, dob, d_row, jb, c, masked))
            dqn, _, _ = _sb_walk(i, one, carry, first_done=True)
            gq = dqn * qw_ref[...]
            dqw_ref[...] += jnp.sum(dqn * qh, axis=0, keepdims=True)
            dq_stage[slot, b * SB_BLOCK:(b + 1) * SB_BLOCK, :] = rq * (gq - qh * jnp.mean(gq * qh, axis=-1, keepdims=True))
        for head in range(N_HEADS):
            @pl.when(h == head)
            def _(head=head):
                dq_copy(slot, head).start()

        @pl.when(grp == n_grp - 1)
        def _():
            def fin(b, c):
                rows = pl.ds(pl.multiple_of(b * SB_PREP_ROWS, SB_BLOCK), SB_PREP_ROWS)
                pad_rows = pl.ds(pl.multiple_of(SB_PADR + b * SB_PREP_ROWS, SB_BLOCK), SB_PREP_ROWS)
                kk = k_ref[rows, :]
                rk = lax.rsqrt(jnp.mean(kk * kk, axis=-1, keepdims=True) + EPS)
                kh = kk * rk
                dkn = dk_acc[pad_rows, :]
                gk = dkn * kw_ref[...]
                dk_acc[pad_rows, :] = rk * (gk - kh * jnp.mean(gk * kh, axis=-1, keepdims=True))
                dkw_ref[...] += jnp.sum(dkn * kh, axis=0, keepdims=True)
                return c

            lax.fori_loop(0, nb * SB_BLOCK // SB_PREP_ROWS, fin, 0)
            for head in range(N_HEADS):
                @pl.when(h == head)
                def _(head=head):
                    outs = [pltpu.make_async_copy(acc.at[pl.ds(SB_PADR, t)], dp_ref.at[:, pl.ds(c0 + head * HEAD, HEAD)],
                                                  kv_sems.at[n])
                            for n, (acc, c0) in enumerate(((dk_acc, C_SBK), (dv_acc, C_SBV)))]
                    for cp in outs:
                        cp.start()
                    for cp in outs:
                        cp.wait()

        @pl.when(step == N_HEADS * n_grp - 1)
        def _():
            dq_copy(1 - slot, 0).wait()
            dq_copy(slot, 0).wait()

    qb, cb, vb = C_SBQ // HEAD, C_SBK // HEAD, C_SBV // HEAD
    blk = pl.BlockSpec((SB_QROWS, HEAD), lambda h, i: (i, h))
    wsp = pl.BlockSpec((1, HEAD), lambda h, i: (0, 0))
    any_spec = pl.BlockSpec(memory_space=pl.ANY)
    return pl.pallas_call(
        body, grid=(N_HEADS, nb // SB_QB),
        in_specs=[pl.BlockSpec((SB_QROWS, HEAD), lambda h, i: (i, qb + h)),
                  pl.BlockSpec((t, HEAD), lambda h, i: (0, cb + h)),
                  pl.BlockSpec((t, HEAD), lambda h, i: (0, vb + h)), wsp, wsp, blk, blk, any_spec],
        out_specs=[any_spec, wsp, wsp],
        out_shape=[jax.ShapeDtypeStruct((t, N_MAIN), f32)] + [jax.ShapeDtypeStruct((1, HEAD), f32)] * 2,
        scratch_shapes=[pltpu.VMEM((_sb_padded(t), HEAD), bf16), pltpu.VMEM((_sb_padded(t), HEAD), bf16),
                        pltpu.VMEM((_sb_padded(t), HEAD), f32), pltpu.VMEM((_sb_padded(t), HEAD), f32),
                        pltpu.VMEM((2, SB_QROWS, HEAD), f32), pltpu.SemaphoreType.DMA((2,)), pltpu.SemaphoreType.DMA((2,))],
        input_output_aliases={7: 0},
        compiler_params=_cparams(("arbitrary", "arbitrary")), name="sb_bwd")(proj, proj, proj, qw, kw, o, do, dproj)


TM_CONV = 640
CONV_W = 4
GQKV = 3 * BRANCH


def conv_fwd(proj, cw):
    t = proj.shape[0]
    halo_blocks = TM_CONV // 8

    def body(x0_ref, x1_ref, x2_ref, p0_ref, p1_ref, p2_ref, cw_ref, y_ref):
        i = pl.program_id(0)
        for s, (x_ref, p_ref) in enumerate(((x0_ref, p0_ref), (x1_ref, p1_ref), (x2_ref, p2_ref))):
            prev = jnp.where(i > 0, p_ref[...], 0.0)
            xx = jnp.concatenate([prev, x_ref[...]], axis=0)
            cols = slice(s * BRANCH, (s + 1) * BRANCH)
            y = xx[8:] * cw_ref[CONV_W - 1:CONV_W, cols]
            for k in range(CONV_W - 1):
                y = y + pltpu.roll(xx, CONV_W - 1 - k, 0)[8:] * cw_ref[k:k + 1, cols]
            y_ref[:, cols] = y

    c0 = C_GQKV // BRANCH
    xs = [pl.BlockSpec((TM_CONV, BRANCH), functools.partial(lambda i, s: (i, c0 + s), s=s)) for s in range(3)]
    ps = [pl.BlockSpec((8, BRANCH), functools.partial(lambda i, s: (jnp.maximum(i * halo_blocks - 1, 0), c0 + s), s=s))
          for s in range(3)]
    return pl.pallas_call(
        body, grid=(t // TM_CONV,),
        in_specs=xs + ps + [pl.BlockSpec((CONV_W, GQKV), lambda i: (0, 0))],
        out_specs=pl.BlockSpec((TM_CONV, GQKV), lambda i: (i, 0)),
        out_shape=jax.ShapeDtypeStruct((t, GQKV), f32),
        compiler_params=_cparams(("arbitrary",)), name="conv_fwd")(proj, proj, proj, proj, proj, proj, cw)


def conv_bwd(proj, cw, dy, dproj):
    t = proj.shape[0]
    nt = t // TM_CONV
    halo_blocks = TM_CONV // 8

    def body(x0_ref, x1_ref, x2_ref, p0_ref, p1_ref, p2_ref, cw_ref, dy_ref, dyn_ref, _, dx_ref, dw_ref):
        i = pl.program_id(0)

        @pl.when(i == 0)
        def _():
            dw_ref[...] = jnp.zeros_like(dw_ref)

        nxt = jnp.where(i < nt - 1, dyn_ref[...], 0.0)
        dyy = jnp.concatenate([dy_ref[...], nxt], axis=0)
        n_rows = TM_CONV + 8
        dx = dyy[:TM_CONV] * cw_ref[CONV_W - 1:CONV_W, :]
        for k in range(CONV_W - 1):
            sh = CONV_W - 1 - k
            dx = dx + pltpu.roll(dyy, n_rows - sh, 0)[:TM_CONV] * cw_ref[k:k + 1, :]
        dx_ref[...] = dx
        dy_c = dy_ref[...]
        for s, (x_ref, p_ref) in enumerate(((x0_ref, p0_ref), (x1_ref, p1_ref), (x2_ref, p2_ref))):
            prev = jnp.where(i > 0, p_ref[...], 0.0)
            xx = jnp.concatenate([prev, x_ref[...]], axis=0)
            cols = slice(s * BRANCH, (s + 1) * BRANCH)
            for k in range(CONV_W):
                sh = CONV_W - 1 - k
                xs = xx[8:] if sh == 0 else pltpu.roll(xx, sh, 0)[8:]
                dw_ref[k:k + 1, cols] += jnp.sum(xs * dy_c[:, cols], axis=0, keepdims=True)

    c0 = C_GQKV // BRANCH
    xs = [pl.BlockSpec((TM_CONV, BRANCH), functools.partial(lambda i, s: (i, c0 + s), s=s)) for s in range(3)]
    ps = [pl.BlockSpec((8, BRANCH), functools.partial(lambda i, s: (jnp.maximum(i * halo_blocks - 1, 0), c0 + s), s=s))
          for s in range(3)]
    return pl.pallas_call(
        body, grid=(nt,),
        in_specs=xs + ps + [pl.BlockSpec((CONV_W, GQKV), lambda i: (0, 0)),
                            pl.BlockSpec((TM_CONV, GQKV), lambda i: (i, 0)),
                            pl.BlockSpec((8, GQKV), lambda i: (jnp.minimum((i + 1) * halo_blocks, nt * halo_blocks - 1), 0)),
                            pl.BlockSpec(memory_space=pl.ANY)],
        out_specs=[pl.BlockSpec((TM_CONV, GQKV), lambda i: (i, C_GQKV // GQKV)), pl.BlockSpec((CONV_W, GQKV), lambda i: (0, 0))],
        out_shape=[jax.ShapeDtypeStruct((t, N_MAIN), f32), jax.ShapeDtypeStruct((CONV_W, GQKV), f32)],
        input_output_aliases={9: 0},
        compiler_params=_cparams(("arbitrary",)), name="conv_bwd")(proj, proj, proj, proj, proj, proj, cw, dy, dy, dproj)


def _iota2(n, m, d):
    return lax.broadcasted_iota(jnp.int32, (n, m), d)


def _lane_pick(row_or_mat, idx):
    lanes = lax.broadcasted_iota(jnp.int32, row_or_mat.shape, row_or_mat.ndim - 1)
    return jnp.sum(jnp.where(lanes == idx, row_or_mat, 0.0), axis=-1, keepdims=True)


def _cumsum_consts():
    i = np.arange(CHUNK)
    incl = i[None, :] <= i[:, None]
    suf = i[None, :] > i[:, None]
    return np.concatenate([incl, suf], 0).astype(np.float32)


HG_LEVELS = (64, 32, 16, 8, 4, 2)


def _hgrn_consts():
    i = np.arange(CHUNK)
    rows = [i[None, :] <= i[:, None], i[None, :] > i[:, None]]
    for b in HG_LEVELS:
        ref = (i // b) * b + b // 2 - 1
        second = (i % b) >= b // 2
        rows.append((i[None, :] > ref[:, None]) & (i[None, :] <= i[:, None]) & second[:, None])
        rows.append((i[None, :] > i[:, None]) & (i[None, :] <= ref[:, None]) & (~second)[:, None])
    return np.concatenate(rows, 0).astype(np.float32)


N_SQUARINGS = 5


def _solve_chain(ms, rhss):
    xs = [r - mm(m, r) for m, r in zip(ms, rhss)]
    powers = [list(ms)]
    for _ in range(N_SQUARINGS):
        powers.append([mm(p, p) for p in powers[-1]])
        xs = [x + mm(p, x) for p, x in zip(powers[-1], xs)]
    return tuple(xs), powers


@jax.custom_vjp
def unit_lower_solve_multi(ms, rhss):
    return _solve_chain(ms, rhss)[0]


def _solve_fwd(ms, rhss):
    xs, powers = _solve_chain(ms, rhss)
    return xs, (powers, xs)


def _solve_bwd(res, gs):
    powers, xs = res
    ys = [g - mm_tn(p, g) for p, g in zip(powers[0], gs)]
    for ps in powers[1:]:
        ys = [y + mm_tn(p, y) for p, y in zip(ps, ys)]
    return tuple(-mm_nt(y, x) for y, x in zip(ys, xs)), tuple(ys)


unit_lower_solve_multi.defvjp(_solve_fwd, _solve_bwd)

HC = N_HEADS * CHUNK
BATCH0 = ((0,), (0,))


def _bdot(a, b, contract):
    return lax.dot_general(a.astype(bf16), b.astype(bf16), (contract, BATCH0), preferred_element_type=f32)


B_NN = ((2,), (1,))
B_NT = ((2,), (2,))
B_TN = ((1,), (1,))


@jax.custom_vjp
def bmm(a, b):
    return _bdot(a, b, B_NN)


bmm.defvjp(lambda a, b: (_bdot(a, b, B_NN), (a, b)),
           lambda r, g: (_bdot(g, r[1], B_NT), _bdot(r[0], g, B_TN)))


@jax.custom_vjp
def bmm_nt(a, b):
    return _bdot(a, b, B_NT)


bmm_nt.defvjp(lambda a, b: (_bdot(a, b, B_NT), (a, b)),
              lambda r, g: (_bdot(g, r[1], B_NN), _bdot(g, r[0], B_TN)))


@jax.custom_vjp
def bmm_tn(a, b):
    return _bdot(a, b, B_TN)


bmm_tn.defvjp(lambda a, b: (_bdot(a, b, B_TN), (a, b)),
              lambda r, g: (_bdot(r[1], g, B_NT), _bdot(r[0], g, B_NN)))


def _stack_heads(x):
    return jnp.concatenate([x[:, h * HEAD:(h + 1) * HEAD] for h in range(N_HEADS)], axis=0)


def _unstack_heads(x):
    return jnp.concatenate([x[h * CHUNK:(h + 1) * CHUNK] for h in range(N_HEADS)], axis=1)


REC_CHUNKS = 5
REC_ROWS = REC_CHUNKS * CHUNK


def _interleave(gens):
    n = len(gens)
    sends, results, done = [None] * n, [None] * n, [False] * n
    while not all(done):
        asks = []
        for j in range(n):
            if done[j]:
                continue
            try:
                ask = gens[j].send(sends[j])
                if ask is not None:
                    asks.append((j, ask))
            except StopIteration as stop:
                results[j], done[j] = stop.value, True
            sends[j] = None
        if asks:
            xs = unit_lower_solve_multi(tuple(a[1] for _, a in asks), tuple(a[2] for _, a in asks))
            for (j, _), x in zip(asks, xs):
                sends[j] = x
    return results


def _chunks_of(a):
    return [a[j * CHUNK:(j + 1) * CHUNK] for j in range(REC_CHUNKS)]


def _gdn_step(state, ypre, small, gz, a_log, dt_b, on_w, c2, first_chunk):
    masks = _gdn_masks()
    gens = [_gdn_intra(y, s, a_log, dt_b, c2, _vmask(first_chunk + j), masks)
            for j, (y, s) in enumerate(zip(_chunks_of(ypre), _chunks_of(small)))]
    outs = []
    for intra, z in zip(_interleave(gens), _chunks_of(gz)):
        state, o = _gdn_inter(state, intra, z, on_w)
        outs.append(o)
    return state, jnp.concatenate(outs, axis=0)


def _gdn_inter(state, intra, gz, on_w):
    u, w, aqk, q_dec, k_dec, g_last = intra
    per_head = lambda a: a.reshape(N_HEADS, CHUNK, HEAD)
    v_new = u - bmm(per_head(w), state).reshape(HC, HEAD)
    o = bmm(per_head(q_dec), state).reshape(HC, HEAD) + mm(aqk, v_new)
    new_state = state * g_last + bmm_tn(per_head(k_dec), per_head(v_new))
    return new_state, _unstack_heads(_rms(o, on_w)) * _silu(gz)


def _gdn_masks():
    r = _iota2(HC, HC, 0)
    c = _iota2(HC, HC, 1)
    same_head = (r >> (CHUNK.bit_length() - 1)) == (c >> (CHUNK.bit_length() - 1))
    return same_head & (r >= c), same_head & (r > c)


def _gdn_intra(ypre, small, a_log, dt_b, c2, vm, masks):
    causal, strict = masks
    q = _silu(_stack_heads(ypre[:, :BRANCH]))
    k = _silu(_stack_heads(ypre[:, BRANCH:2 * BRANCH]))
    v = _silu(_stack_heads(ypre[:, 2 * BRANCH:]))
    q = q * lax.rsqrt(jnp.sum(q * q, axis=-1, keepdims=True) + EPS) * (HEAD ** -0.5)
    k = k * lax.rsqrt(jnp.sum(k * k, axis=-1, keepdims=True) + EPS)
    col = lambda f: jnp.concatenate([f(h) for h in range(N_HEADS)], axis=0)
    chunk_col = lambda x: jnp.broadcast_to(x, (CHUNK, 1))
    beta = _sigmoid(col(lambda h: _lane_pick(small, h))) * col(lambda h: vm)
    g = (-jnp.exp(col(lambda h: chunk_col(_lane_pick(a_log, h))))
         * _softplus(col(lambda h: _lane_pick(small, N_HEADS + h)) + col(lambda h: chunk_col(_lane_pick(dt_b, h)))))
    g_l = _unstack_heads(jnp.broadcast_to(g, (HC, HEAD)))
    e2 = cmm(c2, g_l)
    yield
    gc = _stack_heads(e2[:CHUNK])
    gsuf = _stack_heads(e2[CHUNK:])
    g_row = jnp.broadcast_to(jnp.transpose(gc)[0:1, :], (HC, HC))
    g_col = jnp.concatenate([gc, gc], axis=1)
    dec = jnp.where(causal, jnp.exp(jnp.minimum(g_col - g_row, 0.0)), 0.0)
    kb = k * beta
    kk = mm_nt(kb, k)
    qk = mm_nt(q, k)
    yield
    m = jnp.where(strict, kk * dec, 0.0)
    x = yield ("solve", m, jnp.concatenate([v * beta, kb * jnp.exp(gc)], axis=1))
    aqk = jnp.where(causal, qk * dec, 0.0)
    tot = jnp.sum(g_l, axis=0, keepdims=True)
    g_last = jnp.exp(jnp.stack([tot[:, h * HEAD:(h + 1) * HEAD] for h in range(N_HEADS)], axis=0))
    return x[:, :HEAD], x[:, HEAD:], aqk, q * jnp.exp(gc), k * jnp.exp(gsuf), g_last


def _hgrn_step(state, hq, hf, hi, hz, lb, on_w, cm, first_chunk):
    masks = _hgrn_masks()
    gens = [_hgrn_intra(q, f, i, lb, cm, _vmask(first_chunk + j), masks)
            for j, (q, f, i) in enumerate(zip(_chunks_of(hq), _chunks_of(hf), _chunks_of(hi)))]
    outs = []
    for (q_dec, k_dec, v, o_intra, g_end), z in zip(_interleave(gens), _chunks_of(hz)):
        per_head = lambda a: a.reshape(N_HEADS, CHUNK, HEAD)
        o = bmm_nt(per_head(q_dec), state).reshape(HC, HEAD) + o_intra
        state = state * g_end + bmm_tn(per_head(v), per_head(k_dec))
        outs.append(_unstack_heads(_rms(o, on_w)) * _silu(z))
    return state, jnp.concatenate(outs, axis=0)


def _hgrn_masks():
    r = _iota2(HC, HC, 0)
    c = _iota2(HC, HC, 1)
    pairs = []
    for b in HG_LEVELS:
        sh = b.bit_length() - 1
        pairs.append(((r >> sh) == (c >> sh)) & ((r & (b - 1)) >= b // 2) & ((c & (b - 1)) < b // 2))
    return r == c, pairs


def _hgrn_intra(hq, hf, hi, lb, cm, vm, masks):
    diag, pairs = masks
    forget = lb + (1.0 - lb) * _sigmoid(hf)
    g_l = jnp.log(forget)
    e = cmm(cm, g_l)
    q = _stack_heads(_silu(hq))
    k = _stack_heads((1.0 - lb) * _sigmoid(-hf))
    v = _stack_heads(hi * vm)
    yield
    sect = lambda n: _stack_heads(e[n * CHUNK:(n + 1) * CHUNK])
    gc, gsuf = sect(0), sect(1)
    a = jnp.where(diag, jnp.sum(q * k, axis=-1, keepdims=True), 0.0)
    for li, pair in enumerate(pairs):
        a = a + jnp.where(pair, mm_nt(q * jnp.exp(sect(2 + 2 * li)), k * jnp.exp(sect(3 + 2 * li))), 0.0)
    yield
    o_intra = mm(a, v)
    tot = jnp.sum(g_l, axis=0, keepdims=True)
    g_end = jnp.exp(jnp.stack([tot[:, h * HEAD:(h + 1) * HEAD] for h in range(N_HEADS)], axis=0))
    return q * jnp.exp(gc), k * jnp.exp(gsuf), v, o_intra, g_end


def _vmask(chunk_idx):
    rows = chunk_idx * CHUNK + lax.broadcasted_iota(jnp.int32, (CHUNK, 1), 0)
    return jnp.where(rows >= PAD_FRONT, 1.0, 0.0)


def _row(n):
    return pl.BlockSpec((1, n), lambda i: (0, 0))


def gdn_fwd(ypre, small, proj, a_log, dt_b, on_w):
    t = ypre.shape[0]
    nc = t // REC_ROWS
    c2 = jnp.asarray(_cumsum_consts(), bf16)

    def body(y_ref, s_ref, z_ref, al_ref, dt_ref, on_ref, c2_ref, o_ref, st_ref, state):
        i = pl.program_id(0)

        @pl.when(i == 0)
        def _():
            state[...] = jnp.zeros_like(state)

        s_in = state[...]
        st_ref[0] = s_in
        s_new, out = _gdn_step(s_in, y_ref[...], s_ref[...], z_ref[...], al_ref[...], dt_ref[...], on_ref[...],
                               c2_ref[...], i * REC_CHUNKS)
        state[...] = s_new
        o_ref[...] = out

    return pl.pallas_call(
        body, grid=(nc,),
        in_specs=[pl.BlockSpec((REC_ROWS,GQKV), lambda i: (i, 0)), pl.BlockSpec((REC_ROWS,N_SMALL), lambda i: (i, 0)),
                  pl.BlockSpec((REC_ROWS,BRANCH), lambda i: (i, C_GZ // BRANCH)), _row(128), _row(128), _row(128),
                  pl.BlockSpec((2 * CHUNK, CHUNK), lambda i: (0, 0))],
        out_specs=[pl.BlockSpec((REC_ROWS,BRANCH), lambda i: (i, 0)),
                   pl.BlockSpec((1, N_HEADS, HEAD, HEAD), lambda i: (i, 0, 0, 0))],
        out_shape=[jax.ShapeDtypeStruct((t, BRANCH), f32), jax.ShapeDtypeStruct((nc, N_HEADS, HEAD, HEAD), f32)],
        scratch_shapes=[pltpu.VMEM((N_HEADS, HEAD, HEAD), f32)],
        compiler_params=_cparams(("arbitrary",)), name="gdn_fwd")(ypre, small, proj, a_log, dt_b, on_w, c2)


def gdn_bwd(ypre, small, proj, a_log, dt_b, on_w, states, d_out, dproj):
    t = ypre.shape[0]
    nc = t // REC_ROWS
    c2 = jnp.asarray(_cumsum_consts(), bf16)

    def body(y_ref, s_ref, z_ref, al_ref, dt_ref, on_ref, c2_ref, st_ref, do_ref, _,
             dy_ref, ds_ref, dz_ref, dal_ref, ddt_ref, don_ref, dstate):
        i = pl.program_id(0)

        @pl.when(i == 0)
        def _():
            dstate[...] = jnp.zeros_like(dstate)
            dal_ref[...] = jnp.zeros_like(dal_ref)
            ddt_ref[...] = jnp.zeros_like(ddt_ref)
            don_ref[...] = jnp.zeros_like(don_ref)

        c2v = c2_ref[...]
        fn = lambda s, y, sm, z, al, dt, on: _gdn_step(s, y, sm, z, al, dt, on, c2v, (nc - 1 - i) * REC_CHUNKS)
        _, vjp = jax.vjp(fn, st_ref[0], y_ref[...], s_ref[...], z_ref[...], al_ref[...], dt_ref[...], on_ref[...])
        d_s, d_y, d_sm, d_z, d_al, d_dt, d_on = vjp((dstate[...], do_ref[...]))
        dstate[...] = d_s
        dy_ref[...] = d_y
        ds_ref[...] = d_sm
        dz_ref[...] = d_z
        dal_ref[...] += d_al
        ddt_ref[...] += d_dt
        don_ref[...] += d_on

    rev = lambda i: (nc - 1 - i, 0)
    return pl.pallas_call(
        body, grid=(nc,),
        in_specs=[pl.BlockSpec((REC_ROWS,GQKV), rev), pl.BlockSpec((REC_ROWS,N_SMALL), rev),
                  pl.BlockSpec((REC_ROWS,BRANCH), lambda i: (nc - 1 - i, C_GZ // BRANCH)), _row(128), _row(128), _row(128),
                  pl.BlockSpec((2 * CHUNK, CHUNK), lambda i: (0, 0)),
                  pl.BlockSpec((1, N_HEADS, HEAD, HEAD), lambda i: (nc - 1 - i, 0, 0, 0)),
                  pl.BlockSpec((REC_ROWS,BRANCH), rev), pl.BlockSpec(memory_space=pl.ANY)],
        out_specs=[pl.BlockSpec((REC_ROWS,GQKV), rev), pl.BlockSpec((REC_ROWS,N_SMALL), rev),
                   pl.BlockSpec((REC_ROWS,BRANCH), lambda i: (nc - 1 - i, C_GZ // BRANCH)), _row(128), _row(128), _row(128)],
        out_shape=[jax.ShapeDtypeStruct((t, GQKV), f32), jax.ShapeDtypeStruct((t, N_SMALL), f32),
                   jax.ShapeDtypeStruct((t, N_MAIN), f32)] + [jax.ShapeDtypeStruct((1, 128), f32)] * 3,
        scratch_shapes=[pltpu.VMEM((N_HEADS, HEAD, HEAD), f32)], input_output_aliases={9: 2},
        compiler_params=_cparams(("arbitrary",)), name="gdn_bwd")(
            ypre, small, proj, a_log, dt_b, on_w, c2, states, d_out, dproj)


def hgrn_fwd(proj, lb, on_w):
    t = proj.shape[0]
    nc = t // REC_ROWS
    cm = jnp.asarray(_hgrn_consts(), bf16)
    ncm = cm.shape[0]

    def body(q_ref, f_ref, i_ref, z_ref, lb_ref, on_ref, cm_ref, o_ref, st_ref, state):
        i = pl.program_id(0)

        @pl.when(i == 0)
        def _():
            state[...] = jnp.zeros_like(state)

        s_in = state[...]
        st_ref[0] = s_in
        s_new, out = _hgrn_step(s_in, q_ref[...], f_ref[...], i_ref[...], z_ref[...], lb_ref[...], on_ref[...],
                                cm_ref[...], i * REC_CHUNKS)
        state[...] = s_new
        o_ref[...] = out

    sec = lambda off: pl.BlockSpec((REC_ROWS,BRANCH), functools.partial(lambda i, b: (i, b), b=off // BRANCH))
    return pl.pallas_call(
        body, grid=(nc,),
        in_specs=[sec(C_HQ), sec(C_HF), sec(C_HI), sec(C_HZ), _row(BRANCH), _row(128),
                  pl.BlockSpec((ncm, CHUNK), lambda i: (0, 0))],
        out_specs=[pl.BlockSpec((REC_ROWS,BRANCH), lambda i: (i, 0)),
                   pl.BlockSpec((1, N_HEADS, HEAD, HEAD), lambda i: (i, 0, 0, 0))],
        out_shape=[jax.ShapeDtypeStruct((t, BRANCH), f32), jax.ShapeDtypeStruct((nc, N_HEADS, HEAD, HEAD), f32)],
        scratch_shapes=[pltpu.VMEM((N_HEADS, HEAD, HEAD), f32)],
        compiler_params=_cparams(("arbitrary",)), name="hgrn_fwd")(proj, proj, proj, proj, lb, on_w, cm)


def hgrn_bwd(proj, lb, on_w, states, d_out, dproj):
    t = proj.shape[0]
    nc = t // REC_ROWS
    cm = jnp.asarray(_hgrn_consts(), bf16)
    ncm = cm.shape[0]

    def body(q_ref, f_ref, i_ref, z_ref, lb_ref, on_ref, cm_ref, st_ref, do_ref, _, dh_ref, dlb_ref, don_ref, dstate):
        i = pl.program_id(0)

        @pl.when(i == 0)
        def _():
            dstate[...] = jnp.zeros_like(dstate)
            dlb_ref[...] = jnp.zeros_like(dlb_ref)
            don_ref[...] = jnp.zeros_like(don_ref)

        cmv = cm_ref[...]
        fn = lambda s, a, b, c, d, l, on: _hgrn_step(s, a, b, c, d, l, on, cmv, (nc - 1 - i) * REC_CHUNKS)
        _, vjp = jax.vjp(fn, st_ref[0], q_ref[...], f_ref[...], i_ref[...], z_ref[...], lb_ref[...], on_ref[...])
        d_s, d_q, d_f, d_i, d_z, d_lb, d_on = vjp((dstate[...], do_ref[...]))
        dstate[...] = d_s
        dh_ref[...] = jnp.concatenate([d_q, d_f, d_i, d_z], axis=1)
        dlb_ref[...] += d_lb
        don_ref[...] += d_on

    rev = lambda i: (nc - 1 - i, 0)
    sec = lambda off: pl.BlockSpec((REC_ROWS,BRANCH), functools.partial(lambda i, b: (nc - 1 - i, b), b=off // BRANCH))
    return pl.pallas_call(
        body, grid=(nc,),
        in_specs=[sec(C_HQ), sec(C_HF), sec(C_HI), sec(C_HZ), _row(BRANCH), _row(128),
                  pl.BlockSpec((ncm, CHUNK), lambda i: (0, 0)),
                  pl.BlockSpec((1, N_HEADS, HEAD, HEAD), lambda i: (nc - 1 - i, 0, 0, 0)),
                  pl.BlockSpec((REC_ROWS,BRANCH), rev), pl.BlockSpec(memory_space=pl.ANY)],
        out_specs=[pl.BlockSpec((REC_ROWS,4 * BRANCH), lambda i: (nc - 1 - i, C_HQ // (4 * BRANCH))), _row(BRANCH), _row(128)],
        out_shape=[jax.ShapeDtypeStruct((t, N_MAIN), f32), jax.ShapeDtypeStruct((1, BRANCH), f32),
                   jax.ShapeDtypeStruct((1, 128), f32)],
        scratch_shapes=[pltpu.VMEM((N_HEADS, HEAD, HEAD), f32)], input_output_aliases={9: 0},
        compiler_params=_cparams(("arbitrary",)), name="hgrn_bwd")(proj, proj, proj, proj, lb, on_w, cm, states, d_out, dproj)


TM_MG = 320


def _const_spec(shape):
    nd = len(shape)
    return pl.BlockSpec(shape, lambda i: (0,) * nd, pipeline_mode=pl.Buffered(1))


def merge_fwd(osb, proj, ogd, ohg, wb, wo, h):
    t = h.shape[0]

    def body(osb_ref, ogd_ref, ohg_ref, zm_ref, wb_ref, wo_ref, h_ref, out_ref):
        a = osb_ref[...] * _silu(zm_ref[:, :BRANCH])
        gate = lambda b: _sigmoid(zm_ref[:, C_MIX + b * D_MODEL:C_MIX + (b + 1) * D_MODEL])
        y = (gate(0) * _dot(a, wb_ref[0]) + gate(1) * _dot(ogd_ref[...], wb_ref[1])
             + gate(2) * _dot(ohg_ref[...], wb_ref[2]))
        out_ref[...] = h_ref[...] + _dot(y, wo_ref[...])

    br = pl.BlockSpec((TM_MG, BRANCH), lambda i: (i, 0))
    return pl.pallas_call(
        body, grid=(t // TM_MG,),
        in_specs=[br, br, br, pl.BlockSpec((TM_MG, W_MERGE), lambda i: (i, 0)),
                  _const_spec((3, BRANCH, D_MODEL)), _const_spec((D_MODEL, D_MODEL)),
                  pl.BlockSpec((TM_MG, D_MODEL), lambda i: (i, 0))],
        out_specs=pl.BlockSpec((TM_MG, D_MODEL), lambda i: (i, 0)),
        out_shape=jax.ShapeDtypeStruct((t, D_MODEL), f32),
        compiler_params=_cparams(("arbitrary",)), name="merge_fwd")(osb, ogd, ohg, proj, wb, wo, h)


def merge_bwd(osb, proj, ogd, ohg, wb, wbt, wot, dh):
    t = dh.shape[0]

    def body(osb_ref, ogd_ref, ohg_ref, zm_ref, wb_ref, wbt_ref, wot_ref, dh_ref,
             dosb_ref, dogd_ref, dohg_ref, dzm_ref, dwo_ref, dwb_ref):
        i = pl.program_id(0)

        @pl.when(i == 0)
        def _():
            dwo_ref[...] = jnp.zeros_like(dwo_ref)
            dwb_ref[...] = jnp.zeros_like(dwb_ref)

        osb = osb_ref[...]
        sbz = zm_ref[:, :BRANCH]
        sgz = _sigmoid(sbz)
        sz = sbz * sgz
        branch_in = (osb * sz, ogd_ref[...], ohg_ref[...])
        dh_v = dh_ref[...]
        dy = _dot(dh_v, wot_ref[...])
        y = jnp.zeros((TM_MG, D_MODEL), f32)
        d_in = []
        for b in range(3):
            cols = slice(C_MIX + b * D_MODEL, C_MIX + (b + 1) * D_MODEL)
            p = _dot(branch_in[b], wb_ref[b])
            g = _sigmoid(zm_ref[:, cols])
            y = y + g * p
            dp = dy * g
            dzm_ref[:, cols] = dy * p * g * (1.0 - g)
            d_in.append(_dot(dp, wbt_ref[b]))
            dwb_ref[b] += _dot(branch_in[b], dp, TN)
        dwo_ref[...] += _dot(y, dh_v, TN)
        dosb_ref[...] = d_in[0] * sz
        dzm_ref[:, :BRANCH] = d_in[0] * osb * (sgz * (1.0 + sbz * (1.0 - sgz)))
        dogd_ref[...] = d_in[1]
        dohg_ref[...] = d_in[2]

    br = pl.BlockSpec((TM_MG, BRANCH), lambda i: (i, 0))
    zm = pl.BlockSpec((TM_MG, W_MERGE), lambda i: (i, 0))
    return pl.pallas_call(
        body, grid=(t // TM_MG,),
        in_specs=[br, br, br, zm,
                  _const_spec((3, BRANCH, D_MODEL)), _const_spec((3, D_MODEL, BRANCH)), _const_spec((D_MODEL, D_MODEL)),
                  pl.BlockSpec((TM_MG, D_MODEL), lambda i: (i, 0))],
        out_specs=[br, br, br, zm, _const_spec((D_MODEL, D_MODEL)), _const_spec((3, BRANCH, D_MODEL))],
        out_shape=[jax.ShapeDtypeStruct((t, BRANCH), f32)] * 3 + [jax.ShapeDtypeStruct((t, N_MAIN), f32),
                   jax.ShapeDtypeStruct((D_MODEL, D_MODEL), f32), jax.ShapeDtypeStruct((3, BRANCH, D_MODEL), f32)],
        compiler_params=_cparams(("arbitrary",)), name="merge_bwd")(osb, ogd, ohg, proj, wb, wbt, wot, dh)


def loss_head(h, target):
    t = h.shape[0]
    nb = t // SB_BLOCK

    def body(h_ref, t_ref, dh_ref, loss_ref):
        i = pl.program_id(0)

        @pl.when(i == 0)
        def _():
            loss_ref[...] = jnp.zeros_like(loss_ref)
            dh_ref[...] = jnp.zeros_like(dh_ref)

        @pl.when(i > 0)
        def _():
            err = h_ref[...] - t_ref[...]
            dh_ref[...] = err * (1.0 / D_MODEL)
            loss_ref[...] += jnp.broadcast_to(jnp.sum(err * err) * (0.5 / D_MODEL), loss_ref.shape)

    return pl.pallas_call(
        body, grid=(nb,),
        in_specs=[pl.BlockSpec((SB_BLOCK, D_MODEL), lambda i: (i, 0)),
                  pl.BlockSpec((SB_BLOCK, D_MODEL), lambda i: (jnp.maximum(i - 1, 0), 0))],
        out_specs=[pl.BlockSpec((SB_BLOCK, D_MODEL), lambda i: (i, 0)), pl.BlockSpec((1, 128), lambda i: (0, 0))],
        out_shape=[jax.ShapeDtypeStruct((t, D_MODEL), f32), jax.ShapeDtypeStruct((1, 128), f32)],
        compiler_params=_cparams(("arbitrary",)), name="loss_head")(h, target)


def adamw(parts, w, m, v, rows_per_step, name):
    r, c = w.shape
    tr = min(rows_per_step, r)
    n_parts = parts.shape[0]

    def body(p_ref, w_ref, m_ref, v_ref, g_ref, d_ref, nm_ref, nv_ref):
        g = p_ref[0].astype(f32)
        for k in range(1, n_parts):
            g = g + p_ref[k].astype(f32)
        m_new = ADAM_B1 * m_ref[...] + (1.0 - ADAM_B1) * g
        v_new = ADAM_B2 * v_ref[...] + (1.0 - ADAM_B2) * jnp.square(g)
        m_hat = m_new / (1.0 - ADAM_B1 ** ADAM_STEP)
        v_hat = v_new / (1.0 - ADAM_B2 ** ADAM_STEP)
        g_ref[...] = g
        d_ref[...] = -ADAM_LR * (m_hat / (jnp.sqrt(v_hat) + ADAM_EPS) + ADAM_WD * w_ref[...])
        nm_ref[...] = m_new
        nv_ref[...] = v_new

    blk = pl.BlockSpec((tr, c), lambda i: (i, 0))
    return pl.pallas_call(
        body, grid=(r // tr,),
        in_specs=[pl.BlockSpec((n_parts, tr, c), lambda i: (0, i, 0)), blk, blk, blk],
        out_specs=[blk] * 4, out_shape=[jax.ShapeDtypeStruct((r, c), f32)] * 4,
        compiler_params=_cparams(("arbitrary",)), name=name)(parts, w, m, v)


def _mesh_pos():
    return lax.axis_index("x"), lax.axis_index("y"), lax.axis_index("c")


def _peer(pos, k):
    x, y, c = pos
    return (1 - x if k & 4 else x, 1 - y if k & 2 else y, 1 - c if k & 1 else c)


def _lin(pos):
    return 4 * pos[0] + 2 * pos[1] + pos[2]


N_CHIPS = 4


def _chip(pos):
    return 2 * pos[0] + pos[1]


def exchange(srcs, scatter, name):
    n = len(srcs)
    shapes = [s.shape[1:] if sc else s.shape for s, sc in zip(srcs, scatter)]
    n_slots = [N_CHIPS if sc == "chips" else N_DEV for sc in scatter]

    def body(*refs):
        src_refs, dst_refs = refs[:n], refs[n:2 * n]
        send_sems, recv_sems, local_sems = refs[2 * n:]
        me = _mesh_pos()
        sends, recvs, locals_ = [], [], []
        for t in range(n):
            slot = _chip if scatter[t] == "chips" else _lin
            own = src_refs[t].at[slot(me)] if scatter[t] else src_refs[t]
            locals_.append(pltpu.make_async_copy(own, dst_refs[t].at[slot(me)], local_sems.at[t]))
            for k in range(1, N_DEV):
                if scatter[t] == "chips" and k & 1:
                    continue
                peer = _peer(me, k)
                src = src_refs[t].at[slot(peer)] if scatter[t] else src_refs[t]
                sends.append(pltpu.make_async_remote_copy(
                    src_ref=src, dst_ref=dst_refs[t].at[slot(me)], send_sem=send_sems.at[t, k - 1],
                    recv_sem=recv_sems.at[t, k - 1], device_id=peer, device_id_type=MESH))
                recvs.append(pltpu.make_async_remote_copy(
                    src_ref=src, dst_ref=dst_refs[t].at[slot(peer)], send_sem=send_sems.at[t, k - 1],
                    recv_sem=recv_sems.at[t, k - 1], device_id=peer, device_id_type=MESH))
        for cp in locals_ + sends:
            cp.start()
        for cp in sends:
            cp.wait_send()
        for cp in recvs:
            cp.wait_recv()
        for cp in locals_:
            cp.wait()

    any_spec = pl.BlockSpec(memory_space=pl.ANY)
    return pl.pallas_call(
        body, in_specs=[any_spec] * n, out_specs=[any_spec] * n,
        out_shape=[jax.ShapeDtypeStruct((ns,) + tuple(sh), s.dtype) for ns, sh, s in zip(n_slots, shapes, srcs)],
        scratch_shapes=[pltpu.SemaphoreType.DMA((n, N_DEV - 1)), pltpu.SemaphoreType.DMA((n, N_DEV - 1)),
                        pltpu.SemaphoreType.DMA((n,))],
        compiler_params=pltpu.CompilerParams(has_side_effects=True), name=name)(*srcs)


def exchange_sibling(stacks, name):
    n = len(stacks)

    def body(*refs):
        src_refs, dst_refs = refs[:n], refs[n:2 * n]
        send_sems, recv_sems = refs[2 * n:]
        x, y, c = _mesh_pos()
        copies = [pltpu.make_async_remote_copy(
            src_ref=src_refs[t].at[2 * q + (1 - c)], dst_ref=dst_refs[t].at[q], send_sem=send_sems.at[t, q],
            recv_sem=recv_sems.at[t, q], device_id=(x, y, 1 - c), device_id_type=MESH)
            for t in range(n) for q in range(N_CHIPS)]
        for cp in copies:
            cp.start()
        for cp in copies:
            cp.wait()

    any_spec = pl.BlockSpec(memory_space=pl.ANY)
    return pl.pallas_call(
        body, in_specs=[any_spec] * n, out_specs=[any_spec] * n,
        out_shape=[jax.ShapeDtypeStruct((N_CHIPS,) + tuple(s.shape[1:]), s.dtype) for s in stacks],
        scratch_shapes=[pltpu.SemaphoreType.DMA((n, N_CHIPS)), pltpu.SemaphoreType.DMA((n, N_CHIPS))],
        compiler_params=pltpu.CompilerParams(has_side_effects=True), name=name)(*stacks)


def add_partials(stack, received, core, rows_per_step, name):
    _, r, c = stack.shape
    tr = min(rows_per_step, r)

    def body(core_ref, own_ref, rcv_ref, out_ref):
        del core_ref
        out_ref[...] = (own_ref[...].astype(f32) + rcv_ref[...].astype(f32)).astype(out_ref.dtype)

    return pl.pallas_call(
        body,
        grid_spec=pltpu.PrefetchScalarGridSpec(
            num_scalar_prefetch=1, grid=(N_CHIPS, r // tr),
            in_specs=[pl.BlockSpec((None, None, tr, c), lambda q, i, core_ref: (q, core_ref[0], i, 0)),
                      pl.BlockSpec((None, tr, c), lambda q, i, core_ref: (q, i, 0))],
            out_specs=pl.BlockSpec((None, tr, c), lambda q, i, core_ref: (q, i, 0))),
        out_shape=jax.ShapeDtypeStruct((N_CHIPS, r, c), stack.dtype),
        compiler_params=_cparams(("arbitrary", "arbitrary")), name=name)(
            core, stack.reshape(N_CHIPS, 2, r, c), received)


def gather_two_level(srcs, name):
    n = len(srcs)
    n_cp = N_DEV - 1

    def body(*refs):
        src_refs, dst_refs = refs[:n], refs[n:2 * n]
        send_sems, recv_sems, local_sems = refs[2 * n:]
        x, y, c = _mesh_pos()
        me, sibling = (x, y, c), (x, y, 1 - c)
        chips = [(1 - x, y), (x, 1 - y), (1 - x, 1 - y)]

        def copy(t, k, block, to, src=None):
            slot = dst_refs[t].at[_lin(block)]
            return pltpu.make_async_remote_copy(
                src_ref=slot if src is None else src, dst_ref=slot, send_sem=send_sems.at[t, k],
                recv_sem=recv_sems.at[t, k], device_id=to, device_id_type=MESH)

        mine, first, passed = [], [], []
        for t in range(n):
            mine.append(pltpu.make_async_copy(src_refs[t], dst_refs[t].at[_lin(me)], local_sems.at[t]))
            first.append(copy(t, 0, me, sibling, src=src_refs[t]))
            first += [copy(t, 1 + j, me, (*chip, c), src=src_refs[t]) for j, chip in enumerate(chips)]
        for cp in mine + first:
            cp.start()
        for j, chip in enumerate(chips):
            for t in range(n):
                copy(t, 1 + j, (*chip, c), me).wait_recv()
                fwd = copy(t, 4 + j, (*chip, c), sibling)
                fwd.start()
                passed.append(fwd)
        for t in range(n):
            copy(t, 0, sibling, me).wait_recv()
            for j, chip in enumerate(chips):
                copy(t, 4 + j, (*chip, 1 - c), me).wait_recv()
        for cp in first + passed:
            cp.wait_send()
        for cp in mine:
            cp.wait()

    any_spec = pl.BlockSpec(memory_space=pl.ANY)
    return pl.pallas_call(
        body, in_specs=[any_spec] * n, out_specs=[any_spec] * n,
        out_shape=[jax.ShapeDtypeStruct((N_DEV,) + tuple(s.shape), s.dtype) for s in srcs],
        scratch_shapes=[pltpu.SemaphoreType.DMA((n, n_cp)), pltpu.SemaphoreType.DMA((n, n_cp)),
                        pltpu.SemaphoreType.DMA((n,))],
        compiler_params=pltpu.CompilerParams(has_side_effects=True), name=name)(*srcs)


PACK_ROWS = 104


def _pad_rows(a, rows):
    return jnp.pad(a, ((0, rows - a.shape[0]), (0, 0)))


def _pad_lanes(a):
    return jnp.pad(a, ((0, 0), (0, 128 - a.shape[1])))


def _pack(norm_w, sbq, sbk, alog, dtb, gon, lbl, hon, loss_row):
    parts = [norm_w.reshape(32, 128), _pad_rows(sbq, 8), _pad_rows(sbk, 8), _pad_rows(_pad_lanes(alog), 8),
             _pad_rows(_pad_lanes(dtb), 8), _pad_rows(gon, 8), lbl.reshape(16, 128), _pad_rows(hon, 8),
             _pad_rows(loss_row, 8)]
    return jnp.concatenate(parts, axis=0)


def _unpack(p):
    return dict(norm_w=p[0:32].reshape(DEPTH, D_MODEL), sb_q_norm=p[32:36], sb_k_norm=p[40:44],
                gdn_a_log=p[48:52, :N_HEADS], gdn_dt_bias=p[56:60, :N_HEADS], gdn_out_norm=p[64:68],
                hgrn_lb_logits=p[72:88].reshape(DEPTH, BRANCH), hgrn_out_norm=p[88:92], loss=p[96, 0])


def _lower_bounds(logits):
    p = jax.nn.softmax(logits, axis=0)
    return jnp.cumsum(p, axis=0) - p[0:1]


def _unshard_cols(g):
    nd = g.ndim
    g = jnp.moveaxis(g, 0, nd - 2)
    return g.reshape(g.shape[:-2] + (N_DEV * g.shape[-1],))


def _shard_cols(a):
    n = a.shape[-1] // N_DEV
    return jnp.moveaxis(a.reshape(a.shape[:-1] + (N_DEV, n)), -2, 0)


def kernel(x, meta_tokens, norm_w, w_in, sb_q_norm, sb_k_norm, gdn_conv_w, gdn_a_log, gdn_dt_bias, gdn_out_norm, hgrn_lb_logits, hgrn_out_norm, w_branch, w_out, loss_target, m_meta_tokens, m_norm_w, m_w_in, m_sb_q_norm, m_sb_k_norm, m_gdn_conv_w, m_gdn_a_log, m_gdn_dt_bias, m_gdn_out_norm, m_hgrn_lb_logits, m_hgrn_out_norm, m_w_branch, m_w_out, v_meta_tokens, v_norm_w, v_w_in, v_sb_q_norm, v_sb_k_norm, v_gdn_conv_w, v_gdn_a_log, v_gdn_dt_bias, v_gdn_out_norm, v_hgrn_lb_logits, v_hgrn_out_norm, v_w_branch, v_w_out):
    g_win, g_wbr, g_wout, g_meta, g_conv = gather_two_level(
        [w_in.astype(bf16), w_branch.astype(bf16), w_out.astype(bf16), meta_tokens, gdn_conv_w], "gather_weights")
    w_full = _unshard_cols(g_win)
    w_main = jnp.concatenate([w_full[..., a:b] for a, b in W_IN_ORDER], axis=-1)
    w_small = jnp.pad(w_full[..., SMALL_OFF:SMALL_OFF + 8], ((0, 0), (0, 0), (0, N_SMALL - 8)))
    wt_main = jnp.swapaxes(w_main, 1, 2)
    wt_small = jnp.swapaxes(w_small, 1, 2)
    wbr = _unshard_cols(g_wbr)
    wbr_t = jnp.swapaxes(wbr, 2, 3)
    wout = jnp.moveaxis(g_wout, 0, 1).reshape(DEPTH, D_MODEL, D_MODEL)
    wout_t = jnp.swapaxes(wout, 1, 2)
    meta = _unshard_cols(g_meta)
    conv_w = _unshard_cols(g_conv)
    lbounds, lb_vjp = jax.vjp(_lower_bounds, hgrn_lb_logits)

    h = jnp.concatenate([jnp.zeros((PAD_FRONT, D_MODEL), f32), meta, x[0]], axis=0)
    row = lambda a: a.reshape(1, -1)
    saved = []
    for l in range(DEPTH):
        proj, small, _, xnt = inproj_fwd(h, row(norm_w[l]), w_main[l], w_small[l])
        osb = sb_fwd(proj, row(sb_q_norm[l]), row(sb_k_norm[l]))
        ypre = conv_fwd(proj, conv_w[l])
        al, dtb = _pad_lanes(row(gdn_a_log[l])), _pad_lanes(row(gdn_dt_bias[l]))
        ogd, gst = gdn_fwd(ypre, small, proj, al, dtb, row(gdn_out_norm[l]))
        ohg, hst = hgrn_fwd(proj, row(lbounds[l]), row(hgrn_out_norm[l]))
        h_next = merge_fwd(osb, proj, ogd, ohg, wbr[l], wout[l], h)
        saved.append((h, proj, small, xnt, osb, ypre, ogd, gst, ohg, hst, al, dtb))
        h = h_next

    dh, loss_row = loss_head(h, loss_target[0])

    gw_main, gw_small, gw_br, gw_out, g_conv_w = [None] * DEPTH, [None] * DEPTH, [None] * DEPTH, [None] * DEPTH, [None] * DEPTH
    g_norm, g_sbq, g_sbk, g_al, g_dt, g_gon, g_lb, g_hon = ([None] * DEPTH for _ in range(8))
    for l in reversed(range(DEPTH)):
        h_l, proj, small, xnt, osb, ypre, ogd, gst, ohg, hst, al, dtb = saved[l]
        d_osb, d_ogd, d_ohg, dproj, gw_out[l], gw_br[l] = merge_bwd(osb, proj, ogd, ohg, wbr[l], wbr_t[l], wout_t[l], dh)
        dproj, g_lb[l], g_hon[l] = hgrn_bwd(proj, row(lbounds[l]), row(hgrn_out_norm[l]), hst, d_ohg, dproj)
        d_ypre, d_small, dproj, g_al[l], g_dt[l], g_gon[l] = gdn_bwd(ypre, small, proj, al, dtb, row(gdn_out_norm[l]), gst, d_ogd, dproj)
        dproj, g_conv_w[l] = conv_bwd(proj, conv_w[l], d_ypre, dproj)
        dproj, g_sbq[l], g_sbk[l] = sb_bwd(proj, row(sb_q_norm[l]), row(sb_k_norm[l]), osb, d_osb, dproj)
        gw_main[l], gw_small[l] = inproj_bwd_w(xnt, dproj, d_small)
        dh, g_norm[l] = inproj_bwd_x(dproj, d_small, wt_main[l], wt_small[l], h_l, row(norm_w[l]), dh)

    gw_main, gw_small = jnp.stack(gw_main), jnp.stack(gw_small)
    starts = np.cumsum([0] + [b - a for a, b in W_IN_ORDER])
    pieces = sorted((a, gw_main[..., int(s):int(s) + b - a]) for (a, b), s in zip(W_IN_ORDER, starts))
    pieces.append((SMALL_OFF, gw_small[..., :8].astype(bf16)))
    gw_in = jnp.concatenate([p for _, p in sorted(pieces, key=lambda ap: ap[0])], axis=-1)
    d_lbl = lb_vjp(jnp.concatenate(g_lb, axis=0))[0]
    cat = lambda rows: jnp.concatenate(rows, axis=0)
    pack = _pack(cat(g_norm), cat(g_sbq), cat(g_sbk), cat(g_al)[:, :N_HEADS], cat(g_dt)[:, :N_HEADS], cat(g_gon),
                 d_lbl, cat(g_hon), loss_row)
    g_meta_full = dh[PAD_FRONT:FRONT]
    big = [_shard_cols(gw_in).astype(bf16).reshape(N_DEV, -1, w_in.shape[-1]),
           _shard_cols(jnp.stack(gw_br)).astype(bf16).reshape(N_DEV, -1, w_branch.shape[-1]),
           jnp.swapaxes(jnp.stack(gw_out).reshape(DEPTH, N_DEV, HEAD, D_MODEL), 0, 1).astype(bf16).reshape(N_DEV, -1, D_MODEL)]
    from_sibling = exchange_sibling(big, "exchange_sibling")
    core = lax.axis_index("c").astype(jnp.int32).reshape(1)
    chip_parts = [add_partials(s, r, core, rows, "add_partials_" + nm)
                  for s, r, rows, nm in zip(big, from_sibling, (512, 2048, 512), ("w_in", "w_branch", "w_out"))]
    r_win, r_wbr, r_wout, r_meta, r_conv, r_pack = exchange(
        chip_parts + [_shard_cols(g_meta_full), _shard_cols(jnp.stack(g_conv_w)), pack],
        ["chips", "chips", "chips", True, True, False], "exchange_grads")

    def upd(parts, w, m, v, rows, name):
        shp = w.shape
        two = (-1, shp[-1])
        outs = adamw(parts.reshape((parts.shape[0],) + w.reshape(two).shape), w.reshape(two), m.reshape(two), v.reshape(two), rows, name)
        return [o.reshape(shp) for o in outs]

    res = {}
    res["w_in"] = upd(r_win, w_in, m_w_in, v_w_in, 256, "adamw_w_in")
    res["w_branch"] = upd(r_wbr, w_branch, m_w_branch, v_w_branch, 1024, "adamw_w_branch")
    res["w_out"] = upd(r_wout, w_out, m_w_out, v_w_out, 256, "adamw_w_out")
    res["meta_tokens"] = upd(r_meta, meta_tokens, m_meta_tokens, v_meta_tokens, 16, "adamw_meta")
    res["gdn_conv_w"] = upd(r_conv, gdn_conv_w, m_gdn_conv_w, v_gdn_conv_w, 16, "adamw_conv")
    zero_row = jnp.zeros((1, 128), f32)
    w_pack = _pack(norm_w, sb_q_norm, sb_k_norm, gdn_a_log, gdn_dt_bias, gdn_out_norm, hgrn_lb_logits, hgrn_out_norm, zero_row)
    m_pack = _pack(m_norm_w, m_sb_q_norm, m_sb_k_norm, m_gdn_a_log, m_gdn_dt_bias, m_gdn_out_norm, m_hgrn_lb_logits, m_hgrn_out_norm, zero_row)
    v_pack = _pack(v_norm_w, v_sb_q_norm, v_sb_k_norm, v_gdn_a_log, v_gdn_dt_bias, v_gdn_out_norm, v_hgrn_lb_logits, v_hgrn_out_norm, zero_row)
    packed = [_unpack(o) for o in adamw(r_pack, w_pack, m_pack, v_pack, PACK_ROWS, "adamw_replicated")]
    for name in ("norm_w", "sb_q_norm", "sb_k_norm", "gdn_a_log", "gdn_dt_bias", "gdn_out_norm", "hgrn_lb_logits", "hgrn_out_norm"):
        res[name] = [p[name] for p in packed]
    loss = packed[0]["loss"]
    grad_x = dh[FRONT:][None]

    order = ["meta_tokens", "norm_w", "w_in", "sb_q_norm", "sb_k_norm", "gdn_conv_w", "gdn_a_log", "gdn_dt_bias",
             "gdn_out_norm", "hgrn_lb_logits", "hgrn_out_norm", "w_branch", "w_out"]
    return (loss, grad_x, *[res[n][0] for n in order], *[res[n][1] for n in order],
            *[res[n][2] for n in order], *[res[n][3] for n in order])
```

```python
import functools

import numpy as np
import jax
import jax.numpy as jnp
from jax import lax
from jax.experimental import pallas as pl
from jax.experimental.pallas import tpu as pltpu

f32 = jnp.float32
bf16 = jnp.bfloat16

D_MODEL = 1024
BRANCH = 512
HEAD = 128
N_HEADS = 4
CHUNK = 64
SB_BLOCK = 128
N_META = 16
FRONT = 128
PAD_FRONT = 112
EPS = 1e-6
DEPTH = 4
N_DEV = 8
N_IN = 9224
N_MAIN = 9216
N_SMALL = 128
SMALL_OFF = 4096
C_SBZ, C_MIX = 0, 512
C_GZ = 3584
C_HQ, C_HF, C_HI, C_HZ = 4096, 4608, 5120, 5632
C_GQKV = 6144
C_SBQ, C_SBK, C_SBV = 7680, 8192, 8704
W_MERGE = BRANCH + 3 * D_MODEL
W_IN_ORDER = ((1536, 2048), (6152, 9224), (3584, 4096), (4104, 6152), (2048, 3584), (0, 1536))

ADAM_LR, ADAM_B1, ADAM_B2, ADAM_EPS, ADAM_WD, ADAM_STEP = 0.001, 0.9, 0.999, 1e-08, 0.01, 10

VMEM_LIMIT = 56 * 1024 * 1024
MESH = pl.DeviceIdType.MESH

NN = ((1,), (0,))
NT = ((1,), (1,))
TN = ((0,), (0,))


def _dot(a, b, dims=NN):
    return lax.dot_general(a.astype(bf16), b.astype(bf16), (dims, ((), ())), preferred_element_type=f32)


@jax.custom_vjp
def mm(a, b):
    return _dot(a, b, NN)


mm.defvjp(lambda a, b: (_dot(a, b, NN), (a, b)),
          lambda r, g: (_dot(g, r[1], NT), _dot(r[0], g, TN)))


@jax.custom_vjp
def mm_nt(a, b):
    return _dot(a, b, NT)


mm_nt.defvjp(lambda a, b: (_dot(a, b, NT), (a, b)),
             lambda r, g: (_dot(g, r[1], NN), _dot(g, r[0], TN)))


@jax.custom_vjp
def mm_tn(a, b):
    return _dot(a, b, TN)


mm_tn.defvjp(lambda a, b: (_dot(a, b, TN), (a, b)),
             lambda r, g: (_dot(r[1], g, NT), _dot(r[0], g, NN)))


def _split2(x):
    hi = x.astype(bf16)
    lo = (x - hi.astype(f32)).astype(bf16)
    return hi, lo


def _cdot(c, x, dims):
    hi, lo = _split2(x)
    return (lax.dot_general(c, hi, (dims, ((), ())), preferred_element_type=f32)
            + lax.dot_general(c, lo, (dims, ((), ())), preferred_element_type=f32))


@jax.custom_vjp
def cmm(c, x):
    return _cdot(c, x, NN)


cmm.defvjp(lambda c, x: (_cdot(c, x, NN), c),
           lambda c, g: (jnp.zeros_like(c), _cdot(c, g, TN)))


def _sigmoid(x):
    return jax.nn.sigmoid(x)


def _silu(x):
    return x * jax.nn.sigmoid(x)


def _softplus(x):
    return jnp.maximum(x, 0.0) + jnp.log(1.0 + jnp.exp(-jnp.abs(x)))


def _rms(x, w):
    return x * lax.rsqrt(jnp.mean(x * x, axis=-1, keepdims=True) + EPS) * w


def _cparams(sem=None):
    return pltpu.CompilerParams(dimension_semantics=sem, vmem_limit_bytes=VMEM_LIMIT)


TILES_FWD = (1664, 1024)
TILES_BWD_X = (832, 1024)
TILES_BWD_W = (1664, 1024)


def _row_tile(t, want, unit):
    return max(d for d in range(unit, want + 1, unit) if t % d == 0)


def inproj_fwd(h, nw, w_main, w_small):
    t = h.shape[0]
    TM_IN, TN_IN = _row_tile(t, TILES_FWD[0], 128), TILES_FWD[1]

    def body(h_ref, nw_ref, w_ref, ws_ref, proj_ref, small_ref, xn_ref, xnt_ref):
        @pl.when(pl.program_id(1) == 0)
        def _():
            xn = _rms(h_ref[...], nw_ref[...])
            xn_ref[...] = xn.astype(bf16)
            xnt_ref[...] = jnp.transpose(xn).astype(bf16)
            small_ref[...] = _dot(xn, ws_ref[...])

        proj_ref[...] = jnp.dot(xn_ref[...], w_ref[...], preferred_element_type=f32)

    return pl.pallas_call(
        body, grid=(t // TM_IN, N_MAIN // TN_IN),
        in_specs=[pl.BlockSpec((TM_IN, D_MODEL), lambda i, j: (i, 0)),
                  pl.BlockSpec((1, D_MODEL), lambda i, j: (0, 0)),
                  pl.BlockSpec((D_MODEL, TN_IN), lambda i, j: (0, j)),
                  pl.BlockSpec((D_MODEL, N_SMALL), lambda i, j: (0, 0))],
        out_specs=[pl.BlockSpec((TM_IN, TN_IN), lambda i, j: (i, j)),
                   pl.BlockSpec((TM_IN, N_SMALL), lambda i, j: (i, 0)),
                   pl.BlockSpec((TM_IN, D_MODEL), lambda i, j: (i, 0)),
                   pl.BlockSpec((D_MODEL, TM_IN), lambda i, j: (0, i))],
        out_shape=[jax.ShapeDtypeStruct((t, N_MAIN), f32), jax.ShapeDtypeStruct((t, N_SMALL), f32),
                   jax.ShapeDtypeStruct((t, D_MODEL), bf16), jax.ShapeDtypeStruct((D_MODEL, t), bf16)],
        compiler_params=_cparams(("arbitrary", "arbitrary")), name="inproj_fwd")(h, nw, w_main, w_small)


def inproj_bwd_x(dproj, dsmall, wt_main, wt_small, h, nw, dh_out):
    t = h.shape[0]
    TM_IN, TN_IN = _row_tile(t, TILES_BWD_X[0], 64), TILES_BWD_X[1]
    nk = N_MAIN // TN_IN

    def body(dp_ref, ds_ref, wt_ref, wts_ref, h_ref, nw_ref, dho_ref, dhi_ref, dnw_ref, acc):
        i, k = pl.program_id(0), pl.program_id(1)

        @pl.when(k == 0)
        def _():
            acc[...] = _dot(ds_ref[...], wts_ref[...])

        acc[...] += _dot(dp_ref[...], wt_ref[...])

        @pl.when(k == nk - 1)
        def _():
            x = h_ref[...]
            r = lax.rsqrt(jnp.mean(x * x, axis=-1, keepdims=True) + EPS)
            xh = x * r
            dxn = acc[...]
            dxh = dxn * nw_ref[...]
            dhi_ref[...] = dho_ref[...] + r * (dxh - xh * jnp.mean(dxh * xh, axis=-1, keepdims=True))
            part = jnp.sum(dxn * xh, axis=0, keepdims=True)

            @pl.when(i == 0)
            def _():
                dnw_ref[...] = part

            @pl.when(i > 0)
            def _():
                dnw_ref[...] += part

    return pl.pallas_call(
        body, grid=(t // TM_IN, nk),
        in_specs=[pl.BlockSpec((TM_IN, TN_IN), lambda i, k: (i, k)),
                  pl.BlockSpec((TM_IN, N_SMALL), lambda i, k: (i, 0)),
                  pl.BlockSpec((TN_IN, D_MODEL), lambda i, k: (k, 0)),
                  pl.BlockSpec((N_SMALL, D_MODEL), lambda i, k: (0, 0)),
                  pl.BlockSpec((TM_IN, D_MODEL), lambda i, k: (i, 0)),
                  pl.BlockSpec((1, D_MODEL), lambda i, k: (0, 0)),
                  pl.BlockSpec((TM_IN, D_MODEL), lambda i, k: (i, 0))],
        out_specs=[pl.BlockSpec((TM_IN, D_MODEL), lambda i, k: (i, 0)),
                   pl.BlockSpec((1, D_MODEL), lambda i, k: (0, 0))],
        out_shape=[jax.ShapeDtypeStruct((t, D_MODEL), f32), jax.ShapeDtypeStruct((1, D_MODEL), f32)],
        scratch_shapes=[pltpu.VMEM((TM_IN, D_MODEL), f32)],
        compiler_params=_cparams(("arbitrary", "arbitrary")), name="inproj_bwd_x")(
            dproj, dsmall, wt_main, wt_small, h, nw, dh_out)


def inproj_bwd_w(xnt, dproj, dsmall):
    t = xnt.shape[1]
    TM_IN, TN_IN = _row_tile(t, TILES_BWD_W[0], 128), TILES_BWD_W[1]
    nt = t // TM_IN

    def body(xnt_ref, dp_ref, ds_ref, dw_ref, dws_ref, acc):
        n, s = pl.program_id(0), pl.program_id(1)
        part = _dot(xnt_ref[...], dp_ref[...])

        @pl.when(s == 0)
        def _():
            acc[...] = part

        @pl.when(s > 0)
        def _():
            acc[...] += part

        @pl.when(s == nt - 1)
        def _():
            dw_ref[...] = acc[...].astype(bf16)

        @pl.when(n == 0)
        def _():
            ps = _dot(xnt_ref[...], ds_ref[...])

            @pl.when(s == 0)
            def _():
                dws_ref[...] = ps

            @pl.when(s > 0)
            def _():
                dws_ref[...] += ps

    return pl.pallas_call(
        body, grid=(N_MAIN // TN_IN, nt),
        in_specs=[pl.BlockSpec((D_MODEL, TM_IN), lambda n, s: (0, s)),
                  pl.BlockSpec((TM_IN, TN_IN), lambda n, s: (s, n)),
                  pl.BlockSpec((TM_IN, N_SMALL), lambda n, s: (s, 0))],
        out_specs=[pl.BlockSpec((D_MODEL, TN_IN), lambda n, s: (0, n)),
                   pl.BlockSpec((D_MODEL, N_SMALL), lambda n, s: (0, 0))],
        out_shape=[jax.ShapeDtypeStruct((D_MODEL, N_MAIN), bf16), jax.ShapeDtypeStruct((D_MODEL, N_SMALL), f32)],
        scratch_shapes=[pltpu.VMEM((D_MODEL, TN_IN), f32)],
        compiler_params=_cparams(("arbitrary", "arbitrary")), name="inproj_bwd_w")(xnt, dproj, dsmall)


SB_SCALE = HEAD ** -0.5


SB_SUB = 3
SB_KS = SB_SUB * SB_BLOCK


SB_PADR = SB_KS - SB_BLOCK


def _sb_padded(t):
    return t + SB_PADR


def _sb_rows(i, d):
    start = (i + 1) * SB_BLOCK - (d + 1) * SB_KS
    return start, pl.ds(pl.multiple_of(start + SB_PADR, SB_BLOCK), SB_KS)


SB_PREP_ROWS = 5 * SB_BLOCK


def _sb_prep(k_ref, v_ref, kw_ref, kn_scr, vb_scr, nb):
    def prep(b, c):
        rows = pl.ds(pl.multiple_of(b * SB_PREP_ROWS, SB_BLOCK), SB_PREP_ROWS)
        pad_rows = pl.ds(pl.multiple_of(SB_PADR + b * SB_PREP_ROWS, SB_BLOCK), SB_PREP_ROWS)
        kn_scr[pad_rows, :] = _rms(k_ref[rows, :], kw_ref[...]).astype(bf16)
        vb_scr[pad_rows, :] = v_ref[rows, :].astype(bf16)
        return c

    lax.fori_loop(0, nb * SB_BLOCK // SB_PREP_ROWS, prep, 0)
    kn_scr[:SB_PADR, :] = jnp.zeros((SB_PADR, HEAD), bf16)
    vb_scr[:SB_PADR, :] = jnp.zeros((SB_PADR, HEAD), bf16)


def _tri_ext(cmp):
    r = lax.broadcasted_iota(jnp.int32, (SB_BLOCK, 2 * SB_BLOCK), 0)
    c = lax.broadcasted_iota(jnp.int32, (SB_BLOCK, 2 * SB_BLOCK), 1)
    return jnp.where((c >= SB_BLOCK) | cmp(r, c), 1.0, 0.0).astype(bf16)


def _sb_suffix(x, carry, tri_ext):
    hi, lo = _split2(x)
    parts = [p[:, c * SB_BLOCK:(c + 1) * SB_BLOCK] for p in (hi, lo) for c in range(SB_SUB)]
    w = jnp.dot(jnp.concatenate(parts, axis=0), tri_ext, preferred_element_type=f32)
    outs = [None] * SB_SUB
    for c in reversed(range(SB_SUB)):
        blk = w[c * SB_BLOCK:(c + 1) * SB_BLOCK] + w[(SB_SUB + c) * SB_BLOCK:(SB_SUB + c + 1) * SB_BLOCK]
        outs[c] = carry + blk[:, :SB_BLOCK]
        carry = carry + blk[:, SB_BLOCK:]
    return jnp.concatenate(outs, axis=1), carry


def _sb_scores(qn, kt, i, start, masked):
    z = lax.dot_general(qn, kt, (NT, ((), ())), preferred_element_type=f32) * SB_SCALE
    lsz = jnp.minimum(z, 0.0) - jnp.log(1.0 + jnp.exp(-jnp.abs(z)))
    lk = lsz - z
    mask = None
    if masked:
        t_idx = i * SB_BLOCK + lax.broadcasted_iota(jnp.int32, (SB_BLOCK, SB_KS), 0)
        s_idx = start + lax.broadcasted_iota(jnp.int32, (SB_BLOCK, SB_KS), 1)
        mask = (s_idx < t_idx) & (s_idx >= PAD_FRONT)
        lk = jnp.where(mask, lk, 0.0)
    return mask, lsz, lk


SB_DEAD = -104.0


def _sb_walk(i, tile, carry, first_done=False):
    n = (i + SB_SUB) // SB_SUB
    live = lambda c: jnp.max(c[1]) > SB_DEAD
    if not first_done:
        carry = tile(0, carry, True)
    _, carry = lax.while_loop(lambda st: (st[0] <= n - 2) & live(st[1]),
                              lambda st: (st[0] + 1, tile(st[0], st[1], False)), (1, carry))
    return lax.cond((n >= 2) & live(carry), lambda c: tile(n - 1, c, True), lambda c: c, carry)


SB_QB = 5
SB_QROWS = SB_QB * SB_BLOCK


def _drain(gen):
    return _interleave([gen])[0]


def sb_fwd(proj, qw, kw):
    t = proj.shape[0]
    nb = t // SB_BLOCK

    def body(q_ref, k_ref, v_ref, qw_ref, kw_ref, o_ref, kn_scr, vb_scr):
        grp = pl.program_id(1)

        @pl.when(grp == 0)
        def _():
            _sb_prep(k_ref, v_ref, kw_ref, kn_scr, vb_scr, nb)

        u_ex = _tri_ext(lambda r, c: r > c)

        def tile(i, qn, jb, carry, masked):
            acc, r_carry = carry
            start, rows = _sb_rows(i, jb)
            mask, lsz, lk = _sb_scores(qn, kn_scr[rows, :], i, start, masked)
            yield
            passed, r_carry = _sb_suffix(lk, r_carry, u_ex)
            yield
            a = jnp.exp(lsz + passed)
            if masked:
                a = jnp.where(mask, a, 0.0)
            a_hi, a_lo = _split2(a)
            both = jnp.dot(jnp.concatenate([a_hi, a_lo], axis=0), vb_scr[rows, :], preferred_element_type=f32)
            yield
            return acc + (both[:SB_BLOCK] + both[SB_BLOCK:]), r_carry

        zeros = jnp.zeros((SB_BLOCK, HEAD), f32)
        blocks = [(grp * SB_QB + b, _rms(q_ref[b * SB_BLOCK:(b + 1) * SB_BLOCK, :], qw_ref[...]).astype(bf16))
                  for b in range(SB_QB)]
        firsts = _interleave([tile(i, qn, 0, (zeros, zeros), True) for i, qn in blocks])
        for b, ((i, qn), carry) in enumerate(zip(blocks, firsts)):
            step = lambda jb, c, masked, i=i, qn=qn: _drain(tile(i, qn, jb, c, masked))
            acc, _ = _sb_walk(i, step, carry, first_done=True)
            o_ref[b * SB_BLOCK:(b + 1) * SB_BLOCK, :] = acc

    qb, cb, vb = C_SBQ // HEAD, C_SBK // HEAD, C_SBV // HEAD
    return pl.pallas_call(
        body, grid=(N_HEADS, nb // SB_QB),
        in_specs=[pl.BlockSpec((SB_QROWS, HEAD), lambda h, i: (i, qb + h)),
                  pl.BlockSpec((t, HEAD), lambda h, i: (0, cb + h)),
                  pl.BlockSpec((t, HEAD), lambda h, i: (0, vb + h)),
                  pl.BlockSpec((1, HEAD), lambda h, i: (0, 0)),
                  pl.BlockSpec((1, HEAD), lambda h, i: (0, 0))],
        out_specs=pl.BlockSpec((SB_QROWS, HEAD), lambda h, i: (i, h)),
        out_shape=jax.ShapeDtypeStruct((t, BRANCH), f32),
        scratch_shapes=[pltpu.VMEM((_sb_padded(t), HEAD), bf16), pltpu.VMEM((_sb_padded(t), HEAD), bf16)],
        compiler_params=_cparams(("arbitrary", "arbitrary")), name="sb_fwd")(proj, proj, proj, qw, kw)


def sb_bwd(proj, qw, kw, o, do, dproj):
    t = proj.shape[0]
    nb = t // SB_BLOCK

    def body(q_ref, k_ref, v_ref, qw_ref, kw_ref, o_ref, do_ref, _, dp_ref, dqw_ref, dkw_ref,
             kn_scr, vb_scr, dk_acc, dv_acc, dq_stage, kv_stage, dq_sems, kv_sems):
        h, grp = pl.program_id(0), pl.program_id(1)
        n_grp = nb // SB_QB
        step = h * n_grp + grp
        slot = step % 2

        def dq_copy(sl, head):
            return pltpu.make_async_copy(
                dq_stage.at[sl], dp_ref.at[pl.ds(pl.multiple_of(grp * SB_QROWS, SB_QROWS), SB_QROWS),
                                           pl.ds(C_SBQ + head * HEAD, HEAD)], dq_sems.at[sl])

        @pl.when(grp == 0)
        def _():
            _sb_prep(k_ref, v_ref, kw_ref, kn_scr, vb_scr, nb)
            dk_acc[...] = jnp.zeros_like(dk_acc)
            dv_acc[...] = jnp.zeros_like(dv_acc)

        @pl.when((grp == 0) & (h == 0))
        def _():
            dqw_ref[...] = jnp.zeros_like(dqw_ref)
            dkw_ref[...] = jnp.zeros_like(dkw_ref)

        u_ex = _tri_ext(lambda r, c: r > c)
        u_in = _tri_ext(lambda r, c: r >= c)

        def tile(i, qn, dob, d_row, jb, carry, masked):
            dq, r_carry, f_carry = carry
            start, rows = _sb_rows(i, jb)
            kt = kn_scr[rows, :]
            vt = vb_scr[rows, :]
            mask, lsz, lk = _sb_scores(qn, kt, i, start, masked)
            da = lax.dot_general(dob, vt, (NT, ((), ())), preferred_element_type=f32)
            yield
            passed, r_carry = _sb_suffix(lk, r_carry, u_ex)
            yield
            a = jnp.exp(lsz + passed)
            if masked:
                a = jnp.where(mask, a, 0.0)
            e = a * da
            e_suf, f_carry = _sb_suffix(e, f_carry, u_in)
            yield
            sg = jnp.exp(lsz)
            dz = (e * (1.0 - sg) - (d_row - e_suf) * sg) * SB_SCALE
            if masked:
                dz = jnp.where(mask, dz, 0.0)
            dzb = dz.astype(bf16)
            dq = dq + jnp.dot(dzb, kt, preferred_element_type=f32)
            dk_acc[rows, :] += lax.dot_general(dzb, qn, (TN, ((), ())), preferred_element_type=f32)
            dv_acc[rows, :] += lax.dot_general(a.astype(bf16), dob, (TN, ((), ())), preferred_element_type=f32)
            yield
            return dq, r_carry, f_carry

        zeros = jnp.zeros((SB_BLOCK, HEAD), f32)
        blocks = []
        for b in range(SB_QB):
            rows_b = slice(b * SB_BLOCK, (b + 1) * SB_BLOCK)
            q = q_ref[rows_b, :]
            rq = lax.rsqrt(jnp.mean(q * q, axis=-1, keepdims=True) + EPS)
            qh = q * rq
            dob = do_ref[rows_b, :].astype(bf16)
            d_row = jnp.sum(dob.astype(f32) * o_ref[rows_b, :], axis=-1, keepdims=True)
            blocks.append((grp * SB_QB + b, (qh * qw_ref[...]).astype(bf16), dob, d_row, rq, qh))
        firsts = _interleave([tile(i, qn, dob, d_row, 0, (zeros, zeros, zeros), True)
                              for i, qn, dob, d_row, _, _ in blocks])

        @pl.when(step >= 2)
        def _():
            dq_copy(slot, 0).wait()

        for b, ((i, qn, dob, d_row, rq, qh), carry) in enumerate(zip(blocks, firsts)):
            one = lambda jb, c, masked, i=i, qn=qn, dob=dob, d_row=d_row: _drain(tile(i, qn, dob, d_row, jb, c, masked))
            dqn, _, _ = _sb_walk(i, one, carry, first_done=True)
            gq = dqn * qw_ref[...]
            dqw_ref[...] += jnp.sum(dqn * qh, axis=0, keepdims=True)
            dq_stage[slot, b * SB_BLOCK:(b + 1) * SB_BLOCK, :] = (
                rq * (gq - qh * jnp.mean(gq * qh, axis=-1, keepdims=True))).astype(bf16)
        for head in range(N_HEADS):
            @pl.when(h == head)
            def _(head=head):
                dq_copy(slot, head).start()

        @pl.when(grp == n_grp - 1)
        def _():
            def fin(b, c):
                rows = pl.ds(pl.multiple_of(b * SB_PREP_ROWS, SB_BLOCK), SB_PREP_ROWS)
                pad_rows = pl.ds(pl.multiple_of(SB_PADR + b * SB_PREP_ROWS, SB_BLOCK), SB_PREP_ROWS)
                kk = k_ref[rows, :]
                rk = lax.rsqrt(jnp.mean(kk * kk, axis=-1, keepdims=True) + EPS)
                kh = kk * rk
                dkn = dk_acc[pad_rows, :]
                gk = dkn * kw_ref[...]
                kv_stage[0, rows, :] = (rk * (gk - kh * jnp.mean(gk * kh, axis=-1, keepdims=True))).astype(bf16)
                kv_stage[1, rows, :] = dv_acc[pad_rows, :].astype(bf16)
                dkw_ref[...] += jnp.sum(dkn * kh, axis=0, keepdims=True)
                return c

            lax.fori_loop(0, nb * SB_BLOCK // SB_PREP_ROWS, fin, 0)
            for head in range(N_HEADS):
                @pl.when(h == head)
                def _(head=head):
                    outs = [pltpu.make_async_copy(kv_stage.at[n], dp_ref.at[:, pl.ds(c0 + head * HEAD, HEAD)], kv_sems.at[n])
                            for n, c0 in enumerate((C_SBK, C_SBV))]
                    for cp in outs:
                        cp.start()
                    for cp in outs:
                        cp.wait()

        @pl.when(step == N_HEADS * n_grp - 1)
        def _():
            dq_copy(1 - slot, 0).wait()
            dq_copy(slot, 0).wait()

    qb, cb, vb = C_SBQ // HEAD, C_SBK // HEAD, C_SBV // HEAD
    blk = pl.BlockSpec((SB_QROWS, HEAD), lambda h, i: (i, h))
    wsp = pl.BlockSpec((1, HEAD), lambda h, i: (0, 0))
    any_spec = pl.BlockSpec(memory_space=pl.ANY)
    return pl.pallas_call(
        body, grid=(N_HEADS, nb // SB_QB),
        in_specs=[pl.BlockSpec((SB_QROWS, HEAD), lambda h, i: (i, qb + h)),
                  pl.BlockSpec((t, HEAD), lambda h, i: (0, cb + h)),
                  pl.BlockSpec((t, HEAD), lambda h, i: (0, vb + h)), wsp, wsp, blk, blk, any_spec],
        out_specs=[any_spec, wsp, wsp],
        out_shape=[jax.ShapeDtypeStruct((t, N_MAIN), bf16)] + [jax.ShapeDtypeStruct((1, HEAD), f32)] * 2,
        scratch_shapes=[pltpu.VMEM((_sb_padded(t), HEAD), bf16), pltpu.VMEM((_sb_padded(t), HEAD), bf16),
                        pltpu.VMEM((_sb_padded(t), HEAD), f32), pltpu.VMEM((_sb_padded(t), HEAD), f32),
                        pltpu.VMEM((2, SB_QROWS, HEAD), bf16), pltpu.VMEM((2, t, HEAD), bf16),
                        pltpu.SemaphoreType.DMA((2,)), pltpu.SemaphoreType.DMA((2,))],
        input_output_aliases={7: 0},
        compiler_params=_cparams(("arbitrary", "arbitrary")), name="sb_bwd")(proj, proj, proj, qw, kw, o, do, dproj)


TM_CONV = 640
CONV_W = 4
GQKV = 3 * BRANCH


def conv_fwd(proj, cw):
    t = proj.shape[0]
    halo_blocks = TM_CONV // 8

    def body(x0_ref, x1_ref, x2_ref, p0_ref, p1_ref, p2_ref, cw_ref, y_ref):
        i = pl.program_id(0)
        for s, (x_ref, p_ref) in enumerate(((x0_ref, p0_ref), (x1_ref, p1_ref), (x2_ref, p2_ref))):
            prev = jnp.where(i > 0, p_ref[...], 0.0)
            xx = jnp.concatenate([prev, x_ref[...]], axis=0)
            cols = slice(s * BRANCH, (s + 1) * BRANCH)
            y = xx[8:] * cw_ref[CONV_W - 1:CONV_W, cols]
            for k in range(CONV_W - 1):
                y = y + pltpu.roll(xx, CONV_W - 1 - k, 0)[8:] * cw_ref[k:k + 1, cols]
            y_ref[:, cols] = y

    c0 = C_GQKV // BRANCH
    xs = [pl.BlockSpec((TM_CONV, BRANCH), functools.partial(lambda i, s: (i, c0 + s), s=s)) for s in range(3)]
    ps = [pl.BlockSpec((8, BRANCH), functools.partial(lambda i, s: (jnp.maximum(i * halo_blocks - 1, 0), c0 + s), s=s))
          for s in range(3)]
    return pl.pallas_call(
        body, grid=(t // TM_CONV,),
        in_specs=xs + ps + [pl.BlockSpec((CONV_W, GQKV), lambda i: (0, 0))],
        out_specs=pl.BlockSpec((TM_CONV, GQKV), lambda i: (i, 0)),
        out_shape=jax.ShapeDtypeStruct((t, GQKV), f32),
        compiler_params=_cparams(("arbitrary",)), name="conv_fwd")(proj, proj, proj, proj, proj, proj, cw)


def conv_bwd(proj, cw, dy, dproj):
    t = proj.shape[0]
    nt = t // TM_CONV
    halo_blocks = TM_CONV // 8

    def body(x0_ref, x1_ref, x2_ref, p0_ref, p1_ref, p2_ref, cw_ref, dy_ref, dyn_ref, _, dx_ref, dw_ref):
        i = pl.program_id(0)

        @pl.when(i == 0)
        def _():
            dw_ref[...] = jnp.zeros_like(dw_ref)

        nxt = jnp.where(i < nt - 1, dyn_ref[...], 0.0)
        dyy = jnp.concatenate([dy_ref[...], nxt], axis=0)
        n_rows = TM_CONV + 8
        dx = dyy[:TM_CONV] * cw_ref[CONV_W - 1:CONV_W, :]
        for k in range(CONV_W - 1):
            sh = CONV_W - 1 - k
            dx = dx + pltpu.roll(dyy, n_rows - sh, 0)[:TM_CONV] * cw_ref[k:k + 1, :]
        dx_ref[...] = dx.astype(bf16)
        dy_c = dy_ref[...]
        for s, (x_ref, p_ref) in enumerate(((x0_ref, p0_ref), (x1_ref, p1_ref), (x2_ref, p2_ref))):
            prev = jnp.where(i > 0, p_ref[...], 0.0)
            xx = jnp.concatenate([prev, x_ref[...]], axis=0)
            cols = slice(s * BRANCH, (s + 1) * BRANCH)
            for k in range(CONV_W):
                sh = CONV_W - 1 - k
                xs = xx[8:] if sh == 0 else pltpu.roll(xx, sh, 0)[8:]
                dw_ref[k:k + 1, cols] += jnp.sum(xs * dy_c[:, cols], axis=0, keepdims=True)

    c0 = C_GQKV // BRANCH
    xs = [pl.BlockSpec((TM_CONV, BRANCH), functools.partial(lambda i, s: (i, c0 + s), s=s)) for s in range(3)]
    ps = [pl.BlockSpec((8, BRANCH), functools.partial(lambda i, s: (jnp.maximum(i * halo_blocks - 1, 0), c0 + s), s=s))
          for s in range(3)]
    return pl.pallas_call(
        body, grid=(nt,),
        in_specs=xs + ps + [pl.BlockSpec((CONV_W, GQKV), lambda i: (0, 0)),
                            pl.BlockSpec((TM_CONV, GQKV), lambda i: (i, 0)),
                            pl.BlockSpec((8, GQKV), lambda i: (jnp.minimum((i + 1) * halo_blocks, nt * halo_blocks - 1), 0)),
                            pl.BlockSpec(memory_space=pl.ANY)],
        out_specs=[pl.BlockSpec((TM_CONV, GQKV), lambda i: (i, C_GQKV // GQKV)), pl.BlockSpec((CONV_W, GQKV), lambda i: (0, 0))],
        out_shape=[jax.ShapeDtypeStruct((t, N_MAIN), bf16), jax.ShapeDtypeStruct((CONV_W, GQKV), f32)],
        input_output_aliases={9: 0},
        compiler_params=_cparams(("arbitrary",)), name="conv_bwd")(proj, proj, proj, proj, proj, proj, cw, dy, dy, dproj)


def _iota2(n, m, d):
    return lax.broadcasted_iota(jnp.int32, (n, m), d)


def _lane_pick(row_or_mat, idx):
    lanes = lax.broadcasted_iota(jnp.int32, row_or_mat.shape, row_or_mat.ndim - 1)
    return jnp.sum(jnp.where(lanes == idx, row_or_mat, 0.0), axis=-1, keepdims=True)


def _cumsum_consts():
    i = np.arange(CHUNK)
    incl = i[None, :] <= i[:, None]
    suf = i[None, :] > i[:, None]
    return np.concatenate([incl, suf], 0).astype(np.float32)


HG_LEVELS = (64, 32, 16, 8, 4, 2)


def _hgrn_consts():
    i = np.arange(CHUNK)
    rows = [i[None, :] <= i[:, None], i[None, :] > i[:, None]]
    for b in HG_LEVELS:
        ref = (i // b) * b + b // 2 - 1
        second = (i % b) >= b // 2
        rows.append((i[None, :] > ref[:, None]) & (i[None, :] <= i[:, None]) & second[:, None])
        rows.append((i[None, :] > i[:, None]) & (i[None, :] <= ref[:, None]) & (~second)[:, None])
    return np.concatenate(rows, 0).astype(np.float32)


N_SQUARINGS = 5


def _solve_chain(ms, rhss):
    xs = [r - mm(m, r) for m, r in zip(ms, rhss)]
    powers = [list(ms)]
    for _ in range(N_SQUARINGS):
        powers.append([mm(p, p) for p in powers[-1]])
        xs = [x + mm(p, x) for p, x in zip(powers[-1], xs)]
    return tuple(xs), powers


@jax.custom_vjp
def unit_lower_solve_multi(ms, rhss):
    return _solve_chain(ms, rhss)[0]


def _solve_fwd(ms, rhss):
    xs, powers = _solve_chain(ms, rhss)
    return xs, (powers, xs)


def _solve_bwd(res, gs):
    powers, xs = res
    ys = [g - mm_tn(p, g) for p, g in zip(powers[0], gs)]
    for ps in powers[1:]:
        ys = [y + mm_tn(p, y) for p, y in zip(ps, ys)]
    return tuple(-mm_nt(y, x) for y, x in zip(ys, xs)), tuple(ys)


unit_lower_solve_multi.defvjp(_solve_fwd, _solve_bwd)

HC = N_HEADS * CHUNK
BATCH0 = ((0,), (0,))


def _bdot(a, b, contract):
    return lax.dot_general(a.astype(bf16), b.astype(bf16), (contract, BATCH0), preferred_element_type=f32)


B_NN = ((2,), (1,))
B_NT = ((2,), (2,))
B_TN = ((1,), (1,))


@jax.custom_vjp
def bmm(a, b):
    return _bdot(a, b, B_NN)


bmm.defvjp(lambda a, b: (_bdot(a, b, B_NN), (a, b)),
           lambda r, g: (_bdot(g, r[1], B_NT), _bdot(r[0], g, B_TN)))


@jax.custom_vjp
def bmm_nt(a, b):
    return _bdot(a, b, B_NT)


bmm_nt.defvjp(lambda a, b: (_bdot(a, b, B_NT), (a, b)),
              lambda r, g: (_bdot(g, r[1], B_NN), _bdot(g, r[0], B_TN)))


@jax.custom_vjp
def bmm_tn(a, b):
    return _bdot(a, b, B_TN)


bmm_tn.defvjp(lambda a, b: (_bdot(a, b, B_TN), (a, b)),
              lambda r, g: (_bdot(r[1], g, B_NT), _bdot(r[0], g, B_NN)))


def _stack_heads(x):
    return jnp.concatenate([x[:, h * HEAD:(h + 1) * HEAD] for h in range(N_HEADS)], axis=0)


def _unstack_heads(x):
    return jnp.concatenate([x[h * CHUNK:(h + 1) * CHUNK] for h in range(N_HEADS)], axis=1)


REC_CHUNKS = 5
REC_ROWS = REC_CHUNKS * CHUNK


def _interleave(gens):
    n = len(gens)
    sends, results, done = [None] * n, [None] * n, [False] * n
    while not all(done):
        asks = []
        for j in range(n):
            if done[j]:
                continue
            try:
                ask = gens[j].send(sends[j])
                if ask is not None:
                    asks.append((j, ask))
            except StopIteration as stop:
                results[j], done[j] = stop.value, True
            sends[j] = None
        if asks:
            xs = unit_lower_solve_multi(tuple(a[1] for _, a in asks), tuple(a[2] for _, a in asks))
            for (j, _), x in zip(asks, xs):
                sends[j] = x
    return results


def _chunks_of(a):
    return [a[j * CHUNK:(j + 1) * CHUNK] for j in range(REC_CHUNKS)]


def _gdn_step(state, ypre, small, gz, a_log, dt_b, on_w, c2, first_chunk):
    masks = _gdn_masks()
    gens = [_gdn_intra(y, s, a_log, dt_b, c2, _vmask(first_chunk + j), masks)
            for j, (y, s) in enumerate(zip(_chunks_of(ypre), _chunks_of(small)))]
    outs = []
    for intra, z in zip(_interleave(gens), _chunks_of(gz)):
        state, o = _gdn_inter(state, intra, z, on_w)
        outs.append(o)
    return state, jnp.concatenate(outs, axis=0)


def _gdn_inter(state, intra, gz, on_w):
    u, w, aqk, q_dec, k_dec, g_last = intra
    per_head = lambda a: a.reshape(N_HEADS, CHUNK, HEAD)
    v_new = u - bmm(per_head(w), state).reshape(HC, HEAD)
    o = bmm(per_head(q_dec), state).reshape(HC, HEAD) + mm(aqk, v_new)
    new_state = state * g_last + bmm_tn(per_head(k_dec), per_head(v_new))
    return new_state, _unstack_heads(_rms(o, on_w)) * _silu(gz)


def _gdn_masks():
    r = _iota2(HC, HC, 0)
    c = _iota2(HC, HC, 1)
    same_head = (r >> (CHUNK.bit_length() - 1)) == (c >> (CHUNK.bit_length() - 1))
    return same_head & (r >= c), same_head & (r > c)


def _gdn_intra(ypre, small, a_log, dt_b, c2, vm, masks):
    causal, strict = masks
    q = _silu(_stack_heads(ypre[:, :BRANCH]))
    k = _silu(_stack_heads(ypre[:, BRANCH:2 * BRANCH]))
    v = _silu(_stack_heads(ypre[:, 2 * BRANCH:]))
    q = q * lax.rsqrt(jnp.sum(q * q, axis=-1, keepdims=True) + EPS) * (HEAD ** -0.5)
    k = k * lax.rsqrt(jnp.sum(k * k, axis=-1, keepdims=True) + EPS)
    col = lambda f: jnp.concatenate([f(h) for h in range(N_HEADS)], axis=0)
    chunk_col = lambda x: jnp.broadcast_to(x, (CHUNK, 1))
    beta = _sigmoid(col(lambda h: _lane_pick(small, h))) * col(lambda h: vm)
    g = (-jnp.exp(col(lambda h: chunk_col(_lane_pick(a_log, h))))
         * _softplus(col(lambda h: _lane_pick(small, N_HEADS + h)) + col(lambda h: chunk_col(_lane_pick(dt_b, h)))))
    g_l = _unstack_heads(jnp.broadcast_to(g, (HC, HEAD)))
    e2 = cmm(c2, g_l)
    yield
    gc = _stack_heads(e2[:CHUNK])
    gsuf = _stack_heads(e2[CHUNK:])
    g_row = jnp.broadcast_to(jnp.transpose(gc)[0:1, :], (HC, HC))
    g_col = jnp.concatenate([gc, gc], axis=1)
    dec = jnp.where(causal, jnp.exp(jnp.minimum(g_col - g_row, 0.0)), 0.0)
    kb = k * beta
    kk = mm_nt(kb, k)
    qk = mm_nt(q, k)
    yield
    m = jnp.where(strict, kk * dec, 0.0)
    x = yield ("solve", m, jnp.concatenate([v * beta, kb * jnp.exp(gc)], axis=1))
    aqk = jnp.where(causal, qk * dec, 0.0)
    tot = jnp.sum(g_l, axis=0, keepdims=True)
    g_last = jnp.exp(jnp.stack([tot[:, h * HEAD:(h + 1) * HEAD] for h in range(N_HEADS)], axis=0))
    return x[:, :HEAD], x[:, HEAD:], aqk, q * jnp.exp(gc), k * jnp.exp(gsuf), g_last


def _hgrn_step(state, hq, hf, hi, hz, lb, on_w, cm, first_chunk):
    masks = _hgrn_masks()
    gens = [_hgrn_intra(q, f, i, lb, cm, _vmask(first_chunk + j), masks)
            for j, (q, f, i) in enumerate(zip(_chunks_of(hq), _chunks_of(hf), _chunks_of(hi)))]
    outs = []
    for (q_dec, k_dec, v, o_intra, g_end), z in zip(_interleave(gens), _chunks_of(hz)):
        per_head = lambda a: a.reshape(N_HEADS, CHUNK, HEAD)
        o = bmm_nt(per_head(q_dec), state).reshape(HC, HEAD) + o_intra
        state = state * g_end + bmm_tn(per_head(v), per_head(k_dec))
        outs.append(_unstack_heads(_rms(o, on_w)) * _silu(z))
    return state, jnp.concatenate(outs, axis=0)


def _hgrn_masks():
    r = _iota2(HC, HC, 0)
    c = _iota2(HC, HC, 1)
    pairs = []
    for b in HG_LEVELS:
        sh = b.bit_length() - 1
        pairs.append(((r >> sh) == (c >> sh)) & ((r & (b - 1)) >= b // 2) & ((c & (b - 1)) < b // 2))
    return r == c, pairs


def _hgrn_intra(hq, hf, hi, lb, cm, vm, masks):
    diag, pairs = masks
    forget = lb + (1.0 - lb) * _sigmoid(hf)
    g_l = jnp.log(forget)
    e = cmm(cm, g_l)
    q = _stack_heads(_silu(hq))
    k = _stack_heads((1.0 - lb) * _sigmoid(-hf))
    v = _stack_heads(hi * vm)
    yield
    sect = lambda n: _stack_heads(e[n * CHUNK:(n + 1) * CHUNK])
    gc, gsuf = sect(0), sect(1)
    a = jnp.where(diag, jnp.sum(q * k, axis=-1, keepdims=True), 0.0)
    for li, pair in enumerate(pairs):
        a = a + jnp.where(pair, mm_nt(q * jnp.exp(sect(2 + 2 * li)), k * jnp.exp(sect(3 + 2 * li))), 0.0)
    yield
    o_intra = mm(a, v)
    tot = jnp.sum(g_l, axis=0, keepdims=True)
    g_end = jnp.exp(jnp.stack([tot[:, h * HEAD:(h + 1) * HEAD] for h in range(N_HEADS)], axis=0))
    return q * jnp.exp(gc), k * jnp.exp(gsuf), v, o_intra, g_end


def _vmask(chunk_idx):
    rows = chunk_idx * CHUNK + lax.broadcasted_iota(jnp.int32, (CHUNK, 1), 0)
    return jnp.where(rows >= PAD_FRONT, 1.0, 0.0)


def _row(n):
    return pl.BlockSpec((1, n), lambda i: (0, 0))


def gdn_fwd(ypre, small, proj, a_log, dt_b, on_w):
    t = ypre.shape[0]
    nc = t // REC_ROWS
    c2 = jnp.asarray(_cumsum_consts(), bf16)

    def body(y_ref, s_ref, z_ref, al_ref, dt_ref, on_ref, c2_ref, o_ref, st_ref, state):
        i = pl.program_id(0)

        @pl.when(i == 0)
        def _():
            state[...] = jnp.zeros_like(state)

        s_in = state[...]
        st_ref[0] = s_in
        s_new, out = _gdn_step(s_in, y_ref[...], s_ref[...], z_ref[...], al_ref[...], dt_ref[...], on_ref[...],
                               c2_ref[...], i * REC_CHUNKS)
        state[...] = s_new
        o_ref[...] = out

    return pl.pallas_call(
        body, grid=(nc,),
        in_specs=[pl.BlockSpec((REC_ROWS,GQKV), lambda i: (i, 0)), pl.BlockSpec((REC_ROWS,N_SMALL), lambda i: (i, 0)),
                  pl.BlockSpec((REC_ROWS,BRANCH), lambda i: (i, C_GZ // BRANCH)), _row(128), _row(128), _row(128),
                  pl.BlockSpec((2 * CHUNK, CHUNK), lambda i: (0, 0))],
        out_specs=[pl.BlockSpec((REC_ROWS,BRANCH), lambda i: (i, 0)),
                   pl.BlockSpec((1, N_HEADS, HEAD, HEAD), lambda i: (i, 0, 0, 0))],
        out_shape=[jax.ShapeDtypeStruct((t, BRANCH), f32), jax.ShapeDtypeStruct((nc, N_HEADS, HEAD, HEAD), f32)],
        scratch_shapes=[pltpu.VMEM((N_HEADS, HEAD, HEAD), f32)],
        compiler_params=_cparams(("arbitrary",)), name="gdn_fwd")(ypre, small, proj, a_log, dt_b, on_w, c2)


def gdn_bwd(ypre, small, proj, a_log, dt_b, on_w, states, d_out, dproj):
    t = ypre.shape[0]
    nc = t // REC_ROWS
    c2 = jnp.asarray(_cumsum_consts(), bf16)

    def body(y_ref, s_ref, z_ref, al_ref, dt_ref, on_ref, c2_ref, st_ref, do_ref, _,
             dy_ref, ds_ref, dz_ref, dal_ref, ddt_ref, don_ref, dstate):
        i = pl.program_id(0)

        @pl.when(i == 0)
        def _():
            dstate[...] = jnp.zeros_like(dstate)
            dal_ref[...] = jnp.zeros_like(dal_ref)
            ddt_ref[...] = jnp.zeros_like(ddt_ref)
            don_ref[...] = jnp.zeros_like(don_ref)

        c2v = c2_ref[...]
        fn = lambda s, y, sm, z, al, dt, on: _gdn_step(s, y, sm, z, al, dt, on, c2v, (nc - 1 - i) * REC_CHUNKS)
        _, vjp = jax.vjp(fn, st_ref[0], y_ref[...], s_ref[...], z_ref[...], al_ref[...], dt_ref[...], on_ref[...])
        d_s, d_y, d_sm, d_z, d_al, d_dt, d_on = vjp((dstate[...], do_ref[...]))
        dstate[...] = d_s
        dy_ref[...] = d_y
        ds_ref[...] = d_sm
        dz_ref[...] = d_z.astype(bf16)
        dal_ref[...] += d_al
        ddt_ref[...] += d_dt
        don_ref[...] += d_on

    rev = lambda i: (nc - 1 - i, 0)
    return pl.pallas_call(
        body, grid=(nc,),
        in_specs=[pl.BlockSpec((REC_ROWS,GQKV), rev), pl.BlockSpec((REC_ROWS,N_SMALL), rev),
                  pl.BlockSpec((REC_ROWS,BRANCH), lambda i: (nc - 1 - i, C_GZ // BRANCH)), _row(128), _row(128), _row(128),
                  pl.BlockSpec((2 * CHUNK, CHUNK), lambda i: (0, 0)),
                  pl.BlockSpec((1, N_HEADS, HEAD, HEAD), lambda i: (nc - 1 - i, 0, 0, 0)),
                  pl.BlockSpec((REC_ROWS,BRANCH), rev), pl.BlockSpec(memory_space=pl.ANY)],
        out_specs=[pl.BlockSpec((REC_ROWS,GQKV), rev), pl.BlockSpec((REC_ROWS,N_SMALL), rev),
                   pl.BlockSpec((REC_ROWS,BRANCH), lambda i: (nc - 1 - i, C_GZ // BRANCH)), _row(128), _row(128), _row(128)],
        out_shape=[jax.ShapeDtypeStruct((t, GQKV), f32), jax.ShapeDtypeStruct((t, N_SMALL), f32),
                   jax.ShapeDtypeStruct((t, N_MAIN), bf16)] + [jax.ShapeDtypeStruct((1, 128), f32)] * 3,
        scratch_shapes=[pltpu.VMEM((N_HEADS, HEAD, HEAD), f32)], input_output_aliases={9: 2},
        compiler_params=_cparams(("arbitrary",)), name="gdn_bwd")(
            ypre, small, proj, a_log, dt_b, on_w, c2, states, d_out, dproj)


def hgrn_fwd(proj, lb, on_w):
    t = proj.shape[0]
    nc = t // REC_ROWS
    cm = jnp.asarray(_hgrn_consts(), bf16)
    ncm = cm.shape[0]

    def body(q_ref, f_ref, i_ref, z_ref, lb_ref, on_ref, cm_ref, o_ref, st_ref, state):
        i = pl.program_id(0)

        @pl.when(i == 0)
        def _():
            state[...] = jnp.zeros_like(state)

        s_in = state[...]
        st_ref[0] = s_in
        s_new, out = _hgrn_step(s_in, q_ref[...], f_ref[...], i_ref[...], z_ref[...], lb_ref[...], on_ref[...],
                                cm_ref[...], i * REC_CHUNKS)
        state[...] = s_new
        o_ref[...] = out

    sec = lambda off: pl.BlockSpec((REC_ROWS,BRANCH), functools.partial(lambda i, b: (i, b), b=off // BRANCH))
    return pl.pallas_call(
        body, grid=(nc,),
        in_specs=[sec(C_HQ), sec(C_HF), sec(C_HI), sec(C_HZ), _row(BRANCH), _row(128),
                  pl.BlockSpec((ncm, CHUNK), lambda i: (0, 0))],
        out_specs=[pl.BlockSpec((REC_ROWS,BRANCH), lambda i: (i, 0)),
                   pl.BlockSpec((1, N_HEADS, HEAD, HEAD), lambda i: (i, 0, 0, 0))],
        out_shape=[jax.ShapeDtypeStruct((t, BRANCH), f32), jax.ShapeDtypeStruct((nc, N_HEADS, HEAD, HEAD), f32)],
        scratch_shapes=[pltpu.VMEM((N_HEADS, HEAD, HEAD), f32)],
        compiler_params=_cparams(("arbitrary",)), name="hgrn_fwd")(proj, proj, proj, proj, lb, on_w, cm)


def hgrn_bwd(proj, lb, on_w, states, d_out, dproj):
    t = proj.shape[0]
    nc = t // REC_ROWS
    cm = jnp.asarray(_hgrn_consts(), bf16)
    ncm = cm.shape[0]

    def body(q_ref, f_ref, i_ref, z_ref, lb_ref, on_ref, cm_ref, st_ref, do_ref, _, dh_ref, dlb_ref, don_ref, dstate):
        i = pl.program_id(0)

        @pl.when(i == 0)
        def _():
            dstate[...] = jnp.zeros_like(dstate)
            dlb_ref[...] = jnp.zeros_like(dlb_ref)
            don_ref[...] = jnp.zeros_like(don_ref)

        cmv = cm_ref[...]
        fn = lambda s, a, b, c, d, l, on: _hgrn_step(s, a, b, c, d, l, on, cmv, (nc - 1 - i) * REC_CHUNKS)
        _, vjp = jax.vjp(fn, st_ref[0], q_ref[...], f_ref[...], i_ref[...], z_ref[...], lb_ref[...], on_ref[...])
        d_s, d_q, d_f, d_i, d_z, d_lb, d_on = vjp((dstate[...], do_ref[...]))
        dstate[...] = d_s
        dh_ref[...] = jnp.concatenate([d_q, d_f, d_i, d_z], axis=1).astype(bf16)
        dlb_ref[...] += d_lb
        don_ref[...] += d_on

    rev = lambda i: (nc - 1 - i, 0)
    sec = lambda off: pl.BlockSpec((REC_ROWS,BRANCH), functools.partial(lambda i, b: (nc - 1 - i, b), b=off // BRANCH))
    return pl.pallas_call(
        body, grid=(nc,),
        in_specs=[sec(C_HQ), sec(C_HF), sec(C_HI), sec(C_HZ), _row(BRANCH), _row(128),
                  pl.BlockSpec((ncm, CHUNK), lambda i: (0, 0)),
                  pl.BlockSpec((1, N_HEADS, HEAD, HEAD), lambda i: (nc - 1 - i, 0, 0, 0)),
                  pl.BlockSpec((REC_ROWS,BRANCH), rev), pl.BlockSpec(memory_space=pl.ANY)],
        out_specs=[pl.BlockSpec((REC_ROWS,4 * BRANCH), lambda i: (nc - 1 - i, C_HQ // (4 * BRANCH))), _row(BRANCH), _row(128)],
        out_shape=[jax.ShapeDtypeStruct((t, N_MAIN), bf16), jax.ShapeDtypeStruct((1, BRANCH), f32),
                   jax.ShapeDtypeStruct((1, 128), f32)],
        scratch_shapes=[pltpu.VMEM((N_HEADS, HEAD, HEAD), f32)], input_output_aliases={9: 0},
        compiler_params=_cparams(("arbitrary",)), name="hgrn_bwd")(proj, proj, proj, proj, lb, on_w, cm, states, d_out, dproj)


TM_MG = 320


def _const_spec(shape):
    nd = len(shape)
    return pl.BlockSpec(shape, lambda i: (0,) * nd, pipeline_mode=pl.Buffered(1))


def merge_fwd(osb, proj, ogd, ohg, wb, wo, h):
    t = h.shape[0]

    def body(osb_ref, ogd_ref, ohg_ref, zm_ref, wb_ref, wo_ref, h_ref, out_ref):
        a = osb_ref[...] * _silu(zm_ref[:, :BRANCH])
        gate = lambda b: _sigmoid(zm_ref[:, C_MIX + b * D_MODEL:C_MIX + (b + 1) * D_MODEL])
        y = (gate(0) * _dot(a, wb_ref[0]) + gate(1) * _dot(ogd_ref[...], wb_ref[1])
             + gate(2) * _dot(ohg_ref[...], wb_ref[2]))
        out_ref[...] = h_ref[...] + _dot(y, wo_ref[...])

    br = pl.BlockSpec((TM_MG, BRANCH), lambda i: (i, 0))
    return pl.pallas_call(
        body, grid=(t // TM_MG,),
        in_specs=[br, br, br, pl.BlockSpec((TM_MG, W_MERGE), lambda i: (i, 0)),
                  _const_spec((3, BRANCH, D_MODEL)), _const_spec((D_MODEL, D_MODEL)),
                  pl.BlockSpec((TM_MG, D_MODEL), lambda i: (i, 0))],
        out_specs=pl.BlockSpec((TM_MG, D_MODEL), lambda i: (i, 0)),
        out_shape=jax.ShapeDtypeStruct((t, D_MODEL), f32),
        compiler_params=_cparams(("arbitrary",)), name="merge_fwd")(osb, ogd, ohg, proj, wb, wo, h)


def merge_bwd(osb, proj, ogd, ohg, wb, wbt, wot, dh):
    t = dh.shape[0]

    def body(osb_ref, ogd_ref, ohg_ref, zm_ref, wb_ref, wbt_ref, wot_ref, dh_ref,
             dosb_ref, dogd_ref, dohg_ref, dzm_ref, dwo_ref, dwb_ref):
        i = pl.program_id(0)

        @pl.when(i == 0)
        def _():
            dwo_ref[...] = jnp.zeros_like(dwo_ref)
            dwb_ref[...] = jnp.zeros_like(dwb_ref)

        osb = osb_ref[...]
        sbz = zm_ref[:, :BRANCH]
        sgz = _sigmoid(sbz)
        sz = sbz * sgz
        branch_in = (osb * sz, ogd_ref[...], ohg_ref[...])
        dh_v = dh_ref[...]
        dy = _dot(dh_v, wot_ref[...])
        y = jnp.zeros((TM_MG, D_MODEL), f32)
        d_in = []
        for b in range(3):
            cols = slice(C_MIX + b * D_MODEL, C_MIX + (b + 1) * D_MODEL)
            p = _dot(branch_in[b], wb_ref[b])
            g = _sigmoid(zm_ref[:, cols])
            y = y + g * p
            dp = dy * g
            dzm_ref[:, cols] = (dy * p * g * (1.0 - g)).astype(bf16)
            d_in.append(_dot(dp, wbt_ref[b]))
            dwb_ref[b] += _dot(branch_in[b], dp, TN)
        dwo_ref[...] += _dot(y, dh_v, TN)
        dosb_ref[...] = d_in[0] * sz
        dzm_ref[:, :BRANCH] = (d_in[0] * osb * (sgz * (1.0 + sbz * (1.0 - sgz)))).astype(bf16)
        dogd_ref[...] = d_in[1]
        dohg_ref[...] = d_in[2]

    br = pl.BlockSpec((TM_MG, BRANCH), lambda i: (i, 0))
    zm = pl.BlockSpec((TM_MG, W_MERGE), lambda i: (i, 0))
    return pl.pallas_call(
        body, grid=(t // TM_MG,),
        in_specs=[br, br, br, zm,
                  _const_spec((3, BRANCH, D_MODEL)), _const_spec((3, D_MODEL, BRANCH)), _const_spec((D_MODEL, D_MODEL)),
                  pl.BlockSpec((TM_MG, D_MODEL), lambda i: (i, 0))],
        out_specs=[br, br, br, zm, _const_spec((D_MODEL, D_MODEL)), _const_spec((3, BRANCH, D_MODEL))],
        out_shape=[jax.ShapeDtypeStruct((t, BRANCH), f32)] * 3 + [jax.ShapeDtypeStruct((t, N_MAIN), bf16),
                   jax.ShapeDtypeStruct((D_MODEL, D_MODEL), f32), jax.ShapeDtypeStruct((3, BRANCH, D_MODEL), f32)],
        compiler_params=_cparams(("arbitrary",)), name="merge_bwd")(osb, ogd, ohg, proj, wb, wbt, wot, dh)


def loss_head(h, target):
    t = h.shape[0]
    nb = t // SB_BLOCK

    def body(h_ref, t_ref, dh_ref, loss_ref):
        i = pl.program_id(0)

        @pl.when(i == 0)
        def _():
            loss_ref[...] = jnp.zeros_like(loss_ref)
            dh_ref[...] = jnp.zeros_like(dh_ref)

        @pl.when(i > 0)
        def _():
            err = h_ref[...] - t_ref[...]
            dh_ref[...] = err * (1.0 / D_MODEL)
            loss_ref[...] += jnp.broadcast_to(jnp.sum(err * err) * (0.5 / D_MODEL), loss_ref.shape)

    return pl.pallas_call(
        body, grid=(nb,),
        in_specs=[pl.BlockSpec((SB_BLOCK, D_MODEL), lambda i: (i, 0)),
                  pl.BlockSpec((SB_BLOCK, D_MODEL), lambda i: (jnp.maximum(i - 1, 0), 0))],
        out_specs=[pl.BlockSpec((SB_BLOCK, D_MODEL), lambda i: (i, 0)), pl.BlockSpec((1, 128), lambda i: (0, 0))],
        out_shape=[jax.ShapeDtypeStruct((t, D_MODEL), f32), jax.ShapeDtypeStruct((1, 128), f32)],
        compiler_params=_cparams(("arbitrary",)), name="loss_head")(h, target)


def adamw(parts, w, m, v, rows_per_step, name):
    r, c = w.shape
    tr = min(rows_per_step, r)
    n_parts = parts.shape[0]

    def body(p_ref, w_ref, m_ref, v_ref, g_ref, d_ref, nm_ref, nv_ref):
        g = p_ref[0].astype(f32)
        for k in range(1, n_parts):
            g = g + p_ref[k].astype(f32)
        m_new = ADAM_B1 * m_ref[...] + (1.0 - ADAM_B1) * g
        v_new = ADAM_B2 * v_ref[...] + (1.0 - ADAM_B2) * jnp.square(g)
        m_hat = m_new / (1.0 - ADAM_B1 ** ADAM_STEP)
        v_hat = v_new / (1.0 - ADAM_B2 ** ADAM_STEP)
        g_ref[...] = g
        d_ref[...] = -ADAM_LR * (m_hat / (jnp.sqrt(v_hat) + ADAM_EPS) + ADAM_WD * w_ref[...])
        nm_ref[...] = m_new
        nv_ref[...] = v_new

    blk = pl.BlockSpec((tr, c), lambda i: (i, 0))
    return pl.pallas_call(
        body, grid=(r // tr,),
        in_specs=[pl.BlockSpec((n_parts, tr, c), lambda i: (0, i, 0)), blk, blk, blk],
        out_specs=[blk] * 4, out_shape=[jax.ShapeDtypeStruct((r, c), f32)] * 4,
        compiler_params=_cparams(("arbitrary",)), name=name)(parts, w, m, v)


def _mesh_pos():
    return lax.axis_index("x"), lax.axis_index("y"), lax.axis_index("c")


def _peer(pos, k):
    x, y, c = pos
    return (1 - x if k & 4 else x, 1 - y if k & 2 else y, 1 - c if k & 1 else c)


def _lin(pos):
    return 4 * pos[0] + 2 * pos[1] + pos[2]


N_CHIPS = 4


def _chip(pos):
    return 2 * pos[0] + pos[1]


def exchange(srcs, scatter, name):
    n = len(srcs)
    shapes = [s.shape[1:] if sc else s.shape for s, sc in zip(srcs, scatter)]
    n_slots = [N_CHIPS if sc == "chips" else N_DEV for sc in scatter]

    def body(*refs):
        src_refs, dst_refs = refs[:n], refs[n:2 * n]
        send_sems, recv_sems, local_sems = refs[2 * n:]
        me = _mesh_pos()
        sends, recvs, locals_ = [], [], []
        for t in range(n):
            slot = _chip if scatter[t] == "chips" else _lin
            own = src_refs[t].at[slot(me)] if scatter[t] else src_refs[t]
            locals_.append(pltpu.make_async_copy(own, dst_refs[t].at[slot(me)], local_sems.at[t]))
            for k in range(1, N_DEV):
                if scatter[t] == "chips" and k & 1:
                    continue
                peer = _peer(me, k)
                src = src_refs[t].at[slot(peer)] if scatter[t] else src_refs[t]
                sends.append(pltpu.make_async_remote_copy(
                    src_ref=src, dst_ref=dst_refs[t].at[slot(me)], send_sem=send_sems.at[t, k - 1],
                    recv_sem=recv_sems.at[t, k - 1], device_id=peer, device_id_type=MESH))
                recvs.append(pltpu.make_async_remote_copy(
                    src_ref=src, dst_ref=dst_refs[t].at[slot(peer)], send_sem=send_sems.at[t, k - 1],
                    recv_sem=recv_sems.at[t, k - 1], device_id=peer, device_id_type=MESH))
        for cp in locals_ + sends:
            cp.start()
        for cp in sends:
            cp.wait_send()
        for cp in recvs:
            cp.wait_recv()
        for cp in locals_:
            cp.wait()

    any_spec = pl.BlockSpec(memory_space=pl.ANY)
    return pl.pallas_call(
        body, in_specs=[any_spec] * n, out_specs=[any_spec] * n,
        out_shape=[jax.ShapeDtypeStruct((ns,) + tuple(sh), s.dtype) for ns, sh, s in zip(n_slots, shapes, srcs)],
        scratch_shapes=[pltpu.SemaphoreType.DMA((n, N_DEV - 1)), pltpu.SemaphoreType.DMA((n, N_DEV - 1)),
                        pltpu.SemaphoreType.DMA((n,))],
        compiler_params=pltpu.CompilerParams(has_side_effects=True), name=name)(*srcs)


def exchange_sibling(stacks, name):
    n = len(stacks)

    def body(*refs):
        src_refs, dst_refs = refs[:n], refs[n:2 * n]
        send_sems, recv_sems = refs[2 * n:]
        x, y, c = _mesh_pos()
        copies = [pltpu.make_async_remote_copy(
            src_ref=src_refs[t].at[2 * q + (1 - c)], dst_ref=dst_refs[t].at[q], send_sem=send_sems.at[t, q],
            recv_sem=recv_sems.at[t, q], device_id=(x, y, 1 - c), device_id_type=MESH)
            for t in range(n) for q in range(N_CHIPS)]
        for cp in copies:
            cp.start()
        for cp in copies:
            cp.wait()

    any_spec = pl.BlockSpec(memory_space=pl.ANY)
    return pl.pallas_call(
        body, in_specs=[any_spec] * n, out_specs=[any_spec] * n,
        out_shape=[jax.ShapeDtypeStruct((N_CHIPS,) + tuple(s.shape[1:]), s.dtype) for s in stacks],
        scratch_shapes=[pltpu.SemaphoreType.DMA((n, N_CHIPS)), pltpu.SemaphoreType.DMA((n, N_CHIPS))],
        compiler_params=pltpu.CompilerParams(has_side_effects=True), name=name)(*stacks)


def add_partials(stack, received, core, rows_per_step, name):
    _, r, c = stack.shape
    tr = min(rows_per_step, r)

    def body(core_ref, own_ref, rcv_ref, out_ref):
        del core_ref
        out_ref[...] = (own_ref[...].astype(f32) + rcv_ref[...].astype(f32)).astype(out_ref.dtype)

    return pl.pallas_call(
        body,
        grid_spec=pltpu.PrefetchScalarGridSpec(
            num_scalar_prefetch=1, grid=(N_CHIPS, r // tr),
            in_specs=[pl.BlockSpec((None, None, tr, c), lambda q, i, core_ref: (q, core_ref[0], i, 0)),
                      pl.BlockSpec((None, tr, c), lambda q, i, core_ref: (q, i, 0))],
            out_specs=pl.BlockSpec((None, tr, c), lambda q, i, core_ref: (q, i, 0))),
        out_shape=jax.ShapeDtypeStruct((N_CHIPS, r, c), stack.dtype),
        compiler_params=_cparams(("arbitrary", "arbitrary")), name=name)(
            core, stack.reshape(N_CHIPS, 2, r, c), received)


def gather_two_level(srcs, name):
    n = len(srcs)
    n_cp = N_DEV - 1

    def body(*refs):
        src_refs, dst_refs = refs[:n], refs[n:2 * n]
        send_sems, recv_sems, local_sems = refs[2 * n:]
        x, y, c = _mesh_pos()
        me, sibling = (x, y, c), (x, y, 1 - c)
        chips = [(1 - x, y), (x, 1 - y), (1 - x, 1 - y)]

        def copy(t, k, block, to, src=None):
            slot = dst_refs[t].at[_lin(block)]
            return pltpu.make_async_remote_copy(
                src_ref=slot if src is None else src, dst_ref=slot, send_sem=send_sems.at[t, k],
                recv_sem=recv_sems.at[t, k], device_id=to, device_id_type=MESH)

        mine, first, passed = [], [], []
        for t in range(n):
            mine.append(pltpu.make_async_copy(src_refs[t], dst_refs[t].at[_lin(me)], local_sems.at[t]))
            first.append(copy(t, 0, me, sibling, src=src_refs[t]))
            first += [copy(t, 1 + j, me, (*chip, c), src=src_refs[t]) for j, chip in enumerate(chips)]
        for cp in mine + first:
            cp.start()
        for j, chip in enumerate(chips):
            for t in range(n):
                copy(t, 1 + j, (*chip, c), me).wait_recv()
                fwd = copy(t, 4 + j, (*chip, c), sibling)
                fwd.start()
                passed.append(fwd)
        for t in range(n):
            copy(t, 0, sibling, me).wait_recv()
            for j, chip in enumerate(chips):
                copy(t, 4 + j, (*chip, 1 - c), me).wait_recv()
        for cp in first + passed:
            cp.wait_send()
        for cp in mine:
            cp.wait()

    any_spec = pl.BlockSpec(memory_space=pl.ANY)
    return pl.pallas_call(
        body, in_specs=[any_spec] * n, out_specs=[any_spec] * n,
        out_shape=[jax.ShapeDtypeStruct((N_DEV,) + tuple(s.shape), s.dtype) for s in srcs],
        scratch_shapes=[pltpu.SemaphoreType.DMA((n, n_cp)), pltpu.SemaphoreType.DMA((n, n_cp)),
                        pltpu.SemaphoreType.DMA((n,))],
        compiler_params=pltpu.CompilerParams(has_side_effects=True), name=name)(*srcs)


PACK_ROWS = 104


def _pad_rows(a, rows):
    return jnp.pad(a, ((0, rows - a.shape[0]), (0, 0)))


def _pad_lanes(a):
    return jnp.pad(a, ((0, 0), (0, 128 - a.shape[1])))


def _pack(norm_w, sbq, sbk, alog, dtb, gon, lbl, hon, loss_row):
    parts = [norm_w.reshape(32, 128), _pad_rows(sbq, 8), _pad_rows(sbk, 8), _pad_rows(_pad_lanes(alog), 8),
             _pad_rows(_pad_lanes(dtb), 8), _pad_rows(gon, 8), lbl.reshape(16, 128), _pad_rows(hon, 8),
             _pad_rows(loss_row, 8)]
    return jnp.concatenate(parts, axis=0)


def _unpack(p):
    return dict(norm_w=p[0:32].reshape(DEPTH, D_MODEL), sb_q_norm=p[32:36], sb_k_norm=p[40:44],
                gdn_a_log=p[48:52, :N_HEADS], gdn_dt_bias=p[56:60, :N_HEADS], gdn_out_norm=p[64:68],
                hgrn_lb_logits=p[72:88].reshape(DEPTH, BRANCH), hgrn_out_norm=p[88:92], loss=p[96, 0])


def _lower_bounds(logits):
    p = jax.nn.softmax(logits, axis=0)
    return jnp.cumsum(p, axis=0) - p[0:1]


def _unshard_cols(g):
    nd = g.ndim
    g = jnp.moveaxis(g, 0, nd - 2)
    return g.reshape(g.shape[:-2] + (N_DEV * g.shape[-1],))


def _shard_cols(a):
    n = a.shape[-1] // N_DEV
    return jnp.moveaxis(a.reshape(a.shape[:-1] + (N_DEV, n)), -2, 0)


def kernel(x, meta_tokens, norm_w, w_in, sb_q_norm, sb_k_norm, gdn_conv_w, gdn_a_log, gdn_dt_bias, gdn_out_norm, hgrn_lb_logits, hgrn_out_norm, w_branch, w_out, loss_target, m_meta_tokens, m_norm_w, m_w_in, m_sb_q_norm, m_sb_k_norm, m_gdn_conv_w, m_gdn_a_log, m_gdn_dt_bias, m_gdn_out_norm, m_hgrn_lb_logits, m_hgrn_out_norm, m_w_branch, m_w_out, v_meta_tokens, v_norm_w, v_w_in, v_sb_q_norm, v_sb_k_norm, v_gdn_conv_w, v_gdn_a_log, v_gdn_dt_bias, v_gdn_out_norm, v_hgrn_lb_logits, v_hgrn_out_norm, v_w_branch, v_w_out):
    g_win, g_wbr, g_wout, g_meta, g_conv = gather_two_level(
        [w_in.astype(bf16), w_branch.astype(bf16), w_out.astype(bf16), meta_tokens, gdn_conv_w], "gather_weights")
    w_full = _unshard_cols(g_win)
    w_main = jnp.concatenate([w_full[..., a:b] for a, b in W_IN_ORDER], axis=-1)
    w_small = jnp.pad(w_full[..., SMALL_OFF:SMALL_OFF + 8], ((0, 0), (0, 0), (0, N_SMALL - 8)))
    wt_main = jnp.swapaxes(w_main, 1, 2)
    wt_small = jnp.swapaxes(w_small, 1, 2)
    wbr = _unshard_cols(g_wbr)
    wbr_t = jnp.swapaxes(wbr, 2, 3)
    wout = jnp.moveaxis(g_wout, 0, 1).reshape(DEPTH, D_MODEL, D_MODEL)
    wout_t = jnp.swapaxes(wout, 1, 2)
    meta = _unshard_cols(g_meta)
    conv_w = _unshard_cols(g_conv)
    lbounds, lb_vjp = jax.vjp(_lower_bounds, hgrn_lb_logits)

    h = jnp.concatenate([jnp.zeros((PAD_FRONT, D_MODEL), f32), meta, x[0]], axis=0)
    row = lambda a: a.reshape(1, -1)
    saved = []
    for l in range(DEPTH):
        proj, small, _, xnt = inproj_fwd(h, row(norm_w[l]), w_main[l], w_small[l])
        osb = sb_fwd(proj, row(sb_q_norm[l]), row(sb_k_norm[l]))
        ypre = conv_fwd(proj, conv_w[l])
        al, dtb = _pad_lanes(row(gdn_a_log[l])), _pad_lanes(row(gdn_dt_bias[l]))
        ogd, gst = gdn_fwd(ypre, small, proj, al, dtb, row(gdn_out_norm[l]))
        ohg, hst = hgrn_fwd(proj, row(lbounds[l]), row(hgrn_out_norm[l]))
        h_next = merge_fwd(osb, proj, ogd, ohg, wbr[l], wout[l], h)
        saved.append((h, proj, small, xnt, osb, ypre, ogd, gst, ohg, hst, al, dtb))
        h = h_next

    dh, loss_row = loss_head(h, loss_target[0])

    gw_main, gw_small, gw_br, gw_out, g_conv_w = [None] * DEPTH, [None] * DEPTH, [None] * DEPTH, [None] * DEPTH, [None] * DEPTH
    g_norm, g_sbq, g_sbk, g_al, g_dt, g_gon, g_lb, g_hon = ([None] * DEPTH for _ in range(8))
    for l in reversed(range(DEPTH)):
        h_l, proj, small, xnt, osb, ypre, ogd, gst, ohg, hst, al, dtb = saved[l]
        d_osb, d_ogd, d_ohg, dproj, gw_out[l], gw_br[l] = merge_bwd(osb, proj, ogd, ohg, wbr[l], wbr_t[l], wout_t[l], dh)
        dproj, g_lb[l], g_hon[l] = hgrn_bwd(proj, row(lbounds[l]), row(hgrn_out_norm[l]), hst, d_ohg, dproj)
        d_ypre, d_small, dproj, g_al[l], g_dt[l], g_gon[l] = gdn_bwd(ypre, small, proj, al, dtb, row(gdn_out_norm[l]), gst, d_ogd, dproj)
        dproj, g_conv_w[l] = conv_bwd(proj, conv_w[l], d_ypre, dproj)
        dproj, g_sbq[l], g_sbk[l] = sb_bwd(proj, row(sb_q_norm[l]), row(sb_k_norm[l]), osb, d_osb, dproj)
        gw_main[l], gw_small[l] = inproj_bwd_w(xnt, dproj, d_small)
        dh, g_norm[l] = inproj_bwd_x(dproj, d_small, wt_main[l], wt_small[l], h_l, row(norm_w[l]), dh)

    gw_main, gw_small = jnp.stack(gw_main), jnp.stack(gw_small)
    starts = np.cumsum([0] + [b - a for a, b in W_IN_ORDER])
    pieces = sorted((a, gw_main[..., int(s):int(s) + b - a]) for (a, b), s in zip(W_IN_ORDER, starts))
    pieces.append((SMALL_OFF, gw_small[..., :8].astype(bf16)))
    gw_in = jnp.concatenate([p for _, p in sorted(pieces, key=lambda ap: ap[0])], axis=-1)
    d_lbl = lb_vjp(jnp.concatenate(g_lb, axis=0))[0]
    cat = lambda rows: jnp.concatenate(rows, axis=0)
    pack = _pack(cat(g_norm), cat(g_sbq), cat(g_sbk), cat(g_al)[:, :N_HEADS], cat(g_dt)[:, :N_HEADS], cat(g_gon),
                 d_lbl, cat(g_hon), loss_row)
    g_meta_full = dh[PAD_FRONT:FRONT]
    big = [_shard_cols(gw_in).astype(bf16).reshape(N_DEV, -1, w_in.shape[-1]),
           _shard_cols(jnp.stack(gw_br)).astype(bf16).reshape(N_DEV, -1, w_branch.shape[-1]),
           jnp.swapaxes(jnp.stack(gw_out).reshape(DEPTH, N_DEV, HEAD, D_MODEL), 0, 1).astype(bf16).reshape(N_DEV, -1, D_MODEL)]
    from_sibling = exchange_sibling(big, "exchange_sibling")
    core = lax.axis_index("c").astype(jnp.int32).reshape(1)
    chip_parts = [add_partials(s, r, core, rows, "add_partials_" + nm)
                  for s, r, rows, nm in zip(big, from_sibling, (512, 2048, 512), ("w_in", "w_branch", "w_out"))]
    r_win, r_wbr, r_wout, r_meta, r_conv, r_pack = exchange(
        chip_parts + [_shard_cols(g_meta_full), _shard_cols(jnp.stack(g_conv_w)), pack],
        ["chips", "chips", "chips", True, True, False], "exchange_grads")

    def upd(parts, w, m, v, rows, name):
        shp = w.shape
        two = (-1, shp[-1])
        outs = adamw(parts.reshape((parts.shape[0],) + w.reshape(two).shape), w.reshape(two), m.reshape(two), v.reshape(two), rows, name)
        return [o.reshape(shp) for o in outs]

    res = {}
    res["w_in"] = upd(r_win, w_in, m_w_in, v_w_in, 256, "adamw_w_in")
    res["w_branch"] = upd(r_wbr, w_branch, m_w_branch, v_w_branch, 1024, "adamw_w_branch")
    res["w_out"] = upd(r_wout, w_out, m_w_out, v_w_out, 256, "adamw_w_out")
    res["meta_tokens"] = upd(r_meta, meta_tokens, m_meta_tokens, v_meta_tokens, 16, "adamw_meta")
    res["gdn_conv_w"] = upd(r_conv, gdn_conv_w, m_gdn_conv_w, v_gdn_conv_w, 16, "adamw_conv")
    zero_row = jnp.zeros((1, 128), f32)
    w_pack = _pack(norm_w, sb_q_norm, sb_k_norm, gdn_a_log, gdn_dt_bias, gdn_out_norm, hgrn_lb_logits, hgrn_out_norm, zero_row)
    m_pack = _pack(m_norm_w, m_sb_q_norm, m_sb_k_norm, m_gdn_a_log, m_gdn_dt_bias, m_gdn_out_norm, m_hgrn_lb_logits, m_hgrn_out_norm, zero_row)
    v_pack = _pack(v_norm_w, v_sb_q_norm, v_sb_k_norm, v_gdn_a_log, v_gdn_dt_bias, v_gdn_out_norm, v_hgrn_lb_logits, v_hgrn_out_norm, zero_row)
    packed = [_unpack(o) for o in adamw(r_pack, w_pack, m_pack, v_pack, PACK_ROWS, "adamw_replicated")]
    for name in ("norm_w", "sb_q_norm", "sb_k_norm", "gdn_a_log", "gdn_dt_bias", "gdn_out_norm", "hgrn_lb_logits", "hgrn_out_norm"):
        res[name] = [p[name] for p in packed]
    loss = packed[0]["loss"]
    grad_x = dh[FRONT:][None]

    order = ["meta_tokens", "norm_w", "w_in", "sb_q_norm", "sb_k_norm", "gdn_conv_w", "gdn_a_log", "gdn_dt_bias",
             "gdn_out_norm", "hgrn_lb_logits", "hgrn_out_norm", "w_branch", "w_out"]
    return (loss, grad_x, *[res[n][0] for n in order], *[res[n][1] for n in order],
            *[res[n][2] for n in order], *[res[n][3] for n in order])
```

```python
import functools

import numpy as np
import jax
import jax.numpy as jnp
from jax import lax
from jax.experimental import pallas as pl
from jax.experimental.pallas import tpu as pltpu

f32 = jnp.float32
bf16 = jnp.bfloat16

D_MODEL = 1024
BRANCH = 512
HEAD = 128
N_HEADS = 4
CHUNK = 64
SB_BLOCK = 128
N_META = 16
FRONT = 128
PAD_FRONT = 112
EPS = 1e-6
DEPTH = 4
N_DEV = 8
N_IN = 9224
N_MAIN = 9216
N_SMALL = 128
SMALL_OFF = 4096
C_SBZ, C_MIX = 0, 512
C_GZ = 3584
C_HQ, C_HF, C_HI, C_HZ = 4096, 4608, 5120, 5632
C_GQKV = 6144
C_SBQ, C_SBK, C_SBV = 7680, 8192, 8704
W_MERGE = BRANCH + 3 * D_MODEL
W_IN_ORDER = ((1536, 2048), (6152, 9224), (3584, 4096), (4104, 6152), (2048, 3584), (0, 1536))

ADAM_LR, ADAM_B1, ADAM_B2, ADAM_EPS, ADAM_WD, ADAM_STEP = 0.001, 0.9, 0.999, 1e-08, 0.01, 10

VMEM_LIMIT = 56 * 1024 * 1024
MESH = pl.DeviceIdType.MESH

NN = ((1,), (0,))
NT = ((1,), (1,))
TN = ((0,), (0,))


def _dot(a, b, dims=NN):
    return lax.dot_general(a.astype(bf16), b.astype(bf16), (dims, ((), ())), preferred_element_type=f32)


@jax.custom_vjp
def mm(a, b):
    return _dot(a, b, NN)


mm.defvjp(lambda a, b: (_dot(a, b, NN), (a, b)),
          lambda r, g: (_dot(g, r[1], NT), _dot(r[0], g, TN)))


@jax.custom_vjp
def mm_nt(a, b):
    return _dot(a, b, NT)


mm_nt.defvjp(lambda a, b: (_dot(a, b, NT), (a, b)),
             lambda r, g: (_dot(g, r[1], NN), _dot(g, r[0], TN)))


@jax.custom_vjp
def mm_tn(a, b):
    return _dot(a, b, TN)


mm_tn.defvjp(lambda a, b: (_dot(a, b, TN), (a, b)),
             lambda r, g: (_dot(r[1], g, NT), _dot(r[0], g, NN)))


def _split2(x):
    hi = x.astype(bf16)
    lo = (x - hi.astype(f32)).astype(bf16)
    return hi, lo


def _cdot(c, x, dims):
    hi, lo = _split2(x)
    return (lax.dot_general(c, hi, (dims, ((), ())), preferred_element_type=f32)
            + lax.dot_general(c, lo, (dims, ((), ())), preferred_element_type=f32))


@jax.custom_vjp
def cmm(c, x):
    return _cdot(c, x, NN)


cmm.defvjp(lambda c, x: (_cdot(c, x, NN), c),
           lambda c, g: (jnp.zeros_like(c), _cdot(c, g, TN)))


def _sigmoid(x):
    return jax.nn.sigmoid(x)


def _silu(x):
    return x * jax.nn.sigmoid(x)


def _softplus(x):
    return jnp.maximum(x, 0.0) + jnp.log(1.0 + jnp.exp(-jnp.abs(x)))


def _rms(x, w):
    return x * lax.rsqrt(jnp.mean(x * x, axis=-1, keepdims=True) + EPS) * w


def _cparams(sem=None):
    return pltpu.CompilerParams(dimension_semantics=sem, vmem_limit_bytes=VMEM_LIMIT)


TILES_FWD = (1664, 1024)
TILES_BWD_X = (832, 2304)
TILES_BWD_W = (1664, 1024)


def _row_tile(t, want, unit):
    return max(d for d in range(unit, want + 1, unit) if t % d == 0)


def inproj_fwd(h, nw, w_main, w_small):
    t = h.shape[0]
    TM_IN, TN_IN = _row_tile(t, TILES_FWD[0], 128), TILES_FWD[1]

    def body(h_ref, nw_ref, w_ref, ws_ref, proj_ref, small_ref, xn_ref, xnt_ref):
        @pl.when(pl.program_id(1) == 0)
        def _():
            xn = _rms(h_ref[...], nw_ref[...])
            xn_ref[...] = xn.astype(bf16)
            xnt_ref[...] = jnp.transpose(xn).astype(bf16)
            small_ref[...] = _dot(xn, ws_ref[...])

        proj_ref[...] = jnp.dot(xn_ref[...], w_ref[...], preferred_element_type=f32)

    return pl.pallas_call(
        body, grid=(t // TM_IN, N_MAIN // TN_IN),
        in_specs=[pl.BlockSpec((TM_IN, D_MODEL), lambda i, j: (i, 0)),
                  pl.BlockSpec((1, D_MODEL), lambda i, j: (0, 0)),
                  pl.BlockSpec((D_MODEL, TN_IN), lambda i, j: (0, j)),
                  pl.BlockSpec((D_MODEL, N_SMALL), lambda i, j: (0, 0))],
        out_specs=[pl.BlockSpec((TM_IN, TN_IN), lambda i, j: (i, j)),
                   pl.BlockSpec((TM_IN, N_SMALL), lambda i, j: (i, 0)),
                   pl.BlockSpec((TM_IN, D_MODEL), lambda i, j: (i, 0)),
                   pl.BlockSpec((D_MODEL, TM_IN), lambda i, j: (0, i))],
        out_shape=[jax.ShapeDtypeStruct((t, N_MAIN), f32), jax.ShapeDtypeStruct((t, N_SMALL), f32),
                   jax.ShapeDtypeStruct((t, D_MODEL), bf16), jax.ShapeDtypeStruct((D_MODEL, t), bf16)],
        compiler_params=_cparams(("arbitrary", "arbitrary")), name="inproj_fwd")(h, nw, w_main, w_small)


def inproj_bwd_x(dproj, dsmall, wt_main, wt_small, h, nw, dh_out):
    t = h.shape[0]
    TM_IN, TN_IN = _row_tile(t, TILES_BWD_X[0], 64), TILES_BWD_X[1]
    nk = N_MAIN // TN_IN

    def body(dp_ref, ds_ref, wt_ref, wts_ref, h_ref, nw_ref, dho_ref, dhi_ref, dnw_ref, acc):
        i, k = pl.program_id(0), pl.program_id(1)

        @pl.when(k == 0)
        def _():
            acc[...] = _dot(ds_ref[...], wts_ref[...])

        acc[...] += _dot(dp_ref[...], wt_ref[...])

        @pl.when(k == nk - 1)
        def _():
            x = h_ref[...]
            r = lax.rsqrt(jnp.mean(x * x, axis=-1, keepdims=True) + EPS)
            xh = x * r
            dxn = acc[...]
            dxh = dxn * nw_ref[...]
            dhi_ref[...] = dho_ref[...] + r * (dxh - xh * jnp.mean(dxh * xh, axis=-1, keepdims=True))
            part = jnp.sum(dxn * xh, axis=0, keepdims=True)

            @pl.when(i == 0)
            def _():
                dnw_ref[...] = part

            @pl.when(i > 0)
            def _():
                dnw_ref[...] += part

    return pl.pallas_call(
        body, grid=(t // TM_IN, nk),
        in_specs=[pl.BlockSpec((TM_IN, TN_IN), lambda i, k: (i, k)),
                  pl.BlockSpec((TM_IN, N_SMALL), lambda i, k: (i, 0)),
                  pl.BlockSpec((TN_IN, D_MODEL), lambda i, k: (k, 0)),
                  pl.BlockSpec((N_SMALL, D_MODEL), lambda i, k: (0, 0)),
                  pl.BlockSpec((TM_IN, D_MODEL), lambda i, k: (i, 0)),
                  pl.BlockSpec((1, D_MODEL), lambda i, k: (0, 0)),
                  pl.BlockSpec((TM_IN, D_MODEL), lambda i, k: (i, 0))],
        out_specs=[pl.BlockSpec((TM_IN, D_MODEL), lambda i, k: (i, 0)),
                   pl.BlockSpec((1, D_MODEL), lambda i, k: (0, 0))],
        out_shape=[jax.ShapeDtypeStruct((t, D_MODEL), f32), jax.ShapeDtypeStruct((1, D_MODEL), f32)],
        scratch_shapes=[pltpu.VMEM((TM_IN, D_MODEL), f32)],
        compiler_params=_cparams(("arbitrary", "arbitrary")), name="inproj_bwd_x")(
            dproj, dsmall, wt_main, wt_small, h, nw, dh_out)


def inproj_bwd_w(xnt, dproj, dsmall):
    t = xnt.shape[1]
    TM_IN, TN_IN = _row_tile(t, TILES_BWD_W[0], 128), TILES_BWD_W[1]
    nt = t // TM_IN

    def body(xnt_ref, dp_ref, ds_ref, dw_ref, dws_ref, acc):
        n, s = pl.program_id(0), pl.program_id(1)
        part = _dot(xnt_ref[...], dp_ref[...])

        @pl.when(s == 0)
        def _():
            acc[...] = part

        @pl.when(s > 0)
        def _():
            acc[...] += part

        @pl.when(s == nt - 1)
        def _():
            dw_ref[...] = acc[...].astype(bf16)

        @pl.when(n == 0)
        def _():
            ps = _dot(xnt_ref[...], ds_ref[...])

            @pl.when(s == 0)
            def _():
                dws_ref[...] = ps

            @pl.when(s > 0)
            def _():
                dws_ref[...] += ps

    return pl.pallas_call(
        body, grid=(N_MAIN // TN_IN, nt),
        in_specs=[pl.BlockSpec((D_MODEL, TM_IN), lambda n, s: (0, s)),
                  pl.BlockSpec((TM_IN, TN_IN), lambda n, s: (s, n)),
                  pl.BlockSpec((TM_IN, N_SMALL), lambda n, s: (s, 0))],
        out_specs=[pl.BlockSpec((D_MODEL, TN_IN), lambda n, s: (0, n)),
                   pl.BlockSpec((D_MODEL, N_SMALL), lambda n, s: (0, 0))],
        out_shape=[jax.ShapeDtypeStruct((D_MODEL, N_MAIN), bf16), jax.ShapeDtypeStruct((D_MODEL, N_SMALL), f32)],
        scratch_shapes=[pltpu.VMEM((D_MODEL, TN_IN), f32)],
        compiler_params=_cparams(("arbitrary", "arbitrary")), name="inproj_bwd_w")(xnt, dproj, dsmall)


SB_SCALE = HEAD ** -0.5


SB_SUB = 3
SB_KS = SB_SUB * SB_BLOCK


SB_PADR = SB_KS - SB_BLOCK


def _sb_padded(t):
    return t + SB_PADR


def _sb_rows(i, d):
    start = (i + 1) * SB_BLOCK - (d + 1) * SB_KS
    return start, pl.ds(pl.multiple_of(start + SB_PADR, SB_BLOCK), SB_KS)


SB_PREP_ROWS = 5 * SB_BLOCK


def _sb_prep(k_ref, v_ref, kw_ref, kn_scr, vb_scr, nb):
    def prep(b, c):
        rows = pl.ds(pl.multiple_of(b * SB_PREP_ROWS, SB_BLOCK), SB_PREP_ROWS)
        pad_rows = pl.ds(pl.multiple_of(SB_PADR + b * SB_PREP_ROWS, SB_BLOCK), SB_PREP_ROWS)
        kn_scr[pad_rows, :] = _rms(k_ref[rows, :], kw_ref[...]).astype(bf16)
        vb_scr[pad_rows, :] = v_ref[rows, :].astype(bf16)
        return c

    lax.fori_loop(0, nb * SB_BLOCK // SB_PREP_ROWS, prep, 0)
    kn_scr[:SB_PADR, :] = jnp.zeros((SB_PADR, HEAD), bf16)
    vb_scr[:SB_PADR, :] = jnp.zeros((SB_PADR, HEAD), bf16)


def _tri_ext(cmp):
    r = lax.broadcasted_iota(jnp.int32, (SB_BLOCK, 2 * SB_BLOCK), 0)
    c = lax.broadcasted_iota(jnp.int32, (SB_BLOCK, 2 * SB_BLOCK), 1)
    return jnp.where((c >= SB_BLOCK) | cmp(r, c), 1.0, 0.0).astype(bf16)


def _sb_suffix(x, carry, tri_ext):
    hi, lo = _split2(x)
    parts = [p[:, c * SB_BLOCK:(c + 1) * SB_BLOCK] for p in (hi, lo) for c in range(SB_SUB)]
    w = jnp.dot(jnp.concatenate(parts, axis=0), tri_ext, preferred_element_type=f32)
    outs = [None] * SB_SUB
    for c in reversed(range(SB_SUB)):
        blk = w[c * SB_BLOCK:(c + 1) * SB_BLOCK] + w[(SB_SUB + c) * SB_BLOCK:(SB_SUB + c + 1) * SB_BLOCK]
        outs[c] = carry + blk[:, :SB_BLOCK]
        carry = carry + blk[:, SB_BLOCK:]
    return jnp.concatenate(outs, axis=1), carry


def _sb_scores(qn, kt, i, start, masked):
    z = lax.dot_general(qn, kt, (NT, ((), ())), preferred_element_type=f32) * SB_SCALE
    lsz = jnp.minimum(z, 0.0) - jnp.log(1.0 + jnp.exp(-jnp.abs(z)))
    lk = lsz - z
    mask = None
    if masked:
        t_idx = i * SB_BLOCK + lax.broadcasted_iota(jnp.int32, (SB_BLOCK, SB_KS), 0)
        s_idx = start + lax.broadcasted_iota(jnp.int32, (SB_BLOCK, SB_KS), 1)
        mask = (s_idx < t_idx) & (s_idx >= PAD_FRONT)
        lk = jnp.where(mask, lk, 0.0)
    return mask, lsz, lk


SB_DEAD = -104.0


def _sb_walk(i, tile, carry, first_done=False):
    n = (i + SB_SUB) // SB_SUB
    live = lambda c: jnp.max(c[1]) > SB_DEAD
    if not first_done:
        carry = tile(0, carry, True)
    _, carry = lax.while_loop(lambda st: (st[0] <= n - 2) & live(st[1]),
                              lambda st: (st[0] + 1, tile(st[0], st[1], False)), (1, carry))
    return lax.cond((n >= 2) & live(carry), lambda c: tile(n - 1, c, True), lambda c: c, carry)


SB_QB = 5
SB_QROWS = SB_QB * SB_BLOCK


def _drain(gen):
    return _interleave([gen])[0]


def sb_fwd(proj, qw, kw):
    t = proj.shape[0]
    nb = t // SB_BLOCK

    def body(q_ref, k_ref, v_ref, qw_ref, kw_ref, o_ref, kn_scr, vb_scr):
        grp = pl.program_id(1)

        @pl.when(grp == 0)
        def _():
            _sb_prep(k_ref, v_ref, kw_ref, kn_scr, vb_scr, nb)

        u_ex = _tri_ext(lambda r, c: r > c)

        def tile(i, qn, jb, carry, masked):
            acc, r_carry = carry
            start, rows = _sb_rows(i, jb)
            mask, lsz, lk = _sb_scores(qn, kn_scr[rows, :], i, start, masked)
            yield
            passed, r_carry = _sb_suffix(lk, r_carry, u_ex)
            yield
            a = jnp.exp(lsz + passed)
            if masked:
                a = jnp.where(mask, a, 0.0)
            a_hi, a_lo = _split2(a)
            both = jnp.dot(jnp.concatenate([a_hi, a_lo], axis=0), vb_scr[rows, :], preferred_element_type=f32)
            yield
            return acc + (both[:SB_BLOCK] + both[SB_BLOCK:]), r_carry

        zeros = jnp.zeros((SB_BLOCK, HEAD), f32)
        blocks = [(grp * SB_QB + b, _rms(q_ref[b * SB_BLOCK:(b + 1) * SB_BLOCK, :], qw_ref[...]).astype(bf16))
                  for b in range(SB_QB)]
        firsts = _interleave([tile(i, qn, 0, (zeros, zeros), True) for i, qn in blocks])
        for b, ((i, qn), carry) in enumerate(zip(blocks, firsts)):
            step = lambda jb, c, masked, i=i, qn=qn: _drain(tile(i, qn, jb, c, masked))
            acc, _ = _sb_walk(i, step, carry, first_done=True)
            o_ref[b * SB_BLOCK:(b + 1) * SB_BLOCK, :] = acc

    qb, cb, vb = C_SBQ // HEAD, C_SBK // HEAD, C_SBV // HEAD
    return pl.pallas_call(
        body, grid=(N_HEADS, nb // SB_QB),
        in_specs=[pl.BlockSpec((SB_QROWS, HEAD), lambda h, i: (i, qb + h)),
                  pl.BlockSpec((t, HEAD), lambda h, i: (0, cb + h)),
                  pl.BlockSpec((t, HEAD), lambda h, i: (0, vb + h)),
                  pl.BlockSpec((1, HEAD), lambda h, i: (0, 0)),
                  pl.BlockSpec((1, HEAD), lambda h, i: (0, 0))],
        out_specs=pl.BlockSpec((SB_QROWS, HEAD), lambda h, i: (i, h)),
        out_shape=jax.ShapeDtypeStruct((t, BRANCH), f32),
        scratch_shapes=[pltpu.VMEM((_sb_padded(t), HEAD), bf16), pltpu.VMEM((_sb_padded(t), HEAD), bf16)],
        compiler_params=_cparams(("arbitrary", "arbitrary")), name="sb_fwd")(proj, proj, proj, qw, kw)


def sb_bwd(proj, qw, kw, o, do, dproj):
    t = proj.shape[0]
    nb = t // SB_BLOCK

    def body(q_ref, k_ref, v_ref, qw_ref, kw_ref, o_ref, do_ref, _, dp_ref, dqw_ref, dkw_ref,
             kn_scr, vb_scr, dk_acc, dv_acc, dq_stage, kv_stage, dq_sems, kv_sems):
        h, grp = pl.program_id(0), pl.program_id(1)
        n_grp = nb // SB_QB
        step = h * n_grp + grp
        slot = step % 2

        def dq_copy(sl, head):
            return pltpu.make_async_copy(
                dq_stage.at[sl], dp_ref.at[pl.ds(pl.multiple_of(grp * SB_QROWS, SB_QROWS), SB_QROWS),
                                           pl.ds(C_SBQ + head * HEAD, HEAD)], dq_sems.at[sl])

        @pl.when(grp == 0)
        def _():
            _sb_prep(k_ref, v_ref, kw_ref, kn_scr, vb_scr, nb)
            dk_acc[...] = jnp.zeros_like(dk_acc)
            dv_acc[...] = jnp.zeros_like(dv_acc)

        @pl.when((grp == 0) & (h == 0))
        def _():
            dqw_ref[...] = jnp.zeros_like(dqw_ref)
            dkw_ref[...] = jnp.zeros_like(dkw_ref)

        u_ex = _tri_ext(lambda r, c: r > c)
        u_in = _tri_ext(lambda r, c: r >= c)

        def tile(i, qn, dob, d_row, jb, carry, masked):
            dq, r_carry, f_carry = carry
            start, rows = _sb_rows(i, jb)
            kt = kn_scr[rows, :]
            vt = vb_scr[rows, :]
            mask, lsz, lk = _sb_scores(qn, kt, i, start, masked)
            da = lax.dot_general(dob, vt, (NT, ((), ())), preferred_element_type=f32)
            yield
            passed, r_carry = _sb_suffix(lk, r_carry, u_ex)
            yield
            a = jnp.exp(lsz + passed)
            if masked:
                a = jnp.where(mask, a, 0.0)
            e = a * da
            e_suf, f_carry = _sb_suffix(e, f_carry, u_in)
            yield
            sg = jnp.exp(lsz)
            dz = (e * (1.0 - sg) - (d_row - e_suf) * sg) * SB_SCALE
            if masked:
                dz = jnp.where(mask, dz, 0.0)
            dzb = dz.astype(bf16)
            dq = dq + jnp.dot(dzb, kt, preferred_element_type=f32)
            dk_acc[rows, :] += lax.dot_general(dzb, qn, (TN, ((), ())), preferred_element_type=f32)
            dv_acc[rows, :] += lax.dot_general(a.astype(bf16), dob, (TN, ((), ())), preferred_element_type=f32)
            yield
            return dq, r_carry, f_carry

        zeros = jnp.zeros((SB_BLOCK, HEAD), f32)
        blocks = []
        for b in range(SB_QB):
            rows_b = slice(b * SB_BLOCK, (b + 1) * SB_BLOCK)
            q = q_ref[rows_b, :]
            rq = lax.rsqrt(jnp.mean(q * q, axis=-1, keepdims=True) + EPS)
            qh = q * rq
            dob = do_ref[rows_b, :].astype(bf16)
            d_row = jnp.sum(dob.astype(f32) * o_ref[rows_b, :], axis=-1, keepdims=True)
            blocks.append((grp * SB_QB + b, (qh * qw_ref[...]).astype(bf16), dob, d_row, rq, qh))
        firsts = _interleave([tile(i, qn, dob, d_row, 0, (zeros, zeros, zeros), True)
                              for i, qn, dob, d_row, _, _ in blocks])

        @pl.when(step >= 2)
        def _():
            dq_copy(slot, 0).wait()

        for b, ((i, qn, dob, d_row, rq, qh), carry) in enumerate(zip(blocks, firsts)):
            one = lambda jb, c, masked, i=i, qn=qn, dob=dob, d_row=d_row: _drain(tile(i, qn, dob, d_row, jb, c, masked))
            dqn, _, _ = _sb_walk(i, one, carry, first_done=True)
            gq = dqn * qw_ref[...]
            dqw_ref[...] += jnp.sum(dqn * qh, axis=0, keepdims=True)
            dq_stage[slot, b * SB_BLOCK:(b + 1) * SB_BLOCK, :] = (
                rq * (gq - qh * jnp.mean(gq * qh, axis=-1, keepdims=True))).astype(bf16)
        for head in range(N_HEADS):
            @pl.when(h == head)
            def _(head=head):
                dq_copy(slot, head).start()

        @pl.when(grp == n_grp - 1)
        def _():
            def fin(b, c):
                rows = pl.ds(pl.multiple_of(b * SB_PREP_ROWS, SB_BLOCK), SB_PREP_ROWS)
                pad_rows = pl.ds(pl.multiple_of(SB_PADR + b * SB_PREP_ROWS, SB_BLOCK), SB_PREP_ROWS)
                kk = k_ref[rows, :]
                rk = lax.rsqrt(jnp.mean(kk * kk, axis=-1, keepdims=True) + EPS)
                kh = kk * rk
                dkn = dk_acc[pad_rows, :]
                gk = dkn * kw_ref[...]
                kv_stage[0, rows, :] = (rk * (gk - kh * jnp.mean(gk * kh, axis=-1, keepdims=True))).astype(bf16)
                kv_stage[1, rows, :] = dv_acc[pad_rows, :].astype(bf16)
                dkw_ref[...] += jnp.sum(dkn * kh, axis=0, keepdims=True)
                return c

            lax.fori_loop(0, nb * SB_BLOCK // SB_PREP_ROWS, fin, 0)
            for head in range(N_HEADS):
                @pl.when(h == head)
                def _(head=head):
                    outs = [pltpu.make_async_copy(kv_stage.at[n], dp_ref.at[:, pl.ds(c0 + head * HEAD, HEAD)], kv_sems.at[n])
                            for n, c0 in enumerate((C_SBK, C_SBV))]
                    for cp in outs:
                        cp.start()
                    for cp in outs:
                        cp.wait()

        @pl.when(step == N_HEADS * n_grp - 1)
        def _():
            dq_copy(1 - slot, 0).wait()
            dq_copy(slot, 0).wait()

    qb, cb, vb = C_SBQ // HEAD, C_SBK // HEAD, C_SBV // HEAD
    blk = pl.BlockSpec((SB_QROWS, HEAD), lambda h, i: (i, h))
    wsp = pl.BlockSpec((1, HEAD), lambda h, i: (0, 0))
    any_spec = pl.BlockSpec(memory_space=pl.ANY)
    return pl.pallas_call(
        body, grid=(N_HEADS, nb // SB_QB),
        in_specs=[pl.BlockSpec((SB_QROWS, HEAD), lambda h, i: (i, qb + h)),
                  pl.BlockSpec((t, HEAD), lambda h, i: (0, cb + h)),
                  pl.BlockSpec((t, HEAD), lambda h, i: (0, vb + h)), wsp, wsp, blk, blk, any_spec],
        out_specs=[any_spec, wsp, wsp],
        out_shape=[jax.ShapeDtypeStruct((t, N_MAIN), bf16)] + [jax.ShapeDtypeStruct((1, HEAD), f32)] * 2,
        scratch_shapes=[pltpu.VMEM((_sb_padded(t), HEAD), bf16), pltpu.VMEM((_sb_padded(t), HEAD), bf16),
                        pltpu.VMEM((_sb_padded(t), HEAD), f32), pltpu.VMEM((_sb_padded(t), HEAD), f32),
                        pltpu.VMEM((2, SB_QROWS, HEAD), bf16), pltpu.VMEM((2, t, HEAD), bf16),
                        pltpu.SemaphoreType.DMA((2,)), pltpu.SemaphoreType.DMA((2,))],
        input_output_aliases={7: 0},
        compiler_params=_cparams(("arbitrary", "arbitrary")), name="sb_bwd")(proj, proj, proj, qw, kw, o, do, dproj)


TM_CONV = 640
CONV_W = 4
GQKV = 3 * BRANCH


def conv_fwd(proj, cw):
    t = proj.shape[0]
    halo_blocks = TM_CONV // 8

    def body(x0_ref, x1_ref, x2_ref, p0_ref, p1_ref, p2_ref, cw_ref, y_ref):
        i = pl.program_id(0)
        for s, (x_ref, p_ref) in enumerate(((x0_ref, p0_ref), (x1_ref, p1_ref), (x2_ref, p2_ref))):
            prev = jnp.where(i > 0, p_ref[...], 0.0)
            xx = jnp.concatenate([prev, x_ref[...]], axis=0)
            cols = slice(s * BRANCH, (s + 1) * BRANCH)
            y = xx[8:] * cw_ref[CONV_W - 1:CONV_W, cols]
            for k in range(CONV_W - 1):
                y = y + pltpu.roll(xx, CONV_W - 1 - k, 0)[8:] * cw_ref[k:k + 1, cols]
            y_ref[:, cols] = y

    c0 = C_GQKV // BRANCH
    xs = [pl.BlockSpec((TM_CONV, BRANCH), functools.partial(lambda i, s: (i, c0 + s), s=s)) for s in range(3)]
    ps = [pl.BlockSpec((8, BRANCH), functools.partial(lambda i, s: (jnp.maximum(i * halo_blocks - 1, 0), c0 + s), s=s))
          for s in range(3)]
    return pl.pallas_call(
        body, grid=(t // TM_CONV,),
        in_specs=xs + ps + [pl.BlockSpec((CONV_W, GQKV), lambda i: (0, 0))],
        out_specs=pl.BlockSpec((TM_CONV, GQKV), lambda i: (i, 0)),
        out_shape=jax.ShapeDtypeStruct((t, GQKV), f32),
        compiler_params=_cparams(("arbitrary",)), name="conv_fwd")(proj, proj, proj, proj, proj, proj, cw)


def conv_bwd(proj, cw, dy, dproj):
    t = proj.shape[0]
    nt = t // TM_CONV
    halo_blocks = TM_CONV // 8

    def body(x0_ref, x1_ref, x2_ref, p0_ref, p1_ref, p2_ref, cw_ref, dy_ref, dyn_ref, _, dx_ref, dw_ref):
        i = pl.program_id(0)

        @pl.when(i == 0)
        def _():
            dw_ref[...] = jnp.zeros_like(dw_ref)

        nxt = jnp.where(i < nt - 1, dyn_ref[...], 0.0)
        dyy = jnp.concatenate([dy_ref[...], nxt], axis=0)
        n_rows = TM_CONV + 8
        dx = dyy[:TM_CONV] * cw_ref[CONV_W - 1:CONV_W, :]
        for k in range(CONV_W - 1):
            sh = CONV_W - 1 - k
            dx = dx + pltpu.roll(dyy, n_rows - sh, 0)[:TM_CONV] * cw_ref[k:k + 1, :]
        dx_ref[...] = dx.astype(bf16)
        dy_c = dy_ref[...]
        for s, (x_ref, p_ref) in enumerate(((x0_ref, p0_ref), (x1_ref, p1_ref), (x2_ref, p2_ref))):
            prev = jnp.where(i > 0, p_ref[...], 0.0)
            xx = jnp.concatenate([prev, x_ref[...]], axis=0)
            cols = slice(s * BRANCH, (s + 1) * BRANCH)
            for k in range(CONV_W):
                sh = CONV_W - 1 - k
                xs = xx[8:] if sh == 0 else pltpu.roll(xx, sh, 0)[8:]
                dw_ref[k:k + 1, cols] += jnp.sum(xs * dy_c[:, cols], axis=0, keepdims=True)

    c0 = C_GQKV // BRANCH
    xs = [pl.BlockSpec((TM_CONV, BRANCH), functools.partial(lambda i, s: (i, c0 + s), s=s)) for s in range(3)]
    ps = [pl.BlockSpec((8, BRANCH), functools.partial(lambda i, s: (jnp.maximum(i * halo_blocks - 1, 0), c0 + s), s=s))
          for s in range(3)]
    return pl.pallas_call(
        body, grid=(nt,),
        in_specs=xs + ps + [pl.BlockSpec((CONV_W, GQKV), lambda i: (0, 0)),
                            pl.BlockSpec((TM_CONV, GQKV), lambda i: (i, 0)),
                            pl.BlockSpec((8, GQKV), lambda i: (jnp.minimum((i + 1) * halo_blocks, nt * halo_blocks - 1), 0)),
                            pl.BlockSpec(memory_space=pl.ANY)],
        out_specs=[pl.BlockSpec((TM_CONV, GQKV), lambda i: (i, C_GQKV // GQKV)), pl.BlockSpec((CONV_W, GQKV), lambda i: (0, 0))],
        out_shape=[jax.ShapeDtypeStruct((t, N_MAIN), bf16), jax.ShapeDtypeStruct((CONV_W, GQKV), f32)],
        input_output_aliases={9: 0},
        compiler_params=_cparams(("arbitrary",)), name="conv_bwd")(proj, proj, proj, proj, proj, proj, cw, dy, dy, dproj)


def _iota2(n, m, d):
    return lax.broadcasted_iota(jnp.int32, (n, m), d)


def _lane_pick(row_or_mat, idx):
    lanes = lax.broadcasted_iota(jnp.int32, row_or_mat.shape, row_or_mat.ndim - 1)
    return jnp.sum(jnp.where(lanes == idx, row_or_mat, 0.0), axis=-1, keepdims=True)


def _cumsum_consts():
    i = np.arange(CHUNK)
    incl = i[None, :] <= i[:, None]
    suf = i[None, :] > i[:, None]
    return np.concatenate([incl, suf], 0).astype(np.float32)


HG_LEVELS = (64, 32, 16, 8, 4, 2)


def _hgrn_consts():
    i = np.arange(CHUNK)
    rows = [i[None, :] <= i[:, None], i[None, :] > i[:, None]]
    for b in HG_LEVELS:
        ref = (i // b) * b + b // 2 - 1
        second = (i % b) >= b // 2
        rows.append((i[None, :] > ref[:, None]) & (i[None, :] <= i[:, None]) & second[:, None])
        rows.append((i[None, :] > i[:, None]) & (i[None, :] <= ref[:, None]) & (~second)[:, None])
    return np.concatenate(rows, 0).astype(np.float32)


N_SQUARINGS = 5


def _solve_chain(ms, rhss):
    xs = [r - mm(m, r) for m, r in zip(ms, rhss)]
    powers = [list(ms)]
    for _ in range(N_SQUARINGS):
        powers.append([mm(p, p) for p in powers[-1]])
        xs = [x + mm(p, x) for p, x in zip(powers[-1], xs)]
    return tuple(xs), powers


@jax.custom_vjp
def unit_lower_solve_multi(ms, rhss):
    return _solve_chain(ms, rhss)[0]


def _solve_fwd(ms, rhss):
    xs, powers = _solve_chain(ms, rhss)
    return xs, (powers, xs)


def _solve_bwd(res, gs):
    powers, xs = res
    ys = [g - mm_tn(p, g) for p, g in zip(powers[0], gs)]
    for ps in powers[1:]:
        ys = [y + mm_tn(p, y) for p, y in zip(ps, ys)]
    return tuple(-mm_nt(y, x) for y, x in zip(ys, xs)), tuple(ys)


unit_lower_solve_multi.defvjp(_solve_fwd, _solve_bwd)

HC = N_HEADS * CHUNK
BATCH0 = ((0,), (0,))


def _bdot(a, b, contract):
    return lax.dot_general(a.astype(bf16), b.astype(bf16), (contract, BATCH0), preferred_element_type=f32)


B_NN = ((2,), (1,))
B_NT = ((2,), (2,))
B_TN = ((1,), (1,))


@jax.custom_vjp
def bmm(a, b):
    return _bdot(a, b, B_NN)


bmm.defvjp(lambda a, b: (_bdot(a, b, B_NN), (a, b)),
           lambda r, g: (_bdot(g, r[1], B_NT), _bdot(r[0], g, B_TN)))


@jax.custom_vjp
def bmm_nt(a, b):
    return _bdot(a, b, B_NT)


bmm_nt.defvjp(lambda a, b: (_bdot(a, b, B_NT), (a, b)),
              lambda r, g: (_bdot(g, r[1], B_NN), _bdot(g, r[0], B_TN)))


@jax.custom_vjp
def bmm_tn(a, b):
    return _bdot(a, b, B_TN)


bmm_tn.defvjp(lambda a, b: (_bdot(a, b, B_TN), (a, b)),
              lambda r, g: (_bdot(r[1], g, B_NT), _bdot(r[0], g, B_NN)))


def _stack_heads(x):
    return jnp.concatenate([x[:, h * HEAD:(h + 1) * HEAD] for h in range(N_HEADS)], axis=0)


def _unstack_heads(x):
    return jnp.concatenate([x[h * CHUNK:(h + 1) * CHUNK] for h in range(N_HEADS)], axis=1)


REC_CHUNKS = 5
REC_ROWS = REC_CHUNKS * CHUNK


def _interleave(gens):
    n = len(gens)
    sends, results, done = [None] * n, [None] * n, [False] * n
    while not all(done):
        asks = []
        for j in range(n):
            if done[j]:
                continue
            try:
                ask = gens[j].send(sends[j])
                if ask is not None:
                    asks.append((j, ask))
            except StopIteration as stop:
                results[j], done[j] = stop.value, True
            sends[j] = None
        if asks:
            xs = unit_lower_solve_multi(tuple(a[1] for _, a in asks), tuple(a[2] for _, a in asks))
            for (j, _), x in zip(asks, xs):
                sends[j] = x
    return results


def _chunks_of(a):
    return [a[j * CHUNK:(j + 1) * CHUNK] for j in range(REC_CHUNKS)]


def _gdn_step(state, ypre, small, gz, a_log, dt_b, on_w, c2, first_chunk):
    masks = _gdn_masks()
    gens = [_gdn_intra(y, s, a_log, dt_b, c2, _vmask(first_chunk + j), masks)
            for j, (y, s) in enumerate(zip(_chunks_of(ypre), _chunks_of(small)))]
    outs = []
    for intra, z in zip(_interleave(gens), _chunks_of(gz)):
        state, o = _gdn_inter(state, intra, z, on_w)
        outs.append(o)
    return state, jnp.concatenate(outs, axis=0)


def _gdn_inter(state, intra, gz, on_w):
    u, w, aqk, q_dec, k_dec, g_last = intra
    per_head = lambda a: a.reshape(N_HEADS, CHUNK, HEAD)
    v_new = u - bmm(per_head(w), state).reshape(HC, HEAD)
    o = bmm(per_head(q_dec), state).reshape(HC, HEAD) + mm(aqk, v_new)
    new_state = state * g_last + bmm_tn(per_head(k_dec), per_head(v_new))
    return new_state, _unstack_heads(_rms(o, on_w)) * _silu(gz)


def _gdn_masks():
    r = _iota2(HC, HC, 0)
    c = _iota2(HC, HC, 1)
    same_head = (r >> (CHUNK.bit_length() - 1)) == (c >> (CHUNK.bit_length() - 1))
    return same_head & (r >= c), same_head & (r > c)


def _gdn_intra(ypre, small, a_log, dt_b, c2, vm, masks):
    causal, strict = masks
    q = _silu(_stack_heads(ypre[:, :BRANCH]))
    k = _silu(_stack_heads(ypre[:, BRANCH:2 * BRANCH]))
    v = _silu(_stack_heads(ypre[:, 2 * BRANCH:]))
    q = q * lax.rsqrt(jnp.sum(q * q, axis=-1, keepdims=True) + EPS) * (HEAD ** -0.5)
    k = k * lax.rsqrt(jnp.sum(k * k, axis=-1, keepdims=True) + EPS)
    col = lambda f: jnp.concatenate([f(h) for h in range(N_HEADS)], axis=0)
    chunk_col = lambda x: jnp.broadcast_to(x, (CHUNK, 1))
    beta = _sigmoid(col(lambda h: _lane_pick(small, h))) * col(lambda h: vm)
    g = (-jnp.exp(col(lambda h: chunk_col(_lane_pick(a_log, h))))
         * _softplus(col(lambda h: _lane_pick(small, N_HEADS + h)) + col(lambda h: chunk_col(_lane_pick(dt_b, h)))))
    g_l = _unstack_heads(jnp.broadcast_to(g, (HC, HEAD)))
    e2 = cmm(c2, g_l)
    yield
    gc = _stack_heads(e2[:CHUNK])
    gsuf = _stack_heads(e2[CHUNK:])
    g_row = jnp.broadcast_to(jnp.transpose(gc)[0:1, :], (HC, HC))
    g_col = jnp.concatenate([gc, gc], axis=1)
    dec = jnp.where(causal, jnp.exp(jnp.minimum(g_col - g_row, 0.0)), 0.0)
    kb = k * beta
    kk = mm_nt(kb, k)
    qk = mm_nt(q, k)
    yield
    m = jnp.where(strict, kk * dec, 0.0)
    x = yield ("solve", m, jnp.concatenate([v * beta, kb * jnp.exp(gc)], axis=1))
    aqk = jnp.where(causal, qk * dec, 0.0)
    tot = jnp.sum(g_l, axis=0, keepdims=True)
    g_last = jnp.exp(jnp.stack([tot[:, h * HEAD:(h + 1) * HEAD] for h in range(N_HEADS)], axis=0))
    return x[:, :HEAD], x[:, HEAD:], aqk, q * jnp.exp(gc), k * jnp.exp(gsuf), g_last


def _hgrn_step(state, hq, hf, hi, hz, lb, on_w, cm, first_chunk):
    masks = _hgrn_masks()
    gens = [_hgrn_intra(q, f, i, lb, cm, _vmask(first_chunk + j), masks)
            for j, (q, f, i) in enumerate(zip(_chunks_of(hq), _chunks_of(hf), _chunks_of(hi)))]
    outs = []
    for (q_dec, k_dec, v, o_intra, g_end), z in zip(_interleave(gens), _chunks_of(hz)):
        per_head = lambda a: a.reshape(N_HEADS, CHUNK, HEAD)
        o = bmm_nt(per_head(q_dec), state).reshape(HC, HEAD) + o_intra
        state = state * g_end + bmm_tn(per_head(v), per_head(k_dec))
        outs.append(_unstack_heads(_rms(o, on_w)) * _silu(z))
    return state, jnp.concatenate(outs, axis=0)


def _hgrn_masks():
    r = _iota2(HC, HC, 0)
    c = _iota2(HC, HC, 1)
    pairs = []
    for b in HG_LEVELS:
        sh = b.bit_length() - 1
        pairs.append(((r >> sh) == (c >> sh)) & ((r & (b - 1)) >= b // 2) & ((c & (b - 1)) < b // 2))
    return r == c, pairs


def _hgrn_intra(hq, hf, hi, lb, cm, vm, masks):
    diag, pairs = masks
    forget = lb + (1.0 - lb) * _sigmoid(hf)
    g_l = jnp.log(forget)
    e = cmm(cm, g_l)
    q = _stack_heads(_silu(hq))
    k = _stack_heads((1.0 - lb) * _sigmoid(-hf))
    v = _stack_heads(hi * vm)
    yield
    sect = lambda n: _stack_heads(e[n * CHUNK:(n + 1) * CHUNK])
    gc, gsuf = sect(0), sect(1)
    a = jnp.where(diag, jnp.sum(q * k, axis=-1, keepdims=True), 0.0)
    for li, pair in enumerate(pairs):
        a = a + jnp.where(pair, mm_nt(q * jnp.exp(sect(2 + 2 * li)), k * jnp.exp(sect(3 + 2 * li))), 0.0)
    yield
    o_intra = mm(a, v)
    tot = jnp.sum(g_l, axis=0, keepdims=True)
    g_end = jnp.exp(jnp.stack([tot[:, h * HEAD:(h + 1) * HEAD] for h in range(N_HEADS)], axis=0))
    return q * jnp.exp(gc), k * jnp.exp(gsuf), v, o_intra, g_end


def _vmask(chunk_idx):
    rows = chunk_idx * CHUNK + lax.broadcasted_iota(jnp.int32, (CHUNK, 1), 0)
    return jnp.where(rows >= PAD_FRONT, 1.0, 0.0)


def _row(n):
    return pl.BlockSpec((1, n), lambda i: (0, 0))


def gdn_fwd(ypre, small, proj, a_log, dt_b, on_w):
    t = ypre.shape[0]
    nc = t // REC_ROWS
    c2 = jnp.asarray(_cumsum_consts(), bf16)

    def body(y_ref, s_ref, z_ref, al_ref, dt_ref, on_ref, c2_ref, o_ref, st_ref, state):
        i = pl.program_id(0)

        @pl.when(i == 0)
        def _():
            state[...] = jnp.zeros_like(state)

        s_in = state[...]
        st_ref[0] = s_in
        s_new, out = _gdn_step(s_in, y_ref[...], s_ref[...], z_ref[...], al_ref[...], dt_ref[...], on_ref[...],
                               c2_ref[...], i * REC_CHUNKS)
        state[...] = s_new
        o_ref[...] = out

    return pl.pallas_call(
        body, grid=(nc,),
        in_specs=[pl.BlockSpec((REC_ROWS,GQKV), lambda i: (i, 0)), pl.BlockSpec((REC_ROWS,N_SMALL), lambda i: (i, 0)),
                  pl.BlockSpec((REC_ROWS,BRANCH), lambda i: (i, C_GZ // BRANCH)), _row(128), _row(128), _row(128),
                  pl.BlockSpec((2 * CHUNK, CHUNK), lambda i: (0, 0))],
        out_specs=[pl.BlockSpec((REC_ROWS,BRANCH), lambda i: (i, 0)),
                   pl.BlockSpec((1, N_HEADS, HEAD, HEAD), lambda i: (i, 0, 0, 0))],
        out_shape=[jax.ShapeDtypeStruct((t, BRANCH), f32), jax.ShapeDtypeStruct((nc, N_HEADS, HEAD, HEAD), f32)],
        scratch_shapes=[pltpu.VMEM((N_HEADS, HEAD, HEAD), f32)],
        compiler_params=_cparams(("arbitrary",)), name="gdn_fwd")(ypre, small, proj, a_log, dt_b, on_w, c2)


def gdn_bwd(ypre, small, proj, a_log, dt_b, on_w, states, d_out, dproj):
    t = ypre.shape[0]
    nc = t // REC_ROWS
    c2 = jnp.asarray(_cumsum_consts(), bf16)

    def body(y_ref, s_ref, z_ref, al_ref, dt_ref, on_ref, c2_ref, st_ref, do_ref, _,
             dy_ref, ds_ref, dz_ref, dal_ref, ddt_ref, don_ref, dstate):
        i = pl.program_id(0)

        @pl.when(i == 0)
        def _():
            dstate[...] = jnp.zeros_like(dstate)
            dal_ref[...] = jnp.zeros_like(dal_ref)
            ddt_ref[...] = jnp.zeros_like(ddt_ref)
            don_ref[...] = jnp.zeros_like(don_ref)

        c2v = c2_ref[...]
        fn = lambda s, y, sm, z, al, dt, on: _gdn_step(s, y, sm, z, al, dt, on, c2v, (nc - 1 - i) * REC_CHUNKS)
        _, vjp = jax.vjp(fn, st_ref[0], y_ref[...], s_ref[...], z_ref[...], al_ref[...], dt_ref[...], on_ref[...])
        d_s, d_y, d_sm, d_z, d_al, d_dt, d_on = vjp((dstate[...], do_ref[...]))
        dstate[...] = d_s
        dy_ref[...] = d_y
        ds_ref[...] = d_sm
        dz_ref[...] = d_z.astype(bf16)
        dal_ref[...] += d_al
        ddt_ref[...] += d_dt
        don_ref[...] += d_on

    rev = lambda i: (nc - 1 - i, 0)
    return pl.pallas_call(
        body, grid=(nc,),
        in_specs=[pl.BlockSpec((REC_ROWS,GQKV), rev), pl.BlockSpec((REC_ROWS,N_SMALL), rev),
                  pl.BlockSpec((REC_ROWS,BRANCH), lambda i: (nc - 1 - i, C_GZ // BRANCH)), _row(128), _row(128), _row(128),
                  pl.BlockSpec((2 * CHUNK, CHUNK), lambda i: (0, 0)),
                  pl.BlockSpec((1, N_HEADS, HEAD, HEAD), lambda i: (nc - 1 - i, 0, 0, 0)),
                  pl.BlockSpec((REC_ROWS,BRANCH), rev), pl.BlockSpec(memory_space=pl.ANY)],
        out_specs=[pl.BlockSpec((REC_ROWS,GQKV), rev), pl.BlockSpec((REC_ROWS,N_SMALL), rev),
                   pl.BlockSpec((REC_ROWS,BRANCH), lambda i: (nc - 1 - i, C_GZ // BRANCH)), _row(128), _row(128), _row(128)],
        out_shape=[jax.ShapeDtypeStruct((t, GQKV), f32), jax.ShapeDtypeStruct((t, N_SMALL), f32),
                   jax.ShapeDtypeStruct((t, N_MAIN), bf16)] + [jax.ShapeDtypeStruct((1, 128), f32)] * 3,
        scratch_shapes=[pltpu.VMEM((N_HEADS, HEAD, HEAD), f32)], input_output_aliases={9: 2},
        compiler_params=_cparams(("arbitrary",)), name="gdn_bwd")(
            ypre, small, proj, a_log, dt_b, on_w, c2, states, d_out, dproj)


def hgrn_fwd(proj, lb, on_w):
    t = proj.shape[0]
    nc = t // REC_ROWS
    cm = jnp.asarray(_hgrn_consts(), bf16)
    ncm = cm.shape[0]

    def body(q_ref, f_ref, i_ref, z_ref, lb_ref, on_ref, cm_ref, o_ref, st_ref, state):
        i = pl.program_id(0)

        @pl.when(i == 0)
        def _():
            state[...] = jnp.zeros_like(state)

        s_in = state[...]
        st_ref[0] = s_in
        s_new, out = _hgrn_step(s_in, q_ref[...], f_ref[...], i_ref[...], z_ref[...], lb_ref[...], on_ref[...],
                                cm_ref[...], i * REC_CHUNKS)
        state[...] = s_new
        o_ref[...] = out

    sec = lambda off: pl.BlockSpec((REC_ROWS,BRANCH), functools.partial(lambda i, b: (i, b), b=off // BRANCH))
    return pl.pallas_call(
        body, grid=(nc,),
        in_specs=[sec(C_HQ), sec(C_HF), sec(C_HI), sec(C_HZ), _row(BRANCH), _row(128),
                  pl.BlockSpec((ncm, CHUNK), lambda i: (0, 0))],
        out_specs=[pl.BlockSpec((REC_ROWS,BRANCH), lambda i: (i, 0)),
                   pl.BlockSpec((1, N_HEADS, HEAD, HEAD), lambda i: (i, 0, 0, 0))],
        out_shape=[jax.ShapeDtypeStruct((t, BRANCH), f32), jax.ShapeDtypeStruct((nc, N_HEADS, HEAD, HEAD), f32)],
        scratch_shapes=[pltpu.VMEM((N_HEADS, HEAD, HEAD), f32)],
        compiler_params=_cparams(("arbitrary",)), name="hgrn_fwd")(proj, proj, proj, proj, lb, on_w, cm)


def hgrn_bwd(proj, lb, on_w, states, d_out, dproj):
    t = proj.shape[0]
    nc = t // REC_ROWS
    cm = jnp.asarray(_hgrn_consts(), bf16)
    ncm = cm.shape[0]

    def body(q_ref, f_ref, i_ref, z_ref, lb_ref, on_ref, cm_ref, st_ref, do_ref, _, dh_ref, dlb_ref, don_ref, dstate):
        i = pl.program_id(0)

        @pl.when(i == 0)
        def _():
            dstate[...] = jnp.zeros_like(dstate)
            dlb_ref[...] = jnp.zeros_like(dlb_ref)
            don_ref[...] = jnp.zeros_like(don_ref)

        cmv = cm_ref[...]
        fn = lambda s, a, b, c, d, l, on: _hgrn_step(s, a, b, c, d, l, on, cmv, (nc - 1 - i) * REC_CHUNKS)
        _, vjp = jax.vjp(fn, st_ref[0], q_ref[...], f_ref[...], i_ref[...], z_ref[...], lb_ref[...], on_ref[...])
        d_s, d_q, d_f, d_i, d_z, d_lb, d_on = vjp((dstate[...], do_ref[...]))
        dstate[...] = d_s
        dh_ref[...] = jnp.concatenate([d_q, d_f, d_i, d_z], axis=1).astype(bf16)
        dlb_ref[...] += d_lb
        don_ref[...] += d_on

    rev = lambda i: (nc - 1 - i, 0)
    sec = lambda off: pl.BlockSpec((REC_ROWS,BRANCH), functools.partial(lambda i, b: (nc - 1 - i, b), b=off // BRANCH))
    return pl.pallas_call(
        body, grid=(nc,),
        in_specs=[sec(C_HQ), sec(C_HF), sec(C_HI), sec(C_HZ), _row(BRANCH), _row(128),
                  pl.BlockSpec((ncm, CHUNK), lambda i: (0, 0)),
                  pl.BlockSpec((1, N_HEADS, HEAD, HEAD), lambda i: (nc - 1 - i, 0, 0, 0)),
                  pl.BlockSpec((REC_ROWS,BRANCH), rev), pl.BlockSpec(memory_space=pl.ANY)],
        out_specs=[pl.BlockSpec((REC_ROWS,4 * BRANCH), lambda i: (nc - 1 - i, C_HQ // (4 * BRANCH))), _row(BRANCH), _row(128)],
        out_shape=[jax.ShapeDtypeStruct((t, N_MAIN), bf16), jax.ShapeDtypeStruct((1, BRANCH), f32),
                   jax.ShapeDtypeStruct((1, 128), f32)],
        scratch_shapes=[pltpu.VMEM((N_HEADS, HEAD, HEAD), f32)], input_output_aliases={9: 0},
        compiler_params=_cparams(("arbitrary",)), name="hgrn_bwd")(proj, proj, proj, proj, lb, on_w, cm, states, d_out, dproj)


TM_MG = 320


def _const_spec(shape):
    nd = len(shape)
    return pl.BlockSpec(shape, lambda i: (0,) * nd, pipeline_mode=pl.Buffered(1))


def merge_fwd(osb, proj, ogd, ohg, wb, wo, h):
    t = h.shape[0]

    def body(osb_ref, ogd_ref, ohg_ref, zm_ref, wb_ref, wo_ref, h_ref, out_ref):
        a = osb_ref[...] * _silu(zm_ref[:, :BRANCH])
        gate = lambda b: _sigmoid(zm_ref[:, C_MIX + b * D_MODEL:C_MIX + (b + 1) * D_MODEL])
        y = (gate(0) * _dot(a, wb_ref[0]) + gate(1) * _dot(ogd_ref[...], wb_ref[1])
             + gate(2) * _dot(ohg_ref[...], wb_ref[2]))
        out_ref[...] = h_ref[...] + _dot(y, wo_ref[...])

    br = pl.BlockSpec((TM_MG, BRANCH), lambda i: (i, 0))
    return pl.pallas_call(
        body, grid=(t // TM_MG,),
        in_specs=[br, br, br, pl.BlockSpec((TM_MG, W_MERGE), lambda i: (i, 0)),
                  _const_spec((3, BRANCH, D_MODEL)), _const_spec((D_MODEL, D_MODEL)),
                  pl.BlockSpec((TM_MG, D_MODEL), lambda i: (i, 0))],
        out_specs=pl.BlockSpec((TM_MG, D_MODEL), lambda i: (i, 0)),
        out_shape=jax.ShapeDtypeStruct((t, D_MODEL), f32),
        compiler_params=_cparams(("arbitrary",)), name="merge_fwd")(osb, ogd, ohg, proj, wb, wo, h)


def merge_bwd(osb, proj, ogd, ohg, wb, wbt, wot, dh):
    t = dh.shape[0]

    def body(osb_ref, ogd_ref, ohg_ref, zm_ref, wb_ref, wbt_ref, wot_ref, dh_ref,
             dosb_ref, dogd_ref, dohg_ref, dzm_ref, dwo_ref, dwb_ref):
        i = pl.program_id(0)

        @pl.when(i == 0)
        def _():
            dwo_ref[...] = jnp.zeros_like(dwo_ref)
            dwb_ref[...] = jnp.zeros_like(dwb_ref)

        osb = osb_ref[...]
        sbz = zm_ref[:, :BRANCH]
        sgz = _sigmoid(sbz)
        sz = sbz * sgz
        branch_in = (osb * sz, ogd_ref[...], ohg_ref[...])
        dh_v = dh_ref[...]
        dy = _dot(dh_v, wot_ref[...])
        y = jnp.zeros((TM_MG, D_MODEL), f32)
        d_in = []
        for b in range(3):
            cols = slice(C_MIX + b * D_MODEL, C_MIX + (b + 1) * D_MODEL)
            p = _dot(branch_in[b], wb_ref[b])
            g = _sigmoid(zm_ref[:, cols])
            y = y + g * p
            dp = dy * g
            dzm_ref[:, cols] = (dy * p * g * (1.0 - g)).astype(bf16)
            d_in.append(_dot(dp, wbt_ref[b]))
            dwb_ref[b] += _dot(branch_in[b], dp, TN)
        dwo_ref[...] += _dot(y, dh_v, TN)
        dosb_ref[...] = d_in[0] * sz
        dzm_ref[:, :BRANCH] = (d_in[0] * osb * (sgz * (1.0 + sbz * (1.0 - sgz)))).astype(bf16)
        dogd_ref[...] = d_in[1]
        dohg_ref[...] = d_in[2]

    br = pl.BlockSpec((TM_MG, BRANCH), lambda i: (i, 0))
    zm = pl.BlockSpec((TM_MG, W_MERGE), lambda i: (i, 0))
    return pl.pallas_call(
        body, grid=(t // TM_MG,),
        in_specs=[br, br, br, zm,
                  _const_spec((3, BRANCH, D_MODEL)), _const_spec((3, D_MODEL, BRANCH)), _const_spec((D_MODEL, D_MODEL)),
                  pl.BlockSpec((TM_MG, D_MODEL), lambda i: (i, 0))],
        out_specs=[br, br, br, zm, _const_spec((D_MODEL, D_MODEL)), _const_spec((3, BRANCH, D_MODEL))],
        out_shape=[jax.ShapeDtypeStruct((t, BRANCH), f32)] * 3 + [jax.ShapeDtypeStruct((t, N_MAIN), bf16),
                   jax.ShapeDtypeStruct((D_MODEL, D_MODEL), f32), jax.ShapeDtypeStruct((3, BRANCH, D_MODEL), f32)],
        compiler_params=_cparams(("arbitrary",)), name="merge_bwd")(osb, ogd, ohg, proj, wb, wbt, wot, dh)


def loss_head(h, target):
    t = h.shape[0]
    nb = t // SB_BLOCK

    def body(h_ref, t_ref, dh_ref, loss_ref):
        i = pl.program_id(0)

        @pl.when(i == 0)
        def _():
            loss_ref[...] = jnp.zeros_like(loss_ref)
            dh_ref[...] = jnp.zeros_like(dh_ref)

        @pl.when(i > 0)
        def _():
            err = h_ref[...] - t_ref[...]
            dh_ref[...] = err * (1.0 / D_MODEL)
            loss_ref[...] += jnp.broadcast_to(jnp.sum(err * err) * (0.5 / D_MODEL), loss_ref.shape)

    return pl.pallas_call(
        body, grid=(nb,),
        in_specs=[pl.BlockSpec((SB_BLOCK, D_MODEL), lambda i: (i, 0)),
                  pl.BlockSpec((SB_BLOCK, D_MODEL), lambda i: (jnp.maximum(i - 1, 0), 0))],
        out_specs=[pl.BlockSpec((SB_BLOCK, D_MODEL), lambda i: (i, 0)), pl.BlockSpec((1, 128), lambda i: (0, 0))],
        out_shape=[jax.ShapeDtypeStruct((t, D_MODEL), f32), jax.ShapeDtypeStruct((1, 128), f32)],
        compiler_params=_cparams(("arbitrary",)), name="loss_head")(h, target)


def adamw(parts, w, m, v, rows_per_step, name):
    r, c = w.shape
    tr = min(rows_per_step, r)
    n_parts = parts.shape[0]

    def body(p_ref, w_ref, m_ref, v_ref, g_ref, d_ref, nm_ref, nv_ref):
        g = p_ref[0].astype(f32)
        for k in range(1, n_parts):
            g = g + p_ref[k].astype(f32)
        m_new = ADAM_B1 * m_ref[...] + (1.0 - ADAM_B1) * g
        v_new = ADAM_B2 * v_ref[...] + (1.0 - ADAM_B2) * jnp.square(g)
        m_hat = m_new / (1.0 - ADAM_B1 ** ADAM_STEP)
        v_hat = v_new / (1.0 - ADAM_B2 ** ADAM_STEP)
        g_ref[...] = g
        d_ref[...] = -ADAM_LR * (m_hat / (jnp.sqrt(v_hat) + ADAM_EPS) + ADAM_WD * w_ref[...])
        nm_ref[...] = m_new
        nv_ref[...] = v_new

    blk = pl.BlockSpec((tr, c), lambda i: (i, 0))
    return pl.pallas_call(
        body, grid=(r // tr,),
        in_specs=[pl.BlockSpec((n_parts, tr, c), lambda i: (0, i, 0)), blk, blk, blk],
        out_specs=[blk] * 4, out_shape=[jax.ShapeDtypeStruct((r, c), f32)] * 4,
        compiler_params=_cparams(("arbitrary",)), name=name)(parts, w, m, v)


def _mesh_pos():
    return lax.axis_index("x"), lax.axis_index("y"), lax.axis_index("c")


def _peer(pos, k):
    x, y, c = pos
    return (1 - x if k & 4 else x, 1 - y if k & 2 else y, 1 - c if k & 1 else c)


def _lin(pos):
    return 4 * pos[0] + 2 * pos[1] + pos[2]


N_CHIPS = 4


def _chip(pos):
    return 2 * pos[0] + pos[1]


def exchange(srcs, scatter, name):
    n = len(srcs)
    shapes = [s.shape[1:] if sc else s.shape for s, sc in zip(srcs, scatter)]
    n_slots = [N_CHIPS if sc == "chips" else N_DEV for sc in scatter]

    def body(*refs):
        src_refs, dst_refs = refs[:n], refs[n:2 * n]
        send_sems, recv_sems, local_sems = refs[2 * n:]
        me = _mesh_pos()
        sends, recvs, locals_ = [], [], []
        for t in range(n):
            slot = _chip if scatter[t] == "chips" else _lin
            own = src_refs[t].at[slot(me)] if scatter[t] else src_refs[t]
            locals_.append(pltpu.make_async_copy(own, dst_refs[t].at[slot(me)], local_sems.at[t]))
            for k in range(1, N_DEV):
                if scatter[t] == "chips" and k & 1:
                    continue
                peer = _peer(me, k)
                src = src_refs[t].at[slot(peer)] if scatter[t] else src_refs[t]
                sends.append(pltpu.make_async_remote_copy(
                    src_ref=src, dst_ref=dst_refs[t].at[slot(me)], send_sem=send_sems.at[t, k - 1],
                    recv_sem=recv_sems.at[t, k - 1], device_id=peer, device_id_type=MESH))
                recvs.append(pltpu.make_async_remote_copy(
                    src_ref=src, dst_ref=dst_refs[t].at[slot(peer)], send_sem=send_sems.at[t, k - 1],
                    recv_sem=recv_sems.at[t, k - 1], device_id=peer, device_id_type=MESH))
        for cp in locals_ + sends:
            cp.start()
        for cp in sends:
            cp.wait_send()
        for cp in recvs:
            cp.wait_recv()
        for cp in locals_:
            cp.wait()

    any_spec = pl.BlockSpec(memory_space=pl.ANY)
    return pl.pallas_call(
        body, in_specs=[any_spec] * n, out_specs=[any_spec] * n,
        out_shape=[jax.ShapeDtypeStruct((ns,) + tuple(sh), s.dtype) for ns, sh, s in zip(n_slots, shapes, srcs)],
        scratch_shapes=[pltpu.SemaphoreType.DMA((n, N_DEV - 1)), pltpu.SemaphoreType.DMA((n, N_DEV - 1)),
                        pltpu.SemaphoreType.DMA((n,))],
        compiler_params=pltpu.CompilerParams(has_side_effects=True), name=name)(*srcs)


def exchange_sibling(stacks, name):
    n = len(stacks)

    def body(*refs):
        src_refs, dst_refs = refs[:n], refs[n:2 * n]
        send_sems, recv_sems = refs[2 * n:]
        x, y, c = _mesh_pos()
        copies = [pltpu.make_async_remote_copy(
            src_ref=src_refs[t].at[2 * q + (1 - c)], dst_ref=dst_refs[t].at[q], send_sem=send_sems.at[t, q],
            recv_sem=recv_sems.at[t, q], device_id=(x, y, 1 - c), device_id_type=MESH)
            for t in range(n) for q in range(N_CHIPS)]
        for cp in copies:
            cp.start()
        for cp in copies:
            cp.wait()

    any_spec = pl.BlockSpec(memory_space=pl.ANY)
    return pl.pallas_call(
        body, in_specs=[any_spec] * n, out_specs=[any_spec] * n,
        out_shape=[jax.ShapeDtypeStruct((N_CHIPS,) + tuple(s.shape[1:]), s.dtype) for s in stacks],
        scratch_shapes=[pltpu.SemaphoreType.DMA((n, N_CHIPS)), pltpu.SemaphoreType.DMA((n, N_CHIPS))],
        compiler_params=pltpu.CompilerParams(has_side_effects=True), name=name)(*stacks)


def add_partials(stack, received, core, rows_per_step, name):
    _, r, c = stack.shape
    tr = min(rows_per_step, r)

    def body(core_ref, own_ref, rcv_ref, out_ref):
        del core_ref
        out_ref[...] = (own_ref[...].astype(f32) + rcv_ref[...].astype(f32)).astype(out_ref.dtype)

    return pl.pallas_call(
        body,
        grid_spec=pltpu.PrefetchScalarGridSpec(
            num_scalar_prefetch=1, grid=(N_CHIPS, r // tr),
            in_specs=[pl.BlockSpec((None, None, tr, c), lambda q, i, core_ref: (q, core_ref[0], i, 0)),
                      pl.BlockSpec((None, tr, c), lambda q, i, core_ref: (q, i, 0))],
            out_specs=pl.BlockSpec((None, tr, c), lambda q, i, core_ref: (q, i, 0))),
        out_shape=jax.ShapeDtypeStruct((N_CHIPS, r, c), stack.dtype),
        compiler_params=_cparams(("arbitrary", "arbitrary")), name=name)(
            core, stack.reshape(N_CHIPS, 2, r, c), received)


def gather_two_level(srcs, name):
    n = len(srcs)
    n_cp = N_DEV - 1

    def body(*refs):
        src_refs, dst_refs = refs[:n], refs[n:2 * n]
        send_sems, recv_sems, local_sems = refs[2 * n:]
        x, y, c = _mesh_pos()
        me, sibling = (x, y, c), (x, y, 1 - c)
        chips = [(1 - x, y), (x, 1 - y), (1 - x, 1 - y)]

        def copy(t, k, block, to, src=None):
            slot = dst_refs[t].at[_lin(block)]
            return pltpu.make_async_remote_copy(
                src_ref=slot if src is None else src, dst_ref=slot, send_sem=send_sems.at[t, k],
                recv_sem=recv_sems.at[t, k], device_id=to, device_id_type=MESH)

        mine, first, passed = [], [], []
        for t in range(n):
            mine.append(pltpu.make_async_copy(src_refs[t], dst_refs[t].at[_lin(me)], local_sems.at[t]))
            first.append(copy(t, 0, me, sibling, src=src_refs[t]))
            first += [copy(t, 1 + j, me, (*chip, c), src=src_refs[t]) for j, chip in enumerate(chips)]
        for cp in mine + first:
            cp.start()
        for j, chip in enumerate(chips):
            for t in range(n):
                copy(t, 1 + j, (*chip, c), me).wait_recv()
                fwd = copy(t, 4 + j, (*chip, c), sibling)
                fwd.start()
                passed.append(fwd)
        for t in range(n):
            copy(t, 0, sibling, me).wait_recv()
            for j, chip in enumerate(chips):
                copy(t, 4 + j, (*chip, 1 - c), me).wait_recv()
        for cp in first + passed:
            cp.wait_send()
        for cp in mine:
            cp.wait()

    any_spec = pl.BlockSpec(memory_space=pl.ANY)
    return pl.pallas_call(
        body, in_specs=[any_spec] * n, out_specs=[any_spec] * n,
        out_shape=[jax.ShapeDtypeStruct((N_DEV,) + tuple(s.shape), s.dtype) for s in srcs],
        scratch_shapes=[pltpu.SemaphoreType.DMA((n, n_cp)), pltpu.SemaphoreType.DMA((n, n_cp)),
                        pltpu.SemaphoreType.DMA((n,))],
        compiler_params=pltpu.CompilerParams(has_side_effects=True), name=name)(*srcs)


PACK_ROWS = 104


def _pad_rows(a, rows):
    return jnp.pad(a, ((0, rows - a.shape[0]), (0, 0)))


def _pad_lanes(a):
    return jnp.pad(a, ((0, 0), (0, 128 - a.shape[1])))


def _pack(norm_w, sbq, sbk, alog, dtb, gon, lbl, hon, loss_row):
    parts = [norm_w.reshape(32, 128), _pad_rows(sbq, 8), _pad_rows(sbk, 8), _pad_rows(_pad_lanes(alog), 8),
             _pad_rows(_pad_lanes(dtb), 8), _pad_rows(gon, 8), lbl.reshape(16, 128), _pad_rows(hon, 8),
             _pad_rows(loss_row, 8)]
    return jnp.concatenate(parts, axis=0)


def _unpack(p):
    return dict(norm_w=p[0:32].reshape(DEPTH, D_MODEL), sb_q_norm=p[32:36], sb_k_norm=p[40:44],
                gdn_a_log=p[48:52, :N_HEADS], gdn_dt_bias=p[56:60, :N_HEADS], gdn_out_norm=p[64:68],
                hgrn_lb_logits=p[72:88].reshape(DEPTH, BRANCH), hgrn_out_norm=p[88:92], loss=p[96, 0])


def _lower_bounds(logits):
    p = jax.nn.softmax(logits, axis=0)
    return jnp.cumsum(p, axis=0) - p[0:1]


def _unshard_cols(g):
    nd = g.ndim
    g = jnp.moveaxis(g, 0, nd - 2)
    return g.reshape(g.shape[:-2] + (N_DEV * g.shape[-1],))


def _shard_cols(a):
    n = a.shape[-1] // N_DEV
    return jnp.moveaxis(a.reshape(a.shape[:-1] + (N_DEV, n)), -2, 0)


def kernel(x, meta_tokens, norm_w, w_in, sb_q_norm, sb_k_norm, gdn_conv_w, gdn_a_log, gdn_dt_bias, gdn_out_norm, hgrn_lb_logits, hgrn_out_norm, w_branch, w_out, loss_target, m_meta_tokens, m_norm_w, m_w_in, m_sb_q_norm, m_sb_k_norm, m_gdn_conv_w, m_gdn_a_log, m_gdn_dt_bias, m_gdn_out_norm, m_hgrn_lb_logits, m_hgrn_out_norm, m_w_branch, m_w_out, v_meta_tokens, v_norm_w, v_w_in, v_sb_q_norm, v_sb_k_norm, v_gdn_conv_w, v_gdn_a_log, v_gdn_dt_bias, v_gdn_out_norm, v_hgrn_lb_logits, v_hgrn_out_norm, v_w_branch, v_w_out):
    g_win, g_wbr, g_wout, g_meta, g_conv = gather_two_level(
        [w_in.astype(bf16), w_branch.astype(bf16), w_out.astype(bf16), meta_tokens, gdn_conv_w], "gather_weights")
    w_full = _unshard_cols(g_win)
    w_main = jnp.concatenate([w_full[..., a:b] for a, b in W_IN_ORDER], axis=-1)
    w_small = jnp.pad(w_full[..., SMALL_OFF:SMALL_OFF + 8], ((0, 0), (0, 0), (0, N_SMALL - 8)))
    wt_main = jnp.swapaxes(w_main, 1, 2)
    wt_small = jnp.swapaxes(w_small, 1, 2)
    wbr = _unshard_cols(g_wbr)
    wbr_t = jnp.swapaxes(wbr, 2, 3)
    wout = jnp.moveaxis(g_wout, 0, 1).reshape(DEPTH, D_MODEL, D_MODEL)
    wout_t = jnp.swapaxes(wout, 1, 2)
    meta = _unshard_cols(g_meta)
    conv_w = _unshard_cols(g_conv)
    lbounds, lb_vjp = jax.vjp(_lower_bounds, hgrn_lb_logits)

    h = jnp.concatenate([jnp.zeros((PAD_FRONT, D_MODEL), f32), meta, x[0]], axis=0)
    row = lambda a: a.reshape(1, -1)
    saved = []
    for l in range(DEPTH):
        proj, small, _, xnt = inproj_fwd(h, row(norm_w[l]), w_main[l], w_small[l])
        osb = sb_fwd(proj, row(sb_q_norm[l]), row(sb_k_norm[l]))
        ypre = conv_fwd(proj, conv_w[l])
        al, dtb = _pad_lanes(row(gdn_a_log[l])), _pad_lanes(row(gdn_dt_bias[l]))
        ogd, gst = gdn_fwd(ypre, small, proj, al, dtb, row(gdn_out_norm[l]))
        ohg, hst = hgrn_fwd(proj, row(lbounds[l]), row(hgrn_out_norm[l]))
        h_next = merge_fwd(osb, proj, ogd, ohg, wbr[l], wout[l], h)
        saved.append((h, proj, small, xnt, osb, ypre, ogd, gst, ohg, hst, al, dtb))
        h = h_next

    dh, loss_row = loss_head(h, loss_target[0])

    gw_main, gw_small, gw_br, gw_out, g_conv_w = [None] * DEPTH, [None] * DEPTH, [None] * DEPTH, [None] * DEPTH, [None] * DEPTH
    g_norm, g_sbq, g_sbk, g_al, g_dt, g_gon, g_lb, g_hon = ([None] * DEPTH for _ in range(8))
    for l in reversed(range(DEPTH)):
        h_l, proj, small, xnt, osb, ypre, ogd, gst, ohg, hst, al, dtb = saved[l]
        d_osb, d_ogd, d_ohg, dproj, gw_out[l], gw_br[l] = merge_bwd(osb, proj, ogd, ohg, wbr[l], wbr_t[l], wout_t[l], dh)
        dproj, g_lb[l], g_hon[l] = hgrn_bwd(proj, row(lbounds[l]), row(hgrn_out_norm[l]), hst, d_ohg, dproj)
        d_ypre, d_small, dproj, g_al[l], g_dt[l], g_gon[l] = gdn_bwd(ypre, small, proj, al, dtb, row(gdn_out_norm[l]), gst, d_ogd, dproj)
        dproj, g_conv_w[l] = conv_bwd(proj, conv_w[l], d_ypre, dproj)
        dproj, g_sbq[l], g_sbk[l] = sb_bwd(proj, row(sb_q_norm[l]), row(sb_k_norm[l]), osb, d_osb, dproj)
        gw_main[l], gw_small[l] = inproj_bwd_w(xnt, dproj, d_small)
        dh, g_norm[l] = inproj_bwd_x(dproj, d_small, wt_main[l], wt_small[l], h_l, row(norm_w[l]), dh)

    gw_main, gw_small = jnp.stack(gw_main), jnp.stack(gw_small)
    starts = np.cumsum([0] + [b - a for a, b in W_IN_ORDER])
    pieces = sorted((a, gw_main[..., int(s):int(s) + b - a]) for (a, b), s in zip(W_IN_ORDER, starts))
    pieces.append((SMALL_OFF, gw_small[..., :8].astype(bf16)))
    gw_in = jnp.concatenate([p for _, p in sorted(pieces, key=lambda ap: ap[0])], axis=-1)
    d_lbl = lb_vjp(jnp.concatenate(g_lb, axis=0))[0]
    cat = lambda rows: jnp.concatenate(rows, axis=0)
    pack = _pack(cat(g_norm), cat(g_sbq), cat(g_sbk), cat(g_al)[:, :N_HEADS], cat(g_dt)[:, :N_HEADS], cat(g_gon),
                 d_lbl, cat(g_hon), loss_row)
    g_meta_full = dh[PAD_FRONT:FRONT]
    big = [_shard_cols(gw_in).astype(bf16).reshape(N_DEV, -1, w_in.shape[-1]),
           _shard_cols(jnp.stack(gw_br)).astype(bf16).reshape(N_DEV, -1, w_branch.shape[-1]),
           jnp.swapaxes(jnp.stack(gw_out).reshape(DEPTH, N_DEV, HEAD, D_MODEL), 0, 1).astype(bf16).reshape(N_DEV, -1, D_MODEL)]
    from_sibling = exchange_sibling(big, "exchange_sibling")
    core = lax.axis_index("c").astype(jnp.int32).reshape(1)
    chip_parts = [add_partials(s, r, core, rows, "add_partials_" + nm)
                  for s, r, rows, nm in zip(big, from_sibling, (512, 2048, 512), ("w_in", "w_branch", "w_out"))]
    r_win, r_wbr, r_wout, r_meta, r_conv, r_pack = exchange(
        chip_parts + [_shard_cols(g_meta_full), _shard_cols(jnp.stack(g_conv_w)), pack],
        ["chips", "chips", "chips", True, True, False], "exchange_grads")

    def upd(parts, w, m, v, rows, name):
        shp = w.shape
        two = (-1, shp[-1])
        outs = adamw(parts.reshape((parts.shape[0],) + w.reshape(two).shape), w.reshape(two), m.reshape(two), v.reshape(two), rows, name)
        return [o.reshape(shp) for o in outs]

    res = {}
    res["w_in"] = upd(r_win, w_in, m_w_in, v_w_in, 256, "adamw_w_in")
    res["w_branch"] = upd(r_wbr, w_branch, m_w_branch, v_w_branch, 1024, "adamw_w_branch")
    res["w_out"] = upd(r_wout, w_out, m_w_out, v_w_out, 256, "adamw_w_out")
    res["meta_tokens"] = upd(r_meta, meta_tokens, m_meta_tokens, v_meta_tokens, 16, "adamw_meta")
    res["gdn_conv_w"] = upd(r_conv, gdn_conv_w, m_gdn_conv_w, v_gdn_conv_w, 16, "adamw_conv")
    zero_row = jnp.zeros((1, 128), f32)
    w_pack = _pack(norm_w, sb_q_norm, sb_k_norm, gdn_a_log, gdn_dt_bias, gdn_out_norm, hgrn_lb_logits, hgrn_out_norm, zero_row)
    m_pack = _pack(m_norm_w, m_sb_q_norm, m_sb_k_norm, m_gdn_a_log, m_gdn_dt_bias, m_gdn_out_norm, m_hgrn_lb_logits, m_hgrn_out_norm, zero_row)
    v_pack = _pack(v_norm_w, v_sb_q_norm, v_sb_k_norm, v_gdn_a_log, v_gdn_dt_bias, v_gdn_out_norm, v_hgrn_lb_logits, v_hgrn_out_norm, zero_row)
    packed = [_unpack(o) for o in adamw(r_pack, w_pack, m_pack, v_pack, PACK_ROWS, "adamw_replicated")]
    for name in ("norm_w", "sb_q_norm", "sb_k_norm", "gdn_a_log", "gdn_dt_bias", "gdn_out_norm", "hgrn_lb_logits", "hgrn_out_norm"):
        res[name] = [p[name] for p in packed]
    loss = packed[0]["loss"]
    grad_x = dh[FRONT:][None]

    order = ["meta_tokens", "norm_w", "w_in", "sb_q_norm", "sb_k_norm", "gdn_conv_w", "gdn_a_log", "gdn_dt_bias",
             "gdn_out_norm", "hgrn_lb_logits", "hgrn_out_norm", "w_branch", "w_out"]
    return (loss, grad_x, *[res[n][0] for n in order], *[res[n][1] for n in order],
            *[res[n][2] for n in order], *[res[n][3] for n in order])
```
